```python
import jax, jax.numpy as jnp
from jax import lax
import numpy as np

D_MODEL = 2048
BATCH = 8
SEQ = 2048
DEPTH = 1

HEAD_DIM = 64
FOX_HEADS = D_MODEL // (2 * HEAD_DIM)
SWA_Q_HEADS = D_MODEL // (2 * HEAD_DIM)
SWA_KV_HEADS = SWA_Q_HEADS // 4
D_MIX = (FOX_HEADS + SWA_Q_HEADS) * HEAD_DIM
WINDOW = 128
SWA_BLOCK = WINDOW
Q_BLOCK = 128
D_FF = 5632
ROPE_THETA = 10000.0
EPS = 1e-6

SPLIT_SIZES = (
    FOX_HEADS * HEAD_DIM,
    FOX_HEADS * HEAD_DIM,
    FOX_HEADS * HEAD_DIM,
    FOX_HEADS,
    SWA_Q_HEADS * HEAD_DIM,
    SWA_KV_HEADS * HEAD_DIM,
    SWA_KV_HEADS * HEAD_DIM,
)
D_IN_PROJ = sum(SPLIT_SIZES)
SPLIT_POINTS = tuple(int(v) for v in np.cumsum(SPLIT_SIZES)[:-1])

kernel_name = "hybrid_fox_swa_sink_macaron"


def rms_norm(x, g):
    xf = x.astype(jnp.float32)
    y = xf * lax.rsqrt(jnp.mean(xf * xf, axis=-1, keepdims=True) + EPS)
    return (y * g.astype(jnp.float32)).astype(x.dtype)


def swiglu(x, w_gate, w_up, w_down):
    return (jax.nn.silu(x @ w_gate) * (x @ w_up)) @ w_down


def rope(x, positions):
    d = x.shape[-1]
    inv_freq = ROPE_THETA ** (-jnp.arange(0, d, 2, dtype=jnp.float32) / d)
    ang = positions.astype(jnp.float32)[..., None] * inv_freq
    cos = jnp.cos(ang)[:, :, None, :]
    sin = jnp.sin(ang)[:, :, None, :]
    x1, x2 = jnp.split(x.astype(jnp.float32), 2, axis=-1)
    out = jnp.concatenate([x1 * cos - x2 * sin, x2 * cos + x1 * sin], axis=-1)
    return out.astype(x.dtype)


def forgetting_attention(q, k, v, log_f):
    B, S, H, d = q.shape
    nb = S // Q_BLOCK
    c = jnp.cumsum(log_f, axis=1).transpose(0, 2, 1)
    qh = q.transpose(0, 2, 1, 3)
    kh = k.transpose(0, 2, 1, 3)
    vh = v.transpose(0, 2, 1, 3)
    qb = qh.reshape(B, H, nb, Q_BLOCK, d).transpose(2, 0, 1, 3, 4)
    cb = c.reshape(B, H, nb, Q_BLOCK).transpose(2, 0, 1, 3)
    q_pos = jnp.arange(S).reshape(nb, Q_BLOCK)
    k_pos = jnp.arange(S)
    scale = d ** -0.5

    def block(args):
        q_i, c_i, p_i = args
        s = jnp.einsum('bhqd,bhkd->bhqk', q_i, kh).astype(jnp.float32) * scale
        s = s + c_i[..., None] - c[:, :, None, :]
        causal = p_i[:, None] >= k_pos[None, :]
        s = jnp.where(causal, s, -jnp.inf)
        p = jax.nn.softmax(s, axis=-1).astype(vh.dtype)
        return jnp.einsum('bhqk,bhkd->bhqd', p, vh)

    o = lax.map(block, (qb, cb, q_pos))
    return o.transpose(1, 0, 3, 2, 4).reshape(B, S, H * d)


def sliding_window_sink_attention(q, k, v, sinks):
    B, S, Hq, d = q.shape
    Hk = k.shape[2]
    G = Hq // Hk
    W = SWA_BLOCK
    nb = S // W
    qb = q.reshape(B, nb, W, Hk, G, d)

    def with_prev(t):
        t = t.reshape(B, nb, W, Hk, d)
        prev = jnp.pad(t, ((0, 0), (1, 0), (0, 0), (0, 0), (0, 0)))[:, :-1]
        return jnp.concatenate([prev, t], axis=2)

    kw = with_prev(k)
    vw = with_prev(v)
    s = jnp.einsum('bnqhgd,bnkhd->bnhgqk', qb, kw).astype(jnp.float32) * (d ** -0.5)
    blk = jnp.arange(nb)[:, None]
    q_abs = blk * W + jnp.arange(W)[None, :]
    k_abs = blk * W - W + jnp.arange(2 * W)[None, :]
    rel = q_abs[:, :, None] - k_abs[:, None, :]
    band = (rel >= 0) & (rel < WINDOW) & (k_abs[:, None, :] >= 0)
    s = jnp.where(band[None, :, None, None], s, -jnp.inf)
    sink = jnp.broadcast_to(sinks.astype(jnp.float32).reshape(1, 1, Hk, G, 1, 1), s.shape[:-1] + (1,))
    p = jax.nn.softmax(jnp.concatenate([s, sink], axis=-1), axis=-1)[..., :-1].astype(v.dtype)
    o = jnp.einsum('bnhgqk,bnkhd->bnqhgd', p, vw)
    return o.reshape(B, S, Hq * d)


def _fwd_setup_inputs(seed: int = 0) -> dict:
    key = jax.random.key(seed)
    ks = jax.random.split(key, 24)
    f32 = jnp.float32

    def w(k, shape, fan_in):
        return jax.random.normal(k, shape, f32) * (fan_in ** -0.5)

    def gain(k, shape):
        return 1.0 + 0.02 * jax.random.normal(k, shape, f32)

    x = jax.random.normal(ks[0], (BATCH, SEQ, D_MODEL), f32)
    positions = jnp.broadcast_to(jnp.arange(SEQ, dtype=jnp.int32), (BATCH, SEQ))
    return {
        "x": x,
        "positions": positions,
        "norm_ffn1_g": gain(ks[1], (DEPTH, D_MODEL)),
        "ffn1_w_gate": w(ks[2], (DEPTH, D_MODEL, D_FF), D_MODEL),
        "ffn1_w_up": w(ks[3], (DEPTH, D_MODEL, D_FF), D_MODEL),
        "ffn1_w_down": w(ks[4], (DEPTH, D_FF, D_MODEL), D_FF),
        "norm_mix_g": gain(ks[5], (DEPTH, D_MODEL)),
        "w_in": w(ks[6], (DEPTH, D_MODEL, D_IN_PROJ), D_MODEL),
        "b_forget": 0.1 * jax.random.normal(ks[7], (DEPTH, FOX_HEADS), f32),
        "fox_q_norm_g": gain(ks[8], (DEPTH, HEAD_DIM)),
        "fox_k_norm_g": gain(ks[9], (DEPTH, HEAD_DIM)),
        "swa_q_norm_g": gain(ks[10], (DEPTH, HEAD_DIM)),
        "swa_k_norm_g": gain(ks[11], (DEPTH, HEAD_DIM)),
        "swa_sinks": 0.5 * jax.random.normal(ks[12], (DEPTH, SWA_Q_HEADS), f32),
        "out_norm_fox_g": gain(ks[13], (DEPTH, FOX_HEADS * HEAD_DIM)),
        "out_norm_swa_g": gain(ks[14], (DEPTH, SWA_Q_HEADS * HEAD_DIM)),
        "w_out": w(ks[15], (DEPTH, D_MIX, D_MODEL), D_MIX),
        "norm_ffn2_g": gain(ks[16], (DEPTH, D_MODEL)),
        "ffn2_w_gate": w(ks[17], (DEPTH, D_MODEL, D_FF), D_MODEL),
        "ffn2_w_up": w(ks[18], (DEPTH, D_MODEL, D_FF), D_MODEL),
        "ffn2_w_down": w(ks[19], (DEPTH, D_FF, D_MODEL), D_FF),
    }


def _fwd_reference(x, positions, norm_ffn1_g, ffn1_w_gate, ffn1_w_up, ffn1_w_down, norm_mix_g, w_in,
              b_forget, fox_q_norm_g, fox_k_norm_g, swa_q_norm_g, swa_k_norm_g, swa_sinks,
              out_norm_fox_g, out_norm_swa_g, w_out, norm_ffn2_g, ffn2_w_gate, ffn2_w_up, ffn2_w_down):
    B, S, _ = x.shape
    h = x
    for l in range(DEPTH):
        h = h + 0.5 * swiglu(rms_norm(h, norm_ffn1_g[l]), ffn1_w_gate[l], ffn1_w_up[l], ffn1_w_down[l])

        u = rms_norm(h, norm_mix_g[l])
        proj = u @ w_in[l]
        q_f, k_f, v_f, f_logit, q_s, k_s, v_s = jnp.split(proj, SPLIT_POINTS, axis=-1)

        q_f = rms_norm(q_f.reshape(B, S, FOX_HEADS, HEAD_DIM), fox_q_norm_g[l])
        k_f = rms_norm(k_f.reshape(B, S, FOX_HEADS, HEAD_DIM), fox_k_norm_g[l])
        v_f = v_f.reshape(B, S, FOX_HEADS, HEAD_DIM)
        log_f = jax.nn.log_sigmoid((f_logit + b_forget[l]).astype(jnp.float32))
        o_fox = forgetting_attention(q_f, k_f, v_f, log_f)

        q_s = rope(rms_norm(q_s.reshape(B, S, SWA_Q_HEADS, HEAD_DIM), swa_q_norm_g[l]), positions)
        k_s = rope(rms_norm(k_s.reshape(B, S, SWA_KV_HEADS, HEAD_DIM), swa_k_norm_g[l]), positions)
        v_s = v_s.reshape(B, S, SWA_KV_HEADS, HEAD_DIM)
        o_swa = sliding_window_sink_attention(q_s, k_s, v_s, swa_sinks[l])

        o = jnp.concatenate([rms_norm(o_fox, out_norm_fox_g[l]), rms_norm(o_swa, out_norm_swa_g[l])], axis=-1)
        h = h + o @ w_out[l]

        h = h + 0.5 * swiglu(rms_norm(h, norm_ffn2_g[l]), ffn2_w_gate[l], ffn2_w_up[l], ffn2_w_down[l])
    return h


import jax as _jax
import jax.numpy as _jnp

TWIN_FORMAT = 'train_step'
FWD_PARAMS = ['x', 'positions', 'norm_ffn1_g', 'ffn1_w_gate', 'ffn1_w_up', 'ffn1_w_down', 'norm_mix_g', 'w_in', 'b_forget', 'fox_q_norm_g', 'fox_k_norm_g', 'swa_q_norm_g', 'swa_k_norm_g', 'swa_sinks', 'out_norm_fox_g', 'out_norm_swa_g', 'w_out', 'norm_ffn2_g', 'ffn2_w_gate', 'ffn2_w_up', 'ffn2_w_down']
TWIN_WEIGHTS = ['norm_ffn1_g', 'ffn1_w_gate', 'ffn1_w_up', 'ffn1_w_down', 'norm_mix_g', 'w_in', 'b_forget', 'fox_q_norm_g', 'fox_k_norm_g', 'swa_q_norm_g', 'swa_k_norm_g', 'swa_sinks', 'out_norm_fox_g', 'out_norm_swa_g', 'w_out', 'norm_ffn2_g', 'ffn2_w_gate', 'ffn2_w_up', 'ffn2_w_down']
TWIN_DIFF_INPUT = 'x'
TWIN_INPUTS = ['x', 'positions', 'norm_ffn1_g', 'ffn1_w_gate', 'ffn1_w_up', 'ffn1_w_down', 'norm_mix_g', 'w_in', 'b_forget', 'fox_q_norm_g', 'fox_k_norm_g', 'swa_q_norm_g', 'swa_k_norm_g', 'swa_sinks', 'out_norm_fox_g', 'out_norm_swa_g', 'w_out', 'norm_ffn2_g', 'ffn2_w_gate', 'ffn2_w_up', 'ffn2_w_down', 'loss_target', 'm_norm_ffn1_g', 'm_ffn1_w_gate', 'm_ffn1_w_up', 'm_ffn1_w_down', 'm_norm_mix_g', 'm_w_in', 'm_b_forget', 'm_fox_q_norm_g', 'm_fox_k_norm_g', 'm_swa_q_norm_g', 'm_swa_k_norm_g', 'm_swa_sinks', 'm_out_norm_fox_g', 'm_out_norm_swa_g', 'm_w_out', 'm_norm_ffn2_g', 'm_ffn2_w_gate', 'm_ffn2_w_up', 'm_ffn2_w_down', 'v_norm_ffn1_g', 'v_ffn1_w_gate', 'v_ffn1_w_up', 'v_ffn1_w_down', 'v_norm_mix_g', 'v_w_in', 'v_b_forget', 'v_fox_q_norm_g', 'v_fox_k_norm_g', 'v_swa_q_norm_g', 'v_swa_k_norm_g', 'v_swa_sinks', 'v_out_norm_fox_g', 'v_out_norm_swa_g', 'v_w_out', 'v_norm_ffn2_g', 'v_ffn2_w_gate', 'v_ffn2_w_up', 'v_ffn2_w_down']
TWIN_OUTPUTS = ['loss', 'grad_x', 'grad_norm_ffn1_g', 'grad_ffn1_w_gate', 'grad_ffn1_w_up', 'grad_ffn1_w_down', 'grad_norm_mix_g', 'grad_w_in', 'grad_b_forget', 'grad_fox_q_norm_g', 'grad_fox_k_norm_g', 'grad_swa_q_norm_g', 'grad_swa_k_norm_g', 'grad_swa_sinks', 'grad_out_norm_fox_g', 'grad_out_norm_swa_g', 'grad_w_out', 'grad_norm_ffn2_g', 'grad_ffn2_w_gate', 'grad_ffn2_w_up', 'grad_ffn2_w_down', 'delta_norm_ffn1_g', 'delta_ffn1_w_gate', 'delta_ffn1_w_up', 'delta_ffn1_w_down', 'delta_norm_mix_g', 'delta_w_in', 'delta_b_forget', 'delta_fox_q_norm_g', 'delta_fox_k_norm_g', 'delta_swa_q_norm_g', 'delta_swa_k_norm_g', 'delta_swa_sinks', 'delta_out_norm_fox_g', 'delta_out_norm_swa_g', 'delta_w_out', 'delta_norm_ffn2_g', 'delta_ffn2_w_gate', 'delta_ffn2_w_up', 'delta_ffn2_w_down', 'new_m_norm_ffn1_g', 'new_m_ffn1_w_gate', 'new_m_ffn1_w_up', 'new_m_ffn1_w_down', 'new_m_norm_mix_g', 'new_m_w_in', 'new_m_b_forget', 'new_m_fox_q_norm_g', 'new_m_fox_k_norm_g', 'new_m_swa_q_norm_g', 'new_m_swa_k_norm_g', 'new_m_swa_sinks', 'new_m_out_norm_fox_g', 'new_m_out_norm_swa_g', 'new_m_w_out', 'new_m_norm_ffn2_g', 'new_m_ffn2_w_gate', 'new_m_ffn2_w_up', 'new_m_ffn2_w_down', 'new_v_norm_ffn1_g', 'new_v_ffn1_w_gate', 'new_v_ffn1_w_up', 'new_v_ffn1_w_down', 'new_v_norm_mix_g', 'new_v_w_in', 'new_v_b_forget', 'new_v_fox_q_norm_g', 'new_v_fox_k_norm_g', 'new_v_swa_q_norm_g', 'new_v_swa_k_norm_g', 'new_v_swa_sinks', 'new_v_out_norm_fox_g', 'new_v_out_norm_swa_g', 'new_v_w_out', 'new_v_norm_ffn2_g', 'new_v_ffn2_w_gate', 'new_v_ffn2_w_up', 'new_v_ffn2_w_down']
TWIN_LEAF_KINDS = {'loss': 'loss', 'grad_x': 'grad_x', 'grad_norm_ffn1_g': 'grad_w', 'grad_ffn1_w_gate': 'grad_w', 'grad_ffn1_w_up': 'grad_w', 'grad_ffn1_w_down': 'grad_w', 'grad_norm_mix_g': 'grad_w', 'grad_w_in': 'grad_w', 'grad_b_forget': 'grad_w', 'grad_fox_q_norm_g': 'grad_w', 'grad_fox_k_norm_g': 'grad_w', 'grad_swa_q_norm_g': 'grad_w', 'grad_swa_k_norm_g': 'grad_w', 'grad_swa_sinks': 'grad_w', 'grad_out_norm_fox_g': 'grad_w', 'grad_out_norm_swa_g': 'grad_w', 'grad_w_out': 'grad_w', 'grad_norm_ffn2_g': 'grad_w', 'grad_ffn2_w_gate': 'grad_w', 'grad_ffn2_w_up': 'grad_w', 'grad_ffn2_w_down': 'grad_w', 'delta_norm_ffn1_g': 'delta_w', 'delta_ffn1_w_gate': 'delta_w', 'delta_ffn1_w_up': 'delta_w', 'delta_ffn1_w_down': 'delta_w', 'delta_norm_mix_g': 'delta_w', 'delta_w_in': 'delta_w', 'delta_b_forget': 'delta_w', 'delta_fox_q_norm_g': 'delta_w', 'delta_fox_k_norm_g': 'delta_w', 'delta_swa_q_norm_g': 'delta_w', 'delta_swa_k_norm_g': 'delta_w', 'delta_swa_sinks': 'delta_w', 'delta_out_norm_fox_g': 'delta_w', 'delta_out_norm_swa_g': 'delta_w', 'delta_w_out': 'delta_w', 'delta_norm_ffn2_g': 'delta_w', 'delta_ffn2_w_gate': 'delta_w', 'delta_ffn2_w_up': 'delta_w', 'delta_ffn2_w_down': 'delta_w', 'new_m_norm_ffn1_g': 'new_m', 'new_m_ffn1_w_gate': 'new_m', 'new_m_ffn1_w_up': 'new_m', 'new_m_ffn1_w_down': 'new_m', 'new_m_norm_mix_g': 'new_m', 'new_m_w_in': 'new_m', 'new_m_b_forget': 'new_m', 'new_m_fox_q_norm_g': 'new_m', 'new_m_fox_k_norm_g': 'new_m', 'new_m_swa_q_norm_g': 'new_m', 'new_m_swa_k_norm_g': 'new_m', 'new_m_swa_sinks': 'new_m', 'new_m_out_norm_fox_g': 'new_m', 'new_m_out_norm_swa_g': 'new_m', 'new_m_w_out': 'new_m', 'new_m_norm_ffn2_g': 'new_m', 'new_m_ffn2_w_gate': 'new_m', 'new_m_ffn2_w_up': 'new_m', 'new_m_ffn2_w_down': 'new_m', 'new_v_norm_ffn1_g': 'new_v', 'new_v_ffn1_w_gate': 'new_v', 'new_v_ffn1_w_up': 'new_v', 'new_v_ffn1_w_down': 'new_v', 'new_v_norm_mix_g': 'new_v', 'new_v_w_in': 'new_v', 'new_v_b_forget': 'new_v', 'new_v_fox_q_norm_g': 'new_v', 'new_v_fox_k_norm_g': 'new_v', 'new_v_swa_q_norm_g': 'new_v', 'new_v_swa_k_norm_g': 'new_v', 'new_v_swa_sinks': 'new_v', 'new_v_out_norm_fox_g': 'new_v', 'new_v_out_norm_swa_g': 'new_v', 'new_v_w_out': 'new_v', 'new_v_norm_ffn2_g': 'new_v', 'new_v_ffn2_w_gate': 'new_v', 'new_v_ffn2_w_up': 'new_v', 'new_v_ffn2_w_down': 'new_v'}


def _forward(args):
    return _fwd_reference(*[args[k] for k in FWD_PARAMS])


def _output_shape():
    out = _jax.eval_shape(lambda: _forward(_fwd_setup_inputs(0)))
    return out.shape, out.dtype

N_MICROBATCH = 1
ADAM_LR = 0.001
ADAM_B1 = 0.9
ADAM_B2 = 0.999
ADAM_EPS = 1e-08
ADAM_WD = 0.01
ADAM_STEP = 10
PER_EXAMPLE_BATCH_AXIS = {'x': 0, 'positions': 0, 'loss_target': 0}
SHARED_INPUTS = []
_WEIGHT_DTYPES = {'norm_ffn1_g': _jnp.float32, 'ffn1_w_gate': _jnp.float32, 'ffn1_w_up': _jnp.float32, 'ffn1_w_down': _jnp.float32, 'norm_mix_g': _jnp.float32, 'w_in': _jnp.float32, 'b_forget': _jnp.float32, 'fox_q_norm_g': _jnp.float32, 'fox_k_norm_g': _jnp.float32, 'swa_q_norm_g': _jnp.float32, 'swa_k_norm_g': _jnp.float32, 'swa_sinks': _jnp.float32, 'out_norm_fox_g': _jnp.float32, 'out_norm_swa_g': _jnp.float32, 'w_out': _jnp.float32, 'norm_ffn2_g': _jnp.float32, 'ffn2_w_gate': _jnp.float32, 'ffn2_w_up': _jnp.float32, 'ffn2_w_down': _jnp.float32}
MOMENT_SCALE = {'norm_ffn1_g': 1.528108e+00, 'ffn1_w_gate': 4.472882e-02, 'ffn1_w_up': 4.423877e-02, 'ffn1_w_down': 7.280558e-02, 'norm_mix_g': 2.493457e-01, 'w_in': 1.518286e-01, 'b_forget': 8.127808e-01, 'fox_q_norm_g': 2.591389e-01, 'fox_k_norm_g': 2.569179e-01, 'swa_q_norm_g': 4.914613e-01, 'swa_k_norm_g': 5.174118e-01, 'swa_sinks': 5.653165e-02, 'out_norm_fox_g': 8.017167e+00, 'out_norm_swa_g': 7.968308e+00, 'w_out': 2.391063e-01, 'norm_ffn2_g': 1.548393e+00, 'ffn2_w_gate': 2.374844e-02, 'ffn2_w_up': 2.623667e-02, 'ffn2_w_down': 4.223901e-02}


def _to_microbatches(a, axis):
    t = _jnp.moveaxis(a, axis, 0)
    t = t.reshape((N_MICROBATCH, t.shape[0] // N_MICROBATCH) + t.shape[1:])
    return _jnp.moveaxis(t, 1, axis + 1)


def setup_inputs(seed: int = 0) -> dict:
    inp = _fwd_setup_inputs(seed)
    key = _jax.random.fold_in(_jax.random.key(seed), 7919)
    shape, _ = _output_shape()
    out = dict(inp)
    out["loss_target"] = _jax.random.normal(_jax.random.fold_in(key, 0), shape, _jnp.float32)
    for i, name in enumerate(TWIN_WEIGHTS):
        w = inp[name].astype(_jnp.float32)
        if MOMENT_SCALE is None:
            s = _jnp.sqrt(_jnp.mean(_jnp.square(w)) + 1e-30)
        else:
            s = MOMENT_SCALE[name]
        km, kv = _jax.random.split(_jax.random.fold_in(key, i + 1))
        out[name] = w
        out["m_" + name] = s * _jax.random.normal(km, w.shape, _jnp.float32)
        out["v_" + name] = (s * s) * _jax.random.uniform(kv, w.shape, _jnp.float32, 0.5, 1.5)
    if N_MICROBATCH > 1:
        for name, axis in PER_EXAMPLE_BATCH_AXIS.items():
            out[name] = _to_microbatches(out[name], axis)
    return {'x': out['x'], 'positions': out['positions'], 'norm_ffn1_g': out['norm_ffn1_g'], 'ffn1_w_gate': out['ffn1_w_gate'], 'ffn1_w_up': out['ffn1_w_up'], 'ffn1_w_down': out['ffn1_w_down'], 'norm_mix_g': out['norm_mix_g'], 'w_in': out['w_in'], 'b_forget': out['b_forget'], 'fox_q_norm_g': out['fox_q_norm_g'], 'fox_k_norm_g': out['fox_k_norm_g'], 'swa_q_norm_g': out['swa_q_norm_g'], 'swa_k_norm_g': out['swa_k_norm_g'], 'swa_sinks': out['swa_sinks'], 'out_norm_fox_g': out['out_norm_fox_g'], 'out_norm_swa_g': out['out_norm_swa_g'], 'w_out': out['w_out'], 'norm_ffn2_g': out['norm_ffn2_g'], 'ffn2_w_gate': out['ffn2_w_gate'], 'ffn2_w_up': out['ffn2_w_up'], 'ffn2_w_down': out['ffn2_w_down'], 'loss_target': out['loss_target'], 'm_norm_ffn1_g': out['m_norm_ffn1_g'], 'm_ffn1_w_gate': out['m_ffn1_w_gate'], 'm_ffn1_w_up': out['m_ffn1_w_up'], 'm_ffn1_w_down': out['m_ffn1_w_down'], 'm_norm_mix_g': out['m_norm_mix_g'], 'm_w_in': out['m_w_in'], 'm_b_forget': out['m_b_forget'], 'm_fox_q_norm_g': out['m_fox_q_norm_g'], 'm_fox_k_norm_g': out['m_fox_k_norm_g'], 'm_swa_q_norm_g': out['m_swa_q_norm_g'], 'm_swa_k_norm_g': out['m_swa_k_norm_g'], 'm_swa_sinks': out['m_swa_sinks'], 'm_out_norm_fox_g': out['m_out_norm_fox_g'], 'm_out_norm_swa_g': out['m_out_norm_swa_g'], 'm_w_out': out['m_w_out'], 'm_norm_ffn2_g': out['m_norm_ffn2_g'], 'm_ffn2_w_gate': out['m_ffn2_w_gate'], 'm_ffn2_w_up': out['m_ffn2_w_up'], 'm_ffn2_w_down': out['m_ffn2_w_down'], 'v_norm_ffn1_g': out['v_norm_ffn1_g'], 'v_ffn1_w_gate': out['v_ffn1_w_gate'], 'v_ffn1_w_up': out['v_ffn1_w_up'], 'v_ffn1_w_down': out['v_ffn1_w_down'], 'v_norm_mix_g': out['v_norm_mix_g'], 'v_w_in': out['v_w_in'], 'v_b_forget': out['v_b_forget'], 'v_fox_q_norm_g': out['v_fox_q_norm_g'], 'v_fox_k_norm_g': out['v_fox_k_norm_g'], 'v_swa_q_norm_g': out['v_swa_q_norm_g'], 'v_swa_k_norm_g': out['v_swa_k_norm_g'], 'v_swa_sinks': out['v_swa_sinks'], 'v_out_norm_fox_g': out['v_out_norm_fox_g'], 'v_out_norm_swa_g': out['v_out_norm_swa_g'], 'v_w_out': out['v_w_out'], 'v_norm_ffn2_g': out['v_norm_ffn2_g'], 'v_ffn2_w_gate': out['v_ffn2_w_gate'], 'v_ffn2_w_up': out['v_ffn2_w_up'], 'v_ffn2_w_down': out['v_ffn2_w_down']}


def _loss(weights, diff, rest, loss_target):
    with _jax.named_scope("forward"):
        args = {**rest, TWIN_DIFF_INPUT: diff, **{k: w.astype(_WEIGHT_DTYPES[k]) for k, w in weights.items()}}
        y = _forward(args)
    with _jax.named_scope("loss_head"):
        err = _jnp.square(y.astype(_jnp.float32) - loss_target)
        return 0.5 * _jnp.sum(_jnp.mean(err, axis=-1)) if err.ndim else 0.5 * err


def _adamw(w, g, m, v):
    m = ADAM_B1 * m + (1.0 - ADAM_B1) * g
    v = ADAM_B2 * v + (1.0 - ADAM_B2) * _jnp.square(g)
    m_hat = m / (1.0 - ADAM_B1 ** ADAM_STEP)
    v_hat = v / (1.0 - ADAM_B2 ** ADAM_STEP)
    delta = -ADAM_LR * (m_hat / (_jnp.sqrt(v_hat) + ADAM_EPS) + ADAM_WD * w)
    return delta, m, v


def reference(x, positions, norm_ffn1_g, ffn1_w_gate, ffn1_w_up, ffn1_w_down, norm_mix_g, w_in, b_forget, fox_q_norm_g, fox_k_norm_g, swa_q_norm_g, swa_k_norm_g, swa_sinks, out_norm_fox_g, out_norm_swa_g, w_out, norm_ffn2_g, ffn2_w_gate, ffn2_w_up, ffn2_w_down, loss_target, m_norm_ffn1_g, m_ffn1_w_gate, m_ffn1_w_up, m_ffn1_w_down, m_norm_mix_g, m_w_in, m_b_forget, m_fox_q_norm_g, m_fox_k_norm_g, m_swa_q_norm_g, m_swa_k_norm_g, m_swa_sinks, m_out_norm_fox_g, m_out_norm_swa_g, m_w_out, m_norm_ffn2_g, m_ffn2_w_gate, m_ffn2_w_up, m_ffn2_w_down, v_norm_ffn1_g, v_ffn1_w_gate, v_ffn1_w_up, v_ffn1_w_down, v_norm_mix_g, v_w_in, v_b_forget, v_fox_q_norm_g, v_fox_k_norm_g, v_swa_q_norm_g, v_swa_k_norm_g, v_swa_sinks, v_out_norm_fox_g, v_out_norm_swa_g, v_w_out, v_norm_ffn2_g, v_ffn2_w_gate, v_ffn2_w_up, v_ffn2_w_down):
    given = dict(x=x, positions=positions, norm_ffn1_g=norm_ffn1_g, ffn1_w_gate=ffn1_w_gate, ffn1_w_up=ffn1_w_up, ffn1_w_down=ffn1_w_down, norm_mix_g=norm_mix_g, w_in=w_in, b_forget=b_forget, fox_q_norm_g=fox_q_norm_g, fox_k_norm_g=fox_k_norm_g, swa_q_norm_g=swa_q_norm_g, swa_k_norm_g=swa_k_norm_g, swa_sinks=swa_sinks, out_norm_fox_g=out_norm_fox_g, out_norm_swa_g=out_norm_swa_g, w_out=w_out, norm_ffn2_g=norm_ffn2_g, ffn2_w_gate=ffn2_w_gate, ffn2_w_up=ffn2_w_up, ffn2_w_down=ffn2_w_down, loss_target=loss_target, m_norm_ffn1_g=m_norm_ffn1_g, m_ffn1_w_gate=m_ffn1_w_gate, m_ffn1_w_up=m_ffn1_w_up, m_ffn1_w_down=m_ffn1_w_down, m_norm_mix_g=m_norm_mix_g, m_w_in=m_w_in, m_b_forget=m_b_forget, m_fox_q_norm_g=m_fox_q_norm_g, m_fox_k_norm_g=m_fox_k_norm_g, m_swa_q_norm_g=m_swa_q_norm_g, m_swa_k_norm_g=m_swa_k_norm_g, m_swa_sinks=m_swa_sinks, m_out_norm_fox_g=m_out_norm_fox_g, m_out_norm_swa_g=m_out_norm_swa_g, m_w_out=m_w_out, m_norm_ffn2_g=m_norm_ffn2_g, m_ffn2_w_gate=m_ffn2_w_gate, m_ffn2_w_up=m_ffn2_w_up, m_ffn2_w_down=m_ffn2_w_down, v_norm_ffn1_g=v_norm_ffn1_g, v_ffn1_w_gate=v_ffn1_w_gate, v_ffn1_w_up=v_ffn1_w_up, v_ffn1_w_down=v_ffn1_w_down, v_norm_mix_g=v_norm_mix_g, v_w_in=v_w_in, v_b_forget=v_b_forget, v_fox_q_norm_g=v_fox_q_norm_g, v_fox_k_norm_g=v_fox_k_norm_g, v_swa_q_norm_g=v_swa_q_norm_g, v_swa_k_norm_g=v_swa_k_norm_g, v_swa_sinks=v_swa_sinks, v_out_norm_fox_g=v_out_norm_fox_g, v_out_norm_swa_g=v_out_norm_swa_g, v_w_out=v_w_out, v_norm_ffn2_g=v_norm_ffn2_g, v_ffn2_w_gate=v_ffn2_w_gate, v_ffn2_w_up=v_ffn2_w_up, v_ffn2_w_down=v_ffn2_w_down)
    weights = {n: given[n] for n in TWIN_WEIGHTS}
    shared = {n: given[n] for n in SHARED_INPUTS}
    per_example = {n: given[n] for n in ['x', 'positions']}
    grad_fn = _jax.value_and_grad(_loss, argnums=(0, 1))

    def one_microbatch(ex, loss_target):
        ex = dict(ex)
        diff = ex.pop(TWIN_DIFF_INPUT)
        return grad_fn(weights, diff, {**shared, **ex}, loss_target)

    if N_MICROBATCH == 1:
        loss, (grad_w, grad_x) = one_microbatch(per_example, given["loss_target"])
    else:
        def body(carry, xs):
            loss_sum, grad_sum = carry
            l_k, (gw_k, gx_k) = one_microbatch(xs[0], xs[1])
            with _jax.named_scope("update"):
                return (loss_sum + l_k, _jax.tree.map(_jnp.add, grad_sum, gw_k)), gx_k

        init = (_jnp.zeros((), _jnp.float32), _jax.tree.map(_jnp.zeros_like, weights))
        (loss, grad_w), grad_x = _jax.lax.scan(body, init, (per_example, given["loss_target"]))
    with _jax.named_scope("update"):
        delta_w, new_m, new_v = {}, {}, {}
        for n in TWIN_WEIGHTS:
            delta_w[n], new_m[n], new_v[n] = _adamw(weights[n], grad_w[n], given["m_" + n], given["v_" + n])
    return (loss, grad_x, *[grad_w[n] for n in TWIN_WEIGHTS], *[delta_w[n] for n in TWIN_WEIGHTS],
            *[new_m[n] for n in TWIN_WEIGHTS], *[new_v[n] for n in TWIN_WEIGHTS])
```

```python
import functools
import math

import jax
import jax.numpy as jnp
from jax import lax
from jax.experimental import pallas as pl
from jax.experimental.pallas import tpu as pltpu

F32 = jnp.float32
BF16 = jnp.bfloat16

HEAD_DIM = 64
WINDOW = 128
ROPE_THETA = 10000.0
EPS = 1e-6
N_CHIPS = 4
N_DEV = 8

ADAM_LR = 0.001
ADAM_B1 = 0.9
ADAM_B2 = 0.999
ADAM_EPS = 1e-08
ADAM_WD = 0.01
ADAM_STEP = 10

V7X_VMEM_BYTES = 64 * 1024 * 1024
VMEM_LIMIT = V7X_VMEM_BYTES - 8 * 1024 * 1024
MASK_VALUE = -1e30

_MESH = pl.DeviceIdType.MESH


def _cparams(sem):
    return pltpu.CompilerParams(dimension_semantics=sem, vmem_limit_bytes=VMEM_LIMIT)


def _pick(n, prefs):
    for p in prefs:
        if n % p == 0:
            return p
    return n


_DIMS = {"nn": (((1,), (0,)), ((), ())), "nt": (((1,), (1,)), ((), ())), "tn": (((0,), (0,)), ((), ()))}


def _mm_call(name, mode, a, b, a_spec, b_spec, out_shape, out_spec, grid, acc_shape, scale=1.0, resid=None, resid_spec=None):
    nk = grid[2]
    dims = _DIMS[mode]
    has_resid = resid is not None

    def body(*refs):
        if has_resid:
            a_ref, b_ref, r_ref, o_ref, acc_ref = refs
        else:
            a_ref, b_ref, o_ref, acc_ref = refs
        k = pl.program_id(2)

        @pl.when(k == 0)
        def _():
            acc_ref[...] = jnp.zeros_like(acc_ref)

        acc_ref[...] += lax.dot_general(a_ref[...].astype(BF16), b_ref[...].astype(BF16), dims, preferred_element_type=F32)

        @pl.when(k == nk - 1)
        def _():
            r = acc_ref[...]
            if scale != 1.0:
                r = r * scale
            if has_resid:
                r = r_ref[...].astype(F32) + r
            o_ref[...] = r.astype(o_ref.dtype)

    in_specs = [a_spec, b_spec] + ([resid_spec] if has_resid else [])
    args = (a, b) + ((resid,) if has_resid else ())
    return pl.pallas_call(
        body, name=name, grid=grid, in_specs=in_specs, out_specs=out_spec, out_shape=out_shape,
        scratch_shapes=[pltpu.VMEM(acc_shape, F32)],
        compiler_params=_cparams(("parallel", "parallel", "arbitrary")),
    )(*args)


def mm_nn(name, a, b, *, out_dtype=F32, scale=1.0, resid=None):
    m, kd = a.shape
    n = b.shape[1]
    tm, tn, tk = _pick(m, (512, 256, 128)), _pick(n, (1024, 512, 256, 128)), _pick(kd, (1024, 512, 256, 128))
    return _mm_call(
        name, "nn", a, b, pl.BlockSpec((tm, tk), lambda i, j, k: (i, k)), pl.BlockSpec((tk, tn), lambda i, j, k: (k, j)),
        jax.ShapeDtypeStruct((m, n), out_dtype), pl.BlockSpec((tm, tn), lambda i, j, k: (i, j)),
        (m // tm, n // tn, kd // tk), (tm, tn), scale, resid, pl.BlockSpec((tm, tn), lambda i, j, k: (i, j)))


def mm_nt(name, a, b, *, out_dtype=F32, scale=1.0, resid=None):
    m, kd = a.shape
    n = b.shape[0]
    tm, tn, tk = _pick(m, (512, 256, 128)), _pick(n, (1024, 512, 256, 128)), _pick(kd, (1024, 512, 256, 128))
    return _mm_call(
        name, "nt", a, b, pl.BlockSpec((tm, tk), lambda i, j, k: (i, k)), pl.BlockSpec((tn, tk), lambda i, j, k: (j, k)),
        jax.ShapeDtypeStruct((m, n), out_dtype), pl.BlockSpec((tm, tn), lambda i, j, k: (i, j)),
        (m // tm, n // tn, kd // tk), (tm, tn), scale, resid, pl.BlockSpec((tm, tn), lambda i, j, k: (i, j)))


def mm_tn(name, a, b, *, out_dtype=F32, scale=1.0):
    kd, m = a.shape
    n = b.shape[1]
    tm, tn, tk = _pick(m, (512, 256, 128)), _pick(n, (1024, 512, 256, 128)), _pick(kd, (1024, 512, 256, 128))
    return _mm_call(
        name, "tn", a, b, pl.BlockSpec((tk, tm), lambda i, j, k: (k, i)), pl.BlockSpec((tk, tn), lambda i, j, k: (k, j)),
        jax.ShapeDtypeStruct((m, n), out_dtype), pl.BlockSpec((tm, tn), lambda i, j, k: (i, j)),
        (m // tm, n // tn, kd // tk), (tm, tn), scale)


def mm_nt_sharded(name, a, w, *, resid=None):
    m = a.shape[0]
    ns, n, c = w.shape
    tm, tn = _pick(m, (512, 256, 128)), _pick(n, (512, 256, 128))
    return _mm_call(
        name, "nt", a, w, pl.BlockSpec((tm, c), lambda i, j, k: (i, k)), pl.BlockSpec((None, tn, c), lambda i, j, k: (k, j, 0)),
        jax.ShapeDtypeStruct((m, n), F32), pl.BlockSpec((tm, tn), lambda i, j, k: (i, j)),
        (m // tm, n // tn, ns), (tm, tn), 1.0, resid, pl.BlockSpec((tm, tn), lambda i, j, k: (i, j)))


def mm_tn_sharded(name, a, b, ns):
    kd, m = a.shape
    c = b.shape[1] // ns
    tm, tk = _pick(m, (512, 256, 128)), _pick(kd, (1024, 512, 256, 128))
    return _mm_call(
        name, "tn", a, b, pl.BlockSpec((tk, tm), lambda i, j, k: (k, i)), pl.BlockSpec((tk, c), lambda i, j, k: (k, j)),
        jax.ShapeDtypeStruct((ns, m, c), BF16), pl.BlockSpec((None, tm, c), lambda i, j, k: (j, i, 0)),
        (m // tm, ns, kd // tk), (tm, c))


def rms_fwd(name, x, g, out_dtype):
    r, c = x.shape
    tm = _pick(r, (512, 256, 128, 64, 8))

    def body(x_ref, g_ref, y_ref, r_ref):
        xf = x_ref[...].astype(F32)
        rstd = lax.rsqrt(jnp.mean(xf * xf, axis=-1, keepdims=True) + EPS)
        y_ref[...] = ((xf * rstd) * g_ref[...]).astype(y_ref.dtype)
        r_ref[...] = rstd

    return pl.pallas_call(
        body, name=name, grid=(r // tm,),
        in_specs=[pl.BlockSpec((tm, c), lambda i: (i, 0)), pl.BlockSpec((1, c), lambda i: (0, 0))],
        out_specs=[pl.BlockSpec((tm, c), lambda i: (i, 0)), pl.BlockSpec((tm, 1), lambda i: (i, 0))],
        out_shape=[jax.ShapeDtypeStruct((r, c), out_dtype), jax.ShapeDtypeStruct((r, 1), F32)],
        compiler_params=_cparams(("parallel",)),
    )(x, g.reshape(1, c))


def rms_bwd(name, x, g, rstd, dy, dres=None):
    r, c = x.shape
    tm = _pick(r, (512, 256, 128, 64, 8))
    has_res = dres is not None

    def body(*refs):
        if has_res:
            x_ref, g_ref, r_ref, dy_ref, dres_ref, dx_ref, dg_ref = refs
        else:
            x_ref, g_ref, r_ref, dy_ref, dx_ref, dg_ref = refs
        xhat = x_ref[...].astype(F32) * r_ref[...]
        dyf = dy_ref[...].astype(F32)
        gdy = dyf * g_ref[...]
        dx = r_ref[...] * (gdy - xhat * jnp.mean(gdy * xhat, axis=-1, keepdims=True))
        if has_res:
            dx = dx + dres_ref[...]
        dx_ref[...] = dx

        @pl.when(pl.program_id(0) == 0)
        def _():
            dg_ref[...] = jnp.zeros_like(dg_ref)

        dg_ref[...] += jnp.sum(dyf * xhat, axis=0, keepdims=True)

    row = pl.BlockSpec((tm, c), lambda i: (i, 0))
    in_specs = [row, pl.BlockSpec((1, c), lambda i: (0, 0)), pl.BlockSpec((tm, 1), lambda i: (i, 0)), row] + ([row] if has_res else [])
    args = (x, g.reshape(1, c), rstd, dy) + ((dres,) if has_res else ())
    dx, dg = pl.pallas_call(
        body, name=name, grid=(r // tm,), in_specs=in_specs,
        out_specs=[row, pl.BlockSpec((1, c), lambda i: (0, 0))],
        out_shape=[jax.ShapeDtypeStruct((r, c), F32), jax.ShapeDtypeStruct((1, c), F32)],
        compiler_params=_cparams(("arbitrary",)),
    )(*args)
    return dx, dg.reshape(c)


@functools.partial(jax.custom_vjp, nondiff_argnums=(0,))
def rms_op(name, x, g):
    return rms_fwd(name + "_fwd", x, g, F32)[0]


def _rms_op_fwd(name, x, g):
    y, rstd = rms_fwd(name + "_fwd", x, g, F32)
    return y, (x, g, rstd)


def _rms_op_bwd(name, res, dy):
    x, g, rstd = res
    return rms_bwd(name + "_bwd", x, g, rstd, dy)


rms_op.defvjp(_rms_op_fwd, _rms_op_bwd)


def _sigmoid(x):
    return 1.0 / (1.0 + jnp.exp(-x))


def ffn_gu(name, xn, wg, wu):
    s, d = xn.shape
    ns, _, c = wg.shape
    tm, tk = _pick(s, (512, 256, 128)), _pick(d, (1024, 512, 256, 128))
    nk = d // tk

    def body(x_ref, wg_ref, wu_ref, g_ref, u_ref, h_ref, accg, accu):
        k = pl.program_id(2)

        @pl.when(k == 0)
        def _():
            accg[...] = jnp.zeros_like(accg)
            accu[...] = jnp.zeros_like(accu)

        xb = x_ref[...]
        accg[...] += jnp.dot(xb, wg_ref[...], preferred_element_type=F32)
        accu[...] += jnp.dot(xb, wu_ref[...], preferred_element_type=F32)

        @pl.when(k == nk - 1)
        def _():
            gv, uv = accg[...], accu[...]
            g_ref[...] = gv
            u_ref[...] = uv
            h_ref[...] = ((gv * _sigmoid(gv)) * uv).astype(BF16)

    w_spec = pl.BlockSpec((None, tk, c), lambda i, j, k: (j, k, 0))
    o_spec = pl.BlockSpec((tm, c), lambda i, j, k: (i, j))
    return pl.pallas_call(
        body, name=name, grid=(s // tm, ns, nk),
        in_specs=[pl.BlockSpec((tm, tk), lambda i, j, k: (i, k)), w_spec, w_spec],
        out_specs=[o_spec, o_spec, o_spec],
        out_shape=[jax.ShapeDtypeStruct((s, ns * c), F32), jax.ShapeDtypeStruct((s, ns * c), F32), jax.ShapeDtypeStruct((s, ns * c), BF16)],
        scratch_shapes=[pltpu.VMEM((tm, c), F32), pltpu.VMEM((tm, c), F32)],
        compiler_params=_cparams(("parallel", "parallel", "arbitrary")),
    )(xn, wg, wu)


def ffn_dh(name, dy, wd, g, u, ns, scale):
    s, d = dy.shape
    f = wd.shape[0]
    c = f // ns
    tm, tk = _pick(s, (512, 256, 128)), _pick(d, (1024, 512, 256, 128))
    nk = d // tk

    def body(dy_ref, wd_ref, g_ref, u_ref, dg_ref, du_ref, acc):
        k = pl.program_id(2)

        @pl.when(k == 0)
        def _():
            acc[...] = jnp.zeros_like(acc)

        acc[...] += lax.dot_general(dy_ref[...].astype(BF16), wd_ref[...], _DIMS["nt"], preferred_element_type=F32)

        @pl.when(k == nk - 1)
        def _():
            dh = acc[...] * scale
            gv, uv = g_ref[...], u_ref[...]
            sig = _sigmoid(gv)
            dg_ref[...] = (dh * uv * (sig * (1.0 + gv * (1.0 - sig)))).astype(BF16)
            du_ref[...] = (dh * (gv * sig)).astype(BF16)

    o_spec = pl.BlockSpec((tm, c), lambda i, j, k: (i, j))
    return pl.pallas_call(
        body, name=name, grid=(s // tm, ns, nk),
        in_specs=[pl.BlockSpec((tm, tk), lambda i, j, k: (i, k)), pl.BlockSpec((c, tk), lambda i, j, k: (j, k)), o_spec, o_spec],
        out_specs=[o_spec, o_spec],
        out_shape=[jax.ShapeDtypeStruct((s, f), BF16), jax.ShapeDtypeStruct((s, f), BF16)],
        scratch_shapes=[pltpu.VMEM((tm, c), F32)],
        compiler_params=_cparams(("parallel", "parallel", "arbitrary")),
    )(dy, wd, g, u)


@functools.partial(jax.custom_vjp, nondiff_argnums=(0,))
def ffn_block(name, x, gain, wg, wu, wd):
    return _ffn_block_fwd(name, x, gain, wg, wu, wd)[0]


def _ffn_block_fwd(name, x, gain, wg, wu, wd):
    xn, rstd = rms_fwd(name + "_norm", x, gain, BF16)
    g, u, h = ffn_gu(name + "_gu", xn, wg, wu)
    y = mm_nn(name + "_down", h, wd, scale=0.5, resid=x)
    return y, (x, gain, rstd, xn, g, u, h, wg, wu, wd)


def _ffn_block_bwd(name, res, dy):
    x, gain, rstd, xn, g, u, h, wg, wu, wd = res
    ns = wg.shape[0]
    dg, du = ffn_dh(name + "_dh", dy, wd, g, u, ns, 0.5)
    dxn = mm_nt_sharded(name + "_dxn_g", dg, wg)
    dxn = mm_nt_sharded(name + "_dxn_u", du, wu, resid=dxn)
    dwg = mm_tn_sharded(name + "_dwg", xn, dg, ns)
    dwu = mm_tn_sharded(name + "_dwu", xn, du, ns)
    dwd = mm_tn(name + "_dwd", h, dy, out_dtype=BF16, scale=0.5)
    dx, dgain = rms_bwd(name + "_dnorm", x, gain, rstd, dxn, dres=dy)
    return dx, dgain, dwg, dwu, dwd


ffn_block.defvjp(_ffn_block_fwd, _ffn_block_bwd)


@jax.custom_vjp
def mix_in(h, gain, w):
    return _mix_in_fwd(h, gain, w)[0]


def _mix_in_fwd(h, gain, w):
    u, rstd = rms_fwd("mix_norm", h, gain, BF16)
    return mm_nn("mix_inproj", u, w), (h, gain, rstd, u, w)


def _mix_in_bwd(res, dproj):
    h, gain, rstd, u, w = res
    du = mm_nt("mix_du", dproj, w)
    dw = mm_tn("mix_dwin", u, dproj, out_dtype=BF16)
    dh, dgain = rms_bwd("mix_dnorm", h, gain, rstd, du)
    return dh, dgain, dw


mix_in.defvjp(_mix_in_fwd, _mix_in_bwd)


@jax.custom_vjp
def mix_out(h, o_fox, o_swa, g_fox, g_swa, w):
    return _mix_out_fwd(h, o_fox, o_swa, g_fox, g_swa, w)[0]


def _mix_out_fwd(h, o_fox, o_swa, g_fox, g_swa, w):
    nf, r_fox = rms_fwd("out_norm_fox", o_fox, g_fox, BF16)
    nsw, r_swa = rms_fwd("out_norm_swa", o_swa, g_swa, BF16)
    o = jnp.concatenate([nf, nsw], axis=-1)
    return mm_nn("out_proj", o, w, resid=h), (o_fox, o_swa, g_fox, g_swa, r_fox, r_swa, o, w)


def _mix_out_bwd(res, dh2):
    o_fox, o_swa, g_fox, g_swa, r_fox, r_swa, o, w = res
    do = mm_nt("out_do", dh2, w)
    dw = mm_tn("out_dw", o, dh2, out_dtype=BF16)
    cf = o_fox.shape[1]
    d_fox, dg_fox = rms_bwd("out_dnorm_fox", o_fox, g_fox, r_fox, do[:, :cf])
    d_swa, dg_swa = rms_bwd("out_dnorm_swa", o_swa, g_swa, r_swa, do[:, cf:])
    return dh2, d_fox, d_swa, dg_fox, dg_swa, dw


mix_out.defvjp(_mix_out_fwd, _mix_out_bwd)


FOX_TQ = 256


def _fox_scores(q_ref, k_ref, cq_ref, ck_ref, i, tq):
    s_len = k_ref.shape[0]
    s = lax.dot_general(q_ref[...].astype(BF16), k_ref[...].astype(BF16), _DIMS["nt"], preferred_element_type=F32)
    s = s * (HEAD_DIM ** -0.5) + cq_ref[...] - ck_ref[...]
    row = lax.broadcasted_iota(jnp.int32, (tq, s_len), 0) + i * tq
    col = lax.broadcasted_iota(jnp.int32, (tq, s_len), 1)
    return jnp.where(row >= col, s, MASK_VALUE)


def fox_fwd(q, k, v, cq, ck):
    h, s_len, d = q.shape
    tq = min(FOX_TQ, s_len)

    def body(q_ref, k_ref, v_ref, cq_ref, ck_ref, o_ref, lse_ref):
        s = _fox_scores(q_ref, k_ref, cq_ref, ck_ref, pl.program_id(1), tq)
        m = jnp.max(s, axis=-1, keepdims=True)
        p = jnp.exp(s - m)
        l = jnp.sum(p, axis=-1, keepdims=True)
        o = jnp.dot(p.astype(BF16), v_ref[...].astype(BF16), preferred_element_type=F32)
        o_ref[...] = o / l
        lse_ref[...] = m + jnp.log(l)

    qb = pl.BlockSpec((None, tq, d), lambda hh, i: (hh, i, 0))
    kb = pl.BlockSpec((None, s_len, d), lambda hh, i: (hh, 0, 0))
    colb = pl.BlockSpec((None, tq, 1), lambda hh, i: (hh, i, 0))
    return pl.pallas_call(
        body, name="fox_fwd", grid=(h, s_len // tq),
        in_specs=[qb, kb, kb, colb, pl.BlockSpec((None, 1, s_len), lambda hh, i: (hh, 0, 0))],
        out_specs=[qb, colb],
        out_shape=[jax.ShapeDtypeStruct((h, s_len, d), F32), jax.ShapeDtypeStruct((h, s_len, 1), F32)],
        compiler_params=_cparams(("parallel", "parallel")),
    )(q, k, v, cq, ck)


def fox_bwd(q, k, v, cq, ck, o, lse, do):
    h, s_len, d = q.shape
    tq = min(FOX_TQ, s_len)
    scale = HEAD_DIM ** -0.5

    def body(q_ref, k_ref, v_ref, cq_ref, ck_ref, o_ref, lse_ref, do_ref, dq_ref, dk_ref, dv_ref, dcq_ref, dck_ref):
        i = pl.program_id(1)

        @pl.when(i == 0)
        def _():
            dk_ref[...] = jnp.zeros_like(dk_ref)
            dv_ref[...] = jnp.zeros_like(dv_ref)
            dck_ref[...] = jnp.zeros_like(dck_ref)

        s = _fox_scores(q_ref, k_ref, cq_ref, ck_ref, i, tq)
        p = jnp.exp(s - lse_ref[...])
        dof = do_ref[...]
        dob = dof.astype(BF16)
        delta = jnp.sum(dof * o_ref[...], axis=-1, keepdims=True)
        dp = lax.dot_general(dob, v_ref[...].astype(BF16), _DIMS["nt"], preferred_element_type=F32)
        ds = p * (dp - delta)
        dsb = ds.astype(BF16)
        dv_ref[...] += lax.dot_general(p.astype(BF16), dob, _DIMS["tn"], preferred_element_type=F32)
        dq_ref[...] = jnp.dot(dsb, k_ref[...].astype(BF16), preferred_element_type=F32) * scale
        dk_ref[...] += lax.dot_general(dsb, q_ref[...].astype(BF16), _DIMS["tn"], preferred_element_type=F32) * scale
        dcq_ref[...] = jnp.sum(ds, axis=-1, keepdims=True)
        dck_ref[...] -= jnp.sum(ds, axis=0, keepdims=True)

    qb = pl.BlockSpec((None, tq, d), lambda hh, i: (hh, i, 0))
    kb = pl.BlockSpec((None, s_len, d), lambda hh, i: (hh, 0, 0))
    colb = pl.BlockSpec((None, tq, 1), lambda hh, i: (hh, i, 0))
    rowb = pl.BlockSpec((None, 1, s_len), lambda hh, i: (hh, 0, 0))
    return pl.pallas_call(
        body, name="fox_bwd", grid=(h, s_len // tq),
        in_specs=[qb, kb, kb, colb, rowb, qb, colb, qb],
        out_specs=[qb, kb, kb, colb, rowb],
        out_shape=[jax.ShapeDtypeStruct((h, s_len, d), F32)] * 3
        + [jax.ShapeDtypeStruct((h, s_len, 1), F32), jax.ShapeDtypeStruct((h, 1, s_len), F32)],
        compiler_params=_cparams(("parallel", "arbitrary")),
    )(q, k, v, cq, ck, o, lse, do)


@jax.custom_vjp
def fox_attention(q, k, v, cq, ck):
    return fox_fwd(q, k, v, cq, ck)[0]


def _fox_attention_fwd(q, k, v, cq, ck):
    o, lse = fox_fwd(q, k, v, cq, ck)
    return o, (q, k, v, cq, ck, o, lse)


def _fox_attention_bwd(res, do):
    return fox_bwd(*res, do)


fox_attention.defvjp(_fox_attention_fwd, _fox_attention_bwd)


def _swa_probs(q_ref, kp_ref, kc_ref, sink_ref, n):
    g, w, d = q_ref.shape
    q = q_ref[...].reshape(g * w, d).astype(BF16)
    kw = jnp.concatenate([kp_ref[...], kc_ref[...]], axis=0).astype(BF16)
    s = lax.dot_general(q, kw, _DIMS["nt"], preferred_element_type=F32) * (HEAD_DIM ** -0.5)
    t = lax.broadcasted_iota(jnp.int32, (g * w, 2 * w), 0) & (w - 1)
    col = lax.broadcasted_iota(jnp.int32, (g * w, 2 * w), 1)
    rel = t + w - col
    valid = (rel >= 0) & (rel < w) & ((col >= w) | (n > 0))
    s = jnp.where(valid, s, MASK_VALUE)
    sink = sink_ref[...]
    m = jnp.maximum(jnp.max(s, axis=-1, keepdims=True), sink)
    p = jnp.exp(s - m)
    ps = jnp.exp(sink - m)
    linv = 1.0 / (jnp.sum(p, axis=-1, keepdims=True) + ps)
    return q, kw, p * linv, ps * linv


def swa_fwd(q, k, v, sink):
    hk, g, s_len, d = q.shape
    w = WINDOW
    assert w & (w - 1) == 0 and s_len % w == 0

    def body(q_ref, kp_ref, kc_ref, vp_ref, vc_ref, sink_ref, o_ref):
        _, _, p, _ = _swa_probs(q_ref, kp_ref, kc_ref, sink_ref, pl.program_id(1))
        vw = jnp.concatenate([vp_ref[...], vc_ref[...]], axis=0).astype(BF16)
        o_ref[...] = jnp.dot(p.astype(BF16), vw, preferred_element_type=F32).reshape(g, w, d)

    qb = pl.BlockSpec((None, g, w, d), lambda hh, n: (hh, 0, n, 0))
    prev = pl.BlockSpec((None, w, d), lambda hh, n: (hh, jnp.maximum(n - 1, 0), 0))
    cur = pl.BlockSpec((None, w, d), lambda hh, n: (hh, n, 0))
    sb = pl.BlockSpec((None, g * w, 1), lambda hh, n: (hh, 0, 0))
    return pl.pallas_call(
        body, name="swa_fwd", grid=(hk, s_len // w),
        in_specs=[qb, prev, cur, prev, cur, sb], out_specs=qb,
        out_shape=jax.ShapeDtypeStruct((hk, g, s_len, d), F32),
        compiler_params=_cparams(("parallel", "parallel")),
    )(q, k, k, v, v, sink)


def swa_bwd(q, k, v, sink, o, do):
    hk, g, s_len, d = q.shape
    w = WINDOW
    scale = HEAD_DIM ** -0.5

    def body(q_ref, kp_ref, kc_ref, vp_ref, vc_ref, sink_ref, o_ref, do_ref, dq_ref, dkp_ref, dkc_ref, dvp_ref, dvc_ref, dsink_ref):
        n = pl.program_id(1)

        @pl.when(n == 0)
        def _():
            dsink_ref[...] = jnp.zeros_like(dsink_ref)

        q, kw, p, ps = _swa_probs(q_ref, kp_ref, kc_ref, sink_ref, n)
        vw = jnp.concatenate([vp_ref[...], vc_ref[...]], axis=0).astype(BF16)
        dof = do_ref[...].reshape(g * w, d)
        dob = dof.astype(BF16)
        delta = jnp.sum(dof * o_ref[...].reshape(g * w, d), axis=-1, keepdims=True)
        dp = lax.dot_general(dob, vw, _DIMS["nt"], preferred_element_type=F32)
        ds = p * (dp - delta)
        dsb = ds.astype(BF16)
        dsink_ref[...] -= ps * delta
        dq_ref[...] = (jnp.dot(dsb, kw, preferred_element_type=F32) * scale).reshape(g, w, d)
        dkw = lax.dot_general(dsb, q, _DIMS["tn"], preferred_element_type=F32) * scale
        dvw = lax.dot_general(p.astype(BF16), dob, _DIMS["tn"], preferred_element_type=F32)
        dkp_ref[...] = dkw[:w]
        dkc_ref[...] = dkw[w:]
        dvp_ref[...] = dvw[:w]
        dvc_ref[...] = dvw[w:]

    qb = pl.BlockSpec((None, g, w, d), lambda hh, n: (hh, 0, n, 0))
    prev = pl.BlockSpec((None, w, d), lambda hh, n: (hh, jnp.maximum(n - 1, 0), 0))
    cur = pl.BlockSpec((None, w, d), lambda hh, n: (hh, n, 0))
    sb = pl.BlockSpec((None, g * w, 1), lambda hh, n: (hh, 0, 0))
    kv_shape = jax.ShapeDtypeStruct((hk, s_len, d), F32)
    return pl.pallas_call(
        body, name="swa_bwd", grid=(hk, s_len // w),
        in_specs=[qb, prev, cur, prev, cur, sb, qb, qb],
        out_specs=[qb, cur, cur, cur, cur, sb],
        out_shape=[jax.ShapeDtypeStruct((hk, g, s_len, d), F32), kv_shape, kv_shape, kv_shape, kv_shape,
                   jax.ShapeDtypeStruct((hk, g * w, 1), F32)],
        compiler_params=_cparams(("parallel", "arbitrary")),
    )(q, k, k, v, v, sink, o, do)


@jax.custom_vjp
def swa_attention(q, k, v, sink):
    return swa_fwd(q, k, v, sink)


def _swa_attention_fwd(q, k, v, sink):
    o = swa_fwd(q, k, v, sink)
    return o, (q, k, v, sink, o)


def _shift_up(a, w):
    return jnp.concatenate([a[:, w:], jnp.zeros_like(a[:, :w])], axis=1)


def _swa_attention_bwd(res, do):
    q, k, v, sink, o = res
    dq, dkp, dkc, dvp, dvc, dsink = swa_bwd(q, k, v, sink, o, do)
    return dq, dkc + _shift_up(dkp, WINDOW), dvc + _shift_up(dvp, WINDOW), dsink


swa_attention.defvjp(_swa_attention_fwd, _swa_attention_bwd)


def loss_call(y, target):
    s, d = y.shape
    tm = _pick(s, (512, 256, 128))

    def body(y_ref, t_ref, l_ref, dy_ref):
        e = y_ref[...] - t_ref[...]
        dy_ref[...] = e * (1.0 / d)

        @pl.when(pl.program_id(0) == 0)
        def _():
            l_ref[...] = jnp.zeros_like(l_ref)

        l_ref[...] += jnp.sum(jnp.sum(e * e, axis=0, keepdims=True), axis=1, keepdims=True) * (0.5 / d)

    row = pl.BlockSpec((tm, d), lambda i: (i, 0))
    l, dy = pl.pallas_call(
        body, name="loss_head", grid=(s // tm,), in_specs=[row, row],
        out_specs=[pl.BlockSpec((1, 1), lambda i: (0, 0)), row],
        out_shape=[jax.ShapeDtypeStruct((1, 1), F32), jax.ShapeDtypeStruct((s, d), F32)],
        compiler_params=_cparams(("arbitrary",)),
    )(y, target)
    return l[0, 0], dy


@jax.custom_vjp
def loss_head(y, target):
    return loss_call(y, target)[0]


def _loss_head_fwd(y, target):
    l, dy = loss_call(y, target)
    return l, dy


def _loss_head_bwd(dy, dl):
    return dy * dl, jnp.zeros_like(dy)


loss_head.defvjp(_loss_head_fwd, _loss_head_bwd)


def _rope(x, cos, sin):
    x1, x2 = x[..., : HEAD_DIM // 2], x[..., HEAD_DIM // 2:]
    return jnp.concatenate([x1 * cos - x2 * sin, x2 * cos + x1 * sin], axis=-1)


def _heads(a, nh):
    return a.reshape(a.shape[0], nh, HEAD_DIM).transpose(1, 0, 2)


def _unheads(a):
    return a.transpose(1, 0, 2).reshape(a.shape[1], a.shape[0] * HEAD_DIM)


def _win_layout(d_model):
    hf = hq = d_model // (2 * HEAD_DIM)
    hk = hq // 4
    sizes = [hf * HEAD_DIM, hf * HEAD_DIM, hf * HEAD_DIM, hf, hq * HEAD_DIM, hk * HEAD_DIM, hk * HEAD_DIM]
    return hf, hq, hk, sizes


def _pad_cols(a, n):
    return a if a.shape[1] == n else jnp.pad(a, ((0, 0), (0, n - a.shape[1])))


def _relayout_win(win_g):
    ns, d_model, cs = win_g.shape
    hf, hq, hk, sizes = _win_layout(d_model)
    full = win_g.transpose(1, 0, 2).reshape(d_model, ns * cs)
    offs = [0]
    for sz in sizes:
        offs.append(offs[-1] + sz)
    seg = [full[:, offs[i]:offs[i + 1]] for i in range(7)]
    qf, kf, vf, fl, qs, ks, vs = seg
    body = jnp.concatenate([qf, kf, vf, qs, ks, vs, _pad_cols(fl, 128)], axis=1)
    return _pad_cols(body, -(-body.shape[1] // 512) * 512)


def _forward_loss(big, small, x, positions, target):
    s_len, d_model = x.shape
    hf, hq, hk, sizes = _win_layout(d_model)
    grp = hq // hk
    sm = {k: v[0] for k, v in small.items()}

    h1 = ffn_block("ffn1", x, sm["norm_ffn1_g"], big["ffn1_w_gate"], big["ffn1_w_up"], big["ffn1_w_down"])

    proj = mix_in(h1, sm["norm_mix_g"], _relayout_win(big["w_in"]))
    o0 = 0
    q_f = proj[:, o0:o0 + sizes[0]]; o0 += sizes[0]
    k_f = proj[:, o0:o0 + sizes[1]]; o0 += sizes[1]
    v_f = proj[:, o0:o0 + sizes[2]]; o0 += sizes[2]
    q_s = proj[:, o0:o0 + sizes[4]]; o0 += sizes[4]
    k_s = proj[:, o0:o0 + sizes[5]]; o0 += sizes[5]
    v_s = proj[:, o0:o0 + sizes[6]]; o0 += sizes[6]
    f_logit = proj[:, o0:o0 + hf]

    q_f = rms_op("fox_qnorm", _heads(q_f, hf).reshape(hf * s_len, HEAD_DIM), sm["fox_q_norm_g"]).reshape(hf, s_len, HEAD_DIM)
    k_f = rms_op("fox_knorm", _heads(k_f, hf).reshape(hf * s_len, HEAD_DIM), sm["fox_k_norm_g"]).reshape(hf, s_len, HEAD_DIM)
    v_f = _heads(v_f, hf)
    log_f = jax.nn.log_sigmoid(f_logit + sm["b_forget"])
    c = jnp.cumsum(log_f, axis=0).T
    o_fox = _unheads(fox_attention(q_f, k_f, v_f, c[:, :, None], c[:, None, :]))

    inv_freq = ROPE_THETA ** (-jnp.arange(0, HEAD_DIM, 2, dtype=F32) / HEAD_DIM)
    ang = positions.astype(F32)[:, None] * inv_freq
    cos, sin = jnp.cos(ang), jnp.sin(ang)
    q_s = rms_op("swa_qnorm", _heads(q_s, hq).reshape(hq * s_len, HEAD_DIM), sm["swa_q_norm_g"]).reshape(hq, s_len, HEAD_DIM)
    k_s = rms_op("swa_knorm", _heads(k_s, hk).reshape(hk * s_len, HEAD_DIM), sm["swa_k_norm_g"]).reshape(hk, s_len, HEAD_DIM)
    q_s = _rope(q_s, cos, sin).reshape(hk, grp, s_len, HEAD_DIM)
    k_s = _rope(k_s, cos, sin)
    v_s = _heads(v_s, hk)
    sink = jnp.broadcast_to(sm["swa_sinks"].reshape(hk, grp, 1, 1), (hk, grp, WINDOW, 1)).reshape(hk, grp * WINDOW, 1)
    o_swa = _unheads(swa_attention(q_s, k_s, v_s, sink).reshape(hq, s_len, HEAD_DIM))

    h2 = mix_out(h1, o_fox, o_swa, sm["out_norm_fox_g"], sm["out_norm_swa_g"], big["w_out"])
    y = ffn_block("ffn2", h2, sm["norm_ffn2_g"], big["ffn2_w_gate"], big["ffn2_w_up"], big["ffn2_w_down"])
    return loss_head(y, target)


_HBM = pl.BlockSpec(memory_space=pl.ANY)


def _me():
    return lax.axis_index("x"), lax.axis_index("y"), lax.axis_index("c")


def _other_chips(x, y):
    return [(1 - x, y), (x, 1 - y), (1 - x, 1 - y)]


def _half(ref, h):
    rows = ref.shape[-2] // 2
    idx = (slice(None),) * (len(ref.shape) - 2) + (pl.ds(h * rows, rows), slice(None))
    return ref.at[idx]


def all_gather_weights(shards):
    n = len(shards)

    def body(*refs):
        ins, outs = refs[:n], refs[n:2 * n]
        send_sems, recv_sems, local_sems = refs[2 * n:]
        x, y, c = _me()
        p = 2 * x + y
        sibling = (x, y, 1 - c)
        chips = _other_chips(x, y)

        local = [pltpu.make_async_copy(ins[w], outs[w].at[p], local_sems.at[w]) for w in range(n)]
        for cp in local:
            cp.start()

        def remote(w, k, src, dst, to):
            return pltpu.make_async_remote_copy(src_ref=src, dst_ref=dst, send_sem=send_sems.at[w, k], recv_sem=recv_sems.at[w, k],
                                                device_id=to, device_id_type=_MESH)

        first = []
        for w in range(n):
            for j, chip in enumerate(chips):
                first.append(remote(w, j, _half(ins[w], c), _half(outs[w].at[p], c), (*chip, c)))
                first[-1].start()
        passed = []
        for j, chip in enumerate(chips):
            q = 2 * chip[0] + chip[1]
            for w in range(n):
                landed = _half(outs[w].at[q], c)
                remote(w, j, landed, landed, (*chip, c)).wait_recv()
                passed.append(remote(w, 3 + j, landed, landed, sibling))
                passed[-1].start()
        for j, chip in enumerate(chips):
            q = 2 * chip[0] + chip[1]
            for w in range(n):
                theirs = _half(outs[w].at[q], 1 - c)
                remote(w, 3 + j, theirs, theirs, sibling).wait_recv()
        for cp in first + passed:
            cp.wait_send()
        for cp in local:
            cp.wait()

    return pl.pallas_call(
        body, name="all_gather_weights",
        in_specs=[_HBM] * n, out_specs=[_HBM] * n,
        out_shape=[jax.ShapeDtypeStruct((N_CHIPS,) + s.shape, s.dtype) for s in shards],
        scratch_shapes=[pltpu.SemaphoreType.DMA((n, 6)), pltpu.SemaphoreType.DMA((n, 6)), pltpu.SemaphoreType.DMA((n,))],
    )(*shards)


def exchange_halves(grads):
    n = len(grads)

    def body(*refs):
        ins, outs = refs[:n], refs[n:2 * n]
        send_sems, recv_sems = refs[2 * n:]
        x, y, c = _me()
        cps = []
        for w in range(n):
            cps.append(pltpu.make_async_remote_copy(
                src_ref=_half(ins[w], 1 - c), dst_ref=outs[w], send_sem=send_sems.at[w], recv_sem=recv_sems.at[w],
                device_id=(x, y, 1 - c), device_id_type=_MESH))
            cps[-1].start()
        for cp in cps:
            cp.wait()

    return pl.pallas_call(
        body, name="grad_exchange_halves",
        in_specs=[_HBM] * n, out_specs=[_HBM] * n,
        out_shape=[jax.ShapeDtypeStruct((g.shape[0], g.shape[1] // 2, g.shape[2]), g.dtype) for g in grads],
        scratch_shapes=[pltpu.SemaphoreType.DMA((n,)), pltpu.SemaphoreType.DMA((n,))],
    )(*grads)


def _row_tile(rows, cols, itemsize):
    target = max(16, (1 << 20) // (cols * itemsize))
    for t in (1024, 512, 256, 128, 64, 32, 16):
        if t <= target and rows % t == 0:
            return t
    return rows


def chip_sum(name, grad, theirs, c_idx):
    ns, r, cols = grad.shape
    rh = r // 2
    tr = _row_tile(rh, cols, 2)
    nb = rh // tr

    def body(c_ref, a_ref, b_ref, o_ref):
        o_ref[...] = (a_ref[...].astype(F32) + b_ref[...].astype(F32)).astype(o_ref.dtype)

    return pl.pallas_call(
        body, name=name,
        grid_spec=pltpu.PrefetchScalarGridSpec(
            num_scalar_prefetch=1, grid=(ns, nb),
            in_specs=[pl.BlockSpec((None, tr, cols), lambda q, i, cr: (q, cr[0] * nb + i, 0)),
                      pl.BlockSpec((None, tr, cols), lambda q, i, cr: (q, i, 0))],
            out_specs=pl.BlockSpec((None, tr, cols), lambda q, i, cr: (q, i, 0))),
        out_shape=jax.ShapeDtypeStruct((ns, rh, cols), BF16),
        compiler_params=_cparams(("parallel", "parallel")),
    )(c_idx, grad, theirs)


def scatter_to_owner(sums):
    n = len(sums)

    def body(*refs):
        ins, outs = refs[:n], refs[n:2 * n]
        send_sems, recv_sems = refs[2 * n:]
        x, y, c = _me()
        cps = []
        for w in range(n):
            for j, chip in enumerate(_other_chips(x, y)):
                q = 2 * chip[0] + chip[1]
                cps.append(pltpu.make_async_remote_copy(
                    src_ref=ins[w].at[q], dst_ref=outs[w].at[j], send_sem=send_sems.at[w, j], recv_sem=recv_sems.at[w, j],
                    device_id=(*chip, c), device_id_type=_MESH))
                cps[-1].start()
        for cp in cps:
            cp.wait()

    return pl.pallas_call(
        body, name="grad_scatter_to_owner",
        in_specs=[_HBM] * n, out_specs=[_HBM] * n,
        out_shape=[jax.ShapeDtypeStruct((3,) + s.shape[1:], s.dtype) for s in sums],
        scratch_shapes=[pltpu.SemaphoreType.DMA((n, 3)), pltpu.SemaphoreType.DMA((n, 3))],
    )(*sums)


def owner_sum(name, sums, got, p_idx):
    ns, rh, cols = sums.shape
    tr = _row_tile(rh, cols, 4)

    def body(p_ref, a_ref, b_ref, o_ref):
        o_ref[...] = ((a_ref[...].astype(F32) + b_ref[0].astype(F32)) + b_ref[1].astype(F32)) + b_ref[2].astype(F32)

    return pl.pallas_call(
        body, name=name,
        grid_spec=pltpu.PrefetchScalarGridSpec(
            num_scalar_prefetch=1, grid=(rh // tr,),
            in_specs=[pl.BlockSpec((None, tr, cols), lambda i, pr: (pr[0], i, 0)),
                      pl.BlockSpec((3, tr, cols), lambda i, pr: (0, i, 0))],
            out_specs=pl.BlockSpec((tr, cols), lambda i, pr: (i, 0))),
        out_shape=jax.ShapeDtypeStruct((rh, cols), F32),
        compiler_params=_cparams(("parallel",)),
    )(p_idx, sums, got)


def join_halves(halves):
    n = len(halves)

    def body(*refs):
        ins, outs = refs[:n], refs[n:2 * n]
        send_sems, recv_sems, local_sems = refs[2 * n:]
        x, y, c = _me()
        cps = []
        for w in range(n):
            mine = pltpu.make_async_copy(ins[w], _half(outs[w], c), local_sems.at[w])
            mine.start()
            cps.append(mine)
            cps.append(pltpu.make_async_remote_copy(
                src_ref=ins[w], dst_ref=_half(outs[w], c), send_sem=send_sems.at[w], recv_sem=recv_sems.at[w],
                device_id=(x, y, 1 - c), device_id_type=_MESH))
            cps[-1].start()
        for cp in cps:
            cp.wait()

    return pl.pallas_call(
        body, name="grad_join_halves",
        in_specs=[_HBM] * n, out_specs=[_HBM] * n,
        out_shape=[jax.ShapeDtypeStruct((2 * h.shape[0], h.shape[1]), h.dtype) for h in halves],
        scratch_shapes=[pltpu.SemaphoreType.DMA((n,)), pltpu.SemaphoreType.DMA((n,)), pltpu.SemaphoreType.DMA((n,))],
    )(*halves)


def all_reduce_small(v):
    rows, lanes = v.shape

    def body(v_ref, o_ref, slots, send_sems, recv_sems):
        x, y, c = _me()
        me = 4 * x + 2 * y + c
        slots[me] = v_ref[...]
        cps = []
        for k in range(1, N_DEV):
            peer = (x ^ (k >> 2), y ^ ((k >> 1) & 1), c ^ (k & 1))
            cps.append(pltpu.make_async_remote_copy(
                src_ref=v_ref, dst_ref=slots.at[me], send_sem=send_sems.at[k - 1], recv_sem=recv_sems.at[k - 1],
                device_id=peer, device_id_type=_MESH))
            cps[-1].start()
        for k in range(1, N_DEV):
            theirs = slots.at[me ^ k]
            pltpu.make_async_remote_copy(
                src_ref=theirs, dst_ref=theirs, send_sem=send_sems.at[k - 1], recv_sem=recv_sems.at[k - 1],
                device_id=(x, y, c), device_id_type=_MESH).wait_recv()
        for cp in cps:
            cp.wait_send()
        acc = slots[0]
        for i in range(1, N_DEV):
            acc = acc + slots[i]
        o_ref[...] = acc

    return pl.pallas_call(
        body, name="all_reduce_small",
        in_specs=[pl.BlockSpec(memory_space=pltpu.VMEM)], out_specs=pl.BlockSpec(memory_space=pltpu.VMEM),
        out_shape=jax.ShapeDtypeStruct((rows, lanes), F32),
        scratch_shapes=[pltpu.VMEM((N_DEV, rows, lanes), F32), pltpu.SemaphoreType.DMA((N_DEV - 1,)), pltpu.SemaphoreType.DMA((N_DEV - 1,))],
    )(v)


def adamw(name, w, g, m, v):
    r, cols = w.shape
    tr = _row_tile(r, cols, 4)
    c1 = 1.0 / (1.0 - ADAM_B1 ** ADAM_STEP)
    c2 = 1.0 / (1.0 - ADAM_B2 ** ADAM_STEP)

    def body(w_ref, g_ref, m_ref, v_ref, d_ref, nm_ref, nv_ref):
        gv = g_ref[...]
        nm = ADAM_B1 * m_ref[...] + (1.0 - ADAM_B1) * gv
        nv = ADAM_B2 * v_ref[...] + (1.0 - ADAM_B2) * (gv * gv)
        d_ref[...] = -ADAM_LR * ((nm * c1) / (jnp.sqrt(nv * c2) + ADAM_EPS) + ADAM_WD * w_ref[...])
        nm_ref[...] = nm
        nv_ref[...] = nv

    blk = pl.BlockSpec((tr, cols), lambda i: (i, 0))
    return pl.pallas_call(
        body, name=name, grid=(r // tr,), in_specs=[blk] * 4, out_specs=[blk] * 3,
        out_shape=[jax.ShapeDtypeStruct((r, cols), F32)] * 3,
        compiler_params=_cparams(("parallel",)),
    )(w, g, m, v)


_BIG = ("ffn1_w_gate", "ffn1_w_up", "ffn1_w_down", "w_in", "w_out", "ffn2_w_gate", "ffn2_w_up", "ffn2_w_down")
_ROW_SHARDED = ("ffn1_w_down", "w_out", "ffn2_w_down")
_SMALL = ("norm_ffn1_g", "norm_mix_g", "b_forget", "fox_q_norm_g", "fox_k_norm_g", "swa_q_norm_g", "swa_k_norm_g", "swa_sinks",
          "out_norm_fox_g", "out_norm_swa_g", "norm_ffn2_g")
_ALL = ("norm_ffn1_g", "ffn1_w_gate", "ffn1_w_up", "ffn1_w_down", "norm_mix_g", "w_in", "b_forget", "fox_q_norm_g", "fox_k_norm_g",
        "swa_q_norm_g", "swa_k_norm_g", "swa_sinks", "out_norm_fox_g", "out_norm_swa_g", "w_out", "norm_ffn2_g", "ffn2_w_gate",
        "ffn2_w_up", "ffn2_w_down")
_LANES = 128


def _pack_small(d):
    parts = []
    for k in _SMALL:
        v = d[k].reshape(-1)
        rows = -(-v.shape[0] // _LANES)
        parts.append(jnp.pad(v, (0, rows * _LANES - v.shape[0])).reshape(rows, _LANES))
    a = jnp.concatenate(parts, axis=0)
    return jnp.pad(a, ((0, -a.shape[0] % 8), (0, 0)))


def _unpack_small(a, like):
    out, r0 = {}, 0
    for k in _SMALL:
        nvals = like[k].shape[1]
        rows = -(-nvals // _LANES)
        out[k] = a[r0:r0 + rows].reshape(-1)[:nvals].reshape(1, nvals)
        r0 += rows
    return out


def kernel(x, positions, norm_ffn1_g, ffn1_w_gate, ffn1_w_up, ffn1_w_down, norm_mix_g, w_in, b_forget, fox_q_norm_g, fox_k_norm_g, swa_q_norm_g, swa_k_norm_g, swa_sinks, out_norm_fox_g, out_norm_swa_g, w_out, norm_ffn2_g, ffn2_w_gate, ffn2_w_up, ffn2_w_down, loss_target, m_norm_ffn1_g, m_ffn1_w_gate, m_ffn1_w_up, m_ffn1_w_down, m_norm_mix_g, m_w_in, m_b_forget, m_fox_q_norm_g, m_fox_k_norm_g, m_swa_q_norm_g, m_swa_k_norm_g, m_swa_sinks, m_out_norm_fox_g, m_out_norm_swa_g, m_w_out, m_norm_ffn2_g, m_ffn2_w_gate, m_ffn2_w_up, m_ffn2_w_down, v_norm_ffn1_g, v_ffn1_w_gate, v_ffn1_w_up, v_ffn1_w_down, v_norm_mix_g, v_w_in, v_b_forget, v_fox_q_norm_g, v_fox_k_norm_g, v_swa_q_norm_g, v_swa_k_norm_g, v_swa_sinks, v_out_norm_fox_g, v_out_norm_swa_g, v_w_out, v_norm_ffn2_g, v_ffn2_w_gate, v_ffn2_w_up, v_ffn2_w_down):
    args = dict(locals())
    w = {k: args[k] for k in _ALL}
    m = {k: args["m_" + k] for k in _ALL}
    v = {k: args["v_" + k] for k in _ALL}
    c_idx = lax.axis_index("c").astype(jnp.int32).reshape(1)
    p_idx = (2 * lax.axis_index("x") + lax.axis_index("y")).astype(jnp.int32).reshape(1)

    gathered = all_gather_weights([w[k][0].astype(BF16) for k in _BIG])
    big = dict(zip(_BIG, gathered))
    for k in _ROW_SHARDED:
        big[k] = big[k].reshape(-1, big[k].shape[-1])
    small = {k: w[k] for k in _SMALL}

    loss, (g_big, g_small, grad_x) = jax.value_and_grad(_forward_loss, argnums=(0, 1, 2))(
        big, small, x[0], positions[0], loss_target[0])
    loss = lax.psum(loss, ("x", "y", "c"))

    grads = [g_big[k].reshape((N_CHIPS, -1, g_big[k].shape[-1])) for k in _BIG]
    theirs = exchange_halves(grads)
    sums = [chip_sum("chip_sum_" + k, g, t, c_idx) for k, g, t in zip(_BIG, grads, theirs)]
    got = scatter_to_owner(sums)
    halves = [owner_sum("owner_sum_" + k, s, r, p_idx) for k, s, r in zip(_BIG, sums, got)]
    g_shard = dict(zip(_BIG, join_halves(halves)))
    g_small_sum = _unpack_small(all_reduce_small(_pack_small(g_small)), small)

    grad_w, delta, new_m, new_v = {}, {}, {}, {}
    for k in _BIG:
        grad_w[k] = g_shard[k][None]
        d, nm, nv = adamw("adamw_" + k, w[k][0], g_shard[k], m[k][0], v[k][0])
        delta[k], new_m[k], new_v[k] = d[None], nm[None], nv[None]
    d, nm, nv = adamw("adamw_small", _pack_small(small), _pack_small(g_small_sum), _pack_small({k: m[k] for k in _SMALL}),
                      _pack_small({k: v[k] for k in _SMALL}))
    grad_w.update(g_small_sum)
    delta.update(_unpack_small(d, small))
    new_m.update(_unpack_small(nm, small))
    new_v.update(_unpack_small(nv, small))

    return (loss, grad_x[None], *[grad_w[k] for k in _ALL], *[delta[k] for k in _ALL], *[new_m[k] for k in _ALL], *[new_v[k] for k in _ALL])
```

```python
import functools
import math

import jax
import jax.numpy as jnp
from jax import lax
from jax.experimental import pallas as pl
from jax.experimental.pallas import tpu as pltpu

F32 = jnp.float32
BF16 = jnp.bfloat16

HEAD_DIM = 64
WINDOW = 128
ROPE_THETA = 10000.0
EPS = 1e-6
N_CHIPS = 4
N_DEV = 8

ADAM_LR = 0.001
ADAM_B1 = 0.9
ADAM_B2 = 0.999
ADAM_EPS = 1e-08
ADAM_WD = 0.01
ADAM_STEP = 10

V7X_VMEM_BYTES = 64 * 1024 * 1024
VMEM_LIMIT = V7X_VMEM_BYTES - 8 * 1024 * 1024
MASK_VALUE = -1e30

_MESH = pl.DeviceIdType.MESH


def _cparams(sem):
    return pltpu.CompilerParams(dimension_semantics=sem, vmem_limit_bytes=VMEM_LIMIT)


def _pick(n, prefs):
    for p in prefs:
        if n % p == 0:
            return p
    return n


_DIMS = {"nn": (((1,), (0,)), ((), ())), "nt": (((1,), (1,)), ((), ())), "tn": (((0,), (0,)), ((), ()))}


def _mm_call(name, mode, a, b, a_spec, b_spec, out_shape, out_spec, grid, acc_shape, scale=1.0, resid=None, resid_spec=None):
    nk = grid[2]
    dims = _DIMS[mode]
    has_resid = resid is not None

    def body(*refs):
        if has_resid:
            a_ref, b_ref, r_ref, o_ref, acc_ref = refs
        else:
            a_ref, b_ref, o_ref, acc_ref = refs
        k = pl.program_id(2)

        @pl.when(k == 0)
        def _():
            acc_ref[...] = jnp.zeros_like(acc_ref)

        acc_ref[...] += lax.dot_general(a_ref[...].astype(BF16), b_ref[...].astype(BF16), dims, preferred_element_type=F32)

        @pl.when(k == nk - 1)
        def _():
            r = acc_ref[...]
            if scale != 1.0:
                r = r * scale
            if has_resid:
                r = r_ref[...].astype(F32) + r
            o_ref[...] = r.astype(o_ref.dtype)

    in_specs = [a_spec, b_spec] + ([resid_spec] if has_resid else [])
    args = (a, b) + ((resid,) if has_resid else ())
    return pl.pallas_call(
        body, name=name, grid=grid, in_specs=in_specs, out_specs=out_spec, out_shape=out_shape,
        scratch_shapes=[pltpu.VMEM(acc_shape, F32)],
        compiler_params=_cparams(("parallel", "parallel", "arbitrary")),
    )(*args)


def mm_nn(name, a, b, *, out_dtype=F32, scale=1.0, resid=None):
    m, kd = a.shape
    n = b.shape[1]
    tm, tn, tk = _pick(m, (512, 256, 128)), _pick(n, (1024, 512, 256, 128)), _pick(kd, (1024, 512, 256, 128))
    return _mm_call(
        name, "nn", a, b, pl.BlockSpec((tm, tk), lambda i, j, k: (i, k)), pl.BlockSpec((tk, tn), lambda i, j, k: (k, j)),
        jax.ShapeDtypeStruct((m, n), out_dtype), pl.BlockSpec((tm, tn), lambda i, j, k: (i, j)),
        (m // tm, n // tn, kd // tk), (tm, tn), scale, resid, pl.BlockSpec((tm, tn), lambda i, j, k: (i, j)))


def mm_nt(name, a, b, *, out_dtype=F32, scale=1.0, resid=None):
    m, kd = a.shape
    n = b.shape[0]
    tm, tn, tk = _pick(m, (512, 256, 128)), _pick(n, (1024, 512, 256, 128)), _pick(kd, (1024, 512, 256, 128))
    return _mm_call(
        name, "nt", a, b, pl.BlockSpec((tm, tk), lambda i, j, k: (i, k)), pl.BlockSpec((tn, tk), lambda i, j, k: (j, k)),
        jax.ShapeDtypeStruct((m, n), out_dtype), pl.BlockSpec((tm, tn), lambda i, j, k: (i, j)),
        (m // tm, n // tn, kd // tk), (tm, tn), scale, resid, pl.BlockSpec((tm, tn), lambda i, j, k: (i, j)))


def mm_tn(name, a, b, *, out_dtype=F32, scale=1.0):
    kd, m = a.shape
    n = b.shape[1]
    tm, tn, tk = _pick(m, (512, 256, 128)), _pick(n, (1024, 512, 256, 128)), _pick(kd, (1024, 512, 256, 128))
    return _mm_call(
        name, "tn", a, b, pl.BlockSpec((tk, tm), lambda i, j, k: (k, i)), pl.BlockSpec((tk, tn), lambda i, j, k: (k, j)),
        jax.ShapeDtypeStruct((m, n), out_dtype), pl.BlockSpec((tm, tn), lambda i, j, k: (i, j)),
        (m // tm, n // tn, kd // tk), (tm, tn), scale)


def mm_nt_sharded(name, a, w, *, resid=None):
    m = a.shape[0]
    ns, n, c = w.shape
    tm, tn = _pick(m, (512, 256, 128)), _pick(n, (512, 256, 128))
    return _mm_call(
        name, "nt", a, w, pl.BlockSpec((tm, c), lambda i, j, k: (i, k)), pl.BlockSpec((None, tn, c), lambda i, j, k: (k, j, 0)),
        jax.ShapeDtypeStruct((m, n), F32), pl.BlockSpec((tm, tn), lambda i, j, k: (i, j)),
        (m // tm, n // tn, ns), (tm, tn), 1.0, resid, pl.BlockSpec((tm, tn), lambda i, j, k: (i, j)))


def mm_tn_sharded(name, a, b, ns):
    kd, m = a.shape
    c = b.shape[1] // ns
    tm, tk = _pick(m, (512, 256, 128)), _pick(kd, (1024, 512, 256, 128))
    return _mm_call(
        name, "tn", a, b, pl.BlockSpec((tk, tm), lambda i, j, k: (k, i)), pl.BlockSpec((tk, c), lambda i, j, k: (k, j)),
        jax.ShapeDtypeStruct((ns, m, c), BF16), pl.BlockSpec((None, tm, c), lambda i, j, k: (j, i, 0)),
        (m // tm, ns, kd // tk), (tm, c))


def rms_fwd(name, x, g, out_dtype):
    r, c = x.shape
    tm = _pick(r, (512, 256, 128, 64, 8))

    def body(x_ref, g_ref, y_ref, r_ref):
        xf = x_ref[...].astype(F32)
        rstd = lax.rsqrt(jnp.mean(xf * xf, axis=-1, keepdims=True) + EPS)
        y_ref[...] = ((xf * rstd) * g_ref[...]).astype(y_ref.dtype)
        r_ref[...] = rstd

    return pl.pallas_call(
        body, name=name, grid=(r // tm,),
        in_specs=[pl.BlockSpec((tm, c), lambda i: (i, 0)), pl.BlockSpec((1, c), lambda i: (0, 0))],
        out_specs=[pl.BlockSpec((tm, c), lambda i: (i, 0)), pl.BlockSpec((tm, 1), lambda i: (i, 0))],
        out_shape=[jax.ShapeDtypeStruct((r, c), out_dtype), jax.ShapeDtypeStruct((r, 1), F32)],
        compiler_params=_cparams(("parallel",)),
    )(x, g.reshape(1, c))


def rms_bwd(name, x, g, rstd, dy, dres=None):
    r, c = x.shape
    tm = _pick(r, (512, 256, 128, 64, 8))
    has_res = dres is not None

    def body(*refs):
        if has_res:
            x_ref, g_ref, r_ref, dy_ref, dres_ref, dx_ref, dg_ref = refs
        else:
            x_ref, g_ref, r_ref, dy_ref, dx_ref, dg_ref = refs
        xhat = x_ref[...].astype(F32) * r_ref[...]
        dyf = dy_ref[...].astype(F32)
        gdy = dyf * g_ref[...]
        dx = r_ref[...] * (gdy - xhat * jnp.mean(gdy * xhat, axis=-1, keepdims=True))
        if has_res:
            dx = dx + dres_ref[...]
        dx_ref[...] = dx

        @pl.when(pl.program_id(0) == 0)
        def _():
            dg_ref[...] = jnp.zeros_like(dg_ref)

        dg_ref[...] += jnp.sum(dyf * xhat, axis=0, keepdims=True)

    row = pl.BlockSpec((tm, c), lambda i: (i, 0))
    in_specs = [row, pl.BlockSpec((1, c), lambda i: (0, 0)), pl.BlockSpec((tm, 1), lambda i: (i, 0)), row] + ([row] if has_res else [])
    args = (x, g.reshape(1, c), rstd, dy) + ((dres,) if has_res else ())
    dx, dg = pl.pallas_call(
        body, name=name, grid=(r // tm,), in_specs=in_specs,
        out_specs=[row, pl.BlockSpec((1, c), lambda i: (0, 0))],
        out_shape=[jax.ShapeDtypeStruct((r, c), F32), jax.ShapeDtypeStruct((1, c), F32)],
        compiler_params=_cparams(("arbitrary",)),
    )(*args)
    return dx, dg.reshape(c)


@functools.partial(jax.custom_vjp, nondiff_argnums=(0,))
def rms_op(name, x, g):
    return rms_fwd(name + "_fwd", x, g, F32)[0]


def _rms_op_fwd(name, x, g):
    y, rstd = rms_fwd(name + "_fwd", x, g, F32)
    return y, (x, g, rstd)


def _rms_op_bwd(name, res, dy):
    x, g, rstd = res
    return rms_bwd(name + "_bwd", x, g, rstd, dy)


rms_op.defvjp(_rms_op_fwd, _rms_op_bwd)


def _sigmoid(x):
    return 1.0 / (1.0 + jnp.exp(-x))


def ffn_gu(name, xn, wg, wu):
    s, d = xn.shape
    ns, _, c = wg.shape
    tm, tk = _pick(s, (512, 256, 128)), _pick(d, (1024, 512, 256, 128))
    nk = d // tk

    def body(x_ref, wg_ref, wu_ref, g_ref, u_ref, h_ref, accg, accu):
        k = pl.program_id(2)

        @pl.when(k == 0)
        def _():
            accg[...] = jnp.zeros_like(accg)
            accu[...] = jnp.zeros_like(accu)

        xb = x_ref[...]
        accg[...] += jnp.dot(xb, wg_ref[...], preferred_element_type=F32)
        accu[...] += jnp.dot(xb, wu_ref[...], preferred_element_type=F32)

        @pl.when(k == nk - 1)
        def _():
            gv, uv = accg[...], accu[...]
            g_ref[...] = gv
            u_ref[...] = uv
            h_ref[...] = ((gv * _sigmoid(gv)) * uv).astype(BF16)

    w_spec = pl.BlockSpec((None, tk, c), lambda i, j, k: (j, k, 0))
    o_spec = pl.BlockSpec((tm, c), lambda i, j, k: (i, j))
    return pl.pallas_call(
        body, name=name, grid=(s // tm, ns, nk),
        in_specs=[pl.BlockSpec((tm, tk), lambda i, j, k: (i, k)), w_spec, w_spec],
        out_specs=[o_spec, o_spec, o_spec],
        out_shape=[jax.ShapeDtypeStruct((s, ns * c), F32), jax.ShapeDtypeStruct((s, ns * c), F32), jax.ShapeDtypeStruct((s, ns * c), BF16)],
        scratch_shapes=[pltpu.VMEM((tm, c), F32), pltpu.VMEM((tm, c), F32)],
        compiler_params=_cparams(("parallel", "parallel", "arbitrary")),
    )(xn, wg, wu)


def ffn_dh(name, dy, wd, g, u, ns, scale):
    s, d = dy.shape
    f = wd.shape[0]
    c = f // ns
    tm, tk = _pick(s, (512, 256, 128)), _pick(d, (1024, 512, 256, 128))
    nk = d // tk

    def body(dy_ref, wd_ref, g_ref, u_ref, dg_ref, du_ref, acc):
        k = pl.program_id(2)

        @pl.when(k == 0)
        def _():
            acc[...] = jnp.zeros_like(acc)

        acc[...] += lax.dot_general(dy_ref[...].astype(BF16), wd_ref[...], _DIMS["nt"], preferred_element_type=F32)

        @pl.when(k == nk - 1)
        def _():
            dh = acc[...] * scale
            gv, uv = g_ref[...], u_ref[...]
            sig = _sigmoid(gv)
            dg_ref[...] = (dh * uv * (sig * (1.0 + gv * (1.0 - sig)))).astype(BF16)
            du_ref[...] = (dh * (gv * sig)).astype(BF16)

    o_spec = pl.BlockSpec((tm, c), lambda i, j, k: (i, j))
    return pl.pallas_call(
        body, name=name, grid=(s // tm, ns, nk),
        in_specs=[pl.BlockSpec((tm, tk), lambda i, j, k: (i, k)), pl.BlockSpec((c, tk), lambda i, j, k: (j, k)), o_spec, o_spec],
        out_specs=[o_spec, o_spec],
        out_shape=[jax.ShapeDtypeStruct((s, f), BF16), jax.ShapeDtypeStruct((s, f), BF16)],
        scratch_shapes=[pltpu.VMEM((tm, c), F32)],
        compiler_params=_cparams(("parallel", "parallel", "arbitrary")),
    )(dy, wd, g, u)


@functools.partial(jax.custom_vjp, nondiff_argnums=(0,))
def ffn_block(name, x, gain, wg, wu, wd):
    return _ffn_block_fwd(name, x, gain, wg, wu, wd)[0]


def _ffn_block_fwd(name, x, gain, wg, wu, wd):
    xn, rstd = rms_fwd(name + "_norm", x, gain, BF16)
    g, u, h = ffn_gu(name + "_gu", xn, wg, wu)
    y = mm_nn(name + "_down", h, wd, scale=0.5, resid=x)
    return y, (x, gain, rstd, xn, g, u, h, wg, wu, wd)


def _ffn_block_bwd(name, res, dy):
    x, gain, rstd, xn, g, u, h, wg, wu, wd = res
    ns = wg.shape[0]
    dg, du = ffn_dh(name + "_dh", dy, wd, g, u, ns, 0.5)
    dxn = mm_nt_sharded(name + "_dxn_g", dg, wg)
    dxn = mm_nt_sharded(name + "_dxn_u", du, wu, resid=dxn)
    dwg = mm_tn_sharded(name + "_dwg", xn, dg, ns)
    dwu = mm_tn_sharded(name + "_dwu", xn, du, ns)
    dwd = mm_tn(name + "_dwd", h, dy, out_dtype=BF16, scale=0.5)
    dx, dgain = rms_bwd(name + "_dnorm", x, gain, rstd, dxn, dres=dy)
    return dx, dgain, dwg, dwu, dwd


ffn_block.defvjp(_ffn_block_fwd, _ffn_block_bwd)


@jax.custom_vjp
def mix_in(h, gain, w):
    return _mix_in_fwd(h, gain, w)[0]


def _mix_in_fwd(h, gain, w):
    u, rstd = rms_fwd("mix_norm", h, gain, BF16)
    return mm_nn("mix_inproj", u, w), (h, gain, rstd, u, w)


def _mix_in_bwd(res, dproj):
    h, gain, rstd, u, w = res
    du = mm_nt("mix_du", dproj, w)
    dw = mm_tn("mix_dwin", u, dproj, out_dtype=BF16)
    dh, dgain = rms_bwd("mix_dnorm", h, gain, rstd, du)
    return dh, dgain, dw


mix_in.defvjp(_mix_in_fwd, _mix_in_bwd)


@jax.custom_vjp
def mix_out(h, o_fox, o_swa, g_fox, g_swa, w):
    return _mix_out_fwd(h, o_fox, o_swa, g_fox, g_swa, w)[0]


def _mix_out_fwd(h, o_fox, o_swa, g_fox, g_swa, w):
    nf, r_fox = rms_fwd("out_norm_fox", o_fox, g_fox, BF16)
    nsw, r_swa = rms_fwd("out_norm_swa", o_swa, g_swa, BF16)
    o = jnp.concatenate([nf, nsw], axis=-1)
    return mm_nn("out_proj", o, w, resid=h), (o_fox, o_swa, g_fox, g_swa, r_fox, r_swa, o, w)


def _mix_out_bwd(res, dh2):
    o_fox, o_swa, g_fox, g_swa, r_fox, r_swa, o, w = res
    do = mm_nt("out_do", dh2, w)
    dw = mm_tn("out_dw", o, dh2, out_dtype=BF16)
    cf = o_fox.shape[1]
    d_fox, dg_fox = rms_bwd("out_dnorm_fox", o_fox, g_fox, r_fox, do[:, :cf])
    d_swa, dg_swa = rms_bwd("out_dnorm_swa", o_swa, g_swa, r_swa, do[:, cf:])
    return dh2, d_fox, d_swa, dg_fox, dg_swa, dw


mix_out.defvjp(_mix_out_fwd, _mix_out_bwd)


FOX_TQ = 256


def _fox_scores(q_ref, k_ref, cq_ref, ck_ref, i, tq):
    s_len = k_ref.shape[0]
    s = lax.dot_general(q_ref[...].astype(BF16), k_ref[...].astype(BF16), _DIMS["nt"], preferred_element_type=F32)
    s = s * (HEAD_DIM ** -0.5) + cq_ref[...] - ck_ref[...]
    row = lax.broadcasted_iota(jnp.int32, (tq, s_len), 0) + i * tq
    col = lax.broadcasted_iota(jnp.int32, (tq, s_len), 1)
    return jnp.where(row >= col, s, MASK_VALUE)


def fox_fwd(q, k, v, cq, ck):
    h, s_len, d = q.shape
    tq = min(FOX_TQ, s_len)

    def body(q_ref, k_ref, v_ref, cq_ref, ck_ref, o_ref, lse_ref):
        s = _fox_scores(q_ref, k_ref, cq_ref, ck_ref, pl.program_id(1), tq)
        m = jnp.max(s, axis=-1, keepdims=True)
        p = jnp.exp(s - m)
        l = jnp.sum(p, axis=-1, keepdims=True)
        o = jnp.dot(p.astype(BF16), v_ref[...].astype(BF16), preferred_element_type=F32)
        o_ref[...] = o / l
        lse_ref[...] = m + jnp.log(l)

    qb = pl.BlockSpec((None, tq, d), lambda hh, i: (hh, i, 0))
    kb = pl.BlockSpec((None, s_len, d), lambda hh, i: (hh, 0, 0))
    colb = pl.BlockSpec((None, tq, 1), lambda hh, i: (hh, i, 0))
    return pl.pallas_call(
        body, name="fox_fwd", grid=(h, s_len // tq),
        in_specs=[qb, kb, kb, colb, pl.BlockSpec((None, 1, s_len), lambda hh, i: (hh, 0, 0))],
        out_specs=[qb, colb],
        out_shape=[jax.ShapeDtypeStruct((h, s_len, d), F32), jax.ShapeDtypeStruct((h, s_len, 1), F32)],
        compiler_params=_cparams(("parallel", "parallel")),
    )(q, k, v, cq, ck)


def fox_bwd(q, k, v, cq, ck, o, lse, do):
    h, s_len, d = q.shape
    tq = min(FOX_TQ, s_len)
    scale = HEAD_DIM ** -0.5

    def body(q_ref, k_ref, v_ref, cq_ref, ck_ref, o_ref, lse_ref, do_ref, dq_ref, dk_ref, dv_ref, dcq_ref, dck_ref):
        i = pl.program_id(1)

        @pl.when(i == 0)
        def _():
            dk_ref[...] = jnp.zeros_like(dk_ref)
            dv_ref[...] = jnp.zeros_like(dv_ref)
            dck_ref[...] = jnp.zeros_like(dck_ref)

        s = _fox_scores(q_ref, k_ref, cq_ref, ck_ref, i, tq)
        p = jnp.exp(s - lse_ref[...])
        dof = do_ref[...]
        dob = dof.astype(BF16)
        delta = jnp.sum(dof * o_ref[...], axis=-1, keepdims=True)
        dp = lax.dot_general(dob, v_ref[...].astype(BF16), _DIMS["nt"], preferred_element_type=F32)
        ds = p * (dp - delta)
        dsb = ds.astype(BF16)
        dv_ref[...] += lax.dot_general(p.astype(BF16), dob, _DIMS["tn"], preferred_element_type=F32)
        dq_ref[...] = jnp.dot(dsb, k_ref[...].astype(BF16), preferred_element_type=F32) * scale
        dk_ref[...] += lax.dot_general(dsb, q_ref[...].astype(BF16), _DIMS["tn"], preferred_element_type=F32) * scale
        dcq_ref[...] = jnp.sum(ds, axis=-1, keepdims=True)
        dck_ref[...] -= jnp.sum(ds, axis=0, keepdims=True)

    qb = pl.BlockSpec((None, tq, d), lambda hh, i: (hh, i, 0))
    kb = pl.BlockSpec((None, s_len, d), lambda hh, i: (hh, 0, 0))
    colb = pl.BlockSpec((None, tq, 1), lambda hh, i: (hh, i, 0))
    rowb = pl.BlockSpec((None, 1, s_len), lambda hh, i: (hh, 0, 0))
    return pl.pallas_call(
        body, name="fox_bwd", grid=(h, s_len // tq),
        in_specs=[qb, kb, kb, colb, rowb, qb, colb, qb],
        out_specs=[qb, kb, kb, colb, rowb],
        out_shape=[jax.ShapeDtypeStruct((h, s_len, d), F32)] * 3
        + [jax.ShapeDtypeStruct((h, s_len, 1), F32), jax.ShapeDtypeStruct((h, 1, s_len), F32)],
        compiler_params=_cparams(("parallel", "arbitrary")),
    )(q, k, v, cq, ck, o, lse, do)


@jax.custom_vjp
def fox_attention(q, k, v, cq, ck):
    return fox_fwd(q, k, v, cq, ck)[0]


def _fox_attention_fwd(q, k, v, cq, ck):
    o, lse = fox_fwd(q, k, v, cq, ck)
    return o, (q, k, v, cq, ck, o, lse)


def _fox_attention_bwd(res, do):
    return fox_bwd(*res, do)


fox_attention.defvjp(_fox_attention_fwd, _fox_attention_bwd)


def _swa_probs(q_ref, kp_ref, kc_ref, sink_ref, n):
    g, w, d = q_ref.shape
    q = q_ref[...].reshape(g * w, d).astype(BF16)
    kw = jnp.concatenate([kp_ref[...], kc_ref[...]], axis=0).astype(BF16)
    s = lax.dot_general(q, kw, _DIMS["nt"], preferred_element_type=F32) * (HEAD_DIM ** -0.5)
    t = lax.broadcasted_iota(jnp.int32, (g * w, 2 * w), 0) & (w - 1)
    col = lax.broadcasted_iota(jnp.int32, (g * w, 2 * w), 1)
    rel = t + w - col
    valid = (rel >= 0) & (rel < w) & ((col >= w) | (n > 0))
    s = jnp.where(valid, s, MASK_VALUE)
    sink = sink_ref[...]
    m = jnp.maximum(jnp.max(s, axis=-1, keepdims=True), sink)
    p = jnp.exp(s - m)
    ps = jnp.exp(sink - m)
    linv = 1.0 / (jnp.sum(p, axis=-1, keepdims=True) + ps)
    return q, kw, p * linv, ps * linv


def swa_fwd(q, k, v, sink):
    hk, g, s_len, d = q.shape
    w = WINDOW
    assert w & (w - 1) == 0 and s_len % w == 0

    def body(q_ref, kp_ref, kc_ref, vp_ref, vc_ref, sink_ref, o_ref):
        _, _, p, _ = _swa_probs(q_ref, kp_ref, kc_ref, sink_ref, pl.program_id(1))
        vw = jnp.concatenate([vp_ref[...], vc_ref[...]], axis=0).astype(BF16)
        o_ref[...] = jnp.dot(p.astype(BF16), vw, preferred_element_type=F32).reshape(g, w, d)

    qb = pl.BlockSpec((None, g, w, d), lambda hh, n: (hh, 0, n, 0))
    prev = pl.BlockSpec((None, w, d), lambda hh, n: (hh, jnp.maximum(n - 1, 0), 0))
    cur = pl.BlockSpec((None, w, d), lambda hh, n: (hh, n, 0))
    sb = pl.BlockSpec((None, g * w, 1), lambda hh, n: (hh, 0, 0))
    return pl.pallas_call(
        body, name="swa_fwd", grid=(hk, s_len // w),
        in_specs=[qb, prev, cur, prev, cur, sb], out_specs=qb,
        out_shape=jax.ShapeDtypeStruct((hk, g, s_len, d), F32),
        compiler_params=_cparams(("parallel", "parallel")),
    )(q, k, k, v, v, sink)


def swa_bwd(q, k, v, sink, o, do):
    hk, g, s_len, d = q.shape
    w = WINDOW
    scale = HEAD_DIM ** -0.5

    def body(q_ref, kp_ref, kc_ref, vp_ref, vc_ref, sink_ref, o_ref, do_ref, dq_ref, dkp_ref, dkc_ref, dvp_ref, dvc_ref, dsink_ref):
        n = pl.program_id(1)

        @pl.when(n == 0)
        def _():
            dsink_ref[...] = jnp.zeros_like(dsink_ref)

        q, kw, p, ps = _swa_probs(q_ref, kp_ref, kc_ref, sink_ref, n)
        vw = jnp.concatenate([vp_ref[...], vc_ref[...]], axis=0).astype(BF16)
        dof = do_ref[...].reshape(g * w, d)
        dob = dof.astype(BF16)
        delta = jnp.sum(dof * o_ref[...].reshape(g * w, d), axis=-1, keepdims=True)
        dp = lax.dot_general(dob, vw, _DIMS["nt"], preferred_element_type=F32)
        ds = p * (dp - delta)
        dsb = ds.astype(BF16)
        dsink_ref[...] -= ps * delta
        dq_ref[...] = (jnp.dot(dsb, kw, preferred_element_type=F32) * scale).reshape(g, w, d)
        dkw = lax.dot_general(dsb, q, _DIMS["tn"], preferred_element_type=F32) * scale
        dvw = lax.dot_general(p.astype(BF16), dob, _DIMS["tn"], preferred_element_type=F32)
        dkp_ref[...] = dkw[:w]
        dkc_ref[...] = dkw[w:]
        dvp_ref[...] = dvw[:w]
        dvc_ref[...] = dvw[w:]

    qb = pl.BlockSpec((None, g, w, d), lambda hh, n: (hh, 0, n, 0))
    prev = pl.BlockSpec((None, w, d), lambda hh, n: (hh, jnp.maximum(n - 1, 0), 0))
    cur = pl.BlockSpec((None, w, d), lambda hh, n: (hh, n, 0))
    sb = pl.BlockSpec((None, g * w, 1), lambda hh, n: (hh, 0, 0))
    kv_shape = jax.ShapeDtypeStruct((hk, s_len, d), F32)
    return pl.pallas_call(
        body, name="swa_bwd", grid=(hk, s_len // w),
        in_specs=[qb, prev, cur, prev, cur, sb, qb, qb],
        out_specs=[qb, cur, cur, cur, cur, sb],
        out_shape=[jax.ShapeDtypeStruct((hk, g, s_len, d), F32), kv_shape, kv_shape, kv_shape, kv_shape,
                   jax.ShapeDtypeStruct((hk, g * w, 1), F32)],
        compiler_params=_cparams(("parallel", "arbitrary")),
    )(q, k, k, v, v, sink, o, do)


@jax.custom_vjp
def swa_attention(q, k, v, sink):
    return swa_fwd(q, k, v, sink)


def _swa_attention_fwd(q, k, v, sink):
    o = swa_fwd(q, k, v, sink)
    return o, (q, k, v, sink, o)


def _shift_up(a, w):
    return jnp.concatenate([a[:, w:], jnp.zeros_like(a[:, :w])], axis=1)


def _swa_attention_bwd(res, do):
    q, k, v, sink, o = res
    dq, dkp, dkc, dvp, dvc, dsink = swa_bwd(q, k, v, sink, o, do)
    return dq, dkc + _shift_up(dkp, WINDOW), dvc + _shift_up(dvp, WINDOW), dsink


swa_attention.defvjp(_swa_attention_fwd, _swa_attention_bwd)


def loss_call(y, target):
    s, d = y.shape
    tm = _pick(s, (512, 256, 128))

    def body(y_ref, t_ref, l_ref, dy_ref):
        e = y_ref[...] - t_ref[...]
        dy_ref[...] = e * (1.0 / d)

        @pl.when(pl.program_id(0) == 0)
        def _():
            l_ref[...] = jnp.zeros_like(l_ref)

        l_ref[...] += jnp.sum(jnp.sum(e * e, axis=0, keepdims=True), axis=1, keepdims=True) * (0.5 / d)

    row = pl.BlockSpec((tm, d), lambda i: (i, 0))
    l, dy = pl.pallas_call(
        body, name="loss_head", grid=(s // tm,), in_specs=[row, row],
        out_specs=[pl.BlockSpec((1, 1), lambda i: (0, 0)), row],
        out_shape=[jax.ShapeDtypeStruct((1, 1), F32), jax.ShapeDtypeStruct((s, d), F32)],
        compiler_params=_cparams(("arbitrary",)),
    )(y, target)
    return l[0, 0], dy


@jax.custom_vjp
def loss_head(y, target):
    return loss_call(y, target)[0]


def _loss_head_fwd(y, target):
    l, dy = loss_call(y, target)
    return l, dy


def _loss_head_bwd(dy, dl):
    return dy * dl, jnp.zeros_like(dy)


loss_head.defvjp(_loss_head_fwd, _loss_head_bwd)


def _rope(x, cos, sin):
    x1, x2 = x[..., : HEAD_DIM // 2], x[..., HEAD_DIM // 2:]
    return jnp.concatenate([x1 * cos - x2 * sin, x2 * cos + x1 * sin], axis=-1)


def _heads(a, nh):
    return a.reshape(a.shape[0], nh, HEAD_DIM).transpose(1, 0, 2)


def _unheads(a):
    return a.transpose(1, 0, 2).reshape(a.shape[1], a.shape[0] * HEAD_DIM)


def _win_layout(d_model):
    hf = hq = d_model // (2 * HEAD_DIM)
    hk = hq // 4
    sizes = [hf * HEAD_DIM, hf * HEAD_DIM, hf * HEAD_DIM, hf, hq * HEAD_DIM, hk * HEAD_DIM, hk * HEAD_DIM]
    return hf, hq, hk, sizes


def _pad_cols(a, n):
    return a if a.shape[1] == n else jnp.pad(a, ((0, 0), (0, n - a.shape[1])))


def _relayout_win(win_g):
    ns, d_model, cs = win_g.shape
    hf, hq, hk, sizes = _win_layout(d_model)
    full = win_g.transpose(1, 0, 2).reshape(d_model, ns * cs)
    offs = [0]
    for sz in sizes:
        offs.append(offs[-1] + sz)
    seg = [full[:, offs[i]:offs[i + 1]] for i in range(7)]
    qf, kf, vf, fl, qs, ks, vs = seg
    body = jnp.concatenate([qf, kf, vf, qs, ks, vs, _pad_cols(fl, 128)], axis=1)
    return _pad_cols(body, -(-body.shape[1] // 512) * 512)


def _forward_loss(big, small, x, positions, target):
    s_len, d_model = x.shape
    hf, hq, hk, sizes = _win_layout(d_model)
    grp = hq // hk
    sm = {k: v[0] for k, v in small.items()}

    h1 = ffn_block("ffn1", x, sm["norm_ffn1_g"], big["ffn1_w_gate"], big["ffn1_w_up"], big["ffn1_w_down"])

    proj = mix_in(h1, sm["norm_mix_g"], _relayout_win(big["w_in"]))
    o0 = 0
    q_f = proj[:, o0:o0 + sizes[0]]; o0 += sizes[0]
    k_f = proj[:, o0:o0 + sizes[1]]; o0 += sizes[1]
    v_f = proj[:, o0:o0 + sizes[2]]; o0 += sizes[2]
    q_s = proj[:, o0:o0 + sizes[4]]; o0 += sizes[4]
    k_s = proj[:, o0:o0 + sizes[5]]; o0 += sizes[5]
    v_s = proj[:, o0:o0 + sizes[6]]; o0 += sizes[6]
    f_logit = proj[:, o0:o0 + hf]

    q_f = rms_op("fox_qnorm", _heads(q_f, hf).reshape(hf * s_len, HEAD_DIM), sm["fox_q_norm_g"]).reshape(hf, s_len, HEAD_DIM)
    k_f = rms_op("fox_knorm", _heads(k_f, hf).reshape(hf * s_len, HEAD_DIM), sm["fox_k_norm_g"]).reshape(hf, s_len, HEAD_DIM)
    v_f = _heads(v_f, hf)
    log_f = jax.nn.log_sigmoid(f_logit + sm["b_forget"])
    c = jnp.cumsum(log_f, axis=0).T
    o_fox = _unheads(fox_attention(q_f, k_f, v_f, c[:, :, None], c[:, None, :]))

    inv_freq = ROPE_THETA ** (-jnp.arange(0, HEAD_DIM, 2, dtype=F32) / HEAD_DIM)
    ang = positions.astype(F32)[:, None] * inv_freq
    cos, sin = jnp.cos(ang), jnp.sin(ang)
    q_s = rms_op("swa_qnorm", _heads(q_s, hq).reshape(hq * s_len, HEAD_DIM), sm["swa_q_norm_g"]).reshape(hq, s_len, HEAD_DIM)
    k_s = rms_op("swa_knorm", _heads(k_s, hk).reshape(hk * s_len, HEAD_DIM), sm["swa_k_norm_g"]).reshape(hk, s_len, HEAD_DIM)
    q_s = _rope(q_s, cos, sin).reshape(hk, grp, s_len, HEAD_DIM)
    k_s = _rope(k_s, cos, sin)
    v_s = _heads(v_s, hk)
    sink = jnp.broadcast_to(sm["swa_sinks"].reshape(hk, grp, 1, 1), (hk, grp, WINDOW, 1)).reshape(hk, grp * WINDOW, 1)
    o_swa = _unheads(swa_attention(q_s, k_s, v_s, sink).reshape(hq, s_len, HEAD_DIM))

    h2 = mix_out(h1, o_fox, o_swa, sm["out_norm_fox_g"], sm["out_norm_swa_g"], big["w_out"])
    y = ffn_block("ffn2", h2, sm["norm_ffn2_g"], big["ffn2_w_gate"], big["ffn2_w_up"], big["ffn2_w_down"])
    return loss_head(y, target)


_HBM = pl.BlockSpec(memory_space=pl.ANY)


def _me():
    return lax.axis_index("x"), lax.axis_index("y"), lax.axis_index("c")


def _other_chips(x, y):
    return [(1 - x, y), (x, 1 - y), (1 - x, 1 - y)]


def _half(ref, h):
    rows = ref.shape[-2] // 2
    idx = (slice(None),) * (len(ref.shape) - 2) + (pl.ds(h * rows, rows), slice(None))
    return ref.at[idx]


def cast_place(name, w, p_idx):
    r, cols = w.shape
    tr = _row_tile(r, cols, 4)

    def body(p_ref, w_ref, o_ref):
        o_ref[...] = w_ref[...].astype(BF16)

    return pl.pallas_call(
        body, name=name,
        grid_spec=pltpu.PrefetchScalarGridSpec(
            num_scalar_prefetch=1, grid=(r // tr,),
            in_specs=[pl.BlockSpec((tr, cols), lambda i, pr: (i, 0))],
            out_specs=pl.BlockSpec((None, tr, cols), lambda i, pr: (pr[0], i, 0))),
        out_shape=jax.ShapeDtypeStruct((N_CHIPS, r, cols), BF16),
        compiler_params=_cparams(("parallel",)),
    )(p_idx, w)


def all_gather_weights(placed):
    n = len(placed)

    def body(*refs):
        outs = refs[n:2 * n]
        send_sems, recv_sems = refs[2 * n:]
        x, y, c = _me()
        p = 2 * x + y
        sibling = (x, y, 1 - c)
        chips = _other_chips(x, y)

        def remote(w, k, src, dst, to):
            return pltpu.make_async_remote_copy(src_ref=src, dst_ref=dst, send_sem=send_sems.at[w, k], recv_sem=recv_sems.at[w, k],
                                                device_id=to, device_id_type=_MESH)

        first = []
        for w in range(n):
            mine = _half(outs[w].at[p], c)
            for j, chip in enumerate(chips):
                first.append(remote(w, j, mine, mine, (*chip, c)))
                first[-1].start()
        passed = []
        for j, chip in enumerate(chips):
            q = 2 * chip[0] + chip[1]
            for w in range(n):
                landed = _half(outs[w].at[q], c)
                remote(w, j, landed, landed, (*chip, c)).wait_recv()
                passed.append(remote(w, 3 + j, landed, landed, sibling))
                passed[-1].start()
        for j, chip in enumerate(chips):
            q = 2 * chip[0] + chip[1]
            for w in range(n):
                theirs = _half(outs[w].at[q], 1 - c)
                remote(w, 3 + j, theirs, theirs, sibling).wait_recv()
        for cp in first + passed:
            cp.wait_send()

    return pl.pallas_call(
        body, name="all_gather_weights",
        in_specs=[_HBM] * n, out_specs=[_HBM] * n,
        out_shape=[jax.ShapeDtypeStruct(s.shape, s.dtype) for s in placed],
        input_output_aliases={w: w for w in range(n)},
        scratch_shapes=[pltpu.SemaphoreType.DMA((n, 6)), pltpu.SemaphoreType.DMA((n, 6))],
    )(*placed)


def exchange_halves(grads):
    n = len(grads)

    def body(*refs):
        ins, outs = refs[:n], refs[n:2 * n]
        send_sems, recv_sems = refs[2 * n:]
        x, y, c = _me()
        cps = []
        for w in range(n):
            cps.append(pltpu.make_async_remote_copy(
                src_ref=_half(ins[w], 1 - c), dst_ref=outs[w], send_sem=send_sems.at[w], recv_sem=recv_sems.at[w],
                device_id=(x, y, 1 - c), device_id_type=_MESH))
            cps[-1].start()
        for cp in cps:
            cp.wait()

    return pl.pallas_call(
        body, name="grad_exchange_halves",
        in_specs=[_HBM] * n, out_specs=[_HBM] * n,
        out_shape=[jax.ShapeDtypeStruct((g.shape[0], g.shape[1] // 2, g.shape[2]), g.dtype) for g in grads],
        scratch_shapes=[pltpu.SemaphoreType.DMA((n,)), pltpu.SemaphoreType.DMA((n,))],
    )(*grads)


def _row_tile(rows, cols, itemsize):
    target = max(16, (1 << 20) // (cols * itemsize))
    for t in (1024, 512, 256, 128, 64, 32, 16):
        if t <= target and rows % t == 0:
            return t
    return rows


def chip_sum(name, grad, theirs, c_idx):
    ns, r, cols = grad.shape
    rh = r // 2
    tr = _row_tile(rh, cols, 2)
    nb = rh // tr

    def body(c_ref, a_ref, b_ref, o_ref):
        o_ref[...] = (a_ref[...].astype(F32) + b_ref[...].astype(F32)).astype(o_ref.dtype)

    return pl.pallas_call(
        body, name=name,
        grid_spec=pltpu.PrefetchScalarGridSpec(
            num_scalar_prefetch=1, grid=(ns, nb),
            in_specs=[pl.BlockSpec((None, tr, cols), lambda q, i, cr: (q, cr[0] * nb + i, 0)),
                      pl.BlockSpec((None, tr, cols), lambda q, i, cr: (q, i, 0))],
            out_specs=pl.BlockSpec((None, tr, cols), lambda q, i, cr: (q, i, 0))),
        out_shape=jax.ShapeDtypeStruct((ns, rh, cols), BF16),
        compiler_params=_cparams(("parallel", "parallel")),
    )(c_idx, grad, theirs)


def scatter_to_owner(sums):
    n = len(sums)

    def body(*refs):
        ins, outs = refs[:n], refs[n:2 * n]
        send_sems, recv_sems = refs[2 * n:]
        x, y, c = _me()
        cps = []
        for w in range(n):
            for j, chip in enumerate(_other_chips(x, y)):
                q = 2 * chip[0] + chip[1]
                cps.append(pltpu.make_async_remote_copy(
                    src_ref=ins[w].at[q], dst_ref=outs[w].at[j], send_sem=send_sems.at[w, j], recv_sem=recv_sems.at[w, j],
                    device_id=(*chip, c), device_id_type=_MESH))
                cps[-1].start()
        for cp in cps:
            cp.wait()

    return pl.pallas_call(
        body, name="grad_scatter_to_owner",
        in_specs=[_HBM] * n, out_specs=[_HBM] * n,
        out_shape=[jax.ShapeDtypeStruct((3,) + s.shape[1:], s.dtype) for s in sums],
        scratch_shapes=[pltpu.SemaphoreType.DMA((n, 3)), pltpu.SemaphoreType.DMA((n, 3))],
    )(*sums)


def owner_sum(name, sums, got, pc_idx):
    ns, rh, cols = sums.shape
    tr = _row_tile(rh, cols, 4)
    nb = rh // tr

    def body(pc_ref, a_ref, b_ref, o_ref):
        o_ref[...] = ((a_ref[...].astype(F32) + b_ref[0].astype(F32)) + b_ref[1].astype(F32)) + b_ref[2].astype(F32)

    return pl.pallas_call(
        body, name=name,
        grid_spec=pltpu.PrefetchScalarGridSpec(
            num_scalar_prefetch=1, grid=(nb,),
            in_specs=[pl.BlockSpec((None, tr, cols), lambda i, pc: (pc[0], i, 0)),
                      pl.BlockSpec((3, tr, cols), lambda i, pc: (0, i, 0))],
            out_specs=pl.BlockSpec((tr, cols), lambda i, pc: (pc[1] * nb + i, 0))),
        out_shape=jax.ShapeDtypeStruct((2 * rh, cols), F32),
        compiler_params=_cparams(("parallel",)),
    )(pc_idx, sums, got)


def join_halves(fulls):
    n = len(fulls)

    def body(*refs):
        outs = refs[n:2 * n]
        send_sems, recv_sems = refs[2 * n:]
        x, y, c = _me()
        cps = []
        for w in range(n):
            mine = _half(outs[w], c)
            cps.append(pltpu.make_async_remote_copy(
                src_ref=mine, dst_ref=mine, send_sem=send_sems.at[w], recv_sem=recv_sems.at[w],
                device_id=(x, y, 1 - c), device_id_type=_MESH))
            cps[-1].start()
        for cp in cps:
            cp.wait()

    return pl.pallas_call(
        body, name="grad_join_halves",
        in_specs=[_HBM] * n, out_specs=[_HBM] * n,
        out_shape=[jax.ShapeDtypeStruct(h.shape, h.dtype) for h in fulls],
        input_output_aliases={w: w for w in range(n)},
        scratch_shapes=[pltpu.SemaphoreType.DMA((n,)), pltpu.SemaphoreType.DMA((n,))],
    )(*fulls)


def all_reduce_small(v):
    rows, lanes = v.shape

    def body(v_ref, o_ref, slots, send_sems, recv_sems):
        x, y, c = _me()
        me = 4 * x + 2 * y + c
        slots[me] = v_ref[...]
        cps = []
        for k in range(1, N_DEV):
            peer = (x ^ (k >> 2), y ^ ((k >> 1) & 1), c ^ (k & 1))
            cps.append(pltpu.make_async_remote_copy(
                src_ref=v_ref, dst_ref=slots.at[me], send_sem=send_sems.at[k - 1], recv_sem=recv_sems.at[k - 1],
                device_id=peer, device_id_type=_MESH))
            cps[-1].start()
        for k in range(1, N_DEV):
            theirs = slots.at[me ^ k]
            pltpu.make_async_remote_copy(
                src_ref=theirs, dst_ref=theirs, send_sem=send_sems.at[k - 1], recv_sem=recv_sems.at[k - 1],
                device_id=(x, y, c), device_id_type=_MESH).wait_recv()
        for cp in cps:
            cp.wait_send()
        acc = slots[0]
        for i in range(1, N_DEV):
            acc = acc + slots[i]
        o_ref[...] = acc

    return pl.pallas_call(
        body, name="all_reduce_small",
        in_specs=[pl.BlockSpec(memory_space=pltpu.VMEM)], out_specs=pl.BlockSpec(memory_space=pltpu.VMEM),
        out_shape=jax.ShapeDtypeStruct((rows, lanes), F32),
        scratch_shapes=[pltpu.VMEM((N_DEV, rows, lanes), F32), pltpu.SemaphoreType.DMA((N_DEV - 1,)), pltpu.SemaphoreType.DMA((N_DEV - 1,))],
    )(v)


def adamw(name, w, g, m, v):
    r, cols = w.shape
    tr = _row_tile(r, cols, 4)
    c1 = 1.0 / (1.0 - ADAM_B1 ** ADAM_STEP)
    c2 = 1.0 / (1.0 - ADAM_B2 ** ADAM_STEP)

    def body(w_ref, g_ref, m_ref, v_ref, d_ref, nm_ref, nv_ref):
        gv = g_ref[...]
        nm = ADAM_B1 * m_ref[...] + (1.0 - ADAM_B1) * gv
        nv = ADAM_B2 * v_ref[...] + (1.0 - ADAM_B2) * (gv * gv)
        d_ref[...] = -ADAM_LR * ((nm * c1) / (jnp.sqrt(nv * c2) + ADAM_EPS) + ADAM_WD * w_ref[...])
        nm_ref[...] = nm
        nv_ref[...] = nv

    blk = pl.BlockSpec((tr, cols), lambda i: (i, 0))
    return pl.pallas_call(
        body, name=name, grid=(r // tr,), in_specs=[blk] * 4, out_specs=[blk] * 3,
        out_shape=[jax.ShapeDtypeStruct((r, cols), F32)] * 3,
        compiler_params=_cparams(("parallel",)),
    )(w, g, m, v)


_BIG = ("ffn1_w_gate", "ffn1_w_up", "ffn1_w_down", "w_in", "w_out", "ffn2_w_gate", "ffn2_w_up", "ffn2_w_down")
_ROW_SHARDED = ("ffn1_w_down", "w_out", "ffn2_w_down")
_SMALL = ("norm_ffn1_g", "norm_mix_g", "b_forget", "fox_q_norm_g", "fox_k_norm_g", "swa_q_norm_g", "swa_k_norm_g", "swa_sinks",
          "out_norm_fox_g", "out_norm_swa_g", "norm_ffn2_g")
_ALL = ("norm_ffn1_g", "ffn1_w_gate", "ffn1_w_up", "ffn1_w_down", "norm_mix_g", "w_in", "b_forget", "fox_q_norm_g", "fox_k_norm_g",
        "swa_q_norm_g", "swa_k_norm_g", "swa_sinks", "out_norm_fox_g", "out_norm_swa_g", "w_out", "norm_ffn2_g", "ffn2_w_gate",
        "ffn2_w_up", "ffn2_w_down")
_LANES = 128


def _pack_small(d):
    parts = []
    for k in _SMALL:
        v = d[k].reshape(-1)
        rows = -(-v.shape[0] // _LANES)
        parts.append(jnp.pad(v, (0, rows * _LANES - v.shape[0])).reshape(rows, _LANES))
    a = jnp.concatenate(parts, axis=0)
    return jnp.pad(a, ((0, -a.shape[0] % 8), (0, 0)))


def _unpack_small(a, like):
    out, r0 = {}, 0
    for k in _SMALL:
        nvals = like[k].shape[1]
        rows = -(-nvals // _LANES)
        out[k] = a[r0:r0 + rows].reshape(-1)[:nvals].reshape(1, nvals)
        r0 += rows
    return out


def kernel(x, positions, norm_ffn1_g, ffn1_w_gate, ffn1_w_up, ffn1_w_down, norm_mix_g, w_in, b_forget, fox_q_norm_g, fox_k_norm_g, swa_q_norm_g, swa_k_norm_g, swa_sinks, out_norm_fox_g, out_norm_swa_g, w_out, norm_ffn2_g, ffn2_w_gate, ffn2_w_up, ffn2_w_down, loss_target, m_norm_ffn1_g, m_ffn1_w_gate, m_ffn1_w_up, m_ffn1_w_down, m_norm_mix_g, m_w_in, m_b_forget, m_fox_q_norm_g, m_fox_k_norm_g, m_swa_q_norm_g, m_swa_k_norm_g, m_swa_sinks, m_out_norm_fox_g, m_out_norm_swa_g, m_w_out, m_norm_ffn2_g, m_ffn2_w_gate, m_ffn2_w_up, m_ffn2_w_down, v_norm_ffn1_g, v_ffn1_w_gate, v_ffn1_w_up, v_ffn1_w_down, v_norm_mix_g, v_w_in, v_b_forget, v_fox_q_norm_g, v_fox_k_norm_g, v_swa_q_norm_g, v_swa_k_norm_g, v_swa_sinks, v_out_norm_fox_g, v_out_norm_swa_g, v_w_out, v_norm_ffn2_g, v_ffn2_w_gate, v_ffn2_w_up, v_ffn2_w_down):
    args = dict(locals())
    w = {k: args[k] for k in _ALL}
    m = {k: args["m_" + k] for k in _ALL}
    v = {k: args["v_" + k] for k in _ALL}
    c_idx = lax.axis_index("c").astype(jnp.int32).reshape(1)
    p_idx = (2 * lax.axis_index("x") + lax.axis_index("y")).astype(jnp.int32).reshape(1)

    gathered = all_gather_weights([cast_place("cast_place_" + k, w[k][0], p_idx) for k in _BIG])
    big = dict(zip(_BIG, gathered))
    for k in _ROW_SHARDED:
        big[k] = big[k].reshape(-1, big[k].shape[-1])
    small = {k: w[k] for k in _SMALL}

    loss, (g_big, g_small, grad_x) = jax.value_and_grad(_forward_loss, argnums=(0, 1, 2))(
        big, small, x[0], positions[0], loss_target[0])
    loss = lax.psum(loss, ("x", "y", "c"))

    grads = [g_big[k].reshape((N_CHIPS, -1, g_big[k].shape[-1])) for k in _BIG]
    theirs = exchange_halves(grads)
    sums = [chip_sum("chip_sum_" + k, g, t, c_idx) for k, g, t in zip(_BIG, grads, theirs)]
    got = scatter_to_owner(sums)
    pc_idx = jnp.concatenate([p_idx, c_idx])
    halves = [owner_sum("owner_sum_" + k, s, r, pc_idx) for k, s, r in zip(_BIG, sums, got)]
    g_shard = dict(zip(_BIG, join_halves(halves)))
    g_small_sum = _unpack_small(all_reduce_small(_pack_small(g_small)), small)

    grad_w, delta, new_m, new_v = {}, {}, {}, {}
    for k in _BIG:
        grad_w[k] = g_shard[k][None]
        d, nm, nv = adamw("adamw_" + k, w[k][0], g_shard[k], m[k][0], v[k][0])
        delta[k], new_m[k], new_v[k] = d[None], nm[None], nv[None]
    d, nm, nv = adamw("adamw_small", _pack_small(small), _pack_small(g_small_sum), _pack_small({k: m[k] for k in _SMALL}),
                      _pack_small({k: v[k] for k in _SMALL}))
    grad_w.update(g_small_sum)
    delta.update(_unpack_small(d, small))
    new_m.update(_unpack_small(nm, small))
    new_v.update(_unpack_small(nv, small))

    return (loss, grad_x[None], *[grad_w[k] for k in _ALL], *[delta[k] for k in _ALL], *[new_m[k] for k in _ALL], *[new_v[k] for k in _ALL])
```

```python
import functools

import jax
import jax.numpy as jnp
from jax import lax
from jax.experimental import pallas as pl
from jax.experimental.pallas import tpu as pltpu

F32 = jnp.float32
BF16 = jnp.bfloat16

HEAD_DIM = 64
WINDOW = 128
ROPE_THETA = 10000.0
EPS = 1e-6
N_CHIPS = 4
N_DEV = 8

ADAM_LR = 0.001
ADAM_B1 = 0.9
ADAM_B2 = 0.999
ADAM_EPS = 1e-08
ADAM_WD = 0.01
ADAM_STEP = 10

V7X_VMEM_BYTES = 64 * 1024 * 1024
VMEM_LIMIT = V7X_VMEM_BYTES - 8 * 1024 * 1024
MASK_VALUE = -1e30

_MESH = pl.DeviceIdType.MESH
_HBM = pl.BlockSpec(memory_space=pl.ANY)
_DIMS = {"nn": (((1,), (0,)), ((), ())), "nt": (((1,), (1,)), ((), ())), "tn": (((0,), (0,)), ((), ()))}


def _pick(n, prefs):
    for p in prefs:
        if n % p == 0:
            return p
    return n


class Rider:
    def __init__(self, reads, aliased, news, nsem, build):
        self.reads, self.aliased, self.news, self.nsem, self.build = list(reads), list(aliased), list(news), nsem, build


class _Shifted:
    def __init__(self, ref, off):
        self.ref, self.off = ref, off

    @property
    def at(self):
        return self

    def __getitem__(self, k):
        return self.ref.at[k + self.off]


def combine(*riders):
    def build(reads, al, news, ssem, rsem):
        out = ([], [], [])
        r0 = a0 = n0 = s0 = 0
        for rd in riders:
            nr, na, nn = len(rd.reads), len(rd.aliased), len(rd.news)
            part = rd.build(reads[r0:r0 + nr], al[a0:a0 + na], news[n0:n0 + nn], _Shifted(ssem, s0), _Shifted(rsem, s0))
            for acc, lst in zip(out, part):
                acc.extend(lst)
            r0, a0, n0, s0 = r0 + nr, a0 + na, n0 + nn, s0 + rd.nsem
        return out

    return Rider(sum((r.reads for r in riders), []), sum((r.aliased for r in riders), []), sum((r.news for r in riders), []),
                 sum(r.nsem for r in riders), build)


def _me():
    return lax.axis_index("x"), lax.axis_index("y"), lax.axis_index("c")


def _other_chips(x, y):
    return [(1 - x, y), (x, 1 - y), (1 - x, 1 - y)]


def _half(ref, h):
    rows = ref.shape[-2] // 2
    idx = (slice(None),) * (len(ref.shape) - 2) + (pl.ds(h * rows, rows), slice(None))
    return ref.at[idx]


def _remote(src, dst, ssem, rsem, k, to):
    return pltpu.make_async_remote_copy(src_ref=src, dst_ref=dst, send_sem=ssem.at[k], recv_sem=rsem.at[k], device_id=to,
                                        device_id_type=_MESH)


def _later(*args):
    return functools.partial(_remote, *args)


def gather_ici(bufs):
    def build(reads, al, news, ssem, rsem):
        x, y, c = _me()
        p = 2 * x + y
        starts, arrivals = [], []
        for w, buf in enumerate(al):
            mine = _half(buf.at[p], c)
            for j, chip in enumerate(_other_chips(x, y)):
                landing = _half(buf.at[2 * chip[0] + chip[1]], c)
                starts.append(_later(mine, mine, ssem, rsem, 3 * w + j, (*chip, c)))
                arrivals.append(_later(landing, landing, ssem, rsem, 3 * w + j, (*chip, c)))
        return starts, arrivals, starts

    return Rider([], bufs, [], 3 * len(bufs), build)


def gather_d2d(bufs):
    def build(reads, al, news, ssem, rsem):
        x, y, c = _me()
        starts, arrivals = [], []
        for w, buf in enumerate(al):
            for j, chip in enumerate(_other_chips(x, y)):
                q = 2 * chip[0] + chip[1]
                landed, landing = _half(buf.at[q], c), _half(buf.at[q], 1 - c)
                starts.append(_later(landed, landed, ssem, rsem, 3 * w + j, (x, y, 1 - c)))
                arrivals.append(_later(landing, landing, ssem, rsem, 3 * w + j, (x, y, 1 - c)))
        return starts, arrivals, starts

    return Rider([], bufs, [], 3 * len(bufs), build)


def exchange_halves(grads):
    def build(reads, al, news, ssem, rsem):
        x, y, c = _me()
        cps = [_later(_half(g, 1 - c), t, ssem, rsem, w, (x, y, 1 - c)) for w, (g, t) in enumerate(zip(reads, news))]
        return cps, cps, cps

    return Rider(grads, [], [jax.ShapeDtypeStruct((g.shape[0], g.shape[1] // 2, g.shape[2]), g.dtype) for g in grads], len(grads), build)


def scatter_to_owner(sums):
    def build(reads, al, news, ssem, rsem):
        x, y, c = _me()
        cps = []
        for w, (s, got) in enumerate(zip(reads, news)):
            for j, chip in enumerate(_other_chips(x, y)):
                cps.append(_later(s.at[2 * chip[0] + chip[1]], got.at[j], ssem, rsem, 3 * w + j, (*chip, c)))
        return cps, cps, cps

    return Rider(sums, [], [jax.ShapeDtypeStruct((3,) + s.shape[1:], s.dtype) for s in sums], 3 * len(sums), build)


def join_halves(fulls):
    def build(reads, al, news, ssem, rsem):
        x, y, c = _me()
        starts, arrivals = [], []
        for w, f in enumerate(al):
            mine, landing = _half(f, c), _half(f, 1 - c)
            starts.append(_later(mine, mine, ssem, rsem, w, (x, y, 1 - c)))
            arrivals.append(_later(landing, landing, ssem, rsem, w, (x, y, 1 - c)))
        return starts, arrivals, starts

    return Rider([], fulls, [], len(fulls), build)


def _start_and_wait(rider, reads, al, news, ssem, rsem, first, last):
    @pl.when(first)
    def _():
        for cp in rider.build(reads, al, news, ssem, rsem)[0]:
            cp().start()

    def finish():
        @pl.when(last)
        def _():
            _, arrivals, sends = rider.build(reads, al, news, ssem, rsem)
            for cp in arrivals:
                cp().wait_recv()
            for cp in sends:
                cp().wait_send()

    return finish


def _call(name, body, grid, in_specs, args, out_specs, out_shape, scratch=(), semantics=None, rider=None):
    n_in, n_out, n_scr = len(args), len(out_shape), len(scratch)
    if rider is None:
        outs = pl.pallas_call(
            body, name=name, grid=grid, in_specs=list(in_specs), out_specs=list(out_specs), out_shape=list(out_shape),
            scratch_shapes=list(scratch),
            compiler_params=pltpu.CompilerParams(dimension_semantics=semantics, vmem_limit_bytes=VMEM_LIMIT),
        )(*args)
        return list(outs), None
    nr, na, nn = len(rider.reads), len(rider.aliased), len(rider.news)

    def wrapped(*refs):
        ins, reads = refs[:n_in], refs[n_in:n_in + nr]
        o0 = n_in + nr + na
        outs, al, news = refs[o0:o0 + n_out], refs[o0 + n_out:o0 + n_out + na], refs[o0 + n_out + na:o0 + n_out + na + nn]
        s0 = o0 + n_out + na + nn
        scr, (ssem, rsem) = refs[s0:s0 + n_scr], refs[s0 + n_scr:]
        first = functools.reduce(jnp.logical_and, [pl.program_id(a) == 0 for a in range(len(grid))])
        last = functools.reduce(jnp.logical_and, [pl.program_id(a) == g - 1 for a, g in enumerate(grid)])
        finish = _start_and_wait(rider, reads, al, news, ssem, rsem, first, last)
        body(*ins, *outs, *scr)
        finish()

    outs = pl.pallas_call(
        wrapped, name=name, grid=grid,
        in_specs=list(in_specs) + [_HBM] * (nr + na), out_specs=list(out_specs) + [_HBM] * (na + nn),
        out_shape=list(out_shape) + [jax.ShapeDtypeStruct(a.shape, a.dtype) for a in rider.aliased] + rider.news,
        input_output_aliases={n_in + nr + i: n_out + i for i in range(na)},
        scratch_shapes=list(scratch) + [pltpu.SemaphoreType.DMA((rider.nsem,)), pltpu.SemaphoreType.DMA((rider.nsem,))],
        compiler_params=pltpu.CompilerParams(dimension_semantics=("arbitrary",) * len(grid), vmem_limit_bytes=VMEM_LIMIT),
    )(*args, *rider.reads, *rider.aliased)
    return list(outs[:n_out]), (list(outs[n_out:n_out + na]), list(outs[n_out + na:]))


def run_step(name, rider):
    nr, na, nn = len(rider.reads), len(rider.aliased), len(rider.news)

    def body(*refs):
        reads = refs[:nr]
        al, news = refs[nr + na:nr + 2 * na], refs[nr + 2 * na:nr + 2 * na + nn]
        ssem, rsem = refs[nr + 2 * na + nn:]
        starts, arrivals, sends = rider.build(reads, al, news, ssem, rsem)
        for cp in starts:
            cp().start()
        for cp in arrivals:
            cp().wait_recv()
        for cp in sends:
            cp().wait_send()

    outs = pl.pallas_call(
        body, name=name, in_specs=[_HBM] * (nr + na), out_specs=[_HBM] * (na + nn),
        out_shape=[jax.ShapeDtypeStruct(a.shape, a.dtype) for a in rider.aliased] + rider.news,
        input_output_aliases={nr + i: i for i in range(na)},
        scratch_shapes=[pltpu.SemaphoreType.DMA((rider.nsem,)), pltpu.SemaphoreType.DMA((rider.nsem,))],
    )(*rider.reads, *rider.aliased)
    return list(outs[:na]), list(outs[na:])


def all_reduce_small(v):
    rows, lanes = v.shape

    def body(v_ref, o_ref, slots, send_sems, recv_sems):
        x, y, c = _me()
        me = 4 * x + 2 * y + c
        slots[me] = v_ref[...]
        cps = []
        for k in range(1, N_DEV):
            peer = (x ^ (k >> 2), y ^ ((k >> 1) & 1), c ^ (k & 1))
            cps.append(_remote(v_ref, slots.at[me], send_sems, recv_sems, k - 1, peer))
            cps[-1].start()
        for k in range(1, N_DEV):
            theirs = slots.at[me ^ k]
            _remote(theirs, theirs, send_sems, recv_sems, k - 1, (x, y, c)).wait_recv()
        for cp in cps:
            cp.wait_send()
        acc = slots[0]
        for i in range(1, N_DEV):
            acc = acc + slots[i]
        o_ref[...] = acc

    return pl.pallas_call(
        body, name="all_reduce_small",
        in_specs=[pl.BlockSpec(memory_space=pltpu.VMEM)], out_specs=pl.BlockSpec(memory_space=pltpu.VMEM),
        out_shape=jax.ShapeDtypeStruct((rows, lanes), F32),
        scratch_shapes=[pltpu.VMEM((N_DEV, rows, lanes), F32), pltpu.SemaphoreType.DMA((N_DEV - 1,)), pltpu.SemaphoreType.DMA((N_DEV - 1,))],
    )(v)


def _mm_call(name, mode, a, b, a_spec, b_spec, out_shape, out_spec, grid, acc_shape, scale=1.0, resid=None, resid_spec=None, rider=None):
    nk = grid[2]
    dims = _DIMS[mode]
    has_resid = resid is not None

    def body(*refs):
        if has_resid:
            a_ref, b_ref, r_ref, o_ref, acc_ref = refs
        else:
            a_ref, b_ref, o_ref, acc_ref = refs
        k = pl.program_id(2)

        @pl.when(k == 0)
        def _():
            acc_ref[...] = jnp.zeros_like(acc_ref)

        acc_ref[...] += lax.dot_general(a_ref[...].astype(BF16), b_ref[...].astype(BF16), dims, preferred_element_type=F32)

        @pl.when(k == nk - 1)
        def _():
            r = acc_ref[...]
            if scale != 1.0:
                r = r * scale
            if has_resid:
                r = r_ref[...].astype(F32) + r
            o_ref[...] = r.astype(o_ref.dtype)

    in_specs = [a_spec, b_spec] + ([resid_spec] if has_resid else [])
    args = (a, b) + ((resid,) if has_resid else ())
    (out,), rid = _call(name, body, grid, in_specs, args, [out_spec], [out_shape], [pltpu.VMEM(acc_shape, F32)],
                        ("parallel", "parallel", "arbitrary"), rider)
    return out, rid


def _tiles(m, n, kd):
    return _pick(m, (512, 256, 128)), _pick(n, (1024, 512, 256, 128)), _pick(kd, (1024, 512, 256, 128))


def mm_nn(name, a, b, *, out_dtype=F32, scale=1.0, resid=None, rider=None):
    m, kd = a.shape
    n = b.shape[1]
    tm, tn, tk = _tiles(m, n, kd)
    o_spec = pl.BlockSpec((tm, tn), lambda i, j, k: (i, j))
    return _mm_call(
        name, "nn", a, b, pl.BlockSpec((tm, tk), lambda i, j, k: (i, k)), pl.BlockSpec((tk, tn), lambda i, j, k: (k, j)),
        jax.ShapeDtypeStruct((m, n), out_dtype), o_spec, (m // tm, n // tn, kd // tk), (tm, tn), scale, resid, o_spec, rider)


def mm_nt(name, a, b, *, out_dtype=F32, scale=1.0, resid=None, rider=None):
    m, kd = a.shape
    n = b.shape[0]
    tm, tn, tk = _tiles(m, n, kd)
    o_spec = pl.BlockSpec((tm, tn), lambda i, j, k: (i, j))
    return _mm_call(
        name, "nt", a, b, pl.BlockSpec((tm, tk), lambda i, j, k: (i, k)), pl.BlockSpec((tn, tk), lambda i, j, k: (j, k)),
        jax.ShapeDtypeStruct((m, n), out_dtype), o_spec, (m // tm, n // tn, kd // tk), (tm, tn), scale, resid, o_spec, rider)


def mm_tn(name, a, b, *, out_dtype=F32, scale=1.0, rider=None):
    kd, m = a.shape
    n = b.shape[1]
    tm, tn, tk = _tiles(m, n, kd)
    return _mm_call(
        name, "tn", a, b, pl.BlockSpec((tk, tm), lambda i, j, k: (k, i)), pl.BlockSpec((tk, tn), lambda i, j, k: (k, j)),
        jax.ShapeDtypeStruct((m, n), out_dtype), pl.BlockSpec((tm, tn), lambda i, j, k: (i, j)),
        (m // tm, n // tn, kd // tk), (tm, tn), scale, rider=rider)


def mm_nt_sharded(name, a, w, *, resid=None, rider=None):
    m = a.shape[0]
    ns, n, c = w.shape
    tm, tn = _pick(m, (512, 256, 128)), _pick(n, (512, 256, 128))
    o_spec = pl.BlockSpec((tm, tn), lambda i, j, k: (i, j))
    return _mm_call(
        name, "nt", a, w, pl.BlockSpec((tm, c), lambda i, j, k: (i, k)), pl.BlockSpec((None, tn, c), lambda i, j, k: (k, j, 0)),
        jax.ShapeDtypeStruct((m, n), F32), o_spec, (m // tm, n // tn, ns), (tm, tn), 1.0, resid, o_spec, rider)


def mm_tn_sharded(name, a, b, ns, *, rider=None):
    kd, m = a.shape
    c = b.shape[1] // ns
    tm, tk = _pick(m, (512, 256, 128)), _pick(kd, (1024, 512, 256, 128))
    return _mm_call(
        name, "tn", a, b, pl.BlockSpec((tk, tm), lambda i, j, k: (k, i)), pl.BlockSpec((tk, c), lambda i, j, k: (k, j)),
        jax.ShapeDtypeStruct((ns, m, c), BF16), pl.BlockSpec((None, tm, c), lambda i, j, k: (j, i, 0)),
        (m // tm, ns, kd // tk), (tm, c), rider=rider)


def rms_fwd(name, x, g, out_dtype):
    r, c = x.shape
    tm = _pick(r, (512, 256, 128, 64, 8))

    def body(x_ref, g_ref, y_ref, r_ref):
        xf = x_ref[...].astype(F32)
        rstd = lax.rsqrt(jnp.mean(xf * xf, axis=-1, keepdims=True) + EPS)
        y_ref[...] = ((xf * rstd) * g_ref[...]).astype(y_ref.dtype)
        r_ref[...] = rstd

    (y, rstd), _ = _call(
        name, body, (r // tm,), [pl.BlockSpec((tm, c), lambda i: (i, 0)), pl.BlockSpec((1, c), lambda i: (0, 0))], (x, g.reshape(1, c)),
        [pl.BlockSpec((tm, c), lambda i: (i, 0)), pl.BlockSpec((tm, 1), lambda i: (i, 0))],
        [jax.ShapeDtypeStruct((r, c), out_dtype), jax.ShapeDtypeStruct((r, 1), F32)], (), ("parallel",))
    return y, rstd


def rms_bwd(name, x, g, rstd, dy, dres=None):
    r, c = x.shape
    tm = _pick(r, (512, 256, 128, 64, 8))
    has_res = dres is not None

    def body(*refs):
        if has_res:
            x_ref, g_ref, r_ref, dy_ref, dres_ref, dx_ref, dg_ref = refs
        else:
            x_ref, g_ref, r_ref, dy_ref, dx_ref, dg_ref = refs
        xhat = x_ref[...].astype(F32) * r_ref[...]
        dyf = dy_ref[...].astype(F32)
        gdy = dyf * g_ref[...]
        dx = r_ref[...] * (gdy - xhat * jnp.mean(gdy * xhat, axis=-1, keepdims=True))
        if has_res:
            dx = dx + dres_ref[...]
        dx_ref[...] = dx

        @pl.when(pl.program_id(0) == 0)
        def _():
            dg_ref[...] = jnp.zeros_like(dg_ref)

        dg_ref[...] += jnp.sum(dyf * xhat, axis=0, keepdims=True)

    row = pl.BlockSpec((tm, c), lambda i: (i, 0))
    in_specs = [row, pl.BlockSpec((1, c), lambda i: (0, 0)), pl.BlockSpec((tm, 1), lambda i: (i, 0)), row] + ([row] if has_res else [])
    args = (x, g.reshape(1, c), rstd, dy) + ((dres,) if has_res else ())
    (dx, dg), _ = _call(name, body, (r // tm,), in_specs, args, [row, pl.BlockSpec((1, c), lambda i: (0, 0))],
                        [jax.ShapeDtypeStruct((r, c), F32), jax.ShapeDtypeStruct((1, c), F32)], (), ("arbitrary",))
    return dx, dg.reshape(c)


@functools.partial(jax.custom_vjp, nondiff_argnums=(0,))
def rms_op(name, x, g):
    return rms_fwd(name + "_fwd", x, g, F32)[0]


def _rms_op_fwd(name, x, g):
    y, rstd = rms_fwd(name + "_fwd", x, g, F32)
    return y, (x, g, rstd)


def _rms_op_bwd(name, res, dy):
    x, g, rstd = res
    return rms_bwd(name + "_bwd", x, g, rstd, dy)


rms_op.defvjp(_rms_op_fwd, _rms_op_bwd)


def _sigmoid(x):
    return 1.0 / (1.0 + jnp.exp(-x))


def ffn_gu(name, xn, wg, wu, rider=None):
    s, d = xn.shape
    ns, _, c = wg.shape
    tm, tk = _pick(s, (512, 256, 128)), _pick(d, (1024, 512, 256, 128))
    nk = d // tk

    def body(x_ref, wg_ref, wu_ref, g_ref, u_ref, h_ref, accg, accu):
        k = pl.program_id(2)

        @pl.when(k == 0)
        def _():
            accg[...] = jnp.zeros_like(accg)
            accu[...] = jnp.zeros_like(accu)

        xb = x_ref[...]
        accg[...] += jnp.dot(xb, wg_ref[...], preferred_element_type=F32)
        accu[...] += jnp.dot(xb, wu_ref[...], preferred_element_type=F32)

        @pl.when(k == nk - 1)
        def _():
            gv, uv = accg[...], accu[...]
            g_ref[...] = gv
            u_ref[...] = uv
            h_ref[...] = ((gv * _sigmoid(gv)) * uv).astype(BF16)

    w_spec = pl.BlockSpec((None, tk, c), lambda i, j, k: (j, k, 0))
    o_spec = pl.BlockSpec((tm, c), lambda i, j, k: (i, j))
    return _call(
        name, body, (s // tm, ns, nk), [pl.BlockSpec((tm, tk), lambda i, j, k: (i, k)), w_spec, w_spec], (xn, wg, wu),
        [o_spec, o_spec, o_spec],
        [jax.ShapeDtypeStruct((s, ns * c), F32), jax.ShapeDtypeStruct((s, ns * c), F32), jax.ShapeDtypeStruct((s, ns * c), BF16)],
        [pltpu.VMEM((tm, c), F32), pltpu.VMEM((tm, c), F32)], ("parallel", "parallel", "arbitrary"), rider)


def ffn_dh(name, dy, wd, g, u, ns, scale, rider=None):
    s, d = dy.shape
    f = wd.shape[0]
    c = f // ns
    tm, tk = _pick(s, (512, 256, 128)), _pick(d, (1024, 512, 256, 128))
    nk = d // tk

    def body(dy_ref, wd_ref, g_ref, u_ref, dg_ref, du_ref, acc):
        k = pl.program_id(2)

        @pl.when(k == 0)
        def _():
            acc[...] = jnp.zeros_like(acc)

        acc[...] += lax.dot_general(dy_ref[...].astype(BF16), wd_ref[...], _DIMS["nt"], preferred_element_type=F32)

        @pl.when(k == nk - 1)
        def _():
            dh = acc[...] * scale
            gv, uv = g_ref[...], u_ref[...]
            sig = _sigmoid(gv)
            dg_ref[...] = (dh * uv * (sig * (1.0 + gv * (1.0 - sig)))).astype(BF16)
            du_ref[...] = (dh * (gv * sig)).astype(BF16)

    o_spec = pl.BlockSpec((tm, c), lambda i, j, k: (i, j))
    return _call(
        name, body, (s // tm, ns, nk),
        [pl.BlockSpec((tm, tk), lambda i, j, k: (i, k)), pl.BlockSpec((c, tk), lambda i, j, k: (j, k)), o_spec, o_spec], (dy, wd, g, u),
        [o_spec, o_spec], [jax.ShapeDtypeStruct((s, f), BF16), jax.ShapeDtypeStruct((s, f), BF16)],
        [pltpu.VMEM((tm, c), F32)], ("parallel", "parallel", "arbitrary"), rider)


FOX_TQ = 256


def _fox_scores(q_ref, k_ref, cq_ref, ck_ref, i, tq):
    s_len = k_ref.shape[0]
    s = lax.dot_general(q_ref[...].astype(BF16), k_ref[...].astype(BF16), _DIMS["nt"], preferred_element_type=F32)
    s = s * (HEAD_DIM ** -0.5) + cq_ref[...] - ck_ref[...]
    row = lax.broadcasted_iota(jnp.int32, (tq, s_len), 0) + i * tq
    col = lax.broadcasted_iota(jnp.int32, (tq, s_len), 1)
    return jnp.where(row >= col, s, MASK_VALUE)


def fox_fwd(q, k, v, cq, ck, rider=None):
    h, s_len, d = q.shape
    tq = min(FOX_TQ, s_len)

    def body(q_ref, k_ref, v_ref, cq_ref, ck_ref, o_ref, lse_ref):
        s = _fox_scores(q_ref, k_ref, cq_ref, ck_ref, pl.program_id(1), tq)
        m = jnp.max(s, axis=-1, keepdims=True)
        p = jnp.exp(s - m)
        l = jnp.sum(p, axis=-1, keepdims=True)
        o = jnp.dot(p.astype(BF16), v_ref[...].astype(BF16), preferred_element_type=F32)
        o_ref[...] = o / l
        lse_ref[...] = m + jnp.log(l)

    qb = pl.BlockSpec((None, tq, d), lambda hh, i: (hh, i, 0))
    kb = pl.BlockSpec((None, s_len, d), lambda hh, i: (hh, 0, 0))
    colb = pl.BlockSpec((None, tq, 1), lambda hh, i: (hh, i, 0))
    return _call(
        "fox_fwd", body, (h, s_len // tq), [qb, kb, kb, colb, pl.BlockSpec((None, 1, s_len), lambda hh, i: (hh, 0, 0))],
        (q, k, v, cq, ck), [qb, colb], [jax.ShapeDtypeStruct((h, s_len, d), F32), jax.ShapeDtypeStruct((h, s_len, 1), F32)],
        (), ("parallel", "parallel"), rider)


def fox_bwd(q, k, v, cq, ck, o, lse, do, rider=None):
    h, s_len, d = q.shape
    tq = min(FOX_TQ, s_len)
    scale = HEAD_DIM ** -0.5

    def body(q_ref, k_ref, v_ref, cq_ref, ck_ref, o_ref, lse_ref, do_ref, dq_ref, dk_ref, dv_ref, dcq_ref, dck_ref):
        i = pl.program_id(1)

        @pl.when(i == 0)
        def _():
            dk_ref[...] = jnp.zeros_like(dk_ref)
            dv_ref[...] = jnp.zeros_like(dv_ref)
            dck_ref[...] = jnp.zeros_like(dck_ref)

        s = _fox_scores(q_ref, k_ref, cq_ref, ck_ref, i, tq)
        p = jnp.exp(s - lse_ref[...])
        dof = do_ref[...]
        dob = dof.astype(BF16)
        delta = jnp.sum(dof * o_ref[...], axis=-1, keepdims=True)
        dp = lax.dot_general(dob, v_ref[...].astype(BF16), _DIMS["nt"], preferred_element_type=F32)
        ds = p * (dp - delta)
        dsb = ds.astype(BF16)
        dv_ref[...] += lax.dot_general(p.astype(BF16), dob, _DIMS["tn"], preferred_element_type=F32)
        dq_ref[...] = jnp.dot(dsb, k_ref[...].astype(BF16), preferred_element_type=F32) * scale
        dk_ref[...] += lax.dot_general(dsb, q_ref[...].astype(BF16), _DIMS["tn"], preferred_element_type=F32) * scale
        dcq_ref[...] = jnp.sum(ds, axis=-1, keepdims=True)
        dck_ref[...] -= jnp.sum(ds, axis=0, keepdims=True)

    qb = pl.BlockSpec((None, tq, d), lambda hh, i: (hh, i, 0))
    kb = pl.BlockSpec((None, s_len, d), lambda hh, i: (hh, 0, 0))
    colb = pl.BlockSpec((None, tq, 1), lambda hh, i: (hh, i, 0))
    rowb = pl.BlockSpec((None, 1, s_len), lambda hh, i: (hh, 0, 0))
    return _call(
        "fox_bwd", body, (h, s_len // tq), [qb, kb, kb, colb, rowb, qb, colb, qb], (q, k, v, cq, ck, o, lse, do),
        [qb, kb, kb, colb, rowb],
        [jax.ShapeDtypeStruct((h, s_len, d), F32)] * 3 + [jax.ShapeDtypeStruct((h, s_len, 1), F32), jax.ShapeDtypeStruct((h, 1, s_len), F32)],
        (), ("parallel", "arbitrary"), rider)


def _swa_probs(q_ref, kp_ref, kc_ref, sink_ref, n):
    g, w, d = q_ref.shape
    q = q_ref[...].reshape(g * w, d).astype(BF16)
    kw = jnp.concatenate([kp_ref[...], kc_ref[...]], axis=0).astype(BF16)
    s = lax.dot_general(q, kw, _DIMS["nt"], preferred_element_type=F32) * (HEAD_DIM ** -0.5)
    t = lax.broadcasted_iota(jnp.int32, (g * w, 2 * w), 0) & (w - 1)
    col = lax.broadcasted_iota(jnp.int32, (g * w, 2 * w), 1)
    rel = t + w - col
    valid = (rel >= 0) & (rel < w) & ((col >= w) | (n > 0))
    s = jnp.where(valid, s, MASK_VALUE)
    sink = sink_ref[...]
    m = jnp.maximum(jnp.max(s, axis=-1, keepdims=True), sink)
    p = jnp.exp(s - m)
    ps = jnp.exp(sink - m)
    linv = 1.0 / (jnp.sum(p, axis=-1, keepdims=True) + ps)
    return q, kw, p * linv, ps * linv


def _swa_specs(hk, g, s_len, d):
    w = WINDOW
    assert w & (w - 1) == 0 and s_len % w == 0
    qb = pl.BlockSpec((None, g, w, d), lambda hh, n: (hh, 0, n, 0))
    prev = pl.BlockSpec((None, w, d), lambda hh, n: (hh, jnp.maximum(n - 1, 0), 0))
    cur = pl.BlockSpec((None, w, d), lambda hh, n: (hh, n, 0))
    sb = pl.BlockSpec((None, g * w, 1), lambda hh, n: (hh, 0, 0))
    return qb, prev, cur, sb


def swa_fwd(q, k, v, sink, rider=None):
    hk, g, s_len, d = q.shape
    w = WINDOW
    qb, prev, cur, sb = _swa_specs(hk, g, s_len, d)

    def body(q_ref, kp_ref, kc_ref, vp_ref, vc_ref, sink_ref, o_ref):
        _, _, p, _ = _swa_probs(q_ref, kp_ref, kc_ref, sink_ref, pl.program_id(1))
        vw = jnp.concatenate([vp_ref[...], vc_ref[...]], axis=0).astype(BF16)
        o_ref[...] = jnp.dot(p.astype(BF16), vw, preferred_element_type=F32).reshape(g, w, d)

    (o,), rid = _call("swa_fwd", body, (hk, s_len // w), [qb, prev, cur, prev, cur, sb], (q, k, k, v, v, sink), [qb],
                      [jax.ShapeDtypeStruct((hk, g, s_len, d), F32)], (), ("parallel", "parallel"), rider)
    return o, rid


def swa_bwd(q, k, v, sink, o, do, rider=None):
    hk, g, s_len, d = q.shape
    w = WINDOW
    scale = HEAD_DIM ** -0.5
    qb, prev, cur, sb = _swa_specs(hk, g, s_len, d)

    def body(q_ref, kp_ref, kc_ref, vp_ref, vc_ref, sink_ref, o_ref, do_ref, dq_ref, dkp_ref, dkc_ref, dvp_ref, dvc_ref, dsink_ref):
        n = pl.program_id(1)

        @pl.when(n == 0)
        def _():
            dsink_ref[...] = jnp.zeros_like(dsink_ref)

        q, kw, p, ps = _swa_probs(q_ref, kp_ref, kc_ref, sink_ref, n)
        vw = jnp.concatenate([vp_ref[...], vc_ref[...]], axis=0).astype(BF16)
        dof = do_ref[...].reshape(g * w, d)
        dob = dof.astype(BF16)
        delta = jnp.sum(dof * o_ref[...].reshape(g * w, d), axis=-1, keepdims=True)
        dp = lax.dot_general(dob, vw, _DIMS["nt"], preferred_element_type=F32)
        ds = p * (dp - delta)
        dsb = ds.astype(BF16)
        dsink_ref[...] -= ps * delta
        dq_ref[...] = (jnp.dot(dsb, kw, preferred_element_type=F32) * scale).reshape(g, w, d)
        dkw = lax.dot_general(dsb, q, _DIMS["tn"], preferred_element_type=F32) * scale
        dvw = lax.dot_general(p.astype(BF16), dob, _DIMS["tn"], preferred_element_type=F32)
        dkp_ref[...] = dkw[:w]
        dkc_ref[...] = dkw[w:]
        dvp_ref[...] = dvw[:w]
        dvc_ref[...] = dvw[w:]

    kv_shape = jax.ShapeDtypeStruct((hk, s_len, d), F32)
    (dq, dkp, dkc, dvp, dvc, dsink), rid = _call(
        "swa_bwd", body, (hk, s_len // w), [qb, prev, cur, prev, cur, sb, qb, qb], (q, k, k, v, v, sink, o, do),
        [qb, cur, cur, cur, cur, sb],
        [jax.ShapeDtypeStruct((hk, g, s_len, d), F32), kv_shape, kv_shape, kv_shape, kv_shape, jax.ShapeDtypeStruct((hk, g * w, 1), F32)],
        (), ("parallel", "arbitrary"), rider)

    def shift_up(a):
        return jnp.concatenate([a[:, w:], jnp.zeros_like(a[:, :w])], axis=1)

    return (dq, dkc + shift_up(dkp), dvc + shift_up(dvp), dsink), rid


def loss_call(y, target):
    s, d = y.shape
    tm = _pick(s, (512, 256, 128))

    def body(y_ref, t_ref, l_ref, dy_ref):
        e = y_ref[...] - t_ref[...]
        dy_ref[...] = e * (1.0 / d)

        @pl.when(pl.program_id(0) == 0)
        def _():
            l_ref[...] = jnp.zeros_like(l_ref)

        l_ref[...] += jnp.sum(jnp.sum(e * e, axis=0, keepdims=True), axis=1, keepdims=True) * (0.5 / d)

    row = pl.BlockSpec((tm, d), lambda i: (i, 0))
    (l, dy), _ = _call("loss_head", body, (s // tm,), [row, row], (y, target), [pl.BlockSpec((1, 1), lambda i: (0, 0)), row],
                       [jax.ShapeDtypeStruct((1, 1), F32), jax.ShapeDtypeStruct((s, d), F32)], (), ("arbitrary",))
    return l[0, 0], dy


def _row_tile(rows, cols, itemsize):
    target = max(16, (1 << 20) // (cols * itemsize))
    for t in (1024, 512, 256, 128, 64, 32, 16):
        if t <= target and rows % t == 0:
            return t
    return rows


def cast_place(name, w, p_idx):
    r, cols = w.shape
    tr = _row_tile(r, cols, 4)

    def body(p_ref, w_ref, o_ref):
        o_ref[...] = w_ref[...].astype(BF16)

    return pl.pallas_call(
        body, name=name,
        grid_spec=pltpu.PrefetchScalarGridSpec(
            num_scalar_prefetch=1, grid=(r // tr,),
            in_specs=[pl.BlockSpec((tr, cols), lambda i, pr: (i, 0))],
            out_specs=pl.BlockSpec((None, tr, cols), lambda i, pr: (pr[0], i, 0))),
        out_shape=jax.ShapeDtypeStruct((N_CHIPS, r, cols), BF16),
        compiler_params=pltpu.CompilerParams(dimension_semantics=("parallel",), vmem_limit_bytes=VMEM_LIMIT),
    )(p_idx, w)


def chip_sum(name, grad, theirs, c_idx):
    ns, r, cols = grad.shape
    rh = r // 2
    tr = _row_tile(rh, cols, 2)
    nb = rh // tr

    def body(c_ref, a_ref, b_ref, o_ref):
        o_ref[...] = (a_ref[...].astype(F32) + b_ref[...].astype(F32)).astype(o_ref.dtype)

    return pl.pallas_call(
        body, name=name,
        grid_spec=pltpu.PrefetchScalarGridSpec(
            num_scalar_prefetch=1, grid=(ns, nb),
            in_specs=[pl.BlockSpec((None, tr, cols), lambda q, i, cr: (q, cr[0] * nb + i, 0)),
                      pl.BlockSpec((None, tr, cols), lambda q, i, cr: (q, i, 0))],
            out_specs=pl.BlockSpec((None, tr, cols), lambda q, i, cr: (q, i, 0))),
        out_shape=jax.ShapeDtypeStruct((ns, rh, cols), BF16),
        compiler_params=pltpu.CompilerParams(dimension_semantics=("parallel", "parallel"), vmem_limit_bytes=VMEM_LIMIT),
    )(c_idx, grad, theirs)


def owner_sum(name, sums, got, pc_idx):
    ns, rh, cols = sums.shape
    tr = _row_tile(rh, cols, 4)
    nb = rh // tr

    def body(pc_ref, a_ref, b_ref, o_ref):
        o_ref[...] = ((a_ref[...].astype(F32) + b_ref[0].astype(F32)) + b_ref[1].astype(F32)) + b_ref[2].astype(F32)

    return pl.pallas_call(
        body, name=name,
        grid_spec=pltpu.PrefetchScalarGridSpec(
            num_scalar_prefetch=1, grid=(nb,),
            in_specs=[pl.BlockSpec((None, tr, cols), lambda i, pc: (pc[0], i, 0)),
                      pl.BlockSpec((3, tr, cols), lambda i, pc: (0, i, 0))],
            out_specs=pl.BlockSpec((tr, cols), lambda i, pc: (pc[1] * nb + i, 0))),
        out_shape=jax.ShapeDtypeStruct((2 * rh, cols), F32),
        compiler_params=pltpu.CompilerParams(dimension_semantics=("parallel",), vmem_limit_bytes=VMEM_LIMIT),
    )(pc_idx, sums, got)


def adamw(name, w, g, m, v):
    r, cols = w.shape
    tr = _row_tile(r, cols, 4)
    c1 = 1.0 / (1.0 - ADAM_B1 ** ADAM_STEP)
    c2 = 1.0 / (1.0 - ADAM_B2 ** ADAM_STEP)

    def body(w_ref, g_ref, m_ref, v_ref, d_ref, nm_ref, nv_ref):
        gv = g_ref[...]
        nm = ADAM_B1 * m_ref[...] + (1.0 - ADAM_B1) * gv
        nv = ADAM_B2 * v_ref[...] + (1.0 - ADAM_B2) * (gv * gv)
        d_ref[...] = -ADAM_LR * ((nm * c1) / (jnp.sqrt(nv * c2) + ADAM_EPS) + ADAM_WD * w_ref[...])
        nm_ref[...] = nm
        nv_ref[...] = nv

    blk = pl.BlockSpec((tr, cols), lambda i: (i, 0))
    outs, _ = _call(name, body, (r // tr,), [blk] * 4, (w, g, m, v), [blk] * 3, [jax.ShapeDtypeStruct((r, cols), F32)] * 3, (), ("parallel",))
    return outs


def _rope(x, cos, sin):
    x1, x2 = x[..., : HEAD_DIM // 2], x[..., HEAD_DIM // 2:]
    return jnp.concatenate([x1 * cos - x2 * sin, x2 * cos + x1 * sin], axis=-1)


def _heads(a, nh):
    return a.reshape(a.shape[0], nh, HEAD_DIM).transpose(1, 0, 2)


def _unheads(a):
    return a.transpose(1, 0, 2).reshape(a.shape[1], a.shape[0] * HEAD_DIM)


def _win_layout(d_model):
    hf = hq = d_model // (2 * HEAD_DIM)
    hk = hq // 4
    sizes = [hf * HEAD_DIM, hf * HEAD_DIM, hf * HEAD_DIM, hf, hq * HEAD_DIM, hk * HEAD_DIM, hk * HEAD_DIM]
    return hf, hq, hk, sizes


def _pad_cols(a, n):
    return a if a.shape[1] == n else jnp.pad(a, ((0, 0), (0, n - a.shape[1])))


def _relayout_win(win_g):
    ns, d_model, cs = win_g.shape
    _, _, _, sizes = _win_layout(d_model)
    full = win_g.transpose(1, 0, 2).reshape(d_model, ns * cs)
    offs = [0]
    for sz in sizes:
        offs.append(offs[-1] + sz)
    qf, kf, vf, fl, qs, ks, vs = [full[:, offs[i]:offs[i + 1]] for i in range(7)]
    body = jnp.concatenate([qf, kf, vf, qs, ks, vs, _pad_cols(fl, 128)], axis=1)
    return _pad_cols(body, -(-body.shape[1] // 512) * 512)


def _attn_inputs(proj, sm, positions):
    s_len = proj.shape[0]
    hf, hq, hk, sizes = _win_layout(sm["norm_mix_g"].shape[0])
    grp = hq // hk
    o0 = 0
    cols = []
    for sz in (sizes[0], sizes[1], sizes[2], sizes[4], sizes[5], sizes[6], hf):
        cols.append(proj[:, o0:o0 + sz])
        o0 += sz
    q_f, k_f, v_f, q_s, k_s, v_s, f_logit = cols

    q_f = rms_op("fox_qnorm", _heads(q_f, hf).reshape(hf * s_len, HEAD_DIM), sm["fox_q_norm_g"]).reshape(hf, s_len, HEAD_DIM)
    k_f = rms_op("fox_knorm", _heads(k_f, hf).reshape(hf * s_len, HEAD_DIM), sm["fox_k_norm_g"]).reshape(hf, s_len, HEAD_DIM)
    v_f = _heads(v_f, hf)
    log_f = jax.nn.log_sigmoid(f_logit + sm["b_forget"])
    c = jnp.cumsum(log_f, axis=0).T

    inv_freq = ROPE_THETA ** (-jnp.arange(0, HEAD_DIM, 2, dtype=F32) / HEAD_DIM)
    ang = positions.astype(F32)[:, None] * inv_freq
    cos, sin = jnp.cos(ang), jnp.sin(ang)
    q_s = rms_op("swa_qnorm", _heads(q_s, hq).reshape(hq * s_len, HEAD_DIM), sm["swa_q_norm_g"]).reshape(hq, s_len, HEAD_DIM)
    k_s = rms_op("swa_knorm", _heads(k_s, hk).reshape(hk * s_len, HEAD_DIM), sm["swa_k_norm_g"]).reshape(hk, s_len, HEAD_DIM)
    q_s = _rope(q_s, cos, sin).reshape(hk, grp, s_len, HEAD_DIM)
    k_s = _rope(k_s, cos, sin)
    v_s = _heads(v_s, hk)
    sink = jnp.broadcast_to(sm["swa_sinks"].reshape(hk, grp, 1, 1), (hk, grp, WINDOW, 1)).reshape(hk, grp * WINDOW, 1)
    return (q_f, k_f, v_f, c[:, :, None], c[:, None, :]), (q_s, k_s, v_s, sink)


_BIG = ("ffn1_w_gate", "ffn1_w_up", "ffn1_w_down", "w_in", "w_out", "ffn2_w_gate", "ffn2_w_up", "ffn2_w_down")
_ROW_SHARDED = ("ffn1_w_down", "w_out", "ffn2_w_down")
_SMALL = ("norm_ffn1_g", "norm_mix_g", "b_forget", "fox_q_norm_g", "fox_k_norm_g", "swa_q_norm_g", "swa_k_norm_g", "swa_sinks",
          "out_norm_fox_g", "out_norm_swa_g", "norm_ffn2_g")
_ATTN_SMALL = ("norm_mix_g", "b_forget", "fox_q_norm_g", "fox_k_norm_g", "swa_q_norm_g", "swa_k_norm_g", "swa_sinks")
_ALL = ("norm_ffn1_g", "ffn1_w_gate", "ffn1_w_up", "ffn1_w_down", "norm_mix_g", "w_in", "b_forget", "fox_q_norm_g", "fox_k_norm_g",
        "swa_q_norm_g", "swa_k_norm_g", "swa_sinks", "out_norm_fox_g", "out_norm_swa_g", "w_out", "norm_ffn2_g", "ffn2_w_gate",
        "ffn2_w_up", "ffn2_w_down")
_LANES = 128


def _pack_small(d):
    parts = []
    for k in _SMALL:
        v = d[k].reshape(-1)
        rows = -(-v.shape[0] // _LANES)
        parts.append(jnp.pad(v, (0, rows * _LANES - v.shape[0])).reshape(rows, _LANES))
    a = jnp.concatenate(parts, axis=0)
    return jnp.pad(a, ((0, -a.shape[0] % 8), (0, 0)))


def _unpack_small(a, like):
    out, r0 = {}, 0
    for k in _SMALL:
        nvals = like[k].shape[1]
        rows = -(-nvals // _LANES)
        out[k] = a[r0:r0 + rows].reshape(-1)[:nvals].reshape(1, nvals)
        r0 += rows
    return out


def _stacked(w):
    return w.reshape(-1, w.shape[-1])


def _local_step(bufs, sm, x, positions, target, c_idx, pc_idx):
    bufs = dict(bufs)
    ns = N_CHIPS
    hf, hq, hk, _ = _win_layout(x.shape[1])
    s_len = x.shape[0]
    full = {}

    def take(names, rid):
        for n, b in zip(names, rid[0]):
            bufs[n] = b

    def reduce_start(names, grads):
        return exchange_halves([g.reshape(ns, -1, g.shape[-1]) for g in grads])

    def chip_sums(names, grads, rid):
        return [chip_sum("chip_sum_" + n, g.reshape(ns, -1, g.shape[-1]), t, c_idx) for n, g, t in zip(names, grads, rid[1])]

    n1 = ["ffn1_w_gate", "ffn1_w_up", "ffn1_w_down"]
    take(n1, run_step("gather_ici_ffn1", gather_ici([bufs[n] for n in n1])))
    take(n1, run_step("gather_d2d_ffn1", gather_d2d([bufs[n] for n in n1])))
    xn1, r1 = rms_fwd("ffn1_norm", x, sm["norm_ffn1_g"], BF16)
    na = ["w_in", "w_out"]
    (g1, u1, hid1), rid = ffn_gu("ffn1_gu", xn1, bufs["ffn1_w_gate"], bufs["ffn1_w_up"], rider=gather_ici([bufs[n] for n in na]))
    take(na, rid)
    wd1 = _stacked(bufs["ffn1_w_down"])
    h1, rid = mm_nn("ffn1_down", hid1, wd1, scale=0.5, resid=x,
                    rider=combine(gather_d2d([bufs[n] for n in na]), gather_ici([bufs["ffn2_w_gate"]])))
    take(na + ["ffn2_w_gate"], rid)

    u, r_mix = rms_fwd("mix_norm", h1, sm["norm_mix_g"], BF16)
    win_p, win_vjp = jax.vjp(_relayout_win, bufs["w_in"])
    proj, rid = mm_nn("mix_inproj", u, win_p, rider=gather_ici([bufs["ffn2_w_up"]]))
    take(["ffn2_w_up"], rid)
    sm_attn = {k: sm[k] for k in _ATTN_SMALL}
    (fox_in, swa_in), attn_vjp = jax.vjp(lambda pr, s: _attn_inputs(pr, s, positions), proj, sm_attn)
    (o_f, lse), rid = fox_fwd(*fox_in, rider=combine(gather_d2d([bufs["ffn2_w_gate"], bufs["ffn2_w_up"]]), gather_ici([bufs["ffn2_w_down"]])))
    take(["ffn2_w_gate", "ffn2_w_up", "ffn2_w_down"], rid)
    o_s, rid = swa_fwd(*swa_in, rider=gather_d2d([bufs["ffn2_w_down"]]))
    take(["ffn2_w_down"], rid)
    o_fox, o_swa = _unheads(o_f), _unheads(o_s.reshape(hq, s_len, HEAD_DIM))
    nf, r_fox = rms_fwd("out_norm_fox", o_fox, sm["out_norm_fox_g"], BF16)
    nsw, r_swa = rms_fwd("out_norm_swa", o_swa, sm["out_norm_swa_g"], BF16)
    o = jnp.concatenate([nf, nsw], axis=-1)
    wout = _stacked(bufs["w_out"])
    h2, _ = mm_nn("out_proj", o, wout, resid=h1)

    xn2, r2 = rms_fwd("ffn2_norm", h2, sm["norm_ffn2_g"], BF16)
    (g2, u2, hid2), _ = ffn_gu("ffn2_gu", xn2, bufs["ffn2_w_gate"], bufs["ffn2_w_up"])
    wd2 = _stacked(bufs["ffn2_w_down"])
    y, _ = mm_nn("ffn2_down", hid2, wd2, scale=0.5, resid=h2)
    loss, dy = loss_call(y, target)

    n2 = ["ffn2_w_gate", "ffn2_w_up", "ffn2_w_down"]
    (dg2, du2), _ = ffn_dh("ffn2_dh", dy, wd2, g2, u2, ns, 0.5)
    dwd2, _ = mm_tn("ffn2_dwd", hid2, dy, out_dtype=BF16, scale=0.5)
    dwg2, _ = mm_tn_sharded("ffn2_dwg", xn2, dg2, ns)
    dwu2, _ = mm_tn_sharded("ffn2_dwu", xn2, du2, ns)
    gr2 = [dwg2, dwu2, dwd2]
    dxn, rid = mm_nt_sharded("ffn2_dxn_g", dg2, bufs["ffn2_w_gate"], rider=reduce_start(n2, gr2))
    sums2 = chip_sums(n2, gr2, rid)
    dxn, rid = mm_nt_sharded("ffn2_dxn_u", du2, bufs["ffn2_w_up"], resid=dxn, rider=scatter_to_owner(sums2[:1]))
    got2 = [rid[1][0]]
    dh2, dgain_ffn2 = rms_bwd("ffn2_dnorm", h2, sm["norm_ffn2_g"], r2, dxn, dres=dy)

    do, _ = mm_nt("out_do", dh2, wout)
    dwout, _ = mm_tn("out_dw", o, dh2, out_dtype=BF16)
    cf = o_fox.shape[1]
    d_fox, dgain_fox = rms_bwd("out_dnorm_fox", o_fox, sm["out_norm_fox_g"], r_fox, do[:, :cf])
    d_swa, dgain_swa = rms_bwd("out_dnorm_swa", o_swa, sm["out_norm_swa_g"], r_swa, do[:, cf:])
    swa_cts, rid = swa_bwd(*swa_in, o_s, _heads(d_swa, hq).reshape(o_s.shape), rider=scatter_to_owner(sums2[1:2]))
    got2.append(rid[1][0])
    fox_cts, rid = fox_bwd(*fox_in, o_f, lse, _heads(d_fox, hf),
                           rider=combine(scatter_to_owner(sums2[2:3]), reduce_start(["w_out"], [dwout])))
    got2.append(rid[1][0])
    sum_wout = chip_sums(["w_out"], [dwout], (None, rid[1][1:]))
    halves2 = [owner_sum("owner_sum_" + n, s, g, pc_idx) for n, s, g in zip(n2, sums2, got2)]
    dproj, dsm_attn = attn_vjp((tuple(fox_cts), tuple(swa_cts)))

    du, rid = mm_nt("mix_du", dproj, win_p, rider=join_halves(halves2))
    full.update(zip(n2, rid[0]))
    dwin_p, rid = mm_tn("mix_dwin", u, dproj, out_dtype=BF16, rider=scatter_to_owner(sum_wout))
    got_wout = rid[1]
    (dwin,) = win_vjp(dwin_p)
    dh1, dgain_mix = rms_bwd("mix_dnorm", h1, sm["norm_mix_g"], r_mix, du, dres=dh2)

    (dg1, du1), rid = ffn_dh("ffn1_dh", dh1, wd1, g1, u1, ns, 0.5, rider=reduce_start(["w_in"], [dwin]))
    sum_win = chip_sums(["w_in"], [dwin], rid)
    dwd1, rid = mm_tn("ffn1_dwd", hid1, dh1, out_dtype=BF16, scale=0.5, rider=scatter_to_owner(sum_win))
    got_win = rid[1]
    dwg1, _ = mm_tn_sharded("ffn1_dwg", xn1, dg1, ns)
    dwu1, _ = mm_tn_sharded("ffn1_dwu", xn1, du1, ns)
    gr1 = [dwg1, dwu1, dwd1]
    dxn, rid = mm_nt_sharded("ffn1_dxn_g", dg1, bufs["ffn1_w_gate"], rider=reduce_start(n1, gr1))
    sums1 = chip_sums(n1, gr1, rid)
    dxn, rid = mm_nt_sharded("ffn1_dxn_u", du1, bufs["ffn1_w_up"], resid=dxn, rider=scatter_to_owner(sums1[:1]))
    got1 = [rid[1][0]]
    dx, dgain_ffn1 = rms_bwd("ffn1_dnorm", x, sm["norm_ffn1_g"], r1, dxn, dres=dh1)

    _, got_tail = run_step("scatter_to_owner_tail", scatter_to_owner(sums1[1:]))
    got1 += got_tail
    names = ["w_out", "w_in"] + n1
    halves = [owner_sum("owner_sum_" + n, s, g, pc_idx)
              for n, s, g in zip(names, sum_wout + sum_win + sums1, got_wout + got_win + got1)]
    joined, _ = run_step("join_halves_tail", join_halves(halves))
    full.update(zip(names, joined))

    g_small = dict(dsm_attn)
    g_small["norm_mix_g"] = g_small["norm_mix_g"] + dgain_mix
    g_small.update(norm_ffn1_g=dgain_ffn1, norm_ffn2_g=dgain_ffn2, out_norm_fox_g=dgain_fox, out_norm_swa_g=dgain_swa)
    return loss, dx, full, g_small


def kernel(x, positions, norm_ffn1_g, ffn1_w_gate, ffn1_w_up, ffn1_w_down, norm_mix_g, w_in, b_forget, fox_q_norm_g, fox_k_norm_g, swa_q_norm_g, swa_k_norm_g, swa_sinks, out_norm_fox_g, out_norm_swa_g, w_out, norm_ffn2_g, ffn2_w_gate, ffn2_w_up, ffn2_w_down, loss_target, m_norm_ffn1_g, m_ffn1_w_gate, m_ffn1_w_up, m_ffn1_w_down, m_norm_mix_g, m_w_in, m_b_forget, m_fox_q_norm_g, m_fox_k_norm_g, m_swa_q_norm_g, m_swa_k_norm_g, m_swa_sinks, m_out_norm_fox_g, m_out_norm_swa_g, m_w_out, m_norm_ffn2_g, m_ffn2_w_gate, m_ffn2_w_up, m_ffn2_w_down, v_norm_ffn1_g, v_ffn1_w_gate, v_ffn1_w_up, v_ffn1_w_down, v_norm_mix_g, v_w_in, v_b_forget, v_fox_q_norm_g, v_fox_k_norm_g, v_swa_q_norm_g, v_swa_k_norm_g, v_swa_sinks, v_out_norm_fox_g, v_out_norm_swa_g, v_w_out, v_norm_ffn2_g, v_ffn2_w_gate, v_ffn2_w_up, v_ffn2_w_down):
    args = dict(locals())
    w = {k: args[k] for k in _ALL}
    m = {k: args["m_" + k] for k in _ALL}
    v = {k: args["v_" + k] for k in _ALL}
    c_idx = lax.axis_index("c").astype(jnp.int32).reshape(1)
    p_idx = (2 * lax.axis_index("x") + lax.axis_index("y")).astype(jnp.int32).reshape(1)
    pc_idx = jnp.concatenate([p_idx, c_idx])

    bufs = {k: cast_place("cast_place_" + k, w[k][0], p_idx) for k in _BIG}
    small = {k: w[k] for k in _SMALL}
    loss, grad_x, g_shard, g_small = _local_step(bufs, {k: w[k][0] for k in _SMALL}, x[0], positions[0], loss_target[0], c_idx, pc_idx)
    loss = lax.psum(loss, ("x", "y", "c"))
    g_small_sum = _unpack_small(all_reduce_small(_pack_small({k: g_small[k].reshape(1, -1) for k in _SMALL})), small)

    grad_w, delta, new_m, new_v = {}, {}, {}, {}
    for k in _BIG:
        grad_w[k] = g_shard[k][None]
        d, nm, nv = adamw("adamw_" + k, w[k][0], g_shard[k], m[k][0], v[k][0])
        delta[k], new_m[k], new_v[k] = d[None], nm[None], nv[None]
    d, nm, nv = adamw("adamw_small", _pack_small(small), _pack_small(g_small_sum), _pack_small({k: m[k] for k in _SMALL}),
                      _pack_small({k: v[k] for k in _SMALL}))
    grad_w.update(g_small_sum)
    delta.update(_unpack_small(d, small))
    new_m.update(_unpack_small(nm, small))
    new_v.update(_unpack_small(nv, small))

    return (loss, grad_x[None], *[grad_w[k] for k in _ALL], *[delta[k] for k in _ALL], *[new_m[k] for k in _ALL], *[new_v[k] for k in _ALL])
```

```python
import functools

import jax
import jax.numpy as jnp
from jax import lax
from jax.experimental import pallas as pl
from jax.experimental.pallas import tpu as pltpu

F32 = jnp.float32
BF16 = jnp.bfloat16

HEAD_DIM = 64
WINDOW = 128
ROPE_THETA = 10000.0
EPS = 1e-6
N_CHIPS = 4
N_DEV = 8

ADAM_LR = 0.001
ADAM_B1 = 0.9
ADAM_B2 = 0.999
ADAM_EPS = 1e-08
ADAM_WD = 0.01
ADAM_STEP = 10

V7X_VMEM_BYTES = 64 * 1024 * 1024
VMEM_LIMIT = V7X_VMEM_BYTES - 8 * 1024 * 1024
MASK_VALUE = -1e30

_MESH = pl.DeviceIdType.MESH
_HBM = pl.BlockSpec(memory_space=pl.ANY)
_DIMS = {"nn": (((1,), (0,)), ((), ())), "nt": (((1,), (1,)), ((), ())), "tn": (((0,), (0,)), ((), ()))}


def _pick(n, prefs):
    for p in prefs:
        if n % p == 0:
            return p
    return n


class Rider:
    def __init__(self, reads, aliased, news, nsem, build):
        self.reads, self.aliased, self.news, self.nsem, self.build = list(reads), list(aliased), list(news), nsem, build


class _Shifted:
    def __init__(self, ref, off):
        self.ref, self.off = ref, off

    @property
    def at(self):
        return self

    def __getitem__(self, k):
        return self.ref.at[k + self.off]


def combine(*riders):
    def build(reads, al, news, ssem, rsem):
        out = ([], [], [])
        r0 = a0 = n0 = s0 = 0
        for rd in riders:
            nr, na, nn = len(rd.reads), len(rd.aliased), len(rd.news)
            part = rd.build(reads[r0:r0 + nr], al[a0:a0 + na], news[n0:n0 + nn], _Shifted(ssem, s0), _Shifted(rsem, s0))
            for acc, lst in zip(out, part):
                acc.extend(lst)
            r0, a0, n0, s0 = r0 + nr, a0 + na, n0 + nn, s0 + rd.nsem
        return out

    return Rider(sum((r.reads for r in riders), []), sum((r.aliased for r in riders), []), sum((r.news for r in riders), []),
                 sum(r.nsem for r in riders), build)


def _me():
    return lax.axis_index("x"), lax.axis_index("y"), lax.axis_index("c")


def _other_chips(x, y):
    return [(1 - x, y), (x, 1 - y), (1 - x, 1 - y)]


def _half(ref, h):
    rows = ref.shape[-2] // 2
    idx = (slice(None),) * (len(ref.shape) - 2) + (pl.ds(h * rows, rows), slice(None))
    return ref.at[idx]


def _remote(src, dst, ssem, rsem, k, to):
    return pltpu.make_async_remote_copy(src_ref=src, dst_ref=dst, send_sem=ssem.at[k], recv_sem=rsem.at[k], device_id=to,
                                        device_id_type=_MESH)


def _later(*args):
    return functools.partial(_remote, *args)


def gather_ici(bufs):
    def build(reads, al, news, ssem, rsem):
        x, y, c = _me()
        p = 2 * x + y
        starts, arrivals = [], []
        for w, buf in enumerate(al):
            mine = _half(buf.at[p], c)
            for j, chip in enumerate(_other_chips(x, y)):
                landing = _half(buf.at[2 * chip[0] + chip[1]], c)
                starts.append(_later(mine, mine, ssem, rsem, 3 * w + j, (*chip, c)))
                arrivals.append(_later(landing, landing, ssem, rsem, 3 * w + j, (*chip, c)))
        return starts, arrivals, starts

    return Rider([], bufs, [], 3 * len(bufs), build)


def gather_d2d(bufs):
    def build(reads, al, news, ssem, rsem):
        x, y, c = _me()
        starts, arrivals = [], []
        for w, buf in enumerate(al):
            for j, chip in enumerate(_other_chips(x, y)):
                q = 2 * chip[0] + chip[1]
                landed, landing = _half(buf.at[q], c), _half(buf.at[q], 1 - c)
                starts.append(_later(landed, landed, ssem, rsem, 3 * w + j, (x, y, 1 - c)))
                arrivals.append(_later(landing, landing, ssem, rsem, 3 * w + j, (x, y, 1 - c)))
        return starts, arrivals, starts

    return Rider([], bufs, [], 3 * len(bufs), build)


def exchange_halves(grads):
    def build(reads, al, news, ssem, rsem):
        x, y, c = _me()
        cps = [_later(_half(g, 1 - c), t, ssem, rsem, w, (x, y, 1 - c)) for w, (g, t) in enumerate(zip(reads, news))]
        return cps, cps, cps

    return Rider(grads, [], [jax.ShapeDtypeStruct((g.shape[0], g.shape[1] // 2, g.shape[2]), g.dtype) for g in grads], len(grads), build)


def scatter_to_owner(sums):
    def build(reads, al, news, ssem, rsem):
        x, y, c = _me()
        cps = []
        for w, (s, got) in enumerate(zip(reads, news)):
            for j, chip in enumerate(_other_chips(x, y)):
                cps.append(_later(s.at[2 * chip[0] + chip[1]], got.at[j], ssem, rsem, 3 * w + j, (*chip, c)))
        return cps, cps, cps

    return Rider(sums, [], [jax.ShapeDtypeStruct((3,) + s.shape[1:], s.dtype) for s in sums], 3 * len(sums), build)


def join_halves(fulls):
    def build(reads, al, news, ssem, rsem):
        x, y, c = _me()
        starts, arrivals = [], []
        for w, f in enumerate(al):
            mine, landing = _half(f, c), _half(f, 1 - c)
            starts.append(_later(mine, mine, ssem, rsem, w, (x, y, 1 - c)))
            arrivals.append(_later(landing, landing, ssem, rsem, w, (x, y, 1 - c)))
        return starts, arrivals, starts

    return Rider([], fulls, [], len(fulls), build)


def _start_and_wait(rider, reads, al, news, ssem, rsem, first, last):
    @pl.when(first)
    def _():
        for cp in rider.build(reads, al, news, ssem, rsem)[0]:
            cp().start()

    def finish():
        @pl.when(last)
        def _():
            _, arrivals, sends = rider.build(reads, al, news, ssem, rsem)
            for cp in arrivals:
                cp().wait_recv()
            for cp in sends:
                cp().wait_send()

    return finish


def _call(name, body, grid, in_specs, args, out_specs, out_shape, scratch=(), semantics=None, rider=None, prefetch=()):
    n_pre, n_in, n_out, n_scr = len(prefetch), len(args), len(out_shape), len(scratch)
    nr, na, nn = (len(rider.reads), len(rider.aliased), len(rider.news)) if rider else (0, 0, 0)

    def wrapped(*refs):
        pre, refs = refs[:n_pre], refs[n_pre:]
        ins, reads = refs[:n_in], refs[n_in:n_in + nr]
        o0 = n_in + nr + na
        outs, al, news = refs[o0:o0 + n_out], refs[o0 + n_out:o0 + n_out + na], refs[o0 + n_out + na:o0 + n_out + na + nn]
        s0 = o0 + n_out + na + nn
        scr, (ssem, rsem) = refs[s0:s0 + n_scr], refs[s0 + n_scr:]
        first = functools.reduce(jnp.logical_and, [pl.program_id(a) == 0 for a in range(len(grid))])
        last = functools.reduce(jnp.logical_and, [pl.program_id(a) == g - 1 for a, g in enumerate(grid)])
        finish = _start_and_wait(rider, reads, al, news, ssem, rsem, first, last)
        body(*pre, *ins, *outs, *scr)
        finish()

    kernel_fn, all_in, all_out, shapes, scr = body, list(in_specs), list(out_specs), list(out_shape), list(scratch)
    operands, aliases = (*prefetch, *args), {}
    if rider:
        kernel_fn, semantics = wrapped, ("arbitrary",) * len(grid)
        all_in += [_HBM] * (nr + na)
        all_out += [_HBM] * (na + nn)
        shapes += [jax.ShapeDtypeStruct(a.shape, a.dtype) for a in rider.aliased] + rider.news
        scr += [pltpu.SemaphoreType.DMA((rider.nsem,)), pltpu.SemaphoreType.DMA((rider.nsem,))]
        operands += (*rider.reads, *rider.aliased)
        aliases = {n_pre + n_in + nr + i: n_out + i for i in range(na)}
    params = pltpu.CompilerParams(dimension_semantics=semantics, vmem_limit_bytes=VMEM_LIMIT)
    if n_pre:
        spec = pltpu.PrefetchScalarGridSpec(num_scalar_prefetch=n_pre, grid=grid, in_specs=all_in, out_specs=all_out, scratch_shapes=scr)
        outs = pl.pallas_call(kernel_fn, name=name, grid_spec=spec, out_shape=shapes, input_output_aliases=aliases, compiler_params=params)(*operands)
    else:
        outs = pl.pallas_call(kernel_fn, name=name, grid=grid, in_specs=all_in, out_specs=all_out, out_shape=shapes, scratch_shapes=scr,
                              input_output_aliases=aliases, compiler_params=params)(*operands)
    return list(outs[:n_out]), ((list(outs[n_out:n_out + na]), list(outs[n_out + na:])) if rider else None)


def run_step(name, rider):
    nr, na, nn = len(rider.reads), len(rider.aliased), len(rider.news)

    def body(*refs):
        reads = refs[:nr]
        al, news = refs[nr + na:nr + 2 * na], refs[nr + 2 * na:nr + 2 * na + nn]
        ssem, rsem = refs[nr + 2 * na + nn:]
        starts, arrivals, sends = rider.build(reads, al, news, ssem, rsem)
        for cp in starts:
            cp().start()
        for cp in arrivals:
            cp().wait_recv()
        for cp in sends:
            cp().wait_send()

    outs = pl.pallas_call(
        body, name=name, in_specs=[_HBM] * (nr + na), out_specs=[_HBM] * (na + nn),
        out_shape=[jax.ShapeDtypeStruct(a.shape, a.dtype) for a in rider.aliased] + rider.news,
        input_output_aliases={nr + i: i for i in range(na)},
        scratch_shapes=[pltpu.SemaphoreType.DMA((rider.nsem,)), pltpu.SemaphoreType.DMA((rider.nsem,))],
    )(*rider.reads, *rider.aliased)
    return list(outs[:na]), list(outs[na:])


def all_reduce_small(v):
    rows, lanes = v.shape

    def body(v_ref, o_ref, slots, send_sems, recv_sems):
        x, y, c = _me()
        me = 4 * x + 2 * y + c
        slots[me] = v_ref[...]
        cps = []
        for k in range(1, N_DEV):
            peer = (x ^ (k >> 2), y ^ ((k >> 1) & 1), c ^ (k & 1))
            cps.append(_remote(v_ref, slots.at[me], send_sems, recv_sems, k - 1, peer))
            cps[-1].start()
        for k in range(1, N_DEV):
            theirs = slots.at[me ^ k]
            _remote(theirs, theirs, send_sems, recv_sems, k - 1, (x, y, c)).wait_recv()
        for cp in cps:
            cp.wait_send()
        acc = slots[0]
        for i in range(1, N_DEV):
            acc = acc + slots[i]
        o_ref[...] = acc

    return pl.pallas_call(
        body, name="all_reduce_small",
        in_specs=[pl.BlockSpec(memory_space=pltpu.VMEM)], out_specs=pl.BlockSpec(memory_space=pltpu.VMEM),
        out_shape=jax.ShapeDtypeStruct((rows, lanes), F32),
        scratch_shapes=[pltpu.VMEM((N_DEV, rows, lanes), F32), pltpu.SemaphoreType.DMA((N_DEV - 1,)), pltpu.SemaphoreType.DMA((N_DEV - 1,))],
    )(v)


def _mm_call(name, mode, a, b, a_spec, b_spec, out_shape, out_spec, grid, acc_shape, scale=1.0, resid=None, resid_spec=None, rider=None):
    nk = grid[2]
    dims = _DIMS[mode]
    has_resid = resid is not None

    def body(*refs):
        a_ref, b_ref = refs[:2]
        r_ref = refs[2] if has_resid else None
        o_ref = refs[3] if has_resid else refs[2]

        def finish(r):
            if scale != 1.0:
                r = r * scale
            if has_resid:
                r = r_ref[...].astype(F32) + r
            o_ref[...] = r.astype(o_ref.dtype)

        part = lax.dot_general(a_ref[...].astype(BF16), b_ref[...].astype(BF16), dims, preferred_element_type=F32)
        if nk == 1:
            finish(part)
            return
        acc_ref = refs[-1]
        k = pl.program_id(2)

        @pl.when(k == 0)
        def _():
            acc_ref[...] = part

        @pl.when(k > 0)
        def _():
            acc_ref[...] += part

        @pl.when(k == nk - 1)
        def _():
            finish(acc_ref[...])

    in_specs = [a_spec, b_spec] + ([resid_spec] if has_resid else [])
    args = (a, b) + ((resid,) if has_resid else ())
    (out,), rid = _call(name, body, grid, in_specs, args, [out_spec], [out_shape], [pltpu.VMEM(acc_shape, F32)] if nk > 1 else [],
                        ("parallel", "parallel", "arbitrary"), rider)
    return out, rid


MM_VMEM_BUDGET = 40 * 1024 * 1024
_TILE_OPTS = (2048, 1408, 1024, 512, 256, 128)


def _tiles(m, n, kd, a_item, b_item, o_item, r_item=0, tm=None, tn=None, tk=None):
    def opts(full, fixed, cap):
        return [fixed] if fixed else [t for t in _TILE_OPTS if t <= cap and full % t == 0] or [full]

    best = None
    for cm in opts(m, tm, 1408):
        for cn in opts(n, tn, 1408):
            for ck in opts(kd, tk, 2048):
                blocks = cm * ck * a_item + ck * cn * b_item + cm * cn * (o_item + r_item)
                casts = (cm * ck * 2 if a_item == 4 else 0) + (ck * cn * 2 if b_item == 4 else 0)
                if 2 * blocks + cm * cn * 4 + casts <= MM_VMEM_BUDGET:
                    key = (cm * cn * ck, ck)
                    if best is None or key > best[0]:
                        best = (key, (cm, cn, ck))
    assert best is not None, (m, n, kd)
    return best[1]


def _item(x):
    return jnp.dtype(x.dtype).itemsize


def mm_nn(name, a, b, *, out_dtype=F32, scale=1.0, resid=None, rider=None):
    m, kd = a.shape
    n = b.shape[1]
    tm, tn, tk = _tiles(m, n, kd, _item(a), _item(b), jnp.dtype(out_dtype).itemsize, 0 if resid is None else _item(resid))
    o_spec = pl.BlockSpec((tm, tn), lambda i, j, k: (i, j))
    return _mm_call(
        name, "nn", a, b, pl.BlockSpec((tm, tk), lambda i, j, k: (i, k)), pl.BlockSpec((tk, tn), lambda i, j, k: (k, j)),
        jax.ShapeDtypeStruct((m, n), out_dtype), o_spec, (m // tm, n // tn, kd // tk), (tm, tn), scale, resid, o_spec, rider)


def mm_nt(name, a, b, *, out_dtype=F32, scale=1.0, resid=None, rider=None):
    m, kd = a.shape
    n = b.shape[0]
    tm, tn, tk = _tiles(m, n, kd, _item(a), _item(b), jnp.dtype(out_dtype).itemsize, 0 if resid is None else _item(resid))
    o_spec = pl.BlockSpec((tm, tn), lambda i, j, k: (i, j))
    return _mm_call(
        name, "nt", a, b, pl.BlockSpec((tm, tk), lambda i, j, k: (i, k)), pl.BlockSpec((tn, tk), lambda i, j, k: (j, k)),
        jax.ShapeDtypeStruct((m, n), out_dtype), o_spec, (m // tm, n // tn, kd // tk), (tm, tn), scale, resid, o_spec, rider)


def mm_tn(name, a, b, *, out_dtype=F32, scale=1.0, rider=None):
    kd, m = a.shape
    n = b.shape[1]
    tm, tn, tk = _tiles(m, n, kd, _item(a), _item(b), jnp.dtype(out_dtype).itemsize)
    return _mm_call(
        name, "tn", a, b, pl.BlockSpec((tk, tm), lambda i, j, k: (k, i)), pl.BlockSpec((tk, tn), lambda i, j, k: (k, j)),
        jax.ShapeDtypeStruct((m, n), out_dtype), pl.BlockSpec((tm, tn), lambda i, j, k: (i, j)),
        (m // tm, n // tn, kd // tk), (tm, tn), scale, rider=rider)


def mm_nt_sharded(name, a, w, *, resid=None, rider=None):
    m = a.shape[0]
    ns, n, c = w.shape
    tm, tn, _ = _tiles(m, n, c, _item(a), _item(w), 4, 0 if resid is None else _item(resid), tk=c)
    o_spec = pl.BlockSpec((tm, tn), lambda i, j, k: (i, j))
    return _mm_call(
        name, "nt", a, w, pl.BlockSpec((tm, c), lambda i, j, k: (i, k)), pl.BlockSpec((None, tn, c), lambda i, j, k: (k, j, 0)),
        jax.ShapeDtypeStruct((m, n), F32), o_spec, (m // tm, n // tn, ns), (tm, tn), 1.0, resid, o_spec, rider)


def mm_tn_sharded(name, a, b, ns, *, rider=None):
    kd, m = a.shape
    c = b.shape[1] // ns
    tm, _, tk = _tiles(m, c, kd, _item(a), _item(b), 2, tn=c)
    return _mm_call(
        name, "tn", a, b, pl.BlockSpec((tk, tm), lambda i, j, k: (k, i)), pl.BlockSpec((tk, c), lambda i, j, k: (k, j)),
        jax.ShapeDtypeStruct((ns, m, c), BF16), pl.BlockSpec((None, tm, c), lambda i, j, k: (j, i, 0)),
        (m // tm, ns, kd // tk), (tm, c), rider=rider)


def rms_fwd(name, x, g, out_dtype, rider=None):
    r, c = x.shape
    tm = _pick(r, (512, 256, 128, 64, 8))

    def body(x_ref, g_ref, y_ref, r_ref):
        xf = x_ref[...].astype(F32)
        rstd = lax.rsqrt(jnp.mean(xf * xf, axis=-1, keepdims=True) + EPS)
        y_ref[...] = ((xf * rstd) * g_ref[...]).astype(y_ref.dtype)
        r_ref[...] = rstd

    (y, rstd), rid = _call(
        name, body, (r // tm,), [pl.BlockSpec((tm, c), lambda i: (i, 0)), pl.BlockSpec((1, c), lambda i: (0, 0))], (x, g.reshape(1, c)),
        [pl.BlockSpec((tm, c), lambda i: (i, 0)), pl.BlockSpec((tm, 1), lambda i: (i, 0))],
        [jax.ShapeDtypeStruct((r, c), out_dtype), jax.ShapeDtypeStruct((r, 1), F32)], (), ("parallel",), rider)
    return (y, rstd) if rider is None else (y, rstd, rid)


def rms_bwd(name, x, g, rstd, dy, dres=None):
    r, c = x.shape
    tm = _pick(r, (512, 256, 128, 64, 8))
    has_res = dres is not None

    def body(*refs):
        if has_res:
            x_ref, g_ref, r_ref, dy_ref, dres_ref, dx_ref, dg_ref = refs
        else:
            x_ref, g_ref, r_ref, dy_ref, dx_ref, dg_ref = refs
        xhat = x_ref[...].astype(F32) * r_ref[...]
        dyf = dy_ref[...].astype(F32)
        gdy = dyf * g_ref[...]
        dx = r_ref[...] * (gdy - xhat * jnp.mean(gdy * xhat, axis=-1, keepdims=True))
        if has_res:
            dx = dx + dres_ref[...]
        dx_ref[...] = dx

        @pl.when(pl.program_id(0) == 0)
        def _():
            dg_ref[...] = jnp.zeros_like(dg_ref)

        dg_ref[...] += jnp.sum(dyf * xhat, axis=0, keepdims=True)

    row = pl.BlockSpec((tm, c), lambda i: (i, 0))
    in_specs = [row, pl.BlockSpec((1, c), lambda i: (0, 0)), pl.BlockSpec((tm, 1), lambda i: (i, 0)), row] + ([row] if has_res else [])
    args = (x, g.reshape(1, c), rstd, dy) + ((dres,) if has_res else ())
    (dx, dg), _ = _call(name, body, (r // tm,), in_specs, args, [row, pl.BlockSpec((1, c), lambda i: (0, 0))],
                        [jax.ShapeDtypeStruct((r, c), F32), jax.ShapeDtypeStruct((1, c), F32)], (), ("arbitrary",))
    return dx, dg.reshape(c)


@functools.partial(jax.custom_vjp, nondiff_argnums=(0,))
def rms_op(name, x, g):
    return rms_fwd(name + "_fwd", x, g, F32)[0]


def _rms_op_fwd(name, x, g):
    y, rstd = rms_fwd(name + "_fwd", x, g, F32)
    return y, (x, g, rstd)


def _rms_op_bwd(name, res, dy):
    x, g, rstd = res
    return rms_bwd(name + "_bwd", x, g, rstd, dy)


rms_op.defvjp(_rms_op_fwd, _rms_op_bwd)


FFN_TM = 512


def _sigmoid(x):
    return 1.0 / (1.0 + jnp.exp(-x))


def ffn_gu(name, xn, wg, wu, rider=None):
    s, d = xn.shape
    ns, _, c = wg.shape
    tm = _pick(s, (FFN_TM, 128))

    def body(x_ref, wg_ref, wu_ref, g_ref, u_ref, h_ref):
        xb = x_ref[...]
        gv = jnp.dot(xb, wg_ref[...], preferred_element_type=F32)
        uv = jnp.dot(xb, wu_ref[...], preferred_element_type=F32)
        g_ref[...] = gv
        u_ref[...] = uv
        h_ref[...] = ((gv * _sigmoid(gv)) * uv).astype(BF16)

    w_spec = pl.BlockSpec((None, d, c), lambda j, i: (j, 0, 0))
    o_spec = pl.BlockSpec((tm, c), lambda j, i: (i, j))
    return _call(
        name, body, (ns, s // tm), [pl.BlockSpec((tm, d), lambda j, i: (i, 0)), w_spec, w_spec], (xn, wg, wu),
        [o_spec, o_spec, o_spec],
        [jax.ShapeDtypeStruct((s, ns * c), F32), jax.ShapeDtypeStruct((s, ns * c), F32), jax.ShapeDtypeStruct((s, ns * c), BF16)],
        [], ("parallel", "parallel"), rider)


def ffn_dh(name, dy, wd, g, u, ns, scale, rider=None):
    s, d = dy.shape
    f = wd.shape[0]
    c = f // ns
    tm = _pick(s, (FFN_TM, 128))

    def body(dy_ref, wd_ref, g_ref, u_ref, dg_ref, du_ref):
        dh = lax.dot_general(dy_ref[...].astype(BF16), wd_ref[...], _DIMS["nt"], preferred_element_type=F32) * scale
        gv, uv = g_ref[...], u_ref[...]
        sig = _sigmoid(gv)
        dg_ref[...] = (dh * uv * (sig * (1.0 + gv * (1.0 - sig)))).astype(BF16)
        du_ref[...] = (dh * (gv * sig)).astype(BF16)

    o_spec = pl.BlockSpec((tm, c), lambda j, i: (i, j))
    return _call(
        name, body, (ns, s // tm),
        [pl.BlockSpec((tm, d), lambda j, i: (i, 0)), pl.BlockSpec((c, d), lambda j, i: (j, 0)), o_spec, o_spec], (dy, wd, g, u),
        [o_spec, o_spec], [jax.ShapeDtypeStruct((s, f), BF16), jax.ShapeDtypeStruct((s, f), BF16)],
        [], ("parallel", "parallel"), rider)


FOX_TQ = 256


def _fox_scores(q_ref, k_ref, cq_ref, ck_ref, i, tq):
    s_len = k_ref.shape[0]
    s = lax.dot_general(q_ref[...].astype(BF16), k_ref[...].astype(BF16), _DIMS["nt"], preferred_element_type=F32)
    s = s * (HEAD_DIM ** -0.5) + cq_ref[...] - ck_ref[...]
    row = lax.broadcasted_iota(jnp.int32, (tq, s_len), 0) + i * tq
    col = lax.broadcasted_iota(jnp.int32, (tq, s_len), 1)
    return jnp.where(row >= col, s, MASK_VALUE)


def fox_fwd(q, k, v, cq, ck, rider=None):
    h, s_len, d = q.shape
    tq = min(FOX_TQ, s_len)

    def body(q_ref, k_ref, v_ref, cq_ref, ck_ref, o_ref, lse_ref):
        s = _fox_scores(q_ref, k_ref, cq_ref, ck_ref, pl.program_id(1), tq)
        m = jnp.max(s, axis=-1, keepdims=True)
        p = jnp.exp(s - m)
        l = jnp.sum(p, axis=-1, keepdims=True)
        o = jnp.dot(p.astype(BF16), v_ref[...].astype(BF16), preferred_element_type=F32)
        o_ref[...] = o / l
        lse_ref[...] = m + jnp.log(l)

    qb = pl.BlockSpec((None, tq, d), lambda hh, i: (hh, i, 0))
    kb = pl.BlockSpec((None, s_len, d), lambda hh, i: (hh, 0, 0))
    colb = pl.BlockSpec((None, tq, 1), lambda hh, i: (hh, i, 0))
    return _call(
        "fox_fwd", body, (h, s_len // tq), [qb, kb, kb, colb, pl.BlockSpec((None, 1, s_len), lambda hh, i: (hh, 0, 0))],
        (q, k, v, cq, ck), [qb, colb], [jax.ShapeDtypeStruct((h, s_len, d), F32), jax.ShapeDtypeStruct((h, s_len, 1), F32)],
        (), ("parallel", "parallel"), rider)


def fox_bwd(q, k, v, cq, ck, o, lse, do, rider=None):
    h, s_len, d = q.shape
    tq = min(FOX_TQ, s_len)
    scale = HEAD_DIM ** -0.5

    def body(q_ref, k_ref, v_ref, cq_ref, ck_ref, o_ref, lse_ref, do_ref, dq_ref, dk_ref, dv_ref, dcq_ref, dck_ref):
        i = pl.program_id(1)

        @pl.when(i == 0)
        def _():
            dk_ref[...] = jnp.zeros_like(dk_ref)
            dv_ref[...] = jnp.zeros_like(dv_ref)
            dck_ref[...] = jnp.zeros_like(dck_ref)

        s = _fox_scores(q_ref, k_ref, cq_ref, ck_ref, i, tq)
        p = jnp.exp(s - lse_ref[...])
        dof = do_ref[...]
        dob = dof.astype(BF16)
        delta = jnp.sum(dof * o_ref[...], axis=-1, keepdims=True)
        dp = lax.dot_general(dob, v_ref[...].astype(BF16), _DIMS["nt"], preferred_element_type=F32)
        ds = p * (dp - delta)
        dsb = ds.astype(BF16)
        dv_ref[...] += lax.dot_general(p.astype(BF16), dob, _DIMS["tn"], preferred_element_type=F32)
        dq_ref[...] = jnp.dot(dsb, k_ref[...].astype(BF16), preferred_element_type=F32) * scale
        dk_ref[...] += lax.dot_general(dsb, q_ref[...].astype(BF16), _DIMS["tn"], preferred_element_type=F32) * scale
        dcq_ref[...] = jnp.sum(ds, axis=-1, keepdims=True)
        dck_ref[...] -= jnp.sum(ds, axis=0, keepdims=True)

    qb = pl.BlockSpec((None, tq, d), lambda hh, i: (hh, i, 0))
    kb = pl.BlockSpec((None, s_len, d), lambda hh, i: (hh, 0, 0))
    colb = pl.BlockSpec((None, tq, 1), lambda hh, i: (hh, i, 0))
    rowb = pl.BlockSpec((None, 1, s_len), lambda hh, i: (hh, 0, 0))
    return _call(
        "fox_bwd", body, (h, s_len // tq), [qb, kb, kb, colb, rowb, qb, colb, qb], (q, k, v, cq, ck, o, lse, do),
        [qb, kb, kb, colb, rowb],
        [jax.ShapeDtypeStruct((h, s_len, d), F32)] * 3 + [jax.ShapeDtypeStruct((h, s_len, 1), F32), jax.ShapeDtypeStruct((h, 1, s_len), F32)],
        (), ("parallel", "arbitrary"), rider)


def _swa_probs(q_ref, kp_ref, kc_ref, sink_ref, n):
    g, w, d = q_ref.shape
    q = q_ref[...].reshape(g * w, d).astype(BF16)
    kw = jnp.concatenate([kp_ref[...], kc_ref[...]], axis=0).astype(BF16)
    s = lax.dot_general(q, kw, _DIMS["nt"], preferred_element_type=F32) * (HEAD_DIM ** -0.5)
    t = lax.broadcasted_iota(jnp.int32, (g * w, 2 * w), 0) & (w - 1)
    col = lax.broadcasted_iota(jnp.int32, (g * w, 2 * w), 1)
    rel = t + w - col
    valid = (rel >= 0) & (rel < w) & ((col >= w) | (n > 0))
    s = jnp.where(valid, s, MASK_VALUE)
    sink = sink_ref[...]
    m = jnp.maximum(jnp.max(s, axis=-1, keepdims=True), sink)
    p = jnp.exp(s - m)
    ps = jnp.exp(sink - m)
    linv = 1.0 / (jnp.sum(p, axis=-1, keepdims=True) + ps)
    return q, kw, p * linv, ps * linv


def _swa_specs(hk, g, s_len, d):
    w = WINDOW
    assert w & (w - 1) == 0 and s_len % w == 0
    qb = pl.BlockSpec((None, g, w, d), lambda hh, n: (hh, 0, n, 0))
    prev = pl.BlockSpec((None, w, d), lambda hh, n: (hh, jnp.maximum(n - 1, 0), 0))
    cur = pl.BlockSpec((None, w, d), lambda hh, n: (hh, n, 0))
    sb = pl.BlockSpec((None, g * w, 1), lambda hh, n: (hh, 0, 0))
    return qb, prev, cur, sb


def swa_fwd(q, k, v, sink, rider=None):
    hk, g, s_len, d = q.shape
    w = WINDOW
    qb, prev, cur, sb = _swa_specs(hk, g, s_len, d)

    def body(q_ref, kp_ref, kc_ref, vp_ref, vc_ref, sink_ref, o_ref):
        _, _, p, _ = _swa_probs(q_ref, kp_ref, kc_ref, sink_ref, pl.program_id(1))
        vw = jnp.concatenate([vp_ref[...], vc_ref[...]], axis=0).astype(BF16)
        o_ref[...] = jnp.dot(p.astype(BF16), vw, preferred_element_type=F32).reshape(g, w, d)

    (o,), rid = _call("swa_fwd", body, (hk, s_len // w), [qb, prev, cur, prev, cur, sb], (q, k, k, v, v, sink), [qb],
                      [jax.ShapeDtypeStruct((hk, g, s_len, d), F32)], (), ("parallel", "parallel"), rider)
    return o, rid


def swa_bwd(q, k, v, sink, o, do, rider=None):
    hk, g, s_len, d = q.shape
    w = WINDOW
    scale = HEAD_DIM ** -0.5
    qb, prev, cur, sb = _swa_specs(hk, g, s_len, d)

    def body(q_ref, kp_ref, kc_ref, vp_ref, vc_ref, sink_ref, o_ref, do_ref, dq_ref, dkp_ref, dkc_ref, dvp_ref, dvc_ref, dsink_ref):
        n = pl.program_id(1)

        @pl.when(n == 0)
        def _():
            dsink_ref[...] = jnp.zeros_like(dsink_ref)

        q, kw, p, ps = _swa_probs(q_ref, kp_ref, kc_ref, sink_ref, n)
        vw = jnp.concatenate([vp_ref[...], vc_ref[...]], axis=0).astype(BF16)
        dof = do_ref[...].reshape(g * w, d)
        dob = dof.astype(BF16)
        delta = jnp.sum(dof * o_ref[...].reshape(g * w, d), axis=-1, keepdims=True)
        dp = lax.dot_general(dob, vw, _DIMS["nt"], preferred_element_type=F32)
        ds = p * (dp - delta)
        dsb = ds.astype(BF16)
        dsink_ref[...] -= ps * delta
        dq_ref[...] = (jnp.dot(dsb, kw, preferred_element_type=F32) * scale).reshape(g, w, d)
        dkw = lax.dot_general(dsb, q, _DIMS["tn"], preferred_element_type=F32) * scale
        dvw = lax.dot_general(p.astype(BF16), dob, _DIMS["tn"], preferred_element_type=F32)
        dkp_ref[...] = dkw[:w]
        dkc_ref[...] = dkw[w:]
        dvp_ref[...] = dvw[:w]
        dvc_ref[...] = dvw[w:]

    kv_shape = jax.ShapeDtypeStruct((hk, s_len, d), F32)
    (dq, dkp, dkc, dvp, dvc, dsink), rid = _call(
        "swa_bwd", body, (hk, s_len // w), [qb, prev, cur, prev, cur, sb, qb, qb], (q, k, k, v, v, sink, o, do),
        [qb, cur, cur, cur, cur, sb],
        [jax.ShapeDtypeStruct((hk, g, s_len, d), F32), kv_shape, kv_shape, kv_shape, kv_shape, jax.ShapeDtypeStruct((hk, g * w, 1), F32)],
        (), ("parallel", "arbitrary"), rider)

    def shift_up(a):
        return jnp.concatenate([a[:, w:], jnp.zeros_like(a[:, :w])], axis=1)

    return (dq, dkc + shift_up(dkp), dvc + shift_up(dvp), dsink), rid


def loss_call(y, target):
    s, d = y.shape
    tm = _pick(s, (512, 256, 128))

    def body(y_ref, t_ref, l_ref, dy_ref):
        e = y_ref[...] - t_ref[...]
        dy_ref[...] = e * (1.0 / d)

        @pl.when(pl.program_id(0) == 0)
        def _():
            l_ref[...] = jnp.zeros_like(l_ref)

        l_ref[...] += jnp.sum(jnp.sum(e * e, axis=0, keepdims=True), axis=1, keepdims=True) * (0.5 / d)

    row = pl.BlockSpec((tm, d), lambda i: (i, 0))
    (l, dy), _ = _call("loss_head", body, (s // tm,), [row, row], (y, target), [pl.BlockSpec((1, 1), lambda i: (0, 0)), row],
                       [jax.ShapeDtypeStruct((1, 1), F32), jax.ShapeDtypeStruct((s, d), F32)], (), ("arbitrary",))
    return l[0, 0], dy


def _row_tile(rows, cols, itemsize):
    target = max(16, (1 << 20) // (cols * itemsize))
    for t in (1024, 512, 256, 128, 64, 32, 16):
        if t <= target and rows % t == 0:
            return t
    return rows


CAST_STEPS = 8


def cast_place(name, ws, p_idx, rider=None):
    n = len(ws)
    assert all(w.shape[0] % (16 * CAST_STEPS) == 0 for w in ws), [w.shape for w in ws]

    def body(p_ref, *refs):
        for w_ref, o_ref in zip(refs[:n], refs[n:]):
            o_ref[...] = w_ref[...].astype(BF16)

    return _call(
        name, body, (CAST_STEPS,), [pl.BlockSpec((w.shape[0] // CAST_STEPS, w.shape[1]), lambda i, pr: (i, 0)) for w in ws], tuple(ws),
        [pl.BlockSpec((None, w.shape[0] // CAST_STEPS, w.shape[1]), lambda i, pr: (pr[0], i, 0)) for w in ws],
        [jax.ShapeDtypeStruct((N_CHIPS,) + w.shape, BF16) for w in ws], (), ("parallel",), rider, prefetch=(p_idx,))


def chip_sum(name, grad, theirs, c_idx):
    ns, r, cols = grad.shape
    rh = r // 2
    tr = _row_tile(rh, cols, 2)
    nb = rh // tr

    def body(c_ref, a_ref, b_ref, o_ref):
        o_ref[...] = (a_ref[...].astype(F32) + b_ref[...].astype(F32)).astype(o_ref.dtype)

    return pl.pallas_call(
        body, name=name,
        grid_spec=pltpu.PrefetchScalarGridSpec(
            num_scalar_prefetch=1, grid=(ns, nb),
            in_specs=[pl.BlockSpec((None, tr, cols), lambda q, i, cr: (q, cr[0] * nb + i, 0)),
                      pl.BlockSpec((None, tr, cols), lambda q, i, cr: (q, i, 0))],
            out_specs=pl.BlockSpec((None, tr, cols), lambda q, i, cr: (q, i, 0))),
        out_shape=jax.ShapeDtypeStruct((ns, rh, cols), BF16),
        compiler_params=pltpu.CompilerParams(dimension_semantics=("parallel", "parallel"), vmem_limit_bytes=VMEM_LIMIT),
    )(c_idx, grad, theirs)


def owner_sum(name, sums, got, pc_idx):
    ns, rh, cols = sums.shape
    tr = _row_tile(rh, cols, 4)
    nb = rh // tr

    def body(pc_ref, a_ref, b_ref, o_ref):
        o_ref[...] = ((a_ref[...].astype(F32) + b_ref[0].astype(F32)) + b_ref[1].astype(F32)) + b_ref[2].astype(F32)

    return pl.pallas_call(
        body, name=name,
        grid_spec=pltpu.PrefetchScalarGridSpec(
            num_scalar_prefetch=1, grid=(nb,),
            in_specs=[pl.BlockSpec((None, tr, cols), lambda i, pc: (pc[0], i, 0)),
                      pl.BlockSpec((3, tr, cols), lambda i, pc: (0, i, 0))],
            out_specs=pl.BlockSpec((tr, cols), lambda i, pc: (pc[1] * nb + i, 0))),
        out_shape=jax.ShapeDtypeStruct((2 * rh, cols), F32),
        compiler_params=pltpu.CompilerParams(dimension_semantics=("parallel",), vmem_limit_bytes=VMEM_LIMIT),
    )(pc_idx, sums, got)


def adamw(name, w, g, m, v):
    r, cols = w.shape
    tr = _row_tile(r, cols, 4)
    c1 = 1.0 / (1.0 - ADAM_B1 ** ADAM_STEP)
    c2 = 1.0 / (1.0 - ADAM_B2 ** ADAM_STEP)

    def body(w_ref, g_ref, m_ref, v_ref, d_ref, nm_ref, nv_ref):
        gv = g_ref[...]
        nm = ADAM_B1 * m_ref[...] + (1.0 - ADAM_B1) * gv
        nv = ADAM_B2 * v_ref[...] + (1.0 - ADAM_B2) * (gv * gv)
        d_ref[...] = -ADAM_LR * ((nm * c1) / (jnp.sqrt(nv * c2) + ADAM_EPS) + ADAM_WD * w_ref[...])
        nm_ref[...] = nm
        nv_ref[...] = nv

    blk = pl.BlockSpec((tr, cols), lambda i: (i, 0))
    outs, _ = _call(name, body, (r // tr,), [blk] * 4, (w, g, m, v), [blk] * 3, [jax.ShapeDtypeStruct((r, cols), F32)] * 3, (), ("parallel",))
    return outs


def _rope(x, cos, sin):
    x1, x2 = x[..., : HEAD_DIM // 2], x[..., HEAD_DIM // 2:]
    return jnp.concatenate([x1 * cos - x2 * sin, x2 * cos + x1 * sin], axis=-1)


def _heads(a, nh):
    return a.reshape(a.shape[0], nh, HEAD_DIM).transpose(1, 0, 2)


def _unheads(a):
    return a.transpose(1, 0, 2).reshape(a.shape[1], a.shape[0] * HEAD_DIM)


def _win_layout(d_model):
    hf = hq = d_model // (2 * HEAD_DIM)
    hk = hq // 4
    sizes = [hf * HEAD_DIM, hf * HEAD_DIM, hf * HEAD_DIM, hf, hq * HEAD_DIM, hk * HEAD_DIM, hk * HEAD_DIM]
    return hf, hq, hk, sizes


def _pad_cols(a, n):
    return a if a.shape[1] == n else jnp.pad(a, ((0, 0), (0, n - a.shape[1])))


def _relayout_win(win_g):
    ns, d_model, cs = win_g.shape
    _, _, _, sizes = _win_layout(d_model)
    full = win_g.transpose(1, 0, 2).reshape(d_model, ns * cs)
    offs = [0]
    for sz in sizes:
        offs.append(offs[-1] + sz)
    qf, kf, vf, fl, qs, ks, vs = [full[:, offs[i]:offs[i + 1]] for i in range(7)]
    body = jnp.concatenate([qf, kf, vf, qs, ks, vs, _pad_cols(fl, 128)], axis=1)
    return _pad_cols(body, -(-body.shape[1] // 512) * 512)


def _attn_inputs(proj, sm, positions):
    s_len = proj.shape[0]
    hf, hq, hk, sizes = _win_layout(sm["norm_mix_g"].shape[0])
    grp = hq // hk
    o0 = 0
    cols = []
    for sz in (sizes[0], sizes[1], sizes[2], sizes[4], sizes[5], sizes[6], hf):
        cols.append(proj[:, o0:o0 + sz])
        o0 += sz
    q_f, k_f, v_f, q_s, k_s, v_s, f_logit = cols

    q_f = rms_op("fox_qnorm", _heads(q_f, hf).reshape(hf * s_len, HEAD_DIM), sm["fox_q_norm_g"]).reshape(hf, s_len, HEAD_DIM)
    k_f = rms_op("fox_knorm", _heads(k_f, hf).reshape(hf * s_len, HEAD_DIM), sm["fox_k_norm_g"]).reshape(hf, s_len, HEAD_DIM)
    v_f = _heads(v_f, hf)
    log_f = jax.nn.log_sigmoid(f_logit + sm["b_forget"])
    c = jnp.cumsum(log_f, axis=0).T

    inv_freq = ROPE_THETA ** (-jnp.arange(0, HEAD_DIM, 2, dtype=F32) / HEAD_DIM)
    ang = positions.astype(F32)[:, None] * inv_freq
    cos, sin = jnp.cos(ang), jnp.sin(ang)
    q_s = rms_op("swa_qnorm", _heads(q_s, hq).reshape(hq * s_len, HEAD_DIM), sm["swa_q_norm_g"]).reshape(hq, s_len, HEAD_DIM)
    k_s = rms_op("swa_knorm", _heads(k_s, hk).reshape(hk * s_len, HEAD_DIM), sm["swa_k_norm_g"]).reshape(hk, s_len, HEAD_DIM)
    q_s = _rope(q_s, cos, sin).reshape(hk, grp, s_len, HEAD_DIM)
    k_s = _rope(k_s, cos, sin)
    v_s = _heads(v_s, hk)
    sink = jnp.broadcast_to(sm["swa_sinks"].reshape(hk, grp, 1, 1), (hk, grp, WINDOW, 1)).reshape(hk, grp * WINDOW, 1)
    return (q_f, k_f, v_f, c[:, :, None], c[:, None, :]), (q_s, k_s, v_s, sink)


_BIG = ("ffn1_w_gate", "ffn1_w_up", "ffn1_w_down", "w_in", "w_out", "ffn2_w_gate", "ffn2_w_up", "ffn2_w_down")
_ROW_SHARDED = ("ffn1_w_down", "w_out", "ffn2_w_down")
_SMALL = ("norm_ffn1_g", "norm_mix_g", "b_forget", "fox_q_norm_g", "fox_k_norm_g", "swa_q_norm_g", "swa_k_norm_g", "swa_sinks",
          "out_norm_fox_g", "out_norm_swa_g", "norm_ffn2_g")
_ATTN_SMALL = ("norm_mix_g", "b_forget", "fox_q_norm_g", "fox_k_norm_g", "swa_q_norm_g", "swa_k_norm_g", "swa_sinks")
_ALL = ("norm_ffn1_g", "ffn1_w_gate", "ffn1_w_up", "ffn1_w_down", "norm_mix_g", "w_in", "b_forget", "fox_q_norm_g", "fox_k_norm_g",
        "swa_q_norm_g", "swa_k_norm_g", "swa_sinks", "out_norm_fox_g", "out_norm_swa_g", "w_out", "norm_ffn2_g", "ffn2_w_gate",
        "ffn2_w_up", "ffn2_w_down")
_LANES = 128


def _pack_small(d):
    parts = []
    for k in _SMALL:
        v = d[k].reshape(-1)
        rows = -(-v.shape[0] // _LANES)
        parts.append(jnp.pad(v, (0, rows * _LANES - v.shape[0])).reshape(rows, _LANES))
    a = jnp.concatenate(parts, axis=0)
    return jnp.pad(a, ((0, -a.shape[0] % 8), (0, 0)))


def _unpack_small(a, like):
    out, r0 = {}, 0
    for k in _SMALL:
        nvals = like[k].shape[1]
        rows = -(-nvals // _LANES)
        out[k] = a[r0:r0 + rows].reshape(-1)[:nvals].reshape(1, nvals)
        r0 += rows
    return out


def _stacked(w):
    return w.reshape(-1, w.shape[-1])


def _local_step(shards, sm, x, positions, target, p_idx, c_idx, pc_idx):
    ns = N_CHIPS
    hf, hq, hk, _ = _win_layout(x.shape[1])
    s_len = x.shape[0]
    full = {}

    def take(names, rid):
        for n, b in zip(names, rid[0]):
            bufs[n] = b

    def reduce_start(names, grads):
        return exchange_halves([g.reshape(ns, -1, g.shape[-1]) for g in grads])

    def chip_sums(names, grads, rid):
        return [chip_sum("chip_sum_" + n, g.reshape(ns, -1, g.shape[-1]), t, c_idx) for n, g, t in zip(names, grads, rid[1])]

    n1 = ["ffn1_w_gate", "ffn1_w_up", "ffn1_w_down"]
    n2 = ["ffn2_w_gate", "ffn2_w_up", "ffn2_w_down"]
    later = ["w_in", "w_out"] + n2
    placed, _ = cast_place("cast_place_ffn1", [shards[n] for n in n1], p_idx)
    bufs = dict(zip(n1, placed))
    placed, rid = cast_place("cast_place_later", [shards[n] for n in later], p_idx, rider=gather_ici([bufs[n] for n in n1[:2]]))
    bufs.update(zip(later, placed))
    take(n1[:2], rid)
    xn1, r1, rid = rms_fwd("ffn1_norm", x, sm["norm_ffn1_g"], BF16, rider=gather_d2d([bufs[n] for n in n1[:2]]))
    take(n1[:2], rid)
    (g1, u1, hid1), rid = ffn_gu("ffn1_gu", xn1, bufs["ffn1_w_gate"], bufs["ffn1_w_up"], rider=gather_ici([bufs["ffn1_w_down"]]))
    take(n1[2:], rid)
    take(n1[2:], run_step("gather_d2d_ffn1_down", gather_d2d([bufs["ffn1_w_down"]])))
    wd1 = _stacked(bufs["ffn1_w_down"])
    h1, rid = mm_nn("ffn1_down", hid1, wd1, scale=0.5, resid=x, rider=gather_ici([bufs["w_in"]]))
    take(["w_in"], rid)

    u, r_mix, rid = rms_fwd("mix_norm", h1, sm["norm_mix_g"], BF16, rider=gather_d2d([bufs["w_in"]]))
    take(["w_in"], rid)
    win_p, win_vjp = jax.vjp(_relayout_win, bufs["w_in"])
    proj, rid = mm_nn("mix_inproj", u, win_p, rider=gather_ici([bufs["w_out"]]))
    take(["w_out"], rid)
    sm_attn = {k: sm[k] for k in _ATTN_SMALL}
    (fox_in, swa_in), attn_vjp = jax.vjp(lambda pr, s: _attn_inputs(pr, s, positions), proj, sm_attn)
    (o_f, lse), rid = fox_fwd(*fox_in, rider=combine(gather_d2d([bufs["w_out"]]), gather_ici([bufs[n] for n in n2[:2]])))
    take(["w_out"] + n2[:2], rid)
    o_s, rid = swa_fwd(*swa_in, rider=gather_d2d([bufs[n] for n in n2[:2]]))
    take(n2[:2], rid)
    o_fox, o_swa = _unheads(o_f), _unheads(o_s.reshape(hq, s_len, HEAD_DIM))
    nf, r_fox = rms_fwd("out_norm_fox", o_fox, sm["out_norm_fox_g"], BF16)
    nsw, r_swa = rms_fwd("out_norm_swa", o_swa, sm["out_norm_swa_g"], BF16)
    o = jnp.concatenate([nf, nsw], axis=-1)
    wout = _stacked(bufs["w_out"])
    h2, _ = mm_nn("out_proj", o, wout, resid=h1)

    xn2, r2 = rms_fwd("ffn2_norm", h2, sm["norm_ffn2_g"], BF16)
    (g2, u2, hid2), rid = ffn_gu("ffn2_gu", xn2, bufs["ffn2_w_gate"], bufs["ffn2_w_up"], rider=gather_ici([bufs["ffn2_w_down"]]))
    take(n2[2:], rid)
    take(n2[2:], run_step("gather_d2d_ffn2_down", gather_d2d([bufs["ffn2_w_down"]])))
    wd2 = _stacked(bufs["ffn2_w_down"])
    y, _ = mm_nn("ffn2_down", hid2, wd2, scale=0.5, resid=h2)
    loss, dy = loss_call(y, target)

    (dg2, du2), _= ffn_dh("ffn2_dh", dy, wd2, g2, u2, ns, 0.5)
    dwd2, _ = mm_tn("ffn2_dwd", hid2, dy, out_dtype=BF16, scale=0.5)
    dwg2, _ = mm_tn_sharded("ffn2_dwg", xn2, dg2, ns)
    dwu2, _ = mm_tn_sharded("ffn2_dwu", xn2, du2, ns)
    gr2 = [dwg2, dwu2, dwd2]
    dxn, rid = mm_nt_sharded("ffn2_dxn_g", dg2, bufs["ffn2_w_gate"], rider=reduce_start(n2, gr2))
    sums2 = chip_sums(n2, gr2, rid)
    dxn, rid = mm_nt_sharded("ffn2_dxn_u", du2, bufs["ffn2_w_up"], resid=dxn, rider=scatter_to_owner(sums2[:1]))
    got2 = [rid[1][0]]
    dh2, dgain_ffn2 = rms_bwd("ffn2_dnorm", h2, sm["norm_ffn2_g"], r2, dxn, dres=dy)

    do, _ = mm_nt("out_do", dh2, wout)
    dwout, _ = mm_tn("out_dw", o, dh2, out_dtype=BF16)
    cf = o_fox.shape[1]
    d_fox, dgain_fox = rms_bwd("out_dnorm_fox", o_fox, sm["out_norm_fox_g"], r_fox, do[:, :cf])
    d_swa, dgain_swa = rms_bwd("out_dnorm_swa", o_swa, sm["out_norm_swa_g"], r_swa, do[:, cf:])
    swa_cts, rid = swa_bwd(*swa_in, o_s, _heads(d_swa, hq).reshape(o_s.shape), rider=scatter_to_owner(sums2[1:2]))
    got2.append(rid[1][0])
    fox_cts, rid = fox_bwd(*fox_in, o_f, lse, _heads(d_fox, hf),
                           rider=combine(scatter_to_owner(sums2[2:3]), reduce_start(["w_out"], [dwout])))
    got2.append(rid[1][0])
    sum_wout = chip_sums(["w_out"], [dwout], (None, rid[1][1:]))
    halves2 = [owner_sum("owner_sum_" + n, s, g, pc_idx) for n, s, g in zip(n2, sums2, got2)]
    dproj, dsm_attn = attn_vjp((tuple(fox_cts), tuple(swa_cts)))

    du, rid = mm_nt("mix_du", dproj, win_p, rider=join_halves(halves2))
    full.update(zip(n2, rid[0]))
    dwin_p, rid = mm_tn("mix_dwin", u, dproj, out_dtype=BF16, rider=scatter_to_owner(sum_wout))
    got_wout = rid[1]
    (dwin,) = win_vjp(dwin_p)
    dh1, dgain_mix = rms_bwd("mix_dnorm", h1, sm["norm_mix_g"], r_mix, du, dres=dh2)

    dwd1, rid = mm_tn("ffn1_dwd", hid1, dh1, out_dtype=BF16, scale=0.5, rider=reduce_start(["w_in"], [dwin]))
    sum_win = chip_sums(["w_in"], [dwin], rid)
    (dg1, du1), rid = ffn_dh("ffn1_dh", dh1, wd1, g1, u1, ns, 0.5,
                             rider=combine(scatter_to_owner(sum_win), reduce_start(n1[2:], [dwd1])))
    got_win = rid[1][:1]
    sum_down1 = chip_sums(n1[2:], [dwd1], (None, rid[1][1:]))
    dwg1, rid = mm_tn_sharded("ffn1_dwg", xn1, dg1, ns, rider=scatter_to_owner(sum_down1))
    got_down1 = rid[1]
    dwu1, rid = mm_tn_sharded("ffn1_dwu", xn1, du1, ns, rider=reduce_start(n1[:1], [dwg1]))
    sum_gate1 = chip_sums(n1[:1], [dwg1], rid)
    dxn, rid = mm_nt_sharded("ffn1_dxn_g", dg1, bufs["ffn1_w_gate"],
                             rider=combine(scatter_to_owner(sum_gate1), reduce_start(n1[1:2], [dwu1])))
    got_gate1 = rid[1][:1]
    sum_up1 = chip_sums(n1[1:2], [dwu1], (None, rid[1][1:]))
    dxn, rid = mm_nt_sharded("ffn1_dxn_u", du1, bufs["ffn1_w_up"], resid=dxn, rider=scatter_to_owner(sum_up1))
    got_up1 = rid[1]
    dx, dgain_ffn1 = rms_bwd("ffn1_dnorm", x, sm["norm_ffn1_g"], r1, dxn, dres=dh1)

    names = ["w_out", "w_in"] + n1
    halves = [owner_sum("owner_sum_" + n, s, g, pc_idx)
              for n, s, g in zip(names, sum_wout + sum_win + sum_gate1 + sum_up1 + sum_down1,
                                 got_wout + got_win + got_gate1 + got_up1 + got_down1)]
    joined, _ = run_step("join_halves_tail", join_halves(halves))
    full.update(zip(names, joined))

    g_small = dict(dsm_attn)
    g_small["norm_mix_g"] = g_small["norm_mix_g"] + dgain_mix
    g_small.update(norm_ffn1_g=dgain_ffn1, norm_ffn2_g=dgain_ffn2, out_norm_fox_g=dgain_fox, out_norm_swa_g=dgain_swa)
    return loss, dx, full, g_small


def kernel(x, positions, norm_ffn1_g, ffn1_w_gate, ffn1_w_up, ffn1_w_down, norm_mix_g, w_in, b_forget, fox_q_norm_g, fox_k_norm_g, swa_q_norm_g, swa_k_norm_g, swa_sinks, out_norm_fox_g, out_norm_swa_g, w_out, norm_ffn2_g, ffn2_w_gate, ffn2_w_up, ffn2_w_down, loss_target, m_norm_ffn1_g, m_ffn1_w_gate, m_ffn1_w_up, m_ffn1_w_down, m_norm_mix_g, m_w_in, m_b_forget, m_fox_q_norm_g, m_fox_k_norm_g, m_swa_q_norm_g, m_swa_k_norm_g, m_swa_sinks, m_out_norm_fox_g, m_out_norm_swa_g, m_w_out, m_norm_ffn2_g, m_ffn2_w_gate, m_ffn2_w_up, m_ffn2_w_down, v_norm_ffn1_g, v_ffn1_w_gate, v_ffn1_w_up, v_ffn1_w_down, v_norm_mix_g, v_w_in, v_b_forget, v_fox_q_norm_g, v_fox_k_norm_g, v_swa_q_norm_g, v_swa_k_norm_g, v_swa_sinks, v_out_norm_fox_g, v_out_norm_swa_g, v_w_out, v_norm_ffn2_g, v_ffn2_w_gate, v_ffn2_w_up, v_ffn2_w_down):
    args = dict(locals())
    w = {k: args[k] for k in _ALL}
    m = {k: args["m_" + k] for k in _ALL}
    v = {k: args["v_" + k] for k in _ALL}
    c_idx = lax.axis_index("c").astype(jnp.int32).reshape(1)
    p_idx = (2 * lax.axis_index("x") + lax.axis_index("y")).astype(jnp.int32).reshape(1)
    pc_idx = jnp.concatenate([p_idx, c_idx])

    small = {k: w[k] for k in _SMALL}
    loss, grad_x, g_shard, g_small = _local_step({k: w[k][0] for k in _BIG}, {k: w[k][0] for k in _SMALL}, x[0], positions[0],
                                                 loss_target[0], p_idx, c_idx, pc_idx)
    loss = lax.psum(loss, ("x", "y", "c"))
    g_small_sum = _unpack_small(all_reduce_small(_pack_small({k: g_small[k].reshape(1, -1) for k in _SMALL})), small)

    grad_w, delta, new_m, new_v = {}, {}, {}, {}
    for k in _BIG:
        grad_w[k] = g_shard[k][None]
        d, nm, nv = adamw("adamw_" + k, w[k][0], g_shard[k], m[k][0], v[k][0])
        delta[k], new_m[k], new_v[k] = d[None], nm[None], nv[None]
    d, nm, nv = adamw("adamw_small", _pack_small(small), _pack_small(g_small_sum), _pack_small({k: m[k] for k in _SMALL}),
                      _pack_small({k: v[k] for k in _SMALL}))
    grad_w.update(g_small_sum)
    delta.update(_unpack_small(d, small))
    new_m.update(_unpack_small(nm, small))
    new_v.update(_unpack_small(nv, small))

    return (loss, grad_x[None], *[grad_w[k] for k in _ALL], *[delta[k] for k in _ALL], *[new_m[k] for k in _ALL], *[new_v[k] for k in _ALL])
```

```python
import functools

import jax
import jax.numpy as jnp
from jax import lax
from jax.experimental import pallas as pl
from jax.experimental.pallas import tpu as pltpu

F32 = jnp.float32
BF16 = jnp.bfloat16

HEAD_DIM = 64
WINDOW = 128
ROPE_THETA = 10000.0
EPS = 1e-6
N_CHIPS = 4
N_DEV = 8

ADAM_LR = 0.001
ADAM_B1 = 0.9
ADAM_B2 = 0.999
ADAM_EPS = 1e-08
ADAM_WD = 0.01
ADAM_STEP = 10

V7X_VMEM_BYTES = 64 * 1024 * 1024
VMEM_LIMIT = V7X_VMEM_BYTES - 8 * 1024 * 1024
MASK_VALUE = -1e30

_MESH = pl.DeviceIdType.MESH
_HBM = pl.BlockSpec(memory_space=pl.ANY)
_DIMS = {"nn": (((1,), (0,)), ((), ())), "nt": (((1,), (1,)), ((), ())), "tn": (((0,), (0,)), ((), ()))}


def _pick(n, prefs):
    for p in prefs:
        if n % p == 0:
            return p
    return n


class Rider:
    def __init__(self, reads, aliased, news, nsem, build):
        self.reads, self.aliased, self.news, self.nsem, self.build = list(reads), list(aliased), list(news), nsem, build


class _Shifted:
    def __init__(self, ref, off):
        self.ref, self.off = ref, off

    @property
    def at(self):
        return self

    def __getitem__(self, k):
        return self.ref.at[k + self.off]


def combine(*riders):
    def build(reads, al, news, ssem, rsem):
        out = ([], [], [])
        r0 = a0 = n0 = s0 = 0
        for rd in riders:
            nr, na, nn = len(rd.reads), len(rd.aliased), len(rd.news)
            part = rd.build(reads[r0:r0 + nr], al[a0:a0 + na], news[n0:n0 + nn], _Shifted(ssem, s0), _Shifted(rsem, s0))
            for acc, lst in zip(out, part):
                acc.extend(lst)
            r0, a0, n0, s0 = r0 + nr, a0 + na, n0 + nn, s0 + rd.nsem
        return out

    return Rider(sum((r.reads for r in riders), []), sum((r.aliased for r in riders), []), sum((r.news for r in riders), []),
                 sum(r.nsem for r in riders), build)


def _me():
    return lax.axis_index("x"), lax.axis_index("y"), lax.axis_index("c")


def _other_chips(x, y):
    return [(1 - x, y), (x, 1 - y), (1 - x, 1 - y)]


def _half(ref, h):
    rows = ref.shape[-2] // 2
    idx = (slice(None),) * (len(ref.shape) - 2) + (pl.ds(h * rows, rows), slice(None))
    return ref.at[idx]


def _remote(src, dst, ssem, rsem, k, to):
    return pltpu.make_async_remote_copy(src_ref=src, dst_ref=dst, send_sem=ssem.at[k], recv_sem=rsem.at[k], device_id=to,
                                        device_id_type=_MESH)


def _later(*args):
    return functools.partial(_remote, *args)


def gather_ici(bufs):
    def build(reads, al, news, ssem, rsem):
        x, y, c = _me()
        p = 2 * x + y
        starts, arrivals = [], []
        for w, buf in enumerate(al):
            mine = _half(buf.at[p], c)
            for j, chip in enumerate(_other_chips(x, y)):
                landing = _half(buf.at[2 * chip[0] + chip[1]], c)
                starts.append(_later(mine, mine, ssem, rsem, 3 * w + j, (*chip, c)))
                arrivals.append(_later(landing, landing, ssem, rsem, 3 * w + j, (*chip, c)))
        return starts, arrivals, starts

    return Rider([], bufs, [], 3 * len(bufs), build)


def gather_d2d(bufs):
    def build(reads, al, news, ssem, rsem):
        x, y, c = _me()
        starts, arrivals = [], []
        for w, buf in enumerate(al):
            for j, chip in enumerate(_other_chips(x, y)):
                q = 2 * chip[0] + chip[1]
                landed, landing = _half(buf.at[q], c), _half(buf.at[q], 1 - c)
                starts.append(_later(landed, landed, ssem, rsem, 3 * w + j, (x, y, 1 - c)))
                arrivals.append(_later(landing, landing, ssem, rsem, 3 * w + j, (x, y, 1 - c)))
        return starts, arrivals, starts

    return Rider([], bufs, [], 3 * len(bufs), build)


def exchange_halves(grads):
    def build(reads, al, news, ssem, rsem):
        x, y, c = _me()
        cps = [_later(_half(g, 1 - c), t, ssem, rsem, w, (x, y, 1 - c)) for w, (g, t) in enumerate(zip(reads, news))]
        return cps, cps, cps

    return Rider(grads, [], [jax.ShapeDtypeStruct((g.shape[0], g.shape[1] // 2, g.shape[2]), g.dtype) for g in grads], len(grads), build)


def scatter_to_owner(sums):
    def build(reads, al, news, ssem, rsem):
        x, y, c = _me()
        cps = []
        for w, (s, got) in enumerate(zip(reads, news)):
            for j, chip in enumerate(_other_chips(x, y)):
                cps.append(_later(s.at[2 * chip[0] + chip[1]], got.at[j], ssem, rsem, 3 * w + j, (*chip, c)))
        return cps, cps, cps

    return Rider(sums, [], [jax.ShapeDtypeStruct((3,) + s.shape[1:], s.dtype) for s in sums], 3 * len(sums), build)


def join_halves(fulls):
    def build(reads, al, news, ssem, rsem):
        x, y, c = _me()
        starts, arrivals = [], []
        for w, f in enumerate(al):
            mine, landing = _half(f, c), _half(f, 1 - c)
            starts.append(_later(mine, mine, ssem, rsem, w, (x, y, 1 - c)))
            arrivals.append(_later(landing, landing, ssem, rsem, w, (x, y, 1 - c)))
        return starts, arrivals, starts

    return Rider([], fulls, [], len(fulls), build)


def _start_and_wait(rider, reads, al, news, ssem, rsem, first, last):
    @pl.when(first)
    def _():
        for cp in rider.build(reads, al, news, ssem, rsem)[0]:
            cp().start()

    def finish():
        @pl.when(last)
        def _():
            _, arrivals, sends = rider.build(reads, al, news, ssem, rsem)
            for cp in arrivals:
                cp().wait_recv()
            for cp in sends:
                cp().wait_send()

    return finish


def _call(name, body, grid, in_specs, args, out_specs, out_shape, scratch=(), semantics=None, rider=None, prefetch=()):
    n_pre, n_in, n_out, n_scr = len(prefetch), len(args), len(out_shape), len(scratch)
    nr, na, nn = (len(rider.reads), len(rider.aliased), len(rider.news)) if rider else (0, 0, 0)

    def wrapped(*refs):
        pre, refs = refs[:n_pre], refs[n_pre:]
        ins, reads = refs[:n_in], refs[n_in:n_in + nr]
        o0 = n_in + nr + na
        outs, al, news = refs[o0:o0 + n_out], refs[o0 + n_out:o0 + n_out + na], refs[o0 + n_out + na:o0 + n_out + na + nn]
        s0 = o0 + n_out + na + nn
        scr, (ssem, rsem) = refs[s0:s0 + n_scr], refs[s0 + n_scr:]
        first = functools.reduce(jnp.logical_and, [pl.program_id(a) == 0 for a in range(len(grid))])
        last = functools.reduce(jnp.logical_and, [pl.program_id(a) == g - 1 for a, g in enumerate(grid)])
        finish = _start_and_wait(rider, reads, al, news, ssem, rsem, first, last)
        body(*pre, *ins, *outs, *scr)
        finish()

    kernel_fn, all_in, all_out, shapes, scr = body, list(in_specs), list(out_specs), list(out_shape), list(scratch)
    operands, aliases = (*prefetch, *args), {}
    if rider:
        kernel_fn, semantics = wrapped, ("arbitrary",) * len(grid)
        all_in += [_HBM] * (nr + na)
        all_out += [_HBM] * (na + nn)
        shapes += [jax.ShapeDtypeStruct(a.shape, a.dtype) for a in rider.aliased] + rider.news
        scr += [pltpu.SemaphoreType.DMA((rider.nsem,)), pltpu.SemaphoreType.DMA((rider.nsem,))]
        operands += (*rider.reads, *rider.aliased)
        aliases = {n_pre + n_in + nr + i: n_out + i for i in range(na)}
    params = pltpu.CompilerParams(dimension_semantics=semantics, vmem_limit_bytes=VMEM_LIMIT)
    if n_pre:
        spec = pltpu.PrefetchScalarGridSpec(num_scalar_prefetch=n_pre, grid=grid, in_specs=all_in, out_specs=all_out, scratch_shapes=scr)
        outs = pl.pallas_call(kernel_fn, name=name, grid_spec=spec, out_shape=shapes, input_output_aliases=aliases, compiler_params=params)(*operands)
    else:
        outs = pl.pallas_call(kernel_fn, name=name, grid=grid, in_specs=all_in, out_specs=all_out, out_shape=shapes, scratch_shapes=scr,
                              input_output_aliases=aliases, compiler_params=params)(*operands)
    return list(outs[:n_out]), ((list(outs[n_out:n_out + na]), list(outs[n_out + na:])) if rider else None)


def run_step(name, rider):
    nr, na, nn = len(rider.reads), len(rider.aliased), len(rider.news)

    def body(*refs):
        reads = refs[:nr]
        al, news = refs[nr + na:nr + 2 * na], refs[nr + 2 * na:nr + 2 * na + nn]
        ssem, rsem = refs[nr + 2 * na + nn:]
        starts, arrivals, sends = rider.build(reads, al, news, ssem, rsem)
        for cp in starts:
            cp().start()
        for cp in arrivals:
            cp().wait_recv()
        for cp in sends:
            cp().wait_send()

    outs = pl.pallas_call(
        body, name=name, in_specs=[_HBM] * (nr + na), out_specs=[_HBM] * (na + nn),
        out_shape=[jax.ShapeDtypeStruct(a.shape, a.dtype) for a in rider.aliased] + rider.news,
        input_output_aliases={nr + i: i for i in range(na)},
        scratch_shapes=[pltpu.SemaphoreType.DMA((rider.nsem,)), pltpu.SemaphoreType.DMA((rider.nsem,))],
    )(*rider.reads, *rider.aliased)
    return list(outs[:na]), list(outs[na:])


def all_reduce_small(v):
    rows, lanes = v.shape

    def body(v_ref, o_ref, slots, send_sems, recv_sems):
        x, y, c = _me()
        me = 4 * x + 2 * y + c
        slots[me] = v_ref[...]
        cps = []
        for k in range(1, N_DEV):
            peer = (x ^ (k >> 2), y ^ ((k >> 1) & 1), c ^ (k & 1))
            cps.append(_remote(v_ref, slots.at[me], send_sems, recv_sems, k - 1, peer))
            cps[-1].start()
        for k in range(1, N_DEV):
            theirs = slots.at[me ^ k]
            _remote(theirs, theirs, send_sems, recv_sems, k - 1, (x, y, c)).wait_recv()
        for cp in cps:
            cp.wait_send()
        acc = slots[0]
        for i in range(1, N_DEV):
            acc = acc + slots[i]
        o_ref[...] = acc

    return pl.pallas_call(
        body, name="all_reduce_small",
        in_specs=[pl.BlockSpec(memory_space=pltpu.VMEM)], out_specs=pl.BlockSpec(memory_space=pltpu.VMEM),
        out_shape=jax.ShapeDtypeStruct((rows, lanes), F32),
        scratch_shapes=[pltpu.VMEM((N_DEV, rows, lanes), F32), pltpu.SemaphoreType.DMA((N_DEV - 1,)), pltpu.SemaphoreType.DMA((N_DEV - 1,))],
    )(v)


def _mm_call(name, mode, a, b, a_spec, b_spec, out_shape, out_spec, grid, acc_shape, scale=1.0, resid=None, resid_spec=None, rider=None):
    nk = grid[2]
    dims = _DIMS[mode]
    has_resid = resid is not None

    def body(*refs):
        a_ref, b_ref = refs[:2]
        r_ref = refs[2] if has_resid else None
        o_ref = refs[3] if has_resid else refs[2]

        def finish(r):
            if scale != 1.0:
                r = r * scale
            if has_resid:
                r = r_ref[...].astype(F32) + r
            o_ref[...] = r.astype(o_ref.dtype)

        part = lax.dot_general(a_ref[...].astype(BF16), b_ref[...].astype(BF16), dims, preferred_element_type=F32)
        if nk == 1:
            finish(part)
            return
        acc_ref = refs[-1]
        k = pl.program_id(2)

        @pl.when(k == 0)
        def _():
            acc_ref[...] = part

        @pl.when(k > 0)
        def _():
            acc_ref[...] += part

        @pl.when(k == nk - 1)
        def _():
            finish(acc_ref[...])

    in_specs = [a_spec, b_spec] + ([resid_spec] if has_resid else [])
    args = (a, b) + ((resid,) if has_resid else ())
    (out,), rid = _call(name, body, grid, in_specs, args, [out_spec], [out_shape], [pltpu.VMEM(acc_shape, F32)] if nk > 1 else [],
                        ("parallel", "parallel", "arbitrary"), rider)
    return out, rid


MM_VMEM_BUDGET = 40 * 1024 * 1024
_TILE_OPTS = (2048, 1408, 1024, 512, 256, 128)


def _tiles(m, n, kd, a_item, b_item, o_item, r_item=0, tm=None, tn=None, tk=None):
    def opts(full, fixed, cap):
        return [fixed] if fixed else [t for t in _TILE_OPTS if t <= cap and full % t == 0] or [full]

    best = None
    for cm in opts(m, tm, 1408):
        for cn in opts(n, tn, 1408):
            for ck in opts(kd, tk, 2048):
                blocks = cm * ck * a_item + ck * cn * b_item + cm * cn * (o_item + r_item)
                casts = (cm * ck * 2 if a_item == 4 else 0) + (ck * cn * 2 if b_item == 4 else 0)
                if 2 * blocks + cm * cn * 4 + casts <= MM_VMEM_BUDGET:
                    key = (cm * cn * ck, ck)
                    if best is None or key > best[0]:
                        best = (key, (cm, cn, ck))
    assert best is not None, (m, n, kd)
    return best[1]


def _item(x):
    return jnp.dtype(x.dtype).itemsize


def mm_nn(name, a, b, *, out_dtype=F32, scale=1.0, resid=None, rider=None):
    m, kd = a.shape
    n = b.shape[1]
    tm, tn, tk = _tiles(m, n, kd, _item(a), _item(b), jnp.dtype(out_dtype).itemsize, 0 if resid is None else _item(resid))
    o_spec = pl.BlockSpec((tm, tn), lambda i, j, k: (i, j))
    return _mm_call(
        name, "nn", a, b, pl.BlockSpec((tm, tk), lambda i, j, k: (i, k)), pl.BlockSpec((tk, tn), lambda i, j, k: (k, j)),
        jax.ShapeDtypeStruct((m, n), out_dtype), o_spec, (m // tm, n // tn, kd // tk), (tm, tn), scale, resid, o_spec, rider)


def mm_nt(name, a, b, *, out_dtype=F32, scale=1.0, resid=None, rider=None):
    m, kd = a.shape
    n = b.shape[0]
    tm, tn, tk = _tiles(m, n, kd, _item(a), _item(b), jnp.dtype(out_dtype).itemsize, 0 if resid is None else _item(resid))
    o_spec = pl.BlockSpec((tm, tn), lambda i, j, k: (i, j))
    return _mm_call(
        name, "nt", a, b, pl.BlockSpec((tm, tk), lambda i, j, k: (i, k)), pl.BlockSpec((tn, tk), lambda i, j, k: (j, k)),
        jax.ShapeDtypeStruct((m, n), out_dtype), o_spec, (m // tm, n // tn, kd // tk), (tm, tn), scale, resid, o_spec, rider)


def mm_tn(name, a, b, *, out_dtype=F32, scale=1.0, rider=None):
    kd, m = a.shape
    n = b.shape[1]
    tm, tn, tk = _tiles(m, n, kd, _item(a), _item(b), jnp.dtype(out_dtype).itemsize)
    return _mm_call(
        name, "tn", a, b, pl.BlockSpec((tk, tm), lambda i, j, k: (k, i)), pl.BlockSpec((tk, tn), lambda i, j, k: (k, j)),
        jax.ShapeDtypeStruct((m, n), out_dtype), pl.BlockSpec((tm, tn), lambda i, j, k: (i, j)),
        (m // tm, n // tn, kd // tk), (tm, tn), scale, rider=rider)


def mm_nt_sharded(name, a, w, *, resid=None, rider=None):
    m = a.shape[0]
    ns, n, c = w.shape
    tm, tn, _ = _tiles(m, n, c, _item(a), _item(w), 4, 0 if resid is None else _item(resid), tk=c)
    o_spec = pl.BlockSpec((tm, tn), lambda i, j, k: (i, j))
    return _mm_call(
        name, "nt", a, w, pl.BlockSpec((tm, c), lambda i, j, k: (i, k)), pl.BlockSpec((None, tn, c), lambda i, j, k: (k, j, 0)),
        jax.ShapeDtypeStruct((m, n), F32), o_spec, (m // tm, n // tn, ns), (tm, tn), 1.0, resid, o_spec, rider)


def mm_tn_sharded(name, a, b, ns, *, rider=None):
    kd, m = a.shape
    c = b.shape[1] // ns
    tm, _, tk = _tiles(m, c, kd, _item(a), _item(b), 2, tn=c)
    return _mm_call(
        name, "tn", a, b, pl.BlockSpec((tk, tm), lambda i, j, k: (k, i)), pl.BlockSpec((tk, c), lambda i, j, k: (k, j)),
        jax.ShapeDtypeStruct((ns, m, c), BF16), pl.BlockSpec((None, tm, c), lambda i, j, k: (j, i, 0)),
        (m // tm, ns, kd // tk), (tm, c), rider=rider)


def rms_fwd(name, x, g, out_dtype, rider=None):
    r, c = x.shape
    tm = _pick(r, (512, 256, 128, 64, 8))

    def body(x_ref, g_ref, y_ref, r_ref):
        xf = x_ref[...].astype(F32)
        rstd = lax.rsqrt(jnp.mean(xf * xf, axis=-1, keepdims=True) + EPS)
        y_ref[...] = ((xf * rstd) * g_ref[...]).astype(y_ref.dtype)
        r_ref[...] = rstd

    (y, rstd), rid = _call(
        name, body, (r // tm,), [pl.BlockSpec((tm, c), lambda i: (i, 0)), pl.BlockSpec((1, c), lambda i: (0, 0))], (x, g.reshape(1, c)),
        [pl.BlockSpec((tm, c), lambda i: (i, 0)), pl.BlockSpec((tm, 1), lambda i: (i, 0))],
        [jax.ShapeDtypeStruct((r, c), out_dtype), jax.ShapeDtypeStruct((r, 1), F32)], (), ("parallel",), rider)
    return (y, rstd) if rider is None else (y, rstd, rid)


def rms_bwd(name, x, g, rstd, dy, dres=None):
    r, c = x.shape
    tm = _pick(r, (512, 256, 128, 64, 8))
    has_res = dres is not None

    def body(*refs):
        if has_res:
            x_ref, g_ref, r_ref, dy_ref, dres_ref, dx_ref, dg_ref = refs
        else:
            x_ref, g_ref, r_ref, dy_ref, dx_ref, dg_ref = refs
        xhat = x_ref[...].astype(F32) * r_ref[...]
        dyf = dy_ref[...].astype(F32)
        gdy = dyf * g_ref[...]
        dx = r_ref[...] * (gdy - xhat * jnp.mean(gdy * xhat, axis=-1, keepdims=True))
        if has_res:
            dx = dx + dres_ref[...]
        dx_ref[...] = dx

        @pl.when(pl.program_id(0) == 0)
        def _():
            dg_ref[...] = jnp.zeros_like(dg_ref)

        dg_ref[...] += jnp.sum(dyf * xhat, axis=0, keepdims=True)

    row = pl.BlockSpec((tm, c), lambda i: (i, 0))
    in_specs = [row, pl.BlockSpec((1, c), lambda i: (0, 0)), pl.BlockSpec((tm, 1), lambda i: (i, 0)), row] + ([row] if has_res else [])
    args = (x, g.reshape(1, c), rstd, dy) + ((dres,) if has_res else ())
    (dx, dg), _ = _call(name, body, (r // tm,), in_specs, args, [row, pl.BlockSpec((1, c), lambda i: (0, 0))],
                        [jax.ShapeDtypeStruct((r, c), F32), jax.ShapeDtypeStruct((1, c), F32)], (), ("arbitrary",))
    return dx, dg.reshape(c)


_LANES = 128


def _head_mean(v):
    if v.shape[1] == HEAD_DIM:
        return jnp.mean(v, axis=-1, keepdims=True)
    low = lax.broadcasted_iota(jnp.int32, v.shape, 1) < HEAD_DIM
    lo = jnp.sum(jnp.where(low, v, 0.0), axis=-1, keepdims=True)
    hi = jnp.sum(jnp.where(low, 0.0, v), axis=-1, keepdims=True)
    return jnp.where(low, lo, hi) * (1.0 / HEAD_DIM)


def _head_groups(c):
    width = _LANES if c % _LANES == 0 else HEAD_DIM
    assert c % width == 0, c
    return width, [slice(k * width, (k + 1) * width) for k in range(c // width)]


def _head_gain(g, width):
    return jnp.tile(g.reshape(1, HEAD_DIM), (1, width // HEAD_DIM))


def head_rms_fwd(name, x, g):
    s, c = x.shape
    tm = _pick(s, (256, 128, 8))
    width, groups = _head_groups(c)

    def body(x_ref, g_ref, y_ref):
        for sl in groups:
            xs = x_ref[:, sl]
            y_ref[:, sl] = (xs * lax.rsqrt(_head_mean(xs * xs) + EPS)) * g_ref[...]

    row = pl.BlockSpec((tm, c), lambda i: (i, 0))
    (y,), _ = _call(name, body, (s // tm,), [row, pl.BlockSpec((1, width), lambda i: (0, 0))], (x, _head_gain(g, width)), [row],
                    [jax.ShapeDtypeStruct((s, c), F32)], (), ("parallel",))
    return y


def head_rms_bwd(name, x, g, dy):
    s, c = x.shape
    tm = _pick(s, (256, 128, 8))
    width, groups = _head_groups(c)

    def body(x_ref, g_ref, dy_ref, dx_ref, dg_ref):
        @pl.when(pl.program_id(0) == 0)
        def _():
            dg_ref[...] = jnp.zeros_like(dg_ref)

        for sl in groups:
            xs, dys = x_ref[:, sl], dy_ref[:, sl]
            rstd = lax.rsqrt(_head_mean(xs * xs) + EPS)
            xhat = xs * rstd
            gdy = dys * g_ref[...]
            dx_ref[:, sl] = rstd * (gdy - xhat * _head_mean(gdy * xhat))
            dg_ref[...] += jnp.sum(dys * xhat, axis=0, keepdims=True)

    row = pl.BlockSpec((tm, c), lambda i: (i, 0))
    vec = pl.BlockSpec((1, width), lambda i: (0, 0))
    (dx, dg), _ = _call(name, body, (s // tm,), [row, vec, row], (x, _head_gain(g, width), dy), [row, vec],
                        [jax.ShapeDtypeStruct((s, c), F32), jax.ShapeDtypeStruct((1, width), F32)], (), ("arbitrary",))
    return dx, jnp.sum(dg.reshape(width // HEAD_DIM, HEAD_DIM), axis=0)


@functools.partial(jax.custom_vjp, nondiff_argnums=(0,))
def head_rms(name, x, g):
    return head_rms_fwd(name + "_fwd", x, g)


def _head_rms_fwd(name, x, g):
    return head_rms_fwd(name + "_fwd", x, g), (x, g)


def _head_rms_bwd(name, res, dy):
    return head_rms_bwd(name + "_bwd", *res, dy)


head_rms.defvjp(_head_rms_fwd, _head_rms_bwd)


FFN_TM = 512


def _sigmoid(x):
    return 1.0 / (1.0 + jnp.exp(-x))


def ffn_gu(name, xn, wg, wu, rider=None):
    s, d = xn.shape
    ns, _, c = wg.shape
    tm = _pick(s, (FFN_TM, 128))

    def body(x_ref, wg_ref, wu_ref, g_ref, u_ref, h_ref):
        xb = x_ref[...]
        gv = jnp.dot(xb, wg_ref[...], preferred_element_type=F32)
        uv = jnp.dot(xb, wu_ref[...], preferred_element_type=F32)
        g_ref[...] = gv
        u_ref[...] = uv
        h_ref[...] = ((gv * _sigmoid(gv)) * uv).astype(BF16)

    w_spec = pl.BlockSpec((None, d, c), lambda j, i: (j, 0, 0))
    o_spec = pl.BlockSpec((tm, c), lambda j, i: (i, j))
    return _call(
        name, body, (ns, s // tm), [pl.BlockSpec((tm, d), lambda j, i: (i, 0)), w_spec, w_spec], (xn, wg, wu),
        [o_spec, o_spec, o_spec],
        [jax.ShapeDtypeStruct((s, ns * c), F32), jax.ShapeDtypeStruct((s, ns * c), F32), jax.ShapeDtypeStruct((s, ns * c), BF16)],
        [], ("parallel", "parallel"), rider)


def ffn_dh(name, dy, wd, g, u, ns, scale, rider=None):
    s, d = dy.shape
    f = wd.shape[0]
    c = f // ns
    tm = _pick(s, (FFN_TM, 128))

    def body(dy_ref, wd_ref, g_ref, u_ref, dg_ref, du_ref):
        dh = lax.dot_general(dy_ref[...].astype(BF16), wd_ref[...], _DIMS["nt"], preferred_element_type=F32) * scale
        gv, uv = g_ref[...], u_ref[...]
        sig = _sigmoid(gv)
        dg_ref[...] = (dh * uv * (sig * (1.0 + gv * (1.0 - sig)))).astype(BF16)
        du_ref[...] = (dh * (gv * sig)).astype(BF16)

    o_spec = pl.BlockSpec((tm, c), lambda j, i: (i, j))
    return _call(
        name, body, (ns, s // tm),
        [pl.BlockSpec((tm, d), lambda j, i: (i, 0)), pl.BlockSpec((c, d), lambda j, i: (j, 0)), o_spec, o_spec], (dy, wd, g, u),
        [o_spec, o_spec], [jax.ShapeDtypeStruct((s, f), BF16), jax.ShapeDtypeStruct((s, f), BF16)],
        [], ("parallel", "parallel"), rider)


FOX_TQ = 512


def fox_tile(s_len):
    return min(FOX_TQ, s_len)


def _fox_scores(qb, k_ref, cq, ck_ref, i, c, tq):
    kc = k_ref[pl.ds(pl.multiple_of(c * tq, tq), tq), :].astype(BF16)
    s = lax.dot_general(qb, kc, _DIMS["nt"], preferred_element_type=F32) * (HEAD_DIM ** -0.5) + cq - ck_ref[c]
    row = lax.broadcasted_iota(jnp.int32, (tq, tq), 0) + (i - c) * tq
    col = lax.broadcasted_iota(jnp.int32, (tq, tq), 1)
    return jnp.where(row >= col, s, MASK_VALUE)


def _fox_specs(h, s_len, d, tq):
    qb = pl.BlockSpec((None, tq, d), lambda hh, i: (hh, i, 0))
    kb = pl.BlockSpec((None, s_len, d), lambda hh, i: (hh, 0, 0))
    colb = pl.BlockSpec((None, tq, 1), lambda hh, i: (hh, i, 0))
    rowb = pl.BlockSpec((None, s_len // tq, 1, tq), lambda hh, i: (hh, 0, 0, 0))
    return qb, kb, colb, rowb


def fox_fwd(q, k, v, cq, ck, rider=None):
    h, s_len, d = q.shape
    tq = fox_tile(s_len)

    def body(q_ref, k_ref, v_ref, cq_ref, ck_ref, o_ref, lse_ref):
        i = pl.program_id(1)
        qb, cqv = q_ref[...].astype(BF16), cq_ref[...]

        def chunk(c, carry):
            m, l, acc = carry
            s = _fox_scores(qb, k_ref, cqv, ck_ref, i, c, tq)
            m_new = jnp.maximum(m, jnp.max(s, axis=-1, keepdims=True))
            alpha = jnp.exp(m - m_new)
            p = jnp.exp(s - m_new)
            vc = v_ref[pl.ds(pl.multiple_of(c * tq, tq), tq), :].astype(BF16)
            acc = alpha * acc + jnp.dot(p.astype(BF16), vc, preferred_element_type=F32)
            return m_new, alpha * l + jnp.sum(p, axis=-1, keepdims=True), acc

        init = (jnp.full((tq, 1), MASK_VALUE, F32), jnp.zeros((tq, 1), F32), jnp.zeros((tq, d), F32))
        m, l, acc = lax.fori_loop(0, i + 1, chunk, init)
        o_ref[...] = acc / l
        lse_ref[...] = m + jnp.log(l)

    qb, kb, colb, rowb = _fox_specs(h, s_len, d, tq)
    return _call(
        "fox_fwd", body, (h, s_len // tq), [qb, kb, kb, colb, rowb], (q, k, v, cq, ck), [qb, colb],
        [jax.ShapeDtypeStruct((h, s_len, d), F32), jax.ShapeDtypeStruct((h, s_len, 1), F32)], (), ("parallel", "parallel"), rider)


def fox_bwd(q, k, v, cq, ck, o, lse, do, rider=None):
    h, s_len, d = q.shape
    tq = fox_tile(s_len)
    scale = HEAD_DIM ** -0.5

    def body(q_ref, k_ref, v_ref, cq_ref, ck_ref, o_ref, lse_ref, do_ref, dq_ref, dk_ref, dv_ref, dcq_ref, dck_ref):
        i = pl.program_id(1)

        @pl.when(i == 0)
        def _():
            dk_ref[...] = jnp.zeros_like(dk_ref)
            dv_ref[...] = jnp.zeros_like(dv_ref)
            dck_ref[...] = jnp.zeros_like(dck_ref)

        qb, cqv, lse = q_ref[...].astype(BF16), cq_ref[...], lse_ref[...]
        dof = do_ref[...]
        dob = dof.astype(BF16)
        delta = jnp.sum(dof * o_ref[...], axis=-1, keepdims=True)

        def chunk(c, carry):
            dq, dcq = carry
            rows = pl.ds(pl.multiple_of(c * tq, tq), tq)
            p = jnp.exp(_fox_scores(qb, k_ref, cqv, ck_ref, i, c, tq) - lse)
            dp = lax.dot_general(dob, v_ref[rows, :].astype(BF16), _DIMS["nt"], preferred_element_type=F32)
            ds = p * (dp - delta)
            dsb = ds.astype(BF16)
            dv_ref[rows, :] += lax.dot_general(p.astype(BF16), dob, _DIMS["tn"], preferred_element_type=F32)
            dk_ref[rows, :] += lax.dot_general(dsb, qb, _DIMS["tn"], preferred_element_type=F32) * scale
            dck_ref[c] -= jnp.sum(ds, axis=0, keepdims=True)
            dq = dq + jnp.dot(dsb, k_ref[rows, :].astype(BF16), preferred_element_type=F32)
            return dq, dcq + jnp.sum(ds, axis=-1, keepdims=True)

        dq, dcq = lax.fori_loop(0, i + 1, chunk, (jnp.zeros((tq, d), F32), jnp.zeros((tq, 1), F32)))
        dq_ref[...] = dq * scale
        dcq_ref[...] = dcq

    qb, kb, colb, rowb = _fox_specs(h, s_len, d, tq)
    return _call(
        "fox_bwd", body, (h, s_len // tq), [qb, kb, kb, colb, rowb, qb, colb, qb], (q, k, v, cq, ck, o, lse, do),
        [qb, kb, kb, colb, rowb],
        [jax.ShapeDtypeStruct((h, s_len, d), F32)] * 3
        + [jax.ShapeDtypeStruct((h, s_len, 1), F32), jax.ShapeDtypeStruct((h, s_len // tq, 1, tq), F32)],
        (), ("parallel", "arbitrary"), rider)


def _swa_probs(q_ref, kp_ref, kc_ref, sink_ref, n):
    g, w, d = q_ref.shape
    q = q_ref[...].reshape(g * w, d).astype(BF16)
    kw = jnp.concatenate([kp_ref[...], kc_ref[...]], axis=0).astype(BF16)
    s = lax.dot_general(q, kw, _DIMS["nt"], preferred_element_type=F32) * (HEAD_DIM ** -0.5)
    t = lax.broadcasted_iota(jnp.int32, (g * w, 2 * w), 0) & (w - 1)
    col = lax.broadcasted_iota(jnp.int32, (g * w, 2 * w), 1)
    rel = t + w - col
    valid = (rel >= 0) & (rel < w) & ((col >= w) | (n > 0))
    s = jnp.where(valid, s, MASK_VALUE)
    sink = sink_ref[...]
    m = jnp.maximum(jnp.max(s, axis=-1, keepdims=True), sink)
    p = jnp.exp(s - m)
    ps = jnp.exp(sink - m)
    linv = 1.0 / (jnp.sum(p, axis=-1, keepdims=True) + ps)
    return q, kw, p * linv, ps * linv


def _swa_specs(hk, g, s_len, d):
    w = WINDOW
    assert w & (w - 1) == 0 and s_len % w == 0
    qb = pl.BlockSpec((None, g, w, d), lambda hh, n: (hh, 0, n, 0))
    prev = pl.BlockSpec((None, w, d), lambda hh, n: (hh, jnp.maximum(n - 1, 0), 0))
    cur = pl.BlockSpec((None, w, d), lambda hh, n: (hh, n, 0))
    sb = pl.BlockSpec((None, g * w, 1), lambda hh, n: (hh, 0, 0))
    return qb, prev, cur, sb


def swa_fwd(q, k, v, sink, rider=None):
    hk, g, s_len, d = q.shape
    w = WINDOW
    qb, prev, cur, sb = _swa_specs(hk, g, s_len, d)

    def body(q_ref, kp_ref, kc_ref, vp_ref, vc_ref, sink_ref, o_ref):
        _, _, p, _ = _swa_probs(q_ref, kp_ref, kc_ref, sink_ref, pl.program_id(1))
        vw = jnp.concatenate([vp_ref[...], vc_ref[...]], axis=0).astype(BF16)
        o_ref[...] = jnp.dot(p.astype(BF16), vw, preferred_element_type=F32).reshape(g, w, d)

    (o,), rid = _call("swa_fwd", body, (hk, s_len // w), [qb, prev, cur, prev, cur, sb], (q, k, k, v, v, sink), [qb],
                      [jax.ShapeDtypeStruct((hk, g, s_len, d), F32)], (), ("parallel", "parallel"), rider)
    return o, rid


def swa_bwd(q, k, v, sink, o, do, rider=None):
    hk, g, s_len, d = q.shape
    w = WINDOW
    scale = HEAD_DIM ** -0.5
    qb, prev, cur, sb = _swa_specs(hk, g, s_len, d)

    def body(q_ref, kp_ref, kc_ref, vp_ref, vc_ref, sink_ref, o_ref, do_ref, dq_ref, dkp_ref, dkc_ref, dvp_ref, dvc_ref, dsink_ref):
        n = pl.program_id(1)

        @pl.when(n == 0)
        def _():
            dsink_ref[...] = jnp.zeros_like(dsink_ref)

        q, kw, p, ps = _swa_probs(q_ref, kp_ref, kc_ref, sink_ref, n)
        vw = jnp.concatenate([vp_ref[...], vc_ref[...]], axis=0).astype(BF16)
        dof = do_ref[...].reshape(g * w, d)
        dob = dof.astype(BF16)
        delta = jnp.sum(dof * o_ref[...].reshape(g * w, d), axis=-1, keepdims=True)
        dp = lax.dot_general(dob, vw, _DIMS["nt"], preferred_element_type=F32)
        ds = p * (dp - delta)
        dsb = ds.astype(BF16)
        dsink_ref[...] -= ps * delta
        dq_ref[...] = (jnp.dot(dsb, kw, preferred_element_type=F32) * scale).reshape(g, w, d)
        dkw = lax.dot_general(dsb, q, _DIMS["tn"], preferred_element_type=F32) * scale
        dvw = lax.dot_general(p.astype(BF16), dob, _DIMS["tn"], preferred_element_type=F32)
        dkp_ref[...] = dkw[:w]
        dkc_ref[...] = dkw[w:]
        dvp_ref[...] = dvw[:w]
        dvc_ref[...] = dvw[w:]

    kv_shape = jax.ShapeDtypeStruct((hk, s_len, d), F32)
    (dq, dkp, dkc, dvp, dvc, dsink), rid = _call(
        "swa_bwd", body, (hk, s_len // w), [qb, prev, cur, prev, cur, sb, qb, qb], (q, k, k, v, v, sink, o, do),
        [qb, cur, cur, cur, cur, sb],
        [jax.ShapeDtypeStruct((hk, g, s_len, d), F32), kv_shape, kv_shape, kv_shape, kv_shape, jax.ShapeDtypeStruct((hk, g * w, 1), F32)],
        (), ("parallel", "arbitrary"), rider)

    def shift_up(a):
        return jnp.concatenate([a[:, w:], jnp.zeros_like(a[:, :w])], axis=1)

    return (dq, dkc + shift_up(dkp), dvc + shift_up(dvp), dsink), rid


def loss_call(y, target):
    s, d = y.shape
    tm = _pick(s, (512, 256, 128))

    def body(y_ref, t_ref, l_ref, dy_ref):
        e = y_ref[...] - t_ref[...]
        dy_ref[...] = e * (1.0 / d)

        @pl.when(pl.program_id(0) == 0)
        def _():
            l_ref[...] = jnp.zeros_like(l_ref)

        l_ref[...] += jnp.sum(jnp.sum(e * e, axis=0, keepdims=True), axis=1, keepdims=True) * (0.5 / d)

    row = pl.BlockSpec((tm, d), lambda i: (i, 0))
    (l, dy), _ = _call("loss_head", body, (s // tm,), [row, row], (y, target), [pl.BlockSpec((1, 1), lambda i: (0, 0)), row],
                       [jax.ShapeDtypeStruct((1, 1), F32), jax.ShapeDtypeStruct((s, d), F32)], (), ("arbitrary",))
    return l[0, 0], dy


def _row_tile(rows, cols, itemsize):
    target = max(16, (1 << 20) // (cols * itemsize))
    for t in (1024, 512, 256, 128, 64, 32, 16):
        if t <= target and rows % t == 0:
            return t
    return rows


CAST_STEPS = 8


def cast_place(name, ws, p_idx, rider=None):
    n = len(ws)
    assert all(w.shape[0] % (16 * CAST_STEPS) == 0 for w in ws), [w.shape for w in ws]

    def body(p_ref, *refs):
        for w_ref, o_ref in zip(refs[:n], refs[n:]):
            o_ref[...] = w_ref[...].astype(BF16)

    return _call(
        name, body, (CAST_STEPS,), [pl.BlockSpec((w.shape[0] // CAST_STEPS, w.shape[1]), lambda i, pr: (i, 0)) for w in ws], tuple(ws),
        [pl.BlockSpec((None, w.shape[0] // CAST_STEPS, w.shape[1]), lambda i, pr: (pr[0], i, 0)) for w in ws],
        [jax.ShapeDtypeStruct((N_CHIPS,) + w.shape, BF16) for w in ws], (), ("parallel",), rider, prefetch=(p_idx,))


def chip_sum(name, grad, theirs, c_idx):
    ns, r, cols = grad.shape
    rh = r // 2
    tr = _row_tile(rh, cols, 2)
    nb = rh // tr

    def body(c_ref, a_ref, b_ref, o_ref):
        o_ref[...] = (a_ref[...].astype(F32) + b_ref[...].astype(F32)).astype(o_ref.dtype)

    return pl.pallas_call(
        body, name=name,
        grid_spec=pltpu.PrefetchScalarGridSpec(
            num_scalar_prefetch=1, grid=(ns, nb),
            in_specs=[pl.BlockSpec((None, tr, cols), lambda q, i, cr: (q, cr[0] * nb + i, 0)),
                      pl.BlockSpec((None, tr, cols), lambda q, i, cr: (q, i, 0))],
            out_specs=pl.BlockSpec((None, tr, cols), lambda q, i, cr: (q, i, 0))),
        out_shape=jax.ShapeDtypeStruct((ns, rh, cols), BF16),
        compiler_params=pltpu.CompilerParams(dimension_semantics=("parallel", "parallel"), vmem_limit_bytes=VMEM_LIMIT),
    )(c_idx, grad, theirs)


def owner_sum(name, sums, got, pc_idx):
    ns, rh, cols = sums.shape
    tr = _row_tile(rh, cols, 4)
    nb = rh // tr

    def body(pc_ref, a_ref, b_ref, o_ref):
        o_ref[...] = ((a_ref[...].astype(F32) + b_ref[0].astype(F32)) + b_ref[1].astype(F32)) + b_ref[2].astype(F32)

    return pl.pallas_call(
        body, name=name,
        grid_spec=pltpu.PrefetchScalarGridSpec(
            num_scalar_prefetch=1, grid=(nb,),
            in_specs=[pl.BlockSpec((None, tr, cols), lambda i, pc: (pc[0], i, 0)),
                      pl.BlockSpec((3, tr, cols), lambda i, pc: (0, i, 0))],
            out_specs=pl.BlockSpec((tr, cols), lambda i, pc: (pc[1] * nb + i, 0))),
        out_shape=jax.ShapeDtypeStruct((2 * rh, cols), F32),
        compiler_params=pltpu.CompilerParams(dimension_semantics=("parallel",), vmem_limit_bytes=VMEM_LIMIT),
    )(pc_idx, sums, got)


def adamw(name, w, g, m, v, rider=None):
    r, cols = w.shape
    tr = _row_tile(r, cols, 4)
    c1 = 1.0 / (1.0 - ADAM_B1 ** ADAM_STEP)
    c2 = 1.0 / (1.0 - ADAM_B2 ** ADAM_STEP)

    def body(w_ref, g_ref, m_ref, v_ref, d_ref, nm_ref, nv_ref):
        gv = g_ref[...]
        nm = ADAM_B1 * m_ref[...] + (1.0 - ADAM_B1) * gv
        nv = ADAM_B2 * v_ref[...] + (1.0 - ADAM_B2) * (gv * gv)
        d_ref[...] = -ADAM_LR * ((nm * c1) / (jnp.sqrt(nv * c2) + ADAM_EPS) + ADAM_WD * w_ref[...])
        nm_ref[...] = nm
        nv_ref[...] = nv

    blk = pl.BlockSpec((tr, cols), lambda i: (i, 0))
    return _call(name, body, (r // tr,), [blk] * 4, (w, g, m, v), [blk] * 3, [jax.ShapeDtypeStruct((r, cols), F32)] * 3, (), ("parallel",), rider)


def _rope(x, cos, sin):
    x1, x2 = x[..., : HEAD_DIM // 2], x[..., HEAD_DIM // 2:]
    return jnp.concatenate([x1 * cos - x2 * sin, x2 * cos + x1 * sin], axis=-1)


def _heads(a, nh):
    return a.reshape(a.shape[0], nh, HEAD_DIM).transpose(1, 0, 2)


def _unheads(a):
    return a.transpose(1, 0, 2).reshape(a.shape[1], a.shape[0] * HEAD_DIM)


def _win_layout(d_model):
    hf = hq = d_model // (2 * HEAD_DIM)
    hk = hq // 4
    sizes = [hf * HEAD_DIM, hf * HEAD_DIM, hf * HEAD_DIM, hf, hq * HEAD_DIM, hk * HEAD_DIM, hk * HEAD_DIM]
    return hf, hq, hk, sizes


def _pad_cols(a, n):
    return a if a.shape[1] == n else jnp.pad(a, ((0, 0), (0, n - a.shape[1])))


def _relayout_win(win_g):
    ns, d_model, cs = win_g.shape
    _, _, _, sizes = _win_layout(d_model)
    full = win_g.transpose(1, 0, 2).reshape(d_model, ns * cs)
    offs = [0]
    for sz in sizes:
        offs.append(offs[-1] + sz)
    qf, kf, vf, fl, qs, ks, vs = [full[:, offs[i]:offs[i + 1]] for i in range(7)]
    body = jnp.concatenate([qf, kf, vf, qs, ks, vs, _pad_cols(fl, 128)], axis=1)
    return _pad_cols(body, -(-body.shape[1] // 512) * 512)


def _attn_inputs(proj, sm, positions):
    s_len = proj.shape[0]
    hf, hq, hk, sizes = _win_layout(sm["norm_mix_g"].shape[0])
    grp = hq // hk
    o0 = 0
    cols = []
    for sz in (sizes[0], sizes[1], sizes[2], sizes[4], sizes[5], sizes[6], hf):
        cols.append(proj[:, o0:o0 + sz])
        o0 += sz
    q_f, k_f, v_f, q_s, k_s, v_s, f_logit = cols

    q_f = _heads(head_rms("fox_qnorm", q_f, sm["fox_q_norm_g"]), hf)
    k_f = _heads(head_rms("fox_knorm", k_f, sm["fox_k_norm_g"]), hf)
    v_f = _heads(v_f, hf)
    log_f = jax.nn.log_sigmoid(f_logit + sm["b_forget"])
    c = jnp.cumsum(log_f, axis=0).T

    inv_freq = ROPE_THETA ** (-jnp.arange(0, HEAD_DIM, 2, dtype=F32) / HEAD_DIM)
    ang = positions.astype(F32)[:, None] * inv_freq
    cos, sin = jnp.cos(ang), jnp.sin(ang)
    q_s = _heads(head_rms("swa_qnorm", q_s, sm["swa_q_norm_g"]), hq)
    k_s = _heads(head_rms("swa_knorm", k_s, sm["swa_k_norm_g"]), hk)
    q_s = _rope(q_s, cos, sin).reshape(hk, grp, s_len, HEAD_DIM)
    k_s = _rope(k_s, cos, sin)
    v_s = _heads(v_s, hk)
    sink = jnp.broadcast_to(sm["swa_sinks"].reshape(hk, grp, 1, 1), (hk, grp, WINDOW, 1)).reshape(hk, grp * WINDOW, 1)
    tq = fox_tile(s_len)
    return (q_f, k_f, v_f, c[:, :, None], c.reshape(hf, s_len // tq, 1, tq)), (q_s, k_s, v_s, sink)


_BIG = ("ffn1_w_gate", "ffn1_w_up", "ffn1_w_down", "w_in", "w_out", "ffn2_w_gate", "ffn2_w_up", "ffn2_w_down")
_ROW_SHARDED = ("ffn1_w_down", "w_out", "ffn2_w_down")
_SMALL = ("norm_ffn1_g", "norm_mix_g", "b_forget", "fox_q_norm_g", "fox_k_norm_g", "swa_q_norm_g", "swa_k_norm_g", "swa_sinks",
          "out_norm_fox_g", "out_norm_swa_g", "norm_ffn2_g")
_ATTN_SMALL = ("norm_mix_g", "b_forget", "fox_q_norm_g", "fox_k_norm_g", "swa_q_norm_g", "swa_k_norm_g", "swa_sinks")
_ALL = ("norm_ffn1_g", "ffn1_w_gate", "ffn1_w_up", "ffn1_w_down", "norm_mix_g", "w_in", "b_forget", "fox_q_norm_g", "fox_k_norm_g",
        "swa_q_norm_g", "swa_k_norm_g", "swa_sinks", "out_norm_fox_g", "out_norm_swa_g", "w_out", "norm_ffn2_g", "ffn2_w_gate",
        "ffn2_w_up", "ffn2_w_down")


def _pack_small(d):
    parts = []
    for k in _SMALL:
        v = d[k].reshape(-1)
        rows = -(-v.shape[0] // _LANES)
        parts.append(jnp.pad(v, (0, rows * _LANES - v.shape[0])).reshape(rows, _LANES))
    a = jnp.concatenate(parts, axis=0)
    return jnp.pad(a, ((0, -a.shape[0] % 8), (0, 0)))


def _unpack_small(a, like):
    out, r0 = {}, 0
    for k in _SMALL:
        nvals = like[k].shape[1]
        rows = -(-nvals // _LANES)
        out[k] = a[r0:r0 + rows].reshape(-1)[:nvals].reshape(1, nvals)
        r0 += rows
    return out


def _stacked(w):
    return w.reshape(-1, w.shape[-1])


def _local_step(shards, sm, x, positions, target, p_idx, c_idx, pc_idx):
    ns = N_CHIPS
    hf, hq, hk, _ = _win_layout(x.shape[1])
    s_len = x.shape[0]
    full = {}

    def take(names, rid):
        for n, b in zip(names, rid[0]):
            bufs[n] = b

    def reduce_start(names, grads):
        return exchange_halves([g.reshape(ns, -1, g.shape[-1]) for g in grads])

    def chip_sums(names, grads, rid):
        return [chip_sum("chip_sum_" + n, g.reshape(ns, -1, g.shape[-1]), t, c_idx) for n, g, t in zip(names, grads, rid[1])]

    n1 = ["ffn1_w_gate", "ffn1_w_up", "ffn1_w_down"]
    n2 = ["ffn2_w_gate", "ffn2_w_up", "ffn2_w_down"]
    later = ["w_in", "w_out"] + n2
    placed, _ = cast_place("cast_place_ffn1", [shards[n] for n in n1], p_idx)
    bufs = dict(zip(n1, placed))
    placed, rid = cast_place("cast_place_later", [shards[n] for n in later], p_idx, rider=gather_ici([bufs[n] for n in n1[:2]]))
    bufs.update(zip(later, placed))
    take(n1[:2], rid)
    xn1, r1, rid = rms_fwd("ffn1_norm", x, sm["norm_ffn1_g"], BF16, rider=gather_d2d([bufs[n] for n in n1[:2]]))
    take(n1[:2], rid)
    (g1, u1, hid1), rid = ffn_gu("ffn1_gu", xn1, bufs["ffn1_w_gate"], bufs["ffn1_w_up"], rider=gather_ici([bufs["ffn1_w_down"]]))
    take(n1[2:], rid)
    take(n1[2:], run_step("gather_d2d_ffn1_down", gather_d2d([bufs["ffn1_w_down"]])))
    wd1 = _stacked(bufs["ffn1_w_down"])
    h1, rid = mm_nn("ffn1_down", hid1, wd1, scale=0.5, resid=x, rider=gather_ici([bufs["w_in"]]))
    take(["w_in"], rid)

    u, r_mix, rid = rms_fwd("mix_norm", h1, sm["norm_mix_g"], BF16, rider=gather_d2d([bufs["w_in"]]))
    take(["w_in"], rid)
    win_p, win_vjp = jax.vjp(_relayout_win, bufs["w_in"])
    proj, rid = mm_nn("mix_inproj", u, win_p, rider=gather_ici([bufs["w_out"]]))
    take(["w_out"], rid)
    sm_attn = {k: sm[k] for k in _ATTN_SMALL}
    (fox_in, swa_in), attn_vjp = jax.vjp(lambda pr, s: _attn_inputs(pr, s, positions), proj, sm_attn)
    (o_f, lse), rid = fox_fwd(*fox_in, rider=combine(gather_d2d([bufs["w_out"]]), gather_ici([bufs[n] for n in n2[:2]])))
    take(["w_out"] + n2[:2], rid)
    o_s, rid = swa_fwd(*swa_in, rider=gather_d2d([bufs[n] for n in n2[:2]]))
    take(n2[:2], rid)
    o_fox, o_swa = _unheads(o_f), _unheads(o_s.reshape(hq, s_len, HEAD_DIM))
    nf, r_fox = rms_fwd("out_norm_fox", o_fox, sm["out_norm_fox_g"], BF16)
    nsw, r_swa = rms_fwd("out_norm_swa", o_swa, sm["out_norm_swa_g"], BF16)
    o = jnp.concatenate([nf, nsw], axis=-1)
    wout = _stacked(bufs["w_out"])
    h2, _ = mm_nn("out_proj", o, wout, resid=h1)

    xn2, r2 = rms_fwd("ffn2_norm", h2, sm["norm_ffn2_g"], BF16)
    (g2, u2, hid2), rid = ffn_gu("ffn2_gu", xn2, bufs["ffn2_w_gate"], bufs["ffn2_w_up"], rider=gather_ici([bufs["ffn2_w_down"]]))
    take(n2[2:], rid)
    take(n2[2:], run_step("gather_d2d_ffn2_down", gather_d2d([bufs["ffn2_w_down"]])))
    wd2 = _stacked(bufs["ffn2_w_down"])
    y, _ = mm_nn("ffn2_down", hid2, wd2, scale=0.5, resid=h2)
    loss, dy = loss_call(y, target)

    (dg2, du2), _= ffn_dh("ffn2_dh", dy, wd2, g2, u2, ns, 0.5)
    dwd2, _ = mm_tn("ffn2_dwd", hid2, dy, out_dtype=BF16, scale=0.5)
    dwg2, _ = mm_tn_sharded("ffn2_dwg", xn2, dg2, ns)
    dwu2, _ = mm_tn_sharded("ffn2_dwu", xn2, du2, ns)
    gr2 = [dwg2, dwu2, dwd2]
    dxn, rid = mm_nt_sharded("ffn2_dxn_g", dg2, bufs["ffn2_w_gate"], rider=reduce_start(n2, gr2))
    sums2 = chip_sums(n2, gr2, rid)
    dxn, _ = mm_nt_sharded("ffn2_dxn_u", du2, bufs["ffn2_w_up"], resid=dxn)
    dh2, dgain_ffn2 = rms_bwd("ffn2_dnorm", h2, sm["norm_ffn2_g"], r2, dxn, dres=dy)

    do, _ = mm_nt("out_do", dh2, wout)
    dwout, _ = mm_tn("out_dw", o, dh2, out_dtype=BF16)
    cf = o_fox.shape[1]
    d_fox, dgain_fox = rms_bwd("out_dnorm_fox", o_fox, sm["out_norm_fox_g"], r_fox, do[:, :cf])
    d_swa, dgain_swa = rms_bwd("out_dnorm_swa", o_swa, sm["out_norm_swa_g"], r_swa, do[:, cf:])
    swa_cts, rid = swa_bwd(*swa_in, o_s, _heads(d_swa, hq).reshape(o_s.shape), rider=scatter_to_owner(sums2[:1]))
    got2 = rid[1]
    fox_cts, rid = fox_bwd(*fox_in, o_f, lse, _heads(d_fox, hf),
                           rider=combine(scatter_to_owner(sums2[1:]), reduce_start(["w_out"], [dwout])))
    got2 += rid[1][:2]
    sum_wout = chip_sums(["w_out"], [dwout], (None, rid[1][2:]))
    halves2 = [owner_sum("owner_sum_" + n, s, g, pc_idx) for n, s, g in zip(n2, sums2, got2)]
    dproj, dsm_attn = attn_vjp((tuple(fox_cts), tuple(swa_cts)))

    du, rid = mm_nt("mix_du", dproj, win_p, rider=join_halves(halves2))
    full.update(zip(n2, rid[0]))
    dwin_p, rid = mm_tn("mix_dwin", u, dproj, out_dtype=BF16, rider=scatter_to_owner(sum_wout))
    got_wout = rid[1]
    (dwin,) = win_vjp(dwin_p)
    dh1, dgain_mix = rms_bwd("mix_dnorm", h1, sm["norm_mix_g"], r_mix, du, dres=dh2)

    dwd1, rid = mm_tn("ffn1_dwd", hid1, dh1, out_dtype=BF16, scale=0.5, rider=reduce_start(["w_in"], [dwin]))
    sum_win = chip_sums(["w_in"], [dwin], rid)
    (dg1, du1), rid = ffn_dh("ffn1_dh", dh1, wd1, g1, u1, ns, 0.5,
                             rider=combine(scatter_to_owner(sum_win), reduce_start(n1[2:], [dwd1])))
    got_win = rid[1][:1]
    sum_down1 = chip_sums(n1[2:], [dwd1], (None, rid[1][1:]))
    dwg1, rid = mm_tn_sharded("ffn1_dwg", xn1, dg1, ns, rider=scatter_to_owner(sum_down1))
    got_down1 = rid[1]
    dwu1, rid = mm_tn_sharded("ffn1_dwu", xn1, du1, ns, rider=reduce_start(n1[:1], [dwg1]))
    sum_gate1 = chip_sums(n1[:1], [dwg1], rid)
    dxn, rid = mm_nt_sharded("ffn1_dxn_g", dg1, bufs["ffn1_w_gate"],
                             rider=combine(scatter_to_owner(sum_gate1), reduce_start(n1[1:2], [dwu1])))
    got_gate1 = rid[1][:1]
    sum_up1 = chip_sums(n1[1:2], [dwu1], (None, rid[1][1:]))
    dxn, rid = mm_nt_sharded("ffn1_dxn_u", du1, bufs["ffn1_w_up"], resid=dxn, rider=scatter_to_owner(sum_up1))
    got_up1 = rid[1]
    dx, dgain_ffn1 = rms_bwd("ffn1_dnorm", x, sm["norm_ffn1_g"], r1, dxn, dres=dh1)

    names = ["w_out", "w_in"] + n1
    halves = [owner_sum("owner_sum_" + n, s, g, pc_idx)
              for n, s, g in zip(names, sum_wout + sum_win + sum_gate1 + sum_up1 + sum_down1,
                                 got_wout + got_win + got_gate1 + got_up1 + got_down1)]
    g_small = dict(dsm_attn)
    g_small["norm_mix_g"] = g_small["norm_mix_g"] + dgain_mix
    g_small.update(norm_ffn1_g=dgain_ffn1, norm_ffn2_g=dgain_ffn2, out_norm_fox_g=dgain_fox, out_norm_swa_g=dgain_swa)
    return loss, dx, full, dict(zip(names, halves)), g_small


def kernel(x, positions, norm_ffn1_g, ffn1_w_gate, ffn1_w_up, ffn1_w_down, norm_mix_g, w_in, b_forget, fox_q_norm_g, fox_k_norm_g, swa_q_norm_g, swa_k_norm_g, swa_sinks, out_norm_fox_g, out_norm_swa_g, w_out, norm_ffn2_g, ffn2_w_gate, ffn2_w_up, ffn2_w_down, loss_target, m_norm_ffn1_g, m_ffn1_w_gate, m_ffn1_w_up, m_ffn1_w_down, m_norm_mix_g, m_w_in, m_b_forget, m_fox_q_norm_g, m_fox_k_norm_g, m_swa_q_norm_g, m_swa_k_norm_g, m_swa_sinks, m_out_norm_fox_g, m_out_norm_swa_g, m_w_out, m_norm_ffn2_g, m_ffn2_w_gate, m_ffn2_w_up, m_ffn2_w_down, v_norm_ffn1_g, v_ffn1_w_gate, v_ffn1_w_up, v_ffn1_w_down, v_norm_mix_g, v_w_in, v_b_forget, v_fox_q_norm_g, v_fox_k_norm_g, v_swa_q_norm_g, v_swa_k_norm_g, v_swa_sinks, v_out_norm_fox_g, v_out_norm_swa_g, v_w_out, v_norm_ffn2_g, v_ffn2_w_gate, v_ffn2_w_up, v_ffn2_w_down):
    args = dict(locals())
    w = {k: args[k] for k in _ALL}
    m = {k: args["m_" + k] for k in _ALL}
    v = {k: args["v_" + k] for k in _ALL}
    c_idx = lax.axis_index("c").astype(jnp.int32).reshape(1)
    p_idx = (2 * lax.axis_index("x") + lax.axis_index("y")).astype(jnp.int32).reshape(1)
    pc_idx = jnp.concatenate([p_idx, c_idx])

    small = {k: w[k] for k in _SMALL}
    loss, grad_x, g_shard, halves, g_small = _local_step({k: w[k][0] for k in _BIG}, {k: w[k][0] for k in _SMALL}, x[0], positions[0],
                                                         loss_target[0], p_idx, c_idx, pc_idx)
    loss = lax.psum(loss, ("x", "y", "c"))
    g_small_sum = _unpack_small(all_reduce_small(_pack_small({k: g_small[k].reshape(1, -1) for k in _SMALL})), small)

    grad_w, delta, new_m, new_v = {}, {}, {}, {}
    for k in sorted(_BIG, key=lambda n: n in halves):
        rider = join_halves(list(halves.values())) if not grad_w else None
        (d, nm, nv), rid = adamw("adamw_" + k, w[k][0], g_shard[k], m[k][0], v[k][0], rider=rider)
        if rid is not None:
            g_shard.update(zip(halves, rid[0]))
        grad_w[k] = g_shard[k][None]
        delta[k], new_m[k], new_v[k] = d[None], nm[None], nv[None]
    (d, nm, nv), _ = adamw("adamw_small", _pack_small(small), _pack_small(g_small_sum), _pack_small({k: m[k] for k in _SMALL}),
                           _pack_small({k: v[k] for k in _SMALL}))
    grad_w.update(g_small_sum)
    delta.update(_unpack_small(d, small))
    new_m.update(_unpack_small(nm, small))
    new_v.update(_unpack_small(nv, small))

    return (loss, grad_x[None], *[grad_w[k] for k in _ALL], *[delta[k] for k in _ALL], *[new_m[k] for k in _ALL], *[new_v[k] for k in _ALL])
```

```python
import functools

import jax
import jax.numpy as jnp
from jax import lax
from jax.experimental import pallas as pl
from jax.experimental.pallas import tpu as pltpu

F32 = jnp.float32
BF16 = jnp.bfloat16

HEAD_DIM = 64
WINDOW = 128
ROPE_THETA = 10000.0
EPS = 1e-6
N_CHIPS = 4
N_DEV = 8

ADAM_LR = 0.001
ADAM_B1 = 0.9
ADAM_B2 = 0.999
ADAM_EPS = 1e-08
ADAM_WD = 0.01
ADAM_STEP = 10

V7X_VMEM_BYTES = 64 * 1024 * 1024
VMEM_LIMIT = V7X_VMEM_BYTES - 8 * 1024 * 1024
MASK_VALUE = -1e30

_MESH = pl.DeviceIdType.MESH
_HBM = pl.BlockSpec(memory_space=pl.ANY)
_DIMS = {"nn": (((1,), (0,)), ((), ())), "nt": (((1,), (1,)), ((), ())), "tn": (((0,), (0,)), ((), ()))}


def _pick(n, prefs):
    for p in prefs:
        if n % p == 0:
            return p
    return n


class Rider:
    def __init__(self, reads, aliased, news, nsem, build):
        self.reads, self.aliased, self.news, self.nsem, self.build = list(reads), list(aliased), list(news), nsem, build


class _Shifted:
    def __init__(self, ref, off):
        self.ref, self.off = ref, off

    @property
    def at(self):
        return self

    def __getitem__(self, k):
        return self.ref.at[k + self.off]


def combine(*riders):
    def build(reads, al, news, ssem, rsem):
        out = ([], [], [])
        r0 = a0 = n0 = s0 = 0
        for rd in riders:
            nr, na, nn = len(rd.reads), len(rd.aliased), len(rd.news)
            part = rd.build(reads[r0:r0 + nr], al[a0:a0 + na], news[n0:n0 + nn], _Shifted(ssem, s0), _Shifted(rsem, s0))
            for acc, lst in zip(out, part):
                acc.extend(lst)
            r0, a0, n0, s0 = r0 + nr, a0 + na, n0 + nn, s0 + rd.nsem
        return out

    return Rider(sum((r.reads for r in riders), []), sum((r.aliased for r in riders), []), sum((r.news for r in riders), []),
                 sum(r.nsem for r in riders), build)


def _me():
    return lax.axis_index("x"), lax.axis_index("y"), lax.axis_index("c")


def _other_chips(x, y):
    return [(1 - x, y), (x, 1 - y), (1 - x, 1 - y)]


WHOLE = (0, 1, 1)


def _rows(ref, start, rows, part=WHOLE):
    k0, k1, n = part
    assert rows % n == 0, (rows, part)
    idx = (slice(None),) * (len(ref.shape) - 2) + (pl.ds(start + k0 * (rows // n), (k1 - k0) * (rows // n)), slice(None))
    return ref.at[idx]


def _half(ref, h, part=WHOLE):
    rows = ref.shape[-2] // 2
    return _rows(ref, h * rows, rows, part)


def _remote(src, dst, ssem, rsem, k, to):
    return pltpu.make_async_remote_copy(src_ref=src, dst_ref=dst, send_sem=ssem.at[k], recv_sem=rsem.at[k], device_id=to,
                                        device_id_type=_MESH)


def _later(*args):
    return functools.partial(_remote, *args)


def gather(bufs, jobs):
    def build(reads, al, news, ssem, rsem):
        x, y, c = _me()
        p = 2 * x + y
        starts, arrivals = [], []
        for n, (b, kind, part) in enumerate(jobs):
            for j, chip in enumerate(_other_chips(x, y)):
                q = 2 * chip[0] + chip[1]
                if kind == "ici":
                    src, landing, to = _half(al[b].at[p], c, part), _half(al[b].at[q], c, part), (*chip, c)
                else:
                    src, landing, to = _half(al[b].at[q], c, part), _half(al[b].at[q], 1 - c, part), (x, y, 1 - c)
                starts.append(_later(src, src, ssem, rsem, 3 * n + j, to))
                arrivals.append(_later(landing, landing, ssem, rsem, 3 * n + j, to))
        return starts, arrivals, starts

    return Rider([], bufs, [], 3 * len(jobs), build)


def exchange_halves(grads):
    def build(reads, al, news, ssem, rsem):
        x, y, c = _me()
        cps = [_later(_half(g, 1 - c), t, ssem, rsem, w, (x, y, 1 - c)) for w, (g, t) in enumerate(zip(reads, news))]
        return cps, cps, cps

    return Rider(grads, [], [jax.ShapeDtypeStruct((g.shape[0], g.shape[1] // 2, g.shape[2]), g.dtype) for g in grads], len(grads), build)


def scatter_to_owner(sums, gots=None, part=WHOLE):
    def build(reads, al, news, ssem, rsem):
        x, y, c = _me()
        cps = []
        for w, (s, got) in enumerate(zip(reads, al or news)):
            rows = s.shape[-2]
            for j, chip in enumerate(_other_chips(x, y)):
                cps.append(_later(_rows(s.at[2 * chip[0] + chip[1]], 0, rows, part), _rows(got.at[j], 0, rows, part), ssem, rsem,
                                  3 * w + j, (*chip, c)))
        return cps, cps, cps

    news = [] if gots else [jax.ShapeDtypeStruct((3,) + s.shape[1:], s.dtype) for s in sums]
    return Rider(sums, gots or [], news, 3 * len(sums), build)


def join_halves(fulls):
    def build(reads, al, news, ssem, rsem):
        x, y, c = _me()
        starts, arrivals = [], []
        for w, f in enumerate(al):
            mine, landing = _half(f, c), _half(f, 1 - c)
            starts.append(_later(mine, mine, ssem, rsem, w, (x, y, 1 - c)))
            arrivals.append(_later(landing, landing, ssem, rsem, w, (x, y, 1 - c)))
        return starts, arrivals, starts

    return Rider([], fulls, [], len(fulls), build)


def _start_and_wait(rider, reads, al, news, ssem, rsem, first, last):
    @pl.when(first)
    def _():
        for cp in rider.build(reads, al, news, ssem, rsem)[0]:
            cp().start()

    def finish():
        @pl.when(last)
        def _():
            _, arrivals, sends = rider.build(reads, al, news, ssem, rsem)
            for cp in arrivals:
                cp().wait_recv()
            for cp in sends:
                cp().wait_send()

    return finish


def _call(name, body, grid, in_specs, args, out_specs, out_shape, scratch=(), semantics=None, rider=None, prefetch=()):
    n_pre, n_in, n_out, n_scr = len(prefetch), len(args), len(out_shape), len(scratch)
    nr, na, nn = (len(rider.reads), len(rider.aliased), len(rider.news)) if rider else (0, 0, 0)

    def wrapped(*refs):
        pre, refs = refs[:n_pre], refs[n_pre:]
        ins, reads = refs[:n_in], refs[n_in:n_in + nr]
        o0 = n_in + nr + na
        outs, al, news = refs[o0:o0 + n_out], refs[o0 + n_out:o0 + n_out + na], refs[o0 + n_out + na:o0 + n_out + na + nn]
        s0 = o0 + n_out + na + nn
        scr, (ssem, rsem) = refs[s0:s0 + n_scr], refs[s0 + n_scr:]
        first = functools.reduce(jnp.logical_and, [pl.program_id(a) == 0 for a in range(len(grid))])
        last = functools.reduce(jnp.logical_and, [pl.program_id(a) == g - 1 for a, g in enumerate(grid)])
        finish = _start_and_wait(rider, reads, al, news, ssem, rsem, first, last)
        body(*pre, *ins, *outs, *scr)
        finish()

    kernel_fn, all_in, all_out, shapes, scr = body, list(in_specs), list(out_specs), list(out_shape), list(scratch)
    operands, aliases = (*prefetch, *args), {}
    if rider:
        kernel_fn, semantics = wrapped, ("arbitrary",) * len(grid)
        all_in += [_HBM] * (nr + na)
        all_out += [_HBM] * (na + nn)
        shapes += [jax.ShapeDtypeStruct(a.shape, a.dtype) for a in rider.aliased] + rider.news
        scr += [pltpu.SemaphoreType.DMA((rider.nsem,)), pltpu.SemaphoreType.DMA((rider.nsem,))]
        operands += (*rider.reads, *rider.aliased)
        aliases = {n_pre + n_in + nr + i: n_out + i for i in range(na)}
    params = pltpu.CompilerParams(dimension_semantics=semantics, vmem_limit_bytes=VMEM_LIMIT)
    if n_pre:
        spec = pltpu.PrefetchScalarGridSpec(num_scalar_prefetch=n_pre, grid=grid, in_specs=all_in, out_specs=all_out, scratch_shapes=scr)
        outs = pl.pallas_call(kernel_fn, name=name, grid_spec=spec, out_shape=shapes, input_output_aliases=aliases, compiler_params=params)(*operands)
    else:
        outs = pl.pallas_call(kernel_fn, name=name, grid=grid, in_specs=all_in, out_specs=all_out, out_shape=shapes, scratch_shapes=scr,
                              input_output_aliases=aliases, compiler_params=params)(*operands)
    return list(outs[:n_out]), ((list(outs[n_out:n_out + na]), list(outs[n_out + na:])) if rider else None)


def run_step(name, rider):
    nr, na, nn = len(rider.reads), len(rider.aliased), len(rider.news)

    def body(*refs):
        reads = refs[:nr]
        al, news = refs[nr + na:nr + 2 * na], refs[nr + 2 * na:nr + 2 * na + nn]
        ssem, rsem = refs[nr + 2 * na + nn:]
        starts, arrivals, sends = rider.build(reads, al, news, ssem, rsem)
        for cp in starts:
            cp().start()
        for cp in arrivals:
            cp().wait_recv()
        for cp in sends:
            cp().wait_send()

    outs = pl.pallas_call(
        body, name=name, in_specs=[_HBM] * (nr + na), out_specs=[_HBM] * (na + nn),
        out_shape=[jax.ShapeDtypeStruct(a.shape, a.dtype) for a in rider.aliased] + rider.news,
        input_output_aliases={nr + i: i for i in range(na)},
        scratch_shapes=[pltpu.SemaphoreType.DMA((rider.nsem,)), pltpu.SemaphoreType.DMA((rider.nsem,))],
    )(*rider.reads, *rider.aliased)
    return list(outs[:na]), list(outs[na:])


def all_reduce_small(v):
    rows, lanes = v.shape

    def body(v_ref, o_ref, slots, send_sems, recv_sems):
        x, y, c = _me()
        me = 4 * x + 2 * y + c
        slots[me] = v_ref[...]
        cps = []
        for k in range(1, N_DEV):
            peer = (x ^ (k >> 2), y ^ ((k >> 1) & 1), c ^ (k & 1))
            cps.append(_remote(v_ref, slots.at[me], send_sems, recv_sems, k - 1, peer))
            cps[-1].start()
        for k in range(1, N_DEV):
            theirs = slots.at[me ^ k]
            _remote(theirs, theirs, send_sems, recv_sems, k - 1, (x, y, c)).wait_recv()
        for cp in cps:
            cp.wait_send()
        acc = slots[0]
        for i in range(1, N_DEV):
            acc = acc + slots[i]
        o_ref[...] = acc

    return pl.pallas_call(
        body, name="all_reduce_small",
        in_specs=[pl.BlockSpec(memory_space=pltpu.VMEM)], out_specs=pl.BlockSpec(memory_space=pltpu.VMEM),
        out_shape=jax.ShapeDtypeStruct((rows, lanes), F32),
        scratch_shapes=[pltpu.VMEM((N_DEV, rows, lanes), F32), pltpu.SemaphoreType.DMA((N_DEV - 1,)), pltpu.SemaphoreType.DMA((N_DEV - 1,))],
    )(v)


def _mm_call(name, mode, a, b, a_spec, b_spec, out_shape, out_spec, grid, acc_shape, scale=1.0, resid=None, resid_spec=None, rider=None):
    nk = grid[2]
    dims = _DIMS[mode]
    has_resid = resid is not None

    def body(*refs):
        a_ref, b_ref = refs[:2]
        r_ref = refs[2] if has_resid else None
        o_ref = refs[3] if has_resid else refs[2]

        def finish(r):
            if scale != 1.0:
                r = r * scale
            if has_resid:
                r = r_ref[...].astype(F32) + r
            o_ref[...] = r.astype(o_ref.dtype)

        part = lax.dot_general(a_ref[...].astype(BF16), b_ref[...].astype(BF16), dims, preferred_element_type=F32)
        if nk == 1:
            finish(part)
            return
        acc_ref = refs[-1]
        k = pl.program_id(2)

        @pl.when(k == 0)
        def _():
            acc_ref[...] = part

        @pl.when(k > 0)
        def _():
            acc_ref[...] += part

        @pl.when(k == nk - 1)
        def _():
            finish(acc_ref[...])

    in_specs = [a_spec, b_spec] + ([resid_spec] if has_resid else [])
    args = (a, b) + ((resid,) if has_resid else ())
    (out,), rid = _call(name, body, grid, in_specs, args, [out_spec], [out_shape], [pltpu.VMEM(acc_shape, F32)] if nk > 1 else [],
                        ("parallel", "parallel", "arbitrary"), rider)
    return out, rid


MM_VMEM_BUDGET = 40 * 1024 * 1024
_TILE_OPTS = (2048, 1408, 1024, 512, 256, 128)


def _tiles(m, n, kd, a_item, b_item, o_item, r_item=0, tm=None, tn=None, tk=None):
    def opts(full, fixed, cap):
        return [fixed] if fixed else [t for t in _TILE_OPTS if t <= cap and full % t == 0] or [full]

    best = None
    for cm in opts(m, tm, 1408):
        for cn in opts(n, tn, 1408):
            for ck in opts(kd, tk, 2048):
                blocks = cm * ck * a_item + ck * cn * b_item + cm * cn * (o_item + r_item)
                casts = (cm * ck * 2 if a_item == 4 else 0) + (ck * cn * 2 if b_item == 4 else 0)
                if 2 * blocks + cm * cn * 4 + casts <= MM_VMEM_BUDGET:
                    key = (cm * cn * ck, ck)
                    if best is None or key > best[0]:
                        best = (key, (cm, cn, ck))
    assert best is not None, (m, n, kd)
    return best[1]


def _item(x):
    return jnp.dtype(x.dtype).itemsize


def mm_nn(name, a, b, *, out_dtype=F32, scale=1.0, resid=None, rider=None):
    m, kd = a.shape
    n = b.shape[1]
    tm, tn, tk = _tiles(m, n, kd, _item(a), _item(b), jnp.dtype(out_dtype).itemsize, 0 if resid is None else _item(resid))
    o_spec = pl.BlockSpec((tm, tn), lambda i, j, k: (i, j))
    return _mm_call(
        name, "nn", a, b, pl.BlockSpec((tm, tk), lambda i, j, k: (i, k)), pl.BlockSpec((tk, tn), lambda i, j, k: (k, j)),
        jax.ShapeDtypeStruct((m, n), out_dtype), o_spec, (m // tm, n // tn, kd // tk), (tm, tn), scale, resid, o_spec, rider)


def mm_nt(name, a, b, *, out_dtype=F32, scale=1.0, resid=None, rider=None):
    m, kd = a.shape
    n = b.shape[0]
    tm, tn, tk = _tiles(m, n, kd, _item(a), _item(b), jnp.dtype(out_dtype).itemsize, 0 if resid is None else _item(resid))
    o_spec = pl.BlockSpec((tm, tn), lambda i, j, k: (i, j))
    return _mm_call(
        name, "nt", a, b, pl.BlockSpec((tm, tk), lambda i, j, k: (i, k)), pl.BlockSpec((tn, tk), lambda i, j, k: (j, k)),
        jax.ShapeDtypeStruct((m, n), out_dtype), o_spec, (m // tm, n // tn, kd // tk), (tm, tn), scale, resid, o_spec, rider)


def mm_tn(name, a, b, *, out_dtype=F32, scale=1.0, rider=None):
    kd, m = a.shape
    n = b.shape[1]
    tm, tn, tk = _tiles(m, n, kd, _item(a), _item(b), jnp.dtype(out_dtype).itemsize)
    return _mm_call(
        name, "tn", a, b, pl.BlockSpec((tk, tm), lambda i, j, k: (k, i)), pl.BlockSpec((tk, tn), lambda i, j, k: (k, j)),
        jax.ShapeDtypeStruct((m, n), out_dtype), pl.BlockSpec((tm, tn), lambda i, j, k: (i, j)),
        (m // tm, n // tn, kd // tk), (tm, tn), scale, rider=rider)


def mm_nt_sharded(name, a, w, *, resid=None, rider=None):
    m = a.shape[0]
    ns, n, c = w.shape
    tm, tn, _ = _tiles(m, n, c, _item(a), _item(w), 4, 0 if resid is None else _item(resid), tk=c)
    o_spec = pl.BlockSpec((tm, tn), lambda i, j, k: (i, j))
    return _mm_call(
        name, "nt", a, w, pl.BlockSpec((tm, c), lambda i, j, k: (i, k)), pl.BlockSpec((None, tn, c), lambda i, j, k: (k, j, 0)),
        jax.ShapeDtypeStruct((m, n), F32), o_spec, (m // tm, n // tn, ns), (tm, tn), 1.0, resid, o_spec, rider)


def mm_tn_sharded(name, a, b, ns, *, rider=None):
    kd, m = a.shape
    c = b.shape[1] // ns
    tm, _, tk = _tiles(m, c, kd, _item(a), _item(b), 2, tn=c)
    return _mm_call(
        name, "tn", a, b, pl.BlockSpec((tk, tm), lambda i, j, k: (k, i)), pl.BlockSpec((tk, c), lambda i, j, k: (k, j)),
        jax.ShapeDtypeStruct((ns, m, c), BF16), pl.BlockSpec((None, tm, c), lambda i, j, k: (j, i, 0)),
        (m // tm, ns, kd // tk), (tm, c), rider=rider)


def rms_fwd(name, x, g, out_dtype, rider=None):
    r, c = x.shape
    tm = _pick(r, (512, 256, 128, 64, 8))

    def body(x_ref, g_ref, y_ref, r_ref):
        xf = x_ref[...].astype(F32)
        rstd = lax.rsqrt(jnp.mean(xf * xf, axis=-1, keepdims=True) + EPS)
        y_ref[...] = ((xf * rstd) * g_ref[...]).astype(y_ref.dtype)
        r_ref[...] = rstd

    (y, rstd), rid = _call(
        name, body, (r // tm,), [pl.BlockSpec((tm, c), lambda i: (i, 0)), pl.BlockSpec((1, c), lambda i: (0, 0))], (x, g.reshape(1, c)),
        [pl.BlockSpec((tm, c), lambda i: (i, 0)), pl.BlockSpec((tm, 1), lambda i: (i, 0))],
        [jax.ShapeDtypeStruct((r, c), out_dtype), jax.ShapeDtypeStruct((r, 1), F32)], (), ("parallel",), rider)
    return (y, rstd) if rider is None else (y, rstd, rid)


def rms_bwd(name, x, g, rstd, dy, dres=None):
    r, c = x.shape
    tm = _pick(r, (512, 256, 128, 64, 8))
    has_res = dres is not None

    def body(*refs):
        if has_res:
            x_ref, g_ref, r_ref, dy_ref, dres_ref, dx_ref, dg_ref = refs
        else:
            x_ref, g_ref, r_ref, dy_ref, dx_ref, dg_ref = refs
        xhat = x_ref[...].astype(F32) * r_ref[...]
        dyf = dy_ref[...].astype(F32)
        gdy = dyf * g_ref[...]
        dx = r_ref[...] * (gdy - xhat * jnp.mean(gdy * xhat, axis=-1, keepdims=True))
        if has_res:
            dx = dx + dres_ref[...]
        dx_ref[...] = dx

        @pl.when(pl.program_id(0) == 0)
        def _():
            dg_ref[...] = jnp.zeros_like(dg_ref)

        dg_ref[...] += jnp.sum(dyf * xhat, axis=0, keepdims=True)

    row = pl.BlockSpec((tm, c), lambda i: (i, 0))
    in_specs = [row, pl.BlockSpec((1, c), lambda i: (0, 0)), pl.BlockSpec((tm, 1), lambda i: (i, 0)), row] + ([row] if has_res else [])
    args = (x, g.reshape(1, c), rstd, dy) + ((dres,) if has_res else ())
    (dx, dg), _ = _call(name, body, (r // tm,), in_specs, args, [row, pl.BlockSpec((1, c), lambda i: (0, 0))],
                        [jax.ShapeDtypeStruct((r, c), F32), jax.ShapeDtypeStruct((1, c), F32)], (), ("arbitrary",))
    return dx, dg.reshape(c)


_LANES = 128


def _head_mean(v):
    if v.shape[1] == HEAD_DIM:
        return jnp.mean(v, axis=-1, keepdims=True)
    low = lax.broadcasted_iota(jnp.int32, v.shape, 1) < HEAD_DIM
    lo = jnp.sum(jnp.where(low, v, 0.0), axis=-1, keepdims=True)
    hi = jnp.sum(jnp.where(low, 0.0, v), axis=-1, keepdims=True)
    return jnp.where(low, lo, hi) * (1.0 / HEAD_DIM)


def _head_groups(c):
    width = _LANES if c % _LANES == 0 else HEAD_DIM
    assert c % width == 0, c
    return width, [slice(k * width, (k + 1) * width) for k in range(c // width)]


def _head_gain(g, width):
    return jnp.tile(g.reshape(1, HEAD_DIM), (1, width // HEAD_DIM))


def head_rms_fwd(name, x, g):
    s, c = x.shape
    tm = _pick(s, (256, 128, 8))
    width, groups = _head_groups(c)

    def body(x_ref, g_ref, y_ref):
        for sl in groups:
            xs = x_ref[:, sl]
            y_ref[:, sl] = (xs * lax.rsqrt(_head_mean(xs * xs) + EPS)) * g_ref[...]

    row = pl.BlockSpec((tm, c), lambda i: (i, 0))
    (y,), _ = _call(name, body, (s // tm,), [row, pl.BlockSpec((1, width), lambda i: (0, 0))], (x, _head_gain(g, width)), [row],
                    [jax.ShapeDtypeStruct((s, c), F32)], (), ("parallel",))
    return y


def head_rms_bwd(name, x, g, dy):
    s, c = x.shape
    tm = _pick(s, (256, 128, 8))
    width, groups = _head_groups(c)

    def body(x_ref, g_ref, dy_ref, dx_ref, dg_ref):
        @pl.when(pl.program_id(0) == 0)
        def _():
            dg_ref[...] = jnp.zeros_like(dg_ref)

        for sl in groups:
            xs, dys = x_ref[:, sl], dy_ref[:, sl]
            rstd = lax.rsqrt(_head_mean(xs * xs) + EPS)
            xhat = xs * rstd
            gdy = dys * g_ref[...]
            dx_ref[:, sl] = rstd * (gdy - xhat * _head_mean(gdy * xhat))
            dg_ref[...] += jnp.sum(dys * xhat, axis=0, keepdims=True)

    row = pl.BlockSpec((tm, c), lambda i: (i, 0))
    vec = pl.BlockSpec((1, width), lambda i: (0, 0))
    (dx, dg), _ = _call(name, body, (s // tm,), [row, vec, row], (x, _head_gain(g, width), dy), [row, vec],
                        [jax.ShapeDtypeStruct((s, c), F32), jax.ShapeDtypeStruct((1, width), F32)], (), ("arbitrary",))
    return dx, jnp.sum(dg.reshape(width // HEAD_DIM, HEAD_DIM), axis=0)


@functools.partial(jax.custom_vjp, nondiff_argnums=(0,))
def head_rms(name, x, g):
    return head_rms_fwd(name + "_fwd", x, g)


def _head_rms_fwd(name, x, g):
    return head_rms_fwd(name + "_fwd", x, g), (x, g)


def _head_rms_bwd(name, res, dy):
    return head_rms_bwd(name + "_bwd", *res, dy)


head_rms.defvjp(_head_rms_fwd, _head_rms_bwd)


FFN_TM = 512


def _sigmoid(x):
    return 1.0 / (1.0 + jnp.exp(-x))


def ffn_gu(name, xn, wg, wu, rider=None):
    s, d = xn.shape
    ns, _, c = wg.shape
    tm = _pick(s, (FFN_TM, 128))

    def body(x_ref, wg_ref, wu_ref, g_ref, u_ref, h_ref):
        xb = x_ref[...]
        gv = jnp.dot(xb, wg_ref[...], preferred_element_type=F32)
        uv = jnp.dot(xb, wu_ref[...], preferred_element_type=F32)
        g_ref[...] = gv
        u_ref[...] = uv
        h_ref[...] = ((gv * _sigmoid(gv)) * uv).astype(BF16)

    w_spec = pl.BlockSpec((None, d, c), lambda j, i: (j, 0, 0))
    o_spec = pl.BlockSpec((tm, c), lambda j, i: (i, j))
    return _call(
        name, body, (ns, s // tm), [pl.BlockSpec((tm, d), lambda j, i: (i, 0)), w_spec, w_spec], (xn, wg, wu),
        [o_spec, o_spec, o_spec],
        [jax.ShapeDtypeStruct((s, ns * c), F32), jax.ShapeDtypeStruct((s, ns * c), F32), jax.ShapeDtypeStruct((s, ns * c), BF16)],
        [], ("parallel", "parallel"), rider)


def ffn_dh(name, dy, wd, g, u, ns, scale, rider=None):
    s, d = dy.shape
    f = wd.shape[0]
    c = f // ns
    tm = _pick(s, (FFN_TM, 128))

    def body(dy_ref, wd_ref, g_ref, u_ref, dg_ref, du_ref):
        dh = lax.dot_general(dy_ref[...].astype(BF16), wd_ref[...], _DIMS["nt"], preferred_element_type=F32) * scale
        gv, uv = g_ref[...], u_ref[...]
        sig = _sigmoid(gv)
        dg_ref[...] = (dh * uv * (sig * (1.0 + gv * (1.0 - sig)))).astype(BF16)
        du_ref[...] = (dh * (gv * sig)).astype(BF16)

    o_spec = pl.BlockSpec((tm, c), lambda j, i: (i, j))
    return _call(
        name, body, (ns, s // tm),
        [pl.BlockSpec((tm, d), lambda j, i: (i, 0)), pl.BlockSpec((c, d), lambda j, i: (j, 0)), o_spec, o_spec], (dy, wd, g, u),
        [o_spec, o_spec], [jax.ShapeDtypeStruct((s, f), BF16), jax.ShapeDtypeStruct((s, f), BF16)],
        [], ("parallel", "parallel"), rider)


FOX_TQ = 512


def fox_tile(s_len):
    return min(FOX_TQ, s_len)


def _fox_scores(qb, k_ref, cq, ck_ref, i, c, tq):
    kc = k_ref[pl.ds(pl.multiple_of(c * tq, tq), tq), :].astype(BF16)
    s = lax.dot_general(qb, kc, _DIMS["nt"], preferred_element_type=F32) * (HEAD_DIM ** -0.5) + cq - ck_ref[c]
    row = lax.broadcasted_iota(jnp.int32, (tq, tq), 0) + (i - c) * tq
    col = lax.broadcasted_iota(jnp.int32, (tq, tq), 1)
    return jnp.where(row >= col, s, MASK_VALUE)


def _fox_specs(h, s_len, d, tq):
    qb = pl.BlockSpec((None, tq, d), lambda hh, i: (hh, i, 0))
    kb = pl.BlockSpec((None, s_len, d), lambda hh, i: (hh, 0, 0))
    colb = pl.BlockSpec((None, tq, 1), lambda hh, i: (hh, i, 0))
    rowb = pl.BlockSpec((None, s_len // tq, 1, tq), lambda hh, i: (hh, 0, 0, 0))
    return qb, kb, colb, rowb


def fox_fwd(q, k, v, cq, ck, rider=None):
    h, s_len, d = q.shape
    tq = fox_tile(s_len)

    def body(q_ref, k_ref, v_ref, cq_ref, ck_ref, o_ref, lse_ref):
        i = pl.program_id(1)
        qb, cqv = q_ref[...].astype(BF16), cq_ref[...]

        def chunk(c, carry):
            m, l, acc = carry
            s = _fox_scores(qb, k_ref, cqv, ck_ref, i, c, tq)
            m_new = jnp.maximum(m, jnp.max(s, axis=-1, keepdims=True))
            alpha = jnp.exp(m - m_new)
            p = jnp.exp(s - m_new)
            vc = v_ref[pl.ds(pl.multiple_of(c * tq, tq), tq), :].astype(BF16)
            acc = alpha * acc + jnp.dot(p.astype(BF16), vc, preferred_element_type=F32)
            return m_new, alpha * l + jnp.sum(p, axis=-1, keepdims=True), acc

        init = (jnp.full((tq, 1), MASK_VALUE, F32), jnp.zeros((tq, 1), F32), jnp.zeros((tq, d), F32))
        m, l, acc = lax.fori_loop(0, i + 1, chunk, init)
        o_ref[...] = acc / l
        lse_ref[...] = m + jnp.log(l)

    qb, kb, colb, rowb = _fox_specs(h, s_len, d, tq)
    return _call(
        "fox_fwd", body, (h, s_len // tq), [qb, kb, kb, colb, rowb], (q, k, v, cq, ck), [qb, colb],
        [jax.ShapeDtypeStruct((h, s_len, d), F32), jax.ShapeDtypeStruct((h, s_len, 1), F32)], (), ("parallel", "parallel"), rider)


def fox_bwd(q, k, v, cq, ck, o, lse, do, rider=None):
    h, s_len, d = q.shape
    tq = fox_tile(s_len)
    scale = HEAD_DIM ** -0.5

    def body(q_ref, k_ref, v_ref, cq_ref, ck_ref, o_ref, lse_ref, do_ref, dq_ref, dk_ref, dv_ref, dcq_ref, dck_ref):
        i = pl.program_id(1)

        @pl.when(i == 0)
        def _():
            dk_ref[...] = jnp.zeros_like(dk_ref)
            dv_ref[...] = jnp.zeros_like(dv_ref)
            dck_ref[...] = jnp.zeros_like(dck_ref)

        qb, cqv, lse = q_ref[...].astype(BF16), cq_ref[...], lse_ref[...]
        dof = do_ref[...]
        dob = dof.astype(BF16)
        delta = jnp.sum(dof * o_ref[...], axis=-1, keepdims=True)

        def chunk(c, carry):
            dq, dcq = carry
            rows = pl.ds(pl.multiple_of(c * tq, tq), tq)
            p = jnp.exp(_fox_scores(qb, k_ref, cqv, ck_ref, i, c, tq) - lse)
            dp = lax.dot_general(dob, v_ref[rows, :].astype(BF16), _DIMS["nt"], preferred_element_type=F32)
            ds = p * (dp - delta)
            dsb = ds.astype(BF16)
            dv_ref[rows, :] += lax.dot_general(p.astype(BF16), dob, _DIMS["tn"], preferred_element_type=F32)
            dk_ref[rows, :] += lax.dot_general(dsb, qb, _DIMS["tn"], preferred_element_type=F32) * scale
            dck_ref[c] -= jnp.sum(ds, axis=0, keepdims=True)
            dq = dq + jnp.dot(dsb, k_ref[rows, :].astype(BF16), preferred_element_type=F32)
            return dq, dcq + jnp.sum(ds, axis=-1, keepdims=True)

        dq, dcq = lax.fori_loop(0, i + 1, chunk, (jnp.zeros((tq, d), F32), jnp.zeros((tq, 1), F32)))
        dq_ref[...] = dq * scale
        dcq_ref[...] = dcq

    qb, kb, colb, rowb = _fox_specs(h, s_len, d, tq)
    return _call(
        "fox_bwd", body, (h, s_len // tq), [qb, kb, kb, colb, rowb, qb, colb, qb], (q, k, v, cq, ck, o, lse, do),
        [qb, kb, kb, colb, rowb],
        [jax.ShapeDtypeStruct((h, s_len, d), F32)] * 3
        + [jax.ShapeDtypeStruct((h, s_len, 1), F32), jax.ShapeDtypeStruct((h, s_len // tq, 1, tq), F32)],
        (), ("parallel", "arbitrary"), rider)


def _swa_probs(q_ref, kp_ref, kc_ref, sink_ref, n):
    g, w, d = q_ref.shape
    q = q_ref[...].reshape(g * w, d).astype(BF16)
    kw = jnp.concatenate([kp_ref[...], kc_ref[...]], axis=0).astype(BF16)
    s = lax.dot_general(q, kw, _DIMS["nt"], preferred_element_type=F32) * (HEAD_DIM ** -0.5)
    t = lax.broadcasted_iota(jnp.int32, (g * w, 2 * w), 0) & (w - 1)
    col = lax.broadcasted_iota(jnp.int32, (g * w, 2 * w), 1)
    rel = t + w - col
    valid = (rel >= 0) & (rel < w) & ((col >= w) | (n > 0))
    s = jnp.where(valid, s, MASK_VALUE)
    sink = sink_ref[...]
    m = jnp.maximum(jnp.max(s, axis=-1, keepdims=True), sink)
    p = jnp.exp(s - m)
    ps = jnp.exp(sink - m)
    linv = 1.0 / (jnp.sum(p, axis=-1, keepdims=True) + ps)
    return q, kw, p * linv, ps * linv


def _swa_specs(hk, g, s_len, d):
    w = WINDOW
    assert w & (w - 1) == 0 and s_len % w == 0
    qb = pl.BlockSpec((None, g, w, d), lambda hh, n: (hh, 0, n, 0))
    prev = pl.BlockSpec((None, w, d), lambda hh, n: (hh, jnp.maximum(n - 1, 0), 0))
    cur = pl.BlockSpec((None, w, d), lambda hh, n: (hh, n, 0))
    sb = pl.BlockSpec((None, g * w, 1), lambda hh, n: (hh, 0, 0))
    return qb, prev, cur, sb


def swa_fwd(q, k, v, sink, rider=None):
    hk, g, s_len, d = q.shape
    w = WINDOW
    qb, prev, cur, sb = _swa_specs(hk, g, s_len, d)

    def body(q_ref, kp_ref, kc_ref, vp_ref, vc_ref, sink_ref, o_ref):
        _, _, p, _ = _swa_probs(q_ref, kp_ref, kc_ref, sink_ref, pl.program_id(1))
        vw = jnp.concatenate([vp_ref[...], vc_ref[...]], axis=0).astype(BF16)
        o_ref[...] = jnp.dot(p.astype(BF16), vw, preferred_element_type=F32).reshape(g, w, d)

    (o,), rid = _call("swa_fwd", body, (hk, s_len // w), [qb, prev, cur, prev, cur, sb], (q, k, k, v, v, sink), [qb],
                      [jax.ShapeDtypeStruct((hk, g, s_len, d), F32)], (), ("parallel", "parallel"), rider)
    return o, rid


def swa_bwd(q, k, v, sink, o, do, rider=None):
    hk, g, s_len, d = q.shape
    w = WINDOW
    scale = HEAD_DIM ** -0.5
    qb, prev, cur, sb = _swa_specs(hk, g, s_len, d)

    def body(q_ref, kp_ref, kc_ref, vp_ref, vc_ref, sink_ref, o_ref, do_ref, dq_ref, dkp_ref, dkc_ref, dvp_ref, dvc_ref, dsink_ref):
        n = pl.program_id(1)

        @pl.when(n == 0)
        def _():
            dsink_ref[...] = jnp.zeros_like(dsink_ref)

        q, kw, p, ps = _swa_probs(q_ref, kp_ref, kc_ref, sink_ref, n)
        vw = jnp.concatenate([vp_ref[...], vc_ref[...]], axis=0).astype(BF16)
        dof = do_ref[...].reshape(g * w, d)
        dob = dof.astype(BF16)
        delta = jnp.sum(dof * o_ref[...].reshape(g * w, d), axis=-1, keepdims=True)
        dp = lax.dot_general(dob, vw, _DIMS["nt"], preferred_element_type=F32)
        ds = p * (dp - delta)
        dsb = ds.astype(BF16)
        dsink_ref[...] -= ps * delta
        dq_ref[...] = (jnp.dot(dsb, kw, preferred_element_type=F32) * scale).reshape(g, w, d)
        dkw = lax.dot_general(dsb, q, _DIMS["tn"], preferred_element_type=F32) * scale
        dvw = lax.dot_general(p.astype(BF16), dob, _DIMS["tn"], preferred_element_type=F32)
        dkp_ref[...] = dkw[:w]
        dkc_ref[...] = dkw[w:]
        dvp_ref[...] = dvw[:w]
        dvc_ref[...] = dvw[w:]

    kv_shape = jax.ShapeDtypeStruct((hk, s_len, d), F32)
    (dq, dkp, dkc, dvp, dvc, dsink), rid = _call(
        "swa_bwd", body, (hk, s_len // w), [qb, prev, cur, prev, cur, sb, qb, qb], (q, k, k, v, v, sink, o, do),
        [qb, cur, cur, cur, cur, sb],
        [jax.ShapeDtypeStruct((hk, g, s_len, d), F32), kv_shape, kv_shape, kv_shape, kv_shape, jax.ShapeDtypeStruct((hk, g * w, 1), F32)],
        (), ("parallel", "arbitrary"), rider)

    def shift_up(a):
        return jnp.concatenate([a[:, w:], jnp.zeros_like(a[:, :w])], axis=1)

    return (dq, dkc + shift_up(dkp), dvc + shift_up(dvp), dsink), rid


def loss_call(y, target):
    s, d = y.shape
    tm = _pick(s, (512, 256, 128))

    def body(y_ref, t_ref, l_ref, dy_ref):
        e = y_ref[...] - t_ref[...]
        dy_ref[...] = e * (1.0 / d)

        @pl.when(pl.program_id(0) == 0)
        def _():
            l_ref[...] = jnp.zeros_like(l_ref)

        l_ref[...] += jnp.sum(jnp.sum(e * e, axis=0, keepdims=True), axis=1, keepdims=True) * (0.5 / d)

    row = pl.BlockSpec((tm, d), lambda i: (i, 0))
    (l, dy), _ = _call("loss_head", body, (s // tm,), [row, row], (y, target), [pl.BlockSpec((1, 1), lambda i: (0, 0)), row],
                       [jax.ShapeDtypeStruct((1, 1), F32), jax.ShapeDtypeStruct((s, d), F32)], (), ("arbitrary",))
    return l[0, 0], dy


def _row_tile(rows, cols, itemsize):
    target = max(16, (1 << 20) // (cols * itemsize))
    for t in (1024, 512, 256, 128, 64, 32, 16):
        if t <= target and rows % t == 0:
            return t
    return rows


CAST_STEPS = 8


def cast_place(name, ws, p_idx, rider=None):
    n = len(ws)
    assert all(w.shape[0] % (16 * CAST_STEPS) == 0 for w in ws), [w.shape for w in ws]

    def body(p_ref, *refs):
        for w_ref, o_ref in zip(refs[:n], refs[n:]):
            o_ref[...] = w_ref[...].astype(BF16)

    return _call(
        name, body, (CAST_STEPS,), [pl.BlockSpec((w.shape[0] // CAST_STEPS, w.shape[1]), lambda i, pr: (i, 0)) for w in ws], tuple(ws),
        [pl.BlockSpec((None, w.shape[0] // CAST_STEPS, w.shape[1]), lambda i, pr: (pr[0], i, 0)) for w in ws],
        [jax.ShapeDtypeStruct((N_CHIPS,) + w.shape, BF16) for w in ws], (), ("parallel",), rider, prefetch=(p_idx,))


def chip_sum(name, grad, theirs, c_idx):
    ns, r, cols = grad.shape
    rh = r // 2
    tr = _row_tile(rh, cols, 2)
    nb = rh // tr

    def body(c_ref, a_ref, b_ref, o_ref):
        o_ref[...] = (a_ref[...].astype(F32) + b_ref[...].astype(F32)).astype(o_ref.dtype)

    return pl.pallas_call(
        body, name=name,
        grid_spec=pltpu.PrefetchScalarGridSpec(
            num_scalar_prefetch=1, grid=(ns, nb),
            in_specs=[pl.BlockSpec((None, tr, cols), lambda q, i, cr: (q, cr[0] * nb + i, 0)),
                      pl.BlockSpec((None, tr, cols), lambda q, i, cr: (q, i, 0))],
            out_specs=pl.BlockSpec((None, tr, cols), lambda q, i, cr: (q, i, 0))),
        out_shape=jax.ShapeDtypeStruct((ns, rh, cols), BF16),
        compiler_params=pltpu.CompilerParams(dimension_semantics=("parallel", "parallel"), vmem_limit_bytes=VMEM_LIMIT),
    )(c_idx, grad, theirs)


def owner_sum(name, sums, got, pc_idx):
    ns, rh, cols = sums.shape
    tr = _row_tile(rh, cols, 4)
    nb = rh // tr

    def body(pc_ref, a_ref, b_ref, o_ref):
        o_ref[...] = ((a_ref[...].astype(F32) + b_ref[0].astype(F32)) + b_ref[1].astype(F32)) + b_ref[2].astype(F32)

    return pl.pallas_call(
        body, name=name,
        grid_spec=pltpu.PrefetchScalarGridSpec(
            num_scalar_prefetch=1, grid=(nb,),
            in_specs=[pl.BlockSpec((None, tr, cols), lambda i, pc: (pc[0], i, 0)),
                      pl.BlockSpec((3, tr, cols), lambda i, pc: (0, i, 0))],
            out_specs=pl.BlockSpec((tr, cols), lambda i, pc: (pc[1] * nb + i, 0))),
        out_shape=jax.ShapeDtypeStruct((2 * rh, cols), F32),
        compiler_params=pltpu.CompilerParams(dimension_semantics=("parallel",), vmem_limit_bytes=VMEM_LIMIT),
    )(pc_idx, sums, got)


def adamw(name, w, g, m, v, rider=None):
    r, cols = w.shape
    tr = _row_tile(r, cols, 4)
    c1 = 1.0 / (1.0 - ADAM_B1 ** ADAM_STEP)
    c2 = 1.0 / (1.0 - ADAM_B2 ** ADAM_STEP)

    def body(w_ref, g_ref, m_ref, v_ref, d_ref, nm_ref, nv_ref):
        gv = g_ref[...]
        nm = ADAM_B1 * m_ref[...] + (1.0 - ADAM_B1) * gv
        nv = ADAM_B2 * v_ref[...] + (1.0 - ADAM_B2) * (gv * gv)
        d_ref[...] = -ADAM_LR * ((nm * c1) / (jnp.sqrt(nv * c2) + ADAM_EPS) + ADAM_WD * w_ref[...])
        nm_ref[...] = nm
        nv_ref[...] = nv

    blk = pl.BlockSpec((tr, cols), lambda i: (i, 0))
    return _call(name, body, (r // tr,), [blk] * 4, (w, g, m, v), [blk] * 3, [jax.ShapeDtypeStruct((r, cols), F32)] * 3, (), ("parallel",), rider)


def _rope(x, cos, sin):
    x1, x2 = x[..., : HEAD_DIM // 2], x[..., HEAD_DIM // 2:]
    return jnp.concatenate([x1 * cos - x2 * sin, x2 * cos + x1 * sin], axis=-1)


def _heads(a, nh):
    return a.reshape(a.shape[0], nh, HEAD_DIM).transpose(1, 0, 2)


def _unheads(a):
    return a.transpose(1, 0, 2).reshape(a.shape[1], a.shape[0] * HEAD_DIM)


def _win_layout(d_model):
    hf = hq = d_model // (2 * HEAD_DIM)
    hk = hq // 4
    sizes = [hf * HEAD_DIM, hf * HEAD_DIM, hf * HEAD_DIM, hf, hq * HEAD_DIM, hk * HEAD_DIM, hk * HEAD_DIM]
    return hf, hq, hk, sizes


def _pad_cols(a, n):
    return a if a.shape[1] == n else jnp.pad(a, ((0, 0), (0, n - a.shape[1])))


def _relayout_win(win_g):
    ns, d_model, cs = win_g.shape
    _, _, _, sizes = _win_layout(d_model)
    full = win_g.transpose(1, 0, 2).reshape(d_model, ns * cs)
    offs = [0]
    for sz in sizes:
        offs.append(offs[-1] + sz)
    qf, kf, vf, fl, qs, ks, vs = [full[:, offs[i]:offs[i + 1]] for i in range(7)]
    body = jnp.concatenate([qf, kf, vf, qs, ks, vs, _pad_cols(fl, 128)], axis=1)
    return _pad_cols(body, -(-body.shape[1] // 512) * 512)


def _attn_inputs(proj, sm, positions):
    s_len = proj.shape[0]
    hf, hq, hk, sizes = _win_layout(sm["norm_mix_g"].shape[0])
    grp = hq // hk
    o0 = 0
    cols = []
    for sz in (sizes[0], sizes[1], sizes[2], sizes[4], sizes[5], sizes[6], hf):
        cols.append(proj[:, o0:o0 + sz])
        o0 += sz
    q_f, k_f, v_f, q_s, k_s, v_s, f_logit = cols

    q_f = _heads(head_rms("fox_qnorm", q_f, sm["fox_q_norm_g"]), hf)
    k_f = _heads(head_rms("fox_knorm", k_f, sm["fox_k_norm_g"]), hf)
    v_f = _heads(v_f, hf)
    log_f = jax.nn.log_sigmoid(f_logit + sm["b_forget"])
    c = jnp.cumsum(log_f, axis=0).T

    inv_freq = ROPE_THETA ** (-jnp.arange(0, HEAD_DIM, 2, dtype=F32) / HEAD_DIM)
    ang = positions.astype(F32)[:, None] * inv_freq
    cos, sin = jnp.cos(ang), jnp.sin(ang)
    q_s = _heads(head_rms("swa_qnorm", q_s, sm["swa_q_norm_g"]), hq)
    k_s = _heads(head_rms("swa_knorm", k_s, sm["swa_k_norm_g"]), hk)
    q_s = _rope(q_s, cos, sin).reshape(hk, grp, s_len, HEAD_DIM)
    k_s = _rope(k_s, cos, sin)
    v_s = _heads(v_s, hk)
    sink = jnp.broadcast_to(sm["swa_sinks"].reshape(hk, grp, 1, 1), (hk, grp, WINDOW, 1)).reshape(hk, grp * WINDOW, 1)
    tq = fox_tile(s_len)
    return (q_f, k_f, v_f, c[:, :, None], c.reshape(hf, s_len // tq, 1, tq)), (q_s, k_s, v_s, sink)


_BIG = ("ffn1_w_gate", "ffn1_w_up", "ffn1_w_down", "w_in", "w_out", "ffn2_w_gate", "ffn2_w_up", "ffn2_w_down")
_ROW_SHARDED = ("ffn1_w_down", "w_out", "ffn2_w_down")
_SMALL = ("norm_ffn1_g", "norm_mix_g", "b_forget", "fox_q_norm_g", "fox_k_norm_g", "swa_q_norm_g", "swa_k_norm_g", "swa_sinks",
          "out_norm_fox_g", "out_norm_swa_g", "norm_ffn2_g")
_ATTN_SMALL = ("norm_mix_g", "b_forget", "fox_q_norm_g", "fox_k_norm_g", "swa_q_norm_g", "swa_k_norm_g", "swa_sinks")
_ALL = ("norm_ffn1_g", "ffn1_w_gate", "ffn1_w_up", "ffn1_w_down", "norm_mix_g", "w_in", "b_forget", "fox_q_norm_g", "fox_k_norm_g",
        "swa_q_norm_g", "swa_k_norm_g", "swa_sinks", "out_norm_fox_g", "out_norm_swa_g", "w_out", "norm_ffn2_g", "ffn2_w_gate",
        "ffn2_w_up", "ffn2_w_down")


def _pack_small(d):
    parts = []
    for k in _SMALL:
        v = d[k].reshape(-1)
        rows = -(-v.shape[0] // _LANES)
        parts.append(jnp.pad(v, (0, rows * _LANES - v.shape[0])).reshape(rows, _LANES))
    a = jnp.concatenate(parts, axis=0)
    return jnp.pad(a, ((0, -a.shape[0] % 8), (0, 0)))


def _unpack_small(a, like):
    out, r0 = {}, 0
    for k in _SMALL:
        nvals = like[k].shape[1]
        rows = -(-nvals // _LANES)
        out[k] = a[r0:r0 + rows].reshape(-1)[:nvals].reshape(1, nvals)
        r0 += rows
    return out


def _stacked(w):
    return w.reshape(-1, w.shape[-1])


def _local_step(shards, sm, x, positions, target, p_idx, c_idx, pc_idx):
    ns = N_CHIPS
    hf, hq, hk, _ = _win_layout(x.shape[1])
    s_len = x.shape[0]
    full = {}

    def fetch(*jobs):
        names = list(dict.fromkeys(n for n, _, _ in jobs))
        return names, gather([bufs[n] for n in names], [(names.index(n), kind, part) for n, kind, part in jobs])

    def take(names, rid):
        for n, b in zip(names, rid[0]):
            bufs[n] = b

    n1 = ["ffn1_w_gate", "ffn1_w_up", "ffn1_w_down"]
    n2 = ["ffn2_w_gate", "ffn2_w_up", "ffn2_w_down"]
    later = ["w_in", "w_out"] + n2
    placed, _ = cast_place("cast_place_ffn1", [shards[n] for n in n1], p_idx)
    bufs = dict(zip(n1, placed))
    gate1, up1, down1 = n1
    gate2, up2, down2 = n2
    names, rider = fetch((gate1, "ici", WHOLE), (up1, "ici", WHOLE))
    placed, rid = cast_place("cast_place_later", [shards[n] for n in later], p_idx, rider=rider)
    bufs.update(zip(later, placed))
    take(names, rid)
    names, rider = fetch((gate1, "d2d", WHOLE), (up1, "d2d", WHOLE))
    xn1, r1, rid = rms_fwd("ffn1_norm", x, sm["norm_ffn1_g"], BF16, rider=rider)
    take(names, rid)
    names, rider = fetch((down1, "ici", WHOLE))
    (g1, u1, hid1), rid = ffn_gu("ffn1_gu", xn1, bufs[gate1], bufs[up1], rider=rider)
    take(names, rid)
    names, rider = fetch((down1, "d2d", WHOLE))
    take(names, run_step("gather_d2d_ffn1_down", rider))
    wd1 = _stacked(bufs[down1])
    names, rider = fetch(("w_in", "ici", WHOLE))
    h1, rid = mm_nn("ffn1_down", hid1, wd1, scale=0.5, resid=x, rider=rider)
    take(names, rid)

    names, rider = fetch(("w_in", "d2d", WHOLE))
    u, r_mix, rid = rms_fwd("mix_norm", h1, sm["norm_mix_g"], BF16, rider=rider)
    take(names, rid)
    win_p, win_vjp = jax.vjp(_relayout_win, bufs["w_in"])
    names, rider = fetch(("w_out", "ici", WHOLE), (gate2, "ici", (0, 1, 4)))
    proj, rid = mm_nn("mix_inproj", u, win_p, rider=rider)
    take(names, rid)
    sm_attn = {k: sm[k] for k in _ATTN_SMALL}
    (fox_in, swa_in), attn_vjp = jax.vjp(lambda pr, s: _attn_inputs(pr, s, positions), proj, sm_attn)
    names, rider = fetch((gate2, "ici", (1, 4, 4)), ("w_out", "d2d", WHOLE), (gate2, "d2d", (0, 1, 4)))
    (o_f, lse), rid = fox_fwd(*fox_in, rider=rider)
    take(names, rid)
    names, rider = fetch((up2, "ici", (0, 3, 4)), (gate2, "d2d", (1, 4, 4)))
    o_s, rid = swa_fwd(*swa_in, rider=rider)
    take(names, rid)
    o_fox, o_swa = _unheads(o_f), _unheads(o_s.reshape(hq, s_len, HEAD_DIM))
    nf, r_fox = rms_fwd("out_norm_fox", o_fox, sm["out_norm_fox_g"], BF16)
    nsw, r_swa = rms_fwd("out_norm_swa", o_swa, sm["out_norm_swa_g"], BF16)
    o = jnp.concatenate([nf, nsw], axis=-1)
    wout = _stacked(bufs["w_out"])
    names, rider = fetch((up2, "ici", (3, 4, 4)), (up2, "d2d", (0, 3, 4)))
    h2, rid = mm_nn("out_proj", o, wout, resid=h1, rider=rider)
    take(names, rid)

    names, rider = fetch((up2, "d2d", (3, 4, 4)))
    xn2, r2, rid = rms_fwd("ffn2_norm", h2, sm["norm_ffn2_g"], BF16, rider=rider)
    take(names, rid)
    names, rider = fetch((down2, "ici", WHOLE))
    (g2, u2, hid2), rid = ffn_gu("ffn2_gu", xn2, bufs[gate2], bufs[up2], rider=rider)
    take(names, rid)
    names, rider = fetch((down2, "d2d", WHOLE))
    take(names, run_step("gather_d2d_ffn2_down", rider))
    wd2 = _stacked(bufs["ffn2_w_down"])
    y, _ = mm_nn("ffn2_down", hid2, wd2, scale=0.5, resid=h2)
    loss, dy = loss_call(y, target)

    red = {}

    def grad(n, g):
        red[n] = {"grad": g.reshape(ns, -1, g.shape[-1])}

    def ride(*steps):
        def done(rid):
            a0 = n0 = 0
            for rd, cb in steps:
                cb(rid[0][a0:a0 + len(rd.aliased)], rid[1][n0:n0 + len(rd.news)])
                a0, n0 = a0 + len(rd.aliased), n0 + len(rd.news)

        return (combine(*[s[0] for s in steps]) if len(steps) > 1 else steps[0][0]), done

    def xchg(*names):
        def cb(al, news):
            for n, t in zip(names, news):
                red[n]["sum"] = chip_sum("chip_sum_" + n, red[n]["grad"], t, c_idx)

        return exchange_halves([red[n]["grad"] for n in names]), cb

    def scat(n, part=WHOLE):
        def cb(al, news):
            red[n]["got"] = (al or news)[0]

        return scatter_to_owner([red[n]["sum"]], [red[n]["got"]] if "got" in red[n] else None, part), cb

    def own(n):
        red[n]["half"] = owner_sum("owner_sum_" + n, red[n]["sum"], red[n]["got"], pc_idx)

    def join(*names):
        return join_halves([red[n]["half"] for n in names]), lambda al, news: full.update(zip(names, al))

    dwd2, _ = mm_tn("ffn2_dwd", hid2, dy, out_dtype=BF16, scale=0.5)
    grad(down2, dwd2)
    rider, done = ride(xchg(down2))
    (dg2, du2), rid = ffn_dh("ffn2_dh", dy, wd2, g2, u2, ns, 0.5, rider=rider)
    done(rid)
    rider, done = ride(scat(down2, (0, 1, 2)))
    dwg2, rid = mm_tn_sharded("ffn2_dwg", xn2, dg2, ns, rider=rider)
    done(rid)
    grad(gate2, dwg2)
    rider, done = ride(scat(down2, (1, 2, 2)), xchg(gate2))
    dwu2, rid = mm_tn_sharded("ffn2_dwu", xn2, du2, ns, rider=rider)
    done(rid)
    grad(up2, dwu2)
    rider, done = ride(scat(gate2, (0, 1, 2)), xchg(up2))
    dxn, rid = mm_nt_sharded("ffn2_dxn_g", dg2, bufs[gate2], rider=rider)
    done(rid)
    rider, done = ride(scat(gate2, (1, 2, 2)))
    dxn, rid = mm_nt_sharded("ffn2_dxn_u", du2, bufs[up2], resid=dxn, rider=rider)
    done(rid)
    dh2, dgain_ffn2 = rms_bwd("ffn2_dnorm", h2, sm["norm_ffn2_g"], r2, dxn, dres=dy)
    own(down2)
    own(gate2)

    do, _ = mm_nt("out_do", dh2, wout)
    dwout, _ = mm_tn("out_dw", o, dh2, out_dtype=BF16)
    cf = o_fox.shape[1]
    d_fox, dgain_fox = rms_bwd("out_dnorm_fox", o_fox, sm["out_norm_fox_g"], r_fox, do[:, :cf])
    d_swa, dgain_swa = rms_bwd("out_dnorm_swa", o_swa, sm["out_norm_swa_g"], r_swa, do[:, cf:])
    grad("w_out", dwout)
    rider, done = ride(scat(up2))
    swa_cts, rid = swa_bwd(*swa_in, o_s, _heads(d_swa, hq).reshape(o_s.shape), rider=rider)
    done(rid)
    own(up2)
    rider, done = ride(xchg("w_out"), join(down2, gate2, up2))
    fox_cts, rid = fox_bwd(*fox_in, o_f, lse, _heads(d_fox, hf), rider=rider)
    done(rid)
    dproj, dsm_attn = attn_vjp((tuple(fox_cts), tuple(swa_cts)))

    rider, done = ride(scat("w_out"))
    du, rid = mm_nt("mix_du", dproj, win_p, rider=rider)
    done(rid)
    dwin_p, _ = mm_tn("mix_dwin", u, dproj, out_dtype=BF16)
    (dwin,) = win_vjp(dwin_p)
    grad("w_in", dwin)
    dh1, dgain_mix = rms_bwd("mix_dnorm", h1, sm["norm_mix_g"], r_mix, du, dres=dh2)
    own("w_out")

    rider, done = ride(xchg("w_in"))
    dwd1, rid = mm_tn("ffn1_dwd", hid1, dh1, out_dtype=BF16, scale=0.5, rider=rider)
    done(rid)
    grad(down1, dwd1)
    rider, done = ride(scat("w_in"), xchg(down1))
    (dg1, du1), rid = ffn_dh("ffn1_dh", dh1, wd1, g1, u1, ns, 0.5, rider=rider)
    done(rid)
    own("w_in")
    rider, done = ride(scat(down1, (0, 1, 2)), join("w_out", "w_in"))
    dwg1, rid = mm_tn_sharded("ffn1_dwg", xn1, dg1, ns, rider=rider)
    done(rid)
    grad(gate1, dwg1)
    rider, done = ride(scat(down1, (1, 2, 2)), xchg(gate1))
    dwu1, rid = mm_tn_sharded("ffn1_dwu", xn1, du1, ns, rider=rider)
    done(rid)
    grad(up1, dwu1)
    own(down1)
    rider, done = ride(scat(gate1, (0, 1, 2)), xchg(up1), join(down1))
    dxn, rid = mm_nt_sharded("ffn1_dxn_g", dg1, bufs[gate1], rider=rider)
    done(rid)
    rider, done = ride(scat(gate1, (1, 2, 2)), scat(up1, (0, 1, 4)))
    dxn, rid = mm_nt_sharded("ffn1_dxn_u", du1, bufs[up1], resid=dxn, rider=rider)
    done(rid)
    dx, dgain_ffn1 = rms_bwd("ffn1_dnorm", x, sm["norm_ffn1_g"], r1, dxn, dres=dh1)
    own(gate1)

    rider, done = ride(scat(up1, (1, 4, 4)), join(gate1))
    done(run_step("reduce_tail", rider))
    own(up1)
    rider, done = ride(join(up1))
    done(run_step("join_tail", rider))

    g_small = dict(dsm_attn)
    g_small["norm_mix_g"] = g_small["norm_mix_g"] + dgain_mix
    g_small.update(norm_ffn1_g=dgain_ffn1, norm_ffn2_g=dgain_ffn2, out_norm_fox_g=dgain_fox, out_norm_swa_g=dgain_swa)
    return loss, dx, full, g_small


def kernel(x, positions, norm_ffn1_g, ffn1_w_gate, ffn1_w_up, ffn1_w_down, norm_mix_g, w_in, b_forget, fox_q_norm_g, fox_k_norm_g, swa_q_norm_g, swa_k_norm_g, swa_sinks, out_norm_fox_g, out_norm_swa_g, w_out, norm_ffn2_g, ffn2_w_gate, ffn2_w_up, ffn2_w_down, loss_target, m_norm_ffn1_g, m_ffn1_w_gate, m_ffn1_w_up, m_ffn1_w_down, m_norm_mix_g, m_w_in, m_b_forget, m_fox_q_norm_g, m_fox_k_norm_g, m_swa_q_norm_g, m_swa_k_norm_g, m_swa_sinks, m_out_norm_fox_g, m_out_norm_swa_g, m_w_out, m_norm_ffn2_g, m_ffn2_w_gate, m_ffn2_w_up, m_ffn2_w_down, v_norm_ffn1_g, v_ffn1_w_gate, v_ffn1_w_up, v_ffn1_w_down, v_norm_mix_g, v_w_in, v_b_forget, v_fox_q_norm_g, v_fox_k_norm_g, v_swa_q_norm_g, v_swa_k_norm_g, v_swa_sinks, v_out_norm_fox_g, v_out_norm_swa_g, v_w_out, v_norm_ffn2_g, v_ffn2_w_gate, v_ffn2_w_up, v_ffn2_w_down):
    args = dict(locals())
    w = {k: args[k] for k in _ALL}
    m = {k: args["m_" + k] for k in _ALL}
    v = {k: args["v_" + k] for k in _ALL}
    c_idx = lax.axis_index("c").astype(jnp.int32).reshape(1)
    p_idx = (2 * lax.axis_index("x") + lax.axis_index("y")).astype(jnp.int32).reshape(1)
    pc_idx = jnp.concatenate([p_idx, c_idx])

    small = {k: w[k] for k in _SMALL}
    loss, grad_x, g_shard, g_small = _local_step({k: w[k][0] for k in _BIG}, {k: w[k][0] for k in _SMALL}, x[0], positions[0],
                                                 loss_target[0], p_idx, c_idx, pc_idx)
    loss = lax.psum(loss, ("x", "y", "c"))
    g_small_sum = _unpack_small(all_reduce_small(_pack_small({k: g_small[k].reshape(1, -1) for k in _SMALL})), small)

    grad_w, delta, new_m, new_v = {}, {}, {}, {}
    for k in _BIG:
        (d, nm, nv), _ = adamw("adamw_" + k, w[k][0], g_shard[k], m[k][0], v[k][0])
        grad_w[k] = g_shard[k][None]
        delta[k], new_m[k], new_v[k] = d[None], nm[None], nv[None]
    (d, nm, nv), _ = adamw("adamw_small", _pack_small(small), _pack_small(g_small_sum), _pack_small({k: m[k] for k in _SMALL}),
                           _pack_small({k: v[k] for k in _SMALL}))
    grad_w.update(g_small_sum)
    delta.update(_unpack_small(d, small))
    new_m.update(_unpack_small(nm, small))
    new_v.update(_unpack_small(nv, small))

    return (loss, grad_x[None], *[grad_w[k] for k in _ALL], *[delta[k] for k in _ALL], *[new_m[k] for k in _ALL], *[new_v[k] for k in _ALL])
```

```python
import functools

import jax
import jax.numpy as jnp
from jax import lax
from jax.experimental import pallas as pl
from jax.experimental.pallas import tpu as pltpu

F32 = jnp.float32
BF16 = jnp.bfloat16

HEAD_DIM = 64
WINDOW = 128
ROPE_THETA = 10000.0
EPS = 1e-6
N_CHIPS = 4
N_DEV = 8

ADAM_LR = 0.001
ADAM_B1 = 0.9
ADAM_B2 = 0.999
ADAM_EPS = 1e-08
ADAM_WD = 0.01
ADAM_STEP = 10

V7X_VMEM_BYTES = 64 * 1024 * 1024
VMEM_LIMIT = V7X_VMEM_BYTES - 8 * 1024 * 1024
MASK_VALUE = -1e30

_MESH = pl.DeviceIdType.MESH
_HBM = pl.BlockSpec(memory_space=pl.ANY)
_DIMS = {"nn": (((1,), (0,)), ((), ())), "nt": (((1,), (1,)), ((), ())), "tn": (((0,), (0,)), ((), ()))}


def _pick(n, prefs):
    for p in prefs:
        if n % p == 0:
            return p
    return n


class Rider:
    def __init__(self, reads, aliased, news, nsem, build):
        self.reads, self.aliased, self.news, self.nsem, self.build = list(reads), list(aliased), list(news), nsem, build


class _Shifted:
    def __init__(self, ref, off):
        self.ref, self.off = ref, off

    @property
    def at(self):
        return self

    def __getitem__(self, k):
        return self.ref.at[k + self.off]


def combine(*riders):
    def build(reads, al, news, ssem, rsem):
        out = ([], [], [])
        r0 = a0 = n0 = s0 = 0
        for rd in riders:
            nr, na, nn = len(rd.reads), len(rd.aliased), len(rd.news)
            part = rd.build(reads[r0:r0 + nr], al[a0:a0 + na], news[n0:n0 + nn], _Shifted(ssem, s0), _Shifted(rsem, s0))
            for acc, lst in zip(out, part):
                acc.extend(lst)
            r0, a0, n0, s0 = r0 + nr, a0 + na, n0 + nn, s0 + rd.nsem
        return out

    return Rider(sum((r.reads for r in riders), []), sum((r.aliased for r in riders), []), sum((r.news for r in riders), []),
                 sum(r.nsem for r in riders), build)


def _me():
    return lax.axis_index("x"), lax.axis_index("y"), lax.axis_index("c")


def _other_chips(x, y):
    return [(1 - x, y), (x, 1 - y), (1 - x, 1 - y)]


WHOLE = (0, 1, 1)


def _rows(ref, start, rows, part=WHOLE):
    k0, k1, n = part
    assert rows % n == 0, (rows, part)
    idx = (slice(None),) * (len(ref.shape) - 2) + (pl.ds(start + k0 * (rows // n), (k1 - k0) * (rows // n)), slice(None))
    return ref.at[idx]


def _half(ref, h, part=WHOLE):
    rows = ref.shape[-2] // 2
    return _rows(ref, h * rows, rows, part)


def _remote(src, dst, ssem, rsem, k, to):
    return pltpu.make_async_remote_copy(src_ref=src, dst_ref=dst, send_sem=ssem.at[k], recv_sem=rsem.at[k], device_id=to,
                                        device_id_type=_MESH)


def _later(*args):
    return functools.partial(_remote, *args)


def gather(bufs, jobs):
    def build(reads, al, news, ssem, rsem):
        x, y, c = _me()
        p = 2 * x + y
        starts, arrivals = [], []
        for n, (b, kind, part) in enumerate(jobs):
            for j, chip in enumerate(_other_chips(x, y)):
                q = 2 * chip[0] + chip[1]
                if kind == "ici":
                    src, landing, to = _half(al[b].at[p], c, part), _half(al[b].at[q], c, part), (*chip, c)
                else:
                    src, landing, to = _half(al[b].at[q], c, part), _half(al[b].at[q], 1 - c, part), (x, y, 1 - c)
                starts.append(_later(src, src, ssem, rsem, 3 * n + j, to))
                arrivals.append(_later(landing, landing, ssem, rsem, 3 * n + j, to))
        return starts, arrivals, starts

    return Rider([], bufs, [], 3 * len(jobs), build)


def exchange_halves(grads):
    def build(reads, al, news, ssem, rsem):
        x, y, c = _me()
        cps = [_later(_half(g, 1 - c), t, ssem, rsem, w, (x, y, 1 - c)) for w, (g, t) in enumerate(zip(reads, news))]
        return cps, cps, cps

    return Rider(grads, [], [jax.ShapeDtypeStruct((g.shape[0], g.shape[1] // 2, g.shape[2]), g.dtype) for g in grads], len(grads), build)


def scatter_to_owner(sums, gots=None, part=WHOLE):
    def build(reads, al, news, ssem, rsem):
        x, y, c = _me()
        cps = []
        for w, (s, got) in enumerate(zip(reads, al or news)):
            rows = s.shape[-2]
            for j, chip in enumerate(_other_chips(x, y)):
                cps.append(_later(_rows(s.at[2 * chip[0] + chip[1]], 0, rows, part), _rows(got.at[j], 0, rows, part), ssem, rsem,
                                  3 * w + j, (*chip, c)))
        return cps, cps, cps

    news = [] if gots else [jax.ShapeDtypeStruct((3,) + s.shape[1:], s.dtype) for s in sums]
    return Rider(sums, gots or [], news, 3 * len(sums), build)


def join_halves(fulls):
    def build(reads, al, news, ssem, rsem):
        x, y, c = _me()
        starts, arrivals = [], []
        for w, f in enumerate(al):
            mine, landing = _half(f, c), _half(f, 1 - c)
            starts.append(_later(mine, mine, ssem, rsem, w, (x, y, 1 - c)))
            arrivals.append(_later(landing, landing, ssem, rsem, w, (x, y, 1 - c)))
        return starts, arrivals, starts

    return Rider([], fulls, [], len(fulls), build)


def _start_and_wait(rider, reads, al, news, ssem, rsem, first, last):
    @pl.when(first)
    def _():
        for cp in rider.build(reads, al, news, ssem, rsem)[0]:
            cp().start()

    def finish():
        @pl.when(last)
        def _():
            _, arrivals, sends = rider.build(reads, al, news, ssem, rsem)
            for cp in arrivals:
                cp().wait_recv()
            for cp in sends:
                cp().wait_send()

    return finish


def _call(name, body, grid, in_specs, args, out_specs, out_shape, scratch=(), semantics=None, rider=None, prefetch=()):
    n_pre, n_in, n_out, n_scr = len(prefetch), len(args), len(out_shape), len(scratch)
    nr, na, nn = (len(rider.reads), len(rider.aliased), len(rider.news)) if rider else (0, 0, 0)

    def wrapped(*refs):
        pre, refs = refs[:n_pre], refs[n_pre:]
        ins, reads = refs[:n_in], refs[n_in:n_in + nr]
        o0 = n_in + nr + na
        outs, al, news = refs[o0:o0 + n_out], refs[o0 + n_out:o0 + n_out + na], refs[o0 + n_out + na:o0 + n_out + na + nn]
        s0 = o0 + n_out + na + nn
        scr, (ssem, rsem) = refs[s0:s0 + n_scr], refs[s0 + n_scr:]
        first = functools.reduce(jnp.logical_and, [pl.program_id(a) == 0 for a in range(len(grid))])
        last = functools.reduce(jnp.logical_and, [pl.program_id(a) == g - 1 for a, g in enumerate(grid)])
        finish = _start_and_wait(rider, reads, al, news, ssem, rsem, first, last)
        body(*pre, *ins, *outs, *scr)
        finish()

    kernel_fn, all_in, all_out, shapes, scr = body, list(in_specs), list(out_specs), list(out_shape), list(scratch)
    operands, aliases = (*prefetch, *args), {}
    if rider:
        kernel_fn, semantics = wrapped, ("arbitrary",) * len(grid)
        all_in += [_HBM] * (nr + na)
        all_out += [_HBM] * (na + nn)
        shapes += [jax.ShapeDtypeStruct(a.shape, a.dtype) for a in rider.aliased] + rider.news
        scr += [pltpu.SemaphoreType.DMA((rider.nsem,)), pltpu.SemaphoreType.DMA((rider.nsem,))]
        operands += (*rider.reads, *rider.aliased)
        aliases = {n_pre + n_in + nr + i: n_out + i for i in range(na)}
    params = pltpu.CompilerParams(dimension_semantics=semantics, vmem_limit_bytes=VMEM_LIMIT)
    if n_pre:
        spec = pltpu.PrefetchScalarGridSpec(num_scalar_prefetch=n_pre, grid=grid, in_specs=all_in, out_specs=all_out, scratch_shapes=scr)
        outs = pl.pallas_call(kernel_fn, name=name, grid_spec=spec, out_shape=shapes, input_output_aliases=aliases, compiler_params=params)(*operands)
    else:
        outs = pl.pallas_call(kernel_fn, name=name, grid=grid, in_specs=all_in, out_specs=all_out, out_shape=shapes, scratch_shapes=scr,
                              input_output_aliases=aliases, compiler_params=params)(*operands)
    return list(outs[:n_out]), ((list(outs[n_out:n_out + na]), list(outs[n_out + na:])) if rider else None)


def run_step(name, rider):
    nr, na, nn = len(rider.reads), len(rider.aliased), len(rider.news)

    def body(*refs):
        reads = refs[:nr]
        al, news = refs[nr + na:nr + 2 * na], refs[nr + 2 * na:nr + 2 * na + nn]
        ssem, rsem = refs[nr + 2 * na + nn:]
        starts, arrivals, sends = rider.build(reads, al, news, ssem, rsem)
        for cp in starts:
            cp().start()
        for cp in arrivals:
            cp().wait_recv()
        for cp in sends:
            cp().wait_send()

    outs = pl.pallas_call(
        body, name=name, in_specs=[_HBM] * (nr + na), out_specs=[_HBM] * (na + nn),
        out_shape=[jax.ShapeDtypeStruct(a.shape, a.dtype) for a in rider.aliased] + rider.news,
        input_output_aliases={nr + i: i for i in range(na)},
        scratch_shapes=[pltpu.SemaphoreType.DMA((rider.nsem,)), pltpu.SemaphoreType.DMA((rider.nsem,))],
    )(*rider.reads, *rider.aliased)
    return list(outs[:na]), list(outs[na:])


def all_reduce_small(v):
    rows, lanes = v.shape

    def body(v_ref, o_ref, slots, send_sems, recv_sems):
        x, y, c = _me()
        me = 4 * x + 2 * y + c
        slots[me] = v_ref[...]
        cps = []
        for k in range(1, N_DEV):
            peer = (x ^ (k >> 2), y ^ ((k >> 1) & 1), c ^ (k & 1))
            cps.append(_remote(v_ref, slots.at[me], send_sems, recv_sems, k - 1, peer))
            cps[-1].start()
        for k in range(1, N_DEV):
            theirs = slots.at[me ^ k]
            _remote(theirs, theirs, send_sems, recv_sems, k - 1, (x, y, c)).wait_recv()
        for cp in cps:
            cp.wait_send()
        acc = slots[0]
        for i in range(1, N_DEV):
            acc = acc + slots[i]
        o_ref[...] = acc

    return pl.pallas_call(
        body, name="all_reduce_small",
        in_specs=[pl.BlockSpec(memory_space=pltpu.VMEM)], out_specs=pl.BlockSpec(memory_space=pltpu.VMEM),
        out_shape=jax.ShapeDtypeStruct((rows, lanes), F32),
        scratch_shapes=[pltpu.VMEM((N_DEV, rows, lanes), F32), pltpu.SemaphoreType.DMA((N_DEV - 1,)), pltpu.SemaphoreType.DMA((N_DEV - 1,))],
    )(v)


def _mm_call(name, mode, a, b, a_spec, b_spec, out_shape, out_spec, grid, acc_shape, scale=1.0, resid=None, resid_spec=None, rider=None):
    nk = grid[2]
    dims = _DIMS[mode]
    has_resid = resid is not None

    def body(*refs):
        a_ref, b_ref = refs[:2]
        r_ref = refs[2] if has_resid else None
        o_ref = refs[3] if has_resid else refs[2]

        def finish(r):
            if scale != 1.0:
                r = r * scale
            if has_resid:
                r = r_ref[...].astype(F32) + r
            o_ref[...] = r.astype(o_ref.dtype)

        part = lax.dot_general(a_ref[...].astype(BF16), b_ref[...].astype(BF16), dims, preferred_element_type=F32)
        if nk == 1:
            finish(part)
            return
        acc_ref = refs[-1]
        k = pl.program_id(2)

        @pl.when(k == 0)
        def _():
            acc_ref[...] = part

        @pl.when(k > 0)
        def _():
            acc_ref[...] += part

        @pl.when(k == nk - 1)
        def _():
            finish(acc_ref[...])

    in_specs = [a_spec, b_spec] + ([resid_spec] if has_resid else [])
    args = (a, b) + ((resid,) if has_resid else ())
    (out,), rid = _call(name, body, grid, in_specs, args, [out_spec], [out_shape], [pltpu.VMEM(acc_shape, F32)] if nk > 1 else [],
                        ("parallel", "parallel", "arbitrary"), rider)
    return out, rid


MM_VMEM_BUDGET = 40 * 1024 * 1024
_TILE_OPTS = (2048, 1408, 1024, 512, 256, 128)


def _tiles(m, n, kd, a_item, b_item, o_item, r_item=0, tm=None, tn=None, tk=None):
    def opts(full, fixed, cap):
        return [fixed] if fixed else [t for t in _TILE_OPTS if t <= cap and full % t == 0] or [full]

    best = None
    for cm in opts(m, tm, 1408):
        for cn in opts(n, tn, 1408):
            for ck in opts(kd, tk, 2048):
                blocks = cm * ck * a_item + ck * cn * b_item + cm * cn * (o_item + r_item)
                casts = (cm * ck * 2 if a_item == 4 else 0) + (ck * cn * 2 if b_item == 4 else 0)
                if 2 * blocks + cm * cn * 4 + casts <= MM_VMEM_BUDGET:
                    key = (cm * cn * ck, ck)
                    if best is None or key > best[0]:
                        best = (key, (cm, cn, ck))
    assert best is not None, (m, n, kd)
    return best[1]


def _item(x):
    return jnp.dtype(x.dtype).itemsize


def mm_nn(name, a, b, *, out_dtype=F32, scale=1.0, resid=None, rider=None):
    m, kd = a.shape
    n = b.shape[1]
    tm, tn, tk = _tiles(m, n, kd, _item(a), _item(b), jnp.dtype(out_dtype).itemsize, 0 if resid is None else _item(resid))
    o_spec = pl.BlockSpec((tm, tn), lambda i, j, k: (i, j))
    return _mm_call(
        name, "nn", a, b, pl.BlockSpec((tm, tk), lambda i, j, k: (i, k)), pl.BlockSpec((tk, tn), lambda i, j, k: (k, j)),
        jax.ShapeDtypeStruct((m, n), out_dtype), o_spec, (m // tm, n // tn, kd // tk), (tm, tn), scale, resid, o_spec, rider)


def mm_nt(name, a, b, *, out_dtype=F32, scale=1.0, resid=None, rider=None):
    m, kd = a.shape
    n = b.shape[0]
    tm, tn, tk = _tiles(m, n, kd, _item(a), _item(b), jnp.dtype(out_dtype).itemsize, 0 if resid is None else _item(resid))
    o_spec = pl.BlockSpec((tm, tn), lambda i, j, k: (i, j))
    return _mm_call(
        name, "nt", a, b, pl.BlockSpec((tm, tk), lambda i, j, k: (i, k)), pl.BlockSpec((tn, tk), lambda i, j, k: (j, k)),
        jax.ShapeDtypeStruct((m, n), out_dtype), o_spec, (m // tm, n // tn, kd // tk), (tm, tn), scale, resid, o_spec, rider)


def mm_tn(name, a, b, *, out_dtype=F32, scale=1.0, rider=None):
    kd, m = a.shape
    n = b.shape[1]
    tm, tn, tk = _tiles(m, n, kd, _item(a), _item(b), jnp.dtype(out_dtype).itemsize)
    return _mm_call(
        name, "tn", a, b, pl.BlockSpec((tk, tm), lambda i, j, k: (k, i)), pl.BlockSpec((tk, tn), lambda i, j, k: (k, j)),
        jax.ShapeDtypeStruct((m, n), out_dtype), pl.BlockSpec((tm, tn), lambda i, j, k: (i, j)),
        (m // tm, n // tn, kd // tk), (tm, tn), scale, rider=rider)


def mm_nn_sharded(name, a, w, *, rider=None):
    m, kd = a.shape
    ns, _, c = w.shape
    tm, _, tk = _tiles(m, c, kd, _item(a), _item(w), 4, tn=c)
    return _mm_call(
        name, "nn", a, w, pl.BlockSpec((tm, tk), lambda i, j, k: (i, k)), pl.BlockSpec((None, tk, c), lambda i, j, k: (j, k, 0)),
        jax.ShapeDtypeStruct((ns, m, c), F32), pl.BlockSpec((None, tm, c), lambda i, j, k: (j, i, 0)),
        (m // tm, ns, kd // tk), (tm, c), rider=rider)


def mm_nt_sharded(name, a, w, *, resid=None, rider=None):
    ns, n, c = w.shape
    m = a.shape[-2]
    tm, tn, _ = _tiles(m, n, c, _item(a), _item(w), 4, 0 if resid is None else _item(resid), tk=c)
    o_spec = pl.BlockSpec((tm, tn), lambda i, j, k: (i, j))
    a_spec = pl.BlockSpec((tm, c), lambda i, j, k: (i, k)) if a.ndim == 2 else pl.BlockSpec((None, tm, c), lambda i, j, k: (k, i, 0))
    return _mm_call(
        name, "nt", a, w, a_spec, pl.BlockSpec((None, tn, c), lambda i, j, k: (k, j, 0)),
        jax.ShapeDtypeStruct((m, n), F32), o_spec, (m // tm, n // tn, ns), (tm, tn), 1.0, resid, o_spec, rider)


def mm_tn_sharded(name, a, b, ns, *, rider=None):
    kd, m = a.shape
    c = b.shape[-1] // (ns if b.ndim == 2 else 1)
    tm, _, tk = _tiles(m, c, kd, _item(a), _item(b), 2, tn=c)
    b_spec = pl.BlockSpec((tk, c), lambda i, j, k: (k, j)) if b.ndim == 2 else pl.BlockSpec((None, tk, c), lambda i, j, k: (j, k, 0))
    return _mm_call(
        name, "tn", a, b, pl.BlockSpec((tk, tm), lambda i, j, k: (k, i)), b_spec,
        jax.ShapeDtypeStruct((ns, m, c), BF16), pl.BlockSpec((None, tm, c), lambda i, j, k: (j, i, 0)),
        (m // tm, ns, kd // tk), (tm, c), rider=rider)


def rms_fwd(name, x, g, out_dtype, rider=None):
    r, c = x.shape
    tm = _pick(r, (512, 256, 128, 64, 8))

    def body(x_ref, g_ref, y_ref, r_ref):
        xf = x_ref[...].astype(F32)
        rstd = lax.rsqrt(jnp.mean(xf * xf, axis=-1, keepdims=True) + EPS)
        y_ref[...] = ((xf * rstd) * g_ref[...]).astype(y_ref.dtype)
        r_ref[...] = rstd

    (y, rstd), rid = _call(
        name, body, (r // tm,), [pl.BlockSpec((tm, c), lambda i: (i, 0)), pl.BlockSpec((1, c), lambda i: (0, 0))], (x, g.reshape(1, c)),
        [pl.BlockSpec((tm, c), lambda i: (i, 0)), pl.BlockSpec((tm, 1), lambda i: (i, 0))],
        [jax.ShapeDtypeStruct((r, c), out_dtype), jax.ShapeDtypeStruct((r, 1), F32)], (), ("parallel",), rider)
    return (y, rstd) if rider is None else (y, rstd, rid)


def rms_bwd(name, x, g, rstd, dy, dres=None):
    r, c = x.shape
    tm = _pick(r, (512, 256, 128, 64, 8))
    has_res = dres is not None

    def body(*refs):
        if has_res:
            x_ref, g_ref, r_ref, dy_ref, dres_ref, dx_ref, dg_ref = refs
        else:
            x_ref, g_ref, r_ref, dy_ref, dx_ref, dg_ref = refs
        xhat = x_ref[...].astype(F32) * r_ref[...]
        dyf = dy_ref[...].astype(F32)
        gdy = dyf * g_ref[...]
        dx = r_ref[...] * (gdy - xhat * jnp.mean(gdy * xhat, axis=-1, keepdims=True))
        if has_res:
            dx = dx + dres_ref[...]
        dx_ref[...] = dx

        @pl.when(pl.program_id(0) == 0)
        def _():
            dg_ref[...] = jnp.zeros_like(dg_ref)

        dg_ref[...] += jnp.sum(dyf * xhat, axis=0, keepdims=True)

    row = pl.BlockSpec((tm, c), lambda i: (i, 0))
    in_specs = [row, pl.BlockSpec((1, c), lambda i: (0, 0)), pl.BlockSpec((tm, 1), lambda i: (i, 0)), row] + ([row] if has_res else [])
    args = (x, g.reshape(1, c), rstd, dy) + ((dres,) if has_res else ())
    (dx, dg), _ = _call(name, body, (r // tm,), in_specs, args, [row, pl.BlockSpec((1, c), lambda i: (0, 0))],
                        [jax.ShapeDtypeStruct((r, c), F32), jax.ShapeDtypeStruct((1, c), F32)], (), ("arbitrary",))
    return dx, dg.reshape(c)


_LANES = 128


def _head_mean(v):
    if v.shape[1] == HEAD_DIM:
        return jnp.mean(v, axis=-1, keepdims=True)
    low = lax.broadcasted_iota(jnp.int32, v.shape, 1) < HEAD_DIM
    lo = jnp.sum(jnp.where(low, v, 0.0), axis=-1, keepdims=True)
    hi = jnp.sum(jnp.where(low, 0.0, v), axis=-1, keepdims=True)
    return jnp.where(low, lo, hi) * (1.0 / HEAD_DIM)


def _head_groups(c):
    width = _LANES if c % _LANES == 0 else HEAD_DIM
    assert c % width == 0, c
    return width, [slice(k * width, (k + 1) * width) for k in range(c // width)]


def _head_gain(g, width):
    return jnp.tile(g.reshape(1, HEAD_DIM), (1, width // HEAD_DIM))


def head_rms_fwd(name, x, g):
    s, c = x.shape
    tm = _pick(s, (256, 128, 8))
    width, groups = _head_groups(c)

    def body(x_ref, g_ref, y_ref):
        for sl in groups:
            xs = x_ref[:, sl]
            y_ref[:, sl] = (xs * lax.rsqrt(_head_mean(xs * xs) + EPS)) * g_ref[...]

    row = pl.BlockSpec((tm, c), lambda i: (i, 0))
    (y,), _ = _call(name, body, (s // tm,), [row, pl.BlockSpec((1, width), lambda i: (0, 0))], (x, _head_gain(g, width)), [row],
                    [jax.ShapeDtypeStruct((s, c), F32)], (), ("parallel",))
    return y


def head_rms_bwd(name, x, g, dy):
    s, c = x.shape
    tm = _pick(s, (256, 128, 8))
    width, groups = _head_groups(c)

    def body(x_ref, g_ref, dy_ref, dx_ref, dg_ref):
        @pl.when(pl.program_id(0) == 0)
        def _():
            dg_ref[...] = jnp.zeros_like(dg_ref)

        for sl in groups:
            xs, dys = x_ref[:, sl], dy_ref[:, sl]
            rstd = lax.rsqrt(_head_mean(xs * xs) + EPS)
            xhat = xs * rstd
            gdy = dys * g_ref[...]
            dx_ref[:, sl] = rstd * (gdy - xhat * _head_mean(gdy * xhat))
            dg_ref[...] += jnp.sum(dys * xhat, axis=0, keepdims=True)

    row = pl.BlockSpec((tm, c), lambda i: (i, 0))
    vec = pl.BlockSpec((1, width), lambda i: (0, 0))
    (dx, dg), _ = _call(name, body, (s // tm,), [row, vec, row], (x, _head_gain(g, width), dy), [row, vec],
                        [jax.ShapeDtypeStruct((s, c), F32), jax.ShapeDtypeStruct((1, width), F32)], (), ("arbitrary",))
    return dx, jnp.sum(dg.reshape(width // HEAD_DIM, HEAD_DIM), axis=0)


@functools.partial(jax.custom_vjp, nondiff_argnums=(0,))
def head_rms(name, x, g):
    return head_rms_fwd(name + "_fwd", x, g)


def _head_rms_fwd(name, x, g):
    return head_rms_fwd(name + "_fwd", x, g), (x, g)


def _head_rms_bwd(name, res, dy):
    return head_rms_bwd(name + "_bwd", *res, dy)


head_rms.defvjp(_head_rms_fwd, _head_rms_bwd)


FFN_TM = 512


def _sigmoid(x):
    return 1.0 / (1.0 + jnp.exp(-x))


def ffn_gu(name, xn, wg, wu, rider=None):
    s, d = xn.shape
    ns, _, c = wg.shape
    tm = _pick(s, (FFN_TM, 128))

    def body(x_ref, wg_ref, wu_ref, g_ref, u_ref, h_ref):
        xb = x_ref[...]
        gv = jnp.dot(xb, wg_ref[...], preferred_element_type=F32)
        uv = jnp.dot(xb, wu_ref[...], preferred_element_type=F32)
        g_ref[...] = gv
        u_ref[...] = uv
        h_ref[...] = ((gv * _sigmoid(gv)) * uv).astype(BF16)

    w_spec = pl.BlockSpec((None, d, c), lambda j, i: (j, 0, 0))
    o_spec = pl.BlockSpec((tm, c), lambda j, i: (i, j))
    return _call(
        name, body, (ns, s // tm), [pl.BlockSpec((tm, d), lambda j, i: (i, 0)), w_spec, w_spec], (xn, wg, wu),
        [o_spec, o_spec, o_spec],
        [jax.ShapeDtypeStruct((s, ns * c), F32), jax.ShapeDtypeStruct((s, ns * c), F32), jax.ShapeDtypeStruct((s, ns * c), BF16)],
        [], ("parallel", "parallel"), rider)


def ffn_dh(name, dy, wd, g, u, ns, scale, rider=None):
    s, d = dy.shape
    f = wd.shape[0]
    c = f // ns
    tm = _pick(s, (FFN_TM, 128))

    def body(dy_ref, wd_ref, g_ref, u_ref, dg_ref, du_ref):
        dh = lax.dot_general(dy_ref[...].astype(BF16), wd_ref[...], _DIMS["nt"], preferred_element_type=F32) * scale
        gv, uv = g_ref[...], u_ref[...]
        sig = _sigmoid(gv)
        dg_ref[...] = (dh * uv * (sig * (1.0 + gv * (1.0 - sig)))).astype(BF16)
        du_ref[...] = (dh * (gv * sig)).astype(BF16)

    o_spec = pl.BlockSpec((tm, c), lambda j, i: (i, j))
    return _call(
        name, body, (ns, s // tm),
        [pl.BlockSpec((tm, d), lambda j, i: (i, 0)), pl.BlockSpec((c, d), lambda j, i: (j, 0)), o_spec, o_spec], (dy, wd, g, u),
        [o_spec, o_spec], [jax.ShapeDtypeStruct((s, f), BF16), jax.ShapeDtypeStruct((s, f), BF16)],
        [], ("parallel", "parallel"), rider)


FOX_TQ = 512


def fox_tile(s_len):
    return min(FOX_TQ, s_len)


def _fox_scores(qb, k_ref, cq, ck_ref, i, c, tq):
    kc = k_ref[pl.ds(pl.multiple_of(c * tq, tq), tq), :].astype(BF16)
    s = lax.dot_general(qb, kc, _DIMS["nt"], preferred_element_type=F32) * (HEAD_DIM ** -0.5) + cq - ck_ref[c]
    row = lax.broadcasted_iota(jnp.int32, (tq, tq), 0) + (i - c) * tq
    col = lax.broadcasted_iota(jnp.int32, (tq, tq), 1)
    return jnp.where(row >= col, s, MASK_VALUE)


def _fox_specs(h, s_len, d, tq):
    qb = pl.BlockSpec((None, tq, d), lambda hh, i: (hh, i, 0))
    kb = pl.BlockSpec((None, s_len, d), lambda hh, i: (hh, 0, 0))
    colb = pl.BlockSpec((None, tq, 1), lambda hh, i: (hh, i, 0))
    rowb = pl.BlockSpec((None, s_len // tq, 1, tq), lambda hh, i: (hh, 0, 0, 0))
    return qb, kb, colb, rowb


def fox_fwd(q, k, v, cq, ck, rider=None):
    h, s_len, d = q.shape
    tq = fox_tile(s_len)

    def body(q_ref, k_ref, v_ref, cq_ref, ck_ref, o_ref, lse_ref):
        i = pl.program_id(1)
        qb, cqv = q_ref[...].astype(BF16), cq_ref[...]

        def chunk(c, carry):
            m, l, acc = carry
            s = _fox_scores(qb, k_ref, cqv, ck_ref, i, c, tq)
            m_new = jnp.maximum(m, jnp.max(s, axis=-1, keepdims=True))
            alpha = jnp.exp(m - m_new)
            p = jnp.exp(s - m_new)
            vc = v_ref[pl.ds(pl.multiple_of(c * tq, tq), tq), :].astype(BF16)
            acc = alpha * acc + jnp.dot(p.astype(BF16), vc, preferred_element_type=F32)
            return m_new, alpha * l + jnp.sum(p, axis=-1, keepdims=True), acc

        init = (jnp.full((tq, 1), MASK_VALUE, F32), jnp.zeros((tq, 1), F32), jnp.zeros((tq, d), F32))
        m, l, acc = lax.fori_loop(0, i + 1, chunk, init)
        o_ref[...] = acc / l
        lse_ref[...] = m + jnp.log(l)

    qb, kb, colb, rowb = _fox_specs(h, s_len, d, tq)
    return _call(
        "fox_fwd", body, (h, s_len // tq), [qb, kb, kb, colb, rowb], (q, k, v, cq, ck), [qb, colb],
        [jax.ShapeDtypeStruct((h, s_len, d), F32), jax.ShapeDtypeStruct((h, s_len, 1), F32)], (), ("parallel", "parallel"), rider)


def fox_bwd(q, k, v, cq, ck, o, lse, do, rider=None):
    h, s_len, d = q.shape
    tq = fox_tile(s_len)
    scale = HEAD_DIM ** -0.5

    def body(q_ref, k_ref, v_ref, cq_ref, ck_ref, o_ref, lse_ref, do_ref, dq_ref, dk_ref, dv_ref, dcq_ref, dck_ref):
        i = pl.program_id(1)

        @pl.when(i == 0)
        def _():
            dk_ref[...] = jnp.zeros_like(dk_ref)
            dv_ref[...] = jnp.zeros_like(dv_ref)
            dck_ref[...] = jnp.zeros_like(dck_ref)

        qb, cqv, lse = q_ref[...].astype(BF16), cq_ref[...], lse_ref[...]
        dof = do_ref[...]
        dob = dof.astype(BF16)
        delta = jnp.sum(dof * o_ref[...], axis=-1, keepdims=True)

        def chunk(c, carry):
            dq, dcq = carry
            rows = pl.ds(pl.multiple_of(c * tq, tq), tq)
            p = jnp.exp(_fox_scores(qb, k_ref, cqv, ck_ref, i, c, tq) - lse)
            dp = lax.dot_general(dob, v_ref[rows, :].astype(BF16), _DIMS["nt"], preferred_element_type=F32)
            ds = p * (dp - delta)
            dsb = ds.astype(BF16)
            dv_ref[rows, :] += lax.dot_general(p.astype(BF16), dob, _DIMS["tn"], preferred_element_type=F32)
            dk_ref[rows, :] += lax.dot_general(dsb, qb, _DIMS["tn"], preferred_element_type=F32) * scale
            dck_ref[c] -= jnp.sum(ds, axis=0, keepdims=True)
            dq = dq + jnp.dot(dsb, k_ref[rows, :].astype(BF16), preferred_element_type=F32)
            return dq, dcq + jnp.sum(ds, axis=-1, keepdims=True)

        dq, dcq = lax.fori_loop(0, i + 1, chunk, (jnp.zeros((tq, d), F32), jnp.zeros((tq, 1), F32)))
        dq_ref[...] = dq * scale
        dcq_ref[...] = dcq

    qb, kb, colb, rowb = _fox_specs(h, s_len, d, tq)
    return _call(
        "fox_bwd", body, (h, s_len // tq), [qb, kb, kb, colb, rowb, qb, colb, qb], (q, k, v, cq, ck, o, lse, do),
        [qb, kb, kb, colb, rowb],
        [jax.ShapeDtypeStruct((h, s_len, d), F32)] * 3
        + [jax.ShapeDtypeStruct((h, s_len, 1), F32), jax.ShapeDtypeStruct((h, s_len // tq, 1, tq), F32)],
        (), ("parallel", "arbitrary"), rider)


def _swa_probs(q_ref, kp_ref, kc_ref, sink_ref, n):
    g, w, d = q_ref.shape
    q = q_ref[...].reshape(g * w, d).astype(BF16)
    kw = jnp.concatenate([kp_ref[...], kc_ref[...]], axis=0).astype(BF16)
    s = lax.dot_general(q, kw, _DIMS["nt"], preferred_element_type=F32) * (HEAD_DIM ** -0.5)
    t = lax.broadcasted_iota(jnp.int32, (g * w, 2 * w), 0) & (w - 1)
    col = lax.broadcasted_iota(jnp.int32, (g * w, 2 * w), 1)
    rel = t + w - col
    valid = (rel >= 0) & (rel < w) & ((col >= w) | (n > 0))
    s = jnp.where(valid, s, MASK_VALUE)
    sink = sink_ref[...]
    m = jnp.maximum(jnp.max(s, axis=-1, keepdims=True), sink)
    p = jnp.exp(s - m)
    ps = jnp.exp(sink - m)
    linv = 1.0 / (jnp.sum(p, axis=-1, keepdims=True) + ps)
    return q, kw, p * linv, ps * linv


def _swa_specs(hk, g, s_len, d):
    w = WINDOW
    assert w & (w - 1) == 0 and s_len % w == 0
    qb = pl.BlockSpec((None, g, w, d), lambda hh, n: (hh, 0, n, 0))
    prev = pl.BlockSpec((None, w, d), lambda hh, n: (hh, jnp.maximum(n - 1, 0), 0))
    cur = pl.BlockSpec((None, w, d), lambda hh, n: (hh, n, 0))
    sb = pl.BlockSpec((None, g * w, 1), lambda hh, n: (hh, 0, 0))
    return qb, prev, cur, sb


def swa_fwd(q, k, v, sink, rider=None):
    hk, g, s_len, d = q.shape
    w = WINDOW
    qb, prev, cur, sb = _swa_specs(hk, g, s_len, d)

    def body(q_ref, kp_ref, kc_ref, vp_ref, vc_ref, sink_ref, o_ref):
        _, _, p, _ = _swa_probs(q_ref, kp_ref, kc_ref, sink_ref, pl.program_id(1))
        vw = jnp.concatenate([vp_ref[...], vc_ref[...]], axis=0).astype(BF16)
        o_ref[...] = jnp.dot(p.astype(BF16), vw, preferred_element_type=F32).reshape(g, w, d)

    (o,), rid = _call("swa_fwd", body, (hk, s_len // w), [qb, prev, cur, prev, cur, sb], (q, k, k, v, v, sink), [qb],
                      [jax.ShapeDtypeStruct((hk, g, s_len, d), F32)], (), ("parallel", "parallel"), rider)
    return o, rid


def swa_bwd(q, k, v, sink, o, do, rider=None):
    hk, g, s_len, d = q.shape
    w = WINDOW
    scale = HEAD_DIM ** -0.5
    qb, prev, cur, sb = _swa_specs(hk, g, s_len, d)

    def body(q_ref, kp_ref, kc_ref, vp_ref, vc_ref, sink_ref, o_ref, do_ref, dq_ref, dkp_ref, dkc_ref, dvp_ref, dvc_ref, dsink_ref):
        n = pl.program_id(1)

        @pl.when(n == 0)
        def _():
            dsink_ref[...] = jnp.zeros_like(dsink_ref)

        q, kw, p, ps = _swa_probs(q_ref, kp_ref, kc_ref, sink_ref, n)
        vw = jnp.concatenate([vp_ref[...], vc_ref[...]], axis=0).astype(BF16)
        dof = do_ref[...].reshape(g * w, d)
        dob = dof.astype(BF16)
        delta = jnp.sum(dof * o_ref[...].reshape(g * w, d), axis=-1, keepdims=True)
        dp = lax.dot_general(dob, vw, _DIMS["nt"], preferred_element_type=F32)
        ds = p * (dp - delta)
        dsb = ds.astype(BF16)
        dsink_ref[...] -= ps * delta
        dq_ref[...] = (jnp.dot(dsb, kw, preferred_element_type=F32) * scale).reshape(g, w, d)
        dkw = lax.dot_general(dsb, q, _DIMS["tn"], preferred_element_type=F32) * scale
        dvw = lax.dot_general(p.astype(BF16), dob, _DIMS["tn"], preferred_element_type=F32)
        dkp_ref[...] = dkw[:w]
        dkc_ref[...] = dkw[w:]
        dvp_ref[...] = dvw[:w]
        dvc_ref[...] = dvw[w:]

    kv_shape = jax.ShapeDtypeStruct((hk, s_len, d), F32)
    (dq, dkp, dkc, dvp, dvc, dsink), rid = _call(
        "swa_bwd", body, (hk, s_len // w), [qb, prev, cur, prev, cur, sb, qb, qb], (q, k, k, v, v, sink, o, do),
        [qb, cur, cur, cur, cur, sb],
        [jax.ShapeDtypeStruct((hk, g, s_len, d), F32), kv_shape, kv_shape, kv_shape, kv_shape, jax.ShapeDtypeStruct((hk, g * w, 1), F32)],
        (), ("parallel", "arbitrary"), rider)

    def shift_up(a):
        return jnp.concatenate([a[:, w:], jnp.zeros_like(a[:, :w])], axis=1)

    return (dq, dkc + shift_up(dkp), dvc + shift_up(dvp), dsink), rid


def loss_call(y, target):
    s, d = y.shape
    tm = _pick(s, (512, 256, 128))

    def body(y_ref, t_ref, l_ref, dy_ref):
        e = y_ref[...] - t_ref[...]
        dy_ref[...] = e * (1.0 / d)

        @pl.when(pl.program_id(0) == 0)
        def _():
            l_ref[...] = jnp.zeros_like(l_ref)

        l_ref[...] += jnp.sum(jnp.sum(e * e, axis=0, keepdims=True), axis=1, keepdims=True) * (0.5 / d)

    row = pl.BlockSpec((tm, d), lambda i: (i, 0))
    (l, dy), _ = _call("loss_head", body, (s // tm,), [row, row], (y, target), [pl.BlockSpec((1, 1), lambda i: (0, 0)), row],
                       [jax.ShapeDtypeStruct((1, 1), F32), jax.ShapeDtypeStruct((s, d), F32)], (), ("arbitrary",))
    return l[0, 0], dy


def _row_tile(rows, cols, itemsize):
    target = max(16, (1 << 20) // (cols * itemsize))
    for t in (1024, 512, 256, 128, 64, 32, 16):
        if t <= target and rows % t == 0:
            return t
    return rows


CAST_STEPS = 8


def cast_place(name, ws, p_idx, rider=None):
    n = len(ws)
    assert all(w.shape[0] % (16 * CAST_STEPS) == 0 for w in ws), [w.shape for w in ws]

    def body(p_ref, *refs):
        for w_ref, o_ref in zip(refs[:n], refs[n:]):
            o_ref[...] = w_ref[...].astype(BF16)

    return _call(
        name, body, (CAST_STEPS,), [pl.BlockSpec((w.shape[0] // CAST_STEPS, w.shape[1]), lambda i, pr: (i, 0)) for w in ws], tuple(ws),
        [pl.BlockSpec((None, w.shape[0] // CAST_STEPS, w.shape[1]), lambda i, pr: (pr[0], i, 0)) for w in ws],
        [jax.ShapeDtypeStruct((N_CHIPS,) + w.shape, BF16) for w in ws], (), ("parallel",), rider, prefetch=(p_idx,))


def chip_sum(name, grad, theirs, c_idx):
    ns, r, cols = grad.shape
    rh = r // 2
    tr = _row_tile(rh, cols, 2)
    nb = rh // tr

    def body(c_ref, a_ref, b_ref, o_ref):
        o_ref[...] = (a_ref[...].astype(F32) + b_ref[...].astype(F32)).astype(o_ref.dtype)

    return pl.pallas_call(
        body, name=name,
        grid_spec=pltpu.PrefetchScalarGridSpec(
            num_scalar_prefetch=1, grid=(ns, nb),
            in_specs=[pl.BlockSpec((None, tr, cols), lambda q, i, cr: (q, cr[0] * nb + i, 0)),
                      pl.BlockSpec((None, tr, cols), lambda q, i, cr: (q, i, 0))],
            out_specs=pl.BlockSpec((None, tr, cols), lambda q, i, cr: (q, i, 0))),
        out_shape=jax.ShapeDtypeStruct((ns, rh, cols), BF16),
        compiler_params=pltpu.CompilerParams(dimension_semantics=("parallel", "parallel"), vmem_limit_bytes=VMEM_LIMIT),
    )(c_idx, grad, theirs)


def owner_sum(name, sums, got, pc_idx):
    ns, rh, cols = sums.shape
    tr = _row_tile(rh, cols, 4)
    nb = rh // tr

    def body(pc_ref, a_ref, b_ref, o_ref):
        o_ref[...] = ((a_ref[...].astype(F32) + b_ref[0].astype(F32)) + b_ref[1].astype(F32)) + b_ref[2].astype(F32)

    return pl.pallas_call(
        body, name=name,
        grid_spec=pltpu.PrefetchScalarGridSpec(
            num_scalar_prefetch=1, grid=(nb,),
            in_specs=[pl.BlockSpec((None, tr, cols), lambda i, pc: (pc[0], i, 0)),
                      pl.BlockSpec((3, tr, cols), lambda i, pc: (0, i, 0))],
            out_specs=pl.BlockSpec((tr, cols), lambda i, pc: (pc[1] * nb + i, 0))),
        out_shape=jax.ShapeDtypeStruct((2 * rh, cols), F32),
        compiler_params=pltpu.CompilerParams(dimension_semantics=("parallel",), vmem_limit_bytes=VMEM_LIMIT),
    )(pc_idx, sums, got)


def adamw(name, w, g, m, v):
    r, cols = w.shape
    tr = _row_tile(r, cols, 4)
    c1 = 1.0 / (1.0 - ADAM_B1 ** ADAM_STEP)
    c2 = 1.0 / (1.0 - ADAM_B2 ** ADAM_STEP)

    def body(w_ref, g_ref, m_ref, v_ref, go_ref, d_ref, nm_ref, nv_ref):
        gv = g_ref[...]
        nm = ADAM_B1 * m_ref[...] + (1.0 - ADAM_B1) * gv
        nv = ADAM_B2 * v_ref[...] + (1.0 - ADAM_B2) * (gv * gv)
        go_ref[...] = gv
        d_ref[...] = -ADAM_LR * ((nm * c1) / (jnp.sqrt(nv * c2) + ADAM_EPS) + ADAM_WD * w_ref[...])
        nm_ref[...] = nm
        nv_ref[...] = nv

    blk = pl.BlockSpec((tr, cols), lambda i: (i, 0))
    return _call(name, body, (r // tr,), [blk] * 4, (w, g, m, v), [blk] * 4, [jax.ShapeDtypeStruct((r, cols), F32)] * 4, (), ("parallel",))


def _rope(x, cos, sin):
    x1, x2 = x[..., : HEAD_DIM // 2], x[..., HEAD_DIM // 2:]
    return jnp.concatenate([x1 * cos - x2 * sin, x2 * cos + x1 * sin], axis=-1)


def _heads(a, nh):
    return a.reshape(a.shape[0], nh, HEAD_DIM).transpose(1, 0, 2)


def _unheads(a):
    return a.transpose(1, 0, 2).reshape(a.shape[1], a.shape[0] * HEAD_DIM)


def _win_layout(d_model):
    hf = hq = d_model // (2 * HEAD_DIM)
    hk = hq // 4
    sizes = [hf * HEAD_DIM, hf * HEAD_DIM, hf * HEAD_DIM, hf, hq * HEAD_DIM, hk * HEAD_DIM, hk * HEAD_DIM]
    return hf, hq, hk, sizes


def _pad_lanes(a):
    return jnp.pad(a, ((0, 0), (0, -a.shape[1] % _LANES)))


def _attn_inputs(proj, sm, positions):
    ns, s_len, _ = proj.shape
    hf, hq, hk, sizes = _win_layout(sm["norm_mix_g"].shape[0])
    grp = hq // hk
    cs = sum(sizes) // ns
    full = proj[:, :, :cs].transpose(1, 0, 2).reshape(s_len, ns * cs)
    o0 = 0
    cols = []
    for sz in sizes:
        cols.append(full[:, o0:o0 + sz])
        o0 += sz
    q_f, k_f, v_f, f_logit, q_s, k_s, v_s = cols

    q_f = _heads(head_rms("fox_qnorm", q_f, sm["fox_q_norm_g"]), hf)
    k_f = _heads(head_rms("fox_knorm", k_f, sm["fox_k_norm_g"]), hf)
    v_f = _heads(v_f, hf)
    log_f = jax.nn.log_sigmoid(f_logit + sm["b_forget"])
    c = jnp.cumsum(log_f, axis=0).T

    inv_freq = ROPE_THETA ** (-jnp.arange(0, HEAD_DIM, 2, dtype=F32) / HEAD_DIM)
    ang = positions.astype(F32)[:, None] * inv_freq
    cos, sin = jnp.cos(ang), jnp.sin(ang)
    q_s = _heads(head_rms("swa_qnorm", q_s, sm["swa_q_norm_g"]), hq)
    k_s = _heads(head_rms("swa_knorm", k_s, sm["swa_k_norm_g"]), hk)
    q_s = _rope(q_s, cos, sin).reshape(hk, grp, s_len, HEAD_DIM)
    k_s = _rope(k_s, cos, sin)
    v_s = _heads(v_s, hk)
    sink = jnp.broadcast_to(sm["swa_sinks"].reshape(hk, grp, 1, 1), (hk, grp, WINDOW, 1)).reshape(hk, grp * WINDOW, 1)
    tq = fox_tile(s_len)
    return (q_f, k_f, v_f, c[:, :, None], c.reshape(hf, s_len // tq, 1, tq)), (q_s, k_s, v_s, sink)


_BIG = ("ffn1_w_gate", "ffn1_w_up", "ffn1_w_down", "w_in", "w_out", "ffn2_w_gate", "ffn2_w_up", "ffn2_w_down")
_ROW_SHARDED = ("ffn1_w_down", "w_out", "ffn2_w_down")
_SMALL = ("norm_ffn1_g", "norm_mix_g", "b_forget", "fox_q_norm_g", "fox_k_norm_g", "swa_q_norm_g", "swa_k_norm_g", "swa_sinks",
          "out_norm_fox_g", "out_norm_swa_g", "norm_ffn2_g")
_ATTN_SMALL = ("norm_mix_g", "b_forget", "fox_q_norm_g", "fox_k_norm_g", "swa_q_norm_g", "swa_k_norm_g", "swa_sinks")
_ALL = ("norm_ffn1_g", "ffn1_w_gate", "ffn1_w_up", "ffn1_w_down", "norm_mix_g", "w_in", "b_forget", "fox_q_norm_g", "fox_k_norm_g",
        "swa_q_norm_g", "swa_k_norm_g", "swa_sinks", "out_norm_fox_g", "out_norm_swa_g", "w_out", "norm_ffn2_g", "ffn2_w_gate",
        "ffn2_w_up", "ffn2_w_down")


def _pack_small(d):
    parts = []
    for k in _SMALL:
        v = d[k].reshape(-1)
        rows = -(-v.shape[0] // _LANES)
        parts.append(jnp.pad(v, (0, rows * _LANES - v.shape[0])).reshape(rows, _LANES))
    a = jnp.concatenate(parts, axis=0)
    return jnp.pad(a, ((0, -a.shape[0] % 8), (0, 0)))


def _unpack_small(a, like):
    out, r0 = {}, 0
    for k in _SMALL:
        nvals = like[k].shape[1]
        rows = -(-nvals // _LANES)
        out[k] = a[r0:r0 + rows].reshape(-1)[:nvals].reshape(1, nvals)
        r0 += rows
    return out


def _stacked(w):
    return w.reshape(-1, w.shape[-1])


def _local_step(shards, sm, x, positions, target, p_idx, c_idx, pc_idx):
    ns = N_CHIPS
    hf, hq, hk, _ = _win_layout(x.shape[1])
    s_len = x.shape[0]
    full = {}

    def fetch(*jobs):
        names = list(dict.fromkeys(n for n, _, _ in jobs))
        return names, gather([bufs[n] for n in names], [(names.index(n), kind, part) for n, kind, part in jobs])

    def take(names, rid):
        for n, b in zip(names, rid[0]):
            bufs[n] = b

    n1 = ["ffn1_w_gate", "ffn1_w_up", "ffn1_w_down"]
    n2 = ["ffn2_w_gate", "ffn2_w_up", "ffn2_w_down"]
    later = ["w_in", "w_out"] + n2
    placed, _ = cast_place("cast_place_ffn1", [shards[n] for n in n1], p_idx)
    bufs = dict(zip(n1, placed))
    gate1, up1, down1 = n1
    gate2, up2, down2 = n2
    names, rider = fetch((gate1, "ici", WHOLE), (up1, "ici", WHOLE))
    placed, rid = cast_place("cast_place_later", [shards[n] for n in later], p_idx, rider=rider)
    bufs.update(zip(later, placed))
    take(names, rid)
    names, rider = fetch((gate1, "d2d", WHOLE), (up1, "d2d", WHOLE))
    xn1, r1, rid = rms_fwd("ffn1_norm", x, sm["norm_ffn1_g"], BF16, rider=rider)
    take(names, rid)
    names, rider = fetch((down1, "ici", WHOLE))
    (g1, u1, hid1), rid = ffn_gu("ffn1_gu", xn1, bufs[gate1], bufs[up1], rider=rider)
    take(names, rid)
    names, rider = fetch((down1, "d2d", WHOLE))
    take(names, run_step("gather_d2d_ffn1_down", rider))
    wd1 = _stacked(bufs[down1])
    names, rider = fetch(("w_in", "ici", WHOLE))
    h1, rid = mm_nn("ffn1_down", hid1, wd1, scale=0.5, resid=x, rider=rider)
    take(names, rid)

    names, rider = fetch(("w_in", "d2d", WHOLE))
    u, r_mix, rid = rms_fwd("mix_norm", h1, sm["norm_mix_g"], BF16, rider=rider)
    take(names, rid)
    names, rider = fetch(("w_out", "ici", WHOLE), (gate2, "ici", (0, 1, 4)))
    proj, rid = mm_nn_sharded("mix_inproj", u, bufs["w_in"], rider=rider)
    take(names, rid)
    sm_attn = {k: sm[k] for k in _ATTN_SMALL}
    (fox_in, swa_in), attn_vjp = jax.vjp(lambda pr, s: _attn_inputs(pr, s, positions), proj, sm_attn)
    names, rider = fetch((gate2, "ici", (1, 4, 4)), ("w_out", "d2d", WHOLE), (gate2, "d2d", (0, 1, 4)))
    (o_f, lse), rid = fox_fwd(*fox_in, rider=rider)
    take(names, rid)
    names, rider = fetch((up2, "ici", (0, 3, 4)), (gate2, "d2d", (1, 4, 4)))
    o_s, rid = swa_fwd(*swa_in, rider=rider)
    take(names, rid)
    o_fox, o_swa = _unheads(o_f), _unheads(o_s.reshape(hq, s_len, HEAD_DIM))
    nf, r_fox = rms_fwd("out_norm_fox", o_fox, sm["out_norm_fox_g"], BF16)
    nsw, r_swa = rms_fwd("out_norm_swa", o_swa, sm["out_norm_swa_g"], BF16)
    o = jnp.concatenate([nf, nsw], axis=-1)
    wout = _stacked(bufs["w_out"])
    names, rider = fetch((up2, "ici", (3, 4, 4)), (up2, "d2d", (0, 3, 4)))
    h2, rid = mm_nn("out_proj", o, wout, resid=h1, rider=rider)
    take(names, rid)

    names, rider = fetch((up2, "d2d", (3, 4, 4)))
    xn2, r2, rid = rms_fwd("ffn2_norm", h2, sm["norm_ffn2_g"], BF16, rider=rider)
    take(names, rid)
    names, rider = fetch((down2, "ici", WHOLE))
    (g2, u2, hid2), rid = ffn_gu("ffn2_gu", xn2, bufs[gate2], bufs[up2], rider=rider)
    take(names, rid)
    names, rider = fetch((down2, "d2d", WHOLE))
    take(names, run_step("gather_d2d_ffn2_down", rider))
    wd2 = _stacked(bufs["ffn2_w_down"])
    y, _ = mm_nn("ffn2_down", hid2, wd2, scale=0.5, resid=h2)
    loss, dy = loss_call(y, target)

    red = {}

    def grad(n, g):
        red[n] = {"grad": g.reshape(ns, -1, g.shape[-1])}

    def ride(*steps):
        def done(rid):
            a0 = n0 = 0
            for rd, cb in steps:
                cb(rid[0][a0:a0 + len(rd.aliased)], rid[1][n0:n0 + len(rd.news)])
                a0, n0 = a0 + len(rd.aliased), n0 + len(rd.news)

        return (combine(*[s[0] for s in steps]) if len(steps) > 1 else steps[0][0]), done

    def xchg(*names):
        def cb(al, news):
            for n, t in zip(names, news):
                red[n]["sum"] = chip_sum("chip_sum_" + n, red[n]["grad"], t, c_idx)

        return exchange_halves([red[n]["grad"] for n in names]), cb

    def scat(n, part=WHOLE):
        def cb(al, news):
            red[n]["got"] = (al or news)[0]

        return scatter_to_owner([red[n]["sum"]], [red[n]["got"]] if "got" in red[n] else None, part), cb

    def own(n):
        red[n]["half"] = owner_sum("owner_sum_" + n, red[n]["sum"], red[n]["got"], pc_idx)

    def join(*names):
        return join_halves([red[n]["half"] for n in names]), lambda al, news: full.update(zip(names, al))

    dwd2, _ = mm_tn("ffn2_dwd", hid2, dy, out_dtype=BF16, scale=0.5)
    grad(down2, dwd2)
    rider, done = ride(xchg(down2))
    (dg2, du2), rid = ffn_dh("ffn2_dh", dy, wd2, g2, u2, ns, 0.5, rider=rider)
    done(rid)
    rider, done = ride(scat(down2, (0, 1, 2)))
    dwg2, rid = mm_tn_sharded("ffn2_dwg", xn2, dg2, ns, rider=rider)
    done(rid)
    grad(gate2, dwg2)
    rider, done = ride(scat(down2, (1, 2, 2)), xchg(gate2))
    dwu2, rid = mm_tn_sharded("ffn2_dwu", xn2, du2, ns, rider=rider)
    done(rid)
    grad(up2, dwu2)
    rider, done = ride(scat(gate2, (0, 1, 2)), xchg(up2))
    dxn, rid = mm_nt_sharded("ffn2_dxn_g", dg2, bufs[gate2], rider=rider)
    done(rid)
    rider, done = ride(scat(gate2, (1, 2, 2)))
    dxn, rid = mm_nt_sharded("ffn2_dxn_u", du2, bufs[up2], resid=dxn, rider=rider)
    done(rid)
    dh2, dgain_ffn2 = rms_bwd("ffn2_dnorm", h2, sm["norm_ffn2_g"], r2, dxn, dres=dy)
    own(down2)
    own(gate2)

    do, _ = mm_nt("out_do", dh2, wout)
    dwout, _ = mm_tn("out_dw", o, dh2, out_dtype=BF16)
    cf = o_fox.shape[1]
    d_fox, dgain_fox = rms_bwd("out_dnorm_fox", o_fox, sm["out_norm_fox_g"], r_fox, do[:, :cf])
    d_swa, dgain_swa = rms_bwd("out_dnorm_swa", o_swa, sm["out_norm_swa_g"], r_swa, do[:, cf:])
    grad("w_out", dwout)
    rider, done = ride(scat(up2))
    swa_cts, rid = swa_bwd(*swa_in, o_s, _heads(d_swa, hq).reshape(o_s.shape), rider=rider)
    done(rid)
    own(up2)
    rider, done = ride(xchg("w_out"), join(down2, gate2, up2))
    fox_cts, rid = fox_bwd(*fox_in, o_f, lse, _heads(d_fox, hf), rider=rider)
    done(rid)
    dproj, dsm_attn = attn_vjp((tuple(fox_cts), tuple(swa_cts)))

    rider, done = ride(scat("w_out"))
    du, rid = mm_nt_sharded("mix_du", dproj, bufs["w_in"], rider=rider)
    done(rid)
    dwin, _ = mm_tn_sharded("mix_dwin", u, dproj, ns)
    grad("w_in", dwin)
    dh1, dgain_mix = rms_bwd("mix_dnorm", h1, sm["norm_mix_g"], r_mix, du, dres=dh2)
    own("w_out")

    rider, done = ride(xchg("w_in"))
    dwd1, rid = mm_tn("ffn1_dwd", hid1, dh1, out_dtype=BF16, scale=0.5, rider=rider)
    done(rid)
    grad(down1, dwd1)
    rider, done = ride(scat("w_in"), xchg(down1))
    (dg1, du1), rid = ffn_dh("ffn1_dh", dh1, wd1, g1, u1, ns, 0.5, rider=rider)
    done(rid)
    own("w_in")
    rider, done = ride(scat(down1, (0, 1, 2)), join("w_out", "w_in"))
    dwg1, rid = mm_tn_sharded("ffn1_dwg", xn1, dg1, ns, rider=rider)
    done(rid)
    grad(gate1, dwg1)
    rider, done = ride(scat(down1, (1, 2, 2)), xchg(gate1))
    dwu1, rid = mm_tn_sharded("ffn1_dwu", xn1, du1, ns, rider=rider)
    done(rid)
    grad(up1, dwu1)
    own(down1)
    rider, done = ride(scat(gate1, (0, 1, 2)), xchg(up1), join(down1))
    dxn, rid = mm_nt_sharded("ffn1_dxn_g", dg1, bufs[gate1], rider=rider)
    done(rid)
    rider, done = ride(scat(gate1, (1, 2, 2)), scat(up1, (0, 1, 4)))
    dxn, rid = mm_nt_sharded("ffn1_dxn_u", du1, bufs[up1], resid=dxn, rider=rider)
    done(rid)
    dx, dgain_ffn1 = rms_bwd("ffn1_dnorm", x, sm["norm_ffn1_g"], r1, dxn, dres=dh1)
    own(gate1)

    rider, done = ride(scat(up1, (1, 4, 4)), join(gate1))
    done(run_step("reduce_tail", rider))
    own(up1)
    rider, done = ride(join(up1))
    done(run_step("join_tail", rider))

    g_small = dict(dsm_attn)
    g_small["norm_mix_g"] = g_small["norm_mix_g"] + dgain_mix
    g_small.update(norm_ffn1_g=dgain_ffn1, norm_ffn2_g=dgain_ffn2, out_norm_fox_g=dgain_fox, out_norm_swa_g=dgain_swa)
    return loss, dx, full, g_small


def kernel(x, positions, norm_ffn1_g, ffn1_w_gate, ffn1_w_up, ffn1_w_down, norm_mix_g, w_in, b_forget, fox_q_norm_g, fox_k_norm_g, swa_q_norm_g, swa_k_norm_g, swa_sinks, out_norm_fox_g, out_norm_swa_g, w_out, norm_ffn2_g, ffn2_w_gate, ffn2_w_up, ffn2_w_down, loss_target, m_norm_ffn1_g, m_ffn1_w_gate, m_ffn1_w_up, m_ffn1_w_down, m_norm_mix_g, m_w_in, m_b_forget, m_fox_q_norm_g, m_fox_k_norm_g, m_swa_q_norm_g, m_swa_k_norm_g, m_swa_sinks, m_out_norm_fox_g, m_out_norm_swa_g, m_w_out, m_norm_ffn2_g, m_ffn2_w_gate, m_ffn2_w_up, m_ffn2_w_down, v_norm_ffn1_g, v_ffn1_w_gate, v_ffn1_w_up, v_ffn1_w_down, v_norm_mix_g, v_w_in, v_b_forget, v_fox_q_norm_g, v_fox_k_norm_g, v_swa_q_norm_g, v_swa_k_norm_g, v_swa_sinks, v_out_norm_fox_g, v_out_norm_swa_g, v_w_out, v_norm_ffn2_g, v_ffn2_w_gate, v_ffn2_w_up, v_ffn2_w_down):
    args = dict(locals())
    w = {k: args[k] for k in _ALL}
    m = {k: args["m_" + k] for k in _ALL}
    v = {k: args["v_" + k] for k in _ALL}
    c_idx = lax.axis_index("c").astype(jnp.int32).reshape(1)
    p_idx = (2 * lax.axis_index("x") + lax.axis_index("y")).astype(jnp.int32).reshape(1)
    pc_idx = jnp.concatenate([p_idx, c_idx])

    small = {k: w[k] for k in _SMALL}
    shards = {k: w[k][0] for k in _BIG}
    shards["w_in"] = _pad_lanes(shards["w_in"])
    loss, grad_x, g_shard, g_small = _local_step(shards, {k: w[k][0] for k in _SMALL}, x[0], positions[0], loss_target[0],
                                                 p_idx, c_idx, pc_idx)
    g_shard["w_in"] = g_shard["w_in"][:, :w["w_in"].shape[-1]]
    loss = lax.psum(loss, ("x", "y", "c"))
    g_small_sum = _unpack_small(all_reduce_small(_pack_small({k: g_small[k].reshape(1, -1) for k in _SMALL})), small)

    grad_w, delta, new_m, new_v = {}, {}, {}, {}
    for k in _BIG:
        (g, d, nm, nv), _ = adamw("adamw_" + k, w[k][0], g_shard[k], m[k][0], v[k][0])
        grad_w[k], delta[k], new_m[k], new_v[k] = g[None], d[None], nm[None], nv[None]
    (_, d, nm, nv), _ = adamw("adamw_small", _pack_small(small), _pack_small(g_small_sum), _pack_small({k: m[k] for k in _SMALL}),
                              _pack_small({k: v[k] for k in _SMALL}))
    grad_w.update(g_small_sum)
    delta.update(_unpack_small(d, small))
    new_m.update(_unpack_small(nm, small))
    new_v.update(_unpack_small(nv, small))

    return (loss, grad_x[None], *[grad_w[k] for k in _ALL], *[delta[k] for k in _ALL], *[new_m[k] for k in _ALL], *[new_v[k] for k in _ALL])
```

```python
import functools

import jax
import jax.numpy as jnp
from jax import lax
from jax.experimental import pallas as pl
from jax.experimental.pallas import tpu as pltpu

F32 = jnp.float32
BF16 = jnp.bfloat16

HEAD_DIM = 64
WINDOW = 128
ROPE_THETA = 10000.0
EPS = 1e-6
N_CHIPS = 4
N_DEV = 8

ADAM_LR = 0.001
ADAM_B1 = 0.9
ADAM_B2 = 0.999
ADAM_EPS = 1e-08
ADAM_WD = 0.01
ADAM_STEP = 10

V7X_VMEM_BYTES = 64 * 1024 * 1024
VMEM_LIMIT = V7X_VMEM_BYTES - 8 * 1024 * 1024
MASK_VALUE = -1e30

_MESH = pl.DeviceIdType.MESH
_HBM = pl.BlockSpec(memory_space=pl.ANY)
_DIMS = {"nn": (((1,), (0,)), ((), ())), "nt": (((1,), (1,)), ((), ())), "tn": (((0,), (0,)), ((), ()))}


def _pick(n, prefs):
    for p in prefs:
        if n % p == 0:
            return p
    return n


class Rider:
    def __init__(self, reads, aliased, news, nsem, build):
        self.reads, self.aliased, self.news, self.nsem, self.build = list(reads), list(aliased), list(news), nsem, build


class _Shifted:
    def __init__(self, ref, off):
        self.ref, self.off = ref, off

    @property
    def at(self):
        return self

    def __getitem__(self, k):
        return self.ref.at[k + self.off]


def combine(*riders):
    def build(reads, al, news, ssem, rsem):
        out = ([], [], [])
        r0 = a0 = n0 = s0 = 0
        for rd in riders:
            nr, na, nn = len(rd.reads), len(rd.aliased), len(rd.news)
            part = rd.build(reads[r0:r0 + nr], al[a0:a0 + na], news[n0:n0 + nn], _Shifted(ssem, s0), _Shifted(rsem, s0))
            for acc, lst in zip(out, part):
                acc.extend(lst)
            r0, a0, n0, s0 = r0 + nr, a0 + na, n0 + nn, s0 + rd.nsem
        return out

    return Rider(sum((r.reads for r in riders), []), sum((r.aliased for r in riders), []), sum((r.news for r in riders), []),
                 sum(r.nsem for r in riders), build)


def _me():
    return lax.axis_index("x"), lax.axis_index("y"), lax.axis_index("c")


def _other_chips(x, y):
    return [(1 - x, y), (x, 1 - y), (1 - x, 1 - y)]


WHOLE = (0, 1, 1)


def _rows(ref, start, rows, part=WHOLE):
    k0, k1, n = part
    assert rows % n == 0, (rows, part)
    idx = (slice(None),) * (len(ref.shape) - 2) + (pl.ds(start + k0 * (rows // n), (k1 - k0) * (rows // n)), slice(None))
    return ref.at[idx]


def _half(ref, h, part=WHOLE):
    rows = ref.shape[-2] // 2
    return _rows(ref, h * rows, rows, part)


def _remote(src, dst, ssem, rsem, k, to):
    return pltpu.make_async_remote_copy(src_ref=src, dst_ref=dst, send_sem=ssem.at[k], recv_sem=rsem.at[k], device_id=to,
                                        device_id_type=_MESH)


def _later(*args):
    return functools.partial(_remote, *args)


def gather(bufs, jobs):
    def build(reads, al, news, ssem, rsem):
        x, y, c = _me()
        p = 2 * x + y
        starts, arrivals = [], []
        for n, (b, kind, part) in enumerate(jobs):
            for j, chip in enumerate(_other_chips(x, y)):
                q = 2 * chip[0] + chip[1]
                if kind == "ici":
                    src, landing, to = _half(al[b].at[p], c, part), _half(al[b].at[q], c, part), (*chip, c)
                else:
                    src, landing, to = _half(al[b].at[q], c, part), _half(al[b].at[q], 1 - c, part), (x, y, 1 - c)
                starts.append(_later(src, src, ssem, rsem, 3 * n + j, to))
                arrivals.append(_later(landing, landing, ssem, rsem, 3 * n + j, to))
        return starts, arrivals, starts

    return Rider([], bufs, [], 3 * len(jobs), build)


def exchange_halves(grads):
    def build(reads, al, news, ssem, rsem):
        x, y, c = _me()
        cps = [_later(_half(g, 1 - c), t, ssem, rsem, w, (x, y, 1 - c)) for w, (g, t) in enumerate(zip(reads, news))]
        return cps, cps, cps

    return Rider(grads, [], [jax.ShapeDtypeStruct((g.shape[0], g.shape[1] // 2, g.shape[2]), g.dtype) for g in grads], len(grads), build)


def scatter_to_owner(sums, gots=None, part=WHOLE):
    def build(reads, al, news, ssem, rsem):
        x, y, c = _me()
        cps = []
        for w, (s, got) in enumerate(zip(reads, al or news)):
            rows = s.shape[-2]
            for j, chip in enumerate(_other_chips(x, y)):
                cps.append(_later(_rows(s.at[2 * chip[0] + chip[1]], 0, rows, part), _rows(got.at[j], 0, rows, part), ssem, rsem,
                                  3 * w + j, (*chip, c)))
        return cps, cps, cps

    news = [] if gots else [jax.ShapeDtypeStruct((3,) + s.shape[1:], s.dtype) for s in sums]
    return Rider(sums, gots or [], news, 3 * len(sums), build)


def join_halves(fulls):
    def build(reads, al, news, ssem, rsem):
        x, y, c = _me()
        starts, arrivals = [], []
        for w, f in enumerate(al):
            mine, landing = _half(f, c), _half(f, 1 - c)
            starts.append(_later(mine, mine, ssem, rsem, w, (x, y, 1 - c)))
            arrivals.append(_later(landing, landing, ssem, rsem, w, (x, y, 1 - c)))
        return starts, arrivals, starts

    return Rider([], fulls, [], len(fulls), build)


def _start_and_wait(rider, reads, al, news, ssem, rsem, first, last):
    @pl.when(first)
    def _():
        for cp in rider.build(reads, al, news, ssem, rsem)[0]:
            cp().start()

    def finish():
        @pl.when(last)
        def _():
            _, arrivals, sends = rider.build(reads, al, news, ssem, rsem)
            for cp in arrivals:
                cp().wait_recv()
            for cp in sends:
                cp().wait_send()

    return finish


def _call(name, body, grid, in_specs, args, out_specs, out_shape, scratch=(), semantics=None, rider=None, prefetch=()):
    n_pre, n_in, n_out, n_scr = len(prefetch), len(args), len(out_shape), len(scratch)
    nr, na, nn = (len(rider.reads), len(rider.aliased), len(rider.news)) if rider else (0, 0, 0)

    def wrapped(*refs):
        pre, refs = refs[:n_pre], refs[n_pre:]
        ins, reads = refs[:n_in], refs[n_in:n_in + nr]
        o0 = n_in + nr + na
        outs, al, news = refs[o0:o0 + n_out], refs[o0 + n_out:o0 + n_out + na], refs[o0 + n_out + na:o0 + n_out + na + nn]
        s0 = o0 + n_out + na + nn
        scr, (ssem, rsem) = refs[s0:s0 + n_scr], refs[s0 + n_scr:]
        first = functools.reduce(jnp.logical_and, [pl.program_id(a) == 0 for a in range(len(grid))])
        last = functools.reduce(jnp.logical_and, [pl.program_id(a) == g - 1 for a, g in enumerate(grid)])
        finish = _start_and_wait(rider, reads, al, news, ssem, rsem, first, last)
        body(*pre, *ins, *outs, *scr)
        finish()

    kernel_fn, all_in, all_out, shapes, scr = body, list(in_specs), list(out_specs), list(out_shape), list(scratch)
    operands, aliases = (*prefetch, *args), {}
    if rider:
        kernel_fn, semantics = wrapped, ("arbitrary",) * len(grid)
        all_in += [_HBM] * (nr + na)
        all_out += [_HBM] * (na + nn)
        shapes += [jax.ShapeDtypeStruct(a.shape, a.dtype) for a in rider.aliased] + rider.news
        scr += [pltpu.SemaphoreType.DMA((rider.nsem,)), pltpu.SemaphoreType.DMA((rider.nsem,))]
        operands += (*rider.reads, *rider.aliased)
        aliases = {n_pre + n_in + nr + i: n_out + i for i in range(na)}
    params = pltpu.CompilerParams(dimension_semantics=semantics, vmem_limit_bytes=VMEM_LIMIT)
    if n_pre:
        spec = pltpu.PrefetchScalarGridSpec(num_scalar_prefetch=n_pre, grid=grid, in_specs=all_in, out_specs=all_out, scratch_shapes=scr)
        outs = pl.pallas_call(kernel_fn, name=name, grid_spec=spec, out_shape=shapes, input_output_aliases=aliases, compiler_params=params)(*operands)
    else:
        outs = pl.pallas_call(kernel_fn, name=name, grid=grid, in_specs=all_in, out_specs=all_out, out_shape=shapes, scratch_shapes=scr,
                              input_output_aliases=aliases, compiler_params=params)(*operands)
    return list(outs[:n_out]), ((list(outs[n_out:n_out + na]), list(outs[n_out + na:])) if rider else None)


def run_step(name, rider):
    nr, na, nn = len(rider.reads), len(rider.aliased), len(rider.news)

    def body(*refs):
        reads = refs[:nr]
        al, news = refs[nr + na:nr + 2 * na], refs[nr + 2 * na:nr + 2 * na + nn]
        ssem, rsem = refs[nr + 2 * na + nn:]
        starts, arrivals, sends = rider.build(reads, al, news, ssem, rsem)
        for cp in starts:
            cp().start()
        for cp in arrivals:
            cp().wait_recv()
        for cp in sends:
            cp().wait_send()

    outs = pl.pallas_call(
        body, name=name, in_specs=[_HBM] * (nr + na), out_specs=[_HBM] * (na + nn),
        out_shape=[jax.ShapeDtypeStruct(a.shape, a.dtype) for a in rider.aliased] + rider.news,
        input_output_aliases={nr + i: i for i in range(na)},
        scratch_shapes=[pltpu.SemaphoreType.DMA((rider.nsem,)), pltpu.SemaphoreType.DMA((rider.nsem,))],
    )(*rider.reads, *rider.aliased)
    return list(outs[:na]), list(outs[na:])


def all_reduce_small(v):
    rows, lanes = v.shape

    def body(v_ref, o_ref, slots, send_sems, recv_sems):
        x, y, c = _me()
        me = 4 * x + 2 * y + c
        slots[me] = v_ref[...]
        cps = []
        for k in range(1, N_DEV):
            peer = (x ^ (k >> 2), y ^ ((k >> 1) & 1), c ^ (k & 1))
            cps.append(_remote(v_ref, slots.at[me], send_sems, recv_sems, k - 1, peer))
            cps[-1].start()
        for k in range(1, N_DEV):
            theirs = slots.at[me ^ k]
            _remote(theirs, theirs, send_sems, recv_sems, k - 1, (x, y, c)).wait_recv()
        for cp in cps:
            cp.wait_send()
        acc = slots[0]
        for i in range(1, N_DEV):
            acc = acc + slots[i]
        o_ref[...] = acc

    return pl.pallas_call(
        body, name="all_reduce_small",
        in_specs=[pl.BlockSpec(memory_space=pltpu.VMEM)], out_specs=pl.BlockSpec(memory_space=pltpu.VMEM),
        out_shape=jax.ShapeDtypeStruct((rows, lanes), F32),
        scratch_shapes=[pltpu.VMEM((N_DEV, rows, lanes), F32), pltpu.SemaphoreType.DMA((N_DEV - 1,)), pltpu.SemaphoreType.DMA((N_DEV - 1,))],
    )(v)


def _mm_call(name, mode, a, b, a_spec, b_spec, out_shape, out_spec, grid, acc_shape, scale=1.0, resid=None, resid_spec=None, rider=None):
    nk = grid[2]
    dims = _DIMS[mode]
    has_resid = resid is not None

    def body(*refs):
        a_ref, b_ref = refs[:2]
        r_ref = refs[2] if has_resid else None
        o_ref = refs[3] if has_resid else refs[2]

        def finish(r):
            if scale != 1.0:
                r = r * scale
            if has_resid:
                r = r_ref[...].astype(F32) + r
            o_ref[...] = r.astype(o_ref.dtype)

        part = lax.dot_general(a_ref[...].astype(BF16), b_ref[...].astype(BF16), dims, preferred_element_type=F32)
        if nk == 1:
            finish(part)
            return
        acc_ref = refs[-1]
        k = pl.program_id(2)

        @pl.when(k == 0)
        def _():
            acc_ref[...] = part

        @pl.when(k > 0)
        def _():
            acc_ref[...] += part

        @pl.when(k == nk - 1)
        def _():
            finish(acc_ref[...])

    in_specs = [a_spec, b_spec] + ([resid_spec] if has_resid else [])
    args = (a, b) + ((resid,) if has_resid else ())
    (out,), rid = _call(name, body, grid, in_specs, args, [out_spec], [out_shape], [pltpu.VMEM(acc_shape, F32)] if nk > 1 else [],
                        ("parallel", "parallel", "arbitrary"), rider)
    return out, rid


MM_VMEM_BUDGET = 40 * 1024 * 1024
_TILE_OPTS = (2048, 1408, 1024, 512, 256, 128)


def _tiles(m, n, kd, a_item, b_item, o_item, r_item=0, tm=None, tn=None, tk=None):
    def opts(full, fixed, cap):
        return [fixed] if fixed else [t for t in _TILE_OPTS if t <= cap and full % t == 0] or [full]

    best = None
    for cm in opts(m, tm, 1408):
        for cn in opts(n, tn, 1408):
            for ck in opts(kd, tk, 2048):
                blocks = cm * ck * a_item + ck * cn * b_item + cm * cn * (o_item + r_item)
                casts = (cm * ck * 2 if a_item == 4 else 0) + (ck * cn * 2 if b_item == 4 else 0)
                if 2 * blocks + cm * cn * 4 + casts <= MM_VMEM_BUDGET:
                    key = (cm * cn * ck, ck)
                    if best is None or key > best[0]:
                        best = (key, (cm, cn, ck))
    assert best is not None, (m, n, kd)
    return best[1]


def _item(x):
    return jnp.dtype(x.dtype).itemsize


def mm_nn(name, a, b, *, out_dtype=F32, scale=1.0, resid=None, rider=None):
    m, kd = a.shape
    n = b.shape[1]
    tm, tn, tk = _tiles(m, n, kd, _item(a), _item(b), jnp.dtype(out_dtype).itemsize, 0 if resid is None else _item(resid))
    o_spec = pl.BlockSpec((tm, tn), lambda i, j, k: (i, j))
    return _mm_call(
        name, "nn", a, b, pl.BlockSpec((tm, tk), lambda i, j, k: (i, k)), pl.BlockSpec((tk, tn), lambda i, j, k: (k, j)),
        jax.ShapeDtypeStruct((m, n), out_dtype), o_spec, (m // tm, n // tn, kd // tk), (tm, tn), scale, resid, o_spec, rider)


def mm_nt(name, a, b, *, out_dtype=F32, scale=1.0, resid=None, rider=None):
    m, kd = a.shape
    n = b.shape[0]
    tm, tn, tk = _tiles(m, n, kd, _item(a), _item(b), jnp.dtype(out_dtype).itemsize, 0 if resid is None else _item(resid))
    o_spec = pl.BlockSpec((tm, tn), lambda i, j, k: (i, j))
    return _mm_call(
        name, "nt", a, b, pl.BlockSpec((tm, tk), lambda i, j, k: (i, k)), pl.BlockSpec((tn, tk), lambda i, j, k: (j, k)),
        jax.ShapeDtypeStruct((m, n), out_dtype), o_spec, (m // tm, n // tn, kd // tk), (tm, tn), scale, resid, o_spec, rider)


def mm_tn(name, a, b, *, out_dtype=F32, scale=1.0, rider=None):
    kd, m = a.shape
    n = b.shape[1]
    tm, tn, tk = _tiles(m, n, kd, _item(a), _item(b), jnp.dtype(out_dtype).itemsize)
    return _mm_call(
        name, "tn", a, b, pl.BlockSpec((tk, tm), lambda i, j, k: (k, i)), pl.BlockSpec((tk, tn), lambda i, j, k: (k, j)),
        jax.ShapeDtypeStruct((m, n), out_dtype), pl.BlockSpec((tm, tn), lambda i, j, k: (i, j)),
        (m // tm, n // tn, kd // tk), (tm, tn), scale, rider=rider)


def mm_nn_sharded(name, a, w, *, rider=None):
    m, kd = a.shape
    ns, _, c = w.shape
    tm, _, tk = _tiles(m, c, kd, _item(a), _item(w), 4, tn=c)
    return _mm_call(
        name, "nn", a, w, pl.BlockSpec((tm, tk), lambda i, j, k: (i, k)), pl.BlockSpec((None, tk, c), lambda i, j, k: (j, k, 0)),
        jax.ShapeDtypeStruct((ns, m, c), F32), pl.BlockSpec((None, tm, c), lambda i, j, k: (j, i, 0)),
        (m // tm, ns, kd // tk), (tm, c), rider=rider)


def mm_nt_sharded(name, a, w, *, resid=None, rider=None):
    ns, n, c = w.shape
    m = a.shape[-2]
    tm, tn, _ = _tiles(m, n, c, _item(a), _item(w), 4, 0 if resid is None else _item(resid), tk=c)
    o_spec = pl.BlockSpec((tm, tn), lambda i, j, k: (i, j))
    a_spec = pl.BlockSpec((tm, c), lambda i, j, k: (i, k)) if a.ndim == 2 else pl.BlockSpec((None, tm, c), lambda i, j, k: (k, i, 0))
    return _mm_call(
        name, "nt", a, w, a_spec, pl.BlockSpec((None, tn, c), lambda i, j, k: (k, j, 0)),
        jax.ShapeDtypeStruct((m, n), F32), o_spec, (m // tm, n // tn, ns), (tm, tn), 1.0, resid, o_spec, rider)


def mm_tn_sharded(name, a, b, ns, *, rider=None):
    kd, m = a.shape
    c = b.shape[-1] // (ns if b.ndim == 2 else 1)
    tm, _, tk = _tiles(m, c, kd, _item(a), _item(b), 2, tn=c)
    b_spec = pl.BlockSpec((tk, c), lambda i, j, k: (k, j)) if b.ndim == 2 else pl.BlockSpec((None, tk, c), lambda i, j, k: (j, k, 0))
    return _mm_call(
        name, "tn", a, b, pl.BlockSpec((tk, tm), lambda i, j, k: (k, i)), b_spec,
        jax.ShapeDtypeStruct((ns, m, c), BF16), pl.BlockSpec((None, tm, c), lambda i, j, k: (j, i, 0)),
        (m // tm, ns, kd // tk), (tm, c), rider=rider)


def rms_fwd(name, x, g, out_dtype, rider=None):
    r, c = x.shape
    tm = _pick(r, (512, 256, 128, 64, 8))

    def body(x_ref, g_ref, y_ref, r_ref):
        xf = x_ref[...].astype(F32)
        rstd = lax.rsqrt(jnp.mean(xf * xf, axis=-1, keepdims=True) + EPS)
        y_ref[...] = ((xf * rstd) * g_ref[...]).astype(y_ref.dtype)
        r_ref[...] = rstd

    (y, rstd), rid = _call(
        name, body, (r // tm,), [pl.BlockSpec((tm, c), lambda i: (i, 0)), pl.BlockSpec((1, c), lambda i: (0, 0))], (x, g.reshape(1, c)),
        [pl.BlockSpec((tm, c), lambda i: (i, 0)), pl.BlockSpec((tm, 1), lambda i: (i, 0))],
        [jax.ShapeDtypeStruct((r, c), out_dtype), jax.ShapeDtypeStruct((r, 1), F32)], (), ("parallel",), rider)
    return (y, rstd) if rider is None else (y, rstd, rid)


def rms_bwd(name, x, g, rstd, dy, dres=None):
    r, c = x.shape
    tm = _pick(r, (512, 256, 128, 64, 8))
    has_res = dres is not None

    def body(*refs):
        if has_res:
            x_ref, g_ref, r_ref, dy_ref, dres_ref, dx_ref, dg_ref = refs
        else:
            x_ref, g_ref, r_ref, dy_ref, dx_ref, dg_ref = refs
        xhat = x_ref[...].astype(F32) * r_ref[...]
        dyf = dy_ref[...].astype(F32)
        gdy = dyf * g_ref[...]
        dx = r_ref[...] * (gdy - xhat * jnp.mean(gdy * xhat, axis=-1, keepdims=True))
        if has_res:
            dx = dx + dres_ref[...]
        dx_ref[...] = dx

        @pl.when(pl.program_id(0) == 0)
        def _():
            dg_ref[...] = jnp.zeros_like(dg_ref)

        dg_ref[...] += jnp.sum(dyf * xhat, axis=0, keepdims=True)

    row = pl.BlockSpec((tm, c), lambda i: (i, 0))
    in_specs = [row, pl.BlockSpec((1, c), lambda i: (0, 0)), pl.BlockSpec((tm, 1), lambda i: (i, 0)), row] + ([row] if has_res else [])
    args = (x, g.reshape(1, c), rstd, dy) + ((dres,) if has_res else ())
    (dx, dg), _ = _call(name, body, (r // tm,), in_specs, args, [row, pl.BlockSpec((1, c), lambda i: (0, 0))],
                        [jax.ShapeDtypeStruct((r, c), F32), jax.ShapeDtypeStruct((1, c), F32)], (), ("arbitrary",))
    return dx, dg.reshape(c)


_LANES = 128


def _head_mean(v):
    if v.shape[1] == HEAD_DIM:
        return jnp.mean(v, axis=-1, keepdims=True)
    low = lax.broadcasted_iota(jnp.int32, v.shape, 1) < HEAD_DIM
    lo = jnp.sum(jnp.where(low, v, 0.0), axis=-1, keepdims=True)
    hi = jnp.sum(jnp.where(low, 0.0, v), axis=-1, keepdims=True)
    return jnp.where(low, lo, hi) * (1.0 / HEAD_DIM)


def _head_groups(c):
    width = _LANES if c % _LANES == 0 else HEAD_DIM
    assert c % width == 0, c
    return width, [slice(k * width, (k + 1) * width) for k in range(c // width)]


def _head_gain(g, width):
    return jnp.tile(g.reshape(1, HEAD_DIM), (1, width // HEAD_DIM))


def head_rms_fwd(name, x, g):
    s, c = x.shape
    tm = _pick(s, (256, 128, 8))
    width, groups = _head_groups(c)

    def body(x_ref, g_ref, y_ref):
        for sl in groups:
            xs = x_ref[:, sl]
            y_ref[:, sl] = (xs * lax.rsqrt(_head_mean(xs * xs) + EPS)) * g_ref[...]

    row = pl.BlockSpec((tm, c), lambda i: (i, 0))
    (y,), _ = _call(name, body, (s // tm,), [row, pl.BlockSpec((1, width), lambda i: (0, 0))], (x, _head_gain(g, width)), [row],
                    [jax.ShapeDtypeStruct((s, c), F32)], (), ("parallel",))
    return y


def head_rms_bwd(name, x, g, dy):
    s, c = x.shape
    tm = _pick(s, (256, 128, 8))
    width, groups = _head_groups(c)

    def body(x_ref, g_ref, dy_ref, dx_ref, dg_ref):
        @pl.when(pl.program_id(0) == 0)
        def _():
            dg_ref[...] = jnp.zeros_like(dg_ref)

        for sl in groups:
            xs, dys = x_ref[:, sl], dy_ref[:, sl]
            rstd = lax.rsqrt(_head_mean(xs * xs) + EPS)
            xhat = xs * rstd
            gdy = dys * g_ref[...]
            dx_ref[:, sl] = rstd * (gdy - xhat * _head_mean(gdy * xhat))
            dg_ref[...] += jnp.sum(dys * xhat, axis=0, keepdims=True)

    row = pl.BlockSpec((tm, c), lambda i: (i, 0))
    vec = pl.BlockSpec((1, width), lambda i: (0, 0))
    (dx, dg), _ = _call(name, body, (s // tm,), [row, vec, row], (x, _head_gain(g, width), dy), [row, vec],
                        [jax.ShapeDtypeStruct((s, c), F32), jax.ShapeDtypeStruct((1, width), F32)], (), ("arbitrary",))
    return dx, jnp.sum(dg.reshape(width // HEAD_DIM, HEAD_DIM), axis=0)


@functools.partial(jax.custom_vjp, nondiff_argnums=(0,))
def head_rms(name, x, g):
    return head_rms_fwd(name + "_fwd", x, g)


def _head_rms_fwd(name, x, g):
    return head_rms_fwd(name + "_fwd", x, g), (x, g)


def _head_rms_bwd(name, res, dy):
    return head_rms_bwd(name + "_bwd", *res, dy)


head_rms.defvjp(_head_rms_fwd, _head_rms_bwd)


FFN_TM = 512


def _sigmoid(x):
    return 1.0 / (1.0 + jnp.exp(-x))


def ffn_gu(name, xn, wg, wu, rider=None):
    s, d = xn.shape
    ns, _, c = wg.shape
    tm = _pick(s, (FFN_TM, 128))

    def body(x_ref, wg_ref, wu_ref, g_ref, u_ref, h_ref):
        xb = x_ref[...]
        gv = jnp.dot(xb, wg_ref[...], preferred_element_type=F32)
        uv = jnp.dot(xb, wu_ref[...], preferred_element_type=F32)
        g_ref[...] = gv
        u_ref[...] = uv
        h_ref[...] = ((gv * _sigmoid(gv)) * uv).astype(BF16)

    w_spec = pl.BlockSpec((None, d, c), lambda j, i: (j, 0, 0))
    o_spec = pl.BlockSpec((tm, c), lambda j, i: (i, j))
    return _call(
        name, body, (ns, s // tm), [pl.BlockSpec((tm, d), lambda j, i: (i, 0)), w_spec, w_spec], (xn, wg, wu),
        [o_spec, o_spec, o_spec],
        [jax.ShapeDtypeStruct((s, ns * c), F32), jax.ShapeDtypeStruct((s, ns * c), F32), jax.ShapeDtypeStruct((s, ns * c), BF16)],
        [], ("parallel", "parallel"), rider)


def ffn_dh(name, dy, wd, g, u, ns, scale, rider=None):
    s, d = dy.shape
    f = wd.shape[0]
    c = f // ns
    tm = _pick(s, (FFN_TM, 128))

    def body(dy_ref, wd_ref, g_ref, u_ref, dg_ref, du_ref):
        dh = lax.dot_general(dy_ref[...].astype(BF16), wd_ref[...], _DIMS["nt"], preferred_element_type=F32) * scale
        gv, uv = g_ref[...], u_ref[...]
        sig = _sigmoid(gv)
        dg_ref[...] = (dh * uv * (sig * (1.0 + gv * (1.0 - sig)))).astype(BF16)
        du_ref[...] = (dh * (gv * sig)).astype(BF16)

    o_spec = pl.BlockSpec((tm, c), lambda j, i: (i, j))
    return _call(
        name, body, (ns, s // tm),
        [pl.BlockSpec((tm, d), lambda j, i: (i, 0)), pl.BlockSpec((c, d), lambda j, i: (j, 0)), o_spec, o_spec], (dy, wd, g, u),
        [o_spec, o_spec], [jax.ShapeDtypeStruct((s, f), BF16), jax.ShapeDtypeStruct((s, f), BF16)],
        [], ("parallel", "parallel"), rider)


FOX_TQ = 512


def fox_tile(s_len):
    return min(FOX_TQ, s_len)


def _heads_per_block(h):
    return 2 if h % 2 == 0 else 1


def _fox_scores(qb, kc, cq, ck, i, c, tq):
    s = lax.dot_general(qb, kc.astype(BF16), _DIMS["nt"], preferred_element_type=F32) * (HEAD_DIM ** -0.5) + cq - ck
    row = lax.broadcasted_iota(jnp.int32, (tq, tq), 0) + (i - c) * tq
    col = lax.broadcasted_iota(jnp.int32, (tq, tq), 1)
    return jnp.where(row >= col, s, MASK_VALUE)


def _fox_specs(h, s_len, tq):
    hb = _heads_per_block(h)
    qb = pl.BlockSpec((tq, hb * HEAD_DIM), lambda pp, i: (i, pp))
    kb = pl.BlockSpec((s_len, hb * HEAD_DIM), lambda pp, i: (0, pp))
    colb = pl.BlockSpec((hb, tq, 1), lambda pp, i: (pp, i, 0))
    rowb = pl.BlockSpec((hb, s_len // tq, 1, tq), lambda pp, i: (pp, 0, 0, 0))
    return hb, qb, kb, colb, rowb


def fox_fwd(q, k, v, cq, ck, rider=None):
    s_len, hd = q.shape
    h, d = hd // HEAD_DIM, HEAD_DIM
    tq = fox_tile(s_len)
    hb, qb, kb, colb, rowb = _fox_specs(h, s_len, tq)

    def body(q_ref, k_ref, v_ref, cq_ref, ck_ref, o_ref, lse_ref):
        i = pl.program_id(1)
        for hh in range(hb):
            lanes = slice(hh * d, (hh + 1) * d)
            qh, cqv = q_ref[:, lanes].astype(BF16), cq_ref[hh]

            def chunk(c, carry):
                m, l, acc = carry
                rows = pl.ds(pl.multiple_of(c * tq, tq), tq)
                s = _fox_scores(qh, k_ref[rows, lanes], cqv, ck_ref[hh, c], i, c, tq)
                m_new = jnp.maximum(m, jnp.max(s, axis=-1, keepdims=True))
                alpha = jnp.exp(m - m_new)
                p = jnp.exp(s - m_new)
                acc = alpha * acc + jnp.dot(p.astype(BF16), v_ref[rows, lanes].astype(BF16), preferred_element_type=F32)
                return m_new, alpha * l + jnp.sum(p, axis=-1, keepdims=True), acc

            init = (jnp.full((tq, 1), MASK_VALUE, F32), jnp.zeros((tq, 1), F32), jnp.zeros((tq, d), F32))
            m, l, acc = lax.fori_loop(0, i + 1, chunk, init)
            o_ref[:, lanes] = acc / l
            lse_ref[hh] = m + jnp.log(l)

    return _call(
        "fox_fwd", body, (h // hb, s_len // tq), [qb, kb, kb, colb, rowb], (q, k, v, cq, ck), [qb, colb],
        [jax.ShapeDtypeStruct((s_len, hd), F32), jax.ShapeDtypeStruct((h, s_len, 1), F32)], (), ("parallel", "parallel"), rider)


def fox_bwd(q, k, v, cq, ck, o, lse, do, rider=None):
    s_len, hd = q.shape
    h, d = hd // HEAD_DIM, HEAD_DIM
    tq = fox_tile(s_len)
    scale = HEAD_DIM ** -0.5
    hb, qb, kb, colb, rowb = _fox_specs(h, s_len, tq)

    def body(q_ref, k_ref, v_ref, cq_ref, ck_ref, o_ref, lse_ref, do_ref, dq_ref, dk_ref, dv_ref, dcq_ref, dck_ref):
        i = pl.program_id(1)

        @pl.when(i == 0)
        def _():
            dk_ref[...] = jnp.zeros_like(dk_ref)
            dv_ref[...] = jnp.zeros_like(dv_ref)
            dck_ref[...] = jnp.zeros_like(dck_ref)

        for hh in range(hb):
            lanes = slice(hh * d, (hh + 1) * d)
            qh, cqv, lse_h = q_ref[:, lanes].astype(BF16), cq_ref[hh], lse_ref[hh]
            dof = do_ref[:, lanes]
            dob = dof.astype(BF16)
            delta = jnp.sum(dof * o_ref[:, lanes], axis=-1, keepdims=True)

            def chunk(c, carry):
                dq, dcq = carry
                rows = pl.ds(pl.multiple_of(c * tq, tq), tq)
                p = jnp.exp(_fox_scores(qh, k_ref[rows, lanes], cqv, ck_ref[hh, c], i, c, tq) - lse_h)
                dp = lax.dot_general(dob, v_ref[rows, lanes].astype(BF16), _DIMS["nt"], preferred_element_type=F32)
                ds = p * (dp - delta)
                dsb = ds.astype(BF16)
                dv_ref[rows, lanes] += lax.dot_general(p.astype(BF16), dob, _DIMS["tn"], preferred_element_type=F32)
                dk_ref[rows, lanes] += lax.dot_general(dsb, qh, _DIMS["tn"], preferred_element_type=F32) * scale
                dck_ref[hh, c] -= jnp.sum(ds, axis=0, keepdims=True)
                dq = dq + jnp.dot(dsb, k_ref[rows, lanes].astype(BF16), preferred_element_type=F32)
                return dq, dcq + jnp.sum(ds, axis=-1, keepdims=True)

            dq, dcq = lax.fori_loop(0, i + 1, chunk, (jnp.zeros((tq, d), F32), jnp.zeros((tq, 1), F32)))
            dq_ref[:, lanes] = dq * scale
            dcq_ref[hh] = dcq

    return _call(
        "fox_bwd", body, (h // hb, s_len // tq), [qb, kb, kb, colb, rowb, qb, colb, qb], (q, k, v, cq, ck, o, lse, do),
        [qb, kb, kb, colb, rowb],
        [jax.ShapeDtypeStruct((s_len, hd), F32)] * 3
        + [jax.ShapeDtypeStruct((h, s_len, 1), F32), jax.ShapeDtypeStruct((h, s_len // tq, 1, tq), F32)],
        (), ("parallel", "arbitrary"), rider)


def _stack_heads(ref, first, g):
    return jnp.concatenate([ref[:, (first + j) * HEAD_DIM:(first + j + 1) * HEAD_DIM] for j in range(g)], axis=0)


def _window(prev_ref, cur_ref, hh):
    lanes = slice(hh * HEAD_DIM, (hh + 1) * HEAD_DIM)
    return jnp.concatenate([prev_ref[:, lanes], cur_ref[:, lanes]], axis=0).astype(BF16)


def _swa_probs(q, kw, sink, n, w):
    rows = q.shape[0]
    s = lax.dot_general(q, kw, _DIMS["nt"], preferred_element_type=F32) * (HEAD_DIM ** -0.5)
    t = lax.broadcasted_iota(jnp.int32, (rows, 2 * w), 0) & (w - 1)
    col = lax.broadcasted_iota(jnp.int32, (rows, 2 * w), 1)
    rel = t + w - col
    valid = (rel >= 0) & (rel < w) & ((col >= w) | (n > 0))
    s = jnp.where(valid, s, MASK_VALUE)
    m = jnp.maximum(jnp.max(s, axis=-1, keepdims=True), sink)
    p = jnp.exp(s - m)
    ps = jnp.exp(sink - m)
    linv = 1.0 / (jnp.sum(p, axis=-1, keepdims=True) + ps)
    return p * linv, ps * linv


def _swa_specs(hk, g, s_len):
    w = WINDOW
    assert w & (w - 1) == 0 and s_len % w == 0
    hb = _heads_per_block(hk)
    qb = pl.BlockSpec((w, hb * g * HEAD_DIM), lambda pp, n: (n, pp))
    prev = pl.BlockSpec((w, hb * HEAD_DIM), lambda pp, n: (jnp.maximum(n - 1, 0), pp))
    cur = pl.BlockSpec((w, hb * HEAD_DIM), lambda pp, n: (n, pp))
    sb = pl.BlockSpec((hb, g * w, 1), lambda pp, n: (pp, 0, 0))
    return hb, qb, prev, cur, sb


def swa_fwd(q, k, v, sink, rider=None):
    s_len = q.shape[0]
    hk = k.shape[1] // HEAD_DIM
    g = q.shape[1] // k.shape[1]
    w, d = WINDOW, HEAD_DIM
    hb, qb, prev, cur, sb = _swa_specs(hk, g, s_len)

    def body(q_ref, kp_ref, kc_ref, vp_ref, vc_ref, sink_ref, o_ref):
        for hh in range(hb):
            qs = _stack_heads(q_ref, hh * g, g).astype(BF16)
            p, _ = _swa_probs(qs, _window(kp_ref, kc_ref, hh), sink_ref[hh], pl.program_id(1), w)
            o = jnp.dot(p.astype(BF16), _window(vp_ref, vc_ref, hh), preferred_element_type=F32)
            for j in range(g):
                o_ref[:, (hh * g + j) * d:(hh * g + j + 1) * d] = o[j * w:(j + 1) * w]

    (o,), rid = _call("swa_fwd", body, (hk // hb, s_len // w), [qb, prev, cur, prev, cur, sb], (q, k, k, v, v, sink), [qb],
                      [jax.ShapeDtypeStruct(q.shape, F32)], (), ("parallel", "parallel"), rider)
    return o, rid


def swa_bwd(q, k, v, sink, o, do, rider=None):
    s_len = q.shape[0]
    hk = k.shape[1] // HEAD_DIM
    g = q.shape[1] // k.shape[1]
    w, d = WINDOW, HEAD_DIM
    scale = HEAD_DIM ** -0.5
    hb, qb, prev, cur, sb = _swa_specs(hk, g, s_len)

    def body(q_ref, kp_ref, kc_ref, vp_ref, vc_ref, sink_ref, o_ref, do_ref, dq_ref, dkp_ref, dkc_ref, dvp_ref, dvc_ref, dsink_ref):
        n = pl.program_id(1)

        @pl.when(n == 0)
        def _():
            dsink_ref[...] = jnp.zeros_like(dsink_ref)

        for hh in range(hb):
            lanes = slice(hh * d, (hh + 1) * d)
            qs = _stack_heads(q_ref, hh * g, g).astype(BF16)
            kw, vw = _window(kp_ref, kc_ref, hh), _window(vp_ref, vc_ref, hh)
            p, ps = _swa_probs(qs, kw, sink_ref[hh], n, w)
            dof = _stack_heads(do_ref, hh * g, g)
            dob = dof.astype(BF16)
            delta = jnp.sum(dof * _stack_heads(o_ref, hh * g, g), axis=-1, keepdims=True)
            dp = lax.dot_general(dob, vw, _DIMS["nt"], preferred_element_type=F32)
            ds = p * (dp - delta)
            dsb = ds.astype(BF16)
            dsink_ref[hh] -= ps * delta
            dq = jnp.dot(dsb, kw, preferred_element_type=F32) * scale
            for j in range(g):
                dq_ref[:, (hh * g + j) * d:(hh * g + j + 1) * d] = dq[j * w:(j + 1) * w]
            dkw = lax.dot_general(dsb, qs, _DIMS["tn"], preferred_element_type=F32) * scale
            dvw = lax.dot_general(p.astype(BF16), dob, _DIMS["tn"], preferred_element_type=F32)
            dkp_ref[:, lanes] = dkw[:w]
            dkc_ref[:, lanes] = dkw[w:]
            dvp_ref[:, lanes] = dvw[:w]
            dvc_ref[:, lanes] = dvw[w:]

    kv_shape = jax.ShapeDtypeStruct(k.shape, F32)
    (dq, dkp, dkc, dvp, dvc, dsink), rid = _call(
        "swa_bwd", body, (hk // hb, s_len // w), [qb, prev, cur, prev, cur, sb, qb, qb], (q, k, k, v, v, sink, o, do),
        [qb, cur, cur, cur, cur, sb],
        [jax.ShapeDtypeStruct(q.shape, F32), kv_shape, kv_shape, kv_shape, kv_shape, jax.ShapeDtypeStruct((hk, g * w, 1), F32)],
        (), ("parallel", "arbitrary"), rider)

    def shift_up(a):
        return jnp.concatenate([a[w:], jnp.zeros_like(a[:w])], axis=0)

    return (dq, dkc + shift_up(dkp), dvc + shift_up(dvp), dsink), rid


def loss_call(y, target):
    s, d = y.shape
    tm = _pick(s, (512, 256, 128))

    def body(y_ref, t_ref, l_ref, dy_ref):
        e = y_ref[...] - t_ref[...]
        dy_ref[...] = e * (1.0 / d)

        @pl.when(pl.program_id(0) == 0)
        def _():
            l_ref[...] = jnp.zeros_like(l_ref)

        l_ref[...] += jnp.sum(jnp.sum(e * e, axis=0, keepdims=True), axis=1, keepdims=True) * (0.5 / d)

    row = pl.BlockSpec((tm, d), lambda i: (i, 0))
    (l, dy), _ = _call("loss_head", body, (s // tm,), [row, row], (y, target), [pl.BlockSpec((1, 1), lambda i: (0, 0)), row],
                       [jax.ShapeDtypeStruct((1, 1), F32), jax.ShapeDtypeStruct((s, d), F32)], (), ("arbitrary",))
    return l[0, 0], dy


def _row_tile(rows, cols, itemsize):
    target = max(16, (1 << 20) // (cols * itemsize))
    for t in (1024, 512, 256, 128, 64, 32, 16):
        if t <= target and rows % t == 0:
            return t
    return rows


CAST_STEPS = 8


def cast_place(name, ws, p_idx, rider=None):
    n = len(ws)
    assert all(w.shape[0] % (16 * CAST_STEPS) == 0 for w in ws), [w.shape for w in ws]

    def body(p_ref, *refs):
        for w_ref, o_ref in zip(refs[:n], refs[n:]):
            o_ref[...] = w_ref[...].astype(BF16)

    return _call(
        name, body, (CAST_STEPS,), [pl.BlockSpec((w.shape[0] // CAST_STEPS, w.shape[1]), lambda i, pr: (i, 0)) for w in ws], tuple(ws),
        [pl.BlockSpec((None, w.shape[0] // CAST_STEPS, w.shape[1]), lambda i, pr: (pr[0], i, 0)) for w in ws],
        [jax.ShapeDtypeStruct((N_CHIPS,) + w.shape, BF16) for w in ws], (), ("parallel",), rider, prefetch=(p_idx,))


def chip_sum(name, grad, theirs, c_idx):
    ns, r, cols = grad.shape
    rh = r // 2
    tr = _row_tile(rh, cols, 2)
    nb = rh // tr

    def body(c_ref, a_ref, b_ref, o_ref):
        o_ref[...] = (a_ref[...].astype(F32) + b_ref[...].astype(F32)).astype(o_ref.dtype)

    return pl.pallas_call(
        body, name=name,
        grid_spec=pltpu.PrefetchScalarGridSpec(
            num_scalar_prefetch=1, grid=(ns, nb),
            in_specs=[pl.BlockSpec((None, tr, cols), lambda q, i, cr: (q, cr[0] * nb + i, 0)),
                      pl.BlockSpec((None, tr, cols), lambda q, i, cr: (q, i, 0))],
            out_specs=pl.BlockSpec((None, tr, cols), lambda q, i, cr: (q, i, 0))),
        out_shape=jax.ShapeDtypeStruct((ns, rh, cols), BF16),
        compiler_params=pltpu.CompilerParams(dimension_semantics=("parallel", "parallel"), vmem_limit_bytes=VMEM_LIMIT),
    )(c_idx, grad, theirs)


def owner_sum(name, sums, got, pc_idx):
    ns, rh, cols = sums.shape
    tr = _row_tile(rh, cols, 4)
    nb = rh // tr

    def body(pc_ref, a_ref, b_ref, o_ref):
        o_ref[...] = ((a_ref[...].astype(F32) + b_ref[0].astype(F32)) + b_ref[1].astype(F32)) + b_ref[2].astype(F32)

    return pl.pallas_call(
        body, name=name,
        grid_spec=pltpu.PrefetchScalarGridSpec(
            num_scalar_prefetch=1, grid=(nb,),
            in_specs=[pl.BlockSpec((None, tr, cols), lambda i, pc: (pc[0], i, 0)),
                      pl.BlockSpec((3, tr, cols), lambda i, pc: (0, i, 0))],
            out_specs=pl.BlockSpec((tr, cols), lambda i, pc: (pc[1] * nb + i, 0))),
        out_shape=jax.ShapeDtypeStruct((2 * rh, cols), F32),
        compiler_params=pltpu.CompilerParams(dimension_semantics=("parallel",), vmem_limit_bytes=VMEM_LIMIT),
    )(pc_idx, sums, got)


def adamw(name, w, g, m, v):
    r, cols = w.shape
    tr = _row_tile(r, cols, 4)
    c1 = 1.0 / (1.0 - ADAM_B1 ** ADAM_STEP)
    c2 = 1.0 / (1.0 - ADAM_B2 ** ADAM_STEP)

    def body(w_ref, g_ref, m_ref, v_ref, go_ref, d_ref, nm_ref, nv_ref):
        gv = g_ref[...]
        nm = ADAM_B1 * m_ref[...] + (1.0 - ADAM_B1) * gv
        nv = ADAM_B2 * v_ref[...] + (1.0 - ADAM_B2) * (gv * gv)
        go_ref[...] = gv
        d_ref[...] = -ADAM_LR * ((nm * c1) / (jnp.sqrt(nv * c2) + ADAM_EPS) + ADAM_WD * w_ref[...])
        nm_ref[...] = nm
        nv_ref[...] = nv

    blk = pl.BlockSpec((tr, cols), lambda i: (i, 0))
    return _call(name, body, (r // tr,), [blk] * 4, (w, g, m, v), [blk] * 4, [jax.ShapeDtypeStruct((r, cols), F32)] * 4, (), ("parallel",))


def _rope(x, cos, sin):
    x3 = x.reshape(x.shape[0], -1, HEAD_DIM)
    x1, x2 = x3[..., : HEAD_DIM // 2], x3[..., HEAD_DIM // 2:]
    cos, sin = cos[:, None, :], sin[:, None, :]
    return jnp.concatenate([x1 * cos - x2 * sin, x2 * cos + x1 * sin], axis=-1).reshape(x.shape)


def _win_layout(d_model):
    hf = hq = d_model // (2 * HEAD_DIM)
    hk = hq // 4
    sizes = [hf * HEAD_DIM, hf * HEAD_DIM, hf * HEAD_DIM, hf, hq * HEAD_DIM, hk * HEAD_DIM, hk * HEAD_DIM]
    return hf, hq, hk, sizes


def _pad_lanes(a):
    return jnp.pad(a, ((0, 0), (0, -a.shape[1] % _LANES)))


def _attn_inputs(proj, sm, positions):
    ns, s_len, _ = proj.shape
    hf, hq, hk, sizes = _win_layout(sm["norm_mix_g"].shape[0])
    grp = hq // hk
    cs = sum(sizes) // ns
    full = proj[:, :, :cs].transpose(1, 0, 2).reshape(s_len, ns * cs)
    o0 = 0
    cols = []
    for sz in sizes:
        cols.append(full[:, o0:o0 + sz])
        o0 += sz
    q_f, k_f, v_f, f_logit, q_s, k_s, v_s = cols

    q_f = head_rms("fox_qnorm", q_f, sm["fox_q_norm_g"])
    k_f = head_rms("fox_knorm", k_f, sm["fox_k_norm_g"])
    log_f = jax.nn.log_sigmoid(f_logit + sm["b_forget"])
    c = jnp.cumsum(log_f, axis=0).T

    inv_freq = ROPE_THETA ** (-jnp.arange(0, HEAD_DIM, 2, dtype=F32) / HEAD_DIM)
    ang = positions.astype(F32)[:, None] * inv_freq
    cos, sin = jnp.cos(ang), jnp.sin(ang)
    q_s = _rope(head_rms("swa_qnorm", q_s, sm["swa_q_norm_g"]), cos, sin)
    k_s = _rope(head_rms("swa_knorm", k_s, sm["swa_k_norm_g"]), cos, sin)
    sink = jnp.broadcast_to(sm["swa_sinks"].reshape(hk, grp, 1, 1), (hk, grp, WINDOW, 1)).reshape(hk, grp * WINDOW, 1)
    tq = fox_tile(s_len)
    return (q_f, k_f, v_f, c[:, :, None], c.reshape(hf, s_len // tq, 1, tq)), (q_s, k_s, v_s, sink)


_BIG = ("ffn1_w_gate", "ffn1_w_up", "ffn1_w_down", "w_in", "w_out", "ffn2_w_gate", "ffn2_w_up", "ffn2_w_down")
_ROW_SHARDED = ("ffn1_w_down", "w_out", "ffn2_w_down")
_SMALL = ("norm_ffn1_g", "norm_mix_g", "b_forget", "fox_q_norm_g", "fox_k_norm_g", "swa_q_norm_g", "swa_k_norm_g", "swa_sinks",
          "out_norm_fox_g", "out_norm_swa_g", "norm_ffn2_g")
_ATTN_SMALL = ("norm_mix_g", "b_forget", "fox_q_norm_g", "fox_k_norm_g", "swa_q_norm_g", "swa_k_norm_g", "swa_sinks")
_ALL = ("norm_ffn1_g", "ffn1_w_gate", "ffn1_w_up", "ffn1_w_down", "norm_mix_g", "w_in", "b_forget", "fox_q_norm_g", "fox_k_norm_g",
        "swa_q_norm_g", "swa_k_norm_g", "swa_sinks", "out_norm_fox_g", "out_norm_swa_g", "w_out", "norm_ffn2_g", "ffn2_w_gate",
        "ffn2_w_up", "ffn2_w_down")


def _pack_small(d):
    parts = []
    for k in _SMALL:
        v = d[k].reshape(-1)
        rows = -(-v.shape[0] // _LANES)
        parts.append(jnp.pad(v, (0, rows * _LANES - v.shape[0])).reshape(rows, _LANES))
    a = jnp.concatenate(parts, axis=0)
    return jnp.pad(a, ((0, -a.shape[0] % 8), (0, 0)))


def _unpack_small(a, like):
    out, r0 = {}, 0
    for k in _SMALL:
        nvals = like[k].shape[1]
        rows = -(-nvals // _LANES)
        out[k] = a[r0:r0 + rows].reshape(-1)[:nvals].reshape(1, nvals)
        r0 += rows
    return out


def _stacked(w):
    return w.reshape(-1, w.shape[-1])


def _local_step(shards, sm, x, positions, target, p_idx, c_idx, pc_idx):
    ns = N_CHIPS
    hf, hq, hk, _ = _win_layout(x.shape[1])
    s_len = x.shape[0]
    full = {}

    def fetch(*jobs):
        names = list(dict.fromkeys(n for n, _, _ in jobs))
        return names, gather([bufs[n] for n in names], [(names.index(n), kind, part) for n, kind, part in jobs])

    def take(names, rid):
        for n, b in zip(names, rid[0]):
            bufs[n] = b

    n1 = ["ffn1_w_gate", "ffn1_w_up", "ffn1_w_down"]
    n2 = ["ffn2_w_gate", "ffn2_w_up", "ffn2_w_down"]
    later = ["w_in", "w_out"] + n2
    placed, _ = cast_place("cast_place_ffn1", [shards[n] for n in n1], p_idx)
    bufs = dict(zip(n1, placed))
    gate1, up1, down1 = n1
    gate2, up2, down2 = n2
    names, rider = fetch((gate1, "ici", WHOLE), (up1, "ici", WHOLE))
    placed, rid = cast_place("cast_place_later", [shards[n] for n in later], p_idx, rider=rider)
    bufs.update(zip(later, placed))
    take(names, rid)
    names, rider = fetch((gate1, "d2d", WHOLE), (up1, "d2d", WHOLE))
    xn1, r1, rid = rms_fwd("ffn1_norm", x, sm["norm_ffn1_g"], BF16, rider=rider)
    take(names, rid)
    names, rider = fetch((down1, "ici", WHOLE))
    (g1, u1, hid1), rid = ffn_gu("ffn1_gu", xn1, bufs[gate1], bufs[up1], rider=rider)
    take(names, rid)
    names, rider = fetch((down1, "d2d", WHOLE))
    take(names, run_step("gather_d2d_ffn1_down", rider))
    wd1 = _stacked(bufs[down1])
    names, rider = fetch(("w_in", "ici", WHOLE))
    h1, rid = mm_nn("ffn1_down", hid1, wd1, scale=0.5, resid=x, rider=rider)
    take(names, rid)

    names, rider = fetch(("w_in", "d2d", WHOLE))
    u, r_mix, rid = rms_fwd("mix_norm", h1, sm["norm_mix_g"], BF16, rider=rider)
    take(names, rid)
    names, rider = fetch(("w_out", "ici", WHOLE), (gate2, "ici", (0, 1, 4)))
    proj, rid = mm_nn_sharded("mix_inproj", u, bufs["w_in"], rider=rider)
    take(names, rid)
    sm_attn = {k: sm[k] for k in _ATTN_SMALL}
    (fox_in, swa_in), attn_vjp = jax.vjp(lambda pr, s: _attn_inputs(pr, s, positions), proj, sm_attn)
    names, rider = fetch((gate2, "ici", (1, 4, 4)), ("w_out", "d2d", WHOLE), (gate2, "d2d", (0, 1, 4)))
    (o_f, lse), rid = fox_fwd(*fox_in, rider=rider)
    take(names, rid)
    names, rider = fetch((up2, "ici", (0, 3, 4)), (gate2, "d2d", (1, 4, 4)))
    o_s, rid = swa_fwd(*swa_in, rider=rider)
    take(names, rid)
    o_fox, o_swa = o_f, o_s
    nf, r_fox = rms_fwd("out_norm_fox", o_fox, sm["out_norm_fox_g"], BF16)
    nsw, r_swa = rms_fwd("out_norm_swa", o_swa, sm["out_norm_swa_g"], BF16)
    o = jnp.concatenate([nf, nsw], axis=-1)
    wout = _stacked(bufs["w_out"])
    names, rider = fetch((up2, "ici", (3, 4, 4)), (up2, "d2d", (0, 3, 4)))
    h2, rid = mm_nn("out_proj", o, wout, resid=h1, rider=rider)
    take(names, rid)

    names, rider = fetch((up2, "d2d", (3, 4, 4)))
    xn2, r2, rid = rms_fwd("ffn2_norm", h2, sm["norm_ffn2_g"], BF16, rider=rider)
    take(names, rid)
    names, rider = fetch((down2, "ici", WHOLE))
    (g2, u2, hid2), rid = ffn_gu("ffn2_gu", xn2, bufs[gate2], bufs[up2], rider=rider)
    take(names, rid)
    names, rider = fetch((down2, "d2d", WHOLE))
    take(names, run_step("gather_d2d_ffn2_down", rider))
    wd2 = _stacked(bufs["ffn2_w_down"])
    y, _ = mm_nn("ffn2_down", hid2, wd2, scale=0.5, resid=h2)
    loss, dy = loss_call(y, target)

    red = {}

    def grad(n, g):
        red[n] = {"grad": g.reshape(ns, -1, g.shape[-1])}

    def ride(*steps):
        def done(rid):
            a0 = n0 = 0
            for rd, cb in steps:
                cb(rid[0][a0:a0 + len(rd.aliased)], rid[1][n0:n0 + len(rd.news)])
                a0, n0 = a0 + len(rd.aliased), n0 + len(rd.news)

        return (combine(*[s[0] for s in steps]) if len(steps) > 1 else steps[0][0]), done

    def xchg(*names):
        def cb(al, news):
            for n, t in zip(names, news):
                red[n]["sum"] = chip_sum("chip_sum_" + n, red[n]["grad"], t, c_idx)

        return exchange_halves([red[n]["grad"] for n in names]), cb

    def scat(n, part=WHOLE):
        def cb(al, news):
            red[n]["got"] = (al or news)[0]

        return scatter_to_owner([red[n]["sum"]], [red[n]["got"]] if "got" in red[n] else None, part), cb

    def own(n):
        red[n]["half"] = owner_sum("owner_sum_" + n, red[n]["sum"], red[n]["got"], pc_idx)

    def join(*names):
        return join_halves([red[n]["half"] for n in names]), lambda al, news: full.update(zip(names, al))

    dwd2, _ = mm_tn("ffn2_dwd", hid2, dy, out_dtype=BF16, scale=0.5)
    grad(down2, dwd2)
    rider, done = ride(xchg(down2))
    (dg2, du2), rid = ffn_dh("ffn2_dh", dy, wd2, g2, u2, ns, 0.5, rider=rider)
    done(rid)
    rider, done = ride(scat(down2, (0, 1, 2)))
    dwg2, rid = mm_tn_sharded("ffn2_dwg", xn2, dg2, ns, rider=rider)
    done(rid)
    grad(gate2, dwg2)
    rider, done = ride(scat(down2, (1, 2, 2)), xchg(gate2))
    dwu2, rid = mm_tn_sharded("ffn2_dwu", xn2, du2, ns, rider=rider)
    done(rid)
    grad(up2, dwu2)
    rider, done = ride(scat(gate2, (0, 1, 2)), xchg(up2))
    dxn, rid = mm_nt_sharded("ffn2_dxn_g", dg2, bufs[gate2], rider=rider)
    done(rid)
    rider, done = ride(scat(gate2, (1, 2, 2)))
    dxn, rid = mm_nt_sharded("ffn2_dxn_u", du2, bufs[up2], resid=dxn, rider=rider)
    done(rid)
    dh2, dgain_ffn2 = rms_bwd("ffn2_dnorm", h2, sm["norm_ffn2_g"], r2, dxn, dres=dy)
    own(down2)
    own(gate2)

    do, _ = mm_nt("out_do", dh2, wout)
    dwout, _ = mm_tn("out_dw", o, dh2, out_dtype=BF16)
    cf = o_fox.shape[1]
    d_fox, dgain_fox = rms_bwd("out_dnorm_fox", o_fox, sm["out_norm_fox_g"], r_fox, do[:, :cf])
    d_swa, dgain_swa = rms_bwd("out_dnorm_swa", o_swa, sm["out_norm_swa_g"], r_swa, do[:, cf:])
    grad("w_out", dwout)
    rider, done = ride(scat(up2))
    swa_cts, rid = swa_bwd(*swa_in, o_s, d_swa, rider=rider)
    done(rid)
    own(up2)
    rider, done = ride(xchg("w_out"), join(down2, gate2, up2))
    fox_cts, rid = fox_bwd(*fox_in, o_f, lse, d_fox, rider=rider)
    done(rid)
    dproj, dsm_attn = attn_vjp((tuple(fox_cts), tuple(swa_cts)))

    rider, done = ride(scat("w_out"))
    du, rid = mm_nt_sharded("mix_du", dproj, bufs["w_in"], rider=rider)
    done(rid)
    dwin, _ = mm_tn_sharded("mix_dwin", u, dproj, ns)
    grad("w_in", dwin)
    dh1, dgain_mix = rms_bwd("mix_dnorm", h1, sm["norm_mix_g"], r_mix, du, dres=dh2)
    own("w_out")

    rider, done = ride(xchg("w_in"))
    dwd1, rid = mm_tn("ffn1_dwd", hid1, dh1, out_dtype=BF16, scale=0.5, rider=rider)
    done(rid)
    grad(down1, dwd1)
    rider, done = ride(scat("w_in"), xchg(down1))
    (dg1, du1), rid = ffn_dh("ffn1_dh", dh1, wd1, g1, u1, ns, 0.5, rider=rider)
    done(rid)
    own("w_in")
    rider, done = ride(scat(down1, (0, 1, 2)), join("w_out", "w_in"))
    dwg1, rid = mm_tn_sharded("ffn1_dwg", xn1, dg1, ns, rider=rider)
    done(rid)
    grad(gate1, dwg1)
    rider, done = ride(scat(down1, (1, 2, 2)), xchg(gate1))
    dwu1, rid = mm_tn_sharded("ffn1_dwu", xn1, du1, ns, rider=rider)
    done(rid)
    grad(up1, dwu1)
    own(down1)
    rider, done = ride(scat(gate1, (0, 1, 2)), xchg(up1), join(down1))
    dxn, rid = mm_nt_sharded("ffn1_dxn_g", dg1, bufs[gate1], rider=rider)
    done(rid)
    rider, done = ride(scat(gate1, (1, 2, 2)), scat(up1, (0, 1, 4)))
    dxn, rid = mm_nt_sharded("ffn1_dxn_u", du1, bufs[up1], resid=dxn, rider=rider)
    done(rid)
    dx, dgain_ffn1 = rms_bwd("ffn1_dnorm", x, sm["norm_ffn1_g"], r1, dxn, dres=dh1)
    own(gate1)

    rider, done = ride(scat(up1, (1, 4, 4)), join(gate1))
    done(run_step("reduce_tail", rider))
    own(up1)
    rider, done = ride(join(up1))
    done(run_step("join_tail", rider))

    g_small = dict(dsm_attn)
    g_small["norm_mix_g"] = g_small["norm_mix_g"] + dgain_mix
    g_small.update(norm_ffn1_g=dgain_ffn1, norm_ffn2_g=dgain_ffn2, out_norm_fox_g=dgain_fox, out_norm_swa_g=dgain_swa)
    return loss, dx, full, g_small


def kernel(x, positions, norm_ffn1_g, ffn1_w_gate, ffn1_w_up, ffn1_w_down, norm_mix_g, w_in, b_forget, fox_q_norm_g, fox_k_norm_g, swa_q_norm_g, swa_k_norm_g, swa_sinks, out_norm_fox_g, out_norm_swa_g, w_out, norm_ffn2_g, ffn2_w_gate, ffn2_w_up, ffn2_w_down, loss_target, m_norm_ffn1_g, m_ffn1_w_gate, m_ffn1_w_up, m_ffn1_w_down, m_norm_mix_g, m_w_in, m_b_forget, m_fox_q_norm_g, m_fox_k_norm_g, m_swa_q_norm_g, m_swa_k_norm_g, m_swa_sinks, m_out_norm_fox_g, m_out_norm_swa_g, m_w_out, m_norm_ffn2_g, m_ffn2_w_gate, m_ffn2_w_up, m_ffn2_w_down, v_norm_ffn1_g, v_ffn1_w_gate, v_ffn1_w_up, v_ffn1_w_down, v_norm_mix_g, v_w_in, v_b_forget, v_fox_q_norm_g, v_fox_k_norm_g, v_swa_q_norm_g, v_swa_k_norm_g, v_swa_sinks, v_out_norm_fox_g, v_out_norm_swa_g, v_w_out, v_norm_ffn2_g, v_ffn2_w_gate, v_ffn2_w_up, v_ffn2_w_down):
    args = dict(locals())
    w = {k: args[k] for k in _ALL}
    m = {k: args["m_" + k] for k in _ALL}
    v = {k: args["v_" + k] for k in _ALL}
    c_idx = lax.axis_index("c").astype(jnp.int32).reshape(1)
    p_idx = (2 * lax.axis_index("x") + lax.axis_index("y")).astype(jnp.int32).reshape(1)
    pc_idx = jnp.concatenate([p_idx, c_idx])

    small = {k: w[k] for k in _SMALL}
    shards = {k: w[k][0] for k in _BIG}
    shards["w_in"] = _pad_lanes(shards["w_in"])
    loss, grad_x, g_shard, g_small = _local_step(shards, {k: w[k][0] for k in _SMALL}, x[0], positions[0], loss_target[0],
                                                 p_idx, c_idx, pc_idx)
    g_shard["w_in"] = g_shard["w_in"][:, :w["w_in"].shape[-1]]
    loss = lax.psum(loss, ("x", "y", "c"))
    g_small_sum = _unpack_small(all_reduce_small(_pack_small({k: g_small[k].reshape(1, -1) for k in _SMALL})), small)

    grad_w, delta, new_m, new_v = {}, {}, {}, {}
    for k in _BIG:
        (g, d, nm, nv), _ = adamw("adamw_" + k, w[k][0], g_shard[k], m[k][0], v[k][0])
        grad_w[k], delta[k], new_m[k], new_v[k] = g[None], d[None], nm[None], nv[None]
    (_, d, nm, nv), _ = adamw("adamw_small", _pack_small(small), _pack_small(g_small_sum), _pack_small({k: m[k] for k in _SMALL}),
                              _pack_small({k: v[k] for k in _SMALL}))
    grad_w.update(g_small_sum)
    delta.update(_unpack_small(d, small))
    new_m.update(_unpack_small(nm, small))
    new_v.update(_unpack_small(nv, small))

    return (loss, grad_x[None], *[grad_w[k] for k in _ALL], *[delta[k] for k in _ALL], *[new_m[k] for k in _ALL], *[new_v[k] for k in _ALL])
```

```python
import functools

import jax
import jax.numpy as jnp
from jax import lax
from jax.experimental import pallas as pl
from jax.experimental.pallas import tpu as pltpu

F32 = jnp.float32
BF16 = jnp.bfloat16

HEAD_DIM = 64
WINDOW = 128
ROPE_THETA = 10000.0
EPS = 1e-6
N_CHIPS = 4
N_DEV = 8

ADAM_LR = 0.001
ADAM_B1 = 0.9
ADAM_B2 = 0.999
ADAM_EPS = 1e-08
ADAM_WD = 0.01
ADAM_STEP = 10

V7X_VMEM_BYTES = 64 * 1024 * 1024
VMEM_LIMIT = V7X_VMEM_BYTES - 8 * 1024 * 1024
MASK_VALUE = -1e30

_MESH = pl.DeviceIdType.MESH
_HBM = pl.BlockSpec(memory_space=pl.ANY)
_DIMS = {"nn": (((1,), (0,)), ((), ())), "nt": (((1,), (1,)), ((), ())), "tn": (((0,), (0,)), ((), ()))}


def _pick(n, prefs):
    for p in prefs:
        if n % p == 0:
            return p
    return n


class Rider:
    def __init__(self, reads, aliased, news, nsem, build):
        self.reads, self.aliased, self.news, self.nsem, self.build = list(reads), list(aliased), list(news), nsem, build


class _Shifted:
    def __init__(self, ref, off):
        self.ref, self.off = ref, off

    @property
    def at(self):
        return self

    def __getitem__(self, k):
        return self.ref.at[k + self.off]


def combine(*riders):
    def build(reads, al, news, ssem, rsem):
        out = ([], [], [])
        r0 = a0 = n0 = s0 = 0
        for rd in riders:
            nr, na, nn = len(rd.reads), len(rd.aliased), len(rd.news)
            part = rd.build(reads[r0:r0 + nr], al[a0:a0 + na], news[n0:n0 + nn], _Shifted(ssem, s0), _Shifted(rsem, s0))
            for acc, lst in zip(out, part):
                acc.extend(lst)
            r0, a0, n0, s0 = r0 + nr, a0 + na, n0 + nn, s0 + rd.nsem
        return out

    return Rider(sum((r.reads for r in riders), []), sum((r.aliased for r in riders), []), sum((r.news for r in riders), []),
                 sum(r.nsem for r in riders), build)


def _me():
    return lax.axis_index("x"), lax.axis_index("y"), lax.axis_index("c")


def _other_chips(x, y):
    return [(1 - x, y), (x, 1 - y), (1 - x, 1 - y)]


WHOLE = (0, 1, 1)


def _rows(ref, start, rows, part=WHOLE):
    k0, k1, n = part
    assert rows % n == 0, (rows, part)
    idx = (slice(None),) * (len(ref.shape) - 2) + (pl.ds(start + k0 * (rows // n), (k1 - k0) * (rows // n)), slice(None))
    return ref.at[idx]


def _half(ref, h, part=WHOLE):
    rows = ref.shape[-2] // 2
    return _rows(ref, h * rows, rows, part)


def _remote(src, dst, ssem, rsem, k, to):
    return pltpu.make_async_remote_copy(src_ref=src, dst_ref=dst, send_sem=ssem.at[k], recv_sem=rsem.at[k], device_id=to,
                                        device_id_type=_MESH)


def _later(*args):
    return functools.partial(_remote, *args)


def gather(bufs, jobs):
    def build(reads, al, news, ssem, rsem):
        x, y, c = _me()
        p = 2 * x + y
        starts, arrivals = [], []
        for n, (b, kind, part) in enumerate(jobs):
            for j, chip in enumerate(_other_chips(x, y)):
                q = 2 * chip[0] + chip[1]
                if kind == "ici":
                    src, landing, to = _half(al[b].at[p], c, part), _half(al[b].at[q], c, part), (*chip, c)
                else:
                    src, landing, to = _half(al[b].at[q], c, part), _half(al[b].at[q], 1 - c, part), (x, y, 1 - c)
                starts.append(_later(src, src, ssem, rsem, 3 * n + j, to))
                arrivals.append(_later(landing, landing, ssem, rsem, 3 * n + j, to))
        return starts, arrivals, starts

    return Rider([], bufs, [], 3 * len(jobs), build)


def exchange_halves(grads):
    def build(reads, al, news, ssem, rsem):
        x, y, c = _me()
        cps = [_later(_half(g, 1 - c), t, ssem, rsem, w, (x, y, 1 - c)) for w, (g, t) in enumerate(zip(reads, news))]
        return cps, cps, cps

    return Rider(grads, [], [jax.ShapeDtypeStruct((g.shape[0], g.shape[1] // 2, g.shape[2]), g.dtype) for g in grads], len(grads), build)


def scatter_to_owner(sums, gots=None, part=WHOLE):
    def build(reads, al, news, ssem, rsem):
        x, y, c = _me()
        cps = []
        for w, (s, got) in enumerate(zip(reads, al or news)):
            rows = s.shape[-2]
            for j, chip in enumerate(_other_chips(x, y)):
                cps.append(_later(_rows(s.at[2 * chip[0] + chip[1]], 0, rows, part), _rows(got.at[j], 0, rows, part), ssem, rsem,
                                  3 * w + j, (*chip, c)))
        return cps, cps, cps

    news = [] if gots else [jax.ShapeDtypeStruct((3,) + s.shape[1:], s.dtype) for s in sums]
    return Rider(sums, gots or [], news, 3 * len(sums), build)


def join_halves(fulls):
    def build(reads, al, news, ssem, rsem):
        x, y, c = _me()
        starts, arrivals = [], []
        for w, f in enumerate(al):
            mine, landing = _half(f, c), _half(f, 1 - c)
            starts.append(_later(mine, mine, ssem, rsem, w, (x, y, 1 - c)))
            arrivals.append(_later(landing, landing, ssem, rsem, w, (x, y, 1 - c)))
        return starts, arrivals, starts

    return Rider([], fulls, [], len(fulls), build)


def _start_and_wait(rider, reads, al, news, ssem, rsem, first, last):
    @pl.when(first)
    def _():
        for cp in rider.build(reads, al, news, ssem, rsem)[0]:
            cp().start()

    def finish():
        @pl.when(last)
        def _():
            _, arrivals, sends = rider.build(reads, al, news, ssem, rsem)
            for cp in arrivals:
                cp().wait_recv()
            for cp in sends:
                cp().wait_send()

    return finish


def _call(name, body, grid, in_specs, args, out_specs, out_shape, scratch=(), semantics=None, rider=None, prefetch=()):
    n_pre, n_in, n_out, n_scr = len(prefetch), len(args), len(out_shape), len(scratch)
    nr, na, nn = (len(rider.reads), len(rider.aliased), len(rider.news)) if rider else (0, 0, 0)

    def wrapped(*refs):
        pre, refs = refs[:n_pre], refs[n_pre:]
        ins, reads = refs[:n_in], refs[n_in:n_in + nr]
        o0 = n_in + nr + na
        outs, al, news = refs[o0:o0 + n_out], refs[o0 + n_out:o0 + n_out + na], refs[o0 + n_out + na:o0 + n_out + na + nn]
        s0 = o0 + n_out + na + nn
        scr, (ssem, rsem) = refs[s0:s0 + n_scr], refs[s0 + n_scr:]
        first = functools.reduce(jnp.logical_and, [pl.program_id(a) == 0 for a in range(len(grid))])
        last = functools.reduce(jnp.logical_and, [pl.program_id(a) == g - 1 for a, g in enumerate(grid)])
        finish = _start_and_wait(rider, reads, al, news, ssem, rsem, first, last)
        body(*pre, *ins, *outs, *scr)
        finish()

    kernel_fn, all_in, all_out, shapes, scr = body, list(in_specs), list(out_specs), list(out_shape), list(scratch)
    operands, aliases = (*prefetch, *args), {}
    if rider:
        kernel_fn, semantics = wrapped, ("arbitrary",) * len(grid)
        all_in += [_HBM] * (nr + na)
        all_out += [_HBM] * (na + nn)
        shapes += [jax.ShapeDtypeStruct(a.shape, a.dtype) for a in rider.aliased] + rider.news
        scr += [pltpu.SemaphoreType.DMA((rider.nsem,)), pltpu.SemaphoreType.DMA((rider.nsem,))]
        operands += (*rider.reads, *rider.aliased)
        aliases = {n_pre + n_in + nr + i: n_out + i for i in range(na)}
    params = pltpu.CompilerParams(dimension_semantics=semantics, vmem_limit_bytes=VMEM_LIMIT)
    if n_pre:
        spec = pltpu.PrefetchScalarGridSpec(num_scalar_prefetch=n_pre, grid=grid, in_specs=all_in, out_specs=all_out, scratch_shapes=scr)
        outs = pl.pallas_call(kernel_fn, name=name, grid_spec=spec, out_shape=shapes, input_output_aliases=aliases, compiler_params=params)(*operands)
    else:
        outs = pl.pallas_call(kernel_fn, name=name, grid=grid, in_specs=all_in, out_specs=all_out, out_shape=shapes, scratch_shapes=scr,
                              input_output_aliases=aliases, compiler_params=params)(*operands)
    return list(outs[:n_out]), ((list(outs[n_out:n_out + na]), list(outs[n_out + na:])) if rider else None)


def run_step(name, rider):
    nr, na, nn = len(rider.reads), len(rider.aliased), len(rider.news)

    def body(*refs):
        reads = refs[:nr]
        al, news = refs[nr + na:nr + 2 * na], refs[nr + 2 * na:nr + 2 * na + nn]
        ssem, rsem = refs[nr + 2 * na + nn:]
        starts, arrivals, sends = rider.build(reads, al, news, ssem, rsem)
        for cp in starts:
            cp().start()
        for cp in arrivals:
            cp().wait_recv()
        for cp in sends:
            cp().wait_send()

    outs = pl.pallas_call(
        body, name=name, in_specs=[_HBM] * (nr + na), out_specs=[_HBM] * (na + nn),
        out_shape=[jax.ShapeDtypeStruct(a.shape, a.dtype) for a in rider.aliased] + rider.news,
        input_output_aliases={nr + i: i for i in range(na)},
        scratch_shapes=[pltpu.SemaphoreType.DMA((rider.nsem,)), pltpu.SemaphoreType.DMA((rider.nsem,))],
    )(*rider.reads, *rider.aliased)
    return list(outs[:na]), list(outs[na:])


def all_reduce_small(v):
    rows, lanes = v.shape

    def body(v_ref, o_ref, slots, send_sems, recv_sems):
        x, y, c = _me()
        me = 4 * x + 2 * y + c
        slots[me] = v_ref[...]
        cps = []
        for k in range(1, N_DEV):
            peer = (x ^ (k >> 2), y ^ ((k >> 1) & 1), c ^ (k & 1))
            cps.append(_remote(v_ref, slots.at[me], send_sems, recv_sems, k - 1, peer))
            cps[-1].start()
        for k in range(1, N_DEV):
            theirs = slots.at[me ^ k]
            _remote(theirs, theirs, send_sems, recv_sems, k - 1, (x, y, c)).wait_recv()
        for cp in cps:
            cp.wait_send()
        acc = slots[0]
        for i in range(1, N_DEV):
            acc = acc + slots[i]
        o_ref[...] = acc

    return pl.pallas_call(
        body, name="all_reduce_small",
        in_specs=[pl.BlockSpec(memory_space=pltpu.VMEM)], out_specs=pl.BlockSpec(memory_space=pltpu.VMEM),
        out_shape=jax.ShapeDtypeStruct((rows, lanes), F32),
        scratch_shapes=[pltpu.VMEM((N_DEV, rows, lanes), F32), pltpu.SemaphoreType.DMA((N_DEV - 1,)), pltpu.SemaphoreType.DMA((N_DEV - 1,))],
    )(v)


def _mm_call(name, mode, a, b, a_spec, b_spec, out_shape, out_spec, grid, acc_shape, scale=1.0, resid=None, resid_spec=None, rider=None):
    nk = grid[2]
    dims = _DIMS[mode]
    has_resid = resid is not None

    def body(*refs):
        a_ref, b_ref = refs[:2]
        r_ref = refs[2] if has_resid else None
        o_ref = refs[3] if has_resid else refs[2]

        def finish(r):
            if scale != 1.0:
                r = r * scale
            if has_resid:
                r = r_ref[...].astype(F32) + r
            o_ref[...] = r.astype(o_ref.dtype)

        part = lax.dot_general(a_ref[...].astype(BF16), b_ref[...].astype(BF16), dims, preferred_element_type=F32)
        if nk == 1:
            finish(part)
            return
        acc_ref = refs[-1]
        k = pl.program_id(2)

        @pl.when(k == 0)
        def _():
            acc_ref[...] = part

        @pl.when(k > 0)
        def _():
            acc_ref[...] += part

        @pl.when(k == nk - 1)
        def _():
            finish(acc_ref[...])

    in_specs = [a_spec, b_spec] + ([resid_spec] if has_resid else [])
    args = (a, b) + ((resid,) if has_resid else ())
    (out,), rid = _call(name, body, grid, in_specs, args, [out_spec], [out_shape], [pltpu.VMEM(acc_shape, F32)] if nk > 1 else [],
                        ("parallel", "parallel", "arbitrary"), rider)
    return out, rid


MM_VMEM_BUDGET = 40 * 1024 * 1024
_TILE_OPTS = (2048, 1408, 1024, 512, 256, 128)


def _tiles(m, n, kd, a_item, b_item, o_item, r_item=0, tm=None, tn=None, tk=None):
    def opts(full, fixed, cap):
        return [fixed] if fixed else [t for t in _TILE_OPTS if t <= cap and full % t == 0] or [full]

    best = None
    for cm in opts(m, tm, 1408):
        for cn in opts(n, tn, 1408):
            for ck in opts(kd, tk, 2048):
                blocks = cm * ck * a_item + ck * cn * b_item + cm * cn * (o_item + r_item)
                casts = (cm * ck * 2 if a_item == 4 else 0) + (ck * cn * 2 if b_item == 4 else 0)
                if 2 * blocks + cm * cn * 4 + casts <= MM_VMEM_BUDGET:
                    key = (cm * cn * ck, ck)
                    if best is None or key > best[0]:
                        best = (key, (cm, cn, ck))
    assert best is not None, (m, n, kd)
    return best[1]


def _item(x):
    return jnp.dtype(x.dtype).itemsize


def mm_nn(name, a, b, *, out_dtype=F32, scale=1.0, resid=None, rider=None):
    m, kd = a.shape
    n = b.shape[1]
    tm, tn, tk = _tiles(m, n, kd, _item(a), _item(b), jnp.dtype(out_dtype).itemsize, 0 if resid is None else _item(resid))
    o_spec = pl.BlockSpec((tm, tn), lambda i, j, k: (i, j))
    return _mm_call(
        name, "nn", a, b, pl.BlockSpec((tm, tk), lambda i, j, k: (i, k)), pl.BlockSpec((tk, tn), lambda i, j, k: (k, j)),
        jax.ShapeDtypeStruct((m, n), out_dtype), o_spec, (m // tm, n // tn, kd // tk), (tm, tn), scale, resid, o_spec, rider)


def mm_nt(name, a, b, *, out_dtype=F32, scale=1.0, resid=None, rider=None):
    m, kd = a.shape
    n = b.shape[0]
    tm, tn, tk = _tiles(m, n, kd, _item(a), _item(b), jnp.dtype(out_dtype).itemsize, 0 if resid is None else _item(resid))
    o_spec = pl.BlockSpec((tm, tn), lambda i, j, k: (i, j))
    return _mm_call(
        name, "nt", a, b, pl.BlockSpec((tm, tk), lambda i, j, k: (i, k)), pl.BlockSpec((tn, tk), lambda i, j, k: (j, k)),
        jax.ShapeDtypeStruct((m, n), out_dtype), o_spec, (m // tm, n // tn, kd // tk), (tm, tn), scale, resid, o_spec, rider)


def mm_tn(name, a, b, *, out_dtype=F32, scale=1.0, rider=None):
    kd, m = a.shape
    n = b.shape[1]
    tm, tn, tk = _tiles(m, n, kd, _item(a), _item(b), jnp.dtype(out_dtype).itemsize)
    return _mm_call(
        name, "tn", a, b, pl.BlockSpec((tk, tm), lambda i, j, k: (k, i)), pl.BlockSpec((tk, tn), lambda i, j, k: (k, j)),
        jax.ShapeDtypeStruct((m, n), out_dtype), pl.BlockSpec((tm, tn), lambda i, j, k: (i, j)),
        (m // tm, n // tn, kd // tk), (tm, tn), scale, rider=rider)


def mm_nn_sharded(name, a, w, *, rider=None):
    m, kd = a.shape
    ns, _, c = w.shape
    tm, _, tk = _tiles(m, c, kd, _item(a), _item(w), 4, tn=c)
    return _mm_call(
        name, "nn", a, w, pl.BlockSpec((tm, tk), lambda i, j, k: (i, k)), pl.BlockSpec((None, tk, c), lambda i, j, k: (j, k, 0)),
        jax.ShapeDtypeStruct((ns, m, c), F32), pl.BlockSpec((None, tm, c), lambda i, j, k: (j, i, 0)),
        (m // tm, ns, kd // tk), (tm, c), rider=rider)


def mm_nt_sharded(name, a, w, *, resid=None, rider=None):
    ns, n, c = w.shape
    m = a.shape[-2]
    tm, tn, _ = _tiles(m, n, c, _item(a), _item(w), 4, 0 if resid is None else _item(resid), tk=c)
    o_spec = pl.BlockSpec((tm, tn), lambda i, j, k: (i, j))
    a_spec = pl.BlockSpec((tm, c), lambda i, j, k: (i, k)) if a.ndim == 2 else pl.BlockSpec((None, tm, c), lambda i, j, k: (k, i, 0))
    return _mm_call(
        name, "nt", a, w, a_spec, pl.BlockSpec((None, tn, c), lambda i, j, k: (k, j, 0)),
        jax.ShapeDtypeStruct((m, n), F32), o_spec, (m // tm, n // tn, ns), (tm, tn), 1.0, resid, o_spec, rider)


def mm_tn_sharded(name, a, b, ns, *, rider=None):
    kd, m = a.shape
    c = b.shape[-1] // (ns if b.ndim == 2 else 1)
    tm, _, tk = _tiles(m, c, kd, _item(a), _item(b), 2, tn=c)
    b_spec = pl.BlockSpec((tk, c), lambda i, j, k: (k, j)) if b.ndim == 2 else pl.BlockSpec((None, tk, c), lambda i, j, k: (j, k, 0))
    return _mm_call(
        name, "tn", a, b, pl.BlockSpec((tk, tm), lambda i, j, k: (k, i)), b_spec,
        jax.ShapeDtypeStruct((ns, m, c), BF16), pl.BlockSpec((None, tm, c), lambda i, j, k: (j, i, 0)),
        (m // tm, ns, kd // tk), (tm, c), rider=rider)


def rms_fwd(name, x, g, out_dtype, rider=None):
    r, c = x.shape
    tm = _pick(r, (512, 256, 128, 64, 8))

    def body(x_ref, g_ref, y_ref, r_ref):
        xf = x_ref[...].astype(F32)
        rstd = lax.rsqrt(jnp.mean(xf * xf, axis=-1, keepdims=True) + EPS)
        y_ref[...] = ((xf * rstd) * g_ref[...]).astype(y_ref.dtype)
        r_ref[...] = rstd

    (y, rstd), rid = _call(
        name, body, (r // tm,), [pl.BlockSpec((tm, c), lambda i: (i, 0)), pl.BlockSpec((1, c), lambda i: (0, 0))], (x, g.reshape(1, c)),
        [pl.BlockSpec((tm, c), lambda i: (i, 0)), pl.BlockSpec((tm, 1), lambda i: (i, 0))],
        [jax.ShapeDtypeStruct((r, c), out_dtype), jax.ShapeDtypeStruct((r, 1), F32)], (), ("parallel",), rider)
    return (y, rstd) if rider is None else (y, rstd, rid)


def rms_bwd(name, x, g, rstd, dy, dres=None):
    r, c = x.shape
    tm = _pick(r, (512, 256, 128, 64, 8))
    has_res = dres is not None

    def body(*refs):
        if has_res:
            x_ref, g_ref, r_ref, dy_ref, dres_ref, dx_ref, dg_ref = refs
        else:
            x_ref, g_ref, r_ref, dy_ref, dx_ref, dg_ref = refs
        xhat = x_ref[...].astype(F32) * r_ref[...]
        dyf = dy_ref[...].astype(F32)
        gdy = dyf * g_ref[...]
        dx = r_ref[...] * (gdy - xhat * jnp.mean(gdy * xhat, axis=-1, keepdims=True))
        if has_res:
            dx = dx + dres_ref[...]
        dx_ref[...] = dx

        @pl.when(pl.program_id(0) == 0)
        def _():
            dg_ref[...] = jnp.zeros_like(dg_ref)

        dg_ref[...] += jnp.sum(dyf * xhat, axis=0, keepdims=True)

    row = pl.BlockSpec((tm, c), lambda i: (i, 0))
    in_specs = [row, pl.BlockSpec((1, c), lambda i: (0, 0)), pl.BlockSpec((tm, 1), lambda i: (i, 0)), row] + ([row] if has_res else [])
    args = (x, g.reshape(1, c), rstd, dy) + ((dres,) if has_res else ())
    (dx, dg), _ = _call(name, body, (r // tm,), in_specs, args, [row, pl.BlockSpec((1, c), lambda i: (0, 0))],
                        [jax.ShapeDtypeStruct((r, c), F32), jax.ShapeDtypeStruct((1, c), F32)], (), ("arbitrary",))
    return dx, dg.reshape(c)


_LANES = 128


def _head_mean(v):
    if v.shape[1] == HEAD_DIM:
        return jnp.mean(v, axis=-1, keepdims=True)
    low = lax.broadcasted_iota(jnp.int32, v.shape, 1) < HEAD_DIM
    lo = jnp.sum(jnp.where(low, v, 0.0), axis=-1, keepdims=True)
    hi = jnp.sum(jnp.where(low, 0.0, v), axis=-1, keepdims=True)
    return jnp.where(low, lo, hi) * (1.0 / HEAD_DIM)


def _head_groups(c):
    width = _LANES if c % _LANES == 0 else HEAD_DIM
    assert c % width == 0, c
    return width, [slice(k * width, (k + 1) * width) for k in range(c // width)]


def _head_gain(g, width):
    return jnp.tile(g.reshape(1, HEAD_DIM), (1, width // HEAD_DIM))


def head_rms_fwd(name, x, g):
    s, c = x.shape
    tm = _pick(s, (256, 128, 8))
    width, groups = _head_groups(c)

    def body(x_ref, g_ref, y_ref):
        for sl in groups:
            xs = x_ref[:, sl]
            y_ref[:, sl] = (xs * lax.rsqrt(_head_mean(xs * xs) + EPS)) * g_ref[...]

    row = pl.BlockSpec((tm, c), lambda i: (i, 0))
    (y,), _ = _call(name, body, (s // tm,), [row, pl.BlockSpec((1, width), lambda i: (0, 0))], (x, _head_gain(g, width)), [row],
                    [jax.ShapeDtypeStruct((s, c), F32)], (), ("parallel",))
    return y


def head_rms_bwd(name, x, g, dy):
    s, c = x.shape
    tm = _pick(s, (256, 128, 8))
    width, groups = _head_groups(c)

    def body(x_ref, g_ref, dy_ref, dx_ref, dg_ref):
        @pl.when(pl.program_id(0) == 0)
        def _():
            dg_ref[...] = jnp.zeros_like(dg_ref)

        for sl in groups:
            xs, dys = x_ref[:, sl], dy_ref[:, sl]
            rstd = lax.rsqrt(_head_mean(xs * xs) + EPS)
            xhat = xs * rstd
            gdy = dys * g_ref[...]
            dx_ref[:, sl] = rstd * (gdy - xhat * _head_mean(gdy * xhat))
            dg_ref[...] += jnp.sum(dys * xhat, axis=0, keepdims=True)

    row = pl.BlockSpec((tm, c), lambda i: (i, 0))
    vec = pl.BlockSpec((1, width), lambda i: (0, 0))
    (dx, dg), _ = _call(name, body, (s // tm,), [row, vec, row], (x, _head_gain(g, width), dy), [row, vec],
                        [jax.ShapeDtypeStruct((s, c), F32), jax.ShapeDtypeStruct((1, width), F32)], (), ("arbitrary",))
    return dx, jnp.sum(dg.reshape(width // HEAD_DIM, HEAD_DIM), axis=0)


@functools.partial(jax.custom_vjp, nondiff_argnums=(0,))
def head_rms(name, x, g):
    return head_rms_fwd(name + "_fwd", x, g)


def _head_rms_fwd(name, x, g):
    return head_rms_fwd(name + "_fwd", x, g), (x, g)


def _head_rms_bwd(name, res, dy):
    return head_rms_bwd(name + "_bwd", *res, dy)


head_rms.defvjp(_head_rms_fwd, _head_rms_bwd)


FFN_TM = 512


def _sigmoid(x):
    return 1.0 / (1.0 + jnp.exp(-x))


def ffn_gu(name, xn, wg, wu, rider=None):
    s, d = xn.shape
    ns, _, c = wg.shape
    tm = _pick(s, (FFN_TM, 128))

    def body(x_ref, wg_ref, wu_ref, g_ref, u_ref, h_ref):
        xb = x_ref[...]
        gv = jnp.dot(xb, wg_ref[...], preferred_element_type=F32)
        uv = jnp.dot(xb, wu_ref[...], preferred_element_type=F32)
        g_ref[...] = gv
        u_ref[...] = uv
        h_ref[...] = ((gv * _sigmoid(gv)) * uv).astype(BF16)

    w_spec = pl.BlockSpec((None, d, c), lambda j, i: (j, 0, 0))
    o_spec = pl.BlockSpec((tm, c), lambda j, i: (i, j))
    return _call(
        name, body, (ns, s // tm), [pl.BlockSpec((tm, d), lambda j, i: (i, 0)), w_spec, w_spec], (xn, wg, wu),
        [o_spec, o_spec, o_spec],
        [jax.ShapeDtypeStruct((s, ns * c), F32), jax.ShapeDtypeStruct((s, ns * c), F32), jax.ShapeDtypeStruct((s, ns * c), BF16)],
        [], ("parallel", "parallel"), rider)


def ffn_dh(name, dy, wd, g, u, ns, scale, rider=None):
    s, d = dy.shape
    f = wd.shape[0]
    c = f // ns
    tm = _pick(s, (FFN_TM, 128))

    def body(dy_ref, wd_ref, g_ref, u_ref, dg_ref, du_ref):
        dh = lax.dot_general(dy_ref[...].astype(BF16), wd_ref[...], _DIMS["nt"], preferred_element_type=F32) * scale
        gv, uv = g_ref[...], u_ref[...]
        sig = _sigmoid(gv)
        dg_ref[...] = (dh * uv * (sig * (1.0 + gv * (1.0 - sig)))).astype(BF16)
        du_ref[...] = (dh * (gv * sig)).astype(BF16)

    o_spec = pl.BlockSpec((tm, c), lambda j, i: (i, j))
    return _call(
        name, body, (ns, s // tm),
        [pl.BlockSpec((tm, d), lambda j, i: (i, 0)), pl.BlockSpec((c, d), lambda j, i: (j, 0)), o_spec, o_spec], (dy, wd, g, u),
        [o_spec, o_spec], [jax.ShapeDtypeStruct((s, f), BF16), jax.ShapeDtypeStruct((s, f), BF16)],
        [], ("parallel", "parallel"), rider)


FOX_TQ = 512


def fox_tile(s_len):
    return min(FOX_TQ, s_len)


def _heads_per_block(h):
    return 2 if h % 2 == 0 else 1


def _fox_queries(q):
    return (q * (HEAD_DIM ** -0.5)).astype(BF16)


def _fox_scores(qs, kc, cq, ck, diagonal):
    s = lax.dot_general(qs, kc.astype(BF16), _DIMS["nt"], preferred_element_type=F32) + cq - ck
    if not diagonal:
        return s
    return jnp.where(lax.broadcasted_iota(jnp.int32, s.shape, 0) >= lax.broadcasted_iota(jnp.int32, s.shape, 1), s, MASK_VALUE)


def _fox_specs(h, s_len, tq):
    hb = _heads_per_block(h)
    qb = pl.BlockSpec((tq, hb * HEAD_DIM), lambda pp, i: (i, pp))
    kb = pl.BlockSpec((s_len, hb * HEAD_DIM), lambda pp, i: (0, pp))
    colb = pl.BlockSpec((hb, tq, 1), lambda pp, i: (pp, i, 0))
    rowb = pl.BlockSpec((hb, s_len // tq, 1, tq), lambda pp, i: (pp, 0, 0, 0))
    return hb, qb, kb, colb, rowb


def fox_fwd(q, k, v, cq, ck, rider=None):
    s_len, hd = q.shape
    h, d = hd // HEAD_DIM, HEAD_DIM
    tq = fox_tile(s_len)
    hb, qb, kb, colb, rowb = _fox_specs(h, s_len, tq)

    def body(q_ref, k_ref, v_ref, cq_ref, ck_ref, o_ref, lse_ref):
        i = pl.program_id(1)
        for hh in range(hb):
            lanes = slice(hh * d, (hh + 1) * d)
            qs, cqv = _fox_queries(q_ref[:, lanes]), cq_ref[hh]

            def chunk(c, carry, diagonal=False):
                m, l, acc = carry
                rows = pl.ds(pl.multiple_of(c * tq, tq), tq)
                s = _fox_scores(qs, k_ref[rows, lanes], cqv, ck_ref[hh, c], diagonal)
                m_new = jnp.maximum(m, jnp.max(s, axis=-1, keepdims=True))
                alpha = jnp.exp(m - m_new)
                p = jnp.exp(s - m_new)
                acc = alpha * acc + jnp.dot(p.astype(BF16), v_ref[rows, lanes].astype(BF16), preferred_element_type=F32)
                return m_new, alpha * l + jnp.sum(p, axis=-1, keepdims=True), acc

            init = (jnp.full((tq, 1), MASK_VALUE, F32), jnp.zeros((tq, 1), F32), jnp.zeros((tq, d), F32))
            m, l, acc = chunk(i, lax.fori_loop(0, i, chunk, init), diagonal=True)
            o_ref[:, lanes] = acc / l
            lse_ref[hh] = m + jnp.log(l)

    return _call(
        "fox_fwd", body, (h // hb, s_len // tq), [qb, kb, kb, colb, rowb], (q, k, v, cq, ck), [qb, colb],
        [jax.ShapeDtypeStruct((s_len, hd), F32), jax.ShapeDtypeStruct((h, s_len, 1), F32)], (), ("parallel", "parallel"), rider)


def fox_bwd(q, k, v, cq, ck, o, lse, do, rider=None):
    s_len, hd = q.shape
    h, d = hd // HEAD_DIM, HEAD_DIM
    tq = fox_tile(s_len)
    scale = HEAD_DIM ** -0.5
    hb, qb, kb, colb, rowb = _fox_specs(h, s_len, tq)

    def body(q_ref, k_ref, v_ref, cq_ref, ck_ref, o_ref, lse_ref, do_ref, dq_ref, dk_ref, dv_ref, dcq_ref, dck_ref):
        i = pl.program_id(1)

        @pl.when(i == 0)
        def _():
            dk_ref[...] = jnp.zeros_like(dk_ref)
            dv_ref[...] = jnp.zeros_like(dv_ref)
            dck_ref[...] = jnp.zeros_like(dck_ref)

        heads = []
        for hh in range(hb):
            lanes = slice(hh * d, (hh + 1) * d)
            dof = do_ref[:, lanes]
            heads.append((lanes, _fox_queries(q_ref[:, lanes]), cq_ref[hh], lse_ref[hh], dof.astype(BF16),
                          jnp.sum(dof * o_ref[:, lanes], axis=-1, keepdims=True)))

        def chunk(c, carry, diagonal=False):
            rows = pl.ds(pl.multiple_of(c * tq, tq), tq)
            out, dks, dvs = [], [], []
            for hh, (lanes, qs, cqv, lse_h, dob, delta) in enumerate(heads):
                dq, dcq = carry[hh]
                kc = k_ref[rows, lanes]
                p = jnp.exp(_fox_scores(qs, kc, cqv, ck_ref[hh, c], diagonal) - lse_h)
                dp = lax.dot_general(dob, v_ref[rows, lanes].astype(BF16), _DIMS["nt"], preferred_element_type=F32)
                ds = p * (dp - delta)
                dsb = ds.astype(BF16)
                dvs.append(lax.dot_general(p.astype(BF16), dob, _DIMS["tn"], preferred_element_type=F32))
                dks.append(lax.dot_general(dsb, qs, _DIMS["tn"], preferred_element_type=F32))
                dck_ref[hh, c] -= jnp.sum(ds, axis=0, keepdims=True)
                out.append((dq + jnp.dot(dsb, kc.astype(BF16), preferred_element_type=F32), dcq + jnp.sum(ds, axis=-1, keepdims=True)))
            dk_ref[rows, :] += jnp.concatenate(dks, axis=1)
            dv_ref[rows, :] += jnp.concatenate(dvs, axis=1)
            return tuple(out)

        init = tuple((jnp.zeros((tq, d), F32), jnp.zeros((tq, 1), F32)) for _ in range(hb))
        done = chunk(i, lax.fori_loop(0, i, chunk, init), diagonal=True)
        dq_ref[...] = jnp.concatenate([dq for dq, _ in done], axis=1) * scale
        for hh, (_, dcq) in enumerate(done):
            dcq_ref[hh] = dcq

    return _call(
        "fox_bwd", body, (h // hb, s_len // tq), [qb, kb, kb, colb, rowb, qb, colb, qb], (q, k, v, cq, ck, o, lse, do),
        [qb, kb, kb, colb, rowb],
        [jax.ShapeDtypeStruct((s_len, hd), F32)] * 3
        + [jax.ShapeDtypeStruct((h, s_len, 1), F32), jax.ShapeDtypeStruct((h, s_len // tq, 1, tq), F32)],
        (), ("parallel", "arbitrary"), rider)


def _stack_heads(ref, first, g):
    return jnp.concatenate([ref[:, (first + j) * HEAD_DIM:(first + j + 1) * HEAD_DIM] for j in range(g)], axis=0)


def _window(prev_ref, cur_ref, hh):
    lanes = slice(hh * HEAD_DIM, (hh + 1) * HEAD_DIM)
    return jnp.concatenate([prev_ref[:, lanes], cur_ref[:, lanes]], axis=0).astype(BF16)


def _swa_probs(q, kw, sink, n, w):
    rows = q.shape[0]
    s = lax.dot_general(q, kw, _DIMS["nt"], preferred_element_type=F32) * (HEAD_DIM ** -0.5)
    t = lax.broadcasted_iota(jnp.int32, (rows, 2 * w), 0) & (w - 1)
    col = lax.broadcasted_iota(jnp.int32, (rows, 2 * w), 1)
    rel = t + w - col
    valid = (rel >= 0) & (rel < w) & ((col >= w) | (n > 0))
    s = jnp.where(valid, s, MASK_VALUE)
    m = jnp.maximum(jnp.max(s, axis=-1, keepdims=True), sink)
    p = jnp.exp(s - m)
    ps = jnp.exp(sink - m)
    linv = 1.0 / (jnp.sum(p, axis=-1, keepdims=True) + ps)
    return p * linv, ps * linv


def _swa_specs(hk, g, s_len):
    w = WINDOW
    assert w & (w - 1) == 0 and s_len % w == 0
    hb = _heads_per_block(hk)
    qb = pl.BlockSpec((w, hb * g * HEAD_DIM), lambda pp, n: (n, pp))
    prev = pl.BlockSpec((w, hb * HEAD_DIM), lambda pp, n: (jnp.maximum(n - 1, 0), pp))
    cur = pl.BlockSpec((w, hb * HEAD_DIM), lambda pp, n: (n, pp))
    sb = pl.BlockSpec((hb, g * w, 1), lambda pp, n: (pp, 0, 0))
    return hb, qb, prev, cur, sb


def swa_fwd(q, k, v, sink, rider=None):
    s_len = q.shape[0]
    hk = k.shape[1] // HEAD_DIM
    g = q.shape[1] // k.shape[1]
    w, d = WINDOW, HEAD_DIM
    hb, qb, prev, cur, sb = _swa_specs(hk, g, s_len)

    def body(q_ref, kp_ref, kc_ref, vp_ref, vc_ref, sink_ref, o_ref):
        for hh in range(hb):
            qs = _stack_heads(q_ref, hh * g, g).astype(BF16)
            p, _ = _swa_probs(qs, _window(kp_ref, kc_ref, hh), sink_ref[hh], pl.program_id(1), w)
            o = jnp.dot(p.astype(BF16), _window(vp_ref, vc_ref, hh), preferred_element_type=F32)
            for j in range(g):
                o_ref[:, (hh * g + j) * d:(hh * g + j + 1) * d] = o[j * w:(j + 1) * w]

    (o,), rid = _call("swa_fwd", body, (hk // hb, s_len // w), [qb, prev, cur, prev, cur, sb], (q, k, k, v, v, sink), [qb],
                      [jax.ShapeDtypeStruct(q.shape, F32)], (), ("parallel", "parallel"), rider)
    return o, rid


def swa_bwd(q, k, v, sink, o, do, rider=None):
    s_len = q.shape[0]
    hk = k.shape[1] // HEAD_DIM
    g = q.shape[1] // k.shape[1]
    w, d = WINDOW, HEAD_DIM
    scale = HEAD_DIM ** -0.5
    hb, qb, prev, cur, sb = _swa_specs(hk, g, s_len)

    def body(q_ref, kp_ref, kc_ref, vp_ref, vc_ref, sink_ref, o_ref, do_ref, dq_ref, dkp_ref, dkc_ref, dvp_ref, dvc_ref, dsink_ref):
        n = pl.program_id(1)

        @pl.when(n == 0)
        def _():
            dsink_ref[...] = jnp.zeros_like(dsink_ref)

        for hh in range(hb):
            lanes = slice(hh * d, (hh + 1) * d)
            qs = _stack_heads(q_ref, hh * g, g).astype(BF16)
            kw, vw = _window(kp_ref, kc_ref, hh), _window(vp_ref, vc_ref, hh)
            p, ps = _swa_probs(qs, kw, sink_ref[hh], n, w)
            dof = _stack_heads(do_ref, hh * g, g)
            dob = dof.astype(BF16)
            delta = jnp.sum(dof * _stack_heads(o_ref, hh * g, g), axis=-1, keepdims=True)
            dp = lax.dot_general(dob, vw, _DIMS["nt"], preferred_element_type=F32)
            ds = p * (dp - delta)
            dsb = ds.astype(BF16)
            dsink_ref[hh] -= ps * delta
            dq = jnp.dot(dsb, kw, preferred_element_type=F32) * scale
            for j in range(g):
                dq_ref[:, (hh * g + j) * d:(hh * g + j + 1) * d] = dq[j * w:(j + 1) * w]
            dkw = lax.dot_general(dsb, qs, _DIMS["tn"], preferred_element_type=F32) * scale
            dvw = lax.dot_general(p.astype(BF16), dob, _DIMS["tn"], preferred_element_type=F32)
            dkp_ref[:, lanes] = dkw[:w]
            dkc_ref[:, lanes] = dkw[w:]
            dvp_ref[:, lanes] = dvw[:w]
            dvc_ref[:, lanes] = dvw[w:]

    kv_shape = jax.ShapeDtypeStruct(k.shape, F32)
    (dq, dkp, dkc, dvp, dvc, dsink), rid = _call(
        "swa_bwd", body, (hk // hb, s_len // w), [qb, prev, cur, prev, cur, sb, qb, qb], (q, k, k, v, v, sink, o, do),
        [qb, cur, cur, cur, cur, sb],
        [jax.ShapeDtypeStruct(q.shape, F32), kv_shape, kv_shape, kv_shape, kv_shape, jax.ShapeDtypeStruct((hk, g * w, 1), F32)],
        (), ("parallel", "arbitrary"), rider)

    def shift_up(a):
        return jnp.concatenate([a[w:], jnp.zeros_like(a[:w])], axis=0)

    return (dq, dkc + shift_up(dkp), dvc + shift_up(dvp), dsink), rid


def loss_call(y, target):
    s, d = y.shape
    tm = _pick(s, (512, 256, 128))

    def body(y_ref, t_ref, l_ref, dy_ref):
        e = y_ref[...] - t_ref[...]
        dy_ref[...] = e * (1.0 / d)

        @pl.when(pl.program_id(0) == 0)
        def _():
            l_ref[...] = jnp.zeros_like(l_ref)

        l_ref[...] += jnp.sum(jnp.sum(e * e, axis=0, keepdims=True), axis=1, keepdims=True) * (0.5 / d)

    row = pl.BlockSpec((tm, d), lambda i: (i, 0))
    (l, dy), _ = _call("loss_head", body, (s // tm,), [row, row], (y, target), [pl.BlockSpec((1, 1), lambda i: (0, 0)), row],
                       [jax.ShapeDtypeStruct((1, 1), F32), jax.ShapeDtypeStruct((s, d), F32)], (), ("arbitrary",))
    return l[0, 0], dy


def _row_tile(rows, cols, itemsize):
    target = max(16, (1 << 20) // (cols * itemsize))
    for t in (1024, 512, 256, 128, 64, 32, 16):
        if t <= target and rows % t == 0:
            return t
    return rows


CAST_STEPS = 8


def cast_place(name, ws, p_idx, rider=None):
    n = len(ws)
    assert all(w.shape[0] % (16 * CAST_STEPS) == 0 for w in ws), [w.shape for w in ws]

    def body(p_ref, *refs):
        for w_ref, o_ref in zip(refs[:n], refs[n:]):
            o_ref[...] = w_ref[...].astype(BF16)

    return _call(
        name, body, (CAST_STEPS,), [pl.BlockSpec((w.shape[0] // CAST_STEPS, w.shape[1]), lambda i, pr: (i, 0)) for w in ws], tuple(ws),
        [pl.BlockSpec((None, w.shape[0] // CAST_STEPS, w.shape[1]), lambda i, pr: (pr[0], i, 0)) for w in ws],
        [jax.ShapeDtypeStruct((N_CHIPS,) + w.shape, BF16) for w in ws], (), ("parallel",), rider, prefetch=(p_idx,))


def chip_sum(name, grad, theirs, c_idx):
    ns, r, cols = grad.shape
    rh = r // 2
    tr = _row_tile(rh, cols, 2)
    nb = rh // tr

    def body(c_ref, a_ref, b_ref, o_ref):
        o_ref[...] = (a_ref[...].astype(F32) + b_ref[...].astype(F32)).astype(o_ref.dtype)

    return pl.pallas_call(
        body, name=name,
        grid_spec=pltpu.PrefetchScalarGridSpec(
            num_scalar_prefetch=1, grid=(ns, nb),
            in_specs=[pl.BlockSpec((None, tr, cols), lambda q, i, cr: (q, cr[0] * nb + i, 0)),
                      pl.BlockSpec((None, tr, cols), lambda q, i, cr: (q, i, 0))],
            out_specs=pl.BlockSpec((None, tr, cols), lambda q, i, cr: (q, i, 0))),
        out_shape=jax.ShapeDtypeStruct((ns, rh, cols), BF16),
        compiler_params=pltpu.CompilerParams(dimension_semantics=("parallel", "parallel"), vmem_limit_bytes=VMEM_LIMIT),
    )(c_idx, grad, theirs)


def owner_sum(name, sums, got, pc_idx):
    ns, rh, cols = sums.shape
    tr = _row_tile(rh, cols, 4)
    nb = rh // tr

    def body(pc_ref, a_ref, b_ref, o_ref):
        o_ref[...] = ((a_ref[...].astype(F32) + b_ref[0].astype(F32)) + b_ref[1].astype(F32)) + b_ref[2].astype(F32)

    return pl.pallas_call(
        body, name=name,
        grid_spec=pltpu.PrefetchScalarGridSpec(
            num_scalar_prefetch=1, grid=(nb,),
            in_specs=[pl.BlockSpec((None, tr, cols), lambda i, pc: (pc[0], i, 0)),
                      pl.BlockSpec((3, tr, cols), lambda i, pc: (0, i, 0))],
            out_specs=pl.BlockSpec((tr, cols), lambda i, pc: (pc[1] * nb + i, 0))),
        out_shape=jax.ShapeDtypeStruct((2 * rh, cols), F32),
        compiler_params=pltpu.CompilerParams(dimension_semantics=("parallel",), vmem_limit_bytes=VMEM_LIMIT),
    )(pc_idx, sums, got)


def adamw(name, w, g, m, v):
    r, cols = w.shape
    tr = _row_tile(r, cols, 4)
    c1 = 1.0 / (1.0 - ADAM_B1 ** ADAM_STEP)
    c2 = 1.0 / (1.0 - ADAM_B2 ** ADAM_STEP)

    def body(w_ref, g_ref, m_ref, v_ref, go_ref, d_ref, nm_ref, nv_ref):
        gv = g_ref[...]
        nm = ADAM_B1 * m_ref[...] + (1.0 - ADAM_B1) * gv
        nv = ADAM_B2 * v_ref[...] + (1.0 - ADAM_B2) * (gv * gv)
        go_ref[...] = gv
        d_ref[...] = -ADAM_LR * ((nm * c1) / (jnp.sqrt(nv * c2) + ADAM_EPS) + ADAM_WD * w_ref[...])
        nm_ref[...] = nm
        nv_ref[...] = nv

    blk = pl.BlockSpec((tr, cols), lambda i: (i, 0))
    return _call(name, body, (r // tr,), [blk] * 4, (w, g, m, v), [blk] * 4, [jax.ShapeDtypeStruct((r, cols), F32)] * 4, (), ("parallel",))


def _rope(x, cos, sin):
    x3 = x.reshape(x.shape[0], -1, HEAD_DIM)
    x1, x2 = x3[..., : HEAD_DIM // 2], x3[..., HEAD_DIM // 2:]
    cos, sin = cos[:, None, :], sin[:, None, :]
    return jnp.concatenate([x1 * cos - x2 * sin, x2 * cos + x1 * sin], axis=-1).reshape(x.shape)


def _win_layout(d_model):
    hf = hq = d_model // (2 * HEAD_DIM)
    hk = hq // 4
    sizes = [hf * HEAD_DIM, hf * HEAD_DIM, hf * HEAD_DIM, hf, hq * HEAD_DIM, hk * HEAD_DIM, hk * HEAD_DIM]
    return hf, hq, hk, sizes


def _pad_lanes(a):
    return jnp.pad(a, ((0, 0), (0, -a.shape[1] % _LANES)))


def _attn_inputs(proj, sm, positions):
    ns, s_len, _ = proj.shape
    hf, hq, hk, sizes = _win_layout(sm["norm_mix_g"].shape[0])
    grp = hq // hk
    cs = sum(sizes) // ns
    full = proj[:, :, :cs].transpose(1, 0, 2).reshape(s_len, ns * cs)
    o0 = 0
    cols = []
    for sz in sizes:
        cols.append(full[:, o0:o0 + sz])
        o0 += sz
    q_f, k_f, v_f, f_logit, q_s, k_s, v_s = cols

    q_f = head_rms("fox_qnorm", q_f, sm["fox_q_norm_g"])
    k_f = head_rms("fox_knorm", k_f, sm["fox_k_norm_g"])
    log_f = jax.nn.log_sigmoid(f_logit + sm["b_forget"])
    c = jnp.cumsum(log_f, axis=0).T

    inv_freq = ROPE_THETA ** (-jnp.arange(0, HEAD_DIM, 2, dtype=F32) / HEAD_DIM)
    ang = positions.astype(F32)[:, None] * inv_freq
    cos, sin = jnp.cos(ang), jnp.sin(ang)
    q_s = _rope(head_rms("swa_qnorm", q_s, sm["swa_q_norm_g"]), cos, sin)
    k_s = _rope(head_rms("swa_knorm", k_s, sm["swa_k_norm_g"]), cos, sin)
    sink = jnp.broadcast_to(sm["swa_sinks"].reshape(hk, grp, 1, 1), (hk, grp, WINDOW, 1)).reshape(hk, grp * WINDOW, 1)
    tq = fox_tile(s_len)
    return (q_f, k_f, v_f, c[:, :, None], c.reshape(hf, s_len // tq, 1, tq)), (q_s, k_s, v_s, sink)


_BIG = ("ffn1_w_gate", "ffn1_w_up", "ffn1_w_down", "w_in", "w_out", "ffn2_w_gate", "ffn2_w_up", "ffn2_w_down")
_ROW_SHARDED = ("ffn1_w_down", "w_out", "ffn2_w_down")
_SMALL = ("norm_ffn1_g", "norm_mix_g", "b_forget", "fox_q_norm_g", "fox_k_norm_g", "swa_q_norm_g", "swa_k_norm_g", "swa_sinks",
          "out_norm_fox_g", "out_norm_swa_g", "norm_ffn2_g")
_ATTN_SMALL = ("norm_mix_g", "b_forget", "fox_q_norm_g", "fox_k_norm_g", "swa_q_norm_g", "swa_k_norm_g", "swa_sinks")
_ALL = ("norm_ffn1_g", "ffn1_w_gate", "ffn1_w_up", "ffn1_w_down", "norm_mix_g", "w_in", "b_forget", "fox_q_norm_g", "fox_k_norm_g",
        "swa_q_norm_g", "swa_k_norm_g", "swa_sinks", "out_norm_fox_g", "out_norm_swa_g", "w_out", "norm_ffn2_g", "ffn2_w_gate",
        "ffn2_w_up", "ffn2_w_down")


def _pack_small(d):
    parts = []
    for k in _SMALL:
        v = d[k].reshape(-1)
        rows = -(-v.shape[0] // _LANES)
        parts.append(jnp.pad(v, (0, rows * _LANES - v.shape[0])).reshape(rows, _LANES))
    a = jnp.concatenate(parts, axis=0)
    return jnp.pad(a, ((0, -a.shape[0] % 8), (0, 0)))


def _unpack_small(a, like):
    out, r0 = {}, 0
    for k in _SMALL:
        nvals = like[k].shape[1]
        rows = -(-nvals // _LANES)
        out[k] = a[r0:r0 + rows].reshape(-1)[:nvals].reshape(1, nvals)
        r0 += rows
    return out


def _stacked(w):
    return w.reshape(-1, w.shape[-1])


def _local_step(shards, sm, x, positions, target, p_idx, c_idx, pc_idx):
    ns = N_CHIPS
    hf, hq, hk, _ = _win_layout(x.shape[1])
    s_len = x.shape[0]
    full = {}

    def fetch(*jobs):
        names = list(dict.fromkeys(n for n, _, _ in jobs))
        return names, gather([bufs[n] for n in names], [(names.index(n), kind, part) for n, kind, part in jobs])

    def take(names, rid):
        for n, b in zip(names, rid[0]):
            bufs[n] = b

    n1 = ["ffn1_w_gate", "ffn1_w_up", "ffn1_w_down"]
    n2 = ["ffn2_w_gate", "ffn2_w_up", "ffn2_w_down"]
    later = ["w_in", "w_out"] + n2
    placed, _ = cast_place("cast_place_ffn1", [shards[n] for n in n1], p_idx)
    bufs = dict(zip(n1, placed))
    gate1, up1, down1 = n1
    gate2, up2, down2 = n2
    names, rider = fetch((gate1, "ici", WHOLE), (up1, "ici", WHOLE))
    placed, rid = cast_place("cast_place_later", [shards[n] for n in later], p_idx, rider=rider)
    bufs.update(zip(later, placed))
    take(names, rid)
    names, rider = fetch((gate1, "d2d", WHOLE), (up1, "d2d", WHOLE))
    xn1, r1, rid = rms_fwd("ffn1_norm", x, sm["norm_ffn1_g"], BF16, rider=rider)
    take(names, rid)
    names, rider = fetch((down1, "ici", WHOLE))
    (g1, u1, hid1), rid = ffn_gu("ffn1_gu", xn1, bufs[gate1], bufs[up1], rider=rider)
    take(names, rid)
    names, rider = fetch((down1, "d2d", WHOLE))
    take(names, run_step("gather_d2d_ffn1_down", rider))
    wd1 = _stacked(bufs[down1])
    names, rider = fetch(("w_in", "ici", WHOLE))
    h1, rid = mm_nn("ffn1_down", hid1, wd1, scale=0.5, resid=x, rider=rider)
    take(names, rid)

    names, rider = fetch(("w_in", "d2d", WHOLE))
    u, r_mix, rid = rms_fwd("mix_norm", h1, sm["norm_mix_g"], BF16, rider=rider)
    take(names, rid)
    names, rider = fetch(("w_out", "ici", WHOLE), (gate2, "ici", (0, 1, 4)))
    proj, rid = mm_nn_sharded("mix_inproj", u, bufs["w_in"], rider=rider)
    take(names, rid)
    sm_attn = {k: sm[k] for k in _ATTN_SMALL}
    (fox_in, swa_in), attn_vjp = jax.vjp(lambda pr, s: _attn_inputs(pr, s, positions), proj, sm_attn)
    names, rider = fetch((gate2, "ici", (1, 4, 4)), ("w_out", "d2d", WHOLE), (gate2, "d2d", (0, 1, 4)))
    (o_f, lse), rid = fox_fwd(*fox_in, rider=rider)
    take(names, rid)
    names, rider = fetch((up2, "ici", (0, 3, 4)), (gate2, "d2d", (1, 4, 4)))
    o_s, rid = swa_fwd(*swa_in, rider=rider)
    take(names, rid)
    o_fox, o_swa = o_f, o_s
    nf, r_fox = rms_fwd("out_norm_fox", o_fox, sm["out_norm_fox_g"], BF16)
    nsw, r_swa = rms_fwd("out_norm_swa", o_swa, sm["out_norm_swa_g"], BF16)
    o = jnp.concatenate([nf, nsw], axis=-1)
    wout = _stacked(bufs["w_out"])
    names, rider = fetch((up2, "ici", (3, 4, 4)), (up2, "d2d", (0, 3, 4)))
    h2, rid = mm_nn("out_proj", o, wout, resid=h1, rider=rider)
    take(names, rid)

    names, rider = fetch((up2, "d2d", (3, 4, 4)))
    xn2, r2, rid = rms_fwd("ffn2_norm", h2, sm["norm_ffn2_g"], BF16, rider=rider)
    take(names, rid)
    names, rider = fetch((down2, "ici", WHOLE))
    (g2, u2, hid2), rid = ffn_gu("ffn2_gu", xn2, bufs[gate2], bufs[up2], rider=rider)
    take(names, rid)
    names, rider = fetch((down2, "d2d", WHOLE))
    take(names, run_step("gather_d2d_ffn2_down", rider))
    wd2 = _stacked(bufs["ffn2_w_down"])
    y, _ = mm_nn("ffn2_down", hid2, wd2, scale=0.5, resid=h2)
    loss, dy = loss_call(y, target)

    red = {}

    def grad(n, g):
        red[n] = {"grad": g.reshape(ns, -1, g.shape[-1])}

    def ride(*steps):
        def done(rid):
            a0 = n0 = 0
            for rd, cb in steps:
                cb(rid[0][a0:a0 + len(rd.aliased)], rid[1][n0:n0 + len(rd.news)])
                a0, n0 = a0 + len(rd.aliased), n0 + len(rd.news)

        return (combine(*[s[0] for s in steps]) if len(steps) > 1 else steps[0][0]), done

    def xchg(*names):
        def cb(al, news):
            for n, t in zip(names, news):
                red[n]["sum"] = chip_sum("chip_sum_" + n, red[n]["grad"], t, c_idx)

        return exchange_halves([red[n]["grad"] for n in names]), cb

    def scat(n, part=WHOLE):
        def cb(al, news):
            red[n]["got"] = (al or news)[0]

        return scatter_to_owner([red[n]["sum"]], [red[n]["got"]] if "got" in red[n] else None, part), cb

    def own(n):
        red[n]["half"] = owner_sum("owner_sum_" + n, red[n]["sum"], red[n]["got"], pc_idx)

    def join(*names):
        return join_halves([red[n]["half"] for n in names]), lambda al, news: full.update(zip(names, al))

    dwd2, _ = mm_tn("ffn2_dwd", hid2, dy, out_dtype=BF16, scale=0.5)
    grad(down2, dwd2)
    rider, done = ride(xchg(down2))
    (dg2, du2), rid = ffn_dh("ffn2_dh", dy, wd2, g2, u2, ns, 0.5, rider=rider)
    done(rid)
    rider, done = ride(scat(down2, (0, 1, 2)))
    dwg2, rid = mm_tn_sharded("ffn2_dwg", xn2, dg2, ns, rider=rider)
    done(rid)
    grad(gate2, dwg2)
    rider, done = ride(scat(down2, (1, 2, 2)), xchg(gate2))
    dwu2, rid = mm_tn_sharded("ffn2_dwu", xn2, du2, ns, rider=rider)
    done(rid)
    grad(up2, dwu2)
    rider, done = ride(scat(gate2, (0, 1, 2)), xchg(up2))
    dxn, rid = mm_nt_sharded("ffn2_dxn_g", dg2, bufs[gate2], rider=rider)
    done(rid)
    rider, done = ride(scat(gate2, (1, 2, 2)))
    dxn, rid = mm_nt_sharded("ffn2_dxn_u", du2, bufs[up2], resid=dxn, rider=rider)
    done(rid)
    dh2, dgain_ffn2 = rms_bwd("ffn2_dnorm", h2, sm["norm_ffn2_g"], r2, dxn, dres=dy)
    own(down2)
    own(gate2)

    do, _ = mm_nt("out_do", dh2, wout)
    dwout, _ = mm_tn("out_dw", o, dh2, out_dtype=BF16)
    cf = o_fox.shape[1]
    d_fox, dgain_fox = rms_bwd("out_dnorm_fox", o_fox, sm["out_norm_fox_g"], r_fox, do[:, :cf])
    d_swa, dgain_swa = rms_bwd("out_dnorm_swa", o_swa, sm["out_norm_swa_g"], r_swa, do[:, cf:])
    grad("w_out", dwout)
    rider, done = ride(scat(up2))
    swa_cts, rid = swa_bwd(*swa_in, o_s, d_swa, rider=rider)
    done(rid)
    own(up2)
    rider, done = ride(xchg("w_out"), join(down2, gate2, up2))
    fox_cts, rid = fox_bwd(*fox_in, o_f, lse, d_fox, rider=rider)
    done(rid)
    dproj, dsm_attn = attn_vjp((tuple(fox_cts), tuple(swa_cts)))

    rider, done = ride(scat("w_out"))
    du, rid = mm_nt_sharded("mix_du", dproj, bufs["w_in"], rider=rider)
    done(rid)
    dwin, _ = mm_tn_sharded("mix_dwin", u, dproj, ns)
    grad("w_in", dwin)
    dh1, dgain_mix = rms_bwd("mix_dnorm", h1, sm["norm_mix_g"], r_mix, du, dres=dh2)
    own("w_out")

    rider, done = ride(xchg("w_in"))
    dwd1, rid = mm_tn("ffn1_dwd", hid1, dh1, out_dtype=BF16, scale=0.5, rider=rider)
    done(rid)
    grad(down1, dwd1)
    rider, done = ride(scat("w_in"), xchg(down1))
    (dg1, du1), rid = ffn_dh("ffn1_dh", dh1, wd1, g1, u1, ns, 0.5, rider=rider)
    done(rid)
    own("w_in")
    rider, done = ride(scat(down1, (0, 1, 2)), join("w_out", "w_in"))
    dwg1, rid = mm_tn_sharded("ffn1_dwg", xn1, dg1, ns, rider=rider)
    done(rid)
    grad(gate1, dwg1)
    rider, done = ride(scat(down1, (1, 2, 2)), xchg(gate1))
    dwu1, rid = mm_tn_sharded("ffn1_dwu", xn1, du1, ns, rider=rider)
    done(rid)
    grad(up1, dwu1)
    own(down1)
    rider, done = ride(scat(gate1, (0, 1, 2)), xchg(up1), join(down1))
    dxn, rid = mm_nt_sharded("ffn1_dxn_g", dg1, bufs[gate1], rider=rider)
    done(rid)
    rider, done = ride(scat(gate1, (1, 2, 2)), scat(up1, (0, 1, 4)))
    dxn, rid = mm_nt_sharded("ffn1_dxn_u", du1, bufs[up1], resid=dxn, rider=rider)
    done(rid)
    dx, dgain_ffn1 = rms_bwd("ffn1_dnorm", x, sm["norm_ffn1_g"], r1, dxn, dres=dh1)
    own(gate1)

    rider, done = ride(scat(up1, (1, 4, 4)), join(gate1))
    done(run_step("reduce_tail", rider))
    own(up1)
    rider, done = ride(join(up1))
    done(run_step("join_tail", rider))

    g_small = dict(dsm_attn)
    g_small["norm_mix_g"] = g_small["norm_mix_g"] + dgain_mix
    g_small.update(norm_ffn1_g=dgain_ffn1, norm_ffn2_g=dgain_ffn2, out_norm_fox_g=dgain_fox, out_norm_swa_g=dgain_swa)
    return loss, dx, full, g_small


def kernel(x, positions, norm_ffn1_g, ffn1_w_gate, ffn1_w_up, ffn1_w_down, norm_mix_g, w_in, b_forget, fox_q_norm_g, fox_k_norm_g, swa_q_norm_g, swa_k_norm_g, swa_sinks, out_norm_fox_g, out_norm_swa_g, w_out, norm_ffn2_g, ffn2_w_gate, ffn2_w_up, ffn2_w_down, loss_target, m_norm_ffn1_g, m_ffn1_w_gate, m_ffn1_w_up, m_ffn1_w_down, m_norm_mix_g, m_w_in, m_b_forget, m_fox_q_norm_g, m_fox_k_norm_g, m_swa_q_norm_g, m_swa_k_norm_g, m_swa_sinks, m_out_norm_fox_g, m_out_norm_swa_g, m_w_out, m_norm_ffn2_g, m_ffn2_w_gate, m_ffn2_w_up, m_ffn2_w_down, v_norm_ffn1_g, v_ffn1_w_gate, v_ffn1_w_up, v_ffn1_w_down, v_norm_mix_g, v_w_in, v_b_forget, v_fox_q_norm_g, v_fox_k_norm_g, v_swa_q_norm_g, v_swa_k_norm_g, v_swa_sinks, v_out_norm_fox_g, v_out_norm_swa_g, v_w_out, v_norm_ffn2_g, v_ffn2_w_gate, v_ffn2_w_up, v_ffn2_w_down):
    args = dict(locals())
    w = {k: args[k] for k in _ALL}
    m = {k: args["m_" + k] for k in _ALL}
    v = {k: args["v_" + k] for k in _ALL}
    c_idx = lax.axis_index("c").astype(jnp.int32).reshape(1)
    p_idx = (2 * lax.axis_index("x") + lax.axis_index("y")).astype(jnp.int32).reshape(1)
    pc_idx = jnp.concatenate([p_idx, c_idx])

    small = {k: w[k] for k in _SMALL}
    shards = {k: w[k][0] for k in _BIG}
    shards["w_in"] = _pad_lanes(shards["w_in"])
    loss, grad_x, g_shard, g_small = _local_step(shards, {k: w[k][0] for k in _SMALL}, x[0], positions[0], loss_target[0],
                                                 p_idx, c_idx, pc_idx)
    g_shard["w_in"] = g_shard["w_in"][:, :w["w_in"].shape[-1]]
    loss = lax.psum(loss, ("x", "y", "c"))
    g_small_sum = _unpack_small(all_reduce_small(_pack_small({k: g_small[k].reshape(1, -1) for k in _SMALL})), small)

    grad_w, delta, new_m, new_v = {}, {}, {}, {}
    for k in _BIG:
        (g, d, nm, nv), _ = adamw("adamw_" + k, w[k][0], g_shard[k], m[k][0], v[k][0])
        grad_w[k], delta[k], new_m[k], new_v[k] = g[None], d[None], nm[None], nv[None]
    (_, d, nm, nv), _ = adamw("adamw_small", _pack_small(small), _pack_small(g_small_sum), _pack_small({k: m[k] for k in _SMALL}),
                              _pack_small({k: v[k] for k in _SMALL}))
    grad_w.update(g_small_sum)
    delta.update(_unpack_small(d, small))
    new_m.update(_unpack_small(nm, small))
    new_v.update(_unpack_small(nv, small))

    return (loss, grad_x[None], *[grad_w[k] for k in _ALL], *[delta[k] for k in _ALL], *[new_m[k] for k in _ALL], *[new_v[k] for k in _ALL])
```

```python
import functools

import jax
import jax.numpy as jnp
from jax import lax
from jax.experimental import pallas as pl
from jax.experimental.pallas import tpu as pltpu

F32 = jnp.float32
BF16 = jnp.bfloat16

HEAD_DIM = 64
WINDOW = 128
ROPE_THETA = 10000.0
EPS = 1e-6
N_CHIPS = 4
N_DEV = 8

ADAM_LR = 0.001
ADAM_B1 = 0.9
ADAM_B2 = 0.999
ADAM_EPS = 1e-08
ADAM_WD = 0.01
ADAM_STEP = 10

V7X_VMEM_BYTES = 64 * 1024 * 1024
VMEM_LIMIT = V7X_VMEM_BYTES - 8 * 1024 * 1024
MASK_VALUE = -1e30

_MESH = pl.DeviceIdType.MESH
_HBM = pl.BlockSpec(memory_space=pl.ANY)
_DIMS = {"nn": (((1,), (0,)), ((), ())), "nt": (((1,), (1,)), ((), ())), "tn": (((0,), (0,)), ((), ()))}


def _pick(n, prefs):
    for p in prefs:
        if n % p == 0:
            return p
    return n


class Rider:
    def __init__(self, reads, aliased, news, nsem, build):
        self.reads, self.aliased, self.news, self.nsem, self.build = list(reads), list(aliased), list(news), nsem, build


class _Shifted:
    def __init__(self, ref, off):
        self.ref, self.off = ref, off

    @property
    def at(self):
        return self

    def __getitem__(self, k):
        return self.ref.at[k + self.off]


def combine(*riders):
    def build(reads, al, news, ssem, rsem):
        out = ([], [], [])
        r0 = a0 = n0 = s0 = 0
        for rd in riders:
            nr, na, nn = len(rd.reads), len(rd.aliased), len(rd.news)
            part = rd.build(reads[r0:r0 + nr], al[a0:a0 + na], news[n0:n0 + nn], _Shifted(ssem, s0), _Shifted(rsem, s0))
            for acc, lst in zip(out, part):
                acc.extend(lst)
            r0, a0, n0, s0 = r0 + nr, a0 + na, n0 + nn, s0 + rd.nsem
        return out

    return Rider(sum((r.reads for r in riders), []), sum((r.aliased for r in riders), []), sum((r.news for r in riders), []),
                 sum(r.nsem for r in riders), build)


def _me():
    return lax.axis_index("x"), lax.axis_index("y"), lax.axis_index("c")


def _other_chips(x, y):
    return [(1 - x, y), (x, 1 - y), (1 - x, 1 - y)]


WHOLE = (0, 1, 1)


def _rows(ref, start, rows, part=WHOLE):
    k0, k1, n = part
    assert rows % n == 0, (rows, part)
    idx = (slice(None),) * (len(ref.shape) - 2) + (pl.ds(start + k0 * (rows // n), (k1 - k0) * (rows // n)), slice(None))
    return ref.at[idx]


def _half(ref, h, part=WHOLE):
    rows = ref.shape[-2] // 2
    return _rows(ref, h * rows, rows, part)


def _remote(src, dst, ssem, rsem, k, to):
    return pltpu.make_async_remote_copy(src_ref=src, dst_ref=dst, send_sem=ssem.at[k], recv_sem=rsem.at[k], device_id=to,
                                        device_id_type=_MESH)


def _later(*args):
    return functools.partial(_remote, *args)


def gather(bufs, jobs):
    def build(reads, al, news, ssem, rsem):
        x, y, c = _me()
        p = 2 * x + y
        starts, arrivals = [], []
        for n, (b, kind, part) in enumerate(jobs):
            for j, chip in enumerate(_other_chips(x, y)):
                q = 2 * chip[0] + chip[1]
                if kind == "ici":
                    src, landing, to = _half(al[b].at[p], c, part), _half(al[b].at[q], c, part), (*chip, c)
                else:
                    src, landing, to = _half(al[b].at[q], c, part), _half(al[b].at[q], 1 - c, part), (x, y, 1 - c)
                starts.append(_later(src, src, ssem, rsem, 3 * n + j, to))
                arrivals.append(_later(landing, landing, ssem, rsem, 3 * n + j, to))
        return starts, arrivals, starts

    return Rider([], bufs, [], 3 * len(jobs), build)


def exchange_halves(grads):
    def build(reads, al, news, ssem, rsem):
        x, y, c = _me()
        cps = [_later(_half(g, 1 - c), t, ssem, rsem, w, (x, y, 1 - c)) for w, (g, t) in enumerate(zip(reads, news))]
        return cps, cps, cps

    return Rider(grads, [], [jax.ShapeDtypeStruct((g.shape[0], g.shape[1] // 2, g.shape[2]), g.dtype) for g in grads], len(grads), build)


def scatter_to_owner(sums, gots=None, part=WHOLE):
    def build(reads, al, news, ssem, rsem):
        x, y, c = _me()
        cps = []
        for w, (s, got) in enumerate(zip(reads, al or news)):
            rows = s.shape[-2]
            for j, chip in enumerate(_other_chips(x, y)):
                cps.append(_later(_rows(s.at[2 * chip[0] + chip[1]], 0, rows, part), _rows(got.at[j], 0, rows, part), ssem, rsem,
                                  3 * w + j, (*chip, c)))
        return cps, cps, cps

    news = [] if gots else [jax.ShapeDtypeStruct((3,) + s.shape[1:], s.dtype) for s in sums]
    return Rider(sums, gots or [], news, 3 * len(sums), build)


def join_halves(fulls):
    def build(reads, al, news, ssem, rsem):
        x, y, c = _me()
        starts, arrivals = [], []
        for w, f in enumerate(al):
            mine, landing = _half(f, c), _half(f, 1 - c)
            starts.append(_later(mine, mine, ssem, rsem, w, (x, y, 1 - c)))
            arrivals.append(_later(landing, landing, ssem, rsem, w, (x, y, 1 - c)))
        return starts, arrivals, starts

    return Rider([], fulls, [], len(fulls), build)


def _start_and_wait(rider, reads, al, news, ssem, rsem, first, last):
    @pl.when(first)
    def _():
        for cp in rider.build(reads, al, news, ssem, rsem)[0]:
            cp().start()

    def finish():
        @pl.when(last)
        def _():
            _, arrivals, sends = rider.build(reads, al, news, ssem, rsem)
            for cp in arrivals:
                cp().wait_recv()
            for cp in sends:
                cp().wait_send()

    return finish


def _call(name, body, grid, in_specs, args, out_specs, out_shape, scratch=(), semantics=None, rider=None, prefetch=()):
    n_pre, n_in, n_out, n_scr = len(prefetch), len(args), len(out_shape), len(scratch)
    nr, na, nn = (len(rider.reads), len(rider.aliased), len(rider.news)) if rider else (0, 0, 0)

    def wrapped(*refs):
        pre, refs = refs[:n_pre], refs[n_pre:]
        ins, reads = refs[:n_in], refs[n_in:n_in + nr]
        o0 = n_in + nr + na
        outs, al, news = refs[o0:o0 + n_out], refs[o0 + n_out:o0 + n_out + na], refs[o0 + n_out + na:o0 + n_out + na + nn]
        s0 = o0 + n_out + na + nn
        scr, (ssem, rsem) = refs[s0:s0 + n_scr], refs[s0 + n_scr:]
        first = functools.reduce(jnp.logical_and, [pl.program_id(a) == 0 for a in range(len(grid))])
        last = functools.reduce(jnp.logical_and, [pl.program_id(a) == g - 1 for a, g in enumerate(grid)])
        finish = _start_and_wait(rider, reads, al, news, ssem, rsem, first, last)
        body(*pre, *ins, *outs, *scr)
        finish()

    kernel_fn, all_in, all_out, shapes, scr = body, list(in_specs), list(out_specs), list(out_shape), list(scratch)
    operands, aliases = (*prefetch, *args), {}
    if rider:
        kernel_fn, semantics = wrapped, ("arbitrary",) * len(grid)
        all_in += [_HBM] * (nr + na)
        all_out += [_HBM] * (na + nn)
        shapes += [jax.ShapeDtypeStruct(a.shape, a.dtype) for a in rider.aliased] + rider.news
        scr += [pltpu.SemaphoreType.DMA((rider.nsem,)), pltpu.SemaphoreType.DMA((rider.nsem,))]
        operands += (*rider.reads, *rider.aliased)
        aliases = {n_pre + n_in + nr + i: n_out + i for i in range(na)}
    params = pltpu.CompilerParams(dimension_semantics=semantics, vmem_limit_bytes=VMEM_LIMIT)
    if n_pre:
        spec = pltpu.PrefetchScalarGridSpec(num_scalar_prefetch=n_pre, grid=grid, in_specs=all_in, out_specs=all_out, scratch_shapes=scr)
        outs = pl.pallas_call(kernel_fn, name=name, grid_spec=spec, out_shape=shapes, input_output_aliases=aliases, compiler_params=params)(*operands)
    else:
        outs = pl.pallas_call(kernel_fn, name=name, grid=grid, in_specs=all_in, out_specs=all_out, out_shape=shapes, scratch_shapes=scr,
                              input_output_aliases=aliases, compiler_params=params)(*operands)
    return list(outs[:n_out]), ((list(outs[n_out:n_out + na]), list(outs[n_out + na:])) if rider else None)


def run_step(name, rider):
    nr, na, nn = len(rider.reads), len(rider.aliased), len(rider.news)

    def body(*refs):
        reads = refs[:nr]
        al, news = refs[nr + na:nr + 2 * na], refs[nr + 2 * na:nr + 2 * na + nn]
        ssem, rsem = refs[nr + 2 * na + nn:]
        starts, arrivals, sends = rider.build(reads, al, news, ssem, rsem)
        for cp in starts:
            cp().start()
        for cp in arrivals:
            cp().wait_recv()
        for cp in sends:
            cp().wait_send()

    outs = pl.pallas_call(
        body, name=name, in_specs=[_HBM] * (nr + na), out_specs=[_HBM] * (na + nn),
        out_shape=[jax.ShapeDtypeStruct(a.shape, a.dtype) for a in rider.aliased] + rider.news,
        input_output_aliases={nr + i: i for i in range(na)},
        scratch_shapes=[pltpu.SemaphoreType.DMA((rider.nsem,)), pltpu.SemaphoreType.DMA((rider.nsem,))],
    )(*rider.reads, *rider.aliased)
    return list(outs[:na]), list(outs[na:])


def all_reduce_small(v):
    rows, lanes = v.shape

    def body(v_ref, o_ref, slots, send_sems, recv_sems):
        x, y, c = _me()
        me = 4 * x + 2 * y + c
        slots[me] = v_ref[...]
        cps = []
        for k in range(1, N_DEV):
            peer = (x ^ (k >> 2), y ^ ((k >> 1) & 1), c ^ (k & 1))
            cps.append(_remote(v_ref, slots.at[me], send_sems, recv_sems, k - 1, peer))
            cps[-1].start()
        for k in range(1, N_DEV):
            theirs = slots.at[me ^ k]
            _remote(theirs, theirs, send_sems, recv_sems, k - 1, (x, y, c)).wait_recv()
        for cp in cps:
            cp.wait_send()
        acc = slots[0]
        for i in range(1, N_DEV):
            acc = acc + slots[i]
        o_ref[...] = acc

    return pl.pallas_call(
        body, name="all_reduce_small",
        in_specs=[pl.BlockSpec(memory_space=pltpu.VMEM)], out_specs=pl.BlockSpec(memory_space=pltpu.VMEM),
        out_shape=jax.ShapeDtypeStruct((rows, lanes), F32),
        scratch_shapes=[pltpu.VMEM((N_DEV, rows, lanes), F32), pltpu.SemaphoreType.DMA((N_DEV - 1,)), pltpu.SemaphoreType.DMA((N_DEV - 1,))],
    )(v)


def _mm_call(name, mode, a, b, a_spec, b_spec, out_shape, out_spec, grid, acc_shape, scale=1.0, resid=None, resid_spec=None, rider=None):
    nk = grid[2]
    dims = _DIMS[mode]
    has_resid = resid is not None

    def body(*refs):
        a_ref, b_ref = refs[:2]
        r_ref = refs[2] if has_resid else None
        o_ref = refs[3] if has_resid else refs[2]

        def finish(r):
            if scale != 1.0:
                r = r * scale
            if has_resid:
                r = r_ref[...].astype(F32) + r
            o_ref[...] = r.astype(o_ref.dtype)

        part = lax.dot_general(a_ref[...].astype(BF16), b_ref[...].astype(BF16), dims, preferred_element_type=F32)
        if nk == 1:
            finish(part)
            return
        acc_ref = refs[-1]
        k = pl.program_id(2)

        @pl.when(k == 0)
        def _():
            acc_ref[...] = part

        @pl.when(k > 0)
        def _():
            acc_ref[...] += part

        @pl.when(k == nk - 1)
        def _():
            finish(acc_ref[...])

    in_specs = [a_spec, b_spec] + ([resid_spec] if has_resid else [])
    args = (a, b) + ((resid,) if has_resid else ())
    (out,), rid = _call(name, body, grid, in_specs, args, [out_spec], [out_shape], [pltpu.VMEM(acc_shape, F32)] if nk > 1 else [],
                        ("parallel", "parallel", "arbitrary"), rider)
    return out, rid


MM_VMEM_BUDGET = 40 * 1024 * 1024
_TILE_OPTS = (2048, 1408, 1024, 512, 256, 128)


def _tiles(m, n, kd, a_item, b_item, o_item, r_item=0, tm=None, tn=None, tk=None):
    def opts(full, fixed, cap):
        return [fixed] if fixed else [t for t in _TILE_OPTS if t <= cap and full % t == 0] or [full]

    best = None
    for cm in opts(m, tm, 1408):
        for cn in opts(n, tn, 1408):
            for ck in opts(kd, tk, 2048):
                blocks = cm * ck * a_item + ck * cn * b_item + cm * cn * (o_item + r_item)
                casts = (cm * ck * 2 if a_item == 4 else 0) + (ck * cn * 2 if b_item == 4 else 0)
                if 2 * blocks + cm * cn * 4 + casts <= MM_VMEM_BUDGET:
                    key = (cm * cn * ck, ck)
                    if best is None or key > best[0]:
                        best = (key, (cm, cn, ck))
    assert best is not None, (m, n, kd)
    return best[1]


def _item(x):
    return jnp.dtype(x.dtype).itemsize


def mm_nn(name, a, b, *, out_dtype=F32, scale=1.0, resid=None, rider=None):
    m, kd = a.shape
    n = b.shape[1]
    tm, tn, tk = _tiles(m, n, kd, _item(a), _item(b), jnp.dtype(out_dtype).itemsize, 0 if resid is None else _item(resid))
    o_spec = pl.BlockSpec((tm, tn), lambda i, j, k: (i, j))
    return _mm_call(
        name, "nn", a, b, pl.BlockSpec((tm, tk), lambda i, j, k: (i, k)), pl.BlockSpec((tk, tn), lambda i, j, k: (k, j)),
        jax.ShapeDtypeStruct((m, n), out_dtype), o_spec, (m // tm, n // tn, kd // tk), (tm, tn), scale, resid, o_spec, rider)


def mm_nt(name, a, b, *, out_dtype=F32, scale=1.0, resid=None, rider=None):
    m, kd = a.shape
    n = b.shape[0]
    tm, tn, tk = _tiles(m, n, kd, _item(a), _item(b), jnp.dtype(out_dtype).itemsize, 0 if resid is None else _item(resid))
    o_spec = pl.BlockSpec((tm, tn), lambda i, j, k: (i, j))
    return _mm_call(
        name, "nt", a, b, pl.BlockSpec((tm, tk), lambda i, j, k: (i, k)), pl.BlockSpec((tn, tk), lambda i, j, k: (j, k)),
        jax.ShapeDtypeStruct((m, n), out_dtype), o_spec, (m // tm, n // tn, kd // tk), (tm, tn), scale, resid, o_spec, rider)


def mm_tn(name, a, b, *, out_dtype=F32, scale=1.0, rider=None):
    kd, m = a.shape
    n = b.shape[1]
    tm, tn, tk = _tiles(m, n, kd, _item(a), _item(b), jnp.dtype(out_dtype).itemsize)
    return _mm_call(
        name, "tn", a, b, pl.BlockSpec((tk, tm), lambda i, j, k: (k, i)), pl.BlockSpec((tk, tn), lambda i, j, k: (k, j)),
        jax.ShapeDtypeStruct((m, n), out_dtype), pl.BlockSpec((tm, tn), lambda i, j, k: (i, j)),
        (m // tm, n // tn, kd // tk), (tm, tn), scale, rider=rider)


def mm_nt_sharded(name, a, w, *, resid=None, rider=None):
    m = a.shape[0]
    ns, n, c = w.shape
    tm, tn, _ = _tiles(m, n, c, _item(a), _item(w), 4, 0 if resid is None else _item(resid), tk=c)
    o_spec = pl.BlockSpec((tm, tn), lambda i, j, k: (i, j))
    return _mm_call(
        name, "nt", a, w, pl.BlockSpec((tm, c), lambda i, j, k: (i, k)), pl.BlockSpec((None, tn, c), lambda i, j, k: (k, j, 0)),
        jax.ShapeDtypeStruct((m, n), F32), o_spec, (m // tm, n // tn, ns), (tm, tn), 1.0, resid, o_spec, rider)


def mm_tn_sharded(name, a, b, ns, *, rider=None):
    kd, m = a.shape
    c = b.shape[1] // ns
    tm, _, tk = _tiles(m, c, kd, _item(a), _item(b), 2, tn=c)
    return _mm_call(
        name, "tn", a, b, pl.BlockSpec((tk, tm), lambda i, j, k: (k, i)), pl.BlockSpec((tk, c), lambda i, j, k: (k, j)),
        jax.ShapeDtypeStruct((ns, m, c), BF16), pl.BlockSpec((None, tm, c), lambda i, j, k: (j, i, 0)),
        (m // tm, ns, kd // tk), (tm, c), rider=rider)


def rms_fwd(name, x, g, out_dtype, rider=None):
    r, c = x.shape
    tm = _pick(r, (512, 256, 128, 64, 8))

    def body(x_ref, g_ref, y_ref, r_ref):
        xf = x_ref[...].astype(F32)
        rstd = lax.rsqrt(jnp.mean(xf * xf, axis=-1, keepdims=True) + EPS)
        y_ref[...] = ((xf * rstd) * g_ref[...]).astype(y_ref.dtype)
        r_ref[...] = rstd

    (y, rstd), rid = _call(
        name, body, (r // tm,), [pl.BlockSpec((tm, c), lambda i: (i, 0)), pl.BlockSpec((1, c), lambda i: (0, 0))], (x, g.reshape(1, c)),
        [pl.BlockSpec((tm, c), lambda i: (i, 0)), pl.BlockSpec((tm, 1), lambda i: (i, 0))],
        [jax.ShapeDtypeStruct((r, c), out_dtype), jax.ShapeDtypeStruct((r, 1), F32)], (), ("parallel",), rider)
    return (y, rstd) if rider is None else (y, rstd, rid)


def rms_bwd(name, x, g, rstd, dy, dres=None):
    r, c = x.shape
    tm = _pick(r, (512, 256, 128, 64, 8))
    has_res = dres is not None

    def body(*refs):
        if has_res:
            x_ref, g_ref, r_ref, dy_ref, dres_ref, dx_ref, dg_ref = refs
        else:
            x_ref, g_ref, r_ref, dy_ref, dx_ref, dg_ref = refs
        xhat = x_ref[...].astype(F32) * r_ref[...]
        dyf = dy_ref[...].astype(F32)
        gdy = dyf * g_ref[...]
        dx = r_ref[...] * (gdy - xhat * jnp.mean(gdy * xhat, axis=-1, keepdims=True))
        if has_res:
            dx = dx + dres_ref[...]
        dx_ref[...] = dx

        @pl.when(pl.program_id(0) == 0)
        def _():
            dg_ref[...] = jnp.zeros_like(dg_ref)

        dg_ref[...] += jnp.sum(dyf * xhat, axis=0, keepdims=True)

    row = pl.BlockSpec((tm, c), lambda i: (i, 0))
    in_specs = [row, pl.BlockSpec((1, c), lambda i: (0, 0)), pl.BlockSpec((tm, 1), lambda i: (i, 0)), row] + ([row] if has_res else [])
    args = (x, g.reshape(1, c), rstd, dy) + ((dres,) if has_res else ())
    (dx, dg), _ = _call(name, body, (r // tm,), in_specs, args, [row, pl.BlockSpec((1, c), lambda i: (0, 0))],
                        [jax.ShapeDtypeStruct((r, c), F32), jax.ShapeDtypeStruct((1, c), F32)], (), ("arbitrary",))
    return dx, dg.reshape(c)


_LANES = 128


def _head_mean(v):
    if v.shape[1] == HEAD_DIM:
        return jnp.mean(v, axis=-1, keepdims=True)
    low = lax.broadcasted_iota(jnp.int32, v.shape, 1) < HEAD_DIM
    lo = jnp.sum(jnp.where(low, v, 0.0), axis=-1, keepdims=True)
    hi = jnp.sum(jnp.where(low, 0.0, v), axis=-1, keepdims=True)
    return jnp.where(low, lo, hi) * (1.0 / HEAD_DIM)


def _head_groups(c):
    width = _LANES if c % _LANES == 0 else HEAD_DIM
    assert c % width == 0, c
    return width, [slice(k * width, (k + 1) * width) for k in range(c // width)]


def _head_gain(g, width):
    return jnp.tile(g.reshape(1, HEAD_DIM), (1, width // HEAD_DIM))


def head_rms_fwd(name, x, g):
    s, c = x.shape
    tm = _pick(s, (256, 128, 8))
    width, groups = _head_groups(c)

    def body(x_ref, g_ref, y_ref):
        for sl in groups:
            xs = x_ref[:, sl]
            y_ref[:, sl] = (xs * lax.rsqrt(_head_mean(xs * xs) + EPS)) * g_ref[...]

    row = pl.BlockSpec((tm, c), lambda i: (i, 0))
    (y,), _ = _call(name, body, (s // tm,), [row, pl.BlockSpec((1, width), lambda i: (0, 0))], (x, _head_gain(g, width)), [row],
                    [jax.ShapeDtypeStruct((s, c), F32)], (), ("parallel",))
    return y


def head_rms_bwd(name, x, g, dy):
    s, c = x.shape
    tm = _pick(s, (256, 128, 8))
    width, groups = _head_groups(c)

    def body(x_ref, g_ref, dy_ref, dx_ref, dg_ref):
        @pl.when(pl.program_id(0) == 0)
        def _():
            dg_ref[...] = jnp.zeros_like(dg_ref)

        for sl in groups:
            xs, dys = x_ref[:, sl], dy_ref[:, sl]
            rstd = lax.rsqrt(_head_mean(xs * xs) + EPS)
            xhat = xs * rstd
            gdy = dys * g_ref[...]
            dx_ref[:, sl] = rstd * (gdy - xhat * _head_mean(gdy * xhat))
            dg_ref[...] += jnp.sum(dys * xhat, axis=0, keepdims=True)

    row = pl.BlockSpec((tm, c), lambda i: (i, 0))
    vec = pl.BlockSpec((1, width), lambda i: (0, 0))
    (dx, dg), _ = _call(name, body, (s // tm,), [row, vec, row], (x, _head_gain(g, width), dy), [row, vec],
                        [jax.ShapeDtypeStruct((s, c), F32), jax.ShapeDtypeStruct((1, width), F32)], (), ("arbitrary",))
    return dx, jnp.sum(dg.reshape(width // HEAD_DIM, HEAD_DIM), axis=0)


@functools.partial(jax.custom_vjp, nondiff_argnums=(0,))
def head_rms(name, x, g):
    return head_rms_fwd(name + "_fwd", x, g)


def _head_rms_fwd(name, x, g):
    return head_rms_fwd(name + "_fwd", x, g), (x, g)


def _head_rms_bwd(name, res, dy):
    return head_rms_bwd(name + "_bwd", *res, dy)


head_rms.defvjp(_head_rms_fwd, _head_rms_bwd)


FFN_TM = 512


def _sigmoid(x):
    return 1.0 / (1.0 + jnp.exp(-x))


def ffn_gu(name, xn, wg, wu, rider=None):
    s, d = xn.shape
    ns, _, c = wg.shape
    tm = _pick(s, (FFN_TM, 128))

    def body(x_ref, wg_ref, wu_ref, g_ref, u_ref, h_ref):
        xb = x_ref[...]
        gv = jnp.dot(xb, wg_ref[...], preferred_element_type=F32)
        uv = jnp.dot(xb, wu_ref[...], preferred_element_type=F32)
        g_ref[...] = gv
        u_ref[...] = uv
        h_ref[...] = ((gv * _sigmoid(gv)) * uv).astype(BF16)

    w_spec = pl.BlockSpec((None, d, c), lambda j, i: (j, 0, 0))
    o_spec = pl.BlockSpec((tm, c), lambda j, i: (i, j))
    return _call(
        name, body, (ns, s // tm), [pl.BlockSpec((tm, d), lambda j, i: (i, 0)), w_spec, w_spec], (xn, wg, wu),
        [o_spec, o_spec, o_spec],
        [jax.ShapeDtypeStruct((s, ns * c), F32), jax.ShapeDtypeStruct((s, ns * c), F32), jax.ShapeDtypeStruct((s, ns * c), BF16)],
        [], ("parallel", "parallel"), rider)


def ffn_dh(name, dy, wd, g, u, ns, scale, rider=None):
    s, d = dy.shape
    f = wd.shape[0]
    c = f // ns
    tm = _pick(s, (FFN_TM, 128))

    def body(dy_ref, wd_ref, g_ref, u_ref, dg_ref, du_ref):
        dh = lax.dot_general(dy_ref[...].astype(BF16), wd_ref[...], _DIMS["nt"], preferred_element_type=F32) * scale
        gv, uv = g_ref[...], u_ref[...]
        sig = _sigmoid(gv)
        dg_ref[...] = (dh * uv * (sig * (1.0 + gv * (1.0 - sig)))).astype(BF16)
        du_ref[...] = (dh * (gv * sig)).astype(BF16)

    o_spec = pl.BlockSpec((tm, c), lambda j, i: (i, j))
    return _call(
        name, body, (ns, s // tm),
        [pl.BlockSpec((tm, d), lambda j, i: (i, 0)), pl.BlockSpec((c, d), lambda j, i: (j, 0)), o_spec, o_spec], (dy, wd, g, u),
        [o_spec, o_spec], [jax.ShapeDtypeStruct((s, f), BF16), jax.ShapeDtypeStruct((s, f), BF16)],
        [], ("parallel", "parallel"), rider)


FOX_TQ = 512


def fox_tile(s_len):
    return min(FOX_TQ, s_len)


def _heads_per_block(h):
    return 2 if h % 2 == 0 else 1


def _fox_queries(q):
    return (q * (HEAD_DIM ** -0.5)).astype(BF16)


def _fox_scores(qs, kc, cq, ck, diagonal):
    s = lax.dot_general(qs, kc.astype(BF16), _DIMS["nt"], preferred_element_type=F32) + cq - ck
    if not diagonal:
        return s
    return jnp.where(lax.broadcasted_iota(jnp.int32, s.shape, 0) >= lax.broadcasted_iota(jnp.int32, s.shape, 1), s, MASK_VALUE)


def _fox_specs(h, s_len, tq):
    hb = _heads_per_block(h)
    qb = pl.BlockSpec((tq, hb * HEAD_DIM), lambda pp, i: (i, pp))
    kb = pl.BlockSpec((s_len, hb * HEAD_DIM), lambda pp, i: (0, pp))
    colb = pl.BlockSpec((hb, tq, 1), lambda pp, i: (pp, i, 0))
    rowb = pl.BlockSpec((hb, s_len // tq, 1, tq), lambda pp, i: (pp, 0, 0, 0))
    return hb, qb, kb, colb, rowb


def fox_fwd(q, k, v, cq, ck, rider=None):
    s_len, hd = q.shape
    h, d = hd // HEAD_DIM, HEAD_DIM
    tq = fox_tile(s_len)
    hb, qb, kb, colb, rowb = _fox_specs(h, s_len, tq)

    def body(q_ref, k_ref, v_ref, cq_ref, ck_ref, o_ref, lse_ref):
        i = pl.program_id(1)
        for hh in range(hb):
            lanes = slice(hh * d, (hh + 1) * d)
            qs, cqv = _fox_queries(q_ref[:, lanes]), cq_ref[hh]

            def chunk(c, carry, diagonal=False):
                m, l, acc = carry
                rows = pl.ds(pl.multiple_of(c * tq, tq), tq)
                s = _fox_scores(qs, k_ref[rows, lanes], cqv, ck_ref[hh, c], diagonal)
                m_new = jnp.maximum(m, jnp.max(s, axis=-1, keepdims=True))
                alpha = jnp.exp(m - m_new)
                p = jnp.exp(s - m_new)
                acc = alpha * acc + jnp.dot(p.astype(BF16), v_ref[rows, lanes].astype(BF16), preferred_element_type=F32)
                return m_new, alpha * l + jnp.sum(p, axis=-1, keepdims=True), acc

            init = (jnp.full((tq, 1), MASK_VALUE, F32), jnp.zeros((tq, 1), F32), jnp.zeros((tq, d), F32))
            m, l, acc = chunk(i, lax.fori_loop(0, i, chunk, init), diagonal=True)
            o_ref[:, lanes] = acc / l
            lse_ref[hh] = m + jnp.log(l)

    return _call(
        "fox_fwd", body, (h // hb, s_len // tq), [qb, kb, kb, colb, rowb], (q, k, v, cq, ck), [qb, colb],
        [jax.ShapeDtypeStruct((s_len, hd), F32), jax.ShapeDtypeStruct((h, s_len, 1), F32)], (), ("parallel", "parallel"), rider)


def fox_bwd(q, k, v, cq, ck, o, lse, do, rider=None):
    s_len, hd = q.shape
    h, d = hd // HEAD_DIM, HEAD_DIM
    tq = fox_tile(s_len)
    scale = HEAD_DIM ** -0.5
    hb, qb, kb, colb, rowb = _fox_specs(h, s_len, tq)

    def body(q_ref, k_ref, v_ref, cq_ref, ck_ref, o_ref, lse_ref, do_ref, dq_ref, dk_ref, dv_ref, dcq_ref, dck_ref):
        i = pl.program_id(1)

        @pl.when(i == 0)
        def _():
            dk_ref[...] = jnp.zeros_like(dk_ref)
            dv_ref[...] = jnp.zeros_like(dv_ref)
            dck_ref[...] = jnp.zeros_like(dck_ref)

        heads = []
        for hh in range(hb):
            lanes = slice(hh * d, (hh + 1) * d)
            dof = do_ref[:, lanes]
            heads.append((lanes, _fox_queries(q_ref[:, lanes]), cq_ref[hh], lse_ref[hh], dof.astype(BF16),
                          jnp.sum(dof * o_ref[:, lanes], axis=-1, keepdims=True)))

        def chunk(c, carry, diagonal=False):
            rows = pl.ds(pl.multiple_of(c * tq, tq), tq)
            out, dks, dvs = [], [], []
            for hh, (lanes, qs, cqv, lse_h, dob, delta) in enumerate(heads):
                dq, dcq = carry[hh]
                kc = k_ref[rows, lanes]
                p = jnp.exp(_fox_scores(qs, kc, cqv, ck_ref[hh, c], diagonal) - lse_h)
                dp = lax.dot_general(dob, v_ref[rows, lanes].astype(BF16), _DIMS["nt"], preferred_element_type=F32)
                ds = p * (dp - delta)
                dsb = ds.astype(BF16)
                dvs.append(lax.dot_general(p.astype(BF16), dob, _DIMS["tn"], preferred_element_type=F32))
                dks.append(lax.dot_general(dsb, qs, _DIMS["tn"], preferred_element_type=F32))
                dck_ref[hh, c] -= jnp.sum(ds, axis=0, keepdims=True)
                out.append((dq + jnp.dot(dsb, kc.astype(BF16), preferred_element_type=F32), dcq + jnp.sum(ds, axis=-1, keepdims=True)))
            dk_ref[rows, :] += jnp.concatenate(dks, axis=1)
            dv_ref[rows, :] += jnp.concatenate(dvs, axis=1)
            return tuple(out)

        init = tuple((jnp.zeros((tq, d), F32), jnp.zeros((tq, 1), F32)) for _ in range(hb))
        done = chunk(i, lax.fori_loop(0, i, chunk, init), diagonal=True)
        dq_ref[...] = jnp.concatenate([dq for dq, _ in done], axis=1) * scale
        for hh, (_, dcq) in enumerate(done):
            dcq_ref[hh] = dcq

    return _call(
        "fox_bwd", body, (h // hb, s_len // tq), [qb, kb, kb, colb, rowb, qb, colb, qb], (q, k, v, cq, ck, o, lse, do),
        [qb, kb, kb, colb, rowb],
        [jax.ShapeDtypeStruct((s_len, hd), F32)] * 3
        + [jax.ShapeDtypeStruct((h, s_len, 1), F32), jax.ShapeDtypeStruct((h, s_len // tq, 1, tq), F32)],
        (), ("parallel", "arbitrary"), rider)


def _stack_heads(ref, first, g):
    return jnp.concatenate([ref[:, (first + j) * HEAD_DIM:(first + j + 1) * HEAD_DIM] for j in range(g)], axis=0)


def _window(prev_ref, cur_ref, hh):
    lanes = slice(hh * HEAD_DIM, (hh + 1) * HEAD_DIM)
    return jnp.concatenate([prev_ref[:, lanes], cur_ref[:, lanes]], axis=0).astype(BF16)


def _swa_probs(q, kw, sink, n, w):
    rows = q.shape[0]
    s = lax.dot_general(q, kw, _DIMS["nt"], preferred_element_type=F32) * (HEAD_DIM ** -0.5)
    t = lax.broadcasted_iota(jnp.int32, (rows, 2 * w), 0) & (w - 1)
    col = lax.broadcasted_iota(jnp.int32, (rows, 2 * w), 1)
    rel = t + w - col
    valid = (rel >= 0) & (rel < w) & ((col >= w) | (n > 0))
    s = jnp.where(valid, s, MASK_VALUE)
    m = jnp.maximum(jnp.max(s, axis=-1, keepdims=True), sink)
    p = jnp.exp(s - m)
    ps = jnp.exp(sink - m)
    linv = 1.0 / (jnp.sum(p, axis=-1, keepdims=True) + ps)
    return p * linv, ps * linv


def _swa_specs(hk, g, s_len):
    w = WINDOW
    assert w & (w - 1) == 0 and s_len % w == 0
    hb = _heads_per_block(hk)
    qb = pl.BlockSpec((w, hb * g * HEAD_DIM), lambda pp, n: (n, pp))
    prev = pl.BlockSpec((w, hb * HEAD_DIM), lambda pp, n: (jnp.maximum(n - 1, 0), pp))
    cur = pl.BlockSpec((w, hb * HEAD_DIM), lambda pp, n: (n, pp))
    sb = pl.BlockSpec((hb, g * w, 1), lambda pp, n: (pp, 0, 0))
    return hb, qb, prev, cur, sb


def swa_fwd(q, k, v, sink, rider=None):
    s_len = q.shape[0]
    hk = k.shape[1] // HEAD_DIM
    g = q.shape[1] // k.shape[1]
    w, d = WINDOW, HEAD_DIM
    hb, qb, prev, cur, sb = _swa_specs(hk, g, s_len)

    def body(q_ref, kp_ref, kc_ref, vp_ref, vc_ref, sink_ref, o_ref):
        for hh in range(hb):
            qs = _stack_heads(q_ref, hh * g, g).astype(BF16)
            p, _ = _swa_probs(qs, _window(kp_ref, kc_ref, hh), sink_ref[hh], pl.program_id(1), w)
            o = jnp.dot(p.astype(BF16), _window(vp_ref, vc_ref, hh), preferred_element_type=F32)
            for j in range(g):
                o_ref[:, (hh * g + j) * d:(hh * g + j + 1) * d] = o[j * w:(j + 1) * w]

    (o,), rid = _call("swa_fwd", body, (hk // hb, s_len // w), [qb, prev, cur, prev, cur, sb], (q, k, k, v, v, sink), [qb],
                      [jax.ShapeDtypeStruct(q.shape, F32)], (), ("parallel", "parallel"), rider)
    return o, rid


def swa_bwd(q, k, v, sink, o, do, rider=None):
    s_len = q.shape[0]
    hk = k.shape[1] // HEAD_DIM
    g = q.shape[1] // k.shape[1]
    w, d = WINDOW, HEAD_DIM
    scale = HEAD_DIM ** -0.5
    hb, qb, prev, cur, sb = _swa_specs(hk, g, s_len)

    def body(q_ref, kp_ref, kc_ref, vp_ref, vc_ref, sink_ref, o_ref, do_ref, dq_ref, dkp_ref, dkc_ref, dvp_ref, dvc_ref, dsink_ref):
        n = pl.program_id(1)

        @pl.when(n == 0)
        def _():
            dsink_ref[...] = jnp.zeros_like(dsink_ref)

        for hh in range(hb):
            lanes = slice(hh * d, (hh + 1) * d)
            qs = _stack_heads(q_ref, hh * g, g).astype(BF16)
            kw, vw = _window(kp_ref, kc_ref, hh), _window(vp_ref, vc_ref, hh)
            p, ps = _swa_probs(qs, kw, sink_ref[hh], n, w)
            dof = _stack_heads(do_ref, hh * g, g)
            dob = dof.astype(BF16)
            delta = jnp.sum(dof * _stack_heads(o_ref, hh * g, g), axis=-1, keepdims=True)
            dp = lax.dot_general(dob, vw, _DIMS["nt"], preferred_element_type=F32)
            ds = p * (dp - delta)
            dsb = ds.astype(BF16)
            dsink_ref[hh] -= ps * delta
            dq = jnp.dot(dsb, kw, preferred_element_type=F32) * scale
            for j in range(g):
                dq_ref[:, (hh * g + j) * d:(hh * g + j + 1) * d] = dq[j * w:(j + 1) * w]
            dkw = lax.dot_general(dsb, qs, _DIMS["tn"], preferred_element_type=F32) * scale
            dvw = lax.dot_general(p.astype(BF16), dob, _DIMS["tn"], preferred_element_type=F32)
            dkp_ref[:, lanes] = dkw[:w]
            dkc_ref[:, lanes] = dkw[w:]
            dvp_ref[:, lanes] = dvw[:w]
            dvc_ref[:, lanes] = dvw[w:]

    kv_shape = jax.ShapeDtypeStruct(k.shape, F32)
    (dq, dkp, dkc, dvp, dvc, dsink), rid = _call(
        "swa_bwd", body, (hk // hb, s_len // w), [qb, prev, cur, prev, cur, sb, qb, qb], (q, k, k, v, v, sink, o, do),
        [qb, cur, cur, cur, cur, sb],
        [jax.ShapeDtypeStruct(q.shape, F32), kv_shape, kv_shape, kv_shape, kv_shape, jax.ShapeDtypeStruct((hk, g * w, 1), F32)],
        (), ("parallel", "arbitrary"), rider)

    def shift_up(a):
        return jnp.concatenate([a[w:], jnp.zeros_like(a[:w])], axis=0)

    return (dq, dkc + shift_up(dkp), dvc + shift_up(dvp), dsink), rid


def loss_call(y, target):
    s, d = y.shape
    tm = _pick(s, (512, 256, 128))

    def body(y_ref, t_ref, l_ref, dy_ref):
        e = y_ref[...] - t_ref[...]
        dy_ref[...] = e * (1.0 / d)

        @pl.when(pl.program_id(0) == 0)
        def _():
            l_ref[...] = jnp.zeros_like(l_ref)

        l_ref[...] += jnp.sum(jnp.sum(e * e, axis=0, keepdims=True), axis=1, keepdims=True) * (0.5 / d)

    row = pl.BlockSpec((tm, d), lambda i: (i, 0))
    (l, dy), _ = _call("loss_head", body, (s // tm,), [row, row], (y, target), [pl.BlockSpec((1, 1), lambda i: (0, 0)), row],
                       [jax.ShapeDtypeStruct((1, 1), F32), jax.ShapeDtypeStruct((s, d), F32)], (), ("arbitrary",))
    return l[0, 0], dy


def _row_tile(rows, cols, itemsize, block_bytes=1 << 20):
    target = max(16, block_bytes // (cols * itemsize))
    fits = [t for t in range(16, rows + 1, 16) if rows % t == 0 and t <= target]
    return fits[-1] if fits else rows


CAST_STEPS = 8


def cast_place(name, ws, p_idx, rider=None):
    n = len(ws)
    assert all(w.shape[0] % (16 * CAST_STEPS) == 0 for w in ws), [w.shape for w in ws]

    def body(p_ref, *refs):
        for w_ref, o_ref in zip(refs[:n], refs[n:]):
            o_ref[...] = w_ref[...].astype(BF16)

    return _call(
        name, body, (CAST_STEPS,), [pl.BlockSpec((w.shape[0] // CAST_STEPS, w.shape[1]), lambda i, pr: (i, 0)) for w in ws], tuple(ws),
        [pl.BlockSpec((None, w.shape[0] // CAST_STEPS, w.shape[1]), lambda i, pr: (pr[0], i, 0)) for w in ws],
        [jax.ShapeDtypeStruct((N_CHIPS,) + w.shape, BF16) for w in ws], (), ("parallel",), rider, prefetch=(p_idx,))


def chip_sum(name, grad, theirs, c_idx):
    ns, r, cols = grad.shape
    rh = r // 2
    tr = _row_tile(rh, cols, 2, 2 << 20)
    nb = rh // tr

    def body(c_ref, a_ref, b_ref, o_ref):
        o_ref[...] = (a_ref[...].astype(F32) + b_ref[...].astype(F32)).astype(o_ref.dtype)

    return pl.pallas_call(
        body, name=name,
        grid_spec=pltpu.PrefetchScalarGridSpec(
            num_scalar_prefetch=1, grid=(ns, nb),
            in_specs=[pl.BlockSpec((None, tr, cols), lambda q, i, cr: (q, cr[0] * nb + i, 0)),
                      pl.BlockSpec((None, tr, cols), lambda q, i, cr: (q, i, 0))],
            out_specs=pl.BlockSpec((None, tr, cols), lambda q, i, cr: (q, i, 0))),
        out_shape=jax.ShapeDtypeStruct((ns, rh, cols), BF16),
        compiler_params=pltpu.CompilerParams(dimension_semantics=("parallel", "parallel"), vmem_limit_bytes=VMEM_LIMIT),
    )(c_idx, grad, theirs)


def owner_sum(name, sums, got, pc_idx):
    ns, rh, cols = sums.shape
    tr = _row_tile(rh, cols, 4, 2 << 20)
    nb = rh // tr

    def body(pc_ref, a_ref, b_ref, o_ref):
        o_ref[...] = ((a_ref[...].astype(F32) + b_ref[0].astype(F32)) + b_ref[1].astype(F32)) + b_ref[2].astype(F32)

    return pl.pallas_call(
        body, name=name,
        grid_spec=pltpu.PrefetchScalarGridSpec(
            num_scalar_prefetch=1, grid=(nb,),
            in_specs=[pl.BlockSpec((None, tr, cols), lambda i, pc: (pc[0], i, 0)),
                      pl.BlockSpec((3, tr, cols), lambda i, pc: (0, i, 0))],
            out_specs=pl.BlockSpec((tr, cols), lambda i, pc: (pc[1] * nb + i, 0))),
        out_shape=jax.ShapeDtypeStruct((2 * rh, cols), F32),
        compiler_params=pltpu.CompilerParams(dimension_semantics=("parallel",), vmem_limit_bytes=VMEM_LIMIT),
    )(pc_idx, sums, got)


def adamw(name, w, g, m, v):
    r, cols = w.shape
    tr = _row_tile(r, cols, 4)
    c1 = 1.0 / (1.0 - ADAM_B1 ** ADAM_STEP)
    c2 = 1.0 / (1.0 - ADAM_B2 ** ADAM_STEP)

    def body(w_ref, g_ref, m_ref, v_ref, go_ref, d_ref, nm_ref, nv_ref):
        gv = g_ref[...]
        nm = ADAM_B1 * m_ref[...] + (1.0 - ADAM_B1) * gv
        nv = ADAM_B2 * v_ref[...] + (1.0 - ADAM_B2) * (gv * gv)
        go_ref[...] = gv
        d_ref[...] = -ADAM_LR * ((nm * c1) / (jnp.sqrt(nv * c2) + ADAM_EPS) + ADAM_WD * w_ref[...])
        nm_ref[...] = nm
        nv_ref[...] = nv

    blk = pl.BlockSpec((tr, cols), lambda i: (i, 0))
    return _call(name, body, (r // tr,), [blk] * 4, (w, g, m, v), [blk] * 4, [jax.ShapeDtypeStruct((r, cols), F32)] * 4, (), ("parallel",))


def _rope(x, cos, sin):
    x3 = x.reshape(x.shape[0], -1, HEAD_DIM)
    x1, x2 = x3[..., : HEAD_DIM // 2], x3[..., HEAD_DIM // 2:]
    cos, sin = cos[:, None, :], sin[:, None, :]
    return jnp.concatenate([x1 * cos - x2 * sin, x2 * cos + x1 * sin], axis=-1).reshape(x.shape)


def _win_layout(d_model):
    hf = hq = d_model // (2 * HEAD_DIM)
    hk = hq // 4
    sizes = [hf * HEAD_DIM, hf * HEAD_DIM, hf * HEAD_DIM, hf, hq * HEAD_DIM, hk * HEAD_DIM, hk * HEAD_DIM]
    return hf, hq, hk, sizes


class WinPlan:
    def __init__(self, d_model, ns=N_CHIPS):
        self.hf, self.hq, self.hk, self.sizes = _win_layout(d_model)
        self.ns, self.cs = ns, sum(self.sizes) // ns
        self.jump_at = sum(self.sizes[:4])
        self.jump_by = -self.jump_at % _LANES
        self.base = [self.pos(s * self.cs) // _LANES * _LANES for s in range(ns)]
        ends = [self.pos((s + 1) * self.cs - 1) + 1 - self.base[s] for s in range(ns)]
        self.width = -(-max(ends) // _LANES) * _LANES
        self.total = -(-max(b + self.width for b in self.base) // 1024) * 1024
        starts = [0]
        for sz in self.sizes:
            starts.append(starts[-1] + sz)
        self.segments = [(self.pos(a), sz) for a, sz in zip(starts, self.sizes)]

    def pos(self, g):
        return g if g < self.jump_at else g + self.jump_by

    def pieces(self, s):
        g0, g1 = s * self.cs, (s + 1) * self.cs
        cuts = [g0] + ([self.jump_at] if g0 < self.jump_at < g1 else []) + [g1]
        return [(a - g0, b - a, self.pos(a) - self.base[s]) for a, b in zip(cuts[:-1], cuts[1:])]

    def place(self, w, s):
        parts, at = [], 0
        for t0, n, j0 in self.pieces(s):
            parts += [jnp.zeros((w.shape[0], j0 - at), w.dtype), w[:, t0:t0 + n]]
            at = j0 + n
        return jnp.concatenate(parts + [jnp.zeros((w.shape[0], self.width - at), w.dtype)], axis=1)

    def unplace(self, slab, s):
        return jnp.concatenate([slab[:, j0:j0 + n] for _, n, j0 in self.pieces(s)], axis=1)

    def assemble(self, slabs):
        full = jnp.zeros((slabs.shape[1], self.total), slabs.dtype)
        for s in range(self.ns):
            full = full.at[:, self.base[s]:self.base[s] + self.width].add(slabs[s])
        return full

    def split(self, full):
        return jnp.stack([full[:, b:b + self.width] for b in self.base])


def _attn_inputs(proj, sm, positions):
    s_len = proj.shape[0]
    plan = WinPlan(sm["norm_mix_g"].shape[0])
    hf, hq, hk = plan.hf, plan.hq, plan.hk
    grp = hq // hk
    q_f, k_f, v_f, f_logit, q_s, k_s, v_s = [proj[:, a:a + n] for a, n in plan.segments]

    q_f = head_rms("fox_qnorm", q_f, sm["fox_q_norm_g"])
    k_f = head_rms("fox_knorm", k_f, sm["fox_k_norm_g"])
    log_f = jax.nn.log_sigmoid(f_logit + sm["b_forget"])
    c = jnp.cumsum(log_f, axis=0).T

    inv_freq = ROPE_THETA ** (-jnp.arange(0, HEAD_DIM, 2, dtype=F32) / HEAD_DIM)
    ang = positions.astype(F32)[:, None] * inv_freq
    cos, sin = jnp.cos(ang), jnp.sin(ang)
    q_s = _rope(head_rms("swa_qnorm", q_s, sm["swa_q_norm_g"]), cos, sin)
    k_s = _rope(head_rms("swa_knorm", k_s, sm["swa_k_norm_g"]), cos, sin)
    sink = jnp.broadcast_to(sm["swa_sinks"].reshape(hk, grp, 1, 1), (hk, grp, WINDOW, 1)).reshape(hk, grp * WINDOW, 1)
    tq = fox_tile(s_len)
    return (q_f, k_f, v_f, c[:, :, None], c.reshape(hf, s_len // tq, 1, tq)), (q_s, k_s, v_s, sink)


_BIG = ("ffn1_w_gate", "ffn1_w_up", "ffn1_w_down", "w_in", "w_out", "ffn2_w_gate", "ffn2_w_up", "ffn2_w_down")
_ROW_SHARDED = ("ffn1_w_down", "w_out", "ffn2_w_down")
_SMALL = ("norm_ffn1_g", "norm_mix_g", "b_forget", "fox_q_norm_g", "fox_k_norm_g", "swa_q_norm_g", "swa_k_norm_g", "swa_sinks",
          "out_norm_fox_g", "out_norm_swa_g", "norm_ffn2_g")
_ATTN_SMALL = ("norm_mix_g", "b_forget", "fox_q_norm_g", "fox_k_norm_g", "swa_q_norm_g", "swa_k_norm_g", "swa_sinks")
_ALL = ("norm_ffn1_g", "ffn1_w_gate", "ffn1_w_up", "ffn1_w_down", "norm_mix_g", "w_in", "b_forget", "fox_q_norm_g", "fox_k_norm_g",
        "swa_q_norm_g", "swa_k_norm_g", "swa_sinks", "out_norm_fox_g", "out_norm_swa_g", "w_out", "norm_ffn2_g", "ffn2_w_gate",
        "ffn2_w_up", "ffn2_w_down")


def _pack_small(d):
    parts = []
    for k in _SMALL:
        v = d[k].reshape(-1)
        rows = -(-v.shape[0] // _LANES)
        parts.append(jnp.pad(v, (0, rows * _LANES - v.shape[0])).reshape(rows, _LANES))
    a = jnp.concatenate(parts, axis=0)
    return jnp.pad(a, ((0, -a.shape[0] % 8), (0, 0)))


def _unpack_small(a, like):
    out, r0 = {}, 0
    for k in _SMALL:
        nvals = like[k].shape[1]
        rows = -(-nvals // _LANES)
        out[k] = a[r0:r0 + rows].reshape(-1)[:nvals].reshape(1, nvals)
        r0 += rows
    return out


def _stacked(w):
    return w.reshape(-1, w.shape[-1])


def _local_step(shards, sm, x, positions, target, p_idx, c_idx, pc_idx):
    ns = N_CHIPS
    hf, hq, hk, _ = _win_layout(x.shape[1])
    s_len = x.shape[0]
    full = {}

    def fetch(*jobs):
        names = list(dict.fromkeys(n for n, _, _ in jobs))
        return names, gather([bufs[n] for n in names], [(names.index(n), kind, part) for n, kind, part in jobs])

    def take(names, rid):
        for n, b in zip(names, rid[0]):
            bufs[n] = b

    n1 = ["ffn1_w_gate", "ffn1_w_up", "ffn1_w_down"]
    n2 = ["ffn2_w_gate", "ffn2_w_up", "ffn2_w_down"]
    later = ["w_in", "w_out"] + n2
    placed, _ = cast_place("cast_place_ffn1", [shards[n] for n in n1], p_idx)
    bufs = dict(zip(n1, placed))
    gate1, up1, down1 = n1
    gate2, up2, down2 = n2
    names, rider = fetch((gate1, "ici", WHOLE), (up1, "ici", WHOLE))
    placed, rid = cast_place("cast_place_later", [shards[n] for n in later], p_idx, rider=rider)
    bufs.update(zip(later, placed))
    take(names, rid)
    names, rider = fetch((gate1, "d2d", WHOLE), (up1, "d2d", WHOLE))
    xn1, r1, rid = rms_fwd("ffn1_norm", x, sm["norm_ffn1_g"], BF16, rider=rider)
    take(names, rid)
    names, rider = fetch((down1, "ici", WHOLE))
    (g1, u1, hid1), rid = ffn_gu("ffn1_gu", xn1, bufs[gate1], bufs[up1], rider=rider)
    take(names, rid)
    names, rider = fetch((down1, "d2d", WHOLE))
    take(names, run_step("gather_d2d_ffn1_down", rider))
    wd1 = _stacked(bufs[down1])
    names, rider = fetch(("w_in", "ici", WHOLE))
    h1, rid = mm_nn("ffn1_down", hid1, wd1, scale=0.5, resid=x, rider=rider)
    take(names, rid)

    names, rider = fetch(("w_in", "d2d", WHOLE))
    u, r_mix, rid = rms_fwd("mix_norm", h1, sm["norm_mix_g"], BF16, rider=rider)
    take(names, rid)
    names, rider = fetch(("w_out", "ici", WHOLE), (gate2, "ici", (0, 1, 4)))
    plan = WinPlan(x.shape[1])
    win = plan.assemble(bufs["w_in"])
    proj, rid = mm_nn("mix_inproj", u, win, rider=rider)
    take(names, rid)
    sm_attn = {k: sm[k] for k in _ATTN_SMALL}
    (fox_in, swa_in), attn_vjp = jax.vjp(lambda pr, s: _attn_inputs(pr, s, positions), proj, sm_attn)
    names, rider = fetch((gate2, "ici", (1, 4, 4)), ("w_out", "d2d", WHOLE), (gate2, "d2d", (0, 1, 4)))
    (o_f, lse), rid = fox_fwd(*fox_in, rider=rider)
    take(names, rid)
    names, rider = fetch((up2, "ici", (0, 3, 4)), (gate2, "d2d", (1, 4, 4)))
    o_s, rid = swa_fwd(*swa_in, rider=rider)
    take(names, rid)
    o_fox, o_swa = o_f, o_s
    nf, r_fox = rms_fwd("out_norm_fox", o_fox, sm["out_norm_fox_g"], BF16)
    nsw, r_swa = rms_fwd("out_norm_swa", o_swa, sm["out_norm_swa_g"], BF16)
    o = jnp.concatenate([nf, nsw], axis=-1)
    wout = _stacked(bufs["w_out"])
    names, rider = fetch((up2, "ici", (3, 4, 4)), (up2, "d2d", (0, 3, 4)))
    h2, rid = mm_nn("out_proj", o, wout, resid=h1, rider=rider)
    take(names, rid)

    names, rider = fetch((up2, "d2d", (3, 4, 4)))
    xn2, r2, rid = rms_fwd("ffn2_norm", h2, sm["norm_ffn2_g"], BF16, rider=rider)
    take(names, rid)
    names, rider = fetch((down2, "ici", WHOLE))
    (g2, u2, hid2), rid = ffn_gu("ffn2_gu", xn2, bufs[gate2], bufs[up2], rider=rider)
    take(names, rid)
    names, rider = fetch((down2, "d2d", WHOLE))
    take(names, run_step("gather_d2d_ffn2_down", rider))
    wd2 = _stacked(bufs["ffn2_w_down"])
    y, _ = mm_nn("ffn2_down", hid2, wd2, scale=0.5, resid=h2)
    loss, dy = loss_call(y, target)

    red = {}

    def grad(n, g):
        red[n] = {"grad": g.reshape(ns, -1, g.shape[-1])}

    def ride(*steps):
        def done(rid):
            a0 = n0 = 0
            for rd, cb in steps:
                cb(rid[0][a0:a0 + len(rd.aliased)], rid[1][n0:n0 + len(rd.news)])
                a0, n0 = a0 + len(rd.aliased), n0 + len(rd.news)

        return (combine(*[s[0] for s in steps]) if len(steps) > 1 else steps[0][0]), done

    def xchg(*names):
        def cb(al, news):
            for n, t in zip(names, news):
                red[n]["sum"] = chip_sum("chip_sum_" + n, red[n]["grad"], t, c_idx)

        return exchange_halves([red[n]["grad"] for n in names]), cb

    def scat(n, part=WHOLE):
        def cb(al, news):
            red[n]["got"] = (al or news)[0]

        return scatter_to_owner([red[n]["sum"]], [red[n]["got"]] if "got" in red[n] else None, part), cb

    def own(n):
        red[n]["half"] = owner_sum("owner_sum_" + n, red[n]["sum"], red[n]["got"], pc_idx)

    def join(*names):
        return join_halves([red[n]["half"] for n in names]), lambda al, news: full.update(zip(names, al))

    dwd2, _ = mm_tn("ffn2_dwd", hid2, dy, out_dtype=BF16, scale=0.5)
    grad(down2, dwd2)
    rider, done = ride(xchg(down2))
    (dg2, du2), rid = ffn_dh("ffn2_dh", dy, wd2, g2, u2, ns, 0.5, rider=rider)
    done(rid)
    rider, done = ride(scat(down2, (0, 1, 2)))
    dwg2, rid = mm_tn_sharded("ffn2_dwg", xn2, dg2, ns, rider=rider)
    done(rid)
    grad(gate2, dwg2)
    rider, done = ride(scat(down2, (1, 2, 2)), xchg(gate2))
    dwu2, rid = mm_tn_sharded("ffn2_dwu", xn2, du2, ns, rider=rider)
    done(rid)
    grad(up2, dwu2)
    rider, done = ride(scat(gate2, (0, 1, 2)), xchg(up2))
    dxn, rid = mm_nt_sharded("ffn2_dxn_g", dg2, bufs[gate2], rider=rider)
    done(rid)
    rider, done = ride(scat(gate2, (1, 2, 2)))
    dxn, rid = mm_nt_sharded("ffn2_dxn_u", du2, bufs[up2], resid=dxn, rider=rider)
    done(rid)
    dh2, dgain_ffn2 = rms_bwd("ffn2_dnorm", h2, sm["norm_ffn2_g"], r2, dxn, dres=dy)
    own(down2)
    own(gate2)

    do, _ = mm_nt("out_do", dh2, wout)
    dwout, _ = mm_tn("out_dw", o, dh2, out_dtype=BF16)
    cf = o_fox.shape[1]
    d_fox, dgain_fox = rms_bwd("out_dnorm_fox", o_fox, sm["out_norm_fox_g"], r_fox, do[:, :cf])
    d_swa, dgain_swa = rms_bwd("out_dnorm_swa", o_swa, sm["out_norm_swa_g"], r_swa, do[:, cf:])
    grad("w_out", dwout)
    rider, done = ride(scat(up2))
    swa_cts, rid = swa_bwd(*swa_in, o_s, d_swa, rider=rider)
    done(rid)
    own(up2)
    rider, done = ride(xchg("w_out"), join(down2, gate2, up2))
    fox_cts, rid = fox_bwd(*fox_in, o_f, lse, d_fox, rider=rider)
    done(rid)
    dproj, dsm_attn = attn_vjp((tuple(fox_cts), tuple(swa_cts)))

    rider, done = ride(scat("w_out"))
    du, rid = mm_nt("mix_du", dproj, win, rider=rider)
    done(rid)
    dwin, _ = mm_tn("mix_dwin", u, dproj, out_dtype=BF16)
    grad("w_in", plan.split(dwin))
    dh1, dgain_mix = rms_bwd("mix_dnorm", h1, sm["norm_mix_g"], r_mix, du, dres=dh2)
    own("w_out")

    rider, done = ride(xchg("w_in"))
    dwd1, rid = mm_tn("ffn1_dwd", hid1, dh1, out_dtype=BF16, scale=0.5, rider=rider)
    done(rid)
    grad(down1, dwd1)
    rider, done = ride(scat("w_in"), xchg(down1))
    (dg1, du1), rid = ffn_dh("ffn1_dh", dh1, wd1, g1, u1, ns, 0.5, rider=rider)
    done(rid)
    own("w_in")
    rider, done = ride(scat(down1, (0, 1, 2)), join("w_out", "w_in"))
    dwg1, rid = mm_tn_sharded("ffn1_dwg", xn1, dg1, ns, rider=rider)
    done(rid)
    grad(gate1, dwg1)
    rider, done = ride(scat(down1, (1, 2, 2)), xchg(gate1))
    dwu1, rid = mm_tn_sharded("ffn1_dwu", xn1, du1, ns, rider=rider)
    done(rid)
    grad(up1, dwu1)
    own(down1)
    rider, done = ride(scat(gate1, (0, 1, 2)), xchg(up1), join(down1))
    dxn, rid = mm_nt_sharded("ffn1_dxn_g", dg1, bufs[gate1], rider=rider)
    done(rid)
    rider, done = ride(scat(gate1, (1, 2, 2)), scat(up1, (0, 1, 4)))
    dxn, rid = mm_nt_sharded("ffn1_dxn_u", du1, bufs[up1], resid=dxn, rider=rider)
    done(rid)
    dx, dgain_ffn1 = rms_bwd("ffn1_dnorm", x, sm["norm_ffn1_g"], r1, dxn, dres=dh1)
    own(gate1)

    rider, done = ride(scat(up1, (1, 4, 4)), join(gate1))
    done(run_step("reduce_tail", rider))
    own(up1)
    rider, done = ride(join(up1))
    done(run_step("join_tail", rider))

    g_small = dict(dsm_attn)
    g_small["norm_mix_g"] = g_small["norm_mix_g"] + dgain_mix
    g_small.update(norm_ffn1_g=dgain_ffn1, norm_ffn2_g=dgain_ffn2, out_norm_fox_g=dgain_fox, out_norm_swa_g=dgain_swa)
    return loss, dx, full, g_small


def kernel(x, positions, norm_ffn1_g, ffn1_w_gate, ffn1_w_up, ffn1_w_down, norm_mix_g, w_in, b_forget, fox_q_norm_g, fox_k_norm_g, swa_q_norm_g, swa_k_norm_g, swa_sinks, out_norm_fox_g, out_norm_swa_g, w_out, norm_ffn2_g, ffn2_w_gate, ffn2_w_up, ffn2_w_down, loss_target, m_norm_ffn1_g, m_ffn1_w_gate, m_ffn1_w_up, m_ffn1_w_down, m_norm_mix_g, m_w_in, m_b_forget, m_fox_q_norm_g, m_fox_k_norm_g, m_swa_q_norm_g, m_swa_k_norm_g, m_swa_sinks, m_out_norm_fox_g, m_out_norm_swa_g, m_w_out, m_norm_ffn2_g, m_ffn2_w_gate, m_ffn2_w_up, m_ffn2_w_down, v_norm_ffn1_g, v_ffn1_w_gate, v_ffn1_w_up, v_ffn1_w_down, v_norm_mix_g, v_w_in, v_b_forget, v_fox_q_norm_g, v_fox_k_norm_g, v_swa_q_norm_g, v_swa_k_norm_g, v_swa_sinks, v_out_norm_fox_g, v_out_norm_swa_g, v_w_out, v_norm_ffn2_g, v_ffn2_w_gate, v_ffn2_w_up, v_ffn2_w_down):
    args = dict(locals())
    w = {k: args[k] for k in _ALL}
    m = {k: args["m_" + k] for k in _ALL}
    v = {k: args["v_" + k] for k in _ALL}
    c_idx = lax.axis_index("c").astype(jnp.int32).reshape(1)
    p_idx = (2 * lax.axis_index("x") + lax.axis_index("y")).astype(jnp.int32).reshape(1)
    pc_idx = jnp.concatenate([p_idx, c_idx])

    small = {k: w[k] for k in _SMALL}
    shards = {k: w[k][0] for k in _BIG}
    plan = WinPlan(x.shape[-1])
    shards["w_in"] = lax.switch(p_idx[0], [functools.partial(plan.place, s=s) for s in range(N_CHIPS)], shards["w_in"])
    loss, grad_x, g_shard, g_small = _local_step(shards, {k: w[k][0] for k in _SMALL}, x[0], positions[0], loss_target[0],
                                                 p_idx, c_idx, pc_idx)
    g_shard["w_in"] = lax.switch(p_idx[0], [functools.partial(plan.unplace, s=s) for s in range(N_CHIPS)], g_shard["w_in"])
    loss = lax.psum(loss, ("x", "y", "c"))
    g_small_sum = _unpack_small(all_reduce_small(_pack_small({k: g_small[k].reshape(1, -1) for k in _SMALL})), small)

    grad_w, delta, new_m, new_v = {}, {}, {}, {}
    for k in _BIG:
        (g, d, nm, nv), _ = adamw("adamw_" + k, w[k][0], g_shard[k], m[k][0], v[k][0])
        grad_w[k], delta[k], new_m[k], new_v[k] = g[None], d[None], nm[None], nv[None]
    (_, d, nm, nv), _ = adamw("adamw_small", _pack_small(small), _pack_small(g_small_sum), _pack_small({k: m[k] for k in _SMALL}),
                              _pack_small({k: v[k] for k in _SMALL}))
    grad_w.update(g_small_sum)
    delta.update(_unpack_small(d, small))
    new_m.update(_unpack_small(nm, small))
    new_v.update(_unpack_small(nv, small))

    return (loss, grad_x[None], *[grad_w[k] for k in _ALL], *[delta[k] for k in _ALL], *[new_m[k] for k in _ALL], *[new_v[k] for k in _ALL])
```

```python
import functools

import jax
import jax.numpy as jnp
from jax import lax
from jax.experimental import pallas as pl
from jax.experimental.pallas import tpu as pltpu

F32 = jnp.float32
BF16 = jnp.bfloat16

HEAD_DIM = 64
WINDOW = 128
ROPE_THETA = 10000.0
EPS = 1e-6
N_CHIPS = 4
N_DEV = 8

ADAM_LR = 0.001
ADAM_B1 = 0.9
ADAM_B2 = 0.999
ADAM_EPS = 1e-08
ADAM_WD = 0.01
ADAM_STEP = 10

V7X_VMEM_BYTES = 64 * 1024 * 1024
VMEM_LIMIT = V7X_VMEM_BYTES - 8 * 1024 * 1024
MASK_VALUE = -1e30

_MESH = pl.DeviceIdType.MESH
_HBM = pl.BlockSpec(memory_space=pl.ANY)
_DIMS = {"nn": (((1,), (0,)), ((), ())), "nt": (((1,), (1,)), ((), ())), "tn": (((0,), (0,)), ((), ()))}


def _pick(n, prefs):
    for p in prefs:
        if n % p == 0:
            return p
    return n


class Rider:
    def __init__(self, reads, aliased, news, nsem, build):
        self.reads, self.aliased, self.news, self.nsem, self.build = list(reads), list(aliased), list(news), nsem, build


class _Shifted:
    def __init__(self, ref, off):
        self.ref, self.off = ref, off

    @property
    def at(self):
        return self

    def __getitem__(self, k):
        return self.ref.at[k + self.off]


def combine(*riders):
    def build(reads, al, news, ssem, rsem):
        out = ([], [], [])
        r0 = a0 = n0 = s0 = 0
        for rd in riders:
            nr, na, nn = len(rd.reads), len(rd.aliased), len(rd.news)
            part = rd.build(reads[r0:r0 + nr], al[a0:a0 + na], news[n0:n0 + nn], _Shifted(ssem, s0), _Shifted(rsem, s0))
            for acc, lst in zip(out, part):
                acc.extend(lst)
            r0, a0, n0, s0 = r0 + nr, a0 + na, n0 + nn, s0 + rd.nsem
        return out

    return Rider(sum((r.reads for r in riders), []), sum((r.aliased for r in riders), []), sum((r.news for r in riders), []),
                 sum(r.nsem for r in riders), build)


def _me():
    return lax.axis_index("x"), lax.axis_index("y"), lax.axis_index("c")


def _other_chips(x, y):
    return [(1 - x, y), (x, 1 - y), (1 - x, 1 - y)]


WHOLE = (0, 1, 1)


def _rows(ref, start, rows, part=WHOLE):
    k0, k1, n = part
    assert rows % n == 0, (rows, part)
    idx = (slice(None),) * (len(ref.shape) - 2) + (pl.ds(start + k0 * (rows // n), (k1 - k0) * (rows // n)), slice(None))
    return ref.at[idx]


def _half(ref, h, part=WHOLE):
    rows = ref.shape[-2] // 2
    return _rows(ref, h * rows, rows, part)


def _remote(src, dst, ssem, rsem, k, to):
    return pltpu.make_async_remote_copy(src_ref=src, dst_ref=dst, send_sem=ssem.at[k], recv_sem=rsem.at[k], device_id=to,
                                        device_id_type=_MESH)


def _later(*args):
    return functools.partial(_remote, *args)


def gather(bufs, jobs):
    def build(reads, al, news, ssem, rsem):
        x, y, c = _me()
        p = 2 * x + y
        starts, arrivals = [], []
        for n, (b, kind, part) in enumerate(jobs):
            for j, chip in enumerate(_other_chips(x, y)):
                q = 2 * chip[0] + chip[1]
                if kind == "ici":
                    src, landing, to = _half(al[b].at[p], c, part), _half(al[b].at[q], c, part), (*chip, c)
                else:
                    src, landing, to = _half(al[b].at[q], c, part), _half(al[b].at[q], 1 - c, part), (x, y, 1 - c)
                starts.append(_later(src, src, ssem, rsem, 3 * n + j, to))
                arrivals.append(_later(landing, landing, ssem, rsem, 3 * n + j, to))
        return starts, arrivals, starts

    return Rider([], bufs, [], 3 * len(jobs), build)


def exchange_halves(grads):
    def build(reads, al, news, ssem, rsem):
        x, y, c = _me()
        cps = [_later(_half(g, 1 - c), t, ssem, rsem, w, (x, y, 1 - c)) for w, (g, t) in enumerate(zip(reads, news))]
        return cps, cps, cps

    return Rider(grads, [], [jax.ShapeDtypeStruct((g.shape[0], g.shape[1] // 2, g.shape[2]), g.dtype) for g in grads], len(grads), build)


def scatter_to_owner(sums, gots=None, part=WHOLE):
    def build(reads, al, news, ssem, rsem):
        x, y, c = _me()
        cps = []
        for w, (s, got) in enumerate(zip(reads, al or news)):
            rows = s.shape[-2]
            for j, chip in enumerate(_other_chips(x, y)):
                cps.append(_later(_rows(s.at[2 * chip[0] + chip[1]], 0, rows, part), _rows(got.at[j], 0, rows, part), ssem, rsem,
                                  3 * w + j, (*chip, c)))
        return cps, cps, cps

    news = [] if gots else [jax.ShapeDtypeStruct((3,) + s.shape[1:], s.dtype) for s in sums]
    return Rider(sums, gots or [], news, 3 * len(sums), build)


def join_halves(fulls):
    def build(reads, al, news, ssem, rsem):
        x, y, c = _me()
        starts, arrivals = [], []
        for w, f in enumerate(al):
            mine, landing = _half(f, c), _half(f, 1 - c)
            starts.append(_later(mine, mine, ssem, rsem, w, (x, y, 1 - c)))
            arrivals.append(_later(landing, landing, ssem, rsem, w, (x, y, 1 - c)))
        return starts, arrivals, starts

    return Rider([], fulls, [], len(fulls), build)


def _start_and_wait(rider, reads, al, news, ssem, rsem, first, last):
    @pl.when(first)
    def _():
        for cp in rider.build(reads, al, news, ssem, rsem)[0]:
            cp().start()

    def finish():
        @pl.when(last)
        def _():
            _, arrivals, sends = rider.build(reads, al, news, ssem, rsem)
            for cp in arrivals:
                cp().wait_recv()
            for cp in sends:
                cp().wait_send()

    return finish


def _call(name, body, grid, in_specs, args, out_specs, out_shape, scratch=(), semantics=None, rider=None, prefetch=()):
    n_pre, n_in, n_out, n_scr = len(prefetch), len(args), len(out_shape), len(scratch)
    nr, na, nn = (len(rider.reads), len(rider.aliased), len(rider.news)) if rider else (0, 0, 0)

    def wrapped(*refs):
        pre, refs = refs[:n_pre], refs[n_pre:]
        ins, reads = refs[:n_in], refs[n_in:n_in + nr]
        o0 = n_in + nr + na
        outs, al, news = refs[o0:o0 + n_out], refs[o0 + n_out:o0 + n_out + na], refs[o0 + n_out + na:o0 + n_out + na + nn]
        s0 = o0 + n_out + na + nn
        scr, (ssem, rsem) = refs[s0:s0 + n_scr], refs[s0 + n_scr:]
        first = functools.reduce(jnp.logical_and, [pl.program_id(a) == 0 for a in range(len(grid))])
        last = functools.reduce(jnp.logical_and, [pl.program_id(a) == g - 1 for a, g in enumerate(grid)])
        finish = _start_and_wait(rider, reads, al, news, ssem, rsem, first, last)
        body(*pre, *ins, *outs, *scr)
        finish()

    kernel_fn, all_in, all_out, shapes, scr = body, list(in_specs), list(out_specs), list(out_shape), list(scratch)
    operands, aliases = (*prefetch, *args), {}
    if rider:
        kernel_fn, semantics = wrapped, ("arbitrary",) * len(grid)
        all_in += [_HBM] * (nr + na)
        all_out += [_HBM] * (na + nn)
        shapes += [jax.ShapeDtypeStruct(a.shape, a.dtype) for a in rider.aliased] + rider.news
        scr += [pltpu.SemaphoreType.DMA((rider.nsem,)), pltpu.SemaphoreType.DMA((rider.nsem,))]
        operands += (*rider.reads, *rider.aliased)
        aliases = {n_pre + n_in + nr + i: n_out + i for i in range(na)}
    params = pltpu.CompilerParams(dimension_semantics=semantics, vmem_limit_bytes=VMEM_LIMIT)
    if n_pre:
        spec = pltpu.PrefetchScalarGridSpec(num_scalar_prefetch=n_pre, grid=grid, in_specs=all_in, out_specs=all_out, scratch_shapes=scr)
        outs = pl.pallas_call(kernel_fn, name=name, grid_spec=spec, out_shape=shapes, input_output_aliases=aliases, compiler_params=params)(*operands)
    else:
        outs = pl.pallas_call(kernel_fn, name=name, grid=grid, in_specs=all_in, out_specs=all_out, out_shape=shapes, scratch_shapes=scr,
                              input_output_aliases=aliases, compiler_params=params)(*operands)
    return list(outs[:n_out]), ((list(outs[n_out:n_out + na]), list(outs[n_out + na:])) if rider else None)


def run_step(name, rider):
    nr, na, nn = len(rider.reads), len(rider.aliased), len(rider.news)

    def body(*refs):
        reads = refs[:nr]
        al, news = refs[nr + na:nr + 2 * na], refs[nr + 2 * na:nr + 2 * na + nn]
        ssem, rsem = refs[nr + 2 * na + nn:]
        starts, arrivals, sends = rider.build(reads, al, news, ssem, rsem)
        for cp in starts:
            cp().start()
        for cp in arrivals:
            cp().wait_recv()
        for cp in sends:
            cp().wait_send()

    outs = pl.pallas_call(
        body, name=name, in_specs=[_HBM] * (nr + na), out_specs=[_HBM] * (na + nn),
        out_shape=[jax.ShapeDtypeStruct(a.shape, a.dtype) for a in rider.aliased] + rider.news,
        input_output_aliases={nr + i: i for i in range(na)},
        scratch_shapes=[pltpu.SemaphoreType.DMA((rider.nsem,)), pltpu.SemaphoreType.DMA((rider.nsem,))],
    )(*rider.reads, *rider.aliased)
    return list(outs[:na]), list(outs[na:])


def all_reduce_small(v):
    rows, lanes = v.shape

    def body(v_ref, o_ref, slots, send_sems, recv_sems):
        x, y, c = _me()
        me = 4 * x + 2 * y + c
        slots[me] = v_ref[...]
        cps = []
        for k in range(1, N_DEV):
            peer = (x ^ (k >> 2), y ^ ((k >> 1) & 1), c ^ (k & 1))
            cps.append(_remote(v_ref, slots.at[me], send_sems, recv_sems, k - 1, peer))
            cps[-1].start()
        for k in range(1, N_DEV):
            theirs = slots.at[me ^ k]
            _remote(theirs, theirs, send_sems, recv_sems, k - 1, (x, y, c)).wait_recv()
        for cp in cps:
            cp.wait_send()
        acc = slots[0]
        for i in range(1, N_DEV):
            acc = acc + slots[i]
        o_ref[...] = acc

    return pl.pallas_call(
        body, name="all_reduce_small",
        in_specs=[pl.BlockSpec(memory_space=pltpu.VMEM)], out_specs=pl.BlockSpec(memory_space=pltpu.VMEM),
        out_shape=jax.ShapeDtypeStruct((rows, lanes), F32),
        scratch_shapes=[pltpu.VMEM((N_DEV, rows, lanes), F32), pltpu.SemaphoreType.DMA((N_DEV - 1,)), pltpu.SemaphoreType.DMA((N_DEV - 1,))],
    )(v)


def _mm_call(name, mode, a, b, a_spec, b_spec, out_shape, out_spec, grid, acc_shape, scale=1.0, resid=None, resid_spec=None, rider=None):
    nk = grid[2]
    dims = _DIMS[mode]
    has_resid = resid is not None

    def body(*refs):
        a_ref, b_ref = refs[:2]
        r_ref = refs[2] if has_resid else None
        o_ref = refs[3] if has_resid else refs[2]

        def finish(r):
            if scale != 1.0:
                r = r * scale
            if has_resid:
                r = r_ref[...].astype(F32) + r
            o_ref[...] = r.astype(o_ref.dtype)

        part = lax.dot_general(a_ref[...].astype(BF16), b_ref[...].astype(BF16), dims, preferred_element_type=F32)
        if nk == 1:
            finish(part)
            return
        acc_ref = refs[-1]
        k = pl.program_id(2)

        @pl.when(k == 0)
        def _():
            acc_ref[...] = part

        @pl.when(k > 0)
        def _():
            acc_ref[...] += part

        @pl.when(k == nk - 1)
        def _():
            finish(acc_ref[...])

    in_specs = [a_spec, b_spec] + ([resid_spec] if has_resid else [])
    args = (a, b) + ((resid,) if has_resid else ())
    (out,), rid = _call(name, body, grid, in_specs, args, [out_spec], [out_shape], [pltpu.VMEM(acc_shape, F32)] if nk > 1 else [],
                        ("parallel", "parallel", "arbitrary"), rider)
    return out, rid


MM_VMEM_BUDGET = 40 * 1024 * 1024
_TILE_OPTS = (2048, 1408, 1024, 512, 256, 128)


def _tiles(m, n, kd, a_item, b_item, o_item, r_item=0, tm=None, tn=None, tk=None):
    def opts(full, fixed, cap):
        return [fixed] if fixed else [t for t in _TILE_OPTS if t <= cap and full % t == 0] or [full]

    best = None
    for cm in opts(m, tm, 1408):
        for cn in opts(n, tn, 1408):
            for ck in opts(kd, tk, 2048):
                blocks = cm * ck * a_item + ck * cn * b_item + cm * cn * (o_item + r_item)
                casts = (cm * ck * 2 if a_item == 4 else 0) + (ck * cn * 2 if b_item == 4 else 0)
                if 2 * blocks + cm * cn * 4 + casts <= MM_VMEM_BUDGET:
                    key = (cm * cn * ck, ck)
                    if best is None or key > best[0]:
                        best = (key, (cm, cn, ck))
    assert best is not None, (m, n, kd)
    return best[1]


def _item(x):
    return jnp.dtype(x.dtype).itemsize


def mm_nn(name, a, b, *, out_dtype=F32, scale=1.0, resid=None, rider=None):
    m, kd = a.shape
    n = b.shape[1]
    tm, tn, tk = _tiles(m, n, kd, _item(a), _item(b), jnp.dtype(out_dtype).itemsize, 0 if resid is None else _item(resid))
    o_spec = pl.BlockSpec((tm, tn), lambda i, j, k: (i, j))
    return _mm_call(
        name, "nn", a, b, pl.BlockSpec((tm, tk), lambda i, j, k: (i, k)), pl.BlockSpec((tk, tn), lambda i, j, k: (k, j)),
        jax.ShapeDtypeStruct((m, n), out_dtype), o_spec, (m // tm, n // tn, kd // tk), (tm, tn), scale, resid, o_spec, rider)


def mm_nt(name, a, b, *, out_dtype=F32, scale=1.0, resid=None, rider=None):
    m, kd = a.shape
    n = b.shape[0]
    tm, tn, tk = _tiles(m, n, kd, _item(a), _item(b), jnp.dtype(out_dtype).itemsize, 0 if resid is None else _item(resid))
    o_spec = pl.BlockSpec((tm, tn), lambda i, j, k: (i, j))
    return _mm_call(
        name, "nt", a, b, pl.BlockSpec((tm, tk), lambda i, j, k: (i, k)), pl.BlockSpec((tn, tk), lambda i, j, k: (j, k)),
        jax.ShapeDtypeStruct((m, n), out_dtype), o_spec, (m // tm, n // tn, kd // tk), (tm, tn), scale, resid, o_spec, rider)


def mm_tn(name, a, b, *, out_dtype=F32, scale=1.0, rider=None):
    kd, m = a.shape
    n = b.shape[1]
    tm, tn, tk = _tiles(m, n, kd, _item(a), _item(b), jnp.dtype(out_dtype).itemsize)
    return _mm_call(
        name, "tn", a, b, pl.BlockSpec((tk, tm), lambda i, j, k: (k, i)), pl.BlockSpec((tk, tn), lambda i, j, k: (k, j)),
        jax.ShapeDtypeStruct((m, n), out_dtype), pl.BlockSpec((tm, tn), lambda i, j, k: (i, j)),
        (m // tm, n // tn, kd // tk), (tm, tn), scale, rider=rider)


def mm_nt_sharded(name, a, w, *, resid=None, rider=None):
    m = a.shape[0]
    ns, n, c = w.shape
    tm, tn, _ = _tiles(m, n, c, _item(a), _item(w), 4, 0 if resid is None else _item(resid), tk=c)
    o_spec = pl.BlockSpec((tm, tn), lambda i, j, k: (i, j))
    return _mm_call(
        name, "nt", a, w, pl.BlockSpec((tm, c), lambda i, j, k: (i, k)), pl.BlockSpec((None, tn, c), lambda i, j, k: (k, j, 0)),
        jax.ShapeDtypeStruct((m, n), F32), o_spec, (m // tm, n // tn, ns), (tm, tn), 1.0, resid, o_spec, rider)


def mm_tn_sharded(name, a, b, ns, *, rider=None):
    kd, m = a.shape
    c = b.shape[1] // ns
    tm, _, tk = _tiles(m, c, kd, _item(a), _item(b), 2, tn=c)
    return _mm_call(
        name, "tn", a, b, pl.BlockSpec((tk, tm), lambda i, j, k: (k, i)), pl.BlockSpec((tk, c), lambda i, j, k: (k, j)),
        jax.ShapeDtypeStruct((ns, m, c), BF16), pl.BlockSpec((None, tm, c), lambda i, j, k: (j, i, 0)),
        (m // tm, ns, kd // tk), (tm, c), rider=rider)


def rms_fwd(name, x, g, out_dtype, rider=None):
    r, c = x.shape
    tm = _pick(r, (512, 256, 128, 64, 8))

    def body(x_ref, g_ref, y_ref, r_ref):
        xf = x_ref[...].astype(F32)
        rstd = lax.rsqrt(jnp.mean(xf * xf, axis=-1, keepdims=True) + EPS)
        y_ref[...] = ((xf * rstd) * g_ref[...]).astype(y_ref.dtype)
        r_ref[...] = rstd

    (y, rstd), rid = _call(
        name, body, (r // tm,), [pl.BlockSpec((tm, c), lambda i: (i, 0)), pl.BlockSpec((1, c), lambda i: (0, 0))], (x, g.reshape(1, c)),
        [pl.BlockSpec((tm, c), lambda i: (i, 0)), pl.BlockSpec((tm, 1), lambda i: (i, 0))],
        [jax.ShapeDtypeStruct((r, c), out_dtype), jax.ShapeDtypeStruct((r, 1), F32)], (), ("parallel",), rider)
    return (y, rstd) if rider is None else (y, rstd, rid)


def rms_bwd(name, x, g, rstd, dy, dres=None):
    r, c = x.shape
    tm = _pick(r, (512, 256, 128, 64, 8))
    has_res = dres is not None

    def body(*refs):
        if has_res:
            x_ref, g_ref, r_ref, dy_ref, dres_ref, dx_ref, dg_ref = refs
        else:
            x_ref, g_ref, r_ref, dy_ref, dx_ref, dg_ref = refs
        xhat = x_ref[...].astype(F32) * r_ref[...]
        dyf = dy_ref[...].astype(F32)
        gdy = dyf * g_ref[...]
        dx = r_ref[...] * (gdy - xhat * jnp.mean(gdy * xhat, axis=-1, keepdims=True))
        if has_res:
            dx = dx + dres_ref[...]
        dx_ref[...] = dx

        @pl.when(pl.program_id(0) == 0)
        def _():
            dg_ref[...] = jnp.zeros_like(dg_ref)

        dg_ref[...] += jnp.sum(dyf * xhat, axis=0, keepdims=True)

    row = pl.BlockSpec((tm, c), lambda i: (i, 0))
    in_specs = [row, pl.BlockSpec((1, c), lambda i: (0, 0)), pl.BlockSpec((tm, 1), lambda i: (i, 0)), row] + ([row] if has_res else [])
    args = (x, g.reshape(1, c), rstd, dy) + ((dres,) if has_res else ())
    (dx, dg), _ = _call(name, body, (r // tm,), in_specs, args, [row, pl.BlockSpec((1, c), lambda i: (0, 0))],
                        [jax.ShapeDtypeStruct((r, c), F32), jax.ShapeDtypeStruct((1, c), F32)], (), ("arbitrary",))
    return dx, dg.reshape(c)


_LANES = 128


def _head_mean(v):
    if v.shape[1] == HEAD_DIM:
        return jnp.mean(v, axis=-1, keepdims=True)
    low = lax.broadcasted_iota(jnp.int32, v.shape, 1) < HEAD_DIM
    lo = jnp.sum(jnp.where(low, v, 0.0), axis=-1, keepdims=True)
    hi = jnp.sum(jnp.where(low, 0.0, v), axis=-1, keepdims=True)
    return jnp.where(low, lo, hi) * (1.0 / HEAD_DIM)


def _head_groups(c):
    width = _LANES if c % _LANES == 0 else HEAD_DIM
    assert c % width == 0, c
    return width, [slice(k * width, (k + 1) * width) for k in range(c // width)]


def _head_gain(g, width):
    return jnp.tile(g.reshape(1, HEAD_DIM), (1, width // HEAD_DIM))


def head_rms_fwd(name, x, g):
    s, c = x.shape
    tm = _pick(s, (256, 128, 8))
    width, groups = _head_groups(c)

    def body(x_ref, g_ref, y_ref):
        for sl in groups:
            xs = x_ref[:, sl]
            y_ref[:, sl] = (xs * lax.rsqrt(_head_mean(xs * xs) + EPS)) * g_ref[...]

    row = pl.BlockSpec((tm, c), lambda i: (i, 0))
    (y,), _ = _call(name, body, (s // tm,), [row, pl.BlockSpec((1, width), lambda i: (0, 0))], (x, _head_gain(g, width)), [row],
                    [jax.ShapeDtypeStruct((s, c), F32)], (), ("parallel",))
    return y


def head_rms_bwd(name, x, g, dy):
    s, c = x.shape
    tm = _pick(s, (256, 128, 8))
    width, groups = _head_groups(c)

    def body(x_ref, g_ref, dy_ref, dx_ref, dg_ref):
        @pl.when(pl.program_id(0) == 0)
        def _():
            dg_ref[...] = jnp.zeros_like(dg_ref)

        for sl in groups:
            xs, dys = x_ref[:, sl], dy_ref[:, sl]
            rstd = lax.rsqrt(_head_mean(xs * xs) + EPS)
            xhat = xs * rstd
            gdy = dys * g_ref[...]
            dx_ref[:, sl] = rstd * (gdy - xhat * _head_mean(gdy * xhat))
            dg_ref[...] += jnp.sum(dys * xhat, axis=0, keepdims=True)

    row = pl.BlockSpec((tm, c), lambda i: (i, 0))
    vec = pl.BlockSpec((1, width), lambda i: (0, 0))
    (dx, dg), _ = _call(name, body, (s // tm,), [row, vec, row], (x, _head_gain(g, width), dy), [row, vec],
                        [jax.ShapeDtypeStruct((s, c), F32), jax.ShapeDtypeStruct((1, width), F32)], (), ("arbitrary",))
    return dx, jnp.sum(dg.reshape(width // HEAD_DIM, HEAD_DIM), axis=0)


@functools.partial(jax.custom_vjp, nondiff_argnums=(0,))
def head_rms(name, x, g):
    return head_rms_fwd(name + "_fwd", x, g)


def _head_rms_fwd(name, x, g):
    return head_rms_fwd(name + "_fwd", x, g), (x, g)


def _head_rms_bwd(name, res, dy):
    return head_rms_bwd(name + "_bwd", *res, dy)


head_rms.defvjp(_head_rms_fwd, _head_rms_bwd)


FFN_TM = 512


def _sigmoid(x):
    return 1.0 / (1.0 + jnp.exp(-x))


def ffn_gu(name, xn, wg, wu, rider=None):
    s, d = xn.shape
    ns, _, c = wg.shape
    tm = _pick(s, (FFN_TM, 128))

    def body(x_ref, wg_ref, wu_ref, h_ref, a_ref, b_ref):
        xb = x_ref[...]
        gv = jnp.dot(xb, wg_ref[...], preferred_element_type=F32)
        uv = jnp.dot(xb, wu_ref[...], preferred_element_type=F32)
        sig = _sigmoid(gv)
        silu = gv * sig
        h_ref[...] = (silu * uv).astype(BF16)
        a_ref[...] = (uv * (sig * (1.0 + gv * (1.0 - sig)))).astype(BF16)
        b_ref[...] = silu.astype(BF16)

    w_spec = pl.BlockSpec((None, d, c), lambda j, i: (j, 0, 0))
    o_spec = pl.BlockSpec((tm, c), lambda j, i: (i, j))
    return _call(
        name, body, (ns, s // tm), [pl.BlockSpec((tm, d), lambda j, i: (i, 0)), w_spec, w_spec], (xn, wg, wu),
        [o_spec, o_spec, o_spec], [jax.ShapeDtypeStruct((s, ns * c), BF16)] * 3, [], ("parallel", "parallel"), rider)


def ffn_dh(name, dy, wd, dh_dg, dh_du, ns, scale, rider=None):
    s, d = dy.shape
    f = wd.shape[0]
    c = f // ns
    tm = _pick(s, (FFN_TM, 128))

    def body(dy_ref, wd_ref, a_ref, b_ref, dg_ref, du_ref):
        dh = lax.dot_general(dy_ref[...].astype(BF16), wd_ref[...], _DIMS["nt"], preferred_element_type=F32) * scale
        dg_ref[...] = (dh * a_ref[...].astype(F32)).astype(BF16)
        du_ref[...] = (dh * b_ref[...].astype(F32)).astype(BF16)

    o_spec = pl.BlockSpec((tm, c), lambda j, i: (i, j))
    return _call(
        name, body, (ns, s // tm),
        [pl.BlockSpec((tm, d), lambda j, i: (i, 0)), pl.BlockSpec((c, d), lambda j, i: (j, 0)), o_spec, o_spec], (dy, wd, dh_dg, dh_du),
        [o_spec, o_spec], [jax.ShapeDtypeStruct((s, f), BF16), jax.ShapeDtypeStruct((s, f), BF16)],
        [], ("parallel", "parallel"), rider)


FOX_TQ = 512


def fox_tile(s_len):
    return min(FOX_TQ, s_len)


def _heads_per_block(h):
    return 2 if h % 2 == 0 else 1


def _fox_queries(q):
    return (q * (HEAD_DIM ** -0.5)).astype(BF16)


def _fox_scores(qs, kc, cq, ck, diagonal):
    s = lax.dot_general(qs, kc.astype(BF16), _DIMS["nt"], preferred_element_type=F32) + cq - ck
    if not diagonal:
        return s
    return jnp.where(lax.broadcasted_iota(jnp.int32, s.shape, 0) >= lax.broadcasted_iota(jnp.int32, s.shape, 1), s, MASK_VALUE)


def _fox_specs(h, s_len, tq):
    hb = _heads_per_block(h)
    qb = pl.BlockSpec((tq, hb * HEAD_DIM), lambda pp, i: (i, pp))
    kb = pl.BlockSpec((s_len, hb * HEAD_DIM), lambda pp, i: (0, pp))
    colb = pl.BlockSpec((hb, tq, 1), lambda pp, i: (pp, i, 0))
    rowb = pl.BlockSpec((hb, s_len // tq, 1, tq), lambda pp, i: (pp, 0, 0, 0))
    return hb, qb, kb, colb, rowb


def fox_fwd(q, k, v, cq, ck, rider=None):
    s_len, hd = q.shape
    h, d = hd // HEAD_DIM, HEAD_DIM
    tq = fox_tile(s_len)
    hb, qb, kb, colb, rowb = _fox_specs(h, s_len, tq)

    def body(q_ref, k_ref, v_ref, cq_ref, ck_ref, o_ref, lse_ref):
        i = pl.program_id(1)
        for hh in range(hb):
            lanes = slice(hh * d, (hh + 1) * d)
            qs, cqv = _fox_queries(q_ref[:, lanes]), cq_ref[hh]

            def chunk(c, carry, diagonal=False):
                m, l, acc = carry
                rows = pl.ds(pl.multiple_of(c * tq, tq), tq)
                s = _fox_scores(qs, k_ref[rows, lanes], cqv, ck_ref[hh, c], diagonal)
                m_new = jnp.maximum(m, jnp.max(s, axis=-1, keepdims=True))
                alpha = jnp.exp(m - m_new)
                p = jnp.exp(s - m_new)
                acc = alpha * acc + jnp.dot(p.astype(BF16), v_ref[rows, lanes].astype(BF16), preferred_element_type=F32)
                return m_new, alpha * l + jnp.sum(p, axis=-1, keepdims=True), acc

            init = (jnp.full((tq, 1), MASK_VALUE, F32), jnp.zeros((tq, 1), F32), jnp.zeros((tq, d), F32))
            m, l, acc = chunk(i, lax.fori_loop(0, i, chunk, init), diagonal=True)
            o_ref[:, lanes] = acc / l
            lse_ref[hh] = m + jnp.log(l)

    return _call(
        "fox_fwd", body, (h // hb, s_len // tq), [qb, kb, kb, colb, rowb], (q, k, v, cq, ck), [qb, colb],
        [jax.ShapeDtypeStruct((s_len, hd), F32), jax.ShapeDtypeStruct((h, s_len, 1), F32)], (), ("parallel", "parallel"), rider)


def fox_bwd(q, k, v, cq, ck, o, lse, do, rider=None):
    s_len, hd = q.shape
    h, d = hd // HEAD_DIM, HEAD_DIM
    tq = fox_tile(s_len)
    scale = HEAD_DIM ** -0.5
    hb, qb, kb, colb, rowb = _fox_specs(h, s_len, tq)

    def body(q_ref, k_ref, v_ref, cq_ref, ck_ref, o_ref, lse_ref, do_ref, dq_ref, dk_ref, dv_ref, dcq_ref, dck_ref):
        i = pl.program_id(1)

        @pl.when(i == 0)
        def _():
            dk_ref[...] = jnp.zeros_like(dk_ref)
            dv_ref[...] = jnp.zeros_like(dv_ref)
            dck_ref[...] = jnp.zeros_like(dck_ref)

        heads = []
        for hh in range(hb):
            lanes = slice(hh * d, (hh + 1) * d)
            dof = do_ref[:, lanes]
            heads.append((lanes, _fox_queries(q_ref[:, lanes]), cq_ref[hh], lse_ref[hh], dof.astype(BF16),
                          jnp.sum(dof * o_ref[:, lanes], axis=-1, keepdims=True)))

        def chunk(c, carry, diagonal=False):
            rows = pl.ds(pl.multiple_of(c * tq, tq), tq)
            out, dks, dvs = [], [], []
            for hh, (lanes, qs, cqv, lse_h, dob, delta) in enumerate(heads):
                dq, dcq = carry[hh]
                kc = k_ref[rows, lanes]
                p = jnp.exp(_fox_scores(qs, kc, cqv, ck_ref[hh, c], diagonal) - lse_h)
                dp = lax.dot_general(dob, v_ref[rows, lanes].astype(BF16), _DIMS["nt"], preferred_element_type=F32)
                ds = p * (dp - delta)
                dsb = ds.astype(BF16)
                dvs.append(lax.dot_general(p.astype(BF16), dob, _DIMS["tn"], preferred_element_type=F32))
                dks.append(lax.dot_general(dsb, qs, _DIMS["tn"], preferred_element_type=F32))
                dck_ref[hh, c] -= jnp.sum(ds, axis=0, keepdims=True)
                out.append((dq + jnp.dot(dsb, kc.astype(BF16), preferred_element_type=F32), dcq + jnp.sum(ds, axis=-1, keepdims=True)))
            dk_ref[rows, :] += jnp.concatenate(dks, axis=1)
            dv_ref[rows, :] += jnp.concatenate(dvs, axis=1)
            return tuple(out)

        init = tuple((jnp.zeros((tq, d), F32), jnp.zeros((tq, 1), F32)) for _ in range(hb))
        done = chunk(i, lax.fori_loop(0, i, chunk, init), diagonal=True)
        dq_ref[...] = jnp.concatenate([dq for dq, _ in done], axis=1) * scale
        for hh, (_, dcq) in enumerate(done):
            dcq_ref[hh] = dcq

    return _call(
        "fox_bwd", body, (h // hb, s_len // tq), [qb, kb, kb, colb, rowb, qb, colb, qb], (q, k, v, cq, ck, o, lse, do),
        [qb, kb, kb, colb, rowb],
        [jax.ShapeDtypeStruct((s_len, hd), F32)] * 3
        + [jax.ShapeDtypeStruct((h, s_len, 1), F32), jax.ShapeDtypeStruct((h, s_len // tq, 1, tq), F32)],
        (), ("parallel", "arbitrary"), rider)


def _stack_heads(ref, first, g):
    return jnp.concatenate([ref[:, (first + j) * HEAD_DIM:(first + j + 1) * HEAD_DIM] for j in range(g)], axis=0)


def _window(prev_ref, cur_ref, hh):
    lanes = slice(hh * HEAD_DIM, (hh + 1) * HEAD_DIM)
    return jnp.concatenate([prev_ref[:, lanes], cur_ref[:, lanes]], axis=0).astype(BF16)


def _swa_band(g, w):
    t = lax.broadcasted_iota(jnp.int32, (g * w, 2 * w), 0) % w
    col = lax.broadcasted_iota(jnp.int32, (g * w, 2 * w), 1)
    rel = t + w - col
    band = (rel >= 0) & (rel < w)
    return jnp.where(jnp.stack([band & (col >= w), band]), 0.0, MASK_VALUE).astype(F32)


def _swa_probs(qs, kw, sink, band):
    s = lax.dot_general(qs, kw, _DIMS["nt"], preferred_element_type=F32) + band
    m = jnp.maximum(jnp.max(s, axis=-1, keepdims=True), sink)
    p = jnp.exp(s - m)
    ps = jnp.exp(sink - m)
    linv = 1.0 / (jnp.sum(p, axis=-1, keepdims=True) + ps)
    return p * linv, ps * linv


def _swa_specs(hk, g, s_len):
    w = WINDOW
    assert s_len % w == 0
    hb = _heads_per_block(hk)
    qb = pl.BlockSpec((w, hb * g * HEAD_DIM), lambda pp, n: (n, pp))
    prev = pl.BlockSpec((w, hb * HEAD_DIM), lambda pp, n: (jnp.maximum(n - 1, 0), pp))
    cur = pl.BlockSpec((w, hb * HEAD_DIM), lambda pp, n: (n, pp))
    sb = pl.BlockSpec((hb, g * w, 1), lambda pp, n: (pp, 0, 0))
    band = pl.BlockSpec((None, g * w, 2 * w), lambda pp, n: (jnp.minimum(n, 1), 0, 0))
    return hb, qb, prev, cur, sb, band


def swa_fwd(q, k, v, sink, rider=None):
    s_len = q.shape[0]
    hk = k.shape[1] // HEAD_DIM
    g = q.shape[1] // k.shape[1]
    w, d = WINDOW, HEAD_DIM
    hb, qb, prev, cur, sb, bandb = _swa_specs(hk, g, s_len)

    def body(q_ref, kp_ref, kc_ref, vp_ref, vc_ref, sink_ref, band_ref, o_ref):
        for hh in range(hb):
            qs = (_stack_heads(q_ref, hh * g, g) * (HEAD_DIM ** -0.5)).astype(BF16)
            p, _ = _swa_probs(qs, _window(kp_ref, kc_ref, hh), sink_ref[hh], band_ref[...])
            o = jnp.dot(p.astype(BF16), _window(vp_ref, vc_ref, hh), preferred_element_type=F32)
            for j in range(g):
                o_ref[:, (hh * g + j) * d:(hh * g + j + 1) * d] = o[j * w:(j + 1) * w]

    (o,), rid = _call("swa_fwd", body, (hk // hb, s_len // w), [qb, prev, cur, prev, cur, sb, bandb],
                      (q, k, k, v, v, sink, _swa_band(g, w)), [qb], [jax.ShapeDtypeStruct(q.shape, F32)], (),
                      ("parallel", "parallel"), rider)
    return o, rid


def swa_bwd(q, k, v, sink, o, do, rider=None):
    s_len = q.shape[0]
    hk = k.shape[1] // HEAD_DIM
    g = q.shape[1] // k.shape[1]
    w, d = WINDOW, HEAD_DIM
    scale = HEAD_DIM ** -0.5
    hb, qb, prev, cur, sb, bandb = _swa_specs(hk, g, s_len)

    def body(q_ref, kp_ref, kc_ref, vp_ref, vc_ref, sink_ref, band_ref, o_ref, do_ref, dq_ref, dkp_ref, dkc_ref, dvp_ref, dvc_ref,
             dsink_ref):
        @pl.when(pl.program_id(1) == 0)
        def _():
            dsink_ref[...] = jnp.zeros_like(dsink_ref)

        for hh in range(hb):
            lanes = slice(hh * d, (hh + 1) * d)
            qs = (_stack_heads(q_ref, hh * g, g) * scale).astype(BF16)
            kw, vw = _window(kp_ref, kc_ref, hh), _window(vp_ref, vc_ref, hh)
            p, ps = _swa_probs(qs, kw, sink_ref[hh], band_ref[...])
            dof = _stack_heads(do_ref, hh * g, g)
            dob = dof.astype(BF16)
            delta = jnp.sum(dof * _stack_heads(o_ref, hh * g, g), axis=-1, keepdims=True)
            dp = lax.dot_general(dob, vw, _DIMS["nt"], preferred_element_type=F32)
            ds = p * (dp - delta)
            dsb = ds.astype(BF16)
            dsink_ref[hh] -= ps * delta
            dq = jnp.dot(dsb, kw, preferred_element_type=F32) * scale
            for j in range(g):
                dq_ref[:, (hh * g + j) * d:(hh * g + j + 1) * d] = dq[j * w:(j + 1) * w]
            dkw = lax.dot_general(dsb, qs, _DIMS["tn"], preferred_element_type=F32)
            dvw = lax.dot_general(p.astype(BF16), dob, _DIMS["tn"], preferred_element_type=F32)
            dkp_ref[:, lanes] = dkw[:w]
            dkc_ref[:, lanes] = dkw[w:]
            dvp_ref[:, lanes] = dvw[:w]
            dvc_ref[:, lanes] = dvw[w:]

    kv_shape = jax.ShapeDtypeStruct(k.shape, F32)
    (dq, dkp, dkc, dvp, dvc, dsink), rid = _call(
        "swa_bwd", body, (hk // hb, s_len // w), [qb, prev, cur, prev, cur, sb, bandb, qb, qb],
        (q, k, k, v, v, sink, _swa_band(g, w), o, do),
        [qb, cur, cur, cur, cur, sb],
        [jax.ShapeDtypeStruct(q.shape, F32), kv_shape, kv_shape, kv_shape, kv_shape, jax.ShapeDtypeStruct((hk, g * w, 1), F32)],
        (), ("parallel", "arbitrary"), rider)

    def shift_up(a):
        return jnp.concatenate([a[w:], jnp.zeros_like(a[:w])], axis=0)

    return (dq, dkc + shift_up(dkp), dvc + shift_up(dvp), dsink), rid


def loss_call(y, target):
    s, d = y.shape
    tm = _pick(s, (512, 256, 128))

    def body(y_ref, t_ref, l_ref, dy_ref):
        e = y_ref[...] - t_ref[...]
        dy_ref[...] = e * (1.0 / d)

        @pl.when(pl.program_id(0) == 0)
        def _():
            l_ref[...] = jnp.zeros_like(l_ref)

        l_ref[...] += jnp.sum(jnp.sum(e * e, axis=0, keepdims=True), axis=1, keepdims=True) * (0.5 / d)

    row = pl.BlockSpec((tm, d), lambda i: (i, 0))
    (l, dy), _ = _call("loss_head", body, (s // tm,), [row, row], (y, target), [pl.BlockSpec((1, 1), lambda i: (0, 0)), row],
                       [jax.ShapeDtypeStruct((1, 1), F32), jax.ShapeDtypeStruct((s, d), F32)], (), ("arbitrary",))
    return l[0, 0], dy


def _row_tile(rows, cols, itemsize, block_bytes=1 << 20):
    target = max(16, block_bytes // (cols * itemsize))
    fits = [t for t in range(16, rows + 1, 16) if rows % t == 0 and t <= target]
    return fits[-1] if fits else rows


CAST_STEPS = 8


def cast_place(name, ws, p_idx, rider=None):
    n = len(ws)
    assert all(w.shape[0] % (16 * CAST_STEPS) == 0 for w in ws), [w.shape for w in ws]

    def body(p_ref, *refs):
        for w_ref, o_ref in zip(refs[:n], refs[n:]):
            o_ref[...] = w_ref[...].astype(BF16)

    return _call(
        name, body, (CAST_STEPS,), [pl.BlockSpec((w.shape[0] // CAST_STEPS, w.shape[1]), lambda i, pr: (i, 0)) for w in ws], tuple(ws),
        [pl.BlockSpec((None, w.shape[0] // CAST_STEPS, w.shape[1]), lambda i, pr: (pr[0], i, 0)) for w in ws],
        [jax.ShapeDtypeStruct((N_CHIPS,) + w.shape, BF16) for w in ws], (), ("parallel",), rider, prefetch=(p_idx,))


def chip_sum(name, grad, theirs, c_idx):
    ns, r, cols = grad.shape
    rh = r // 2
    tr = _row_tile(rh, cols, 2, 2 << 20)
    nb = rh // tr

    def body(c_ref, a_ref, b_ref, o_ref):
        o_ref[...] = (a_ref[...].astype(F32) + b_ref[...].astype(F32)).astype(o_ref.dtype)

    return pl.pallas_call(
        body, name=name,
        grid_spec=pltpu.PrefetchScalarGridSpec(
            num_scalar_prefetch=1, grid=(ns, nb),
            in_specs=[pl.BlockSpec((None, tr, cols), lambda q, i, cr: (q, cr[0] * nb + i, 0)),
                      pl.BlockSpec((None, tr, cols), lambda q, i, cr: (q, i, 0))],
            out_specs=pl.BlockSpec((None, tr, cols), lambda q, i, cr: (q, i, 0))),
        out_shape=jax.ShapeDtypeStruct((ns, rh, cols), BF16),
        compiler_params=pltpu.CompilerParams(dimension_semantics=("parallel", "parallel"), vmem_limit_bytes=VMEM_LIMIT),
    )(c_idx, grad, theirs)


def owner_sum(name, sums, got, pc_idx):
    ns, rh, cols = sums.shape
    tr = _row_tile(rh, cols, 4, 2 << 20)
    nb = rh // tr

    def body(pc_ref, a_ref, b_ref, o_ref):
        o_ref[...] = ((a_ref[...].astype(F32) + b_ref[0].astype(F32)) + b_ref[1].astype(F32)) + b_ref[2].astype(F32)

    return pl.pallas_call(
        body, name=name,
        grid_spec=pltpu.PrefetchScalarGridSpec(
            num_scalar_prefetch=1, grid=(nb,),
            in_specs=[pl.BlockSpec((None, tr, cols), lambda i, pc: (pc[0], i, 0)),
                      pl.BlockSpec((3, tr, cols), lambda i, pc: (0, i, 0))],
            out_specs=pl.BlockSpec((tr, cols), lambda i, pc: (pc[1] * nb + i, 0))),
        out_shape=jax.ShapeDtypeStruct((2 * rh, cols), F32),
        compiler_params=pltpu.CompilerParams(dimension_semantics=("parallel",), vmem_limit_bytes=VMEM_LIMIT),
    )(pc_idx, sums, got)


def adamw(name, w, g, m, v):
    r, cols = w.shape
    tr = _row_tile(r, cols, 4)
    c1 = 1.0 / (1.0 - ADAM_B1 ** ADAM_STEP)
    c2 = 1.0 / (1.0 - ADAM_B2 ** ADAM_STEP)

    def body(w_ref, g_ref, m_ref, v_ref, go_ref, d_ref, nm_ref, nv_ref):
        gv = g_ref[...]
        nm = ADAM_B1 * m_ref[...] + (1.0 - ADAM_B1) * gv
        nv = ADAM_B2 * v_ref[...] + (1.0 - ADAM_B2) * (gv * gv)
        go_ref[...] = gv
        d_ref[...] = -ADAM_LR * ((nm * c1) / (jnp.sqrt(nv * c2) + ADAM_EPS) + ADAM_WD * w_ref[...])
        nm_ref[...] = nm
        nv_ref[...] = nv

    blk = pl.BlockSpec((tr, cols), lambda i: (i, 0))
    return _call(name, body, (r // tr,), [blk] * 4, (w, g, m, v), [blk] * 4, [jax.ShapeDtypeStruct((r, cols), F32)] * 4, (), ("parallel",))


def _rope(x, cos, sin):
    x3 = x.reshape(x.shape[0], -1, HEAD_DIM)
    x1, x2 = x3[..., : HEAD_DIM // 2], x3[..., HEAD_DIM // 2:]
    cos, sin = cos[:, None, :], sin[:, None, :]
    return jnp.concatenate([x1 * cos - x2 * sin, x2 * cos + x1 * sin], axis=-1).reshape(x.shape)


def _win_layout(d_model):
    hf = hq = d_model // (2 * HEAD_DIM)
    hk = hq // 4
    sizes = [hf * HEAD_DIM, hf * HEAD_DIM, hf * HEAD_DIM, hf, hq * HEAD_DIM, hk * HEAD_DIM, hk * HEAD_DIM]
    return hf, hq, hk, sizes


class WinPlan:
    def __init__(self, d_model, ns=N_CHIPS):
        self.hf, self.hq, self.hk, self.sizes = _win_layout(d_model)
        self.ns, self.cs = ns, sum(self.sizes) // ns
        self.jump_at = sum(self.sizes[:4])
        self.jump_by = -self.jump_at % _LANES
        self.base = [self.pos(s * self.cs) // _LANES * _LANES for s in range(ns)]
        ends = [self.pos((s + 1) * self.cs - 1) + 1 - self.base[s] for s in range(ns)]
        self.width = -(-max(ends) // _LANES) * _LANES
        self.total = -(-max(b + self.width for b in self.base) // 1024) * 1024
        starts = [0]
        for sz in self.sizes:
            starts.append(starts[-1] + sz)
        self.segments = [(self.pos(a), sz) for a, sz in zip(starts, self.sizes)]

    def pos(self, g):
        return g if g < self.jump_at else g + self.jump_by

    def pieces(self, s):
        g0, g1 = s * self.cs, (s + 1) * self.cs
        cuts = [g0] + ([self.jump_at] if g0 < self.jump_at < g1 else []) + [g1]
        return [(a - g0, b - a, self.pos(a) - self.base[s]) for a, b in zip(cuts[:-1], cuts[1:])]

    def place(self, w, s):
        parts, at = [], 0
        for t0, n, j0 in self.pieces(s):
            parts += [jnp.zeros((w.shape[0], j0 - at), w.dtype), w[:, t0:t0 + n]]
            at = j0 + n
        return jnp.concatenate(parts + [jnp.zeros((w.shape[0], self.width - at), w.dtype)], axis=1)

    def unplace(self, slab, s):
        return jnp.concatenate([slab[:, j0:j0 + n] for _, n, j0 in self.pieces(s)], axis=1)

    def assemble(self, slabs):
        full = jnp.zeros((slabs.shape[1], self.total), slabs.dtype)
        for s in range(self.ns):
            full = full.at[:, self.base[s]:self.base[s] + self.width].add(slabs[s])
        return full

    def split(self, full):
        return jnp.stack([full[:, b:b + self.width] for b in self.base])


def _attn_inputs(proj, sm, positions):
    s_len = proj.shape[0]
    plan = WinPlan(sm["norm_mix_g"].shape[0])
    hf, hq, hk = plan.hf, plan.hq, plan.hk
    grp = hq // hk
    q_f, k_f, v_f, f_logit, q_s, k_s, v_s = [proj[:, a:a + n] for a, n in plan.segments]

    q_f = head_rms("fox_qnorm", q_f, sm["fox_q_norm_g"])
    k_f = head_rms("fox_knorm", k_f, sm["fox_k_norm_g"])
    log_f = jax.nn.log_sigmoid(f_logit + sm["b_forget"])
    c = jnp.cumsum(log_f, axis=0).T

    inv_freq = ROPE_THETA ** (-jnp.arange(0, HEAD_DIM, 2, dtype=F32) / HEAD_DIM)
    ang = positions.astype(F32)[:, None] * inv_freq
    cos, sin = jnp.cos(ang), jnp.sin(ang)
    q_s = _rope(head_rms("swa_qnorm", q_s, sm["swa_q_norm_g"]), cos, sin)
    k_s = _rope(head_rms("swa_knorm", k_s, sm["swa_k_norm_g"]), cos, sin)
    sink = jnp.broadcast_to(sm["swa_sinks"].reshape(hk, grp, 1, 1), (hk, grp, WINDOW, 1)).reshape(hk, grp * WINDOW, 1)
    tq = fox_tile(s_len)
    return (q_f, k_f, v_f, c[:, :, None], c.reshape(hf, s_len // tq, 1, tq)), (q_s, k_s, v_s, sink)


_BIG = ("ffn1_w_gate", "ffn1_w_up", "ffn1_w_down", "w_in", "w_out", "ffn2_w_gate", "ffn2_w_up", "ffn2_w_down")
_ROW_SHARDED = ("ffn1_w_down", "w_out", "ffn2_w_down")
_SMALL = ("norm_ffn1_g", "norm_mix_g", "b_forget", "fox_q_norm_g", "fox_k_norm_g", "swa_q_norm_g", "swa_k_norm_g", "swa_sinks",
          "out_norm_fox_g", "out_norm_swa_g", "norm_ffn2_g")
_ATTN_SMALL = ("norm_mix_g", "b_forget", "fox_q_norm_g", "fox_k_norm_g", "swa_q_norm_g", "swa_k_norm_g", "swa_sinks")
_ALL = ("norm_ffn1_g", "ffn1_w_gate", "ffn1_w_up", "ffn1_w_down", "norm_mix_g", "w_in", "b_forget", "fox_q_norm_g", "fox_k_norm_g",
        "swa_q_norm_g", "swa_k_norm_g", "swa_sinks", "out_norm_fox_g", "out_norm_swa_g", "w_out", "norm_ffn2_g", "ffn2_w_gate",
        "ffn2_w_up", "ffn2_w_down")


def _pack_small(d):
    parts = []
    for k in _SMALL:
        v = d[k].reshape(-1)
        rows = -(-v.shape[0] // _LANES)
        parts.append(jnp.pad(v, (0, rows * _LANES - v.shape[0])).reshape(rows, _LANES))
    a = jnp.concatenate(parts, axis=0)
    return jnp.pad(a, ((0, -a.shape[0] % 8), (0, 0)))


def _unpack_small(a, like):
    out, r0 = {}, 0
    for k in _SMALL:
        nvals = like[k].shape[1]
        rows = -(-nvals // _LANES)
        out[k] = a[r0:r0 + rows].reshape(-1)[:nvals].reshape(1, nvals)
        r0 += rows
    return out


def _stacked(w):
    return w.reshape(-1, w.shape[-1])


def _local_step(shards, sm, x, positions, target, p_idx, c_idx, pc_idx):
    ns = N_CHIPS
    hf, hq, hk, _ = _win_layout(x.shape[1])
    s_len = x.shape[0]
    full = {}

    def fetch(*jobs):
        names = list(dict.fromkeys(n for n, _, _ in jobs))
        return names, gather([bufs[n] for n in names], [(names.index(n), kind, part) for n, kind, part in jobs])

    def take(names, rid):
        for n, b in zip(names, rid[0]):
            bufs[n] = b

    n1 = ["ffn1_w_gate", "ffn1_w_up", "ffn1_w_down"]
    n2 = ["ffn2_w_gate", "ffn2_w_up", "ffn2_w_down"]
    later = ["w_in", "w_out"] + n2
    placed, _ = cast_place("cast_place_ffn1", [shards[n] for n in n1], p_idx)
    bufs = dict(zip(n1, placed))
    gate1, up1, down1 = n1
    gate2, up2, down2 = n2
    names, rider = fetch((gate1, "ici", WHOLE), (up1, "ici", WHOLE))
    placed, rid = cast_place("cast_place_later", [shards[n] for n in later], p_idx, rider=rider)
    bufs.update(zip(later, placed))
    take(names, rid)
    names, rider = fetch((gate1, "d2d", WHOLE), (up1, "d2d", WHOLE))
    xn1, r1, rid = rms_fwd("ffn1_norm", x, sm["norm_ffn1_g"], BF16, rider=rider)
    take(names, rid)
    names, rider = fetch((down1, "ici", WHOLE))
    (hid1, g1, u1), rid = ffn_gu("ffn1_gu", xn1, bufs[gate1], bufs[up1], rider=rider)
    take(names, rid)
    names, rider = fetch((down1, "d2d", WHOLE))
    take(names, run_step("gather_d2d_ffn1_down", rider))
    wd1 = _stacked(bufs[down1])
    names, rider = fetch(("w_in", "ici", WHOLE))
    h1, rid = mm_nn("ffn1_down", hid1, wd1, scale=0.5, resid=x, rider=rider)
    take(names, rid)

    names, rider = fetch(("w_in", "d2d", WHOLE))
    u, r_mix, rid = rms_fwd("mix_norm", h1, sm["norm_mix_g"], BF16, rider=rider)
    take(names, rid)
    names, rider = fetch(("w_out", "ici", WHOLE), (gate2, "ici", (0, 1, 4)))
    plan = WinPlan(x.shape[1])
    win = plan.assemble(bufs["w_in"])
    proj, rid = mm_nn("mix_inproj", u, win, rider=rider)
    take(names, rid)
    sm_attn = {k: sm[k] for k in _ATTN_SMALL}
    (fox_in, swa_in), attn_vjp = jax.vjp(lambda pr, s: _attn_inputs(pr, s, positions), proj, sm_attn)
    names, rider = fetch((gate2, "ici", (1, 4, 4)), ("w_out", "d2d", WHOLE), (gate2, "d2d", (0, 1, 4)))
    (o_f, lse), rid = fox_fwd(*fox_in, rider=rider)
    take(names, rid)
    names, rider = fetch((up2, "ici", (0, 3, 4)), (gate2, "d2d", (1, 4, 4)))
    o_s, rid = swa_fwd(*swa_in, rider=rider)
    take(names, rid)
    o_fox, o_swa = o_f, o_s
    nf, r_fox = rms_fwd("out_norm_fox", o_fox, sm["out_norm_fox_g"], BF16)
    nsw, r_swa = rms_fwd("out_norm_swa", o_swa, sm["out_norm_swa_g"], BF16)
    o = jnp.concatenate([nf, nsw], axis=-1)
    wout = _stacked(bufs["w_out"])
    names, rider = fetch((up2, "ici", (3, 4, 4)), (up2, "d2d", (0, 3, 4)))
    h2, rid = mm_nn("out_proj", o, wout, resid=h1, rider=rider)
    take(names, rid)

    names, rider = fetch((up2, "d2d", (3, 4, 4)))
    xn2, r2, rid = rms_fwd("ffn2_norm", h2, sm["norm_ffn2_g"], BF16, rider=rider)
    take(names, rid)
    names, rider = fetch((down2, "ici", WHOLE))
    (hid2, g2, u2), rid = ffn_gu("ffn2_gu", xn2, bufs[gate2], bufs[up2], rider=rider)
    take(names, rid)
    names, rider = fetch((down2, "d2d", WHOLE))
    take(names, run_step("gather_d2d_ffn2_down", rider))
    wd2 = _stacked(bufs["ffn2_w_down"])
    y, _ = mm_nn("ffn2_down", hid2, wd2, scale=0.5, resid=h2)
    loss, dy = loss_call(y, target)

    red = {}

    def grad(n, g):
        red[n] = {"grad": g.reshape(ns, -1, g.shape[-1])}

    def ride(*steps):
        def done(rid):
            a0 = n0 = 0
            for rd, cb in steps:
                cb(rid[0][a0:a0 + len(rd.aliased)], rid[1][n0:n0 + len(rd.news)])
                a0, n0 = a0 + len(rd.aliased), n0 + len(rd.news)

        return (combine(*[s[0] for s in steps]) if len(steps) > 1 else steps[0][0]), done

    def xchg(*names):
        def cb(al, news):
            for n, t in zip(names, news):
                red[n]["sum"] = chip_sum("chip_sum_" + n, red[n]["grad"], t, c_idx)

        return exchange_halves([red[n]["grad"] for n in names]), cb

    def scat(n, part=WHOLE):
        def cb(al, news):
            red[n]["got"] = (al or news)[0]

        return scatter_to_owner([red[n]["sum"]], [red[n]["got"]] if "got" in red[n] else None, part), cb

    def own(n):
        red[n]["half"] = owner_sum("owner_sum_" + n, red[n]["sum"], red[n]["got"], pc_idx)

    def join(*names):
        return join_halves([red[n]["half"] for n in names]), lambda al, news: full.update(zip(names, al))

    dwd2, _ = mm_tn("ffn2_dwd", hid2, dy, out_dtype=BF16, scale=0.5)
    grad(down2, dwd2)
    rider, done = ride(xchg(down2))
    (dg2, du2), rid = ffn_dh("ffn2_dh", dy, wd2, g2, u2, ns, 0.5, rider=rider)
    done(rid)
    rider, done = ride(scat(down2, (0, 1, 2)))
    dwg2, rid = mm_tn_sharded("ffn2_dwg", xn2, dg2, ns, rider=rider)
    done(rid)
    grad(gate2, dwg2)
    rider, done = ride(scat(down2, (1, 2, 2)), xchg(gate2))
    dwu2, rid = mm_tn_sharded("ffn2_dwu", xn2, du2, ns, rider=rider)
    done(rid)
    grad(up2, dwu2)
    rider, done = ride(scat(gate2, (0, 1, 2)), xchg(up2))
    dxn, rid = mm_nt_sharded("ffn2_dxn_g", dg2, bufs[gate2], rider=rider)
    done(rid)
    rider, done = ride(scat(gate2, (1, 2, 2)))
    dxn, rid = mm_nt_sharded("ffn2_dxn_u", du2, bufs[up2], resid=dxn, rider=rider)
    done(rid)
    dh2, dgain_ffn2 = rms_bwd("ffn2_dnorm", h2, sm["norm_ffn2_g"], r2, dxn, dres=dy)
    own(down2)
    own(gate2)

    do, _ = mm_nt("out_do", dh2, wout)
    dwout, _ = mm_tn("out_dw", o, dh2, out_dtype=BF16)
    cf = o_fox.shape[1]
    d_fox, dgain_fox = rms_bwd("out_dnorm_fox", o_fox, sm["out_norm_fox_g"], r_fox, do[:, :cf])
    d_swa, dgain_swa = rms_bwd("out_dnorm_swa", o_swa, sm["out_norm_swa_g"], r_swa, do[:, cf:])
    grad("w_out", dwout)
    rider, done = ride(scat(up2))
    swa_cts, rid = swa_bwd(*swa_in, o_s, d_swa, rider=rider)
    done(rid)
    own(up2)
    rider, done = ride(xchg("w_out"), join(down2, gate2, up2))
    fox_cts, rid = fox_bwd(*fox_in, o_f, lse, d_fox, rider=rider)
    done(rid)
    dproj, dsm_attn = attn_vjp((tuple(fox_cts), tuple(swa_cts)))

    rider, done = ride(scat("w_out"))
    du, rid = mm_nt("mix_du", dproj, win, rider=rider)
    done(rid)
    dwin, _ = mm_tn("mix_dwin", u, dproj, out_dtype=BF16)
    grad("w_in", plan.split(dwin))
    dh1, dgain_mix = rms_bwd("mix_dnorm", h1, sm["norm_mix_g"], r_mix, du, dres=dh2)
    own("w_out")

    rider, done = ride(xchg("w_in"))
    dwd1, rid = mm_tn("ffn1_dwd", hid1, dh1, out_dtype=BF16, scale=0.5, rider=rider)
    done(rid)
    grad(down1, dwd1)
    rider, done = ride(scat("w_in"), xchg(down1))
    (dg1, du1), rid = ffn_dh("ffn1_dh", dh1, wd1, g1, u1, ns, 0.5, rider=rider)
    done(rid)
    own("w_in")
    rider, done = ride(scat(down1, (0, 1, 2)), join("w_out", "w_in"))
    dwg1, rid = mm_tn_sharded("ffn1_dwg", xn1, dg1, ns, rider=rider)
    done(rid)
    grad(gate1, dwg1)
    rider, done = ride(scat(down1, (1, 2, 2)), xchg(gate1))
    dwu1, rid = mm_tn_sharded("ffn1_dwu", xn1, du1, ns, rider=rider)
    done(rid)
    grad(up1, dwu1)
    own(down1)
    rider, done = ride(scat(gate1, (0, 1, 2)), xchg(up1), join(down1))
    dxn, rid = mm_nt_sharded("ffn1_dxn_g", dg1, bufs[gate1], rider=rider)
    done(rid)
    rider, done = ride(scat(gate1, (1, 2, 2)), scat(up1, (0, 1, 4)))
    dxn, rid = mm_nt_sharded("ffn1_dxn_u", du1, bufs[up1], resid=dxn, rider=rider)
    done(rid)
    dx, dgain_ffn1 = rms_bwd("ffn1_dnorm", x, sm["norm_ffn1_g"], r1, dxn, dres=dh1)
    own(gate1)

    rider, done = ride(scat(up1, (1, 4, 4)), join(gate1))
    done(run_step("reduce_tail", rider))
    own(up1)
    rider, done = ride(join(up1))
    done(run_step("join_tail", rider))

    g_small = dict(dsm_attn)
    g_small["norm_mix_g"] = g_small["norm_mix_g"] + dgain_mix
    g_small.update(norm_ffn1_g=dgain_ffn1, norm_ffn2_g=dgain_ffn2, out_norm_fox_g=dgain_fox, out_norm_swa_g=dgain_swa)
    return loss, dx, full, g_small


def kernel(x, positions, norm_ffn1_g, ffn1_w_gate, ffn1_w_up, ffn1_w_down, norm_mix_g, w_in, b_forget, fox_q_norm_g, fox_k_norm_g, swa_q_norm_g, swa_k_norm_g, swa_sinks, out_norm_fox_g, out_norm_swa_g, w_out, norm_ffn2_g, ffn2_w_gate, ffn2_w_up, ffn2_w_down, loss_target, m_norm_ffn1_g, m_ffn1_w_gate, m_ffn1_w_up, m_ffn1_w_down, m_norm_mix_g, m_w_in, m_b_forget, m_fox_q_norm_g, m_fox_k_norm_g, m_swa_q_norm_g, m_swa_k_norm_g, m_swa_sinks, m_out_norm_fox_g, m_out_norm_swa_g, m_w_out, m_norm_ffn2_g, m_ffn2_w_gate, m_ffn2_w_up, m_ffn2_w_down, v_norm_ffn1_g, v_ffn1_w_gate, v_ffn1_w_up, v_ffn1_w_down, v_norm_mix_g, v_w_in, v_b_forget, v_fox_q_norm_g, v_fox_k_norm_g, v_swa_q_norm_g, v_swa_k_norm_g, v_swa_sinks, v_out_norm_fox_g, v_out_norm_swa_g, v_w_out, v_norm_ffn2_g, v_ffn2_w_gate, v_ffn2_w_up, v_ffn2_w_down):
    args = dict(locals())
    w = {k: args[k] for k in _ALL}
    m = {k: args["m_" + k] for k in _ALL}
    v = {k: args["v_" + k] for k in _ALL}
    c_idx = lax.axis_index("c").astype(jnp.int32).reshape(1)
    p_idx = (2 * lax.axis_index("x") + lax.axis_index("y")).astype(jnp.int32).reshape(1)
    pc_idx = jnp.concatenate([p_idx, c_idx])

    small = {k: w[k] for k in _SMALL}
    shards = {k: w[k][0] for k in _BIG}
    plan = WinPlan(x.shape[-1])
    shards["w_in"] = lax.switch(p_idx[0], [functools.partial(plan.place, s=s) for s in range(N_CHIPS)], shards["w_in"])
    loss, grad_x, g_shard, g_small = _local_step(shards, {k: w[k][0] for k in _SMALL}, x[0], positions[0], loss_target[0],
                                                 p_idx, c_idx, pc_idx)
    g_shard["w_in"] = lax.switch(p_idx[0], [functools.partial(plan.unplace, s=s) for s in range(N_CHIPS)], g_shard["w_in"])
    loss = lax.psum(loss, ("x", "y", "c"))
    g_small_sum = _unpack_small(all_reduce_small(_pack_small({k: g_small[k].reshape(1, -1) for k in _SMALL})), small)

    grad_w, delta, new_m, new_v = {}, {}, {}, {}
    for k in _BIG:
        (g, d, nm, nv), _ = adamw("adamw_" + k, w[k][0], g_shard[k], m[k][0], v[k][0])
        grad_w[k], delta[k], new_m[k], new_v[k] = g[None], d[None], nm[None], nv[None]
    (_, d, nm, nv), _ = adamw("adamw_small", _pack_small(small), _pack_small(g_small_sum), _pack_small({k: m[k] for k in _SMALL}),
                              _pack_small({k: v[k] for k in _SMALL}))
    grad_w.update(g_small_sum)
    delta.update(_unpack_small(d, small))
    new_m.update(_unpack_small(nm, small))
    new_v.update(_unpack_small(nv, small))

    return (loss, grad_x[None], *[grad_w[k] for k in _ALL], *[delta[k] for k in _ALL], *[new_m[k] for k in _ALL], *[new_v[k] for k in _ALL])
```

```python
import functools

import jax
import jax.numpy as jnp
from jax import lax
from jax.experimental import pallas as pl
from jax.experimental.pallas import tpu as pltpu

F32 = jnp.float32
BF16 = jnp.bfloat16

HEAD_DIM = 64
WINDOW = 128
ROPE_THETA = 10000.0
EPS = 1e-6
N_CHIPS = 4
N_DEV = 8

ADAM_LR = 0.001
ADAM_B1 = 0.9
ADAM_B2 = 0.999
ADAM_EPS = 1e-08
ADAM_WD = 0.01
ADAM_STEP = 10

V7X_VMEM_BYTES = 64 * 1024 * 1024
VMEM_LIMIT = V7X_VMEM_BYTES - 8 * 1024 * 1024
MASK_VALUE = -1e30

_MESH = pl.DeviceIdType.MESH
_HBM = pl.BlockSpec(memory_space=pl.ANY)
_DIMS = {"nn": (((1,), (0,)), ((), ())), "nt": (((1,), (1,)), ((), ())), "tn": (((0,), (0,)), ((), ()))}


def _pick(n, prefs):
    for p in prefs:
        if n % p == 0:
            return p
    return n


class Rider:
    def __init__(self, reads, aliased, news, nsem, build):
        self.reads, self.aliased, self.news, self.nsem, self.build = list(reads), list(aliased), list(news), nsem, build


class _Shifted:
    def __init__(self, ref, off):
        self.ref, self.off = ref, off

    @property
    def at(self):
        return self

    def __getitem__(self, k):
        return self.ref.at[k + self.off]


def combine(*riders):
    def build(reads, al, news, ssem, rsem):
        out = ([], [], [])
        r0 = a0 = n0 = s0 = 0
        for rd in riders:
            nr, na, nn = len(rd.reads), len(rd.aliased), len(rd.news)
            part = rd.build(reads[r0:r0 + nr], al[a0:a0 + na], news[n0:n0 + nn], _Shifted(ssem, s0), _Shifted(rsem, s0))
            for acc, lst in zip(out, part):
                acc.extend(lst)
            r0, a0, n0, s0 = r0 + nr, a0 + na, n0 + nn, s0 + rd.nsem
        return out

    return Rider(sum((r.reads for r in riders), []), sum((r.aliased for r in riders), []), sum((r.news for r in riders), []),
                 sum(r.nsem for r in riders), build)


def _me():
    return lax.axis_index("x"), lax.axis_index("y"), lax.axis_index("c")


def _other_chips(x, y):
    return [(1 - x, y), (x, 1 - y), (1 - x, 1 - y)]


WHOLE = (0, 1, 1)


def _rows(ref, start, rows, part=WHOLE):
    k0, k1, n = part
    assert rows % n == 0, (rows, part)
    idx = (slice(None),) * (len(ref.shape) - 2) + (pl.ds(start + k0 * (rows // n), (k1 - k0) * (rows // n)), slice(None))
    return ref.at[idx]


def _half(ref, h, part=WHOLE):
    rows = ref.shape[-2] // 2
    return _rows(ref, h * rows, rows, part)


def _remote(src, dst, ssem, rsem, k, to):
    return pltpu.make_async_remote_copy(src_ref=src, dst_ref=dst, send_sem=ssem.at[k], recv_sem=rsem.at[k], device_id=to,
                                        device_id_type=_MESH)


def _later(*args):
    return functools.partial(_remote, *args)


def gather(bufs, jobs):
    def build(reads, al, news, ssem, rsem):
        x, y, c = _me()
        p = 2 * x + y
        starts, arrivals = [], []
        for n, (b, kind, part) in enumerate(jobs):
            for j, chip in enumerate(_other_chips(x, y)):
                q = 2 * chip[0] + chip[1]
                if kind == "ici":
                    src, landing, to = _half(al[b].at[p], c, part), _half(al[b].at[q], c, part), (*chip, c)
                else:
                    src, landing, to = _half(al[b].at[q], c, part), _half(al[b].at[q], 1 - c, part), (x, y, 1 - c)
                starts.append(_later(src, src, ssem, rsem, 3 * n + j, to))
                arrivals.append(_later(landing, landing, ssem, rsem, 3 * n + j, to))
        return starts, arrivals, starts

    return Rider([], bufs, [], 3 * len(jobs), build)


def exchange_halves(grads):
    def build(reads, al, news, ssem, rsem):
        x, y, c = _me()
        cps = [_later(_half(g, 1 - c), t, ssem, rsem, w, (x, y, 1 - c)) for w, (g, t) in enumerate(zip(reads, news))]
        return cps, cps, cps

    return Rider(grads, [], [jax.ShapeDtypeStruct((g.shape[0], g.shape[1] // 2, g.shape[2]), g.dtype) for g in grads], len(grads), build)


def scatter_to_owner(sums, gots=None, part=WHOLE):
    def build(reads, al, news, ssem, rsem):
        x, y, c = _me()
        cps = []
        for w, (s, got) in enumerate(zip(reads, al or news)):
            rows = s.shape[-2]
            for j, chip in enumerate(_other_chips(x, y)):
                cps.append(_later(_rows(s.at[2 * chip[0] + chip[1]], 0, rows, part), _rows(got.at[j], 0, rows, part), ssem, rsem,
                                  3 * w + j, (*chip, c)))
        return cps, cps, cps

    news = [] if gots else [jax.ShapeDtypeStruct((3,) + s.shape[1:], s.dtype) for s in sums]
    return Rider(sums, gots or [], news, 3 * len(sums), build)


def join_halves(fulls):
    def build(reads, al, news, ssem, rsem):
        x, y, c = _me()
        starts, arrivals = [], []
        for w, f in enumerate(al):
            mine, landing = _half(f, c), _half(f, 1 - c)
            starts.append(_later(mine, mine, ssem, rsem, w, (x, y, 1 - c)))
            arrivals.append(_later(landing, landing, ssem, rsem, w, (x, y, 1 - c)))
        return starts, arrivals, starts

    return Rider([], fulls, [], len(fulls), build)


def _start_and_wait(rider, reads, al, news, ssem, rsem, first, last):
    @pl.when(first)
    def _():
        for cp in rider.build(reads, al, news, ssem, rsem)[0]:
            cp().start()

    def finish():
        @pl.when(last)
        def _():
            _, arrivals, sends = rider.build(reads, al, news, ssem, rsem)
            for cp in arrivals:
                cp().wait_recv()
            for cp in sends:
                cp().wait_send()

    return finish


def _call(name, body, grid, in_specs, args, out_specs, out_shape, scratch=(), semantics=None, rider=None, prefetch=()):
    n_pre, n_in, n_out, n_scr = len(prefetch), len(args), len(out_shape), len(scratch)
    nr, na, nn = (len(rider.reads), len(rider.aliased), len(rider.news)) if rider else (0, 0, 0)

    def wrapped(*refs):
        pre, refs = refs[:n_pre], refs[n_pre:]
        ins, reads = refs[:n_in], refs[n_in:n_in + nr]
        o0 = n_in + nr + na
        outs, al, news = refs[o0:o0 + n_out], refs[o0 + n_out:o0 + n_out + na], refs[o0 + n_out + na:o0 + n_out + na + nn]
        s0 = o0 + n_out + na + nn
        scr, (ssem, rsem) = refs[s0:s0 + n_scr], refs[s0 + n_scr:]
        first = functools.reduce(jnp.logical_and, [pl.program_id(a) == 0 for a in range(len(grid))])
        last = functools.reduce(jnp.logical_and, [pl.program_id(a) == g - 1 for a, g in enumerate(grid)])
        finish = _start_and_wait(rider, reads, al, news, ssem, rsem, first, last)
        body(*pre, *ins, *outs, *scr)
        finish()

    kernel_fn, all_in, all_out, shapes, scr = body, list(in_specs), list(out_specs), list(out_shape), list(scratch)
    operands, aliases = (*prefetch, *args), {}
    if rider:
        kernel_fn, semantics = wrapped, ("arbitrary",) * len(grid)
        all_in += [_HBM] * (nr + na)
        all_out += [_HBM] * (na + nn)
        shapes += [jax.ShapeDtypeStruct(a.shape, a.dtype) for a in rider.aliased] + rider.news
        scr += [pltpu.SemaphoreType.DMA((rider.nsem,)), pltpu.SemaphoreType.DMA((rider.nsem,))]
        operands += (*rider.reads, *rider.aliased)
        aliases = {n_pre + n_in + nr + i: n_out + i for i in range(na)}
    params = pltpu.CompilerParams(dimension_semantics=semantics, vmem_limit_bytes=VMEM_LIMIT)
    if n_pre:
        spec = pltpu.PrefetchScalarGridSpec(num_scalar_prefetch=n_pre, grid=grid, in_specs=all_in, out_specs=all_out, scratch_shapes=scr)
        outs = pl.pallas_call(kernel_fn, name=name, grid_spec=spec, out_shape=shapes, input_output_aliases=aliases, compiler_params=params)(*operands)
    else:
        outs = pl.pallas_call(kernel_fn, name=name, grid=grid, in_specs=all_in, out_specs=all_out, out_shape=shapes, scratch_shapes=scr,
                              input_output_aliases=aliases, compiler_params=params)(*operands)
    return list(outs[:n_out]), ((list(outs[n_out:n_out + na]), list(outs[n_out + na:])) if rider else None)


def run_step(name, rider):
    nr, na, nn = len(rider.reads), len(rider.aliased), len(rider.news)

    def body(*refs):
        reads = refs[:nr]
        al, news = refs[nr + na:nr + 2 * na], refs[nr + 2 * na:nr + 2 * na + nn]
        ssem, rsem = refs[nr + 2 * na + nn:]
        starts, arrivals, sends = rider.build(reads, al, news, ssem, rsem)
        for cp in starts:
            cp().start()
        for cp in arrivals:
            cp().wait_recv()
        for cp in sends:
            cp().wait_send()

    outs = pl.pallas_call(
        body, name=name, in_specs=[_HBM] * (nr + na), out_specs=[_HBM] * (na + nn),
        out_shape=[jax.ShapeDtypeStruct(a.shape, a.dtype) for a in rider.aliased] + rider.news,
        input_output_aliases={nr + i: i for i in range(na)},
        scratch_shapes=[pltpu.SemaphoreType.DMA((rider.nsem,)), pltpu.SemaphoreType.DMA((rider.nsem,))],
    )(*rider.reads, *rider.aliased)
    return list(outs[:na]), list(outs[na:])


def all_reduce_small(v):
    rows, lanes = v.shape

    def body(v_ref, o_ref, slots, send_sems, recv_sems):
        x, y, c = _me()
        me = 4 * x + 2 * y + c
        slots[me] = v_ref[...]
        cps = []
        for k in range(1, N_DEV):
            peer = (x ^ (k >> 2), y ^ ((k >> 1) & 1), c ^ (k & 1))
            cps.append(_remote(v_ref, slots.at[me], send_sems, recv_sems, k - 1, peer))
            cps[-1].start()
        for k in range(1, N_DEV):
            theirs = slots.at[me ^ k]
            _remote(theirs, theirs, send_sems, recv_sems, k - 1, (x, y, c)).wait_recv()
        for cp in cps:
            cp.wait_send()
        acc = slots[0]
        for i in range(1, N_DEV):
            acc = acc + slots[i]
        o_ref[...] = acc

    return pl.pallas_call(
        body, name="all_reduce_small",
        in_specs=[pl.BlockSpec(memory_space=pltpu.VMEM)], out_specs=pl.BlockSpec(memory_space=pltpu.VMEM),
        out_shape=jax.ShapeDtypeStruct((rows, lanes), F32),
        scratch_shapes=[pltpu.VMEM((N_DEV, rows, lanes), F32), pltpu.SemaphoreType.DMA((N_DEV - 1,)), pltpu.SemaphoreType.DMA((N_DEV - 1,))],
    )(v)


def _mm_call(name, mode, a, b, a_spec, b_spec, out_shape, out_spec, grid, acc_shape, scale=1.0, resid=None, resid_spec=None, rider=None):
    nk = grid[2]
    dims = _DIMS[mode]
    has_resid = resid is not None

    def body(*refs):
        a_ref, b_ref = refs[:2]
        r_ref = refs[2] if has_resid else None
        o_ref = refs[3] if has_resid else refs[2]

        def finish(r):
            if scale != 1.0:
                r = r * scale
            if has_resid:
                r = r_ref[...].astype(F32) + r
            o_ref[...] = r.astype(o_ref.dtype)

        part = lax.dot_general(a_ref[...].astype(BF16), b_ref[...].astype(BF16), dims, preferred_element_type=F32)
        if nk == 1:
            finish(part)
            return
        acc_ref = refs[-1]
        k = pl.program_id(2)

        @pl.when(k == 0)
        def _():
            acc_ref[...] = part

        @pl.when(k > 0)
        def _():
            acc_ref[...] += part

        @pl.when(k == nk - 1)
        def _():
            finish(acc_ref[...])

    in_specs = [a_spec, b_spec] + ([resid_spec] if has_resid else [])
    args = (a, b) + ((resid,) if has_resid else ())
    (out,), rid = _call(name, body, grid, in_specs, args, [out_spec], [out_shape], [pltpu.VMEM(acc_shape, F32)] if nk > 1 else [],
                        ("parallel", "parallel", "arbitrary"), rider)
    return out, rid


MM_VMEM_BUDGET = 40 * 1024 * 1024
_TILE_OPTS = (2048, 1408, 1024, 512, 256, 128)


def _tiles(m, n, kd, a_item, b_item, o_item, r_item=0, tm=None, tn=None, tk=None):
    def opts(full, fixed, cap):
        return [fixed] if fixed else [t for t in _TILE_OPTS if t <= cap and full % t == 0] or [full]

    best = None
    for cm in opts(m, tm, 1408):
        for cn in opts(n, tn, 1408):
            for ck in opts(kd, tk, 2048):
                blocks = cm * ck * a_item + ck * cn * b_item + cm * cn * (o_item + r_item)
                casts = (cm * ck * 2 if a_item == 4 else 0) + (ck * cn * 2 if b_item == 4 else 0)
                if 2 * blocks + cm * cn * 4 + casts <= MM_VMEM_BUDGET:
                    key = (cm * cn * ck, ck)
                    if best is None or key > best[0]:
                        best = (key, (cm, cn, ck))
    assert best is not None, (m, n, kd)
    return best[1]


def _item(x):
    return jnp.dtype(x.dtype).itemsize


def mm_nn(name, a, b, *, out_dtype=F32, scale=1.0, resid=None, rider=None):
    m, kd = a.shape
    n = b.shape[1]
    tm, tn, tk = _tiles(m, n, kd, _item(a), _item(b), jnp.dtype(out_dtype).itemsize, 0 if resid is None else _item(resid))
    o_spec = pl.BlockSpec((tm, tn), lambda i, j, k: (i, j))
    return _mm_call(
        name, "nn", a, b, pl.BlockSpec((tm, tk), lambda i, j, k: (i, k)), pl.BlockSpec((tk, tn), lambda i, j, k: (k, j)),
        jax.ShapeDtypeStruct((m, n), out_dtype), o_spec, (m // tm, n // tn, kd // tk), (tm, tn), scale, resid, o_spec, rider)


def mm_nt(name, a, b, *, out_dtype=F32, scale=1.0, resid=None, rider=None):
    m, kd = a.shape
    n = b.shape[0]
    tm, tn, tk = _tiles(m, n, kd, _item(a), _item(b), jnp.dtype(out_dtype).itemsize, 0 if resid is None else _item(resid))
    o_spec = pl.BlockSpec((tm, tn), lambda i, j, k: (i, j))
    return _mm_call(
        name, "nt", a, b, pl.BlockSpec((tm, tk), lambda i, j, k: (i, k)), pl.BlockSpec((tn, tk), lambda i, j, k: (j, k)),
        jax.ShapeDtypeStruct((m, n), out_dtype), o_spec, (m // tm, n // tn, kd // tk), (tm, tn), scale, resid, o_spec, rider)


def mm_tn(name, a, b, *, out_dtype=F32, scale=1.0, rider=None):
    kd, m = a.shape
    n = b.shape[1]
    tm, tn, tk = _tiles(m, n, kd, _item(a), _item(b), jnp.dtype(out_dtype).itemsize)
    return _mm_call(
        name, "tn", a, b, pl.BlockSpec((tk, tm), lambda i, j, k: (k, i)), pl.BlockSpec((tk, tn), lambda i, j, k: (k, j)),
        jax.ShapeDtypeStruct((m, n), out_dtype), pl.BlockSpec((tm, tn), lambda i, j, k: (i, j)),
        (m // tm, n // tn, kd // tk), (tm, tn), scale, rider=rider)


def mm_nt_sharded(name, a, w, *, resid=None, rider=None):
    m = a.shape[0]
    ns, n, c = w.shape
    tm, tn, _ = _tiles(m, n, c, _item(a), _item(w), 4, 0 if resid is None else _item(resid), tk=c)
    o_spec = pl.BlockSpec((tm, tn), lambda i, j, k: (i, j))
    return _mm_call(
        name, "nt", a, w, pl.BlockSpec((tm, c), lambda i, j, k: (i, k)), pl.BlockSpec((None, tn, c), lambda i, j, k: (k, j, 0)),
        jax.ShapeDtypeStruct((m, n), F32), o_spec, (m // tm, n // tn, ns), (tm, tn), 1.0, resid, o_spec, rider)


def mm_tn_sharded(name, a, b, ns, *, rider=None):
    kd, m = a.shape
    c = b.shape[1] // ns
    tm, _, tk = _tiles(m, c, kd, _item(a), _item(b), 2, tn=c)
    return _mm_call(
        name, "tn", a, b, pl.BlockSpec((tk, tm), lambda i, j, k: (k, i)), pl.BlockSpec((tk, c), lambda i, j, k: (k, j)),
        jax.ShapeDtypeStruct((ns, m, c), BF16), pl.BlockSpec((None, tm, c), lambda i, j, k: (j, i, 0)),
        (m // tm, ns, kd // tk), (tm, c), rider=rider)


def rms_fwd(name, x, g, out_dtype, rider=None):
    r, c = x.shape
    tm = _pick(r, (512, 256, 128, 64, 8))

    def body(x_ref, g_ref, y_ref, r_ref):
        xf = x_ref[...].astype(F32)
        rstd = lax.rsqrt(jnp.mean(xf * xf, axis=-1, keepdims=True) + EPS)
        y_ref[...] = ((xf * rstd) * g_ref[...]).astype(y_ref.dtype)
        r_ref[...] = rstd

    (y, rstd), rid = _call(
        name, body, (r // tm,), [pl.BlockSpec((tm, c), lambda i: (i, 0)), pl.BlockSpec((1, c), lambda i: (0, 0))], (x, g.reshape(1, c)),
        [pl.BlockSpec((tm, c), lambda i: (i, 0)), pl.BlockSpec((tm, 1), lambda i: (i, 0))],
        [jax.ShapeDtypeStruct((r, c), out_dtype), jax.ShapeDtypeStruct((r, 1), F32)], (), ("parallel",), rider)
    return (y, rstd) if rider is None else (y, rstd, rid)


def rms_bwd(name, x, g, rstd, dy, dres=None, rider=None):
    r, c = x.shape
    tm = _pick(r, (512, 256, 128, 64, 8))
    has_res = dres is not None

    def body(*refs):
        if has_res:
            x_ref, g_ref, r_ref, dy_ref, dres_ref, dx_ref, dg_ref = refs
        else:
            x_ref, g_ref, r_ref, dy_ref, dx_ref, dg_ref = refs
        xhat = x_ref[...].astype(F32) * r_ref[...]
        dyf = dy_ref[...].astype(F32)
        gdy = dyf * g_ref[...]
        dx = r_ref[...] * (gdy - xhat * jnp.mean(gdy * xhat, axis=-1, keepdims=True))
        if has_res:
            dx = dx + dres_ref[...]
        dx_ref[...] = dx

        @pl.when(pl.program_id(0) == 0)
        def _():
            dg_ref[...] = jnp.zeros_like(dg_ref)

        dg_ref[...] += jnp.sum(dyf * xhat, axis=0, keepdims=True)

    row = pl.BlockSpec((tm, c), lambda i: (i, 0))
    in_specs = [row, pl.BlockSpec((1, c), lambda i: (0, 0)), pl.BlockSpec((tm, 1), lambda i: (i, 0)), row] + ([row] if has_res else [])
    args = (x, g.reshape(1, c), rstd, dy) + ((dres,) if has_res else ())
    (dx, dg), rid = _call(name, body, (r // tm,), in_specs, args, [row, pl.BlockSpec((1, c), lambda i: (0, 0))],
                          [jax.ShapeDtypeStruct((r, c), F32), jax.ShapeDtypeStruct((1, c), F32)], (), ("arbitrary",), rider)
    return (dx, dg.reshape(c)) if rider is None else (dx, dg.reshape(c), rid)


_LANES = 128


def _head_mean(v):
    if v.shape[1] == HEAD_DIM:
        return jnp.mean(v, axis=-1, keepdims=True)
    low = lax.broadcasted_iota(jnp.int32, v.shape, 1) < HEAD_DIM
    lo = jnp.sum(jnp.where(low, v, 0.0), axis=-1, keepdims=True)
    hi = jnp.sum(jnp.where(low, 0.0, v), axis=-1, keepdims=True)
    return jnp.where(low, lo, hi) * (1.0 / HEAD_DIM)


def _head_groups(c):
    width = _LANES if c % _LANES == 0 else HEAD_DIM
    assert c % width == 0, c
    return width, [slice(k * width, (k + 1) * width) for k in range(c // width)]


def _head_gain(g, width):
    return jnp.tile(g.reshape(1, HEAD_DIM), (1, width // HEAD_DIM))


def _rotate_half(y):
    half = HEAD_DIM // 2
    first = lax.broadcasted_iota(jnp.int32, y.shape, 1) % HEAD_DIM < half
    return jnp.where(first, -pltpu.roll(y, y.shape[1] - half, axis=1), pltpu.roll(y, half, axis=1))


def _rope_tables(rope, width):
    return [jnp.tile(t, (1, 2 * width // HEAD_DIM)) for t in rope]


def head_rms_fwd(name, x, g, rope=None):
    s, c = x.shape
    tm = _pick(s, (256, 128, 8))
    width, groups = _head_groups(c)

    def body(x_ref, g_ref, *refs):
        y_ref = refs[-1]
        for sl in groups:
            xs = x_ref[:, sl]
            y = (xs * lax.rsqrt(_head_mean(xs * xs) + EPS)) * g_ref[...]
            if rope:
                y = y * refs[0][...] + _rotate_half(y) * refs[1][...]
            y_ref[:, sl] = y

    row = pl.BlockSpec((tm, c), lambda i: (i, 0))
    tab = pl.BlockSpec((tm, width), lambda i: (i, 0))
    tables = _rope_tables(rope, width) if rope else []
    (y,), _ = _call(name, body, (s // tm,), [row, pl.BlockSpec((1, width), lambda i: (0, 0))] + [tab] * len(tables),
                    (x, _head_gain(g, width), *tables), [row], [jax.ShapeDtypeStruct((s, c), F32)], (), ("parallel",))
    return y


def head_rms_bwd(name, x, g, dy, rope=None):
    s, c = x.shape
    tm = _pick(s, (256, 128, 8))
    width, groups = _head_groups(c)

    def body(x_ref, g_ref, dy_ref, *refs):
        dx_ref, dg_ref = refs[-2:]

        @pl.when(pl.program_id(0) == 0)
        def _():
            dg_ref[...] = jnp.zeros_like(dg_ref)

        for sl in groups:
            xs, dys = x_ref[:, sl], dy_ref[:, sl]
            if rope:
                dys = dys * refs[0][...] - _rotate_half(dys * refs[1][...])
            rstd = lax.rsqrt(_head_mean(xs * xs) + EPS)
            xhat = xs * rstd
            gdy = dys * g_ref[...]
            dx_ref[:, sl] = rstd * (gdy - xhat * _head_mean(gdy * xhat))
            dg_ref[...] += jnp.sum(dys * xhat, axis=0, keepdims=True)

    row = pl.BlockSpec((tm, c), lambda i: (i, 0))
    vec = pl.BlockSpec((1, width), lambda i: (0, 0))
    tab = pl.BlockSpec((tm, width), lambda i: (i, 0))
    tables = _rope_tables(rope, width) if rope else []
    (dx, dg), _ = _call(name, body, (s // tm,), [row, vec, row] + [tab] * len(tables), (x, _head_gain(g, width), dy, *tables), [row, vec],
                        [jax.ShapeDtypeStruct((s, c), F32), jax.ShapeDtypeStruct((1, width), F32)], (), ("arbitrary",))
    return dx, jnp.sum(dg.reshape(width // HEAD_DIM, HEAD_DIM), axis=0)


@functools.partial(jax.custom_vjp, nondiff_argnums=(0,))
def head_rms(name, x, g):
    return head_rms_fwd(name + "_fwd", x, g)


def _head_rms_fwd(name, x, g):
    return head_rms_fwd(name + "_fwd", x, g), (x, g)


def _head_rms_bwd(name, res, dy):
    return head_rms_bwd(name + "_bwd", *res, dy)


head_rms.defvjp(_head_rms_fwd, _head_rms_bwd)


@functools.partial(jax.custom_vjp, nondiff_argnums=(0,))
def head_rms_rope(name, x, g, cos, sin):
    return head_rms_fwd(name + "_fwd", x, g, (cos, sin))


def _head_rms_rope_fwd(name, x, g, cos, sin):
    return head_rms_fwd(name + "_fwd", x, g, (cos, sin)), (x, g, cos, sin)


def _head_rms_rope_bwd(name, res, dy):
    x, g, cos, sin = res
    return (*head_rms_bwd(name + "_bwd", x, g, dy, (cos, sin)), jnp.zeros_like(cos), jnp.zeros_like(sin))


head_rms_rope.defvjp(_head_rms_rope_fwd, _head_rms_rope_bwd)


FFN_TM = 512


def _sigmoid(x):
    return 1.0 / (1.0 + jnp.exp(-x))


def ffn_gu(name, xn, wg, wu, rider=None):
    s, d = xn.shape
    ns, _, c = wg.shape
    tm = _pick(s, (FFN_TM, 128))

    def body(x_ref, wg_ref, wu_ref, h_ref, a_ref, b_ref):
        xb = x_ref[...]
        gv = jnp.dot(xb, wg_ref[...], preferred_element_type=F32)
        uv = jnp.dot(xb, wu_ref[...], preferred_element_type=F32)
        sig = _sigmoid(gv)
        silu = gv * sig
        h_ref[...] = (silu * uv).astype(BF16)
        a_ref[...] = (uv * (sig * (1.0 + gv * (1.0 - sig)))).astype(BF16)
        b_ref[...] = silu.astype(BF16)

    w_spec = pl.BlockSpec((None, d, c), lambda j, i: (j, 0, 0))
    o_spec = pl.BlockSpec((tm, c), lambda j, i: (i, j))
    return _call(
        name, body, (ns, s // tm), [pl.BlockSpec((tm, d), lambda j, i: (i, 0)), w_spec, w_spec], (xn, wg, wu),
        [o_spec, o_spec, o_spec], [jax.ShapeDtypeStruct((s, ns * c), BF16)] * 3, [], ("parallel", "parallel"), rider)


def ffn_dh(name, dy, wd, dh_dg, dh_du, ns, scale, rider=None):
    s, d = dy.shape
    f = wd.shape[0]
    c = f // ns
    tm = _pick(s, (FFN_TM, 128))

    def body(dy_ref, wd_ref, a_ref, b_ref, dg_ref, du_ref):
        dh = lax.dot_general(dy_ref[...].astype(BF16), wd_ref[...], _DIMS["nt"], preferred_element_type=F32) * scale
        dg_ref[...] = (dh * a_ref[...].astype(F32)).astype(BF16)
        du_ref[...] = (dh * b_ref[...].astype(F32)).astype(BF16)

    o_spec = pl.BlockSpec((tm, c), lambda j, i: (i, j))
    return _call(
        name, body, (ns, s // tm),
        [pl.BlockSpec((tm, d), lambda j, i: (i, 0)), pl.BlockSpec((c, d), lambda j, i: (j, 0)), o_spec, o_spec], (dy, wd, dh_dg, dh_du),
        [o_spec, o_spec], [jax.ShapeDtypeStruct((s, f), BF16), jax.ShapeDtypeStruct((s, f), BF16)],
        [], ("parallel", "parallel"), rider)


FOX_TQ = 512


def fox_tile(s_len):
    return min(FOX_TQ, s_len)


def _heads_per_block(h):
    return 2 if h % 2 == 0 else 1


def _fox_queries(q):
    return (q * (HEAD_DIM ** -0.5)).astype(BF16)


def _fox_scores(qs, kc, cq, ck, diagonal):
    s = lax.dot_general(qs, kc.astype(BF16), _DIMS["nt"], preferred_element_type=F32) + cq - ck
    if not diagonal:
        return s
    return jnp.where(lax.broadcasted_iota(jnp.int32, s.shape, 0) >= lax.broadcasted_iota(jnp.int32, s.shape, 1), s, MASK_VALUE)


def _fox_specs(h, s_len, tq):
    hb = _heads_per_block(h)
    qb = pl.BlockSpec((tq, hb * HEAD_DIM), lambda pp, i: (i, pp))
    kb = pl.BlockSpec((s_len, hb * HEAD_DIM), lambda pp, i: (0, pp))
    colb = pl.BlockSpec((hb, tq, 1), lambda pp, i: (pp, i, 0))
    rowb = pl.BlockSpec((hb, s_len // tq, 1, tq), lambda pp, i: (pp, 0, 0, 0))
    return hb, qb, kb, colb, rowb


def fox_fwd(q, k, v, cq, ck, rider=None):
    s_len, hd = q.shape
    h, d = hd // HEAD_DIM, HEAD_DIM
    tq = fox_tile(s_len)
    hb, qb, kb, colb, rowb = _fox_specs(h, s_len, tq)

    def body(q_ref, k_ref, v_ref, cq_ref, ck_ref, o_ref, lse_ref):
        i = pl.program_id(1)
        for hh in range(hb):
            lanes = slice(hh * d, (hh + 1) * d)
            qs, cqv = _fox_queries(q_ref[:, lanes]), cq_ref[hh]

            def chunk(c, carry, diagonal=False):
                m, l, acc = carry
                rows = pl.ds(pl.multiple_of(c * tq, tq), tq)
                s = _fox_scores(qs, k_ref[rows, lanes], cqv, ck_ref[hh, c], diagonal)
                m_new = jnp.maximum(m, jnp.max(s, axis=-1, keepdims=True))
                alpha = jnp.exp(m - m_new)
                p = jnp.exp(s - m_new)
                acc = alpha * acc + jnp.dot(p.astype(BF16), v_ref[rows, lanes].astype(BF16), preferred_element_type=F32)
                return m_new, alpha * l + jnp.sum(p, axis=-1, keepdims=True), acc

            init = (jnp.full((tq, 1), MASK_VALUE, F32), jnp.zeros((tq, 1), F32), jnp.zeros((tq, d), F32))
            m, l, acc = chunk(i, lax.fori_loop(0, i, chunk, init), diagonal=True)
            o_ref[:, lanes] = acc / l
            lse_ref[hh] = m + jnp.log(l)

    return _call(
        "fox_fwd", body, (h // hb, s_len // tq), [qb, kb, kb, colb, rowb], (q, k, v, cq, ck), [qb, colb],
        [jax.ShapeDtypeStruct((s_len, hd), F32), jax.ShapeDtypeStruct((h, s_len, 1), F32)], (), ("parallel", "parallel"), rider)


def fox_bwd(q, k, v, cq, ck, o, lse, do, rider=None):
    s_len, hd = q.shape
    h, d = hd // HEAD_DIM, HEAD_DIM
    tq = fox_tile(s_len)
    scale = HEAD_DIM ** -0.5
    hb, qb, kb, colb, rowb = _fox_specs(h, s_len, tq)

    def body(q_ref, k_ref, v_ref, cq_ref, ck_ref, o_ref, lse_ref, do_ref, dq_ref, dk_ref, dv_ref, dcq_ref, dck_ref):
        i = pl.program_id(1)

        @pl.when(i == 0)
        def _():
            dk_ref[...] = jnp.zeros_like(dk_ref)
            dv_ref[...] = jnp.zeros_like(dv_ref)
            dck_ref[...] = jnp.zeros_like(dck_ref)

        heads = []
        for hh in range(hb):
            lanes = slice(hh * d, (hh + 1) * d)
            dof = do_ref[:, lanes]
            heads.append((lanes, _fox_queries(q_ref[:, lanes]), cq_ref[hh], lse_ref[hh], dof.astype(BF16),
                          jnp.sum(dof * o_ref[:, lanes], axis=-1, keepdims=True)))

        def chunk(c, carry, diagonal=False):
            rows = pl.ds(pl.multiple_of(c * tq, tq), tq)
            out, dks, dvs = [], [], []
            for hh, (lanes, qs, cqv, lse_h, dob, delta) in enumerate(heads):
                dq, dcq = carry[hh]
                kc = k_ref[rows, lanes]
                p = jnp.exp(_fox_scores(qs, kc, cqv, ck_ref[hh, c], diagonal) - lse_h)
                dp = lax.dot_general(dob, v_ref[rows, lanes].astype(BF16), _DIMS["nt"], preferred_element_type=F32)
                ds = p * (dp - delta)
                dsb = ds.astype(BF16)
                dvs.append(lax.dot_general(p.astype(BF16), dob, _DIMS["tn"], preferred_element_type=F32))
                dks.append(lax.dot_general(dsb, qs, _DIMS["tn"], preferred_element_type=F32))
                dck_ref[hh, c] -= jnp.sum(ds, axis=0, keepdims=True)
                out.append((dq + jnp.dot(dsb, kc.astype(BF16), preferred_element_type=F32), dcq + jnp.sum(ds, axis=-1, keepdims=True)))
            dk_ref[rows, :] += jnp.concatenate(dks, axis=1)
            dv_ref[rows, :] += jnp.concatenate(dvs, axis=1)
            return tuple(out)

        init = tuple((jnp.zeros((tq, d), F32), jnp.zeros((tq, 1), F32)) for _ in range(hb))
        done = chunk(i, lax.fori_loop(0, i, chunk, init), diagonal=True)
        dq_ref[...] = jnp.concatenate([dq for dq, _ in done], axis=1) * scale
        for hh, (_, dcq) in enumerate(done):
            dcq_ref[hh] = dcq

    return _call(
        "fox_bwd", body, (h // hb, s_len // tq), [qb, kb, kb, colb, rowb, qb, colb, qb], (q, k, v, cq, ck, o, lse, do),
        [qb, kb, kb, colb, rowb],
        [jax.ShapeDtypeStruct((s_len, hd), F32)] * 3
        + [jax.ShapeDtypeStruct((h, s_len, 1), F32), jax.ShapeDtypeStruct((h, s_len // tq, 1, tq), F32)],
        (), ("parallel", "arbitrary"), rider)


def _stack_heads(ref, first, g):
    return jnp.concatenate([ref[:, (first + j) * HEAD_DIM:(first + j + 1) * HEAD_DIM] for j in range(g)], axis=0)


def _window(prev_ref, cur_ref, hh):
    lanes = slice(hh * HEAD_DIM, (hh + 1) * HEAD_DIM)
    return jnp.concatenate([prev_ref[:, lanes], cur_ref[:, lanes]], axis=0).astype(BF16)


def _swa_band(g, w):
    t = lax.broadcasted_iota(jnp.int32, (g * w, 2 * w), 0) % w
    col = lax.broadcasted_iota(jnp.int32, (g * w, 2 * w), 1)
    rel = t + w - col
    band = (rel >= 0) & (rel < w)
    return jnp.where(jnp.stack([band & (col >= w), band]), 0.0, MASK_VALUE).astype(F32)


def _swa_probs(qs, kw, sink, band):
    s = lax.dot_general(qs, kw, _DIMS["nt"], preferred_element_type=F32) + band
    m = jnp.maximum(jnp.max(s, axis=-1, keepdims=True), sink)
    p = jnp.exp(s - m)
    ps = jnp.exp(sink - m)
    linv = 1.0 / (jnp.sum(p, axis=-1, keepdims=True) + ps)
    return p * linv, ps * linv


def _swa_specs(hk, g, s_len):
    w = WINDOW
    assert s_len % w == 0
    hb = _heads_per_block(hk)
    qb = pl.BlockSpec((w, hb * g * HEAD_DIM), lambda pp, n: (n, pp))
    prev = pl.BlockSpec((w, hb * HEAD_DIM), lambda pp, n: (jnp.maximum(n - 1, 0), pp))
    cur = pl.BlockSpec((w, hb * HEAD_DIM), lambda pp, n: (n, pp))
    sb = pl.BlockSpec((hb, g * w, 1), lambda pp, n: (pp, 0, 0))
    band = pl.BlockSpec((None, g * w, 2 * w), lambda pp, n: (jnp.minimum(n, 1), 0, 0))
    return hb, qb, prev, cur, sb, band


def swa_fwd(q, k, v, sink, rider=None):
    s_len = q.shape[0]
    hk = k.shape[1] // HEAD_DIM
    g = q.shape[1] // k.shape[1]
    w, d = WINDOW, HEAD_DIM
    hb, qb, prev, cur, sb, bandb = _swa_specs(hk, g, s_len)

    def body(q_ref, kp_ref, kc_ref, vp_ref, vc_ref, sink_ref, band_ref, o_ref):
        for hh in range(hb):
            qs = (_stack_heads(q_ref, hh * g, g) * (HEAD_DIM ** -0.5)).astype(BF16)
            p, _ = _swa_probs(qs, _window(kp_ref, kc_ref, hh), sink_ref[hh], band_ref[...])
            o = jnp.dot(p.astype(BF16), _window(vp_ref, vc_ref, hh), preferred_element_type=F32)
            for j in range(g):
                o_ref[:, (hh * g + j) * d:(hh * g + j + 1) * d] = o[j * w:(j + 1) * w]

    (o,), rid = _call("swa_fwd", body, (hk // hb, s_len // w), [qb, prev, cur, prev, cur, sb, bandb],
                      (q, k, k, v, v, sink, _swa_band(g, w)), [qb], [jax.ShapeDtypeStruct(q.shape, F32)], (),
                      ("parallel", "parallel"), rider)
    return o, rid


def swa_bwd(q, k, v, sink, o, do, rider=None):
    s_len = q.shape[0]
    hk = k.shape[1] // HEAD_DIM
    g = q.shape[1] // k.shape[1]
    w, d = WINDOW, HEAD_DIM
    scale = HEAD_DIM ** -0.5
    hb, qb, prev, cur, sb, bandb = _swa_specs(hk, g, s_len)

    def body(q_ref, kp_ref, kc_ref, vp_ref, vc_ref, sink_ref, band_ref, o_ref, do_ref, dq_ref, dkp_ref, dkc_ref, dvp_ref, dvc_ref,
             dsink_ref):
        @pl.when(pl.program_id(1) == 0)
        def _():
            dsink_ref[...] = jnp.zeros_like(dsink_ref)

        for hh in range(hb):
            lanes = slice(hh * d, (hh + 1) * d)
            qs = (_stack_heads(q_ref, hh * g, g) * scale).astype(BF16)
            kw, vw = _window(kp_ref, kc_ref, hh), _window(vp_ref, vc_ref, hh)
            p, ps = _swa_probs(qs, kw, sink_ref[hh], band_ref[...])
            dof = _stack_heads(do_ref, hh * g, g)
            dob = dof.astype(BF16)
            delta = jnp.sum(dof * _stack_heads(o_ref, hh * g, g), axis=-1, keepdims=True)
            dp = lax.dot_general(dob, vw, _DIMS["nt"], preferred_element_type=F32)
            ds = p * (dp - delta)
            dsb = ds.astype(BF16)
            dsink_ref[hh] -= ps * delta
            dq = jnp.dot(dsb, kw, preferred_element_type=F32) * scale
            for j in range(g):
                dq_ref[:, (hh * g + j) * d:(hh * g + j + 1) * d] = dq[j * w:(j + 1) * w]
            dkw = lax.dot_general(dsb, qs, _DIMS["tn"], preferred_element_type=F32)
            dvw = lax.dot_general(p.astype(BF16), dob, _DIMS["tn"], preferred_element_type=F32)
            dkp_ref[:, lanes] = dkw[:w]
            dkc_ref[:, lanes] = dkw[w:]
            dvp_ref[:, lanes] = dvw[:w]
            dvc_ref[:, lanes] = dvw[w:]

    kv_shape = jax.ShapeDtypeStruct(k.shape, F32)
    (dq, dkp, dkc, dvp, dvc, dsink), rid = _call(
        "swa_bwd", body, (hk // hb, s_len // w), [qb, prev, cur, prev, cur, sb, bandb, qb, qb],
        (q, k, k, v, v, sink, _swa_band(g, w), o, do),
        [qb, cur, cur, cur, cur, sb],
        [jax.ShapeDtypeStruct(q.shape, F32), kv_shape, kv_shape, kv_shape, kv_shape, jax.ShapeDtypeStruct((hk, g * w, 1), F32)],
        (), ("parallel", "arbitrary"), rider)

    def shift_up(a):
        return jnp.concatenate([a[w:], jnp.zeros_like(a[:w])], axis=0)

    return (dq, dkc + shift_up(dkp), dvc + shift_up(dvp), dsink), rid


def loss_call(y, target):
    s, d = y.shape
    tm = _pick(s, (512, 256, 128))

    def body(y_ref, t_ref, l_ref, dy_ref):
        e = y_ref[...] - t_ref[...]
        dy_ref[...] = e * (1.0 / d)

        @pl.when(pl.program_id(0) == 0)
        def _():
            l_ref[...] = jnp.zeros_like(l_ref)

        l_ref[...] += jnp.sum(jnp.sum(e * e, axis=0, keepdims=True), axis=1, keepdims=True) * (0.5 / d)

    row = pl.BlockSpec((tm, d), lambda i: (i, 0))
    (l, dy), _ = _call("loss_head", body, (s // tm,), [row, row], (y, target), [pl.BlockSpec((1, 1), lambda i: (0, 0)), row],
                       [jax.ShapeDtypeStruct((1, 1), F32), jax.ShapeDtypeStruct((s, d), F32)], (), ("arbitrary",))
    return l[0, 0], dy


def _row_tile(rows, cols, itemsize, block_bytes=1 << 20):
    target = max(16, block_bytes // (cols * itemsize))
    fits = [t for t in range(16, rows + 1, 16) if rows % t == 0 and t <= target]
    return fits[-1] if fits else rows


CAST_STEPS = 8


def cast_place(name, ws, p_idx, rider=None):
    n = len(ws)
    assert all(w.shape[0] % (16 * CAST_STEPS) == 0 for w in ws), [w.shape for w in ws]

    def body(p_ref, *refs):
        for w_ref, o_ref in zip(refs[:n], refs[n:]):
            o_ref[...] = w_ref[...].astype(BF16)

    return _call(
        name, body, (CAST_STEPS,), [pl.BlockSpec((w.shape[0] // CAST_STEPS, w.shape[1]), lambda i, pr: (i, 0)) for w in ws], tuple(ws),
        [pl.BlockSpec((None, w.shape[0] // CAST_STEPS, w.shape[1]), lambda i, pr: (pr[0], i, 0)) for w in ws],
        [jax.ShapeDtypeStruct((N_CHIPS,) + w.shape, BF16) for w in ws], (), ("parallel",), rider, prefetch=(p_idx,))


def chip_sum(name, grad, theirs, c_idx):
    ns, r, cols = grad.shape
    rh = r // 2
    tr = _row_tile(rh, cols, 2, 2 << 20)
    nb = rh // tr

    def body(c_ref, a_ref, b_ref, o_ref):
        o_ref[...] = (a_ref[...].astype(F32) + b_ref[...].astype(F32)).astype(o_ref.dtype)

    return pl.pallas_call(
        body, name=name,
        grid_spec=pltpu.PrefetchScalarGridSpec(
            num_scalar_prefetch=1, grid=(ns, nb),
            in_specs=[pl.BlockSpec((None, tr, cols), lambda q, i, cr: (q, cr[0] * nb + i, 0)),
                      pl.BlockSpec((None, tr, cols), lambda q, i, cr: (q, i, 0))],
            out_specs=pl.BlockSpec((None, tr, cols), lambda q, i, cr: (q, i, 0))),
        out_shape=jax.ShapeDtypeStruct((ns, rh, cols), BF16),
        compiler_params=pltpu.CompilerParams(dimension_semantics=("parallel", "parallel"), vmem_limit_bytes=VMEM_LIMIT),
    )(c_idx, grad, theirs)


def owner_sum(name, sums, got, pc_idx):
    ns, rh, cols = sums.shape
    tr = _row_tile(rh, cols, 4, 2 << 20)
    nb = rh // tr

    def body(pc_ref, a_ref, b_ref, o_ref):
        o_ref[...] = ((a_ref[...].astype(F32) + b_ref[0].astype(F32)) + b_ref[1].astype(F32)) + b_ref[2].astype(F32)

    return pl.pallas_call(
        body, name=name,
        grid_spec=pltpu.PrefetchScalarGridSpec(
            num_scalar_prefetch=1, grid=(nb,),
            in_specs=[pl.BlockSpec((None, tr, cols), lambda i, pc: (pc[0], i, 0)),
                      pl.BlockSpec((3, tr, cols), lambda i, pc: (0, i, 0))],
            out_specs=pl.BlockSpec((tr, cols), lambda i, pc: (pc[1] * nb + i, 0))),
        out_shape=jax.ShapeDtypeStruct((2 * rh, cols), F32),
        compiler_params=pltpu.CompilerParams(dimension_semantics=("parallel",), vmem_limit_bytes=VMEM_LIMIT),
    )(pc_idx, sums, got)


def adamw(name, w, g, m, v):
    r, cols = w.shape
    tr = _row_tile(r, cols, 4)
    c1 = 1.0 / (1.0 - ADAM_B1 ** ADAM_STEP)
    c2 = 1.0 / (1.0 - ADAM_B2 ** ADAM_STEP)

    def body(w_ref, g_ref, m_ref, v_ref, go_ref, d_ref, nm_ref, nv_ref):
        gv = g_ref[...]
        nm = ADAM_B1 * m_ref[...] + (1.0 - ADAM_B1) * gv
        nv = ADAM_B2 * v_ref[...] + (1.0 - ADAM_B2) * (gv * gv)
        go_ref[...] = gv
        d_ref[...] = -ADAM_LR * ((nm * c1) / (jnp.sqrt(nv * c2) + ADAM_EPS) + ADAM_WD * w_ref[...])
        nm_ref[...] = nm
        nv_ref[...] = nv

    blk = pl.BlockSpec((tr, cols), lambda i: (i, 0))
    return _call(name, body, (r // tr,), [blk] * 4, (w, g, m, v), [blk] * 4, [jax.ShapeDtypeStruct((r, cols), F32)] * 4, (), ("parallel",))


def _win_layout(d_model):
    hf = hq = d_model // (2 * HEAD_DIM)
    hk = hq // 4
    sizes = [hf * HEAD_DIM, hf * HEAD_DIM, hf * HEAD_DIM, hf, hq * HEAD_DIM, hk * HEAD_DIM, hk * HEAD_DIM]
    return hf, hq, hk, sizes


class WinPlan:
    def __init__(self, d_model, ns=N_CHIPS):
        self.hf, self.hq, self.hk, self.sizes = _win_layout(d_model)
        self.ns, self.cs = ns, sum(self.sizes) // ns
        self.jump_at = sum(self.sizes[:4])
        self.jump_by = -self.jump_at % _LANES
        self.base = [self.pos(s * self.cs) // _LANES * _LANES for s in range(ns)]
        ends = [self.pos((s + 1) * self.cs - 1) + 1 - self.base[s] for s in range(ns)]
        self.width = -(-max(ends) // _LANES) * _LANES
        self.total = -(-max(b + self.width for b in self.base) // 1024) * 1024
        starts = [0]
        for sz in self.sizes:
            starts.append(starts[-1] + sz)
        self.segments = [(self.pos(a), sz) for a, sz in zip(starts, self.sizes)]

    def pos(self, g):
        return g if g < self.jump_at else g + self.jump_by

    def pieces(self, s):
        g0, g1 = s * self.cs, (s + 1) * self.cs
        cuts = [g0] + ([self.jump_at] if g0 < self.jump_at < g1 else []) + [g1]
        return [(a - g0, b - a, self.pos(a) - self.base[s]) for a, b in zip(cuts[:-1], cuts[1:])]

    def place(self, w, s):
        parts, at = [], 0
        for t0, n, j0 in self.pieces(s):
            parts += [jnp.zeros((w.shape[0], j0 - at), w.dtype), w[:, t0:t0 + n]]
            at = j0 + n
        return jnp.concatenate(parts + [jnp.zeros((w.shape[0], self.width - at), w.dtype)], axis=1)

    def unplace(self, slab, s):
        return jnp.concatenate([slab[:, j0:j0 + n] for _, n, j0 in self.pieces(s)], axis=1)

    def assemble(self, slabs):
        return sum(jnp.pad(slabs[s], ((0, 0), (b, self.total - b - self.width))) for s, b in enumerate(self.base))

    def split(self, full):
        return jnp.stack([full[:, b:b + self.width] for b in self.base])


def _attn_inputs(proj, sm, positions):
    s_len = proj.shape[0]
    plan = WinPlan(sm["norm_mix_g"].shape[0])
    hf, hq, hk = plan.hf, plan.hq, plan.hk
    grp = hq // hk
    q_f, k_f, v_f, f_logit, q_s, k_s, v_s = [proj[:, a:a + n] for a, n in plan.segments]

    q_f = head_rms("fox_qnorm", q_f, sm["fox_q_norm_g"])
    k_f = head_rms("fox_knorm", k_f, sm["fox_k_norm_g"])
    log_f = jax.nn.log_sigmoid(f_logit + sm["b_forget"])
    c = jnp.cumsum(log_f, axis=0).T

    inv_freq = ROPE_THETA ** (-jnp.arange(0, HEAD_DIM, 2, dtype=F32) / HEAD_DIM)
    ang = positions.astype(F32)[:, None] * inv_freq
    cos, sin = jnp.cos(ang), jnp.sin(ang)
    q_s = head_rms_rope("swa_qnorm", q_s, sm["swa_q_norm_g"], cos, sin)
    k_s = head_rms_rope("swa_knorm", k_s, sm["swa_k_norm_g"], cos, sin)
    sink = jnp.broadcast_to(sm["swa_sinks"].reshape(hk, grp, 1, 1), (hk, grp, WINDOW, 1)).reshape(hk, grp * WINDOW, 1)
    tq = fox_tile(s_len)
    return (q_f, k_f, v_f, c[:, :, None], c.reshape(hf, s_len // tq, 1, tq)), (q_s, k_s, v_s, sink)


_BIG = ("ffn1_w_gate", "ffn1_w_up", "ffn1_w_down", "w_in", "w_out", "ffn2_w_gate", "ffn2_w_up", "ffn2_w_down")
_ROW_SHARDED = ("ffn1_w_down", "w_out", "ffn2_w_down")
_SMALL = ("norm_ffn1_g", "norm_mix_g", "b_forget", "fox_q_norm_g", "fox_k_norm_g", "swa_q_norm_g", "swa_k_norm_g", "swa_sinks",
          "out_norm_fox_g", "out_norm_swa_g", "norm_ffn2_g")
_ATTN_SMALL = ("norm_mix_g", "b_forget", "fox_q_norm_g", "fox_k_norm_g", "swa_q_norm_g", "swa_k_norm_g", "swa_sinks")
_ALL = ("norm_ffn1_g", "ffn1_w_gate", "ffn1_w_up", "ffn1_w_down", "norm_mix_g", "w_in", "b_forget", "fox_q_norm_g", "fox_k_norm_g",
        "swa_q_norm_g", "swa_k_norm_g", "swa_sinks", "out_norm_fox_g", "out_norm_swa_g", "w_out", "norm_ffn2_g", "ffn2_w_gate",
        "ffn2_w_up", "ffn2_w_down")


def _pack_small(d):
    parts = []
    for k in _SMALL:
        v = d[k].reshape(-1)
        rows = -(-v.shape[0] // _LANES)
        parts.append(jnp.pad(v, (0, rows * _LANES - v.shape[0])).reshape(rows, _LANES))
    a = jnp.concatenate(parts, axis=0)
    return jnp.pad(a, ((0, -a.shape[0] % 8), (0, 0)))


def _unpack_small(a, like):
    out, r0 = {}, 0
    for k in _SMALL:
        nvals = like[k].shape[1]
        rows = -(-nvals // _LANES)
        out[k] = a[r0:r0 + rows].reshape(-1)[:nvals].reshape(1, nvals)
        r0 += rows
    return out


def _stacked(w):
    return w.reshape(-1, w.shape[-1])


def _local_step(shards, sm, x, positions, target, p_idx, c_idx, pc_idx):
    ns = N_CHIPS
    hf, hq, hk, _ = _win_layout(x.shape[1])
    s_len = x.shape[0]
    full = {}

    def fetch(*jobs):
        names = list(dict.fromkeys(n for n, _, _ in jobs))
        return names, gather([bufs[n] for n in names], [(names.index(n), kind, part) for n, kind, part in jobs])

    def take(names, rid):
        for n, b in zip(names, rid[0]):
            bufs[n] = b

    n1 = ["ffn1_w_gate", "ffn1_w_up", "ffn1_w_down"]
    n2 = ["ffn2_w_gate", "ffn2_w_up", "ffn2_w_down"]
    later = ["w_in", "w_out"] + n2
    placed, _ = cast_place("cast_place_ffn1", [shards[n] for n in n1], p_idx)
    bufs = dict(zip(n1, placed))
    gate1, up1, down1 = n1
    gate2, up2, down2 = n2
    names, rider = fetch((gate1, "ici", WHOLE), (up1, "ici", WHOLE))
    placed, rid = cast_place("cast_place_later", [shards[n] for n in later], p_idx, rider=rider)
    bufs.update(zip(later, placed))
    take(names, rid)
    names, rider = fetch((gate1, "d2d", WHOLE), (up1, "d2d", WHOLE))
    xn1, r1, rid = rms_fwd("ffn1_norm", x, sm["norm_ffn1_g"], BF16, rider=rider)
    take(names, rid)
    names, rider = fetch((down1, "ici", WHOLE))
    (hid1, g1, u1), rid = ffn_gu("ffn1_gu", xn1, bufs[gate1], bufs[up1], rider=rider)
    take(names, rid)
    names, rider = fetch((down1, "d2d", WHOLE))
    take(names, run_step("gather_d2d_ffn1_down", rider))
    wd1 = _stacked(bufs[down1])
    names, rider = fetch(("w_in", "ici", WHOLE))
    h1, rid = mm_nn("ffn1_down", hid1, wd1, scale=0.5, resid=x, rider=rider)
    take(names, rid)

    names, rider = fetch(("w_in", "d2d", WHOLE))
    u, r_mix, rid = rms_fwd("mix_norm", h1, sm["norm_mix_g"], BF16, rider=rider)
    take(names, rid)
    names, rider = fetch(("w_out", "ici", WHOLE), (gate2, "ici", (0, 1, 4)))
    plan = WinPlan(x.shape[1])
    win = plan.assemble(bufs["w_in"])
    proj, rid = mm_nn("mix_inproj", u, win, rider=rider)
    take(names, rid)
    sm_attn = {k: sm[k] for k in _ATTN_SMALL}
    (fox_in, swa_in), attn_vjp = jax.vjp(lambda pr, s: _attn_inputs(pr, s, positions), proj, sm_attn)
    names, rider = fetch((gate2, "ici", (1, 4, 4)), ("w_out", "d2d", WHOLE), (gate2, "d2d", (0, 1, 4)))
    (o_f, lse), rid = fox_fwd(*fox_in, rider=rider)
    take(names, rid)
    names, rider = fetch((up2, "ici", (0, 3, 4)), (gate2, "d2d", (1, 4, 4)))
    o_s, rid = swa_fwd(*swa_in, rider=rider)
    take(names, rid)
    o_fox, o_swa = o_f, o_s
    nf, r_fox = rms_fwd("out_norm_fox", o_fox, sm["out_norm_fox_g"], BF16)
    nsw, r_swa = rms_fwd("out_norm_swa", o_swa, sm["out_norm_swa_g"], BF16)
    o = jnp.concatenate([nf, nsw], axis=-1)
    wout = _stacked(bufs["w_out"])
    names, rider = fetch((up2, "ici", (3, 4, 4)), (up2, "d2d", (0, 3, 4)))
    h2, rid = mm_nn("out_proj", o, wout, resid=h1, rider=rider)
    take(names, rid)

    names, rider = fetch((up2, "d2d", (3, 4, 4)))
    xn2, r2, rid = rms_fwd("ffn2_norm", h2, sm["norm_ffn2_g"], BF16, rider=rider)
    take(names, rid)
    names, rider = fetch((down2, "ici", WHOLE))
    (hid2, g2, u2), rid = ffn_gu("ffn2_gu", xn2, bufs[gate2], bufs[up2], rider=rider)
    take(names, rid)
    names, rider = fetch((down2, "d2d", WHOLE))
    take(names, run_step("gather_d2d_ffn2_down", rider))
    wd2 = _stacked(bufs["ffn2_w_down"])
    y, _ = mm_nn("ffn2_down", hid2, wd2, scale=0.5, resid=h2)
    loss, dy = loss_call(y, target)

    red = {}

    def grad(n, g):
        red[n] = {"grad": g.reshape(ns, -1, g.shape[-1])}

    def ride(*steps):
        def done(rid):
            a0 = n0 = 0
            for rd, cb in steps:
                cb(rid[0][a0:a0 + len(rd.aliased)], rid[1][n0:n0 + len(rd.news)])
                a0, n0 = a0 + len(rd.aliased), n0 + len(rd.news)

        return (combine(*[s[0] for s in steps]) if len(steps) > 1 else steps[0][0]), done

    def xchg(*names):
        def cb(al, news):
            for n, t in zip(names, news):
                red[n]["sum"] = chip_sum("chip_sum_" + n, red[n]["grad"], t, c_idx)

        return exchange_halves([red[n]["grad"] for n in names]), cb

    def scat(n, part=WHOLE):
        def cb(al, news):
            red[n]["got"] = (al or news)[0]

        return scatter_to_owner([red[n]["sum"]], [red[n]["got"]] if "got" in red[n] else None, part), cb

    def own(n):
        red[n]["half"] = owner_sum("owner_sum_" + n, red[n]["sum"], red[n]["got"], pc_idx)

    def join(*names):
        return join_halves([red[n]["half"] for n in names]), lambda al, news: full.update(zip(names, al))

    dwd2, _ = mm_tn("ffn2_dwd", hid2, dy, out_dtype=BF16, scale=0.5)
    grad(down2, dwd2)
    rider, done = ride(xchg(down2))
    (dg2, du2), rid = ffn_dh("ffn2_dh", dy, wd2, g2, u2, ns, 0.5, rider=rider)
    done(rid)
    rider, done = ride(scat(down2, (0, 1, 2)))
    dwg2, rid = mm_tn_sharded("ffn2_dwg", xn2, dg2, ns, rider=rider)
    done(rid)
    grad(gate2, dwg2)
    rider, done = ride(scat(down2, (1, 2, 2)), xchg(gate2))
    dwu2, rid = mm_tn_sharded("ffn2_dwu", xn2, du2, ns, rider=rider)
    done(rid)
    grad(up2, dwu2)
    rider, done = ride(scat(gate2, (0, 1, 2)), xchg(up2))
    dxn, rid = mm_nt_sharded("ffn2_dxn_g", dg2, bufs[gate2], rider=rider)
    done(rid)
    rider, done = ride(scat(gate2, (1, 2, 2)))
    dxn, rid = mm_nt_sharded("ffn2_dxn_u", du2, bufs[up2], resid=dxn, rider=rider)
    done(rid)
    dh2, dgain_ffn2 = rms_bwd("ffn2_dnorm", h2, sm["norm_ffn2_g"], r2, dxn, dres=dy)
    own(down2)
    own(gate2)

    do, _ = mm_nt("out_do", dh2, wout)
    dwout, _ = mm_tn("out_dw", o, dh2, out_dtype=BF16)
    cf = o_fox.shape[1]
    d_fox, dgain_fox = rms_bwd("out_dnorm_fox", o_fox, sm["out_norm_fox_g"], r_fox, do[:, :cf])
    d_swa, dgain_swa = rms_bwd("out_dnorm_swa", o_swa, sm["out_norm_swa_g"], r_swa, do[:, cf:])
    grad("w_out", dwout)
    rider, done = ride(scat(up2))
    swa_cts, rid = swa_bwd(*swa_in, o_s, d_swa, rider=rider)
    done(rid)
    own(up2)
    rider, done = ride(xchg("w_out"), join(down2, gate2, up2))
    fox_cts, rid = fox_bwd(*fox_in, o_f, lse, d_fox, rider=rider)
    done(rid)
    dproj, dsm_attn = attn_vjp((tuple(fox_cts), tuple(swa_cts)))

    rider, done = ride(scat("w_out"))
    du, rid = mm_nt("mix_du", dproj, win, rider=rider)
    done(rid)
    dwin, _ = mm_tn("mix_dwin", u, dproj, out_dtype=BF16)
    grad("w_in", plan.split(dwin))
    rider, done = ride(xchg("w_in"))
    dh1, dgain_mix, rid = rms_bwd("mix_dnorm", h1, sm["norm_mix_g"], r_mix, du, dres=dh2, rider=rider)
    done(rid)
    own("w_out")

    rider, done = ride(scat("w_in", (0, 1, 2)))
    dwd1, rid = mm_tn("ffn1_dwd", hid1, dh1, out_dtype=BF16, scale=0.5, rider=rider)
    done(rid)
    grad(down1, dwd1)
    rider, done = ride(scat("w_in", (1, 2, 2)), xchg(down1))
    (dg1, du1), rid = ffn_dh("ffn1_dh", dh1, wd1, g1, u1, ns, 0.5, rider=rider)
    done(rid)
    own("w_in")
    rider, done = ride(scat(down1, (0, 1, 2)), join("w_out"))
    dwg1, rid = mm_tn_sharded("ffn1_dwg", xn1, dg1, ns, rider=rider)
    done(rid)
    grad(gate1, dwg1)
    rider, done = ride(scat(down1, (1, 2, 2)), xchg(gate1), join("w_in"))
    dwu1, rid = mm_tn_sharded("ffn1_dwu", xn1, du1, ns, rider=rider)
    done(rid)
    grad(up1, dwu1)
    own(down1)
    rider, done = ride(scat(gate1, (0, 1, 2)), xchg(up1), join(down1))
    dxn, rid = mm_nt_sharded("ffn1_dxn_g", dg1, bufs[gate1], rider=rider)
    done(rid)
    rider, done = ride(scat(gate1, (1, 2, 2)), scat(up1, (0, 1, 4)))
    dxn, rid = mm_nt_sharded("ffn1_dxn_u", du1, bufs[up1], resid=dxn, rider=rider)
    done(rid)
    dx, dgain_ffn1 = rms_bwd("ffn1_dnorm", x, sm["norm_ffn1_g"], r1, dxn, dres=dh1)
    own(gate1)

    rider, done = ride(scat(up1, (1, 4, 4)), join(gate1))
    done(run_step("reduce_tail", rider))
    own(up1)
    rider, done = ride(join(up1))
    done(run_step("join_tail", rider))

    g_small = dict(dsm_attn)
    g_small["norm_mix_g"] = g_small["norm_mix_g"] + dgain_mix
    g_small.update(norm_ffn1_g=dgain_ffn1, norm_ffn2_g=dgain_ffn2, out_norm_fox_g=dgain_fox, out_norm_swa_g=dgain_swa)
    return loss, dx, full, g_small


def kernel(x, positions, norm_ffn1_g, ffn1_w_gate, ffn1_w_up, ffn1_w_down, norm_mix_g, w_in, b_forget, fox_q_norm_g, fox_k_norm_g, swa_q_norm_g, swa_k_norm_g, swa_sinks, out_norm_fox_g, out_norm_swa_g, w_out, norm_ffn2_g, ffn2_w_gate, ffn2_w_up, ffn2_w_down, loss_target, m_norm_ffn1_g, m_ffn1_w_gate, m_ffn1_w_up, m_ffn1_w_down, m_norm_mix_g, m_w_in, m_b_forget, m_fox_q_norm_g, m_fox_k_norm_g, m_swa_q_norm_g, m_swa_k_norm_g, m_swa_sinks, m_out_norm_fox_g, m_out_norm_swa_g, m_w_out, m_norm_ffn2_g, m_ffn2_w_gate, m_ffn2_w_up, m_ffn2_w_down, v_norm_ffn1_g, v_ffn1_w_gate, v_ffn1_w_up, v_ffn1_w_down, v_norm_mix_g, v_w_in, v_b_forget, v_fox_q_norm_g, v_fox_k_norm_g, v_swa_q_norm_g, v_swa_k_norm_g, v_swa_sinks, v_out_norm_fox_g, v_out_norm_swa_g, v_w_out, v_norm_ffn2_g, v_ffn2_w_gate, v_ffn2_w_up, v_ffn2_w_down):
    args = dict(locals())
    w = {k: args[k] for k in _ALL}
    m = {k: args["m_" + k] for k in _ALL}
    v = {k: args["v_" + k] for k in _ALL}
    c_idx = lax.axis_index("c").astype(jnp.int32).reshape(1)
    p_idx = (2 * lax.axis_index("x") + lax.axis_index("y")).astype(jnp.int32).reshape(1)
    pc_idx = jnp.concatenate([p_idx, c_idx])

    small = {k: w[k] for k in _SMALL}
    shards = {k: w[k][0] for k in _BIG}
    plan = WinPlan(x.shape[-1])
    shards["w_in"] = lax.switch(p_idx[0], [functools.partial(plan.place, s=s) for s in range(N_CHIPS)], shards["w_in"])
    loss, grad_x, g_shard, g_small = _local_step(shards, {k: w[k][0] for k in _SMALL}, x[0], positions[0], loss_target[0],
                                                 p_idx, c_idx, pc_idx)
    g_shard["w_in"] = lax.switch(p_idx[0], [functools.partial(plan.unplace, s=s) for s in range(N_CHIPS)], g_shard["w_in"])
    loss = lax.psum(loss, ("x", "y", "c"))
    g_small_sum = _unpack_small(all_reduce_small(_pack_small({k: g_small[k].reshape(1, -1) for k in _SMALL})), small)

    grad_w, delta, new_m, new_v = {}, {}, {}, {}
    for k in _BIG:
        (g, d, nm, nv), _ = adamw("adamw_" + k, w[k][0], g_shard[k], m[k][0], v[k][0])
        grad_w[k], delta[k], new_m[k], new_v[k] = g[None], d[None], nm[None], nv[None]
    (_, d, nm, nv), _ = adamw("adamw_small", _pack_small(small), _pack_small(g_small_sum), _pack_small({k: m[k] for k in _SMALL}),
                              _pack_small({k: v[k] for k in _SMALL}))
    grad_w.update(g_small_sum)
    delta.update(_unpack_small(d, small))
    new_m.update(_unpack_small(nm, small))
    new_v.update(_unpack_small(nv, small))

    return (loss, grad_x[None], *[grad_w[k] for k in _ALL], *[delta[k] for k in _ALL], *[new_m[k] for k in _ALL], *[new_v[k] for k in _ALL])
```

```python
import functools

import jax
import jax.numpy as jnp
from jax import lax
from jax.experimental import pallas as pl
from jax.experimental.pallas import tpu as pltpu

F32 = jnp.float32
BF16 = jnp.bfloat16

HEAD_DIM = 64
WINDOW = 128
ROPE_THETA = 10000.0
EPS = 1e-6
N_CHIPS = 4
N_DEV = 8

ADAM_LR = 0.001
ADAM_B1 = 0.9
ADAM_B2 = 0.999
ADAM_EPS = 1e-08
ADAM_WD = 0.01
ADAM_STEP = 10

V7X_VMEM_BYTES = 64 * 1024 * 1024
VMEM_LIMIT = V7X_VMEM_BYTES - 8 * 1024 * 1024
MASK_VALUE = -1e30

_MESH = pl.DeviceIdType.MESH
_HBM = pl.BlockSpec(memory_space=pl.ANY)
_DIMS = {"nn": (((1,), (0,)), ((), ())), "nt": (((1,), (1,)), ((), ())), "tn": (((0,), (0,)), ((), ()))}


def _pick(n, prefs):
    for p in prefs:
        if n % p == 0:
            return p
    return n


class Rider:
    def __init__(self, reads, aliased, news, nsem, build):
        self.reads, self.aliased, self.news, self.nsem, self.build = list(reads), list(aliased), list(news), nsem, build


class _Shifted:
    def __init__(self, ref, off):
        self.ref, self.off = ref, off

    @property
    def at(self):
        return self

    def __getitem__(self, k):
        return self.ref.at[k + self.off]


def combine(*riders):
    def build(reads, al, news, ssem, rsem):
        out = ([], [], [])
        r0 = a0 = n0 = s0 = 0
        for rd in riders:
            nr, na, nn = len(rd.reads), len(rd.aliased), len(rd.news)
            part = rd.build(reads[r0:r0 + nr], al[a0:a0 + na], news[n0:n0 + nn], _Shifted(ssem, s0), _Shifted(rsem, s0))
            for acc, lst in zip(out, part):
                acc.extend(lst)
            r0, a0, n0, s0 = r0 + nr, a0 + na, n0 + nn, s0 + rd.nsem
        return out

    return Rider(sum((r.reads for r in riders), []), sum((r.aliased for r in riders), []), sum((r.news for r in riders), []),
                 sum(r.nsem for r in riders), build)


def _me():
    return lax.axis_index("x"), lax.axis_index("y"), lax.axis_index("c")


def _other_chips(x, y):
    return [(1 - x, y), (x, 1 - y), (1 - x, 1 - y)]


WHOLE = (0, 1, 1)


def _rows(ref, start, rows, part=WHOLE):
    k0, k1, n = part
    assert rows % n == 0, (rows, part)
    idx = (slice(None),) * (len(ref.shape) - 2) + (pl.ds(start + k0 * (rows // n), (k1 - k0) * (rows // n)), slice(None))
    return ref.at[idx]


def _half(ref, h, part=WHOLE):
    rows = ref.shape[-2] // 2
    return _rows(ref, h * rows, rows, part)


def _remote(src, dst, ssem, rsem, k, to):
    return pltpu.make_async_remote_copy(src_ref=src, dst_ref=dst, send_sem=ssem.at[k], recv_sem=rsem.at[k], device_id=to,
                                        device_id_type=_MESH)


def _later(*args):
    return functools.partial(_remote, *args)


def gather(bufs, jobs):
    def build(reads, al, news, ssem, rsem):
        x, y, c = _me()
        p = 2 * x + y
        starts, arrivals = [], []
        for n, (b, kind, part) in enumerate(jobs):
            for j, chip in enumerate(_other_chips(x, y)):
                q = 2 * chip[0] + chip[1]
                if kind == "ici":
                    src, landing, to = _half(al[b].at[p], c, part), _half(al[b].at[q], c, part), (*chip, c)
                else:
                    src, landing, to = _half(al[b].at[q], c, part), _half(al[b].at[q], 1 - c, part), (x, y, 1 - c)
                starts.append(_later(src, src, ssem, rsem, 3 * n + j, to))
                arrivals.append(_later(landing, landing, ssem, rsem, 3 * n + j, to))
        return starts, arrivals, starts

    return Rider([], bufs, [], 3 * len(jobs), build)


def exchange_halves(grads):
    def build(reads, al, news, ssem, rsem):
        x, y, c = _me()
        cps = [_later(_half(g, 1 - c), t, ssem, rsem, w, (x, y, 1 - c)) for w, (g, t) in enumerate(zip(reads, news))]
        return cps, cps, cps

    return Rider(grads, [], [jax.ShapeDtypeStruct((g.shape[0], g.shape[1] // 2, g.shape[2]), g.dtype) for g in grads], len(grads), build)


def scatter_to_owner(sums, gots=None, part=WHOLE):
    def build(reads, al, news, ssem, rsem):
        x, y, c = _me()
        cps = []
        for w, (s, got) in enumerate(zip(reads, al or news)):
            rows = s.shape[-2]
            for j, chip in enumerate(_other_chips(x, y)):
                cps.append(_later(_rows(s.at[2 * chip[0] + chip[1]], 0, rows, part), _rows(got.at[j], 0, rows, part), ssem, rsem,
                                  3 * w + j, (*chip, c)))
        return cps, cps, cps

    news = [] if gots else [jax.ShapeDtypeStruct((3,) + s.shape[1:], s.dtype) for s in sums]
    return Rider(sums, gots or [], news, 3 * len(sums), build)


def join_halves(fulls):
    def build(reads, al, news, ssem, rsem):
        x, y, c = _me()
        starts, arrivals = [], []
        for w, f in enumerate(al):
            mine, landing = _half(f, c), _half(f, 1 - c)
            starts.append(_later(mine, mine, ssem, rsem, w, (x, y, 1 - c)))
            arrivals.append(_later(landing, landing, ssem, rsem, w, (x, y, 1 - c)))
        return starts, arrivals, starts

    return Rider([], fulls, [], len(fulls), build)


def _start_and_wait(rider, reads, al, news, ssem, rsem, first, last):
    @pl.when(first)
    def _():
        for cp in rider.build(reads, al, news, ssem, rsem)[0]:
            cp().start()

    def finish():
        @pl.when(last)
        def _():
            _, arrivals, sends = rider.build(reads, al, news, ssem, rsem)
            for cp in arrivals:
                cp().wait_recv()
            for cp in sends:
                cp().wait_send()

    return finish


def _call(name, body, grid, in_specs, args, out_specs, out_shape, scratch=(), semantics=None, rider=None, prefetch=()):
    n_pre, n_in, n_out, n_scr = len(prefetch), len(args), len(out_shape), len(scratch)
    nr, na, nn = (len(rider.reads), len(rider.aliased), len(rider.news)) if rider else (0, 0, 0)

    def wrapped(*refs):
        pre, refs = refs[:n_pre], refs[n_pre:]
        ins, reads = refs[:n_in], refs[n_in:n_in + nr]
        o0 = n_in + nr + na
        outs, al, news = refs[o0:o0 + n_out], refs[o0 + n_out:o0 + n_out + na], refs[o0 + n_out + na:o0 + n_out + na + nn]
        s0 = o0 + n_out + na + nn
        scr, (ssem, rsem) = refs[s0:s0 + n_scr], refs[s0 + n_scr:]
        first = functools.reduce(jnp.logical_and, [pl.program_id(a) == 0 for a in range(len(grid))])
        last = functools.reduce(jnp.logical_and, [pl.program_id(a) == g - 1 for a, g in enumerate(grid)])
        finish = _start_and_wait(rider, reads, al, news, ssem, rsem, first, last)
        body(*pre, *ins, *outs, *scr)
        finish()

    kernel_fn, all_in, all_out, shapes, scr = body, list(in_specs), list(out_specs), list(out_shape), list(scratch)
    operands, aliases = (*prefetch, *args), {}
    if rider:
        kernel_fn, semantics = wrapped, ("arbitrary",) * len(grid)
        all_in += [_HBM] * (nr + na)
        all_out += [_HBM] * (na + nn)
        shapes += [jax.ShapeDtypeStruct(a.shape, a.dtype) for a in rider.aliased] + rider.news
        scr += [pltpu.SemaphoreType.DMA((rider.nsem,)), pltpu.SemaphoreType.DMA((rider.nsem,))]
        operands += (*rider.reads, *rider.aliased)
        aliases = {n_pre + n_in + nr + i: n_out + i for i in range(na)}
    params = pltpu.CompilerParams(dimension_semantics=semantics, vmem_limit_bytes=VMEM_LIMIT)
    if n_pre:
        spec = pltpu.PrefetchScalarGridSpec(num_scalar_prefetch=n_pre, grid=grid, in_specs=all_in, out_specs=all_out, scratch_shapes=scr)
        outs = pl.pallas_call(kernel_fn, name=name, grid_spec=spec, out_shape=shapes, input_output_aliases=aliases, compiler_params=params)(*operands)
    else:
        outs = pl.pallas_call(kernel_fn, name=name, grid=grid, in_specs=all_in, out_specs=all_out, out_shape=shapes, scratch_shapes=scr,
                              input_output_aliases=aliases, compiler_params=params)(*operands)
    return list(outs[:n_out]), ((list(outs[n_out:n_out + na]), list(outs[n_out + na:])) if rider else None)


def run_step(name, rider):
    nr, na, nn = len(rider.reads), len(rider.aliased), len(rider.news)

    def body(*refs):
        reads = refs[:nr]
        al, news = refs[nr + na:nr + 2 * na], refs[nr + 2 * na:nr + 2 * na + nn]
        ssem, rsem = refs[nr + 2 * na + nn:]
        starts, arrivals, sends = rider.build(reads, al, news, ssem, rsem)
        for cp in starts:
            cp().start()
        for cp in arrivals:
            cp().wait_recv()
        for cp in sends:
            cp().wait_send()

    outs = pl.pallas_call(
        body, name=name, in_specs=[_HBM] * (nr + na), out_specs=[_HBM] * (na + nn),
        out_shape=[jax.ShapeDtypeStruct(a.shape, a.dtype) for a in rider.aliased] + rider.news,
        input_output_aliases={nr + i: i for i in range(na)},
        scratch_shapes=[pltpu.SemaphoreType.DMA((rider.nsem,)), pltpu.SemaphoreType.DMA((rider.nsem,))],
    )(*rider.reads, *rider.aliased)
    return list(outs[:na]), list(outs[na:])


def all_reduce_small(v):
    rows, lanes = v.shape

    def body(v_ref, o_ref, slots, send_sems, recv_sems):
        x, y, c = _me()
        me = 4 * x + 2 * y + c
        slots[me] = v_ref[...]
        cps = []
        for k in range(1, N_DEV):
            peer = (x ^ (k >> 2), y ^ ((k >> 1) & 1), c ^ (k & 1))
            cps.append(_remote(v_ref, slots.at[me], send_sems, recv_sems, k - 1, peer))
            cps[-1].start()
        for k in range(1, N_DEV):
            theirs = slots.at[me ^ k]
            _remote(theirs, theirs, send_sems, recv_sems, k - 1, (x, y, c)).wait_recv()
        for cp in cps:
            cp.wait_send()
        acc = slots[0]
        for i in range(1, N_DEV):
            acc = acc + slots[i]
        o_ref[...] = acc

    return pl.pallas_call(
        body, name="all_reduce_small",
        in_specs=[pl.BlockSpec(memory_space=pltpu.VMEM)], out_specs=pl.BlockSpec(memory_space=pltpu.VMEM),
        out_shape=jax.ShapeDtypeStruct((rows, lanes), F32),
        scratch_shapes=[pltpu.VMEM((N_DEV, rows, lanes), F32), pltpu.SemaphoreType.DMA((N_DEV - 1,)), pltpu.SemaphoreType.DMA((N_DEV - 1,))],
    )(v)


def _mm_call(name, mode, a, b, a_spec, b_spec, out_shape, out_spec, grid, acc_shape, scale=1.0, resid=None, resid_spec=None, rider=None):
    nk = grid[2]
    dims = _DIMS[mode]
    has_resid = resid is not None

    def body(*refs):
        a_ref, b_ref = refs[:2]
        r_ref = refs[2] if has_resid else None
        o_ref = refs[3] if has_resid else refs[2]

        def finish(r):
            if scale != 1.0:
                r = r * scale
            if has_resid:
                r = r_ref[...].astype(F32) + r
            o_ref[...] = r.astype(o_ref.dtype)

        part = lax.dot_general(a_ref[...].astype(BF16), b_ref[...].astype(BF16), dims, preferred_element_type=F32)
        if nk == 1:
            finish(part)
            return
        acc_ref = refs[-1]
        k = pl.program_id(2)

        @pl.when(k == 0)
        def _():
            acc_ref[...] = part

        @pl.when(k > 0)
        def _():
            acc_ref[...] += part

        @pl.when(k == nk - 1)
        def _():
            finish(acc_ref[...])

    in_specs = [a_spec, b_spec] + ([resid_spec] if has_resid else [])
    args = (a, b) + ((resid,) if has_resid else ())
    (out,), rid = _call(name, body, grid, in_specs, args, [out_spec], [out_shape], [pltpu.VMEM(acc_shape, F32)] if nk > 1 else [],
                        ("parallel", "parallel", "arbitrary"), rider)
    return out, rid


MM_VMEM_BUDGET = 40 * 1024 * 1024
_TILE_OPTS = (2048, 1408, 1024, 512, 256, 128)


def _tiles(m, n, kd, a_item, b_item, o_item, r_item=0, tm=None, tn=None, tk=None):
    def opts(full, fixed, cap):
        return [fixed] if fixed else [t for t in _TILE_OPTS if t <= cap and full % t == 0] or [full]

    best = None
    for cm in opts(m, tm, 1408):
        for cn in opts(n, tn, 1408):
            for ck in opts(kd, tk, 2048):
                blocks = cm * ck * a_item + ck * cn * b_item + cm * cn * (o_item + r_item)
                casts = (cm * ck * 2 if a_item == 4 else 0) + (ck * cn * 2 if b_item == 4 else 0)
                if 2 * blocks + cm * cn * 4 + casts <= MM_VMEM_BUDGET:
                    key = (cm * cn * ck, ck)
                    if best is None or key > best[0]:
                        best = (key, (cm, cn, ck))
    assert best is not None, (m, n, kd)
    return best[1]


def _item(x):
    return jnp.dtype(x.dtype).itemsize


def mm_nn(name, a, b, *, out_dtype=F32, scale=1.0, resid=None, rider=None):
    m, kd = a.shape
    n = b.shape[1]
    tm, tn, tk = _tiles(m, n, kd, _item(a), _item(b), jnp.dtype(out_dtype).itemsize, 0 if resid is None else _item(resid))
    o_spec = pl.BlockSpec((tm, tn), lambda i, j, k: (i, j))
    return _mm_call(
        name, "nn", a, b, pl.BlockSpec((tm, tk), lambda i, j, k: (i, k)), pl.BlockSpec((tk, tn), lambda i, j, k: (k, j)),
        jax.ShapeDtypeStruct((m, n), out_dtype), o_spec, (m // tm, n // tn, kd // tk), (tm, tn), scale, resid, o_spec, rider)


def mm_nt(name, a, b, *, out_dtype=F32, scale=1.0, resid=None, rider=None):
    m, kd = a.shape
    n = b.shape[0]
    tm, tn, tk = _tiles(m, n, kd, _item(a), _item(b), jnp.dtype(out_dtype).itemsize, 0 if resid is None else _item(resid))
    o_spec = pl.BlockSpec((tm, tn), lambda i, j, k: (i, j))
    return _mm_call(
        name, "nt", a, b, pl.BlockSpec((tm, tk), lambda i, j, k: (i, k)), pl.BlockSpec((tn, tk), lambda i, j, k: (j, k)),
        jax.ShapeDtypeStruct((m, n), out_dtype), o_spec, (m // tm, n // tn, kd // tk), (tm, tn), scale, resid, o_spec, rider)


def mm_tn(name, a, b, *, out_dtype=F32, scale=1.0, rider=None):
    kd, m = a.shape
    n = b.shape[1]
    tm, tn, tk = _tiles(m, n, kd, _item(a), _item(b), jnp.dtype(out_dtype).itemsize)
    return _mm_call(
        name, "tn", a, b, pl.BlockSpec((tk, tm), lambda i, j, k: (k, i)), pl.BlockSpec((tk, tn), lambda i, j, k: (k, j)),
        jax.ShapeDtypeStruct((m, n), out_dtype), pl.BlockSpec((tm, tn), lambda i, j, k: (i, j)),
        (m // tm, n // tn, kd // tk), (tm, tn), scale, rider=rider)


def mm_nt_sharded(name, a, w, *, resid=None, rider=None):
    m = a.shape[0]
    ns, n, c = w.shape
    tm, tn, _ = _tiles(m, n, c, _item(a), _item(w), 4, 0 if resid is None else _item(resid), tk=c)
    o_spec = pl.BlockSpec((tm, tn), lambda i, j, k: (i, j))
    return _mm_call(
        name, "nt", a, w, pl.BlockSpec((tm, c), lambda i, j, k: (i, k)), pl.BlockSpec((None, tn, c), lambda i, j, k: (k, j, 0)),
        jax.ShapeDtypeStruct((m, n), F32), o_spec, (m // tm, n // tn, ns), (tm, tn), 1.0, resid, o_spec, rider)


def mm_tn_sharded(name, a, b, ns, *, rider=None):
    kd, m = a.shape
    c = b.shape[1] // ns
    tm, _, tk = _tiles(m, c, kd, _item(a), _item(b), 2, tn=c)
    return _mm_call(
        name, "tn", a, b, pl.BlockSpec((tk, tm), lambda i, j, k: (k, i)), pl.BlockSpec((tk, c), lambda i, j, k: (k, j)),
        jax.ShapeDtypeStruct((ns, m, c), BF16), pl.BlockSpec((None, tm, c), lambda i, j, k: (j, i, 0)),
        (m // tm, ns, kd // tk), (tm, c), rider=rider)


def rms_fwd(name, x, g, out_dtype, rider=None):
    r, c = x.shape
    tm = _pick(r, (512, 256, 128, 64, 8))

    def body(x_ref, g_ref, y_ref, r_ref):
        xf = x_ref[...].astype(F32)
        rstd = lax.rsqrt(jnp.mean(xf * xf, axis=-1, keepdims=True) + EPS)
        y_ref[...] = ((xf * rstd) * g_ref[...]).astype(y_ref.dtype)
        r_ref[...] = rstd

    (y, rstd), rid = _call(
        name, body, (r // tm,), [pl.BlockSpec((tm, c), lambda i: (i, 0)), pl.BlockSpec((1, c), lambda i: (0, 0))], (x, g.reshape(1, c)),
        [pl.BlockSpec((tm, c), lambda i: (i, 0)), pl.BlockSpec((tm, 1), lambda i: (i, 0))],
        [jax.ShapeDtypeStruct((r, c), out_dtype), jax.ShapeDtypeStruct((r, 1), F32)], (), ("parallel",), rider)
    return (y, rstd) if rider is None else (y, rstd, rid)


def rms_bwd(name, x, g, rstd, dy, dres=None, rider=None, also_bf16=False):
    r, c = x.shape
    tm = _pick(r, (512, 256, 128, 64, 8))
    has_res = dres is not None

    def body(*refs):
        x_ref, g_ref, r_ref, dy_ref = refs[:4]
        dres_ref = refs[4] if has_res else None
        dx_ref, dg_ref = refs[4 + has_res:6 + has_res]
        xhat = x_ref[...].astype(F32) * r_ref[...]
        dyf = dy_ref[...].astype(F32)
        gdy = dyf * g_ref[...]
        dx = r_ref[...] * (gdy - xhat * jnp.mean(gdy * xhat, axis=-1, keepdims=True))
        if has_res:
            dx = dx + dres_ref[...]
        dx_ref[...] = dx
        if also_bf16:
            refs[-1][...] = dx.astype(BF16)

        @pl.when(pl.program_id(0) == 0)
        def _():
            dg_ref[...] = jnp.zeros_like(dg_ref)

        dg_ref[...] += jnp.sum(dyf * xhat, axis=0, keepdims=True)

    row = pl.BlockSpec((tm, c), lambda i: (i, 0))
    in_specs = [row, pl.BlockSpec((1, c), lambda i: (0, 0)), pl.BlockSpec((tm, 1), lambda i: (i, 0)), row] + ([row] if has_res else [])
    args = (x, g.reshape(1, c), rstd, dy) + ((dres,) if has_res else ())
    outs, rid = _call(name, body, (r // tm,), in_specs, args, [row, pl.BlockSpec((1, c), lambda i: (0, 0))] + [row] * also_bf16,
                      [jax.ShapeDtypeStruct((r, c), F32), jax.ShapeDtypeStruct((1, c), F32)] + [jax.ShapeDtypeStruct((r, c), BF16)] * also_bf16,
                      (), ("arbitrary",), rider)
    return (outs[0], outs[1].reshape(c), *outs[2:], *([] if rider is None else [rid]))


_LANES = 128


def _head_mean(v):
    if v.shape[1] == HEAD_DIM:
        return jnp.mean(v, axis=-1, keepdims=True)
    low = lax.broadcasted_iota(jnp.int32, v.shape, 1) < HEAD_DIM
    lo = jnp.sum(jnp.where(low, v, 0.0), axis=-1, keepdims=True)
    hi = jnp.sum(jnp.where(low, 0.0, v), axis=-1, keepdims=True)
    return jnp.where(low, lo, hi) * (1.0 / HEAD_DIM)


def _head_groups(c):
    width = _LANES if c % _LANES == 0 else HEAD_DIM
    assert c % width == 0, c
    return width, [slice(k * width, (k + 1) * width) for k in range(c // width)]


def _head_gain(g, width):
    return jnp.tile(g.reshape(1, HEAD_DIM), (1, width // HEAD_DIM))


def _rotate_half(y):
    half = HEAD_DIM // 2
    first = lax.broadcasted_iota(jnp.int32, y.shape, 1) % HEAD_DIM < half
    return jnp.where(first, -pltpu.roll(y, y.shape[1] - half, axis=1), pltpu.roll(y, half, axis=1))


def _rope_tables(rope, width):
    return [jnp.tile(t, (1, 2 * width // HEAD_DIM)) for t in rope]


def head_rms_fwd(name, x, g, rope=None):
    s, c = x.shape
    tm = _pick(s, (256, 128, 8))
    width, groups = _head_groups(c)

    def body(x_ref, g_ref, *refs):
        y_ref = refs[-1]
        for sl in groups:
            xs = x_ref[:, sl]
            y = (xs * lax.rsqrt(_head_mean(xs * xs) + EPS)) * g_ref[...]
            if rope:
                y = y * refs[0][...] + _rotate_half(y) * refs[1][...]
            y_ref[:, sl] = y

    row = pl.BlockSpec((tm, c), lambda i: (i, 0))
    tab = pl.BlockSpec((tm, width), lambda i: (i, 0))
    tables = _rope_tables(rope, width) if rope else []
    (y,), _ = _call(name, body, (s // tm,), [row, pl.BlockSpec((1, width), lambda i: (0, 0))] + [tab] * len(tables),
                    (x, _head_gain(g, width), *tables), [row], [jax.ShapeDtypeStruct((s, c), F32)], (), ("parallel",))
    return y


def head_rms_bwd(name, x, g, dy, rope=None):
    s, c = x.shape
    tm = _pick(s, (256, 128, 8))
    width, groups = _head_groups(c)

    def body(x_ref, g_ref, dy_ref, *refs):
        dx_ref, dg_ref = refs[-2:]

        @pl.when(pl.program_id(0) == 0)
        def _():
            dg_ref[...] = jnp.zeros_like(dg_ref)

        for sl in groups:
            xs, dys = x_ref[:, sl], dy_ref[:, sl]
            if rope:
                dys = dys * refs[0][...] - _rotate_half(dys * refs[1][...])
            rstd = lax.rsqrt(_head_mean(xs * xs) + EPS)
            xhat = xs * rstd
            gdy = dys * g_ref[...]
            dx_ref[:, sl] = rstd * (gdy - xhat * _head_mean(gdy * xhat))
            dg_ref[...] += jnp.sum(dys * xhat, axis=0, keepdims=True)

    row = pl.BlockSpec((tm, c), lambda i: (i, 0))
    vec = pl.BlockSpec((1, width), lambda i: (0, 0))
    tab = pl.BlockSpec((tm, width), lambda i: (i, 0))
    tables = _rope_tables(rope, width) if rope else []
    (dx, dg), _ = _call(name, body, (s // tm,), [row, vec, row] + [tab] * len(tables), (x, _head_gain(g, width), dy, *tables), [row, vec],
                        [jax.ShapeDtypeStruct((s, c), F32), jax.ShapeDtypeStruct((1, width), F32)], (), ("arbitrary",))
    return dx, jnp.sum(dg.reshape(width // HEAD_DIM, HEAD_DIM), axis=0)


@functools.partial(jax.custom_vjp, nondiff_argnums=(0,))
def head_rms(name, x, g):
    return head_rms_fwd(name + "_fwd", x, g)


def _head_rms_fwd(name, x, g):
    return head_rms_fwd(name + "_fwd", x, g), (x, g)


def _head_rms_bwd(name, res, dy):
    return head_rms_bwd(name + "_bwd", *res, dy)


head_rms.defvjp(_head_rms_fwd, _head_rms_bwd)


@functools.partial(jax.custom_vjp, nondiff_argnums=(0,))
def head_rms_rope(name, x, g, cos, sin):
    return head_rms_fwd(name + "_fwd", x, g, (cos, sin))


def _head_rms_rope_fwd(name, x, g, cos, sin):
    return head_rms_fwd(name + "_fwd", x, g, (cos, sin)), (x, g, cos, sin)


def _head_rms_rope_bwd(name, res, dy):
    x, g, cos, sin = res
    return (*head_rms_bwd(name + "_bwd", x, g, dy, (cos, sin)), jnp.zeros_like(cos), jnp.zeros_like(sin))


head_rms_rope.defvjp(_head_rms_rope_fwd, _head_rms_rope_bwd)


FFN_TM = 512


def _sigmoid(x):
    return 1.0 / (1.0 + jnp.exp(-x))


def ffn_gu(name, xn, wg, wu, rider=None):
    s, d = xn.shape
    ns, _, c = wg.shape
    tm = _pick(s, (FFN_TM, 128))

    def body(x_ref, wg_ref, wu_ref, h_ref, a_ref, b_ref):
        xb = x_ref[...]
        gv = jnp.dot(xb, wg_ref[...], preferred_element_type=F32)
        uv = jnp.dot(xb, wu_ref[...], preferred_element_type=F32)
        sig = _sigmoid(gv)
        silu = gv * sig
        h_ref[...] = (silu * uv).astype(BF16)
        a_ref[...] = (uv * (sig * (1.0 + gv * (1.0 - sig)))).astype(BF16)
        b_ref[...] = silu.astype(BF16)

    w_spec = pl.BlockSpec((None, d, c), lambda j, i: (j, 0, 0))
    o_spec = pl.BlockSpec((tm, c), lambda j, i: (i, j))
    return _call(
        name, body, (ns, s // tm), [pl.BlockSpec((tm, d), lambda j, i: (i, 0)), w_spec, w_spec], (xn, wg, wu),
        [o_spec, o_spec, o_spec], [jax.ShapeDtypeStruct((s, ns * c), BF16)] * 3, [], ("parallel", "parallel"), rider)


def ffn_dh(name, dy, wd, dh_dg, dh_du, ns, scale, rider=None):
    s, d = dy.shape
    f = wd.shape[0]
    c = f // ns
    tm = _pick(s, (FFN_TM, 128))

    def body(dy_ref, wd_ref, a_ref, b_ref, dg_ref, du_ref):
        dh = lax.dot_general(dy_ref[...].astype(BF16), wd_ref[...], _DIMS["nt"], preferred_element_type=F32) * scale
        dg_ref[...] = (dh * a_ref[...].astype(F32)).astype(BF16)
        du_ref[...] = (dh * b_ref[...].astype(F32)).astype(BF16)

    o_spec = pl.BlockSpec((tm, c), lambda j, i: (i, j))
    return _call(
        name, body, (ns, s // tm),
        [pl.BlockSpec((tm, d), lambda j, i: (i, 0)), pl.BlockSpec((c, d), lambda j, i: (j, 0)), o_spec, o_spec], (dy, wd, dh_dg, dh_du),
        [o_spec, o_spec], [jax.ShapeDtypeStruct((s, f), BF16), jax.ShapeDtypeStruct((s, f), BF16)],
        [], ("parallel", "parallel"), rider)


FOX_TQ = 512


def fox_tile(s_len):
    return min(FOX_TQ, s_len)


def _heads_per_block(h):
    return 2 if h % 2 == 0 else 1


def _fox_queries(q):
    return (q * (HEAD_DIM ** -0.5)).astype(BF16)


def _fox_scores(qs, kc, cq, ck, diagonal):
    s = lax.dot_general(qs, kc.astype(BF16), _DIMS["nt"], preferred_element_type=F32) + cq - ck
    if not diagonal:
        return s
    return jnp.where(lax.broadcasted_iota(jnp.int32, s.shape, 0) >= lax.broadcasted_iota(jnp.int32, s.shape, 1), s, MASK_VALUE)


def _fox_specs(h, s_len, tq):
    hb = _heads_per_block(h)
    qb = pl.BlockSpec((tq, hb * HEAD_DIM), lambda pp, i: (i, pp))
    kb = pl.BlockSpec((s_len, hb * HEAD_DIM), lambda pp, i: (0, pp))
    colb = pl.BlockSpec((hb, tq, 1), lambda pp, i: (pp, i, 0))
    rowb = pl.BlockSpec((hb, s_len // tq, 1, tq), lambda pp, i: (pp, 0, 0, 0))
    return hb, qb, kb, colb, rowb


def fox_fwd(q, k, v, cq, ck, rider=None):
    s_len, hd = q.shape
    h, d = hd // HEAD_DIM, HEAD_DIM
    tq = fox_tile(s_len)
    hb, qb, kb, colb, rowb = _fox_specs(h, s_len, tq)

    def body(q_ref, k_ref, v_ref, cq_ref, ck_ref, o_ref, lse_ref):
        i = pl.program_id(1)
        for hh in range(hb):
            lanes = slice(hh * d, (hh + 1) * d)
            qs, cqv = _fox_queries(q_ref[:, lanes]), cq_ref[hh]

            def chunk(c, carry, diagonal=False):
                m, l, acc = carry
                rows = pl.ds(pl.multiple_of(c * tq, tq), tq)
                s = _fox_scores(qs, k_ref[rows, lanes], cqv, ck_ref[hh, c], diagonal)
                m_new = jnp.maximum(m, jnp.max(s, axis=-1, keepdims=True))
                alpha = jnp.exp(m - m_new)
                p = jnp.exp(s - m_new)
                acc = alpha * acc + jnp.dot(p.astype(BF16), v_ref[rows, lanes].astype(BF16), preferred_element_type=F32)
                return m_new, alpha * l + jnp.sum(p, axis=-1, keepdims=True), acc

            init = (jnp.full((tq, 1), MASK_VALUE, F32), jnp.zeros((tq, 1), F32), jnp.zeros((tq, d), F32))
            m, l, acc = chunk(i, lax.fori_loop(0, i, chunk, init), diagonal=True)
            o_ref[:, lanes] = acc / l
            lse_ref[hh] = m + jnp.log(l)

    return _call(
        "fox_fwd", body, (h // hb, s_len // tq), [qb, kb, kb, colb, rowb], (q, k, v, cq, ck), [qb, colb],
        [jax.ShapeDtypeStruct((s_len, hd), F32), jax.ShapeDtypeStruct((h, s_len, 1), F32)], (), ("parallel", "parallel"), rider)


def fox_bwd(q, k, v, cq, ck, o, lse, do, rider=None):
    s_len, hd = q.shape
    h, d = hd // HEAD_DIM, HEAD_DIM
    tq = fox_tile(s_len)
    scale = HEAD_DIM ** -0.5
    hb, qb, kb, colb, rowb = _fox_specs(h, s_len, tq)

    def body(q_ref, k_ref, v_ref, cq_ref, ck_ref, o_ref, lse_ref, do_ref, dq_ref, dk_ref, dv_ref, dcq_ref, dck_ref):
        i = pl.program_id(1)

        @pl.when(i == 0)
        def _():
            dk_ref[...] = jnp.zeros_like(dk_ref)
            dv_ref[...] = jnp.zeros_like(dv_ref)
            dck_ref[...] = jnp.zeros_like(dck_ref)

        heads = []
        for hh in range(hb):
            lanes = slice(hh * d, (hh + 1) * d)
            dof = do_ref[:, lanes]
            heads.append((lanes, _fox_queries(q_ref[:, lanes]), cq_ref[hh], lse_ref[hh], dof.astype(BF16),
                          jnp.sum(dof * o_ref[:, lanes], axis=-1, keepdims=True)))

        def chunk(c, carry, diagonal=False):
            rows = pl.ds(pl.multiple_of(c * tq, tq), tq)
            out, dks, dvs = [], [], []
            for hh, (lanes, qs, cqv, lse_h, dob, delta) in enumerate(heads):
                dq, dcq = carry[hh]
                kc = k_ref[rows, lanes]
                p = jnp.exp(_fox_scores(qs, kc, cqv, ck_ref[hh, c], diagonal) - lse_h)
                dp = lax.dot_general(dob, v_ref[rows, lanes].astype(BF16), _DIMS["nt"], preferred_element_type=F32)
                ds = p * (dp - delta)
                dsb = ds.astype(BF16)
                dvs.append(lax.dot_general(p.astype(BF16), dob, _DIMS["tn"], preferred_element_type=F32))
                dks.append(lax.dot_general(dsb, qs, _DIMS["tn"], preferred_element_type=F32))
                dck_ref[hh, c] -= jnp.sum(ds, axis=0, keepdims=True)
                out.append((dq + jnp.dot(dsb, kc.astype(BF16), preferred_element_type=F32), dcq + jnp.sum(ds, axis=-1, keepdims=True)))
            dk_ref[rows, :] += jnp.concatenate(dks, axis=1)
            dv_ref[rows, :] += jnp.concatenate(dvs, axis=1)
            return tuple(out)

        init = tuple((jnp.zeros((tq, d), F32), jnp.zeros((tq, 1), F32)) for _ in range(hb))
        done = chunk(i, lax.fori_loop(0, i, chunk, init), diagonal=True)
        dq_ref[...] = jnp.concatenate([dq for dq, _ in done], axis=1) * scale
        for hh, (_, dcq) in enumerate(done):
            dcq_ref[hh] = dcq

    return _call(
        "fox_bwd", body, (h // hb, s_len // tq), [qb, kb, kb, colb, rowb, qb, colb, qb], (q, k, v, cq, ck, o, lse, do),
        [qb, kb, kb, colb, rowb],
        [jax.ShapeDtypeStruct((s_len, hd), F32)] * 3
        + [jax.ShapeDtypeStruct((h, s_len, 1), F32), jax.ShapeDtypeStruct((h, s_len // tq, 1, tq), F32)],
        (), ("parallel", "arbitrary"), rider)


def _stack_heads(ref, first, g):
    return jnp.concatenate([ref[:, (first + j) * HEAD_DIM:(first + j + 1) * HEAD_DIM] for j in range(g)], axis=0)


def _window(prev_ref, cur_ref, hh):
    lanes = slice(hh * HEAD_DIM, (hh + 1) * HEAD_DIM)
    return jnp.concatenate([prev_ref[:, lanes], cur_ref[:, lanes]], axis=0).astype(BF16)


def _swa_band(g, w):
    t = lax.broadcasted_iota(jnp.int32, (g * w, 2 * w), 0) % w
    col = lax.broadcasted_iota(jnp.int32, (g * w, 2 * w), 1)
    rel = t + w - col
    band = (rel >= 0) & (rel < w)
    return jnp.where(jnp.stack([band & (col >= w), band]), 0.0, MASK_VALUE).astype(F32)


def _swa_probs(qs, kw, sink, band):
    s = lax.dot_general(qs, kw, _DIMS["nt"], preferred_element_type=F32) + band
    m = jnp.maximum(jnp.max(s, axis=-1, keepdims=True), sink)
    p = jnp.exp(s - m)
    ps = jnp.exp(sink - m)
    linv = 1.0 / (jnp.sum(p, axis=-1, keepdims=True) + ps)
    return p * linv, ps * linv


def _swa_specs(hk, g, s_len):
    w = WINDOW
    assert s_len % w == 0
    hb = _heads_per_block(hk)
    qb = pl.BlockSpec((w, hb * g * HEAD_DIM), lambda pp, n: (n, pp))
    prev = pl.BlockSpec((w, hb * HEAD_DIM), lambda pp, n: (jnp.maximum(n - 1, 0), pp))
    cur = pl.BlockSpec((w, hb * HEAD_DIM), lambda pp, n: (n, pp))
    sb = pl.BlockSpec((hb, g * w, 1), lambda pp, n: (pp, 0, 0))
    band = pl.BlockSpec((None, g * w, 2 * w), lambda pp, n: (jnp.minimum(n, 1), 0, 0))
    return hb, qb, prev, cur, sb, band


def swa_fwd(q, k, v, sink, rider=None):
    s_len = q.shape[0]
    hk = k.shape[1] // HEAD_DIM
    g = q.shape[1] // k.shape[1]
    w, d = WINDOW, HEAD_DIM
    hb, qb, prev, cur, sb, bandb = _swa_specs(hk, g, s_len)

    def body(q_ref, kp_ref, kc_ref, vp_ref, vc_ref, sink_ref, band_ref, o_ref):
        for hh in range(hb):
            qs = (_stack_heads(q_ref, hh * g, g) * (HEAD_DIM ** -0.5)).astype(BF16)
            p, _ = _swa_probs(qs, _window(kp_ref, kc_ref, hh), sink_ref[hh], band_ref[...])
            o = jnp.dot(p.astype(BF16), _window(vp_ref, vc_ref, hh), preferred_element_type=F32)
            for j in range(g):
                o_ref[:, (hh * g + j) * d:(hh * g + j + 1) * d] = o[j * w:(j + 1) * w]

    (o,), rid = _call("swa_fwd", body, (hk // hb, s_len // w), [qb, prev, cur, prev, cur, sb, bandb],
                      (q, k, k, v, v, sink, _swa_band(g, w)), [qb], [jax.ShapeDtypeStruct(q.shape, F32)], (),
                      ("parallel", "parallel"), rider)
    return o, rid


def swa_bwd(q, k, v, sink, o, do, rider=None):
    s_len = q.shape[0]
    hk = k.shape[1] // HEAD_DIM
    g = q.shape[1] // k.shape[1]
    w, d = WINDOW, HEAD_DIM
    scale = HEAD_DIM ** -0.5
    hb, qb, prev, cur, sb, bandb = _swa_specs(hk, g, s_len)

    def body(q_ref, kp_ref, kc_ref, vp_ref, vc_ref, sink_ref, band_ref, o_ref, do_ref, dq_ref, dkp_ref, dkc_ref, dvp_ref, dvc_ref,
             dsink_ref):
        @pl.when(pl.program_id(1) == 0)
        def _():
            dsink_ref[...] = jnp.zeros_like(dsink_ref)

        for hh in range(hb):
            lanes = slice(hh * d, (hh + 1) * d)
            qs = (_stack_heads(q_ref, hh * g, g) * scale).astype(BF16)
            kw, vw = _window(kp_ref, kc_ref, hh), _window(vp_ref, vc_ref, hh)
            p, ps = _swa_probs(qs, kw, sink_ref[hh], band_ref[...])
            dof = _stack_heads(do_ref, hh * g, g)
            dob = dof.astype(BF16)
            delta = jnp.sum(dof * _stack_heads(o_ref, hh * g, g), axis=-1, keepdims=True)
            dp = lax.dot_general(dob, vw, _DIMS["nt"], preferred_element_type=F32)
            ds = p * (dp - delta)
            dsb = ds.astype(BF16)
            dsink_ref[hh] -= ps * delta
            dq = jnp.dot(dsb, kw, preferred_element_type=F32) * scale
            for j in range(g):
                dq_ref[:, (hh * g + j) * d:(hh * g + j + 1) * d] = dq[j * w:(j + 1) * w]
            dkw = lax.dot_general(dsb, qs, _DIMS["tn"], preferred_element_type=F32)
            dvw = lax.dot_general(p.astype(BF16), dob, _DIMS["tn"], preferred_element_type=F32)
            dkp_ref[:, lanes] = dkw[:w]
            dkc_ref[:, lanes] = dkw[w:]
            dvp_ref[:, lanes] = dvw[:w]
            dvc_ref[:, lanes] = dvw[w:]

    kv_shape = jax.ShapeDtypeStruct(k.shape, F32)
    (dq, dkp, dkc, dvp, dvc, dsink), rid = _call(
        "swa_bwd", body, (hk // hb, s_len // w), [qb, prev, cur, prev, cur, sb, bandb, qb, qb],
        (q, k, k, v, v, sink, _swa_band(g, w), o, do),
        [qb, cur, cur, cur, cur, sb],
        [jax.ShapeDtypeStruct(q.shape, F32), kv_shape, kv_shape, kv_shape, kv_shape, jax.ShapeDtypeStruct((hk, g * w, 1), F32)],
        (), ("parallel", "arbitrary"), rider)

    def shift_up(a):
        return jnp.concatenate([a[w:], jnp.zeros_like(a[:w])], axis=0)

    return (dq, dkc + shift_up(dkp), dvc + shift_up(dvp), dsink), rid


def loss_call(y, target):
    s, d = y.shape
    tm = _pick(s, (512, 256, 128))

    def body(y_ref, t_ref, l_ref, dy_ref, dyb_ref):
        e = y_ref[...] - t_ref[...]
        dy = e * (1.0 / d)
        dy_ref[...] = dy
        dyb_ref[...] = dy.astype(BF16)

        @pl.when(pl.program_id(0) == 0)
        def _():
            l_ref[...] = jnp.zeros_like(l_ref)

        l_ref[...] += jnp.sum(jnp.sum(e * e, axis=0, keepdims=True), axis=1, keepdims=True) * (0.5 / d)

    row = pl.BlockSpec((tm, d), lambda i: (i, 0))
    (l, dy, dyb), _ = _call("loss_head", body, (s // tm,), [row, row], (y, target), [pl.BlockSpec((1, 1), lambda i: (0, 0)), row, row],
                            [jax.ShapeDtypeStruct((1, 1), F32), jax.ShapeDtypeStruct((s, d), F32), jax.ShapeDtypeStruct((s, d), BF16)],
                            (), ("arbitrary",))
    return l[0, 0], dy, dyb


def _row_tile(rows, cols, itemsize, block_bytes=1 << 20):
    target = max(16, block_bytes // (cols * itemsize))
    fits = [t for t in range(16, rows + 1, 16) if rows % t == 0 and t <= target]
    return fits[-1] if fits else rows


CAST_STEPS = 8


def cast_place(name, ws, p_idx, rider=None):
    n = len(ws)
    assert all(w.shape[0] % (16 * CAST_STEPS) == 0 for w in ws), [w.shape for w in ws]

    def body(p_ref, *refs):
        for w_ref, o_ref in zip(refs[:n], refs[n:]):
            o_ref[...] = w_ref[...].astype(BF16)

    return _call(
        name, body, (CAST_STEPS,), [pl.BlockSpec((w.shape[0] // CAST_STEPS, w.shape[1]), lambda i, pr: (i, 0)) for w in ws], tuple(ws),
        [pl.BlockSpec((None, w.shape[0] // CAST_STEPS, w.shape[1]), lambda i, pr: (pr[0], i, 0)) for w in ws],
        [jax.ShapeDtypeStruct((N_CHIPS,) + w.shape, BF16) for w in ws], (), ("parallel",), rider, prefetch=(p_idx,))


def chip_sum(name, grad, theirs, c_idx):
    ns, r, cols = grad.shape
    rh = r // 2
    tr = _row_tile(rh, cols, 2, 2 << 20)
    nb = rh // tr

    def body(c_ref, a_ref, b_ref, o_ref):
        o_ref[...] = (a_ref[...].astype(F32) + b_ref[...].astype(F32)).astype(o_ref.dtype)

    return pl.pallas_call(
        body, name=name,
        grid_spec=pltpu.PrefetchScalarGridSpec(
            num_scalar_prefetch=1, grid=(ns, nb),
            in_specs=[pl.BlockSpec((None, tr, cols), lambda q, i, cr: (q, cr[0] * nb + i, 0)),
                      pl.BlockSpec((None, tr, cols), lambda q, i, cr: (q, i, 0))],
            out_specs=pl.BlockSpec((None, tr, cols), lambda q, i, cr: (q, i, 0))),
        out_shape=jax.ShapeDtypeStruct((ns, rh, cols), BF16),
        compiler_params=pltpu.CompilerParams(dimension_semantics=("parallel", "parallel"), vmem_limit_bytes=VMEM_LIMIT),
    )(c_idx, grad, theirs)


def owner_sum(name, sums, got, pc_idx):
    ns, rh, cols = sums.shape
    tr = _row_tile(rh, cols, 4, 2 << 20)
    nb = rh // tr

    def body(pc_ref, a_ref, b_ref, o_ref):
        o_ref[...] = ((a_ref[...].astype(F32) + b_ref[0].astype(F32)) + b_ref[1].astype(F32)) + b_ref[2].astype(F32)

    return pl.pallas_call(
        body, name=name,
        grid_spec=pltpu.PrefetchScalarGridSpec(
            num_scalar_prefetch=1, grid=(nb,),
            in_specs=[pl.BlockSpec((None, tr, cols), lambda i, pc: (pc[0], i, 0)),
                      pl.BlockSpec((3, tr, cols), lambda i, pc: (0, i, 0))],
            out_specs=pl.BlockSpec((tr, cols), lambda i, pc: (pc[1] * nb + i, 0))),
        out_shape=jax.ShapeDtypeStruct((2 * rh, cols), F32),
        compiler_params=pltpu.CompilerParams(dimension_semantics=("parallel",), vmem_limit_bytes=VMEM_LIMIT),
    )(pc_idx, sums, got)


def adamw(name, w, g, m, v):
    r, cols = w.shape
    tr = _row_tile(r, cols, 4)
    c1 = 1.0 / (1.0 - ADAM_B1 ** ADAM_STEP)
    c2 = 1.0 / (1.0 - ADAM_B2 ** ADAM_STEP)

    def body(w_ref, g_ref, m_ref, v_ref, go_ref, d_ref, nm_ref, nv_ref):
        gv = g_ref[...]
        nm = ADAM_B1 * m_ref[...] + (1.0 - ADAM_B1) * gv
        nv = ADAM_B2 * v_ref[...] + (1.0 - ADAM_B2) * (gv * gv)
        go_ref[...] = gv
        d_ref[...] = -ADAM_LR * ((nm * c1) / (jnp.sqrt(nv * c2) + ADAM_EPS) + ADAM_WD * w_ref[...])
        nm_ref[...] = nm
        nv_ref[...] = nv

    blk = pl.BlockSpec((tr, cols), lambda i: (i, 0))
    return _call(name, body, (r // tr,), [blk] * 4, (w, g, m, v), [blk] * 4, [jax.ShapeDtypeStruct((r, cols), F32)] * 4, (), ("parallel",))


def _win_layout(d_model):
    hf = hq = d_model // (2 * HEAD_DIM)
    hk = hq // 4
    sizes = [hf * HEAD_DIM, hf * HEAD_DIM, hf * HEAD_DIM, hf, hq * HEAD_DIM, hk * HEAD_DIM, hk * HEAD_DIM]
    return hf, hq, hk, sizes


class WinPlan:
    def __init__(self, d_model, ns=N_CHIPS):
        self.hf, self.hq, self.hk, self.sizes = _win_layout(d_model)
        self.ns, self.cs = ns, sum(self.sizes) // ns
        self.jump_at = sum(self.sizes[:4])
        self.jump_by = -self.jump_at % _LANES
        self.base = [self.pos(s * self.cs) // _LANES * _LANES for s in range(ns)]
        ends = [self.pos((s + 1) * self.cs - 1) + 1 - self.base[s] for s in range(ns)]
        self.width = -(-max(ends) // _LANES) * _LANES
        self.total = -(-max(b + self.width for b in self.base) // 1024) * 1024
        starts = [0]
        for sz in self.sizes:
            starts.append(starts[-1] + sz)
        self.segments = [(self.pos(a), sz) for a, sz in zip(starts, self.sizes)]

    def pos(self, g):
        return g if g < self.jump_at else g + self.jump_by

    def pieces(self, s):
        g0, g1 = s * self.cs, (s + 1) * self.cs
        cuts = [g0] + ([self.jump_at] if g0 < self.jump_at < g1 else []) + [g1]
        return [(a - g0, b - a, self.pos(a) - self.base[s]) for a, b in zip(cuts[:-1], cuts[1:])]

    def place(self, w, s):
        parts, at = [], 0
        for t0, n, j0 in self.pieces(s):
            parts += [jnp.zeros((w.shape[0], j0 - at), w.dtype), w[:, t0:t0 + n]]
            at = j0 + n
        return jnp.concatenate(parts + [jnp.zeros((w.shape[0], self.width - at), w.dtype)], axis=1)

    def unplace(self, slab, s):
        return jnp.concatenate([slab[:, j0:j0 + n] for _, n, j0 in self.pieces(s)], axis=1)

    def assemble(self, slabs):
        return sum(jnp.pad(slabs[s], ((0, 0), (b, self.total - b - self.width))) for s, b in enumerate(self.base))

    def split(self, full):
        return jnp.stack([full[:, b:b + self.width] for b in self.base])


def _attn_inputs(proj, sm, positions):
    s_len = proj.shape[0]
    plan = WinPlan(sm["norm_mix_g"].shape[0])
    hf, hq, hk = plan.hf, plan.hq, plan.hk
    grp = hq // hk
    q_f, k_f, v_f, f_logit, q_s, k_s, v_s = [proj[:, a:a + n] for a, n in plan.segments]

    q_f = head_rms("fox_qnorm", q_f, sm["fox_q_norm_g"])
    k_f = head_rms("fox_knorm", k_f, sm["fox_k_norm_g"])
    log_f = jax.nn.log_sigmoid(f_logit + sm["b_forget"])
    c = jnp.cumsum(log_f, axis=0).T

    inv_freq = ROPE_THETA ** (-jnp.arange(0, HEAD_DIM, 2, dtype=F32) / HEAD_DIM)
    ang = positions.astype(F32)[:, None] * inv_freq
    cos, sin = jnp.cos(ang), jnp.sin(ang)
    q_s = head_rms_rope("swa_qnorm", q_s, sm["swa_q_norm_g"], cos, sin)
    k_s = head_rms_rope("swa_knorm", k_s, sm["swa_k_norm_g"], cos, sin)
    sink = jnp.broadcast_to(sm["swa_sinks"].reshape(hk, grp, 1, 1), (hk, grp, WINDOW, 1)).reshape(hk, grp * WINDOW, 1)
    tq = fox_tile(s_len)
    return (q_f, k_f, v_f, c[:, :, None], c.reshape(hf, s_len // tq, 1, tq)), (q_s, k_s, v_s, sink)


_BIG = ("ffn1_w_gate", "ffn1_w_up", "ffn1_w_down", "w_in", "w_out", "ffn2_w_gate", "ffn2_w_up", "ffn2_w_down")
_ROW_SHARDED = ("ffn1_w_down", "w_out", "ffn2_w_down")
_SMALL = ("norm_ffn1_g", "norm_mix_g", "b_forget", "fox_q_norm_g", "fox_k_norm_g", "swa_q_norm_g", "swa_k_norm_g", "swa_sinks",
          "out_norm_fox_g", "out_norm_swa_g", "norm_ffn2_g")
_ATTN_SMALL = ("norm_mix_g", "b_forget", "fox_q_norm_g", "fox_k_norm_g", "swa_q_norm_g", "swa_k_norm_g", "swa_sinks")
_ALL = ("norm_ffn1_g", "ffn1_w_gate", "ffn1_w_up", "ffn1_w_down", "norm_mix_g", "w_in", "b_forget", "fox_q_norm_g", "fox_k_norm_g",
        "swa_q_norm_g", "swa_k_norm_g", "swa_sinks", "out_norm_fox_g", "out_norm_swa_g", "w_out", "norm_ffn2_g", "ffn2_w_gate",
        "ffn2_w_up", "ffn2_w_down")


def _pack_small(d):
    parts = []
    for k in _SMALL:
        v = d[k].reshape(-1)
        rows = -(-v.shape[0] // _LANES)
        parts.append(jnp.pad(v, (0, rows * _LANES - v.shape[0])).reshape(rows, _LANES))
    a = jnp.concatenate(parts, axis=0)
    return jnp.pad(a, ((0, -a.shape[0] % 8), (0, 0)))


def _unpack_small(a, like):
    out, r0 = {}, 0
    for k in _SMALL:
        nvals = like[k].shape[1]
        rows = -(-nvals // _LANES)
        out[k] = a[r0:r0 + rows].reshape(-1)[:nvals].reshape(1, nvals)
        r0 += rows
    return out


def _stacked(w):
    return w.reshape(-1, w.shape[-1])


def _local_step(shards, sm, x, positions, target, p_idx, c_idx, pc_idx):
    ns = N_CHIPS
    full = {}

    def fetch(*jobs):
        names = list(dict.fromkeys(n for n, _, _ in jobs))
        return names, gather([bufs[n] for n in names], [(names.index(n), kind, part) for n, kind, part in jobs])

    def take(names, rid):
        for n, b in zip(names, rid[0]):
            bufs[n] = b

    n1 = ["ffn1_w_gate", "ffn1_w_up", "ffn1_w_down"]
    n2 = ["ffn2_w_gate", "ffn2_w_up", "ffn2_w_down"]
    later = ["w_in", "w_out"] + n2
    placed, _ = cast_place("cast_place_ffn1", [shards[n] for n in n1], p_idx)
    bufs = dict(zip(n1, placed))
    gate1, up1, down1 = n1
    gate2, up2, down2 = n2
    names, rider = fetch((gate1, "ici", WHOLE), (up1, "ici", WHOLE))
    placed, rid = cast_place("cast_place_later", [shards[n] for n in later], p_idx, rider=rider)
    bufs.update(zip(later, placed))
    take(names, rid)
    names, rider = fetch((gate1, "d2d", WHOLE), (up1, "d2d", WHOLE))
    xn1, r1, rid = rms_fwd("ffn1_norm", x, sm["norm_ffn1_g"], BF16, rider=rider)
    take(names, rid)
    names, rider = fetch((down1, "ici", WHOLE))
    (hid1, hdg1, hdu1), rid = ffn_gu("ffn1_gu", xn1, bufs[gate1], bufs[up1], rider=rider)
    take(names, rid)
    names, rider = fetch((down1, "d2d", WHOLE))
    take(names, run_step("gather_d2d_ffn1_down", rider))
    wd1 = _stacked(bufs[down1])
    names, rider = fetch(("w_in", "ici", WHOLE))
    h1, rid = mm_nn("ffn1_down", hid1, wd1, scale=0.5, resid=x, rider=rider)
    take(names, rid)

    names, rider = fetch(("w_in", "d2d", WHOLE))
    u, r_mix, rid = rms_fwd("mix_norm", h1, sm["norm_mix_g"], BF16, rider=rider)
    take(names, rid)
    names, rider = fetch(("w_out", "ici", WHOLE), (gate2, "ici", (0, 1, 4)))
    plan = WinPlan(x.shape[1])
    win = plan.assemble(bufs["w_in"])
    proj, rid = mm_nn("mix_inproj", u, win, rider=rider)
    take(names, rid)
    sm_attn = {k: sm[k] for k in _ATTN_SMALL}
    (fox_in, swa_in), attn_vjp = jax.vjp(lambda pr, s: _attn_inputs(pr, s, positions), proj, sm_attn)
    names, rider = fetch((gate2, "ici", (1, 4, 4)), ("w_out", "d2d", WHOLE), (gate2, "d2d", (0, 1, 4)))
    (o_f, lse), rid = fox_fwd(*fox_in, rider=rider)
    take(names, rid)
    names, rider = fetch((up2, "ici", (0, 3, 4)), (gate2, "d2d", (1, 4, 4)))
    o_s, rid = swa_fwd(*swa_in, rider=rider)
    take(names, rid)
    o_fox, o_swa = o_f, o_s
    nf, r_fox = rms_fwd("out_norm_fox", o_fox, sm["out_norm_fox_g"], BF16)
    nsw, r_swa = rms_fwd("out_norm_swa", o_swa, sm["out_norm_swa_g"], BF16)
    o = jnp.concatenate([nf, nsw], axis=-1)
    wout = _stacked(bufs["w_out"])
    names, rider = fetch((up2, "ici", (3, 4, 4)), (up2, "d2d", (0, 3, 4)))
    h2, rid = mm_nn("out_proj", o, wout, resid=h1, rider=rider)
    take(names, rid)

    names, rider = fetch((up2, "d2d", (3, 4, 4)))
    xn2, r2, rid = rms_fwd("ffn2_norm", h2, sm["norm_ffn2_g"], BF16, rider=rider)
    take(names, rid)
    names, rider = fetch((down2, "ici", WHOLE))
    (hid2, hdg2, hdu2), rid = ffn_gu("ffn2_gu", xn2, bufs[gate2], bufs[up2], rider=rider)
    take(names, rid)
    names, rider = fetch((down2, "d2d", WHOLE))
    take(names, run_step("gather_d2d_ffn2_down", rider))
    wd2 = _stacked(bufs["ffn2_w_down"])
    y, _ = mm_nn("ffn2_down", hid2, wd2, scale=0.5, resid=h2)
    loss, dy, dy_b = loss_call(y, target)

    red = {}

    def grad(n, g):
        red[n] = {"grad": g.reshape(ns, -1, g.shape[-1])}

    def ride(*steps):
        def done(rid):
            a0 = n0 = 0
            for rd, cb in steps:
                cb(rid[0][a0:a0 + len(rd.aliased)], rid[1][n0:n0 + len(rd.news)])
                a0, n0 = a0 + len(rd.aliased), n0 + len(rd.news)

        return (combine(*[s[0] for s in steps]) if len(steps) > 1 else steps[0][0]), done

    def xchg(*names):
        def cb(al, news):
            for n, t in zip(names, news):
                red[n]["sum"] = chip_sum("chip_sum_" + n, red[n]["grad"], t, c_idx)

        return exchange_halves([red[n]["grad"] for n in names]), cb

    def scat(n, part=WHOLE):
        def cb(al, news):
            red[n]["got"] = (al or news)[0]

        return scatter_to_owner([red[n]["sum"]], [red[n]["got"]] if "got" in red[n] else None, part), cb

    def own(n):
        red[n]["half"] = owner_sum("owner_sum_" + n, red[n]["sum"], red[n]["got"], pc_idx)

    def join(*names):
        return join_halves([red[n]["half"] for n in names]), lambda al, news: full.update(zip(names, al))

    dwd2, _ = mm_tn("ffn2_dwd", hid2, dy_b, out_dtype=BF16, scale=0.5)
    grad(down2, dwd2)
    rider, done = ride(xchg(down2))
    (dg2, du2), rid = ffn_dh("ffn2_dh", dy_b, wd2, hdg2, hdu2, ns, 0.5, rider=rider)
    done(rid)
    rider, done = ride(scat(down2, (0, 1, 2)))
    dwg2, rid = mm_tn_sharded("ffn2_dwg", xn2, dg2, ns, rider=rider)
    done(rid)
    grad(gate2, dwg2)
    rider, done = ride(scat(down2, (1, 2, 2)), xchg(gate2))
    dwu2, rid = mm_tn_sharded("ffn2_dwu", xn2, du2, ns, rider=rider)
    done(rid)
    grad(up2, dwu2)
    rider, done = ride(scat(gate2, (0, 1, 2)), xchg(up2))
    dxn, rid = mm_nt_sharded("ffn2_dxn_g", dg2, bufs[gate2], rider=rider)
    done(rid)
    rider, done = ride(scat(gate2, (1, 2, 2)))
    dxn, rid = mm_nt_sharded("ffn2_dxn_u", du2, bufs[up2], resid=dxn, rider=rider)
    done(rid)
    dh2, dgain_ffn2, dh2_b = rms_bwd("ffn2_dnorm", h2, sm["norm_ffn2_g"], r2, dxn, dres=dy, also_bf16=True)
    own(down2)
    own(gate2)

    do, _ = mm_nt("out_do", dh2_b, wout)
    dwout, _ = mm_tn("out_dw", o, dh2_b, out_dtype=BF16)
    cf = o_fox.shape[1]
    d_fox, dgain_fox = rms_bwd("out_dnorm_fox", o_fox, sm["out_norm_fox_g"], r_fox, do[:, :cf])
    d_swa, dgain_swa = rms_bwd("out_dnorm_swa", o_swa, sm["out_norm_swa_g"], r_swa, do[:, cf:])
    grad("w_out", dwout)
    rider, done = ride(scat(up2))
    swa_cts, rid = swa_bwd(*swa_in, o_s, d_swa, rider=rider)
    done(rid)
    own(up2)
    rider, done = ride(xchg("w_out"), join(down2, gate2, up2))
    fox_cts, rid = fox_bwd(*fox_in, o_f, lse, d_fox, rider=rider)
    done(rid)
    dproj, dsm_attn = attn_vjp((tuple(fox_cts), tuple(swa_cts)))
    dproj = dproj.astype(BF16)

    rider, done = ride(scat("w_out"))
    du, rid = mm_nt("mix_du", dproj, win, rider=rider)
    done(rid)
    dwin, _ = mm_tn("mix_dwin", u, dproj, out_dtype=BF16)
    grad("w_in", plan.split(dwin))
    rider, done = ride(xchg("w_in"))
    dh1, dgain_mix, dh1_b, rid = rms_bwd("mix_dnorm", h1, sm["norm_mix_g"], r_mix, du, dres=dh2, rider=rider, also_bf16=True)
    done(rid)
    own("w_out")

    rider, done = ride(scat("w_in", (0, 1, 2)))
    dwd1, rid = mm_tn("ffn1_dwd", hid1, dh1_b, out_dtype=BF16, scale=0.5, rider=rider)
    done(rid)
    grad(down1, dwd1)
    rider, done = ride(scat("w_in", (1, 2, 2)), xchg(down1))
    (dg1, du1), rid = ffn_dh("ffn1_dh", dh1_b, wd1, hdg1, hdu1, ns, 0.5, rider=rider)
    done(rid)
    own("w_in")
    rider, done = ride(scat(down1, (0, 1, 2)), join("w_out"))
    dwg1, rid = mm_tn_sharded("ffn1_dwg", xn1, dg1, ns, rider=rider)
    done(rid)
    grad(gate1, dwg1)
    rider, done = ride(scat(down1, (1, 2, 2)), xchg(gate1), join("w_in"))
    dwu1, rid = mm_tn_sharded("ffn1_dwu", xn1, du1, ns, rider=rider)
    done(rid)
    grad(up1, dwu1)
    own(down1)
    rider, done = ride(scat(gate1, (0, 1, 2)), xchg(up1), join(down1))
    dxn, rid = mm_nt_sharded("ffn1_dxn_g", dg1, bufs[gate1], rider=rider)
    done(rid)
    rider, done = ride(scat(gate1, (1, 2, 2)), scat(up1, (0, 1, 4)))
    dxn, rid = mm_nt_sharded("ffn1_dxn_u", du1, bufs[up1], resid=dxn, rider=rider)
    done(rid)
    dx, dgain_ffn1 = rms_bwd("ffn1_dnorm", x, sm["norm_ffn1_g"], r1, dxn, dres=dh1)
    own(gate1)

    rider, done = ride(scat(up1, (1, 4, 4)), join(gate1))
    done(run_step("reduce_tail", rider))
    own(up1)
    rider, done = ride(join(up1))
    done(run_step("join_tail", rider))

    g_small = dict(dsm_attn)
    g_small["norm_mix_g"] = g_small["norm_mix_g"] + dgain_mix
    g_small.update(norm_ffn1_g=dgain_ffn1, norm_ffn2_g=dgain_ffn2, out_norm_fox_g=dgain_fox, out_norm_swa_g=dgain_swa)
    return loss, dx, full, g_small


def kernel(x, positions, norm_ffn1_g, ffn1_w_gate, ffn1_w_up, ffn1_w_down, norm_mix_g, w_in, b_forget, fox_q_norm_g, fox_k_norm_g, swa_q_norm_g, swa_k_norm_g, swa_sinks, out_norm_fox_g, out_norm_swa_g, w_out, norm_ffn2_g, ffn2_w_gate, ffn2_w_up, ffn2_w_down, loss_target, m_norm_ffn1_g, m_ffn1_w_gate, m_ffn1_w_up, m_ffn1_w_down, m_norm_mix_g, m_w_in, m_b_forget, m_fox_q_norm_g, m_fox_k_norm_g, m_swa_q_norm_g, m_swa_k_norm_g, m_swa_sinks, m_out_norm_fox_g, m_out_norm_swa_g, m_w_out, m_norm_ffn2_g, m_ffn2_w_gate, m_ffn2_w_up, m_ffn2_w_down, v_norm_ffn1_g, v_ffn1_w_gate, v_ffn1_w_up, v_ffn1_w_down, v_norm_mix_g, v_w_in, v_b_forget, v_fox_q_norm_g, v_fox_k_norm_g, v_swa_q_norm_g, v_swa_k_norm_g, v_swa_sinks, v_out_norm_fox_g, v_out_norm_swa_g, v_w_out, v_norm_ffn2_g, v_ffn2_w_gate, v_ffn2_w_up, v_ffn2_w_down):
    args = dict(locals())
    w = {k: args[k] for k in _ALL}
    m = {k: args["m_" + k] for k in _ALL}
    v = {k: args["v_" + k] for k in _ALL}
    c_idx = lax.axis_index("c").astype(jnp.int32).reshape(1)
    p_idx = (2 * lax.axis_index("x") + lax.axis_index("y")).astype(jnp.int32).reshape(1)
    pc_idx = jnp.concatenate([p_idx, c_idx])

    small = {k: w[k] for k in _SMALL}
    shards = {k: w[k][0] for k in _BIG}
    plan = WinPlan(x.shape[-1])
    shards["w_in"] = lax.switch(p_idx[0], [functools.partial(plan.place, s=s) for s in range(N_CHIPS)], shards["w_in"])
    loss, grad_x, g_shard, g_small = _local_step(shards, {k: w[k][0] for k in _SMALL}, x[0], positions[0], loss_target[0],
                                                 p_idx, c_idx, pc_idx)
    g_shard["w_in"] = lax.switch(p_idx[0], [functools.partial(plan.unplace, s=s) for s in range(N_CHIPS)], g_shard["w_in"])
    loss = lax.psum(loss, ("x", "y", "c"))
    g_small_sum = _unpack_small(all_reduce_small(_pack_small({k: g_small[k].reshape(1, -1) for k in _SMALL})), small)

    grad_w, delta, new_m, new_v = {}, {}, {}, {}
    for k in _BIG:
        (g, d, nm, nv), _ = adamw("adamw_" + k, w[k][0], g_shard[k], m[k][0], v[k][0])
        grad_w[k], delta[k], new_m[k], new_v[k] = g[None], d[None], nm[None], nv[None]
    (_, d, nm, nv), _ = adamw("adamw_small", _pack_small(small), _pack_small(g_small_sum), _pack_small({k: m[k] for k in _SMALL}),
                              _pack_small({k: v[k] for k in _SMALL}))
    grad_w.update(g_small_sum)
    delta.update(_unpack_small(d, small))
    new_m.update(_unpack_small(nm, small))
    new_v.update(_unpack_small(nv, small))

    return (loss, grad_x[None], *[grad_w[k] for k in _ALL], *[delta[k] for k in _ALL], *[new_m[k] for k in _ALL], *[new_v[k] for k in _ALL])
```

```python
import functools

import jax
import jax.numpy as jnp
from jax import lax
from jax.experimental import pallas as pl
from jax.experimental.pallas import tpu as pltpu

F32 = jnp.float32
BF16 = jnp.bfloat16

HEAD_DIM = 64
WINDOW = 128
ROPE_THETA = 10000.0
EPS = 1e-6
N_CHIPS = 4
N_DEV = 8

ADAM_LR = 0.001
ADAM_B1 = 0.9
ADAM_B2 = 0.999
ADAM_EPS = 1e-08
ADAM_WD = 0.01
ADAM_STEP = 10

V7X_VMEM_BYTES = 64 * 1024 * 1024
VMEM_LIMIT = V7X_VMEM_BYTES - 8 * 1024 * 1024
MASK_VALUE = -1e30

_MESH = pl.DeviceIdType.MESH
_HBM = pl.BlockSpec(memory_space=pl.ANY)
_DIMS = {"nn": (((1,), (0,)), ((), ())), "nt": (((1,), (1,)), ((), ())), "tn": (((0,), (0,)), ((), ()))}


def _pick(n, prefs):
    for p in prefs:
        if n % p == 0:
            return p
    return n


class Rider:
    def __init__(self, reads, aliased, news, nsem, build):
        self.reads, self.aliased, self.news, self.nsem, self.build = list(reads), list(aliased), list(news), nsem, build


class _Shifted:
    def __init__(self, ref, off):
        self.ref, self.off = ref, off

    @property
    def at(self):
        return self

    def __getitem__(self, k):
        return self.ref.at[k + self.off]


def combine(*riders):
    def build(reads, al, news, ssem, rsem):
        out = ([], [], [])
        r0 = a0 = n0 = s0 = 0
        for rd in riders:
            nr, na, nn = len(rd.reads), len(rd.aliased), len(rd.news)
            part = rd.build(reads[r0:r0 + nr], al[a0:a0 + na], news[n0:n0 + nn], _Shifted(ssem, s0), _Shifted(rsem, s0))
            for acc, lst in zip(out, part):
                acc.extend(lst)
            r0, a0, n0, s0 = r0 + nr, a0 + na, n0 + nn, s0 + rd.nsem
        return out

    return Rider(sum((r.reads for r in riders), []), sum((r.aliased for r in riders), []), sum((r.news for r in riders), []),
                 sum(r.nsem for r in riders), build)


def _me():
    return lax.axis_index("x"), lax.axis_index("y"), lax.axis_index("c")


def _other_chips(x, y):
    return [(1 - x, y), (x, 1 - y), (1 - x, 1 - y)]


WHOLE = (0, 1, 1)


def _rows(ref, start, rows, part=WHOLE):
    k0, k1, n = part
    assert rows % n == 0, (rows, part)
    idx = (slice(None),) * (len(ref.shape) - 2) + (pl.ds(start + k0 * (rows // n), (k1 - k0) * (rows // n)), slice(None))
    return ref.at[idx]


def _half(ref, h, part=WHOLE):
    rows = ref.shape[-2] // 2
    return _rows(ref, h * rows, rows, part)


def _remote(src, dst, ssem, rsem, k, to):
    return pltpu.make_async_remote_copy(src_ref=src, dst_ref=dst, send_sem=ssem.at[k], recv_sem=rsem.at[k], device_id=to,
                                        device_id_type=_MESH)


def _later(*args):
    return functools.partial(_remote, *args)


def gather(bufs, jobs):
    def build(reads, al, news, ssem, rsem):
        x, y, c = _me()
        p = 2 * x + y
        starts, arrivals = [], []
        for n, (b, kind, part) in enumerate(jobs):
            for j, chip in enumerate(_other_chips(x, y)):
                q = 2 * chip[0] + chip[1]
                if kind == "ici":
                    src, landing, to = _half(al[b].at[p], c, part), _half(al[b].at[q], c, part), (*chip, c)
                else:
                    src, landing, to = _half(al[b].at[q], c, part), _half(al[b].at[q], 1 - c, part), (x, y, 1 - c)
                starts.append(_later(src, src, ssem, rsem, 3 * n + j, to))
                arrivals.append(_later(landing, landing, ssem, rsem, 3 * n + j, to))
        return starts, arrivals, starts

    return Rider([], bufs, [], 3 * len(jobs), build)


def exchange_halves(grads):
    def build(reads, al, news, ssem, rsem):
        x, y, c = _me()
        cps = [_later(_half(g, 1 - c), t, ssem, rsem, w, (x, y, 1 - c)) for w, (g, t) in enumerate(zip(reads, news))]
        return cps, cps, cps

    return Rider(grads, [], [jax.ShapeDtypeStruct((g.shape[0], g.shape[1] // 2, g.shape[2]), g.dtype) for g in grads], len(grads), build)


def scatter_to_owner(sums, gots=None, part=WHOLE):
    def build(reads, al, news, ssem, rsem):
        x, y, c = _me()
        cps = []
        for w, (s, got) in enumerate(zip(reads, al or news)):
            rows = s.shape[-2]
            for j, chip in enumerate(_other_chips(x, y)):
                cps.append(_later(_rows(s.at[2 * chip[0] + chip[1]], 0, rows, part), _rows(got.at[j], 0, rows, part), ssem, rsem,
                                  3 * w + j, (*chip, c)))
        return cps, cps, cps

    news = [] if gots else [jax.ShapeDtypeStruct((3,) + s.shape[1:], s.dtype) for s in sums]
    return Rider(sums, gots or [], news, 3 * len(sums), build)


def join_halves(fulls):
    def build(reads, al, news, ssem, rsem):
        x, y, c = _me()
        starts, arrivals = [], []
        for w, f in enumerate(al):
            mine, landing = _half(f, c), _half(f, 1 - c)
            starts.append(_later(mine, mine, ssem, rsem, w, (x, y, 1 - c)))
            arrivals.append(_later(landing, landing, ssem, rsem, w, (x, y, 1 - c)))
        return starts, arrivals, starts

    return Rider([], fulls, [], len(fulls), build)


def _start_and_wait(rider, reads, al, news, ssem, rsem, first, last):
    @pl.when(first)
    def _():
        for cp in rider.build(reads, al, news, ssem, rsem)[0]:
            cp().start()

    def finish():
        @pl.when(last)
        def _():
            _, arrivals, sends = rider.build(reads, al, news, ssem, rsem)
            for cp in arrivals:
                cp().wait_recv()
            for cp in sends:
                cp().wait_send()

    return finish


def _call(name, body, grid, in_specs, args, out_specs, out_shape, scratch=(), semantics=None, rider=None, prefetch=()):
    n_pre, n_in, n_out, n_scr = len(prefetch), len(args), len(out_shape), len(scratch)
    nr, na, nn = (len(rider.reads), len(rider.aliased), len(rider.news)) if rider else (0, 0, 0)

    def wrapped(*refs):
        pre, refs = refs[:n_pre], refs[n_pre:]
        ins, reads = refs[:n_in], refs[n_in:n_in + nr]
        o0 = n_in + nr + na
        outs, al, news = refs[o0:o0 + n_out], refs[o0 + n_out:o0 + n_out + na], refs[o0 + n_out + na:o0 + n_out + na + nn]
        s0 = o0 + n_out + na + nn
        scr, (ssem, rsem) = refs[s0:s0 + n_scr], refs[s0 + n_scr:]
        first = functools.reduce(jnp.logical_and, [pl.program_id(a) == 0 for a in range(len(grid))])
        last = functools.reduce(jnp.logical_and, [pl.program_id(a) == g - 1 for a, g in enumerate(grid)])
        finish = _start_and_wait(rider, reads, al, news, ssem, rsem, first, last)
        body(*pre, *ins, *outs, *scr)
        finish()

    kernel_fn, all_in, all_out, shapes, scr = body, list(in_specs), list(out_specs), list(out_shape), list(scratch)
    operands, aliases = (*prefetch, *args), {}
    if rider:
        kernel_fn, semantics = wrapped, ("arbitrary",) * len(grid)
        all_in += [_HBM] * (nr + na)
        all_out += [_HBM] * (na + nn)
        shapes += [jax.ShapeDtypeStruct(a.shape, a.dtype) for a in rider.aliased] + rider.news
        scr += [pltpu.SemaphoreType.DMA((rider.nsem,)), pltpu.SemaphoreType.DMA((rider.nsem,))]
        operands += (*rider.reads, *rider.aliased)
        aliases = {n_pre + n_in + nr + i: n_out + i for i in range(na)}
    params = pltpu.CompilerParams(dimension_semantics=semantics, vmem_limit_bytes=VMEM_LIMIT)
    if n_pre:
        spec = pltpu.PrefetchScalarGridSpec(num_scalar_prefetch=n_pre, grid=grid, in_specs=all_in, out_specs=all_out, scratch_shapes=scr)
        outs = pl.pallas_call(kernel_fn, name=name, grid_spec=spec, out_shape=shapes, input_output_aliases=aliases, compiler_params=params)(*operands)
    else:
        outs = pl.pallas_call(kernel_fn, name=name, grid=grid, in_specs=all_in, out_specs=all_out, out_shape=shapes, scratch_shapes=scr,
                              input_output_aliases=aliases, compiler_params=params)(*operands)
    return list(outs[:n_out]), ((list(outs[n_out:n_out + na]), list(outs[n_out + na:])) if rider else None)


def run_step(name, rider):
    nr, na, nn = len(rider.reads), len(rider.aliased), len(rider.news)

    def body(*refs):
        reads = refs[:nr]
        al, news = refs[nr + na:nr + 2 * na], refs[nr + 2 * na:nr + 2 * na + nn]
        ssem, rsem = refs[nr + 2 * na + nn:]
        starts, arrivals, sends = rider.build(reads, al, news, ssem, rsem)
        for cp in starts:
            cp().start()
        for cp in arrivals:
            cp().wait_recv()
        for cp in sends:
            cp().wait_send()

    outs = pl.pallas_call(
        body, name=name, in_specs=[_HBM] * (nr + na), out_specs=[_HBM] * (na + nn),
        out_shape=[jax.ShapeDtypeStruct(a.shape, a.dtype) for a in rider.aliased] + rider.news,
        input_output_aliases={nr + i: i for i in range(na)},
        scratch_shapes=[pltpu.SemaphoreType.DMA((rider.nsem,)), pltpu.SemaphoreType.DMA((rider.nsem,))],
    )(*rider.reads, *rider.aliased)
    return list(outs[:na]), list(outs[na:])


def all_reduce_small(v):
    rows, lanes = v.shape

    def body(v_ref, o_ref, slots, send_sems, recv_sems):
        x, y, c = _me()
        me = 4 * x + 2 * y + c
        slots[me] = v_ref[...]
        cps = []
        for k in range(1, N_DEV):
            peer = (x ^ (k >> 2), y ^ ((k >> 1) & 1), c ^ (k & 1))
            cps.append(_remote(v_ref, slots.at[me], send_sems, recv_sems, k - 1, peer))
            cps[-1].start()
        for k in range(1, N_DEV):
            theirs = slots.at[me ^ k]
            _remote(theirs, theirs, send_sems, recv_sems, k - 1, (x, y, c)).wait_recv()
        for cp in cps:
            cp.wait_send()
        acc = slots[0]
        for i in range(1, N_DEV):
            acc = acc + slots[i]
        o_ref[...] = acc

    return pl.pallas_call(
        body, name="all_reduce_small",
        in_specs=[pl.BlockSpec(memory_space=pltpu.VMEM)], out_specs=pl.BlockSpec(memory_space=pltpu.VMEM),
        out_shape=jax.ShapeDtypeStruct((rows, lanes), F32),
        scratch_shapes=[pltpu.VMEM((N_DEV, rows, lanes), F32), pltpu.SemaphoreType.DMA((N_DEV - 1,)), pltpu.SemaphoreType.DMA((N_DEV - 1,))],
    )(v)


def _mm_call(name, mode, a, b, a_spec, b_spec, out_shape, out_spec, grid, acc_shape, scale=1.0, resid=None, resid_spec=None, rider=None):
    nk = grid[2]
    dims = _DIMS[mode]
    has_resid = resid is not None

    def body(*refs):
        a_ref, b_ref = refs[:2]
        r_ref = refs[2] if has_resid else None
        o_ref = refs[3] if has_resid else refs[2]

        def finish(r):
            if scale != 1.0:
                r = r * scale
            if has_resid:
                r = r_ref[...].astype(F32) + r
            o_ref[...] = r.astype(o_ref.dtype)

        part = lax.dot_general(a_ref[...].astype(BF16), b_ref[...].astype(BF16), dims, preferred_element_type=F32)
        if nk == 1:
            finish(part)
            return
        acc_ref = refs[-1]
        k = pl.program_id(2)

        @pl.when(k == 0)
        def _():
            acc_ref[...] = part

        @pl.when(k > 0)
        def _():
            acc_ref[...] += part

        @pl.when(k == nk - 1)
        def _():
            finish(acc_ref[...])

    in_specs = [a_spec, b_spec] + ([resid_spec] if has_resid else [])
    args = (a, b) + ((resid,) if has_resid else ())
    (out,), rid = _call(name, body, grid, in_specs, args, [out_spec], [out_shape], [pltpu.VMEM(acc_shape, F32)] if nk > 1 else [],
                        ("parallel", "parallel", "arbitrary"), rider)
    return out, rid


MM_VMEM_BUDGET = 40 * 1024 * 1024
_TILE_OPTS = (2048, 1408, 1024, 512, 256, 128)


def _tiles(m, n, kd, a_item, b_item, o_item, r_item=0, tm=None, tn=None, tk=None):
    def opts(full, fixed, cap):
        return [fixed] if fixed else [t for t in _TILE_OPTS if t <= cap and full % t == 0] or [full]

    best = None
    for cm in opts(m, tm, 1408):
        for cn in opts(n, tn, 1408):
            for ck in opts(kd, tk, 2048):
                blocks = cm * ck * a_item + ck * cn * b_item + cm * cn * (o_item + r_item)
                casts = (cm * ck * 2 if a_item == 4 else 0) + (ck * cn * 2 if b_item == 4 else 0)
                if 2 * blocks + cm * cn * 4 + casts <= MM_VMEM_BUDGET:
                    key = (cm * cn * ck, ck)
                    if best is None or key > best[0]:
                        best = (key, (cm, cn, ck))
    assert best is not None, (m, n, kd)
    return best[1]


def _item(x):
    return jnp.dtype(x.dtype).itemsize


def mm_nn(name, a, b, *, out_dtype=F32, scale=1.0, resid=None, rider=None):
    m, kd = a.shape
    n = b.shape[1]
    tm, tn, tk = _tiles(m, n, kd, _item(a), _item(b), jnp.dtype(out_dtype).itemsize, 0 if resid is None else _item(resid))
    o_spec = pl.BlockSpec((tm, tn), lambda i, j, k: (i, j))
    return _mm_call(
        name, "nn", a, b, pl.BlockSpec((tm, tk), lambda i, j, k: (i, k)), pl.BlockSpec((tk, tn), lambda i, j, k: (k, j)),
        jax.ShapeDtypeStruct((m, n), out_dtype), o_spec, (m // tm, n // tn, kd // tk), (tm, tn), scale, resid, o_spec, rider)


def mm_nt(name, a, b, *, out_dtype=F32, scale=1.0, resid=None, rider=None):
    m, kd = a.shape
    n = b.shape[0]
    tm, tn, tk = _tiles(m, n, kd, _item(a), _item(b), jnp.dtype(out_dtype).itemsize, 0 if resid is None else _item(resid))
    o_spec = pl.BlockSpec((tm, tn), lambda i, j, k: (i, j))
    return _mm_call(
        name, "nt", a, b, pl.BlockSpec((tm, tk), lambda i, j, k: (i, k)), pl.BlockSpec((tn, tk), lambda i, j, k: (j, k)),
        jax.ShapeDtypeStruct((m, n), out_dtype), o_spec, (m // tm, n // tn, kd // tk), (tm, tn), scale, resid, o_spec, rider)


def mm_tn(name, a, b, *, out_dtype=F32, scale=1.0, rider=None):
    kd, m = a.shape
    n = b.shape[1]
    tm, tn, tk = _tiles(m, n, kd, _item(a), _item(b), jnp.dtype(out_dtype).itemsize)
    return _mm_call(
        name, "tn", a, b, pl.BlockSpec((tk, tm), lambda i, j, k: (k, i)), pl.BlockSpec((tk, tn), lambda i, j, k: (k, j)),
        jax.ShapeDtypeStruct((m, n), out_dtype), pl.BlockSpec((tm, tn), lambda i, j, k: (i, j)),
        (m // tm, n // tn, kd // tk), (tm, tn), scale, rider=rider)


def mm_nt_sharded(name, a, w, *, resid=None, rider=None):
    m = a.shape[0]
    ns, n, c = w.shape
    tm, tn, _ = _tiles(m, n, c, _item(a), _item(w), 4, 0 if resid is None else _item(resid), tk=c)
    o_spec = pl.BlockSpec((tm, tn), lambda i, j, k: (i, j))
    return _mm_call(
        name, "nt", a, w, pl.BlockSpec((tm, c), lambda i, j, k: (i, k)), pl.BlockSpec((None, tn, c), lambda i, j, k: (k, j, 0)),
        jax.ShapeDtypeStruct((m, n), F32), o_spec, (m // tm, n // tn, ns), (tm, tn), 1.0, resid, o_spec, rider)


def mm_tn_sharded(name, a, b, ns, *, rider=None):
    kd, m = a.shape
    c = b.shape[1] // ns
    tm, _, tk = _tiles(m, c, kd, _item(a), _item(b), 2, tn=c)
    return _mm_call(
        name, "tn", a, b, pl.BlockSpec((tk, tm), lambda i, j, k: (k, i)), pl.BlockSpec((tk, c), lambda i, j, k: (k, j)),
        jax.ShapeDtypeStruct((ns, m, c), BF16), pl.BlockSpec((None, tm, c), lambda i, j, k: (j, i, 0)),
        (m // tm, ns, kd // tk), (tm, c), rider=rider)


def rms_fwd(name, x, g, out_dtype, rider=None):
    r, c = x.shape
    tm = _pick(r, (512, 256, 128, 64, 8))

    def body(x_ref, g_ref, y_ref, r_ref):
        xf = x_ref[...].astype(F32)
        rstd = lax.rsqrt(jnp.mean(xf * xf, axis=-1, keepdims=True) + EPS)
        y_ref[...] = ((xf * rstd) * g_ref[...]).astype(y_ref.dtype)
        r_ref[...] = rstd

    (y, rstd), rid = _call(
        name, body, (r // tm,), [pl.BlockSpec((tm, c), lambda i: (i, 0)), pl.BlockSpec((1, c), lambda i: (0, 0))], (x, g.reshape(1, c)),
        [pl.BlockSpec((tm, c), lambda i: (i, 0)), pl.BlockSpec((tm, 1), lambda i: (i, 0))],
        [jax.ShapeDtypeStruct((r, c), out_dtype), jax.ShapeDtypeStruct((r, 1), F32)], (), ("parallel",), rider)
    return (y, rstd) if rider is None else (y, rstd, rid)


def rms_bwd(name, x, g, rstd, dy, dres=None, rider=None, also_bf16=False):
    r, c = x.shape
    tm = _pick(r, (512, 256, 128, 64, 8))
    has_res = dres is not None

    def body(*refs):
        x_ref, g_ref, r_ref, dy_ref = refs[:4]
        dres_ref = refs[4] if has_res else None
        dx_ref, dg_ref = refs[4 + has_res:6 + has_res]
        xhat = x_ref[...].astype(F32) * r_ref[...]
        dyf = dy_ref[...].astype(F32)
        gdy = dyf * g_ref[...]
        dx = r_ref[...] * (gdy - xhat * jnp.mean(gdy * xhat, axis=-1, keepdims=True))
        if has_res:
            dx = dx + dres_ref[...]
        dx_ref[...] = dx
        if also_bf16:
            refs[-1][...] = dx.astype(BF16)

        @pl.when(pl.program_id(0) == 0)
        def _():
            dg_ref[...] = jnp.zeros_like(dg_ref)

        dg_ref[...] += jnp.sum(dyf * xhat, axis=0, keepdims=True)

    row = pl.BlockSpec((tm, c), lambda i: (i, 0))
    in_specs = [row, pl.BlockSpec((1, c), lambda i: (0, 0)), pl.BlockSpec((tm, 1), lambda i: (i, 0)), row] + ([row] if has_res else [])
    args = (x, g.reshape(1, c), rstd, dy) + ((dres,) if has_res else ())
    outs, rid = _call(name, body, (r // tm,), in_specs, args, [row, pl.BlockSpec((1, c), lambda i: (0, 0))] + [row] * also_bf16,
                      [jax.ShapeDtypeStruct((r, c), F32), jax.ShapeDtypeStruct((1, c), F32)] + [jax.ShapeDtypeStruct((r, c), BF16)] * also_bf16,
                      (), ("arbitrary",), rider)
    return (outs[0], outs[1].reshape(c), *outs[2:], *([] if rider is None else [rid]))


_LANES = 128


def _head_mean(v):
    if v.shape[1] == HEAD_DIM:
        return jnp.mean(v, axis=-1, keepdims=True)
    low = lax.broadcasted_iota(jnp.int32, v.shape, 1) < HEAD_DIM
    lo = jnp.sum(jnp.where(low, v, 0.0), axis=-1, keepdims=True)
    hi = jnp.sum(jnp.where(low, 0.0, v), axis=-1, keepdims=True)
    return jnp.where(low, lo, hi) * (1.0 / HEAD_DIM)


def _head_groups(c):
    width = _LANES if c % _LANES == 0 else HEAD_DIM
    assert c % width == 0, c
    return width, [slice(k * width, (k + 1) * width) for k in range(c // width)]


def _head_gain(g, width):
    return jnp.tile(g.reshape(1, HEAD_DIM), (1, width // HEAD_DIM))


def _rotate_half(y):
    half = HEAD_DIM // 2
    first = lax.broadcasted_iota(jnp.int32, y.shape, 1) % HEAD_DIM < half
    return jnp.where(first, -pltpu.roll(y, y.shape[1] - half, axis=1), pltpu.roll(y, half, axis=1))


def _rope_tables(rope, width):
    return [jnp.tile(t, (1, 2 * width // HEAD_DIM)) for t in rope]


def head_rms_fwd(name, x, g, rope=None):
    s, c = x.shape
    tm = _pick(s, (256, 128, 8))
    width, groups = _head_groups(c)

    def body(x_ref, g_ref, *refs):
        y_ref = refs[-1]
        for sl in groups:
            xs = x_ref[:, sl]
            y = (xs * lax.rsqrt(_head_mean(xs * xs) + EPS)) * g_ref[...]
            if rope:
                y = y * refs[0][...] + _rotate_half(y) * refs[1][...]
            y_ref[:, sl] = y

    row = pl.BlockSpec((tm, c), lambda i: (i, 0))
    tab = pl.BlockSpec((tm, width), lambda i: (i, 0))
    tables = _rope_tables(rope, width) if rope else []
    (y,), _ = _call(name, body, (s // tm,), [row, pl.BlockSpec((1, width), lambda i: (0, 0))] + [tab] * len(tables),
                    (x, _head_gain(g, width), *tables), [row], [jax.ShapeDtypeStruct((s, c), F32)], (), ("parallel",))
    return y


def head_rms_bwd(name, x, g, dy, rope=None):
    s, c = x.shape
    tm = _pick(s, (256, 128, 8))
    width, groups = _head_groups(c)

    def body(x_ref, g_ref, dy_ref, *refs):
        dx_ref, dg_ref = refs[-2:]

        @pl.when(pl.program_id(0) == 0)
        def _():
            dg_ref[...] = jnp.zeros_like(dg_ref)

        for sl in groups:
            xs, dys = x_ref[:, sl], dy_ref[:, sl]
            if rope:
                dys = dys * refs[0][...] - _rotate_half(dys * refs[1][...])
            rstd = lax.rsqrt(_head_mean(xs * xs) + EPS)
            xhat = xs * rstd
            gdy = dys * g_ref[...]
            dx_ref[:, sl] = rstd * (gdy - xhat * _head_mean(gdy * xhat))
            dg_ref[...] += jnp.sum(dys * xhat, axis=0, keepdims=True)

    row = pl.BlockSpec((tm, c), lambda i: (i, 0))
    vec = pl.BlockSpec((1, width), lambda i: (0, 0))
    tab = pl.BlockSpec((tm, width), lambda i: (i, 0))
    tables = _rope_tables(rope, width) if rope else []
    (dx, dg), _ = _call(name, body, (s // tm,), [row, vec, row] + [tab] * len(tables), (x, _head_gain(g, width), dy, *tables), [row, vec],
                        [jax.ShapeDtypeStruct((s, c), F32), jax.ShapeDtypeStruct((1, width), F32)], (), ("arbitrary",))
    return dx, jnp.sum(dg.reshape(width // HEAD_DIM, HEAD_DIM), axis=0)


@functools.partial(jax.custom_vjp, nondiff_argnums=(0,))
def head_rms(name, x, g):
    return head_rms_fwd(name + "_fwd", x, g)


def _head_rms_fwd(name, x, g):
    return head_rms_fwd(name + "_fwd", x, g), (x, g)


def _head_rms_bwd(name, res, dy):
    return head_rms_bwd(name + "_bwd", *res, dy)


head_rms.defvjp(_head_rms_fwd, _head_rms_bwd)


@functools.partial(jax.custom_vjp, nondiff_argnums=(0,))
def head_rms_rope(name, x, g, cos, sin):
    return head_rms_fwd(name + "_fwd", x, g, (cos, sin))


def _head_rms_rope_fwd(name, x, g, cos, sin):
    return head_rms_fwd(name + "_fwd", x, g, (cos, sin)), (x, g, cos, sin)


def _head_rms_rope_bwd(name, res, dy):
    x, g, cos, sin = res
    return (*head_rms_bwd(name + "_bwd", x, g, dy, (cos, sin)), jnp.zeros_like(cos), jnp.zeros_like(sin))


head_rms_rope.defvjp(_head_rms_rope_fwd, _head_rms_rope_bwd)


def _cumsum_call(name, a, reverse):
    h, s = a.shape
    tb = _LANES
    assert s % tb == 0

    def body(a_ref, o_ref):
        t_in = lax.broadcasted_iota(jnp.int32, (tb, tb), 0)
        t_out = lax.broadcasted_iota(jnp.int32, (tb, tb), 1)
        tri = jnp.where((t_in >= t_out) if reverse else (t_in <= t_out), 1.0, 0.0).astype(BF16)
        carry = jnp.zeros((h, 1), F32)
        blocks = range(s // tb)
        for b in (reversed(blocks) if reverse else blocks):
            cols = slice(b * tb, (b + 1) * tb)
            block = a_ref[:, cols]
            rest, local = block, jnp.zeros((h, tb), F32)
            for _ in range(3):
                piece = rest.astype(BF16)
                local = local + jnp.dot(piece, tri, preferred_element_type=F32)
                rest = rest - piece.astype(F32)
            o_ref[:, cols] = local + carry
            carry = carry + jnp.sum(block, axis=1, keepdims=True)

    whole = pl.BlockSpec((h, s), lambda j: (0, 0))
    (out,), _ = _call(name, body, (1,), [whole], (a,), [whole], [jax.ShapeDtypeStruct((h, s), F32)], (), ("arbitrary",))
    return out


@jax.custom_vjp
def time_cumsum(a):
    return _cumsum_call("gate_cumsum", a, False)


def _time_cumsum_fwd(a):
    return _cumsum_call("gate_cumsum", a, False), None


def _time_cumsum_bwd(_, dc):
    return (_cumsum_call("gate_cumsum_bwd", dc, True),)


time_cumsum.defvjp(_time_cumsum_fwd, _time_cumsum_bwd)


FFN_TM = 512


def _sigmoid(x):
    return 1.0 / (1.0 + jnp.exp(-x))


def ffn_gu(name, xn, wg, wu, rider=None):
    s, d = xn.shape
    ns, _, c = wg.shape
    tm = _pick(s, (FFN_TM, 128))

    def body(x_ref, wg_ref, wu_ref, h_ref, a_ref, b_ref):
        xb = x_ref[...]
        gv = jnp.dot(xb, wg_ref[...], preferred_element_type=F32)
        uv = jnp.dot(xb, wu_ref[...], preferred_element_type=F32)
        sig = _sigmoid(gv)
        silu = gv * sig
        h_ref[...] = (silu * uv).astype(BF16)
        a_ref[...] = (uv * (sig * (1.0 + gv * (1.0 - sig)))).astype(BF16)
        b_ref[...] = silu.astype(BF16)

    w_spec = pl.BlockSpec((None, d, c), lambda j, i: (j, 0, 0))
    o_spec = pl.BlockSpec((tm, c), lambda j, i: (i, j))
    return _call(
        name, body, (ns, s // tm), [pl.BlockSpec((tm, d), lambda j, i: (i, 0)), w_spec, w_spec], (xn, wg, wu),
        [o_spec, o_spec, o_spec], [jax.ShapeDtypeStruct((s, ns * c), BF16)] * 3, [], ("parallel", "parallel"), rider)


def ffn_dh(name, dy, wd, dh_dg, dh_du, ns, scale, rider=None):
    s, d = dy.shape
    f = wd.shape[0]
    c = f // ns
    tm = _pick(s, (FFN_TM, 128))

    def body(dy_ref, wd_ref, a_ref, b_ref, dg_ref, du_ref):
        dh = lax.dot_general(dy_ref[...].astype(BF16), wd_ref[...], _DIMS["nt"], preferred_element_type=F32) * scale
        dg_ref[...] = (dh * a_ref[...].astype(F32)).astype(BF16)
        du_ref[...] = (dh * b_ref[...].astype(F32)).astype(BF16)

    o_spec = pl.BlockSpec((tm, c), lambda j, i: (i, j))
    return _call(
        name, body, (ns, s // tm),
        [pl.BlockSpec((tm, d), lambda j, i: (i, 0)), pl.BlockSpec((c, d), lambda j, i: (j, 0)), o_spec, o_spec], (dy, wd, dh_dg, dh_du),
        [o_spec, o_spec], [jax.ShapeDtypeStruct((s, f), BF16), jax.ShapeDtypeStruct((s, f), BF16)],
        [], ("parallel", "parallel"), rider)


FOX_TQ = 512


def fox_tile(s_len):
    return min(FOX_TQ, s_len)


def _heads_per_block(h):
    return 2 if h % 2 == 0 else 1


def _fox_queries(q):
    return (q * (HEAD_DIM ** -0.5)).astype(BF16)


def _fox_scores(qs, kc, cq, ck, diagonal):
    s = lax.dot_general(qs, kc.astype(BF16), _DIMS["nt"], preferred_element_type=F32) + cq - ck
    if not diagonal:
        return s
    return jnp.where(lax.broadcasted_iota(jnp.int32, s.shape, 0) >= lax.broadcasted_iota(jnp.int32, s.shape, 1), s, MASK_VALUE)


def _fox_specs(h, s_len, tq):
    hb = _heads_per_block(h)
    qb = pl.BlockSpec((tq, hb * HEAD_DIM), lambda pp, i: (i, pp))
    kb = pl.BlockSpec((s_len, hb * HEAD_DIM), lambda pp, i: (0, pp))
    colb = pl.BlockSpec((hb, tq, 1), lambda pp, i: (pp, i, 0))
    rowb = pl.BlockSpec((hb, s_len // tq, 1, tq), lambda pp, i: (pp, 0, 0, 0))
    return hb, qb, kb, colb, rowb


def fox_fwd(q, k, v, cq, ck, rider=None):
    s_len, hd = q.shape
    h, d = hd // HEAD_DIM, HEAD_DIM
    tq = fox_tile(s_len)
    hb, qb, kb, colb, rowb = _fox_specs(h, s_len, tq)

    def body(q_ref, k_ref, v_ref, cq_ref, ck_ref, o_ref, lse_ref):
        i = pl.program_id(1)
        for hh in range(hb):
            lanes = slice(hh * d, (hh + 1) * d)
            qs, cqv = _fox_queries(q_ref[:, lanes]), cq_ref[hh]

            def chunk(c, carry, diagonal=False):
                m, l, acc = carry
                rows = pl.ds(pl.multiple_of(c * tq, tq), tq)
                s = _fox_scores(qs, k_ref[rows, lanes], cqv, ck_ref[hh, c], diagonal)
                m_new = jnp.maximum(m, jnp.max(s, axis=-1, keepdims=True))
                alpha = jnp.exp(m - m_new)
                p = jnp.exp(s - m_new)
                acc = alpha * acc + jnp.dot(p.astype(BF16), v_ref[rows, lanes].astype(BF16), preferred_element_type=F32)
                return m_new, alpha * l + jnp.sum(p, axis=-1, keepdims=True), acc

            init = (jnp.full((tq, 1), MASK_VALUE, F32), jnp.zeros((tq, 1), F32), jnp.zeros((tq, d), F32))
            m, l, acc = chunk(i, lax.fori_loop(0, i, chunk, init), diagonal=True)
            o_ref[:, lanes] = acc / l
            lse_ref[hh] = m + jnp.log(l)

    return _call(
        "fox_fwd", body, (h // hb, s_len // tq), [qb, kb, kb, colb, rowb], (q, k, v, cq, ck), [qb, colb],
        [jax.ShapeDtypeStruct((s_len, hd), F32), jax.ShapeDtypeStruct((h, s_len, 1), F32)], (), ("parallel", "parallel"), rider)


def fox_bwd(q, k, v, cq, ck, o, lse, do, rider=None):
    s_len, hd = q.shape
    h, d = hd // HEAD_DIM, HEAD_DIM
    tq = fox_tile(s_len)
    scale = HEAD_DIM ** -0.5
    hb, qb, kb, colb, rowb = _fox_specs(h, s_len, tq)

    def body(q_ref, k_ref, v_ref, cq_ref, ck_ref, o_ref, lse_ref, do_ref, dq_ref, dk_ref, dv_ref, dcq_ref, dck_ref):
        i = pl.program_id(1)

        @pl.when(i == 0)
        def _():
            dk_ref[...] = jnp.zeros_like(dk_ref)
            dv_ref[...] = jnp.zeros_like(dv_ref)
            dck_ref[...] = jnp.zeros_like(dck_ref)

        heads = []
        for hh in range(hb):
            lanes = slice(hh * d, (hh + 1) * d)
            dof = do_ref[:, lanes]
            heads.append((lanes, _fox_queries(q_ref[:, lanes]), cq_ref[hh], lse_ref[hh], dof.astype(BF16),
                          jnp.sum(dof * o_ref[:, lanes], axis=-1, keepdims=True)))

        def chunk(c, carry, diagonal=False):
            rows = pl.ds(pl.multiple_of(c * tq, tq), tq)
            out, dks, dvs = [], [], []
            for hh, (lanes, qs, cqv, lse_h, dob, delta) in enumerate(heads):
                dq, dcq = carry[hh]
                kc = k_ref[rows, lanes]
                p = jnp.exp(_fox_scores(qs, kc, cqv, ck_ref[hh, c], diagonal) - lse_h)
                dp = lax.dot_general(dob, v_ref[rows, lanes].astype(BF16), _DIMS["nt"], preferred_element_type=F32)
                ds = p * (dp - delta)
                dsb = ds.astype(BF16)
                dvs.append(lax.dot_general(p.astype(BF16), dob, _DIMS["tn"], preferred_element_type=F32))
                dks.append(lax.dot_general(dsb, qs, _DIMS["tn"], preferred_element_type=F32))
                dck_ref[hh, c] -= jnp.sum(ds, axis=0, keepdims=True)
                out.append((dq + jnp.dot(dsb, kc.astype(BF16), preferred_element_type=F32), dcq + jnp.sum(ds, axis=-1, keepdims=True)))
            dk_ref[rows, :] += jnp.concatenate(dks, axis=1)
            dv_ref[rows, :] += jnp.concatenate(dvs, axis=1)
            return tuple(out)

        init = tuple((jnp.zeros((tq, d), F32), jnp.zeros((tq, 1), F32)) for _ in range(hb))
        done = chunk(i, lax.fori_loop(0, i, chunk, init), diagonal=True)
        dq_ref[...] = jnp.concatenate([dq for dq, _ in done], axis=1) * scale
        for hh, (_, dcq) in enumerate(done):
            dcq_ref[hh] = dcq

    return _call(
        "fox_bwd", body, (h // hb, s_len // tq), [qb, kb, kb, colb, rowb, qb, colb, qb], (q, k, v, cq, ck, o, lse, do),
        [qb, kb, kb, colb, rowb],
        [jax.ShapeDtypeStruct((s_len, hd), F32)] * 3
        + [jax.ShapeDtypeStruct((h, s_len, 1), F32), jax.ShapeDtypeStruct((h, s_len // tq, 1, tq), F32)],
        (), ("parallel", "arbitrary"), rider)


def _stack_heads(ref, first, g):
    return jnp.concatenate([ref[:, (first + j) * HEAD_DIM:(first + j + 1) * HEAD_DIM] for j in range(g)], axis=0)


def _window(prev_ref, cur_ref, hh):
    lanes = slice(hh * HEAD_DIM, (hh + 1) * HEAD_DIM)
    return jnp.concatenate([prev_ref[:, lanes], cur_ref[:, lanes]], axis=0).astype(BF16)


def _swa_band(g, w):
    t = lax.broadcasted_iota(jnp.int32, (g * w, 2 * w), 0) % w
    col = lax.broadcasted_iota(jnp.int32, (g * w, 2 * w), 1)
    rel = t + w - col
    band = (rel >= 0) & (rel < w)
    return jnp.where(jnp.stack([band & (col >= w), band]), 0.0, MASK_VALUE).astype(F32)


def _swa_probs(qs, kw, sink, band):
    s = lax.dot_general(qs, kw, _DIMS["nt"], preferred_element_type=F32) + band
    m = jnp.maximum(jnp.max(s, axis=-1, keepdims=True), sink)
    p = jnp.exp(s - m)
    ps = jnp.exp(sink - m)
    linv = 1.0 / (jnp.sum(p, axis=-1, keepdims=True) + ps)
    return p * linv, ps * linv


def _swa_specs(hk, g, s_len):
    w = WINDOW
    assert s_len % w == 0
    hb = _heads_per_block(hk)
    qb = pl.BlockSpec((w, hb * g * HEAD_DIM), lambda pp, n: (n, pp))
    prev = pl.BlockSpec((w, hb * HEAD_DIM), lambda pp, n: (jnp.maximum(n - 1, 0), pp))
    cur = pl.BlockSpec((w, hb * HEAD_DIM), lambda pp, n: (n, pp))
    sb = pl.BlockSpec((hb, g * w, 1), lambda pp, n: (pp, 0, 0))
    band = pl.BlockSpec((None, g * w, 2 * w), lambda pp, n: (jnp.minimum(n, 1), 0, 0))
    return hb, qb, prev, cur, sb, band


def swa_fwd(q, k, v, sink, rider=None):
    s_len = q.shape[0]
    hk = k.shape[1] // HEAD_DIM
    g = q.shape[1] // k.shape[1]
    w, d = WINDOW, HEAD_DIM
    hb, qb, prev, cur, sb, bandb = _swa_specs(hk, g, s_len)

    def body(q_ref, kp_ref, kc_ref, vp_ref, vc_ref, sink_ref, band_ref, o_ref):
        for hh in range(hb):
            qs = (_stack_heads(q_ref, hh * g, g) * (HEAD_DIM ** -0.5)).astype(BF16)
            p, _ = _swa_probs(qs, _window(kp_ref, kc_ref, hh), sink_ref[hh], band_ref[...])
            o = jnp.dot(p.astype(BF16), _window(vp_ref, vc_ref, hh), preferred_element_type=F32)
            for j in range(g):
                o_ref[:, (hh * g + j) * d:(hh * g + j + 1) * d] = o[j * w:(j + 1) * w]

    (o,), rid = _call("swa_fwd", body, (hk // hb, s_len // w), [qb, prev, cur, prev, cur, sb, bandb],
                      (q, k, k, v, v, sink, _swa_band(g, w)), [qb], [jax.ShapeDtypeStruct(q.shape, F32)], (),
                      ("parallel", "parallel"), rider)
    return o, rid


def swa_bwd(q, k, v, sink, o, do, rider=None):
    s_len = q.shape[0]
    hk = k.shape[1] // HEAD_DIM
    g = q.shape[1] // k.shape[1]
    w, d = WINDOW, HEAD_DIM
    scale = HEAD_DIM ** -0.5
    hb, qb, prev, cur, sb, bandb = _swa_specs(hk, g, s_len)

    def body(q_ref, kp_ref, kc_ref, vp_ref, vc_ref, sink_ref, band_ref, o_ref, do_ref, dq_ref, dkp_ref, dkc_ref, dvp_ref, dvc_ref,
             dsink_ref):
        @pl.when(pl.program_id(1) == 0)
        def _():
            dsink_ref[...] = jnp.zeros_like(dsink_ref)

        for hh in range(hb):
            lanes = slice(hh * d, (hh + 1) * d)
            qs = (_stack_heads(q_ref, hh * g, g) * scale).astype(BF16)
            kw, vw = _window(kp_ref, kc_ref, hh), _window(vp_ref, vc_ref, hh)
            p, ps = _swa_probs(qs, kw, sink_ref[hh], band_ref[...])
            dof = _stack_heads(do_ref, hh * g, g)
            dob = dof.astype(BF16)
            delta = jnp.sum(dof * _stack_heads(o_ref, hh * g, g), axis=-1, keepdims=True)
            dp = lax.dot_general(dob, vw, _DIMS["nt"], preferred_element_type=F32)
            ds = p * (dp - delta)
            dsb = ds.astype(BF16)
            dsink_ref[hh] -= ps * delta
            dq = jnp.dot(dsb, kw, preferred_element_type=F32) * scale
            for j in range(g):
                dq_ref[:, (hh * g + j) * d:(hh * g + j + 1) * d] = dq[j * w:(j + 1) * w]
            dkw = lax.dot_general(dsb, qs, _DIMS["tn"], preferred_element_type=F32)
            dvw = lax.dot_general(p.astype(BF16), dob, _DIMS["tn"], preferred_element_type=F32)
            dkp_ref[:, lanes] = dkw[:w]
            dkc_ref[:, lanes] = dkw[w:]
            dvp_ref[:, lanes] = dvw[:w]
            dvc_ref[:, lanes] = dvw[w:]

    kv_shape = jax.ShapeDtypeStruct(k.shape, F32)
    (dq, dkp, dkc, dvp, dvc, dsink), rid = _call(
        "swa_bwd", body, (hk // hb, s_len // w), [qb, prev, cur, prev, cur, sb, bandb, qb, qb],
        (q, k, k, v, v, sink, _swa_band(g, w), o, do),
        [qb, cur, cur, cur, cur, sb],
        [jax.ShapeDtypeStruct(q.shape, F32), kv_shape, kv_shape, kv_shape, kv_shape, jax.ShapeDtypeStruct((hk, g * w, 1), F32)],
        (), ("parallel", "arbitrary"), rider)

    def shift_up(a):
        return jnp.concatenate([a[w:], jnp.zeros_like(a[:w])], axis=0)

    return (dq, dkc + shift_up(dkp), dvc + shift_up(dvp), dsink), rid


def loss_call(y, target):
    s, d = y.shape
    tm = _pick(s, (512, 256, 128))

    def body(y_ref, t_ref, l_ref, dy_ref, dyb_ref):
        e = y_ref[...] - t_ref[...]
        dy = e * (1.0 / d)
        dy_ref[...] = dy
        dyb_ref[...] = dy.astype(BF16)

        @pl.when(pl.program_id(0) == 0)
        def _():
            l_ref[...] = jnp.zeros_like(l_ref)

        l_ref[...] += jnp.sum(jnp.sum(e * e, axis=0, keepdims=True), axis=1, keepdims=True) * (0.5 / d)

    row = pl.BlockSpec((tm, d), lambda i: (i, 0))
    (l, dy, dyb), _ = _call("loss_head", body, (s // tm,), [row, row], (y, target), [pl.BlockSpec((1, 1), lambda i: (0, 0)), row, row],
                            [jax.ShapeDtypeStruct((1, 1), F32), jax.ShapeDtypeStruct((s, d), F32), jax.ShapeDtypeStruct((s, d), BF16)],
                            (), ("arbitrary",))
    return l[0, 0], dy, dyb


def _row_tile(rows, cols, itemsize, block_bytes=1 << 20):
    target = max(16, block_bytes // (cols * itemsize))
    fits = [t for t in range(16, rows + 1, 16) if rows % t == 0 and t <= target]
    return fits[-1] if fits else rows


CAST_STEPS = 8


def cast_place(name, ws, p_idx, rider=None):
    n = len(ws)
    assert all(w.shape[0] % (16 * CAST_STEPS) == 0 for w in ws), [w.shape for w in ws]

    def body(p_ref, *refs):
        for w_ref, o_ref in zip(refs[:n], refs[n:]):
            o_ref[...] = w_ref[...].astype(BF16)

    return _call(
        name, body, (CAST_STEPS,), [pl.BlockSpec((w.shape[0] // CAST_STEPS, w.shape[1]), lambda i, pr: (i, 0)) for w in ws], tuple(ws),
        [pl.BlockSpec((None, w.shape[0] // CAST_STEPS, w.shape[1]), lambda i, pr: (pr[0], i, 0)) for w in ws],
        [jax.ShapeDtypeStruct((N_CHIPS,) + w.shape, BF16) for w in ws], (), ("parallel",), rider, prefetch=(p_idx,))


def chip_sum(name, grad, theirs, c_idx):
    ns, r, cols = grad.shape
    rh = r // 2
    tr = _row_tile(rh, cols, 2, 2 << 20)
    nb = rh // tr

    def body(c_ref, a_ref, b_ref, o_ref):
        o_ref[...] = (a_ref[...].astype(F32) + b_ref[...].astype(F32)).astype(o_ref.dtype)

    return pl.pallas_call(
        body, name=name,
        grid_spec=pltpu.PrefetchScalarGridSpec(
            num_scalar_prefetch=1, grid=(ns, nb),
            in_specs=[pl.BlockSpec((None, tr, cols), lambda q, i, cr: (q, cr[0] * nb + i, 0)),
                      pl.BlockSpec((None, tr, cols), lambda q, i, cr: (q, i, 0))],
            out_specs=pl.BlockSpec((None, tr, cols), lambda q, i, cr: (q, i, 0))),
        out_shape=jax.ShapeDtypeStruct((ns, rh, cols), BF16),
        compiler_params=pltpu.CompilerParams(dimension_semantics=("parallel", "parallel"), vmem_limit_bytes=VMEM_LIMIT),
    )(c_idx, grad, theirs)


def owner_sum(name, sums, got, pc_idx):
    ns, rh, cols = sums.shape
    tr = _row_tile(rh, cols, 4, 2 << 20)
    nb = rh // tr

    def body(pc_ref, a_ref, b_ref, o_ref):
        o_ref[...] = ((a_ref[...].astype(F32) + b_ref[0].astype(F32)) + b_ref[1].astype(F32)) + b_ref[2].astype(F32)

    return pl.pallas_call(
        body, name=name,
        grid_spec=pltpu.PrefetchScalarGridSpec(
            num_scalar_prefetch=1, grid=(nb,),
            in_specs=[pl.BlockSpec((None, tr, cols), lambda i, pc: (pc[0], i, 0)),
                      pl.BlockSpec((3, tr, cols), lambda i, pc: (0, i, 0))],
            out_specs=pl.BlockSpec((tr, cols), lambda i, pc: (pc[1] * nb + i, 0))),
        out_shape=jax.ShapeDtypeStruct((2 * rh, cols), F32),
        compiler_params=pltpu.CompilerParams(dimension_semantics=("parallel",), vmem_limit_bytes=VMEM_LIMIT),
    )(pc_idx, sums, got)


def adamw(name, w, g, m, v):
    r, cols = w.shape
    tr = _row_tile(r, cols, 4)
    c1 = 1.0 / (1.0 - ADAM_B1 ** ADAM_STEP)
    c2 = 1.0 / (1.0 - ADAM_B2 ** ADAM_STEP)

    def body(w_ref, g_ref, m_ref, v_ref, go_ref, d_ref, nm_ref, nv_ref):
        gv = g_ref[...]
        nm = ADAM_B1 * m_ref[...] + (1.0 - ADAM_B1) * gv
        nv = ADAM_B2 * v_ref[...] + (1.0 - ADAM_B2) * (gv * gv)
        go_ref[...] = gv
        d_ref[...] = -ADAM_LR * ((nm * c1) / (jnp.sqrt(nv * c2) + ADAM_EPS) + ADAM_WD * w_ref[...])
        nm_ref[...] = nm
        nv_ref[...] = nv

    blk = pl.BlockSpec((tr, cols), lambda i: (i, 0))
    return _call(name, body, (r // tr,), [blk] * 4, (w, g, m, v), [blk] * 4, [jax.ShapeDtypeStruct((r, cols), F32)] * 4, (), ("parallel",))


def _win_layout(d_model):
    hf = hq = d_model // (2 * HEAD_DIM)
    hk = hq // 4
    sizes = [hf * HEAD_DIM, hf * HEAD_DIM, hf * HEAD_DIM, hf, hq * HEAD_DIM, hk * HEAD_DIM, hk * HEAD_DIM]
    return hf, hq, hk, sizes


class WinPlan:
    def __init__(self, d_model, ns=N_CHIPS):
        self.hf, self.hq, self.hk, self.sizes = _win_layout(d_model)
        self.ns, self.cs = ns, sum(self.sizes) // ns
        self.jump_at = sum(self.sizes[:4])
        self.jump_by = -self.jump_at % _LANES
        self.base = [self.pos(s * self.cs) // _LANES * _LANES for s in range(ns)]
        ends = [self.pos((s + 1) * self.cs - 1) + 1 - self.base[s] for s in range(ns)]
        self.width = -(-max(ends) // _LANES) * _LANES
        self.total = -(-max(b + self.width for b in self.base) // 1024) * 1024
        starts = [0]
        for sz in self.sizes:
            starts.append(starts[-1] + sz)
        self.segments = [(self.pos(a), sz) for a, sz in zip(starts, self.sizes)]

    def pos(self, g):
        return g if g < self.jump_at else g + self.jump_by

    def pieces(self, s):
        g0, g1 = s * self.cs, (s + 1) * self.cs
        cuts = [g0] + ([self.jump_at] if g0 < self.jump_at < g1 else []) + [g1]
        return [(a - g0, b - a, self.pos(a) - self.base[s]) for a, b in zip(cuts[:-1], cuts[1:])]

    def place(self, w, s):
        parts, at = [], 0
        for t0, n, j0 in self.pieces(s):
            parts += [jnp.zeros((w.shape[0], j0 - at), w.dtype), w[:, t0:t0 + n]]
            at = j0 + n
        return jnp.concatenate(parts + [jnp.zeros((w.shape[0], self.width - at), w.dtype)], axis=1)

    def unplace(self, slab, s):
        return jnp.concatenate([slab[:, j0:j0 + n] for _, n, j0 in self.pieces(s)], axis=1)

    def assemble(self, slabs):
        return sum(jnp.pad(slabs[s], ((0, 0), (b, self.total - b - self.width))) for s, b in enumerate(self.base))

    def split(self, full):
        return jnp.stack([full[:, b:b + self.width] for b in self.base])


def _attn_inputs(proj, sm, positions):
    s_len = proj.shape[0]
    plan = WinPlan(sm["norm_mix_g"].shape[0])
    hf, hq, hk = plan.hf, plan.hq, plan.hk
    grp = hq // hk
    q_f, k_f, v_f, f_logit, q_s, k_s, v_s = [proj[:, a:a + n] for a, n in plan.segments]

    q_f = head_rms("fox_qnorm", q_f, sm["fox_q_norm_g"])
    k_f = head_rms("fox_knorm", k_f, sm["fox_k_norm_g"])
    log_f = jax.nn.log_sigmoid(f_logit + sm["b_forget"])
    c = time_cumsum(log_f.T)

    inv_freq = ROPE_THETA ** (-jnp.arange(0, HEAD_DIM, 2, dtype=F32) / HEAD_DIM)
    ang = positions.astype(F32)[:, None] * inv_freq
    cos, sin = jnp.cos(ang), jnp.sin(ang)
    q_s = head_rms_rope("swa_qnorm", q_s, sm["swa_q_norm_g"], cos, sin)
    k_s = head_rms_rope("swa_knorm", k_s, sm["swa_k_norm_g"], cos, sin)
    sink = jnp.broadcast_to(sm["swa_sinks"].reshape(hk, grp, 1, 1), (hk, grp, WINDOW, 1)).reshape(hk, grp * WINDOW, 1)
    tq = fox_tile(s_len)
    return (q_f, k_f, v_f, c[:, :, None], c.reshape(hf, s_len // tq, 1, tq)), (q_s, k_s, v_s, sink)


_BIG = ("ffn1_w_gate", "ffn1_w_up", "ffn1_w_down", "w_in", "w_out", "ffn2_w_gate", "ffn2_w_up", "ffn2_w_down")
_ROW_SHARDED = ("ffn1_w_down", "w_out", "ffn2_w_down")
_SMALL = ("norm_ffn1_g", "norm_mix_g", "b_forget", "fox_q_norm_g", "fox_k_norm_g", "swa_q_norm_g", "swa_k_norm_g", "swa_sinks",
          "out_norm_fox_g", "out_norm_swa_g", "norm_ffn2_g")
_ATTN_SMALL = ("norm_mix_g", "b_forget", "fox_q_norm_g", "fox_k_norm_g", "swa_q_norm_g", "swa_k_norm_g", "swa_sinks")
_ALL = ("norm_ffn1_g", "ffn1_w_gate", "ffn1_w_up", "ffn1_w_down", "norm_mix_g", "w_in", "b_forget", "fox_q_norm_g", "fox_k_norm_g",
        "swa_q_norm_g", "swa_k_norm_g", "swa_sinks", "out_norm_fox_g", "out_norm_swa_g", "w_out", "norm_ffn2_g", "ffn2_w_gate",
        "ffn2_w_up", "ffn2_w_down")


def _pack_small(d):
    parts = []
    for k in _SMALL:
        v = d[k].reshape(-1)
        rows = -(-v.shape[0] // _LANES)
        parts.append(jnp.pad(v, (0, rows * _LANES - v.shape[0])).reshape(rows, _LANES))
    a = jnp.concatenate(parts, axis=0)
    return jnp.pad(a, ((0, -a.shape[0] % 8), (0, 0)))


def _unpack_small(a, like):
    out, r0 = {}, 0
    for k in _SMALL:
        nvals = like[k].shape[1]
        rows = -(-nvals // _LANES)
        out[k] = a[r0:r0 + rows].reshape(-1)[:nvals].reshape(1, nvals)
        r0 += rows
    return out


def _stacked(w):
    return w.reshape(-1, w.shape[-1])


def _local_step(shards, sm, x, positions, target, p_idx, c_idx, pc_idx):
    ns = N_CHIPS
    full = {}

    def fetch(*jobs):
        names = list(dict.fromkeys(n for n, _, _ in jobs))
        return names, gather([bufs[n] for n in names], [(names.index(n), kind, part) for n, kind, part in jobs])

    def take(names, rid):
        for n, b in zip(names, rid[0]):
            bufs[n] = b

    n1 = ["ffn1_w_gate", "ffn1_w_up", "ffn1_w_down"]
    n2 = ["ffn2_w_gate", "ffn2_w_up", "ffn2_w_down"]
    later = ["w_in", "w_out"] + n2
    placed, _ = cast_place("cast_place_ffn1", [shards[n] for n in n1], p_idx)
    bufs = dict(zip(n1, placed))
    gate1, up1, down1 = n1
    gate2, up2, down2 = n2
    names, rider = fetch((gate1, "ici", WHOLE), (up1, "ici", WHOLE))
    placed, rid = cast_place("cast_place_later", [shards[n] for n in later], p_idx, rider=rider)
    bufs.update(zip(later, placed))
    take(names, rid)
    names, rider = fetch((gate1, "d2d", WHOLE), (up1, "d2d", WHOLE))
    xn1, r1, rid = rms_fwd("ffn1_norm", x, sm["norm_ffn1_g"], BF16, rider=rider)
    take(names, rid)
    names, rider = fetch((down1, "ici", WHOLE))
    (hid1, hdg1, hdu1), rid = ffn_gu("ffn1_gu", xn1, bufs[gate1], bufs[up1], rider=rider)
    take(names, rid)
    names, rider = fetch((down1, "d2d", WHOLE))
    take(names, run_step("gather_d2d_ffn1_down", rider))
    wd1 = _stacked(bufs[down1])
    names, rider = fetch(("w_in", "ici", WHOLE))
    h1, rid = mm_nn("ffn1_down", hid1, wd1, scale=0.5, resid=x, rider=rider)
    take(names, rid)

    names, rider = fetch(("w_in", "d2d", WHOLE))
    u, r_mix, rid = rms_fwd("mix_norm", h1, sm["norm_mix_g"], BF16, rider=rider)
    take(names, rid)
    names, rider = fetch(("w_out", "ici", WHOLE), (gate2, "ici", (0, 1, 4)))
    plan = WinPlan(x.shape[1])
    win = plan.assemble(bufs["w_in"])
    proj, rid = mm_nn("mix_inproj", u, win, rider=rider)
    take(names, rid)
    sm_attn = {k: sm[k] for k in _ATTN_SMALL}
    (fox_in, swa_in), attn_vjp = jax.vjp(lambda pr, s: _attn_inputs(pr, s, positions), proj, sm_attn)
    names, rider = fetch((gate2, "ici", (1, 4, 4)), ("w_out", "d2d", WHOLE), (gate2, "d2d", (0, 1, 4)))
    (o_f, lse), rid = fox_fwd(*fox_in, rider=rider)
    take(names, rid)
    names, rider = fetch((up2, "ici", (0, 3, 4)), (gate2, "d2d", (1, 4, 4)))
    o_s, rid = swa_fwd(*swa_in, rider=rider)
    take(names, rid)
    o_fox, o_swa = o_f, o_s
    nf, r_fox = rms_fwd("out_norm_fox", o_fox, sm["out_norm_fox_g"], BF16)
    nsw, r_swa = rms_fwd("out_norm_swa", o_swa, sm["out_norm_swa_g"], BF16)
    o = jnp.concatenate([nf, nsw], axis=-1)
    wout = _stacked(bufs["w_out"])
    names, rider = fetch((up2, "ici", (3, 4, 4)), (up2, "d2d", (0, 3, 4)))
    h2, rid = mm_nn("out_proj", o, wout, resid=h1, rider=rider)
    take(names, rid)

    names, rider = fetch((up2, "d2d", (3, 4, 4)))
    xn2, r2, rid = rms_fwd("ffn2_norm", h2, sm["norm_ffn2_g"], BF16, rider=rider)
    take(names, rid)
    names, rider = fetch((down2, "ici", WHOLE))
    (hid2, hdg2, hdu2), rid = ffn_gu("ffn2_gu", xn2, bufs[gate2], bufs[up2], rider=rider)
    take(names, rid)
    names, rider = fetch((down2, "d2d", WHOLE))
    take(names, run_step("gather_d2d_ffn2_down", rider))
    wd2 = _stacked(bufs["ffn2_w_down"])
    y, _ = mm_nn("ffn2_down", hid2, wd2, scale=0.5, resid=h2)
    loss, dy, dy_b = loss_call(y, target)

    red = {}

    def grad(n, g):
        red[n] = {"grad": g.reshape(ns, -1, g.shape[-1])}

    def ride(*steps):
        def done(rid):
            a0 = n0 = 0
            for rd, cb in steps:
                cb(rid[0][a0:a0 + len(rd.aliased)], rid[1][n0:n0 + len(rd.news)])
                a0, n0 = a0 + len(rd.aliased), n0 + len(rd.news)

        return (combine(*[s[0] for s in steps]) if len(steps) > 1 else steps[0][0]), done

    def xchg(*names):
        def cb(al, news):
            for n, t in zip(names, news):
                red[n]["sum"] = chip_sum("chip_sum_" + n, red[n]["grad"], t, c_idx)

        return exchange_halves([red[n]["grad"] for n in names]), cb

    def scat(n, part=WHOLE):
        def cb(al, news):
            red[n]["got"] = (al or news)[0]

        return scatter_to_owner([red[n]["sum"]], [red[n]["got"]] if "got" in red[n] else None, part), cb

    def own(n):
        red[n]["half"] = owner_sum("owner_sum_" + n, red[n]["sum"], red[n]["got"], pc_idx)

    def join(*names):
        return join_halves([red[n]["half"] for n in names]), lambda al, news: full.update(zip(names, al))

    dwd2, _ = mm_tn("ffn2_dwd", hid2, dy_b, out_dtype=BF16, scale=0.5)
    grad(down2, dwd2)
    rider, done = ride(xchg(down2))
    (dg2, du2), rid = ffn_dh("ffn2_dh", dy_b, wd2, hdg2, hdu2, ns, 0.5, rider=rider)
    done(rid)
    rider, done = ride(scat(down2, (0, 1, 2)))
    dwg2, rid = mm_tn_sharded("ffn2_dwg", xn2, dg2, ns, rider=rider)
    done(rid)
    grad(gate2, dwg2)
    rider, done = ride(scat(down2, (1, 2, 2)), xchg(gate2))
    dwu2, rid = mm_tn_sharded("ffn2_dwu", xn2, du2, ns, rider=rider)
    done(rid)
    grad(up2, dwu2)
    rider, done = ride(scat(gate2, (0, 1, 2)), xchg(up2))
    dxn, rid = mm_nt_sharded("ffn2_dxn_g", dg2, bufs[gate2], rider=rider)
    done(rid)
    rider, done = ride(scat(gate2, (1, 2, 2)))
    dxn, rid = mm_nt_sharded("ffn2_dxn_u", du2, bufs[up2], resid=dxn, rider=rider)
    done(rid)
    dh2, dgain_ffn2, dh2_b = rms_bwd("ffn2_dnorm", h2, sm["norm_ffn2_g"], r2, dxn, dres=dy, also_bf16=True)
    own(down2)
    own(gate2)

    do, _ = mm_nt("out_do", dh2_b, wout)
    dwout, _ = mm_tn("out_dw", o, dh2_b, out_dtype=BF16)
    cf = o_fox.shape[1]
    d_fox, dgain_fox = rms_bwd("out_dnorm_fox", o_fox, sm["out_norm_fox_g"], r_fox, do[:, :cf])
    d_swa, dgain_swa = rms_bwd("out_dnorm_swa", o_swa, sm["out_norm_swa_g"], r_swa, do[:, cf:])
    grad("w_out", dwout)
    rider, done = ride(scat(up2))
    swa_cts, rid = swa_bwd(*swa_in, o_s, d_swa, rider=rider)
    done(rid)
    own(up2)
    rider, done = ride(xchg("w_out"), join(down2, gate2, up2))
    fox_cts, rid = fox_bwd(*fox_in, o_f, lse, d_fox, rider=rider)
    done(rid)
    dproj, dsm_attn = attn_vjp((tuple(fox_cts), tuple(swa_cts)))
    dproj = dproj.astype(BF16)

    rider, done = ride(scat("w_out"))
    du, rid = mm_nt("mix_du", dproj, win, rider=rider)
    done(rid)
    dwin, _ = mm_tn("mix_dwin", u, dproj, out_dtype=BF16)
    grad("w_in", plan.split(dwin))
    rider, done = ride(xchg("w_in"))
    dh1, dgain_mix, dh1_b, rid = rms_bwd("mix_dnorm", h1, sm["norm_mix_g"], r_mix, du, dres=dh2, rider=rider, also_bf16=True)
    done(rid)
    own("w_out")

    rider, done = ride(scat("w_in", (0, 1, 2)))
    dwd1, rid = mm_tn("ffn1_dwd", hid1, dh1_b, out_dtype=BF16, scale=0.5, rider=rider)
    done(rid)
    grad(down1, dwd1)
    rider, done = ride(scat("w_in", (1, 2, 2)), xchg(down1))
    (dg1, du1), rid = ffn_dh("ffn1_dh", dh1_b, wd1, hdg1, hdu1, ns, 0.5, rider=rider)
    done(rid)
    own("w_in")
    rider, done = ride(scat(down1, (0, 1, 2)), join("w_out"))
    dwg1, rid = mm_tn_sharded("ffn1_dwg", xn1, dg1, ns, rider=rider)
    done(rid)
    grad(gate1, dwg1)
    rider, done = ride(scat(down1, (1, 2, 2)), xchg(gate1), join("w_in"))
    dwu1, rid = mm_tn_sharded("ffn1_dwu", xn1, du1, ns, rider=rider)
    done(rid)
    grad(up1, dwu1)
    own(down1)
    rider, done = ride(scat(gate1, (0, 1, 2)), xchg(up1), join(down1))
    dxn, rid = mm_nt_sharded("ffn1_dxn_g", dg1, bufs[gate1], rider=rider)
    done(rid)
    rider, done = ride(scat(gate1, (1, 2, 2)), scat(up1, (0, 1, 4)))
    dxn, rid = mm_nt_sharded("ffn1_dxn_u", du1, bufs[up1], resid=dxn, rider=rider)
    done(rid)
    dx, dgain_ffn1 = rms_bwd("ffn1_dnorm", x, sm["norm_ffn1_g"], r1, dxn, dres=dh1)
    own(gate1)

    rider, done = ride(scat(up1, (1, 4, 4)), join(gate1))
    done(run_step("reduce_tail", rider))
    own(up1)
    rider, done = ride(join(up1))
    done(run_step("join_tail", rider))

    g_small = dict(dsm_attn)
    g_small["norm_mix_g"] = g_small["norm_mix_g"] + dgain_mix
    g_small.update(norm_ffn1_g=dgain_ffn1, norm_ffn2_g=dgain_ffn2, out_norm_fox_g=dgain_fox, out_norm_swa_g=dgain_swa)
    return loss, dx, full, g_small


def kernel(x, positions, norm_ffn1_g, ffn1_w_gate, ffn1_w_up, ffn1_w_down, norm_mix_g, w_in, b_forget, fox_q_norm_g, fox_k_norm_g, swa_q_norm_g, swa_k_norm_g, swa_sinks, out_norm_fox_g, out_norm_swa_g, w_out, norm_ffn2_g, ffn2_w_gate, ffn2_w_up, ffn2_w_down, loss_target, m_norm_ffn1_g, m_ffn1_w_gate, m_ffn1_w_up, m_ffn1_w_down, m_norm_mix_g, m_w_in, m_b_forget, m_fox_q_norm_g, m_fox_k_norm_g, m_swa_q_norm_g, m_swa_k_norm_g, m_swa_sinks, m_out_norm_fox_g, m_out_norm_swa_g, m_w_out, m_norm_ffn2_g, m_ffn2_w_gate, m_ffn2_w_up, m_ffn2_w_down, v_norm_ffn1_g, v_ffn1_w_gate, v_ffn1_w_up, v_ffn1_w_down, v_norm_mix_g, v_w_in, v_b_forget, v_fox_q_norm_g, v_fox_k_norm_g, v_swa_q_norm_g, v_swa_k_norm_g, v_swa_sinks, v_out_norm_fox_g, v_out_norm_swa_g, v_w_out, v_norm_ffn2_g, v_ffn2_w_gate, v_ffn2_w_up, v_ffn2_w_down):
    args = dict(locals())
    w = {k: args[k] for k in _ALL}
    m = {k: args["m_" + k] for k in _ALL}
    v = {k: args["v_" + k] for k in _ALL}
    c_idx = lax.axis_index("c").astype(jnp.int32).reshape(1)
    p_idx = (2 * lax.axis_index("x") + lax.axis_index("y")).astype(jnp.int32).reshape(1)
    pc_idx = jnp.concatenate([p_idx, c_idx])

    small = {k: w[k] for k in _SMALL}
    shards = {k: w[k][0] for k in _BIG}
    plan = WinPlan(x.shape[-1])
    shards["w_in"] = lax.switch(p_idx[0], [functools.partial(plan.place, s=s) for s in range(N_CHIPS)], shards["w_in"])
    loss, grad_x, g_shard, g_small = _local_step(shards, {k: w[k][0] for k in _SMALL}, x[0], positions[0], loss_target[0],
                                                 p_idx, c_idx, pc_idx)
    g_shard["w_in"] = lax.switch(p_idx[0], [functools.partial(plan.unplace, s=s) for s in range(N_CHIPS)], g_shard["w_in"])
    loss = lax.psum(loss, ("x", "y", "c"))
    g_small_sum = _unpack_small(all_reduce_small(_pack_small({k: g_small[k].reshape(1, -1) for k in _SMALL})), small)

    grad_w, delta, new_m, new_v = {}, {}, {}, {}
    for k in _BIG:
        (g, d, nm, nv), _ = adamw("adamw_" + k, w[k][0], g_shard[k], m[k][0], v[k][0])
        grad_w[k], delta[k], new_m[k], new_v[k] = g[None], d[None], nm[None], nv[None]
    (_, d, nm, nv), _ = adamw("adamw_small", _pack_small(small), _pack_small(g_small_sum), _pack_small({k: m[k] for k in _SMALL}),
                              _pack_small({k: v[k] for k in _SMALL}))
    grad_w.update(g_small_sum)
    delta.update(_unpack_small(d, small))
    new_m.update(_unpack_small(nm, small))
    new_v.update(_unpack_small(nv, small))

    return (loss, grad_x[None], *[grad_w[k] for k in _ALL], *[delta[k] for k in _ALL], *[new_m[k] for k in _ALL], *[new_v[k] for k in _ALL])
```

```python
import functools

import jax
import jax.numpy as jnp
from jax import lax
from jax.experimental import pallas as pl
from jax.experimental.pallas import tpu as pltpu

F32 = jnp.float32
BF16 = jnp.bfloat16

HEAD_DIM = 64
WINDOW = 128
ROPE_THETA = 10000.0
EPS = 1e-6
N_CHIPS = 4
N_DEV = 8

ADAM_LR = 0.001
ADAM_B1 = 0.9
ADAM_B2 = 0.999
ADAM_EPS = 1e-08
ADAM_WD = 0.01
ADAM_STEP = 10

V7X_VMEM_BYTES = 64 * 1024 * 1024
VMEM_LIMIT = V7X_VMEM_BYTES - 8 * 1024 * 1024
MASK_VALUE = -1e30

_MESH = pl.DeviceIdType.MESH
_HBM = pl.BlockSpec(memory_space=pl.ANY)
_DIMS = {"nn": (((1,), (0,)), ((), ())), "nt": (((1,), (1,)), ((), ())), "tn": (((0,), (0,)), ((), ()))}


def _pick(n, prefs):
    for p in prefs:
        if n % p == 0:
            return p
    return n


class Rider:
    def __init__(self, reads, aliased, news, nsem, build):
        self.reads, self.aliased, self.news, self.nsem, self.build = list(reads), list(aliased), list(news), nsem, build


class _Shifted:
    def __init__(self, ref, off):
        self.ref, self.off = ref, off

    @property
    def at(self):
        return self

    def __getitem__(self, k):
        return self.ref.at[k + self.off]


def combine(*riders):
    def build(reads, al, news, ssem, rsem):
        out = ([], [], [])
        r0 = a0 = n0 = s0 = 0
        for rd in riders:
            nr, na, nn = len(rd.reads), len(rd.aliased), len(rd.news)
            part = rd.build(reads[r0:r0 + nr], al[a0:a0 + na], news[n0:n0 + nn], _Shifted(ssem, s0), _Shifted(rsem, s0))
            for acc, lst in zip(out, part):
                acc.extend(lst)
            r0, a0, n0, s0 = r0 + nr, a0 + na, n0 + nn, s0 + rd.nsem
        return out

    return Rider(sum((r.reads for r in riders), []), sum((r.aliased for r in riders), []), sum((r.news for r in riders), []),
                 sum(r.nsem for r in riders), build)


def _me():
    return lax.axis_index("x"), lax.axis_index("y"), lax.axis_index("c")


def _other_chips(x, y):
    return [(1 - x, y), (x, 1 - y), (1 - x, 1 - y)]


WHOLE = (0, 1, 1)


def _rows(ref, start, rows, part=WHOLE):
    k0, k1, n = part
    assert rows % n == 0, (rows, part)
    idx = (slice(None),) * (len(ref.shape) - 2) + (pl.ds(start + k0 * (rows // n), (k1 - k0) * (rows // n)), slice(None))
    return ref.at[idx]


def _half(ref, h, part=WHOLE):
    rows = ref.shape[-2] // 2
    return _rows(ref, h * rows, rows, part)


def _remote(src, dst, ssem, rsem, k, to):
    return pltpu.make_async_remote_copy(src_ref=src, dst_ref=dst, send_sem=ssem.at[k], recv_sem=rsem.at[k], device_id=to,
                                        device_id_type=_MESH)


def _later(*args):
    return functools.partial(_remote, *args)


def gather(bufs, jobs):
    def build(reads, al, news, ssem, rsem):
        x, y, c = _me()
        p = 2 * x + y
        starts, arrivals = [], []
        for n, (b, kind, part) in enumerate(jobs):
            for j, chip in enumerate(_other_chips(x, y)):
                q = 2 * chip[0] + chip[1]
                if kind == "ici":
                    src, landing, to = _half(al[b].at[p], c, part), _half(al[b].at[q], c, part), (*chip, c)
                else:
                    src, landing, to = _half(al[b].at[q], c, part), _half(al[b].at[q], 1 - c, part), (x, y, 1 - c)
                starts.append(_later(src, src, ssem, rsem, 3 * n + j, to))
                arrivals.append(_later(landing, landing, ssem, rsem, 3 * n + j, to))
        return starts, arrivals, starts

    return Rider([], bufs, [], 3 * len(jobs), build)


def exchange_halves(grads):
    def build(reads, al, news, ssem, rsem):
        x, y, c = _me()
        cps = [_later(_half(g, 1 - c), t, ssem, rsem, w, (x, y, 1 - c)) for w, (g, t) in enumerate(zip(reads, news))]
        return cps, cps, cps

    return Rider(grads, [], [jax.ShapeDtypeStruct((g.shape[0], g.shape[1] // 2, g.shape[2]), g.dtype) for g in grads], len(grads), build)


def scatter_to_owner(sums, gots=None, part=WHOLE):
    def build(reads, al, news, ssem, rsem):
        x, y, c = _me()
        cps = []
        for w, (s, got) in enumerate(zip(reads, al or news)):
            rows = s.shape[-2]
            for j, chip in enumerate(_other_chips(x, y)):
                cps.append(_later(_rows(s.at[2 * chip[0] + chip[1]], 0, rows, part), _rows(got.at[j], 0, rows, part), ssem, rsem,
                                  3 * w + j, (*chip, c)))
        return cps, cps, cps

    news = [] if gots else [jax.ShapeDtypeStruct((3,) + s.shape[1:], s.dtype) for s in sums]
    return Rider(sums, gots or [], news, 3 * len(sums), build)


def join_halves(fulls):
    def build(reads, al, news, ssem, rsem):
        x, y, c = _me()
        starts, arrivals = [], []
        for w, f in enumerate(al):
            mine, landing = _half(f, c), _half(f, 1 - c)
            starts.append(_later(mine, mine, ssem, rsem, w, (x, y, 1 - c)))
            arrivals.append(_later(landing, landing, ssem, rsem, w, (x, y, 1 - c)))
        return starts, arrivals, starts

    return Rider([], fulls, [], len(fulls), build)


def _start_and_wait(rider, reads, al, news, ssem, rsem, first, last):
    @pl.when(first)
    def _():
        for cp in rider.build(reads, al, news, ssem, rsem)[0]:
            cp().start()

    def finish():
        @pl.when(last)
        def _():
            _, arrivals, sends = rider.build(reads, al, news, ssem, rsem)
            for cp in arrivals:
                cp().wait_recv()
            for cp in sends:
                cp().wait_send()

    return finish


def _call(name, body, grid, in_specs, args, out_specs, out_shape, scratch=(), semantics=None, rider=None, prefetch=()):
    n_pre, n_in, n_out, n_scr = len(prefetch), len(args), len(out_shape), len(scratch)
    nr, na, nn = (len(rider.reads), len(rider.aliased), len(rider.news)) if rider else (0, 0, 0)

    def wrapped(*refs):
        pre, refs = refs[:n_pre], refs[n_pre:]
        ins, reads = refs[:n_in], refs[n_in:n_in + nr]
        o0 = n_in + nr + na
        outs, al, news = refs[o0:o0 + n_out], refs[o0 + n_out:o0 + n_out + na], refs[o0 + n_out + na:o0 + n_out + na + nn]
        s0 = o0 + n_out + na + nn
        scr, (ssem, rsem) = refs[s0:s0 + n_scr], refs[s0 + n_scr:]
        first = functools.reduce(jnp.logical_and, [pl.program_id(a) == 0 for a in range(len(grid))])
        last = functools.reduce(jnp.logical_and, [pl.program_id(a) == g - 1 for a, g in enumerate(grid)])
        finish = _start_and_wait(rider, reads, al, news, ssem, rsem, first, last)
        body(*pre, *ins, *outs, *scr)
        finish()

    kernel_fn, all_in, all_out, shapes, scr = body, list(in_specs), list(out_specs), list(out_shape), list(scratch)
    operands, aliases = (*prefetch, *args), {}
    if rider:
        kernel_fn, semantics = wrapped, ("arbitrary",) * len(grid)
        all_in += [_HBM] * (nr + na)
        all_out += [_HBM] * (na + nn)
        shapes += [jax.ShapeDtypeStruct(a.shape, a.dtype) for a in rider.aliased] + rider.news
        scr += [pltpu.SemaphoreType.DMA((rider.nsem,)), pltpu.SemaphoreType.DMA((rider.nsem,))]
        operands += (*rider.reads, *rider.aliased)
        aliases = {n_pre + n_in + nr + i: n_out + i for i in range(na)}
    params = pltpu.CompilerParams(dimension_semantics=semantics, vmem_limit_bytes=VMEM_LIMIT)
    if n_pre:
        spec = pltpu.PrefetchScalarGridSpec(num_scalar_prefetch=n_pre, grid=grid, in_specs=all_in, out_specs=all_out, scratch_shapes=scr)
        outs = pl.pallas_call(kernel_fn, name=name, grid_spec=spec, out_shape=shapes, input_output_aliases=aliases, compiler_params=params)(*operands)
    else:
        outs = pl.pallas_call(kernel_fn, name=name, grid=grid, in_specs=all_in, out_specs=all_out, out_shape=shapes, scratch_shapes=scr,
                              input_output_aliases=aliases, compiler_params=params)(*operands)
    return list(outs[:n_out]), ((list(outs[n_out:n_out + na]), list(outs[n_out + na:])) if rider else None)


def run_step(name, rider):
    nr, na, nn = len(rider.reads), len(rider.aliased), len(rider.news)

    def body(*refs):
        reads = refs[:nr]
        al, news = refs[nr + na:nr + 2 * na], refs[nr + 2 * na:nr + 2 * na + nn]
        ssem, rsem = refs[nr + 2 * na + nn:]
        starts, arrivals, sends = rider.build(reads, al, news, ssem, rsem)
        for cp in starts:
            cp().start()
        for cp in arrivals:
            cp().wait_recv()
        for cp in sends:
            cp().wait_send()

    outs = pl.pallas_call(
        body, name=name, in_specs=[_HBM] * (nr + na), out_specs=[_HBM] * (na + nn),
        out_shape=[jax.ShapeDtypeStruct(a.shape, a.dtype) for a in rider.aliased] + rider.news,
        input_output_aliases={nr + i: i for i in range(na)},
        scratch_shapes=[pltpu.SemaphoreType.DMA((rider.nsem,)), pltpu.SemaphoreType.DMA((rider.nsem,))],
    )(*rider.reads, *rider.aliased)
    return list(outs[:na]), list(outs[na:])


def all_reduce_small(v):
    rows, lanes = v.shape

    def body(v_ref, o_ref, slots, send_sems, recv_sems):
        x, y, c = _me()
        me = 4 * x + 2 * y + c
        slots[me] = v_ref[...]
        cps = []
        for k in range(1, N_DEV):
            peer = (x ^ (k >> 2), y ^ ((k >> 1) & 1), c ^ (k & 1))
            cps.append(_remote(v_ref, slots.at[me], send_sems, recv_sems, k - 1, peer))
            cps[-1].start()
        for k in range(1, N_DEV):
            theirs = slots.at[me ^ k]
            _remote(theirs, theirs, send_sems, recv_sems, k - 1, (x, y, c)).wait_recv()
        for cp in cps:
            cp.wait_send()
        acc = slots[0]
        for i in range(1, N_DEV):
            acc = acc + slots[i]
        o_ref[...] = acc

    return pl.pallas_call(
        body, name="all_reduce_small",
        in_specs=[pl.BlockSpec(memory_space=pltpu.VMEM)], out_specs=pl.BlockSpec(memory_space=pltpu.VMEM),
        out_shape=jax.ShapeDtypeStruct((rows, lanes), F32),
        scratch_shapes=[pltpu.VMEM((N_DEV, rows, lanes), F32), pltpu.SemaphoreType.DMA((N_DEV - 1,)), pltpu.SemaphoreType.DMA((N_DEV - 1,))],
    )(v)


def _mm_call(name, mode, a, b, a_spec, b_spec, out_shape, out_spec, grid, acc_shape, scale=1.0, resid=None, resid_spec=None, rider=None):
    nk = grid[2]
    dims = _DIMS[mode]
    has_resid = resid is not None

    def body(*refs):
        a_ref, b_ref = refs[:2]
        r_ref = refs[2] if has_resid else None
        o_ref = refs[3] if has_resid else refs[2]

        def finish(r):
            if scale != 1.0:
                r = r * scale
            if has_resid:
                r = r_ref[...].astype(F32) + r
            o_ref[...] = r.astype(o_ref.dtype)

        part = lax.dot_general(a_ref[...].astype(BF16), b_ref[...].astype(BF16), dims, preferred_element_type=F32)
        if nk == 1:
            finish(part)
            return
        acc_ref = refs[-1]
        k = pl.program_id(2)

        @pl.when(k == 0)
        def _():
            acc_ref[...] = part

        @pl.when(k > 0)
        def _():
            acc_ref[...] += part

        @pl.when(k == nk - 1)
        def _():
            finish(acc_ref[...])

    in_specs = [a_spec, b_spec] + ([resid_spec] if has_resid else [])
    args = (a, b) + ((resid,) if has_resid else ())
    (out,), rid = _call(name, body, grid, in_specs, args, [out_spec], [out_shape], [pltpu.VMEM(acc_shape, F32)] if nk > 1 else [],
                        ("parallel", "parallel", "arbitrary"), rider)
    return out, rid


MM_VMEM_BUDGET = 40 * 1024 * 1024
_TILE_OPTS = (2048, 1408, 1024, 512, 256, 128)


def _tiles(m, n, kd, a_item, b_item, o_item, r_item=0, tm=None, tn=None, tk=None):
    def opts(full, fixed, cap):
        return [fixed] if fixed else [t for t in _TILE_OPTS if t <= cap and full % t == 0] or [full]

    best = None
    for cm in opts(m, tm, 1408):
        for cn in opts(n, tn, 1408):
            for ck in opts(kd, tk, 2048):
                blocks = cm * ck * a_item + ck * cn * b_item + cm * cn * (o_item + r_item)
                casts = (cm * ck * 2 if a_item == 4 else 0) + (ck * cn * 2 if b_item == 4 else 0)
                if 2 * blocks + cm * cn * 4 + casts <= MM_VMEM_BUDGET:
                    key = (cm * cn * ck, ck)
                    if best is None or key > best[0]:
                        best = (key, (cm, cn, ck))
    assert best is not None, (m, n, kd)
    return best[1]


def _item(x):
    return jnp.dtype(x.dtype).itemsize


def mm_nn(name, a, b, *, out_dtype=F32, scale=1.0, resid=None, rider=None):
    m, kd = a.shape
    n = b.shape[1]
    tm, tn, tk = _tiles(m, n, kd, _item(a), _item(b), jnp.dtype(out_dtype).itemsize, 0 if resid is None else _item(resid))
    o_spec = pl.BlockSpec((tm, tn), lambda i, j, k: (i, j))
    return _mm_call(
        name, "nn", a, b, pl.BlockSpec((tm, tk), lambda i, j, k: (i, k)), pl.BlockSpec((tk, tn), lambda i, j, k: (k, j)),
        jax.ShapeDtypeStruct((m, n), out_dtype), o_spec, (m // tm, n // tn, kd // tk), (tm, tn), scale, resid, o_spec, rider)


def mm_nt(name, a, b, *, out_dtype=F32, scale=1.0, resid=None, rider=None):
    m, kd = a.shape
    n = b.shape[0]
    tm, tn, tk = _tiles(m, n, kd, _item(a), _item(b), jnp.dtype(out_dtype).itemsize, 0 if resid is None else _item(resid))
    o_spec = pl.BlockSpec((tm, tn), lambda i, j, k: (i, j))
    return _mm_call(
        name, "nt", a, b, pl.BlockSpec((tm, tk), lambda i, j, k: (i, k)), pl.BlockSpec((tn, tk), lambda i, j, k: (j, k)),
        jax.ShapeDtypeStruct((m, n), out_dtype), o_spec, (m // tm, n // tn, kd // tk), (tm, tn), scale, resid, o_spec, rider)


def mm_tn(name, a, b, *, out_dtype=F32, scale=1.0, rider=None):
    kd, m = a.shape
    n = b.shape[1]
    tm, tn, tk = _tiles(m, n, kd, _item(a), _item(b), jnp.dtype(out_dtype).itemsize)
    return _mm_call(
        name, "tn", a, b, pl.BlockSpec((tk, tm), lambda i, j, k: (k, i)), pl.BlockSpec((tk, tn), lambda i, j, k: (k, j)),
        jax.ShapeDtypeStruct((m, n), out_dtype), pl.BlockSpec((tm, tn), lambda i, j, k: (i, j)),
        (m // tm, n // tn, kd // tk), (tm, tn), scale, rider=rider)


def mm_nt_sharded(name, a, w, *, resid=None, rider=None):
    m = a.shape[0]
    ns, n, c = w.shape
    tm, tn, _ = _tiles(m, n, c, _item(a), _item(w), 4, 0 if resid is None else _item(resid), tk=c)
    o_spec = pl.BlockSpec((tm, tn), lambda i, j, k: (i, j))
    return _mm_call(
        name, "nt", a, w, pl.BlockSpec((tm, c), lambda i, j, k: (i, k)), pl.BlockSpec((None, tn, c), lambda i, j, k: (k, j, 0)),
        jax.ShapeDtypeStruct((m, n), F32), o_spec, (m // tm, n // tn, ns), (tm, tn), 1.0, resid, o_spec, rider)


def mm_tn_sharded(name, a, b, ns, *, rider=None):
    kd, m = a.shape
    c = b.shape[1] // ns
    tm, _, tk = _tiles(m, c, kd, _item(a), _item(b), 2, tn=c)
    return _mm_call(
        name, "tn", a, b, pl.BlockSpec((tk, tm), lambda i, j, k: (k, i)), pl.BlockSpec((tk, c), lambda i, j, k: (k, j)),
        jax.ShapeDtypeStruct((ns, m, c), BF16), pl.BlockSpec((None, tm, c), lambda i, j, k: (j, i, 0)),
        (m // tm, ns, kd // tk), (tm, c), rider=rider)


def rms_fwd(name, x, g, out_dtype, rider=None):
    r, c = x.shape
    tm = _pick(r, (512, 256, 128, 64, 8))

    def body(x_ref, g_ref, y_ref, r_ref):
        xf = x_ref[...].astype(F32)
        rstd = lax.rsqrt(jnp.mean(xf * xf, axis=-1, keepdims=True) + EPS)
        y_ref[...] = ((xf * rstd) * g_ref[...]).astype(y_ref.dtype)
        r_ref[...] = rstd

    (y, rstd), rid = _call(
        name, body, (r // tm,), [pl.BlockSpec((tm, c), lambda i: (i, 0)), pl.BlockSpec((1, c), lambda i: (0, 0))], (x, g.reshape(1, c)),
        [pl.BlockSpec((tm, c), lambda i: (i, 0)), pl.BlockSpec((tm, 1), lambda i: (i, 0))],
        [jax.ShapeDtypeStruct((r, c), out_dtype), jax.ShapeDtypeStruct((r, 1), F32)], (), ("parallel",), rider)
    return (y, rstd) if rider is None else (y, rstd, rid)


def rms_bwd(name, x, g, rstd, dy, dres=None, rider=None, also_bf16=False):
    r, c = x.shape
    tm = _pick(r, (512, 256, 128, 64, 8))
    has_res = dres is not None

    def body(*refs):
        x_ref, g_ref, r_ref, dy_ref = refs[:4]
        dres_ref = refs[4] if has_res else None
        dx_ref, dg_ref = refs[4 + has_res:6 + has_res]
        xhat = x_ref[...].astype(F32) * r_ref[...]
        dyf = dy_ref[...].astype(F32)
        gdy = dyf * g_ref[...]
        dx = r_ref[...] * (gdy - xhat * jnp.mean(gdy * xhat, axis=-1, keepdims=True))
        if has_res:
            dx = dx + dres_ref[...]
        dx_ref[...] = dx
        if also_bf16:
            refs[-1][...] = dx.astype(BF16)

        @pl.when(pl.program_id(0) == 0)
        def _():
            dg_ref[...] = jnp.zeros_like(dg_ref)

        dg_ref[...] += jnp.sum(dyf * xhat, axis=0, keepdims=True)

    row = pl.BlockSpec((tm, c), lambda i: (i, 0))
    in_specs = [row, pl.BlockSpec((1, c), lambda i: (0, 0)), pl.BlockSpec((tm, 1), lambda i: (i, 0)), row] + ([row] if has_res else [])
    args = (x, g.reshape(1, c), rstd, dy) + ((dres,) if has_res else ())
    outs, rid = _call(name, body, (r // tm,), in_specs, args, [row, pl.BlockSpec((1, c), lambda i: (0, 0))] + [row] * also_bf16,
                      [jax.ShapeDtypeStruct((r, c), F32), jax.ShapeDtypeStruct((1, c), F32)] + [jax.ShapeDtypeStruct((r, c), BF16)] * also_bf16,
                      (), ("arbitrary",), rider)
    return (outs[0], outs[1].reshape(c), *outs[2:], *([] if rider is None else [rid]))


_LANES = 128


def _head_mean(v):
    if v.shape[1] == HEAD_DIM:
        return jnp.mean(v, axis=-1, keepdims=True)
    low = lax.broadcasted_iota(jnp.int32, v.shape, 1) < HEAD_DIM
    lo = jnp.sum(jnp.where(low, v, 0.0), axis=-1, keepdims=True)
    hi = jnp.sum(jnp.where(low, 0.0, v), axis=-1, keepdims=True)
    return jnp.where(low, lo, hi) * (1.0 / HEAD_DIM)


def _head_groups(c):
    width = _LANES if c % _LANES == 0 else HEAD_DIM
    assert c % width == 0, c
    return width, [slice(k * width, (k + 1) * width) for k in range(c // width)]


def _head_gain(g, width):
    return jnp.tile(g.reshape(1, HEAD_DIM), (1, width // HEAD_DIM))


def _rotate_half(y):
    half = HEAD_DIM // 2
    first = lax.broadcasted_iota(jnp.int32, y.shape, 1) % HEAD_DIM < half
    return jnp.where(first, -pltpu.roll(y, y.shape[1] - half, axis=1), pltpu.roll(y, half, axis=1))


def _rope_tables(rope, width):
    return [jnp.tile(t, (1, 2 * width // HEAD_DIM)) for t in rope]


def head_rms_fwd(name, x, g, rope=None):
    s, c = x.shape
    tm = _pick(s, (256, 128, 8))
    width, groups = _head_groups(c)

    def body(x_ref, g_ref, *refs):
        y_ref = refs[-1]
        for sl in groups:
            xs = x_ref[:, sl]
            y = (xs * lax.rsqrt(_head_mean(xs * xs) + EPS)) * g_ref[...]
            if rope:
                y = y * refs[0][...] + _rotate_half(y) * refs[1][...]
            y_ref[:, sl] = y

    row = pl.BlockSpec((tm, c), lambda i: (i, 0))
    tab = pl.BlockSpec((tm, width), lambda i: (i, 0))
    tables = _rope_tables(rope, width) if rope else []
    (y,), _ = _call(name, body, (s // tm,), [row, pl.BlockSpec((1, width), lambda i: (0, 0))] + [tab] * len(tables),
                    (x, _head_gain(g, width), *tables), [row], [jax.ShapeDtypeStruct((s, c), F32)], (), ("parallel",))
    return y


def head_rms_bwd(name, x, g, dy, rope=None):
    s, c = x.shape
    tm = _pick(s, (256, 128, 8))
    width, groups = _head_groups(c)

    def body(x_ref, g_ref, dy_ref, *refs):
        dx_ref, dg_ref = refs[-2:]

        @pl.when(pl.program_id(0) == 0)
        def _():
            dg_ref[...] = jnp.zeros_like(dg_ref)

        for sl in groups:
            xs, dys = x_ref[:, sl], dy_ref[:, sl]
            if rope:
                dys = dys * refs[0][...] - _rotate_half(dys * refs[1][...])
            rstd = lax.rsqrt(_head_mean(xs * xs) + EPS)
            xhat = xs * rstd
            gdy = dys * g_ref[...]
            dx_ref[:, sl] = rstd * (gdy - xhat * _head_mean(gdy * xhat))
            dg_ref[...] += jnp.sum(dys * xhat, axis=0, keepdims=True)

    row = pl.BlockSpec((tm, c), lambda i: (i, 0))
    vec = pl.BlockSpec((1, width), lambda i: (0, 0))
    tab = pl.BlockSpec((tm, width), lambda i: (i, 0))
    tables = _rope_tables(rope, width) if rope else []
    (dx, dg), _ = _call(name, body, (s // tm,), [row, vec, row] + [tab] * len(tables), (x, _head_gain(g, width), dy, *tables), [row, vec],
                        [jax.ShapeDtypeStruct((s, c), F32), jax.ShapeDtypeStruct((1, width), F32)], (), ("arbitrary",))
    return dx, jnp.sum(dg.reshape(width // HEAD_DIM, HEAD_DIM), axis=0)


@functools.partial(jax.custom_vjp, nondiff_argnums=(0,))
def head_rms(name, x, g):
    return head_rms_fwd(name + "_fwd", x, g)


def _head_rms_fwd(name, x, g):
    return head_rms_fwd(name + "_fwd", x, g), (x, g)


def _head_rms_bwd(name, res, dy):
    return head_rms_bwd(name + "_bwd", *res, dy)


head_rms.defvjp(_head_rms_fwd, _head_rms_bwd)


@functools.partial(jax.custom_vjp, nondiff_argnums=(0,))
def head_rms_rope(name, x, g, cos, sin):
    return head_rms_fwd(name + "_fwd", x, g, (cos, sin))


def _head_rms_rope_fwd(name, x, g, cos, sin):
    return head_rms_fwd(name + "_fwd", x, g, (cos, sin)), (x, g, cos, sin)


def _head_rms_rope_bwd(name, res, dy):
    x, g, cos, sin = res
    return (*head_rms_bwd(name + "_bwd", x, g, dy, (cos, sin)), jnp.zeros_like(cos), jnp.zeros_like(sin))


head_rms_rope.defvjp(_head_rms_rope_fwd, _head_rms_rope_bwd)


def _cumsum_call(name, a, reverse):
    h, s = a.shape
    tb = _LANES
    assert s % tb == 0

    def body(a_ref, o_ref):
        t_in = lax.broadcasted_iota(jnp.int32, (tb, tb), 0)
        t_out = lax.broadcasted_iota(jnp.int32, (tb, tb), 1)
        tri = jnp.where((t_in >= t_out) if reverse else (t_in <= t_out), 1.0, 0.0).astype(BF16)
        carry = jnp.zeros((h, 1), F32)
        blocks = range(s // tb)
        for b in (reversed(blocks) if reverse else blocks):
            cols = slice(b * tb, (b + 1) * tb)
            block = a_ref[:, cols]
            rest, local = block, jnp.zeros((h, tb), F32)
            for _ in range(3):
                piece = rest.astype(BF16)
                local = local + jnp.dot(piece, tri, preferred_element_type=F32)
                rest = rest - piece.astype(F32)
            o_ref[:, cols] = local + carry
            carry = carry + jnp.sum(block, axis=1, keepdims=True)

    whole = pl.BlockSpec((h, s), lambda j: (0, 0))
    (out,), _ = _call(name, body, (1,), [whole], (a,), [whole], [jax.ShapeDtypeStruct((h, s), F32)], (), ("arbitrary",))
    return out


@jax.custom_vjp
def time_cumsum(a):
    return _cumsum_call("gate_cumsum", a, False)


def _time_cumsum_fwd(a):
    return _cumsum_call("gate_cumsum", a, False), None


def _time_cumsum_bwd(_, dc):
    return (_cumsum_call("gate_cumsum_bwd", dc, True),)


time_cumsum.defvjp(_time_cumsum_fwd, _time_cumsum_bwd)


FFN_TM = 512


def _sigmoid(x):
    return 1.0 / (1.0 + jnp.exp(-x))


def ffn_gu(name, xn, wg, wu, rider=None):
    s, d = xn.shape
    ns, _, c = wg.shape
    tm = _pick(s, (FFN_TM, 128))

    def body(x_ref, wg_ref, wu_ref, h_ref, a_ref, b_ref):
        xb = x_ref[...]
        gv = jnp.dot(xb, wg_ref[...], preferred_element_type=F32)
        uv = jnp.dot(xb, wu_ref[...], preferred_element_type=F32)
        sig = _sigmoid(gv)
        silu = gv * sig
        h_ref[...] = (silu * uv).astype(BF16)
        a_ref[...] = (uv * (sig * (1.0 + gv * (1.0 - sig)))).astype(BF16)
        b_ref[...] = silu.astype(BF16)

    w_spec = pl.BlockSpec((None, d, c), lambda j, i: (j, 0, 0))
    o_spec = pl.BlockSpec((tm, c), lambda j, i: (i, j))
    return _call(
        name, body, (ns, s // tm), [pl.BlockSpec((tm, d), lambda j, i: (i, 0)), w_spec, w_spec], (xn, wg, wu),
        [o_spec, o_spec, o_spec], [jax.ShapeDtypeStruct((s, ns * c), BF16)] * 3, [], ("parallel", "parallel"), rider)


def ffn_dh(name, dy, wd, dh_dg, dh_du, ns, scale, rider=None):
    s, d = dy.shape
    f = wd.shape[0]
    c = f // ns
    tm = _pick(s, (FFN_TM, 128))

    def body(dy_ref, wd_ref, a_ref, b_ref, dg_ref, du_ref):
        dh = lax.dot_general(dy_ref[...].astype(BF16), wd_ref[...], _DIMS["nt"], preferred_element_type=F32) * scale
        dg_ref[...] = (dh * a_ref[...].astype(F32)).astype(BF16)
        du_ref[...] = (dh * b_ref[...].astype(F32)).astype(BF16)

    o_spec = pl.BlockSpec((tm, c), lambda j, i: (i, j))
    return _call(
        name, body, (ns, s // tm),
        [pl.BlockSpec((tm, d), lambda j, i: (i, 0)), pl.BlockSpec((c, d), lambda j, i: (j, 0)), o_spec, o_spec], (dy, wd, dh_dg, dh_du),
        [o_spec, o_spec], [jax.ShapeDtypeStruct((s, f), BF16), jax.ShapeDtypeStruct((s, f), BF16)],
        [], ("parallel", "parallel"), rider)


FOX_TQ = 512


def fox_tile(s_len):
    return min(FOX_TQ, s_len)


def _heads_per_block(h):
    return 2 if h % 2 == 0 else 1


def _fox_queries(q):
    return (q * (HEAD_DIM ** -0.5)).astype(BF16)


def _fox_scores(qs, kc, cq, ck, diagonal):
    s = lax.dot_general(qs, kc.astype(BF16), _DIMS["nt"], preferred_element_type=F32) + cq - ck
    if not diagonal:
        return s
    return jnp.where(lax.broadcasted_iota(jnp.int32, s.shape, 0) >= lax.broadcasted_iota(jnp.int32, s.shape, 1), s, MASK_VALUE)


def _fox_specs(h, s_len, tq):
    hb = _heads_per_block(h)
    qb = pl.BlockSpec((tq, hb * HEAD_DIM), lambda pp, i: (i, pp))
    kb = pl.BlockSpec((s_len, hb * HEAD_DIM), lambda pp, i: (0, pp))
    colb = pl.BlockSpec((hb, tq, 1), lambda pp, i: (pp, i, 0))
    rowb = pl.BlockSpec((hb, s_len // tq, 1, tq), lambda pp, i: (pp, 0, 0, 0))
    return hb, qb, kb, colb, rowb


def fox_fwd(q, k, v, cq, ck, rider=None):
    s_len, hd = q.shape
    h, d = hd // HEAD_DIM, HEAD_DIM
    tq = fox_tile(s_len)
    hb, qb, kb, colb, rowb = _fox_specs(h, s_len, tq)

    def body(q_ref, k_ref, v_ref, cq_ref, ck_ref, o_ref, lse_ref):
        i = pl.program_id(1)
        for hh in range(hb):
            lanes = slice(hh * d, (hh + 1) * d)
            qs, cqv = _fox_queries(q_ref[:, lanes]), cq_ref[hh]

            def chunk(c, carry, diagonal=False):
                m, l, acc = carry
                rows = pl.ds(pl.multiple_of(c * tq, tq), tq)
                s = _fox_scores(qs, k_ref[rows, lanes], cqv, ck_ref[hh, c], diagonal)
                m_new = jnp.maximum(m, jnp.max(s, axis=-1, keepdims=True))
                alpha = jnp.exp(m - m_new)
                p = jnp.exp(s - m_new)
                acc = alpha * acc + jnp.dot(p.astype(BF16), v_ref[rows, lanes].astype(BF16), preferred_element_type=F32)
                return m_new, alpha * l + jnp.sum(p, axis=-1, keepdims=True), acc

            init = (jnp.full((tq, 1), MASK_VALUE, F32), jnp.zeros((tq, 1), F32), jnp.zeros((tq, d), F32))
            m, l, acc = chunk(i, lax.fori_loop(0, i, chunk, init), diagonal=True)
            o_ref[:, lanes] = acc / l
            lse_ref[hh] = m + jnp.log(l)

    return _call(
        "fox_fwd", body, (h // hb, s_len // tq), [qb, kb, kb, colb, rowb], (q, k, v, cq, ck), [qb, colb],
        [jax.ShapeDtypeStruct((s_len, hd), F32), jax.ShapeDtypeStruct((h, s_len, 1), F32)], (), ("parallel", "parallel"), rider)


def fox_bwd(q, k, v, cq, ck, o, lse, do, rider=None):
    s_len, hd = q.shape
    h, d = hd // HEAD_DIM, HEAD_DIM
    tq = fox_tile(s_len)
    scale = HEAD_DIM ** -0.5
    hb, qb, kb, colb, rowb = _fox_specs(h, s_len, tq)

    def body(q_ref, k_ref, v_ref, cq_ref, ck_ref, o_ref, lse_ref, do_ref, dq_ref, dk_ref, dv_ref, dcq_ref, dck_ref):
        i = pl.program_id(1)

        @pl.when(i == 0)
        def _():
            dk_ref[...] = jnp.zeros_like(dk_ref)
            dv_ref[...] = jnp.zeros_like(dv_ref)
            dck_ref[...] = jnp.zeros_like(dck_ref)

        heads = []
        for hh in range(hb):
            lanes = slice(hh * d, (hh + 1) * d)
            dof = do_ref[:, lanes]
            heads.append((lanes, _fox_queries(q_ref[:, lanes]), cq_ref[hh], lse_ref[hh], dof.astype(BF16),
                          jnp.sum(dof * o_ref[:, lanes], axis=-1, keepdims=True)))

        def chunk(c, carry, diagonal=False):
            rows = pl.ds(pl.multiple_of(c * tq, tq), tq)
            out, dks, dvs = [], [], []
            for hh, (lanes, qs, cqv, lse_h, dob, delta) in enumerate(heads):
                dq, dcq = carry[hh]
                kc = k_ref[rows, lanes]
                p = jnp.exp(_fox_scores(qs, kc, cqv, ck_ref[hh, c], diagonal) - lse_h)
                dp = lax.dot_general(dob, v_ref[rows, lanes].astype(BF16), _DIMS["nt"], preferred_element_type=F32)
                ds = p * (dp - delta)
                dsb = ds.astype(BF16)
                dvs.append(lax.dot_general(p.astype(BF16), dob, _DIMS["tn"], preferred_element_type=F32))
                dks.append(lax.dot_general(dsb, qs, _DIMS["tn"], preferred_element_type=F32))
                dck_ref[hh, c] -= jnp.sum(ds, axis=0, keepdims=True)
                out.append((dq + jnp.dot(dsb, kc.astype(BF16), preferred_element_type=F32), dcq + jnp.sum(ds, axis=-1, keepdims=True)))
            dk_ref[rows, :] += jnp.concatenate(dks, axis=1)
            dv_ref[rows, :] += jnp.concatenate(dvs, axis=1)
            return tuple(out)

        init = tuple((jnp.zeros((tq, d), F32), jnp.zeros((tq, 1), F32)) for _ in range(hb))
        done = chunk(i, lax.fori_loop(0, i, chunk, init), diagonal=True)
        dq_ref[...] = jnp.concatenate([dq for dq, _ in done], axis=1) * scale
        for hh, (_, dcq) in enumerate(done):
            dcq_ref[hh] = dcq

    return _call(
        "fox_bwd", body, (h // hb, s_len // tq), [qb, kb, kb, colb, rowb, qb, colb, qb], (q, k, v, cq, ck, o, lse, do),
        [qb, kb, kb, colb, rowb],
        [jax.ShapeDtypeStruct((s_len, hd), F32)] * 3
        + [jax.ShapeDtypeStruct((h, s_len, 1), F32), jax.ShapeDtypeStruct((h, s_len // tq, 1, tq), F32)],
        (), ("parallel", "arbitrary"), rider)


def _stack_heads(ref, first, g):
    return jnp.concatenate([ref[:, (first + j) * HEAD_DIM:(first + j + 1) * HEAD_DIM] for j in range(g)], axis=0)


def _window(prev_ref, cur_ref, hh):
    lanes = slice(hh * HEAD_DIM, (hh + 1) * HEAD_DIM)
    return jnp.concatenate([prev_ref[:, lanes], cur_ref[:, lanes]], axis=0).astype(BF16)


def _swa_band(g, w):
    t = lax.broadcasted_iota(jnp.int32, (g * w, 2 * w), 0) % w
    col = lax.broadcasted_iota(jnp.int32, (g * w, 2 * w), 1)
    rel = t + w - col
    band = (rel >= 0) & (rel < w)
    return jnp.where(jnp.stack([band & (col >= w), band]), 0.0, MASK_VALUE).astype(F32)


def _swa_probs(qs, kw, sink, band):
    s = lax.dot_general(qs, kw, _DIMS["nt"], preferred_element_type=F32) + band
    m = jnp.maximum(jnp.max(s, axis=-1, keepdims=True), sink)
    p = jnp.exp(s - m)
    ps = jnp.exp(sink - m)
    linv = 1.0 / (jnp.sum(p, axis=-1, keepdims=True) + ps)
    return p * linv, ps * linv


def _swa_specs(hk, g, s_len):
    w = WINDOW
    assert s_len % w == 0
    hb = _heads_per_block(hk)
    qb = pl.BlockSpec((w, hb * g * HEAD_DIM), lambda pp, n: (n, pp))
    prev = pl.BlockSpec((w, hb * HEAD_DIM), lambda pp, n: (jnp.maximum(n - 1, 0), pp))
    cur = pl.BlockSpec((w, hb * HEAD_DIM), lambda pp, n: (n, pp))
    sb = pl.BlockSpec((hb, g * w, 1), lambda pp, n: (pp, 0, 0))
    band = pl.BlockSpec((None, g * w, 2 * w), lambda pp, n: (jnp.minimum(n, 1), 0, 0))
    return hb, qb, prev, cur, sb, band


def swa_fwd(q, k, v, sink, rider=None):
    s_len = q.shape[0]
    hk = k.shape[1] // HEAD_DIM
    g = q.shape[1] // k.shape[1]
    w, d = WINDOW, HEAD_DIM
    hb, qb, prev, cur, sb, bandb = _swa_specs(hk, g, s_len)

    def body(q_ref, kp_ref, kc_ref, vp_ref, vc_ref, sink_ref, band_ref, o_ref):
        for hh in range(hb):
            qs = (_stack_heads(q_ref, hh * g, g) * (HEAD_DIM ** -0.5)).astype(BF16)
            p, _ = _swa_probs(qs, _window(kp_ref, kc_ref, hh), sink_ref[hh], band_ref[...])
            o = jnp.dot(p.astype(BF16), _window(vp_ref, vc_ref, hh), preferred_element_type=F32)
            for j in range(g):
                o_ref[:, (hh * g + j) * d:(hh * g + j + 1) * d] = o[j * w:(j + 1) * w]

    (o,), rid = _call("swa_fwd", body, (hk // hb, s_len // w), [qb, prev, cur, prev, cur, sb, bandb],
                      (q, k, k, v, v, sink, _swa_band(g, w)), [qb], [jax.ShapeDtypeStruct(q.shape, F32)], (),
                      ("parallel", "parallel"), rider)
    return o, rid


def swa_bwd(q, k, v, sink, o, do, rider=None):
    s_len = q.shape[0]
    hk = k.shape[1] // HEAD_DIM
    g = q.shape[1] // k.shape[1]
    w, d = WINDOW, HEAD_DIM
    scale = HEAD_DIM ** -0.5
    hb, qb, prev, cur, sb, bandb = _swa_specs(hk, g, s_len)

    def body(q_ref, kp_ref, kc_ref, vp_ref, vc_ref, sink_ref, band_ref, o_ref, do_ref, dq_ref, dkp_ref, dkc_ref, dvp_ref, dvc_ref,
             dsink_ref):
        @pl.when(pl.program_id(1) == 0)
        def _():
            dsink_ref[...] = jnp.zeros_like(dsink_ref)

        for hh in range(hb):
            lanes = slice(hh * d, (hh + 1) * d)
            qs = (_stack_heads(q_ref, hh * g, g) * scale).astype(BF16)
            kw, vw = _window(kp_ref, kc_ref, hh), _window(vp_ref, vc_ref, hh)
            p, ps = _swa_probs(qs, kw, sink_ref[hh], band_ref[...])
            dof = _stack_heads(do_ref, hh * g, g)
            dob = dof.astype(BF16)
            delta = jnp.sum(dof * _stack_heads(o_ref, hh * g, g), axis=-1, keepdims=True)
            dp = lax.dot_general(dob, vw, _DIMS["nt"], preferred_element_type=F32)
            ds = p * (dp - delta)
            dsb = ds.astype(BF16)
            dsink_ref[hh] -= ps * delta
            dq = jnp.dot(dsb, kw, preferred_element_type=F32) * scale
            for j in range(g):
                dq_ref[:, (hh * g + j) * d:(hh * g + j + 1) * d] = dq[j * w:(j + 1) * w]
            dkw = lax.dot_general(dsb, qs, _DIMS["tn"], preferred_element_type=F32)
            dvw = lax.dot_general(p.astype(BF16), dob, _DIMS["tn"], preferred_element_type=F32)
            dkp_ref[:, lanes] = dkw[:w]
            dkc_ref[:, lanes] = dkw[w:]
            dvp_ref[:, lanes] = dvw[:w]
            dvc_ref[:, lanes] = dvw[w:]

    kv_shape = jax.ShapeDtypeStruct(k.shape, F32)
    (dq, dkp, dkc, dvp, dvc, dsink), rid = _call(
        "swa_bwd", body, (hk // hb, s_len // w), [qb, prev, cur, prev, cur, sb, bandb, qb, qb],
        (q, k, k, v, v, sink, _swa_band(g, w), o, do),
        [qb, cur, cur, cur, cur, sb],
        [jax.ShapeDtypeStruct(q.shape, F32), kv_shape, kv_shape, kv_shape, kv_shape, jax.ShapeDtypeStruct((hk, g * w, 1), F32)],
        (), ("parallel", "arbitrary"), rider)

    def shift_up(a):
        return jnp.concatenate([a[w:], jnp.zeros_like(a[:w])], axis=0)

    return (dq, dkc + shift_up(dkp), dvc + shift_up(dvp), dsink), rid


def loss_call(y, target):
    s, d = y.shape
    tm = _pick(s, (512, 256, 128))

    def body(y_ref, t_ref, l_ref, dy_ref, dyb_ref):
        e = y_ref[...] - t_ref[...]
        dy = e * (1.0 / d)
        dy_ref[...] = dy
        dyb_ref[...] = dy.astype(BF16)

        @pl.when(pl.program_id(0) == 0)
        def _():
            l_ref[...] = jnp.zeros_like(l_ref)

        l_ref[...] += jnp.sum(jnp.sum(e * e, axis=0, keepdims=True), axis=1, keepdims=True) * (0.5 / d)

    row = pl.BlockSpec((tm, d), lambda i: (i, 0))
    (l, dy, dyb), _ = _call("loss_head", body, (s // tm,), [row, row], (y, target), [pl.BlockSpec((1, 1), lambda i: (0, 0)), row, row],
                            [jax.ShapeDtypeStruct((1, 1), F32), jax.ShapeDtypeStruct((s, d), F32), jax.ShapeDtypeStruct((s, d), BF16)],
                            (), ("arbitrary",))
    return l[0, 0], dy, dyb


def _row_tile(rows, cols, itemsize, block_bytes=1 << 20):
    target = max(16, block_bytes // (cols * itemsize))
    fits = [t for t in range(16, rows + 1, 16) if rows % t == 0 and t <= target]
    return fits[-1] if fits else rows


CAST_STEPS = 8


def cast_place(name, ws, p_idx, rider=None):
    n = len(ws)
    assert all(w.shape[0] % (16 * CAST_STEPS) == 0 for w in ws), [w.shape for w in ws]

    def body(p_ref, *refs):
        for w_ref, o_ref in zip(refs[:n], refs[n:]):
            o_ref[...] = w_ref[...].astype(BF16)

    return _call(
        name, body, (CAST_STEPS,), [pl.BlockSpec((w.shape[0] // CAST_STEPS, w.shape[1]), lambda i, pr: (i, 0)) for w in ws], tuple(ws),
        [pl.BlockSpec((None, w.shape[0] // CAST_STEPS, w.shape[1]), lambda i, pr: (pr[0], i, 0)) for w in ws],
        [jax.ShapeDtypeStruct((N_CHIPS,) + w.shape, BF16) for w in ws], (), ("parallel",), rider, prefetch=(p_idx,))


def chip_sum(name, grad, theirs, c_idx):
    ns, r, cols = grad.shape
    rh = r // 2
    tr = _row_tile(rh, cols, 2, 2 << 20)
    nb = rh // tr

    def body(c_ref, a_ref, b_ref, o_ref):
        o_ref[...] = (a_ref[...].astype(F32) + b_ref[...].astype(F32)).astype(o_ref.dtype)

    return pl.pallas_call(
        body, name=name,
        grid_spec=pltpu.PrefetchScalarGridSpec(
            num_scalar_prefetch=1, grid=(ns, nb),
            in_specs=[pl.BlockSpec((None, tr, cols), lambda q, i, cr: (q, cr[0] * nb + i, 0)),
                      pl.BlockSpec((None, tr, cols), lambda q, i, cr: (q, i, 0))],
            out_specs=pl.BlockSpec((None, tr, cols), lambda q, i, cr: (q, i, 0))),
        out_shape=jax.ShapeDtypeStruct((ns, rh, cols), BF16),
        compiler_params=pltpu.CompilerParams(dimension_semantics=("parallel", "parallel"), vmem_limit_bytes=VMEM_LIMIT),
    )(c_idx, grad, theirs)


def owner_sum(name, sums, got, pc_idx):
    ns, rh, cols = sums.shape
    tr = _row_tile(rh, cols, 4, 2 << 20)
    nb = rh // tr

    def body(pc_ref, a_ref, b_ref, o_ref):
        o_ref[...] = ((a_ref[...].astype(F32) + b_ref[0].astype(F32)) + b_ref[1].astype(F32)) + b_ref[2].astype(F32)

    return pl.pallas_call(
        body, name=name,
        grid_spec=pltpu.PrefetchScalarGridSpec(
            num_scalar_prefetch=1, grid=(nb,),
            in_specs=[pl.BlockSpec((None, tr, cols), lambda i, pc: (pc[0], i, 0)),
                      pl.BlockSpec((3, tr, cols), lambda i, pc: (0, i, 0))],
            out_specs=pl.BlockSpec((tr, cols), lambda i, pc: (pc[1] * nb + i, 0))),
        out_shape=jax.ShapeDtypeStruct((2 * rh, cols), F32),
        compiler_params=pltpu.CompilerParams(dimension_semantics=("parallel",), vmem_limit_bytes=VMEM_LIMIT),
    )(pc_idx, sums, got)


def adamw(name, w, g, m, v):
    r, cols = w.shape
    tr = _row_tile(r, cols, 4)
    c1 = 1.0 / (1.0 - ADAM_B1 ** ADAM_STEP)
    c2 = 1.0 / (1.0 - ADAM_B2 ** ADAM_STEP)

    def body(w_ref, g_ref, m_ref, v_ref, go_ref, d_ref, nm_ref, nv_ref):
        gv = g_ref[...]
        nm = ADAM_B1 * m_ref[...] + (1.0 - ADAM_B1) * gv
        nv = ADAM_B2 * v_ref[...] + (1.0 - ADAM_B2) * (gv * gv)
        go_ref[...] = gv
        d_ref[...] = -ADAM_LR * ((nm * c1) / (jnp.sqrt(nv * c2) + ADAM_EPS) + ADAM_WD * w_ref[...])
        nm_ref[...] = nm
        nv_ref[...] = nv

    blk = pl.BlockSpec((tr, cols), lambda i: (i, 0))
    return _call(name, body, (r // tr,), [blk] * 4, (w, g, m, v), [blk] * 4, [jax.ShapeDtypeStruct((r, cols), F32)] * 4, (), ("parallel",))


def _win_layout(d_model):
    hf = hq = d_model // (2 * HEAD_DIM)
    hk = hq // 4
    sizes = [hf * HEAD_DIM, hf * HEAD_DIM, hf * HEAD_DIM, hf, hq * HEAD_DIM, hk * HEAD_DIM, hk * HEAD_DIM]
    return hf, hq, hk, sizes


class WinPlan:
    def __init__(self, d_model, ns=N_CHIPS):
        self.hf, self.hq, self.hk, self.sizes = _win_layout(d_model)
        self.ns, self.cs = ns, sum(self.sizes) // ns
        self.jump_at = sum(self.sizes[:4])
        self.jump_by = -self.jump_at % _LANES
        self.base = [self.pos(s * self.cs) // _LANES * _LANES for s in range(ns)]
        ends = [self.pos((s + 1) * self.cs - 1) + 1 - self.base[s] for s in range(ns)]
        self.width = -(-max(ends) // _LANES) * _LANES
        self.total = -(-max(b + self.width for b in self.base) // 1024) * 1024
        starts = [0]
        for sz in self.sizes:
            starts.append(starts[-1] + sz)
        self.segments = [(self.pos(a), sz) for a, sz in zip(starts, self.sizes)]

    def pos(self, g):
        return g if g < self.jump_at else g + self.jump_by

    def pieces(self, s):
        g0, g1 = s * self.cs, (s + 1) * self.cs
        cuts = [g0] + ([self.jump_at] if g0 < self.jump_at < g1 else []) + [g1]
        return [(a - g0, b - a, self.pos(a) - self.base[s]) for a, b in zip(cuts[:-1], cuts[1:])]

    def place(self, w, s):
        parts, at = [], 0
        for t0, n, j0 in self.pieces(s):
            parts += [jnp.zeros((w.shape[0], j0 - at), w.dtype), w[:, t0:t0 + n]]
            at = j0 + n
        return jnp.concatenate(parts + [jnp.zeros((w.shape[0], self.width - at), w.dtype)], axis=1)

    def unplace(self, slab, s):
        return jnp.concatenate([slab[:, j0:j0 + n] for _, n, j0 in self.pieces(s)], axis=1)

    def assemble(self, slabs):
        return sum(jnp.pad(slabs[s], ((0, 0), (b, self.total - b - self.width))) for s, b in enumerate(self.base))

    def split(self, full):
        return jnp.stack([full[:, b:b + self.width] for b in self.base])


def _attn_inputs(proj, sm, positions):
    s_len = proj.shape[0]
    plan = WinPlan(sm["norm_mix_g"].shape[0])
    hf, hq, hk = plan.hf, plan.hq, plan.hk
    grp = hq // hk
    q_f, k_f, v_f, f_logit, q_s, k_s, v_s = [proj[:, a:a + n] for a, n in plan.segments]

    q_f = head_rms("fox_qnorm", q_f, sm["fox_q_norm_g"])
    k_f = head_rms("fox_knorm", k_f, sm["fox_k_norm_g"])
    log_f = jax.nn.log_sigmoid(f_logit + sm["b_forget"])
    c = time_cumsum(log_f.T)

    inv_freq = ROPE_THETA ** (-jnp.arange(0, HEAD_DIM, 2, dtype=F32) / HEAD_DIM)
    ang = positions.astype(F32)[:, None] * inv_freq
    cos, sin = jnp.cos(ang), jnp.sin(ang)
    q_s = head_rms_rope("swa_qnorm", q_s, sm["swa_q_norm_g"], cos, sin)
    k_s = head_rms_rope("swa_knorm", k_s, sm["swa_k_norm_g"], cos, sin)
    sink = jnp.broadcast_to(sm["swa_sinks"].reshape(hk, grp, 1, 1), (hk, grp, WINDOW, 1)).reshape(hk, grp * WINDOW, 1)
    tq = fox_tile(s_len)
    return (q_f, k_f, v_f, c[:, :, None], c.reshape(hf, s_len // tq, 1, tq)), (q_s, k_s, v_s, sink)


_BIG = ("ffn1_w_gate", "ffn1_w_up", "ffn1_w_down", "w_in", "w_out", "ffn2_w_gate", "ffn2_w_up", "ffn2_w_down")
_SMALL = ("norm_ffn1_g", "norm_mix_g", "b_forget", "fox_q_norm_g", "fox_k_norm_g", "swa_q_norm_g", "swa_k_norm_g", "swa_sinks",
          "out_norm_fox_g", "out_norm_swa_g", "norm_ffn2_g")
_ATTN_SMALL = ("norm_mix_g", "b_forget", "fox_q_norm_g", "fox_k_norm_g", "swa_q_norm_g", "swa_k_norm_g", "swa_sinks")
_ALL = ("norm_ffn1_g", "ffn1_w_gate", "ffn1_w_up", "ffn1_w_down", "norm_mix_g", "w_in", "b_forget", "fox_q_norm_g", "fox_k_norm_g",
        "swa_q_norm_g", "swa_k_norm_g", "swa_sinks", "out_norm_fox_g", "out_norm_swa_g", "w_out", "norm_ffn2_g", "ffn2_w_gate",
        "ffn2_w_up", "ffn2_w_down")


def _pack_small(d):
    parts = []
    for k in _SMALL:
        v = d[k].reshape(-1)
        rows = -(-v.shape[0] // _LANES)
        parts.append(jnp.pad(v, (0, rows * _LANES - v.shape[0])).reshape(rows, _LANES))
    a = jnp.concatenate(parts, axis=0)
    return jnp.pad(a, ((0, -a.shape[0] % 8), (0, 0)))


def _unpack_small(a, like):
    out, r0 = {}, 0
    for k in _SMALL:
        nvals = like[k].shape[1]
        rows = -(-nvals // _LANES)
        out[k] = a[r0:r0 + rows].reshape(-1)[:nvals].reshape(1, nvals)
        r0 += rows
    return out


def _stacked(w):
    return w.reshape(-1, w.shape[-1])


def _local_step(shards, sm, x, positions, target, p_idx, c_idx, pc_idx):
    ns = N_CHIPS
    full = {}

    def fetch(*jobs):
        names = list(dict.fromkeys(n for n, _, _ in jobs))
        return names, gather([bufs[n] for n in names], [(names.index(n), kind, part) for n, kind, part in jobs])

    def take(names, rid):
        for n, b in zip(names, rid[0]):
            bufs[n] = b

    n1 = ["ffn1_w_gate", "ffn1_w_up", "ffn1_w_down"]
    n2 = ["ffn2_w_gate", "ffn2_w_up", "ffn2_w_down"]
    later = ["w_in", "w_out"] + n2
    placed, _ = cast_place("cast_place_ffn1", [shards[n] for n in n1], p_idx)
    bufs = dict(zip(n1, placed))
    gate1, up1, down1 = n1
    gate2, up2, down2 = n2
    names, rider = fetch((gate1, "ici", WHOLE), (up1, "ici", WHOLE))
    placed, rid = cast_place("cast_place_later", [shards[n] for n in later], p_idx, rider=rider)
    bufs.update(zip(later, placed))
    take(names, rid)
    names, rider = fetch((gate1, "d2d", WHOLE), (up1, "d2d", WHOLE))
    xn1, r1, rid = rms_fwd("ffn1_norm", x, sm["norm_ffn1_g"], BF16, rider=rider)
    take(names, rid)
    names, rider = fetch((down1, "ici", WHOLE))
    (hid1, hdg1, hdu1), rid = ffn_gu("ffn1_gu", xn1, bufs[gate1], bufs[up1], rider=rider)
    take(names, rid)
    names, rider = fetch((down1, "d2d", WHOLE))
    take(names, run_step("gather_d2d_ffn1_down", rider))
    wd1 = _stacked(bufs[down1])
    names, rider = fetch(("w_in", "ici", WHOLE))
    h1, rid = mm_nn("ffn1_down", hid1, wd1, scale=0.5, resid=x, rider=rider)
    take(names, rid)

    names, rider = fetch(("w_in", "d2d", WHOLE))
    u, r_mix, rid = rms_fwd("mix_norm", h1, sm["norm_mix_g"], BF16, rider=rider)
    take(names, rid)
    names, rider = fetch(("w_out", "ici", WHOLE), (gate2, "ici", (0, 1, 4)))
    plan = WinPlan(x.shape[1])
    win = plan.assemble(bufs["w_in"])
    proj, rid = mm_nn("mix_inproj", u, win, rider=rider)
    take(names, rid)
    sm_attn = {k: sm[k] for k in _ATTN_SMALL}
    (fox_in, swa_in), attn_vjp = jax.vjp(lambda pr, s: _attn_inputs(pr, s, positions), proj, sm_attn)
    names, rider = fetch((gate2, "ici", (1, 4, 4)), ("w_out", "d2d", WHOLE), (gate2, "d2d", (0, 1, 4)))
    (o_f, lse), rid = fox_fwd(*fox_in, rider=rider)
    take(names, rid)
    names, rider = fetch((up2, "ici", (0, 3, 4)), (gate2, "d2d", (1, 4, 4)))
    o_s, rid = swa_fwd(*swa_in, rider=rider)
    take(names, rid)
    o_fox, o_swa = o_f, o_s
    nf, r_fox = rms_fwd("out_norm_fox", o_fox, sm["out_norm_fox_g"], BF16)
    nsw, r_swa = rms_fwd("out_norm_swa", o_swa, sm["out_norm_swa_g"], BF16)
    o = jnp.concatenate([nf, nsw], axis=-1)
    wout = _stacked(bufs["w_out"])
    names, rider = fetch((up2, "ici", (3, 4, 4)), (up2, "d2d", (0, 3, 4)))
    h2, rid = mm_nn("out_proj", o, wout, resid=h1, rider=rider)
    take(names, rid)

    names, rider = fetch((up2, "d2d", (3, 4, 4)))
    xn2, r2, rid = rms_fwd("ffn2_norm", h2, sm["norm_ffn2_g"], BF16, rider=rider)
    take(names, rid)
    names, rider = fetch((down2, "ici", WHOLE))
    (hid2, hdg2, hdu2), rid = ffn_gu("ffn2_gu", xn2, bufs[gate2], bufs[up2], rider=rider)
    take(names, rid)
    names, rider = fetch((down2, "d2d", WHOLE))
    take(names, run_step("gather_d2d_ffn2_down", rider))
    wd2 = _stacked(bufs["ffn2_w_down"])
    y, _ = mm_nn("ffn2_down", hid2, wd2, scale=0.5, resid=h2)
    loss, dy, dy_b = loss_call(y, target)

    red = {}

    def grad(n, g):
        red[n] = {"grad": g.reshape(ns, -1, g.shape[-1])}

    def ride(*steps):
        def done(rid):
            a0 = n0 = 0
            for rd, cb in steps:
                cb(rid[0][a0:a0 + len(rd.aliased)], rid[1][n0:n0 + len(rd.news)])
                a0, n0 = a0 + len(rd.aliased), n0 + len(rd.news)

        return (combine(*[s[0] for s in steps]) if len(steps) > 1 else steps[0][0]), done

    def xchg(*names):
        def cb(al, news):
            for n, t in zip(names, news):
                red[n]["sum"] = chip_sum("chip_sum_" + n, red[n]["grad"], t, c_idx)

        return exchange_halves([red[n]["grad"] for n in names]), cb

    def scat(n, part=WHOLE):
        def cb(al, news):
            red[n]["got"] = (al or news)[0]

        return scatter_to_owner([red[n]["sum"]], [red[n]["got"]] if "got" in red[n] else None, part), cb

    def own(n):
        red[n]["half"] = owner_sum("owner_sum_" + n, red[n]["sum"], red[n]["got"], pc_idx)

    def join(*names):
        return join_halves([red[n]["half"] for n in names]), lambda al, news: full.update(zip(names, al))

    dwd2, _ = mm_tn("ffn2_dwd", hid2, dy_b, out_dtype=BF16, scale=0.5)
    grad(down2, dwd2)
    rider, done = ride(xchg(down2))
    (dg2, du2), rid = ffn_dh("ffn2_dh", dy_b, wd2, hdg2, hdu2, ns, 0.5, rider=rider)
    done(rid)
    rider, done = ride(scat(down2, (0, 1, 2)))
    dwg2, rid = mm_tn_sharded("ffn2_dwg", xn2, dg2, ns, rider=rider)
    done(rid)
    grad(gate2, dwg2)
    rider, done = ride(scat(down2, (1, 2, 2)), xchg(gate2))
    dwu2, rid = mm_tn_sharded("ffn2_dwu", xn2, du2, ns, rider=rider)
    done(rid)
    grad(up2, dwu2)
    rider, done = ride(scat(gate2, (0, 1, 2)), xchg(up2))
    dxn, rid = mm_nt_sharded("ffn2_dxn_g", dg2, bufs[gate2], rider=rider)
    done(rid)
    rider, done = ride(scat(gate2, (1, 2, 2)))
    dxn, rid = mm_nt_sharded("ffn2_dxn_u", du2, bufs[up2], resid=dxn, rider=rider)
    done(rid)
    dh2, dgain_ffn2, dh2_b = rms_bwd("ffn2_dnorm", h2, sm["norm_ffn2_g"], r2, dxn, dres=dy, also_bf16=True)
    own(down2)
    own(gate2)

    do, _ = mm_nt("out_do", dh2_b, wout)
    dwout, _ = mm_tn("out_dw", o, dh2_b, out_dtype=BF16)
    cf = o_fox.shape[1]
    d_fox, dgain_fox = rms_bwd("out_dnorm_fox", o_fox, sm["out_norm_fox_g"], r_fox, do[:, :cf])
    d_swa, dgain_swa = rms_bwd("out_dnorm_swa", o_swa, sm["out_norm_swa_g"], r_swa, do[:, cf:])
    grad("w_out", dwout)
    rider, done = ride(scat(up2))
    swa_cts, rid = swa_bwd(*swa_in, o_s, d_swa, rider=rider)
    done(rid)
    own(up2)
    rider, done = ride(xchg("w_out"), join(down2, gate2, up2))
    fox_cts, rid = fox_bwd(*fox_in, o_f, lse, d_fox, rider=rider)
    done(rid)
    dproj, dsm_attn = attn_vjp((tuple(fox_cts), tuple(swa_cts)))
    dproj = dproj.astype(BF16)

    rider, done = ride(scat("w_out"))
    du, rid = mm_nt("mix_du", dproj, win, rider=rider)
    done(rid)
    dwin, _ = mm_tn("mix_dwin", u, dproj, out_dtype=BF16)
    grad("w_in", plan.split(dwin))
    rider, done = ride(xchg("w_in"))
    dh1, dgain_mix, dh1_b, rid = rms_bwd("mix_dnorm", h1, sm["norm_mix_g"], r_mix, du, dres=dh2, rider=rider, also_bf16=True)
    done(rid)
    own("w_out")

    rider, done = ride(scat("w_in", (0, 1, 2)))
    dwd1, rid = mm_tn("ffn1_dwd", hid1, dh1_b, out_dtype=BF16, scale=0.5, rider=rider)
    done(rid)
    grad(down1, dwd1)
    rider, done = ride(scat("w_in", (1, 2, 2)), xchg(down1))
    (dg1, du1), rid = ffn_dh("ffn1_dh", dh1_b, wd1, hdg1, hdu1, ns, 0.5, rider=rider)
    done(rid)
    own("w_in")
    rider, done = ride(scat(down1, (0, 1, 2)), join("w_out"))
    dwg1, rid = mm_tn_sharded("ffn1_dwg", xn1, dg1, ns, rider=rider)
    done(rid)
    grad(gate1, dwg1)
    rider, done = ride(scat(down1, (1, 2, 2)), xchg(gate1), join("w_in"))
    dwu1, rid = mm_tn_sharded("ffn1_dwu", xn1, du1, ns, rider=rider)
    done(rid)
    grad(up1, dwu1)
    own(down1)
    rider, done = ride(scat(gate1, (0, 1, 2)), xchg(up1), join(down1))
    dxn, rid = mm_nt_sharded("ffn1_dxn_g", dg1, bufs[gate1], rider=rider)
    done(rid)
    rider, done = ride(scat(gate1, (1, 2, 2)), scat(up1, (0, 1, 4)))
    dxn, rid = mm_nt_sharded("ffn1_dxn_u", du1, bufs[up1], resid=dxn, rider=rider)
    done(rid)
    own(gate1)
    rider, done = ride(scat(up1, (1, 2, 4)), join(gate1))
    dx, dgain_ffn1, rid = rms_bwd("ffn1_dnorm", x, sm["norm_ffn1_g"], r1, dxn, dres=dh1, rider=rider)
    done(rid)

    rider, done = ride(scat(up1, (2, 4, 4)))
    done(run_step("reduce_tail", rider))
    own(up1)
    rider, done = ride(join(up1))
    done(run_step("join_tail", rider))

    g_small = dict(dsm_attn)
    g_small["norm_mix_g"] = g_small["norm_mix_g"] + dgain_mix
    g_small.update(norm_ffn1_g=dgain_ffn1, norm_ffn2_g=dgain_ffn2, out_norm_fox_g=dgain_fox, out_norm_swa_g=dgain_swa)
    return loss, dx, full, g_small


def kernel(x, positions, norm_ffn1_g, ffn1_w_gate, ffn1_w_up, ffn1_w_down, norm_mix_g, w_in, b_forget, fox_q_norm_g, fox_k_norm_g, swa_q_norm_g, swa_k_norm_g, swa_sinks, out_norm_fox_g, out_norm_swa_g, w_out, norm_ffn2_g, ffn2_w_gate, ffn2_w_up, ffn2_w_down, loss_target, m_norm_ffn1_g, m_ffn1_w_gate, m_ffn1_w_up, m_ffn1_w_down, m_norm_mix_g, m_w_in, m_b_forget, m_fox_q_norm_g, m_fox_k_norm_g, m_swa_q_norm_g, m_swa_k_norm_g, m_swa_sinks, m_out_norm_fox_g, m_out_norm_swa_g, m_w_out, m_norm_ffn2_g, m_ffn2_w_gate, m_ffn2_w_up, m_ffn2_w_down, v_norm_ffn1_g, v_ffn1_w_gate, v_ffn1_w_up, v_ffn1_w_down, v_norm_mix_g, v_w_in, v_b_forget, v_fox_q_norm_g, v_fox_k_norm_g, v_swa_q_norm_g, v_swa_k_norm_g, v_swa_sinks, v_out_norm_fox_g, v_out_norm_swa_g, v_w_out, v_norm_ffn2_g, v_ffn2_w_gate, v_ffn2_w_up, v_ffn2_w_down):
    args = dict(locals())
    w = {k: args[k] for k in _ALL}
    m = {k: args["m_" + k] for k in _ALL}
    v = {k: args["v_" + k] for k in _ALL}
    c_idx = lax.axis_index("c").astype(jnp.int32).reshape(1)
    p_idx = (2 * lax.axis_index("x") + lax.axis_index("y")).astype(jnp.int32).reshape(1)
    pc_idx = jnp.concatenate([p_idx, c_idx])

    small = {k: w[k] for k in _SMALL}
    shards = {k: w[k][0] for k in _BIG}
    plan = WinPlan(x.shape[-1])
    shards["w_in"] = lax.switch(p_idx[0], [functools.partial(plan.place, s=s) for s in range(N_CHIPS)], shards["w_in"])
    loss, grad_x, g_shard, g_small = _local_step(shards, {k: w[k][0] for k in _SMALL}, x[0], positions[0], loss_target[0],
                                                 p_idx, c_idx, pc_idx)
    g_shard["w_in"] = lax.switch(p_idx[0], [functools.partial(plan.unplace, s=s) for s in range(N_CHIPS)], g_shard["w_in"])
    loss = lax.psum(loss, ("x", "y", "c"))
    g_small_sum = _unpack_small(all_reduce_small(_pack_small({k: g_small[k].reshape(1, -1) for k in _SMALL})), small)

    grad_w, delta, new_m, new_v = {}, {}, {}, {}
    for k in _BIG:
        (g, d, nm, nv), _ = adamw("adamw_" + k, w[k][0], g_shard[k], m[k][0], v[k][0])
        grad_w[k], delta[k], new_m[k], new_v[k] = g[None], d[None], nm[None], nv[None]
    (_, d, nm, nv), _ = adamw("adamw_small", _pack_small(small), _pack_small(g_small_sum), _pack_small({k: m[k] for k in _SMALL}),
                              _pack_small({k: v[k] for k in _SMALL}))
    grad_w.update(g_small_sum)
    delta.update(_unpack_small(d, small))
    new_m.update(_unpack_small(nm, small))
    new_v.update(_unpack_small(nv, small))

    return (loss, grad_x[None], *[grad_w[k] for k in _ALL], *[delta[k] for k in _ALL], *[new_m[k] for k in _ALL], *[new_v[k] for k in _ALL])
```

```python
import functools

import jax
import jax.numpy as jnp
from jax import lax
from jax.experimental import pallas as pl
from jax.experimental.pallas import tpu as pltpu

F32 = jnp.float32
BF16 = jnp.bfloat16

HEAD_DIM = 64
WINDOW = 128
ROPE_THETA = 10000.0
EPS = 1e-6
N_CHIPS = 4
N_DEV = 8

ADAM_LR = 0.001
ADAM_B1 = 0.9
ADAM_B2 = 0.999
ADAM_EPS = 1e-08
ADAM_WD = 0.01
ADAM_STEP = 10

V7X_VMEM_BYTES = 64 * 1024 * 1024
VMEM_LIMIT = V7X_VMEM_BYTES - 8 * 1024 * 1024
MASK_VALUE = -1e30

_MESH = pl.DeviceIdType.MESH
_HBM = pl.BlockSpec(memory_space=pl.ANY)
_DIMS = {"nn": (((1,), (0,)), ((), ())), "nt": (((1,), (1,)), ((), ())), "tn": (((0,), (0,)), ((), ()))}


def _pick(n, prefs):
    for p in prefs:
        if n % p == 0:
            return p
    return n


class Rider:
    def __init__(self, reads, aliased, news, nsem, build):
        self.reads, self.aliased, self.news, self.nsem, self.build = list(reads), list(aliased), list(news), nsem, build


class _Shifted:
    def __init__(self, ref, off):
        self.ref, self.off = ref, off

    @property
    def at(self):
        return self

    def __getitem__(self, k):
        return self.ref.at[k + self.off]


def combine(*riders):
    def build(reads, al, news, ssem, rsem):
        out = ([], [], [])
        r0 = a0 = n0 = s0 = 0
        for rd in riders:
            nr, na, nn = len(rd.reads), len(rd.aliased), len(rd.news)
            part = rd.build(reads[r0:r0 + nr], al[a0:a0 + na], news[n0:n0 + nn], _Shifted(ssem, s0), _Shifted(rsem, s0))
            for acc, lst in zip(out, part):
                acc.extend(lst)
            r0, a0, n0, s0 = r0 + nr, a0 + na, n0 + nn, s0 + rd.nsem
        return out

    return Rider(sum((r.reads for r in riders), []), sum((r.aliased for r in riders), []), sum((r.news for r in riders), []),
                 sum(r.nsem for r in riders), build)


def _me():
    return lax.axis_index("x"), lax.axis_index("y"), lax.axis_index("c")


def _other_chips(x, y):
    return [(1 - x, y), (x, 1 - y), (1 - x, 1 - y)]


WHOLE = (0, 1, 1)


def _rows(ref, start, rows, part=WHOLE):
    k0, k1, n = part
    assert rows % n == 0, (rows, part)
    idx = (slice(None),) * (len(ref.shape) - 2) + (pl.ds(start + k0 * (rows // n), (k1 - k0) * (rows // n)), slice(None))
    return ref.at[idx]


def _half(ref, h, part=WHOLE):
    rows = ref.shape[-2] // 2
    return _rows(ref, h * rows, rows, part)


def _remote(src, dst, ssem, rsem, k, to):
    return pltpu.make_async_remote_copy(src_ref=src, dst_ref=dst, send_sem=ssem.at[k], recv_sem=rsem.at[k], device_id=to,
                                        device_id_type=_MESH)


def _later(*args):
    return functools.partial(_remote, *args)


def gather(bufs, jobs):
    def build(reads, al, news, ssem, rsem):
        x, y, c = _me()
        p = 2 * x + y
        starts, arrivals = [], []
        for n, (b, kind, part) in enumerate(jobs):
            for j, chip in enumerate(_other_chips(x, y)):
                q = 2 * chip[0] + chip[1]
                if kind == "ici":
                    src, landing, to = _half(al[b].at[p], c, part), _half(al[b].at[q], c, part), (*chip, c)
                else:
                    src, landing, to = _half(al[b].at[q], c, part), _half(al[b].at[q], 1 - c, part), (x, y, 1 - c)
                starts.append(_later(src, src, ssem, rsem, 3 * n + j, to))
                arrivals.append(_later(landing, landing, ssem, rsem, 3 * n + j, to))
        return starts, arrivals, starts

    return Rider([], bufs, [], 3 * len(jobs), build)


def exchange_halves(grads):
    def build(reads, al, news, ssem, rsem):
        x, y, c = _me()
        cps = [_later(_half(g, 1 - c), t, ssem, rsem, w, (x, y, 1 - c)) for w, (g, t) in enumerate(zip(reads, news))]
        return cps, cps, cps

    return Rider(grads, [], [jax.ShapeDtypeStruct((g.shape[0], g.shape[1] // 2, g.shape[2]), g.dtype) for g in grads], len(grads), build)


def scatter_to_owner(sums, gots=None, part=WHOLE):
    def build(reads, al, news, ssem, rsem):
        x, y, c = _me()
        cps = []
        for w, (s, got) in enumerate(zip(reads, al or news)):
            rows = s.shape[-2]
            for j, chip in enumerate(_other_chips(x, y)):
                cps.append(_later(_rows(s.at[2 * chip[0] + chip[1]], 0, rows, part), _rows(got.at[j], 0, rows, part), ssem, rsem,
                                  3 * w + j, (*chip, c)))
        return cps, cps, cps

    news = [] if gots else [jax.ShapeDtypeStruct((3,) + s.shape[1:], s.dtype) for s in sums]
    return Rider(sums, gots or [], news, 3 * len(sums), build)


def join_halves(fulls):
    def build(reads, al, news, ssem, rsem):
        x, y, c = _me()
        starts, arrivals = [], []
        for w, f in enumerate(al):
            mine, landing = _half(f, c), _half(f, 1 - c)
            starts.append(_later(mine, mine, ssem, rsem, w, (x, y, 1 - c)))
            arrivals.append(_later(landing, landing, ssem, rsem, w, (x, y, 1 - c)))
        return starts, arrivals, starts

    return Rider([], fulls, [], len(fulls), build)


def _start_and_wait(rider, reads, al, news, ssem, rsem, first, last):
    @pl.when(first)
    def _():
        for cp in rider.build(reads, al, news, ssem, rsem)[0]:
            cp().start()

    def finish():
        @pl.when(last)
        def _():
            _, arrivals, sends = rider.build(reads, al, news, ssem, rsem)
            for cp in arrivals:
                cp().wait_recv()
            for cp in sends:
                cp().wait_send()

    return finish


def _call(name, body, grid, in_specs, args, out_specs, out_shape, scratch=(), semantics=None, rider=None, prefetch=()):
    n_pre, n_in, n_out, n_scr = len(prefetch), len(args), len(out_shape), len(scratch)
    nr, na, nn = (len(rider.reads), len(rider.aliased), len(rider.news)) if rider else (0, 0, 0)

    def wrapped(*refs):
        pre, refs = refs[:n_pre], refs[n_pre:]
        ins, reads = refs[:n_in], refs[n_in:n_in + nr]
        o0 = n_in + nr + na
        outs, al, news = refs[o0:o0 + n_out], refs[o0 + n_out:o0 + n_out + na], refs[o0 + n_out + na:o0 + n_out + na + nn]
        s0 = o0 + n_out + na + nn
        scr, (ssem, rsem) = refs[s0:s0 + n_scr], refs[s0 + n_scr:]
        first = functools.reduce(jnp.logical_and, [pl.program_id(a) == 0 for a in range(len(grid))])
        last = functools.reduce(jnp.logical_and, [pl.program_id(a) == g - 1 for a, g in enumerate(grid)])
        finish = _start_and_wait(rider, reads, al, news, ssem, rsem, first, last)
        body(*pre, *ins, *outs, *scr)
        finish()

    kernel_fn, all_in, all_out, shapes, scr = body, list(in_specs), list(out_specs), list(out_shape), list(scratch)
    operands, aliases = (*prefetch, *args), {}
    if rider:
        kernel_fn, semantics = wrapped, ("arbitrary",) * len(grid)
        all_in += [_HBM] * (nr + na)
        all_out += [_HBM] * (na + nn)
        shapes += [jax.ShapeDtypeStruct(a.shape, a.dtype) for a in rider.aliased] + rider.news
        scr += [pltpu.SemaphoreType.DMA((rider.nsem,)), pltpu.SemaphoreType.DMA((rider.nsem,))]
        operands += (*rider.reads, *rider.aliased)
        aliases = {n_pre + n_in + nr + i: n_out + i for i in range(na)}
    params = pltpu.CompilerParams(dimension_semantics=semantics, vmem_limit_bytes=VMEM_LIMIT)
    if n_pre:
        spec = pltpu.PrefetchScalarGridSpec(num_scalar_prefetch=n_pre, grid=grid, in_specs=all_in, out_specs=all_out, scratch_shapes=scr)
        outs = pl.pallas_call(kernel_fn, name=name, grid_spec=spec, out_shape=shapes, input_output_aliases=aliases, compiler_params=params)(*operands)
    else:
        outs = pl.pallas_call(kernel_fn, name=name, grid=grid, in_specs=all_in, out_specs=all_out, out_shape=shapes, scratch_shapes=scr,
                              input_output_aliases=aliases, compiler_params=params)(*operands)
    return list(outs[:n_out]), ((list(outs[n_out:n_out + na]), list(outs[n_out + na:])) if rider else None)


def run_step(name, rider):
    nr, na, nn = len(rider.reads), len(rider.aliased), len(rider.news)

    def body(*refs):
        reads = refs[:nr]
        al, news = refs[nr + na:nr + 2 * na], refs[nr + 2 * na:nr + 2 * na + nn]
        ssem, rsem = refs[nr + 2 * na + nn:]
        starts, arrivals, sends = rider.build(reads, al, news, ssem, rsem)
        for cp in starts:
            cp().start()
        for cp in arrivals:
            cp().wait_recv()
        for cp in sends:
            cp().wait_send()

    outs = pl.pallas_call(
        body, name=name, in_specs=[_HBM] * (nr + na), out_specs=[_HBM] * (na + nn),
        out_shape=[jax.ShapeDtypeStruct(a.shape, a.dtype) for a in rider.aliased] + rider.news,
        input_output_aliases={nr + i: i for i in range(na)},
        scratch_shapes=[pltpu.SemaphoreType.DMA((rider.nsem,)), pltpu.SemaphoreType.DMA((rider.nsem,))],
    )(*rider.reads, *rider.aliased)
    return list(outs[:na]), list(outs[na:])


def all_reduce_small(v):
    rows, lanes = v.shape

    def body(v_ref, o_ref, slots, send_sems, recv_sems):
        x, y, c = _me()
        me = 4 * x + 2 * y + c
        slots[me] = v_ref[...]
        cps = []
        for k in range(1, N_DEV):
            peer = (x ^ (k >> 2), y ^ ((k >> 1) & 1), c ^ (k & 1))
            cps.append(_remote(v_ref, slots.at[me], send_sems, recv_sems, k - 1, peer))
            cps[-1].start()
        for k in range(1, N_DEV):
            theirs = slots.at[me ^ k]
            _remote(theirs, theirs, send_sems, recv_sems, k - 1, (x, y, c)).wait_recv()
        for cp in cps:
            cp.wait_send()
        acc = slots[0]
        for i in range(1, N_DEV):
            acc = acc + slots[i]
        o_ref[...] = acc

    return pl.pallas_call(
        body, name="all_reduce_small",
        in_specs=[pl.BlockSpec(memory_space=pltpu.VMEM)], out_specs=pl.BlockSpec(memory_space=pltpu.VMEM),
        out_shape=jax.ShapeDtypeStruct((rows, lanes), F32),
        scratch_shapes=[pltpu.VMEM((N_DEV, rows, lanes), F32), pltpu.SemaphoreType.DMA((N_DEV - 1,)), pltpu.SemaphoreType.DMA((N_DEV - 1,))],
    )(v)


def _mm_call(name, mode, a, b, a_spec, b_spec, out_shape, out_spec, grid, acc_shape, scale=1.0, resid=None, resid_spec=None, rider=None):
    nk = grid[2]
    dims = _DIMS[mode]
    has_resid = resid is not None

    def body(*refs):
        a_ref, b_ref = refs[:2]
        r_ref = refs[2] if has_resid else None
        o_ref = refs[3] if has_resid else refs[2]

        def finish(r):
            if scale != 1.0:
                r = r * scale
            if has_resid:
                r = r_ref[...].astype(F32) + r
            o_ref[...] = r.astype(o_ref.dtype)

        part = lax.dot_general(a_ref[...].astype(BF16), b_ref[...].astype(BF16), dims, preferred_element_type=F32)
        if nk == 1:
            finish(part)
            return
        acc_ref = refs[-1]
        k = pl.program_id(2)

        @pl.when(k == 0)
        def _():
            acc_ref[...] = part

        @pl.when(k > 0)
        def _():
            acc_ref[...] += part

        @pl.when(k == nk - 1)
        def _():
            finish(acc_ref[...])

    in_specs = [a_spec, b_spec] + ([resid_spec] if has_resid else [])
    args = (a, b) + ((resid,) if has_resid else ())
    (out,), rid = _call(name, body, grid, in_specs, args, [out_spec], [out_shape], [pltpu.VMEM(acc_shape, F32)] if nk > 1 else [],
                        ("parallel", "parallel", "arbitrary"), rider)
    return out, rid


MM_VMEM_BUDGET = 40 * 1024 * 1024
_TILE_OPTS = (2048, 1408, 1024, 512, 256, 128)


def _tiles(m, n, kd, a_item, b_item, o_item, r_item=0, tm=None, tn=None, tk=None):
    def opts(full, fixed, cap):
        return [fixed] if fixed else [t for t in _TILE_OPTS if t <= cap and full % t == 0] or [full]

    best = None
    for cm in opts(m, tm, 1408):
        for cn in opts(n, tn, 1408):
            for ck in opts(kd, tk, 2048):
                blocks = cm * ck * a_item + ck * cn * b_item + cm * cn * (o_item + r_item)
                casts = (cm * ck * 2 if a_item == 4 else 0) + (ck * cn * 2 if b_item == 4 else 0)
                if 2 * blocks + cm * cn * 4 + casts <= MM_VMEM_BUDGET:
                    key = (cm * cn * ck, ck)
                    if best is None or key > best[0]:
                        best = (key, (cm, cn, ck))
    assert best is not None, (m, n, kd)
    return best[1]


def _item(x):
    return jnp.dtype(x.dtype).itemsize


def mm_nn(name, a, b, *, out_dtype=F32, scale=1.0, resid=None, rider=None):
    m, kd = a.shape
    n = b.shape[1]
    tm, tn, tk = _tiles(m, n, kd, _item(a), _item(b), jnp.dtype(out_dtype).itemsize, 0 if resid is None else _item(resid))
    o_spec = pl.BlockSpec((tm, tn), lambda i, j, k: (i, j))
    return _mm_call(
        name, "nn", a, b, pl.BlockSpec((tm, tk), lambda i, j, k: (i, k)), pl.BlockSpec((tk, tn), lambda i, j, k: (k, j)),
        jax.ShapeDtypeStruct((m, n), out_dtype), o_spec, (m // tm, n // tn, kd // tk), (tm, tn), scale, resid, o_spec, rider)


def mm_nt(name, a, b, *, out_dtype=F32, scale=1.0, resid=None, rider=None):
    m, kd = a.shape
    n = b.shape[0]
    tm, tn, tk = _tiles(m, n, kd, _item(a), _item(b), jnp.dtype(out_dtype).itemsize, 0 if resid is None else _item(resid))
    o_spec = pl.BlockSpec((tm, tn), lambda i, j, k: (i, j))
    return _mm_call(
        name, "nt", a, b, pl.BlockSpec((tm, tk), lambda i, j, k: (i, k)), pl.BlockSpec((tn, tk), lambda i, j, k: (j, k)),
        jax.ShapeDtypeStruct((m, n), out_dtype), o_spec, (m // tm, n // tn, kd // tk), (tm, tn), scale, resid, o_spec, rider)


def mm_tn(name, a, b, *, out_dtype=F32, scale=1.0, rider=None):
    kd, m = a.shape
    n = b.shape[1]
    tm, tn, tk = _tiles(m, n, kd, _item(a), _item(b), jnp.dtype(out_dtype).itemsize)
    return _mm_call(
        name, "tn", a, b, pl.BlockSpec((tk, tm), lambda i, j, k: (k, i)), pl.BlockSpec((tk, tn), lambda i, j, k: (k, j)),
        jax.ShapeDtypeStruct((m, n), out_dtype), pl.BlockSpec((tm, tn), lambda i, j, k: (i, j)),
        (m // tm, n // tn, kd // tk), (tm, tn), scale, rider=rider)


def mm_nt_sharded(name, a, w, *, resid=None, rider=None):
    m = a.shape[0]
    ns, n, c = w.shape
    tm, tn, _ = _tiles(m, n, c, _item(a), _item(w), 4, 0 if resid is None else _item(resid), tk=c)
    o_spec = pl.BlockSpec((tm, tn), lambda i, j, k: (i, j))
    return _mm_call(
        name, "nt", a, w, pl.BlockSpec((tm, c), lambda i, j, k: (i, k)), pl.BlockSpec((None, tn, c), lambda i, j, k: (k, j, 0)),
        jax.ShapeDtypeStruct((m, n), F32), o_spec, (m // tm, n // tn, ns), (tm, tn), 1.0, resid, o_spec, rider)


def mm_tn_sharded(name, a, b, ns, *, rider=None):
    kd, m = a.shape
    c = b.shape[1] // ns
    tm, _, tk = _tiles(m, c, kd, _item(a), _item(b), 2, tn=c)
    return _mm_call(
        name, "tn", a, b, pl.BlockSpec((tk, tm), lambda i, j, k: (k, i)), pl.BlockSpec((tk, c), lambda i, j, k: (k, j)),
        jax.ShapeDtypeStruct((ns, m, c), BF16), pl.BlockSpec((None, tm, c), lambda i, j, k: (j, i, 0)),
        (m // tm, ns, kd // tk), (tm, c), rider=rider)


def rms_fwd(name, x, g, out_dtype, rider=None):
    r, c = x.shape
    tm = _pick(r, (512, 256, 128, 64, 8))

    def body(x_ref, g_ref, y_ref, r_ref):
        xf = x_ref[...].astype(F32)
        rstd = lax.rsqrt(jnp.mean(xf * xf, axis=-1, keepdims=True) + EPS)
        y_ref[...] = ((xf * rstd) * g_ref[...]).astype(y_ref.dtype)
        r_ref[...] = rstd

    (y, rstd), rid = _call(
        name, body, (r // tm,), [pl.BlockSpec((tm, c), lambda i: (i, 0)), pl.BlockSpec((1, c), lambda i: (0, 0))], (x, g.reshape(1, c)),
        [pl.BlockSpec((tm, c), lambda i: (i, 0)), pl.BlockSpec((tm, 1), lambda i: (i, 0))],
        [jax.ShapeDtypeStruct((r, c), out_dtype), jax.ShapeDtypeStruct((r, 1), F32)], (), ("parallel",), rider)
    return (y, rstd) if rider is None else (y, rstd, rid)


def rms_bwd(name, x, g, rstd, dy, dres=None, rider=None, also_bf16=False):
    r, c = x.shape
    tm = _pick(r, (512, 256, 128, 64, 8))
    has_res = dres is not None

    def body(*refs):
        x_ref, g_ref, r_ref, dy_ref = refs[:4]
        dres_ref = refs[4] if has_res else None
        dx_ref, dg_ref = refs[4 + has_res:6 + has_res]
        xhat = x_ref[...].astype(F32) * r_ref[...]
        dyf = dy_ref[...].astype(F32)
        gdy = dyf * g_ref[...]
        dx = r_ref[...] * (gdy - xhat * jnp.mean(gdy * xhat, axis=-1, keepdims=True))
        if has_res:
            dx = dx + dres_ref[...]
        dx_ref[...] = dx
        if also_bf16:
            refs[-1][...] = dx.astype(BF16)

        @pl.when(pl.program_id(0) == 0)
        def _():
            dg_ref[...] = jnp.zeros_like(dg_ref)

        dg_ref[...] += jnp.sum(dyf * xhat, axis=0, keepdims=True)

    row = pl.BlockSpec((tm, c), lambda i: (i, 0))
    in_specs = [row, pl.BlockSpec((1, c), lambda i: (0, 0)), pl.BlockSpec((tm, 1), lambda i: (i, 0)), row] + ([row] if has_res else [])
    args = (x, g.reshape(1, c), rstd, dy) + ((dres,) if has_res else ())
    outs, rid = _call(name, body, (r // tm,), in_specs, args, [row, pl.BlockSpec((1, c), lambda i: (0, 0))] + [row] * also_bf16,
                      [jax.ShapeDtypeStruct((r, c), F32), jax.ShapeDtypeStruct((1, c), F32)] + [jax.ShapeDtypeStruct((r, c), BF16)] * also_bf16,
                      (), ("arbitrary",), rider)
    return (outs[0], outs[1].reshape(c), *outs[2:], *([] if rider is None else [rid]))


_LANES = 128


def _head_mean(v):
    if v.shape[1] == HEAD_DIM:
        return jnp.mean(v, axis=-1, keepdims=True)
    low = lax.broadcasted_iota(jnp.int32, v.shape, 1) < HEAD_DIM
    lo = jnp.sum(jnp.where(low, v, 0.0), axis=-1, keepdims=True)
    hi = jnp.sum(jnp.where(low, 0.0, v), axis=-1, keepdims=True)
    return jnp.where(low, lo, hi) * (1.0 / HEAD_DIM)


def _head_groups(c):
    width = _LANES if c % _LANES == 0 else HEAD_DIM
    assert c % width == 0, c
    return width, [slice(k * width, (k + 1) * width) for k in range(c // width)]


def _head_gain(g, width):
    return jnp.tile(g.reshape(1, HEAD_DIM), (1, width // HEAD_DIM))


def _rotate_half(y):
    half = HEAD_DIM // 2
    first = lax.broadcasted_iota(jnp.int32, y.shape, 1) % HEAD_DIM < half
    return jnp.where(first, -pltpu.roll(y, y.shape[1] - half, axis=1), pltpu.roll(y, half, axis=1))


def _rope_tables(rope, width):
    return [jnp.tile(t, (1, 2 * width // HEAD_DIM)) for t in rope]


def head_rms_fwd(name, x, g, rope=None):
    s, c = x.shape
    tm = _pick(s, (256, 128, 8))
    width, groups = _head_groups(c)

    def body(x_ref, g_ref, *refs):
        y_ref = refs[-1]
        for sl in groups:
            xs = x_ref[:, sl]
            y = (xs * lax.rsqrt(_head_mean(xs * xs) + EPS)) * g_ref[...]
            if rope:
                y = y * refs[0][...] + _rotate_half(y) * refs[1][...]
            y_ref[:, sl] = y

    row = pl.BlockSpec((tm, c), lambda i: (i, 0))
    tab = pl.BlockSpec((tm, width), lambda i: (i, 0))
    tables = _rope_tables(rope, width) if rope else []
    (y,), _ = _call(name, body, (s // tm,), [row, pl.BlockSpec((1, width), lambda i: (0, 0))] + [tab] * len(tables),
                    (x, _head_gain(g, width), *tables), [row], [jax.ShapeDtypeStruct((s, c), F32)], (), ("parallel",))
    return y


def head_rms_bwd(name, x, g, dy, rope=None):
    s, c = x.shape
    tm = _pick(s, (256, 128, 8))
    width, groups = _head_groups(c)

    def body(x_ref, g_ref, dy_ref, *refs):
        dx_ref, dg_ref = refs[-2:]

        @pl.when(pl.program_id(0) == 0)
        def _():
            dg_ref[...] = jnp.zeros_like(dg_ref)

        for sl in groups:
            xs, dys = x_ref[:, sl], dy_ref[:, sl]
            if rope:
                dys = dys * refs[0][...] - _rotate_half(dys * refs[1][...])
            rstd = lax.rsqrt(_head_mean(xs * xs) + EPS)
            xhat = xs * rstd
            gdy = dys * g_ref[...]
            dx_ref[:, sl] = rstd * (gdy - xhat * _head_mean(gdy * xhat))
            dg_ref[...] += jnp.sum(dys * xhat, axis=0, keepdims=True)

    row = pl.BlockSpec((tm, c), lambda i: (i, 0))
    vec = pl.BlockSpec((1, width), lambda i: (0, 0))
    tab = pl.BlockSpec((tm, width), lambda i: (i, 0))
    tables = _rope_tables(rope, width) if rope else []
    (dx, dg), _ = _call(name, body, (s // tm,), [row, vec, row] + [tab] * len(tables), (x, _head_gain(g, width), dy, *tables), [row, vec],
                        [jax.ShapeDtypeStruct((s, c), F32), jax.ShapeDtypeStruct((1, width), F32)], (), ("arbitrary",))
    return dx, jnp.sum(dg.reshape(width // HEAD_DIM, HEAD_DIM), axis=0)


@functools.partial(jax.custom_vjp, nondiff_argnums=(0,))
def head_rms(name, x, g):
    return head_rms_fwd(name + "_fwd", x, g)


def _head_rms_fwd(name, x, g):
    return head_rms_fwd(name + "_fwd", x, g), (x, g)


def _head_rms_bwd(name, res, dy):
    return head_rms_bwd(name + "_bwd", *res, dy)


head_rms.defvjp(_head_rms_fwd, _head_rms_bwd)


@functools.partial(jax.custom_vjp, nondiff_argnums=(0,))
def head_rms_rope(name, x, g, cos, sin):
    return head_rms_fwd(name + "_fwd", x, g, (cos, sin))


def _head_rms_rope_fwd(name, x, g, cos, sin):
    return head_rms_fwd(name + "_fwd", x, g, (cos, sin)), (x, g, cos, sin)


def _head_rms_rope_bwd(name, res, dy):
    x, g, cos, sin = res
    return (*head_rms_bwd(name + "_bwd", x, g, dy, (cos, sin)), jnp.zeros_like(cos), jnp.zeros_like(sin))


head_rms_rope.defvjp(_head_rms_rope_fwd, _head_rms_rope_bwd)


def _cumsum_call(name, a, reverse):
    h, s = a.shape
    tb = _LANES
    assert s % tb == 0

    def body(a_ref, o_ref):
        t_in = lax.broadcasted_iota(jnp.int32, (tb, tb), 0)
        t_out = lax.broadcasted_iota(jnp.int32, (tb, tb), 1)
        tri = jnp.where((t_in >= t_out) if reverse else (t_in <= t_out), 1.0, 0.0).astype(BF16)
        carry = jnp.zeros((h, 1), F32)
        blocks = range(s // tb)
        for b in (reversed(blocks) if reverse else blocks):
            cols = slice(b * tb, (b + 1) * tb)
            block = a_ref[:, cols]
            rest, local = block, jnp.zeros((h, tb), F32)
            for _ in range(3):
                piece = rest.astype(BF16)
                local = local + jnp.dot(piece, tri, preferred_element_type=F32)
                rest = rest - piece.astype(F32)
            o_ref[:, cols] = local + carry
            carry = carry + jnp.sum(block, axis=1, keepdims=True)

    whole = pl.BlockSpec((h, s), lambda j: (0, 0))
    (out,), _ = _call(name, body, (1,), [whole], (a,), [whole], [jax.ShapeDtypeStruct((h, s), F32)], (), ("arbitrary",))
    return out


@jax.custom_vjp
def time_cumsum(a):
    return _cumsum_call("gate_cumsum", a, False)


def _time_cumsum_fwd(a):
    return _cumsum_call("gate_cumsum", a, False), None


def _time_cumsum_bwd(_, dc):
    return (_cumsum_call("gate_cumsum_bwd", dc, True),)


time_cumsum.defvjp(_time_cumsum_fwd, _time_cumsum_bwd)


FFN_TM = 512


def _sigmoid(x):
    return 1.0 / (1.0 + jnp.exp(-x))


def ffn_gu(name, xn, wg, wu, rider=None):
    s, d = xn.shape
    ns, _, c = wg.shape
    tm = _pick(s, (FFN_TM, 128))

    def body(x_ref, wg_ref, wu_ref, h_ref, a_ref, b_ref):
        xb = x_ref[...]
        gv = jnp.dot(xb, wg_ref[...], preferred_element_type=F32)
        uv = jnp.dot(xb, wu_ref[...], preferred_element_type=F32)
        sig = _sigmoid(gv)
        silu = gv * sig
        h_ref[...] = (silu * uv).astype(BF16)
        a_ref[...] = (uv * (sig * (1.0 + gv * (1.0 - sig)))).astype(BF16)
        b_ref[...] = silu.astype(BF16)

    w_spec = pl.BlockSpec((None, d, c), lambda j, i: (j, 0, 0))
    o_spec = pl.BlockSpec((tm, c), lambda j, i: (i, j))
    return _call(
        name, body, (ns, s // tm), [pl.BlockSpec((tm, d), lambda j, i: (i, 0)), w_spec, w_spec], (xn, wg, wu),
        [o_spec, o_spec, o_spec], [jax.ShapeDtypeStruct((s, ns * c), BF16)] * 3, [], ("parallel", "parallel"), rider)


def ffn_dh(name, dy, wd, dh_dg, dh_du, ns, scale, rider=None):
    s, d = dy.shape
    f = wd.shape[0]
    c = f // ns
    tm = _pick(s, (FFN_TM, 128))

    def body(dy_ref, wd_ref, a_ref, b_ref, dg_ref, du_ref):
        dh = lax.dot_general(dy_ref[...].astype(BF16), wd_ref[...], _DIMS["nt"], preferred_element_type=F32) * scale
        dg_ref[...] = (dh * a_ref[...].astype(F32)).astype(BF16)
        du_ref[...] = (dh * b_ref[...].astype(F32)).astype(BF16)

    o_spec = pl.BlockSpec((tm, c), lambda j, i: (i, j))
    return _call(
        name, body, (ns, s // tm),
        [pl.BlockSpec((tm, d), lambda j, i: (i, 0)), pl.BlockSpec((c, d), lambda j, i: (j, 0)), o_spec, o_spec], (dy, wd, dh_dg, dh_du),
        [o_spec, o_spec], [jax.ShapeDtypeStruct((s, f), BF16), jax.ShapeDtypeStruct((s, f), BF16)],
        [], ("parallel", "parallel"), rider)


FOX_TQ = 512


def fox_tile(s_len):
    return min(FOX_TQ, s_len)


def _heads_per_block(h):
    return 2 if h % 2 == 0 else 1


def _fox_queries(q):
    return (q * (HEAD_DIM ** -0.5)).astype(BF16)


def _fox_scores(qs, kc, cq, ck, diagonal):
    s = lax.dot_general(qs, kc.astype(BF16), _DIMS["nt"], preferred_element_type=F32) + cq - ck
    if not diagonal:
        return s
    return jnp.where(lax.broadcasted_iota(jnp.int32, s.shape, 0) >= lax.broadcasted_iota(jnp.int32, s.shape, 1), s, MASK_VALUE)


def _fox_specs(h, s_len, tq):
    hb = _heads_per_block(h)
    qb = pl.BlockSpec((tq, hb * HEAD_DIM), lambda pp, i: (i, pp))
    kb = pl.BlockSpec((s_len, hb * HEAD_DIM), lambda pp, i: (0, pp))
    colb = pl.BlockSpec((hb, tq, 1), lambda pp, i: (pp, i, 0))
    rowb = pl.BlockSpec((hb, s_len // tq, 1, tq), lambda pp, i: (pp, 0, 0, 0))
    return hb, qb, kb, colb, rowb


def fox_fwd(q, k, v, cq, ck, rider=None):
    s_len, hd = q.shape
    h, d = hd // HEAD_DIM, HEAD_DIM
    tq = fox_tile(s_len)
    hb, qb, kb, colb, rowb = _fox_specs(h, s_len, tq)

    def body(q_ref, k_ref, v_ref, cq_ref, ck_ref, o_ref, lse_ref):
        i = pl.program_id(1)
        for hh in range(hb):
            lanes = slice(hh * d, (hh + 1) * d)
            qs, cqv = _fox_queries(q_ref[:, lanes]), cq_ref[hh]

            def chunk(c, carry, diagonal=False):
                m, l, acc = carry
                rows = pl.ds(pl.multiple_of(c * tq, tq), tq)
                s = _fox_scores(qs, k_ref[rows, lanes], cqv, ck_ref[hh, c], diagonal)
                m_new = jnp.maximum(m, jnp.max(s, axis=-1, keepdims=True))
                alpha = jnp.exp(m - m_new)
                p = jnp.exp(s - m_new)
                acc = alpha * acc + jnp.dot(p.astype(BF16), v_ref[rows, lanes].astype(BF16), preferred_element_type=F32)
                return m_new, alpha * l + jnp.sum(p, axis=-1, keepdims=True), acc

            init = (jnp.full((tq, 1), MASK_VALUE, F32), jnp.zeros((tq, 1), F32), jnp.zeros((tq, d), F32))
            m, l, acc = chunk(i, lax.fori_loop(0, i, chunk, init), diagonal=True)
            o_ref[:, lanes] = acc / l
            lse_ref[hh] = m + jnp.log(l)

    return _call(
        "fox_fwd", body, (h // hb, s_len // tq), [qb, kb, kb, colb, rowb], (q, k, v, cq, ck), [qb, colb],
        [jax.ShapeDtypeStruct((s_len, hd), F32), jax.ShapeDtypeStruct((h, s_len, 1), F32)], (), ("parallel", "parallel"), rider)


def fox_bwd(q, k, v, cq, ck, o, lse, do, rider=None):
    s_len, hd = q.shape
    h, d = hd // HEAD_DIM, HEAD_DIM
    tq = fox_tile(s_len)
    scale = HEAD_DIM ** -0.5
    hb, qb, kb, colb, rowb = _fox_specs(h, s_len, tq)

    def body(q_ref, k_ref, v_ref, cq_ref, ck_ref, o_ref, lse_ref, do_ref, dq_ref, dk_ref, dv_ref, dcq_ref, dck_ref):
        i = pl.program_id(1)

        @pl.when(i == 0)
        def _():
            dk_ref[...] = jnp.zeros_like(dk_ref)
            dv_ref[...] = jnp.zeros_like(dv_ref)
            dck_ref[...] = jnp.zeros_like(dck_ref)

        heads = []
        for hh in range(hb):
            lanes = slice(hh * d, (hh + 1) * d)
            dof = do_ref[:, lanes]
            heads.append((lanes, _fox_queries(q_ref[:, lanes]), cq_ref[hh], lse_ref[hh], dof.astype(BF16),
                          jnp.sum(dof * o_ref[:, lanes], axis=-1, keepdims=True)))

        def chunk(c, carry, diagonal=False):
            rows = pl.ds(pl.multiple_of(c * tq, tq), tq)
            out, dks, dvs = [], [], []
            for hh, (lanes, qs, cqv, lse_h, dob, delta) in enumerate(heads):
                dq, dcq = carry[hh]
                kc = k_ref[rows, lanes]
                p = jnp.exp(_fox_scores(qs, kc, cqv, ck_ref[hh, c], diagonal) - lse_h)
                dp = lax.dot_general(dob, v_ref[rows, lanes].astype(BF16), _DIMS["nt"], preferred_element_type=F32)
                ds = p * (dp - delta)
                dsb = ds.astype(BF16)
                dvs.append(lax.dot_general(p.astype(BF16), dob, _DIMS["tn"], preferred_element_type=F32))
                dks.append(lax.dot_general(dsb, qs, _DIMS["tn"], preferred_element_type=F32))
                dck_ref[hh, c] -= jnp.sum(ds, axis=0, keepdims=True)
                out.append((dq + jnp.dot(dsb, kc.astype(BF16), preferred_element_type=F32), dcq + jnp.sum(ds, axis=-1, keepdims=True)))
            dk_ref[rows, :] += jnp.concatenate(dks, axis=1)
            dv_ref[rows, :] += jnp.concatenate(dvs, axis=1)
            return tuple(out)

        init = tuple((jnp.zeros((tq, d), F32), jnp.zeros((tq, 1), F32)) for _ in range(hb))
        done = chunk(i, lax.fori_loop(0, i, chunk, init), diagonal=True)
        dq_ref[...] = jnp.concatenate([dq for dq, _ in done], axis=1) * scale
        for hh, (_, dcq) in enumerate(done):
            dcq_ref[hh] = dcq

    return _call(
        "fox_bwd", body, (h // hb, s_len // tq), [qb, kb, kb, colb, rowb, qb, colb, qb], (q, k, v, cq, ck, o, lse, do),
        [qb, kb, kb, colb, rowb],
        [jax.ShapeDtypeStruct((s_len, hd), F32)] * 3
        + [jax.ShapeDtypeStruct((h, s_len, 1), F32), jax.ShapeDtypeStruct((h, s_len // tq, 1, tq), F32)],
        (), ("parallel", "arbitrary"), rider)


def _stack_heads(ref, first, g):
    return jnp.concatenate([ref[:, (first + j) * HEAD_DIM:(first + j + 1) * HEAD_DIM] for j in range(g)], axis=0)


def _window(prev_ref, cur_ref, hh):
    lanes = slice(hh * HEAD_DIM, (hh + 1) * HEAD_DIM)
    return jnp.concatenate([prev_ref[:, lanes], cur_ref[:, lanes]], axis=0).astype(BF16)


def _swa_band(g, w):
    t = lax.broadcasted_iota(jnp.int32, (g * w, 2 * w), 0) % w
    col = lax.broadcasted_iota(jnp.int32, (g * w, 2 * w), 1)
    rel = t + w - col
    band = (rel >= 0) & (rel < w)
    return jnp.where(jnp.stack([band & (col >= w), band]), 0.0, MASK_VALUE).astype(F32)


def _swa_probs(qs, kw, sink, band):
    s = lax.dot_general(qs, kw, _DIMS["nt"], preferred_element_type=F32) + band
    m = jnp.maximum(jnp.max(s, axis=-1, keepdims=True), sink)
    p = jnp.exp(s - m)
    ps = jnp.exp(sink - m)
    linv = 1.0 / (jnp.sum(p, axis=-1, keepdims=True) + ps)
    return p * linv, ps * linv


def _swa_specs(hk, g, s_len):
    w = WINDOW
    assert s_len % w == 0
    hb = _heads_per_block(hk)
    qb = pl.BlockSpec((w, hb * g * HEAD_DIM), lambda pp, n: (n, pp))
    prev = pl.BlockSpec((w, hb * HEAD_DIM), lambda pp, n: (jnp.maximum(n - 1, 0), pp))
    cur = pl.BlockSpec((w, hb * HEAD_DIM), lambda pp, n: (n, pp))
    sb = pl.BlockSpec((hb, g * w, 1), lambda pp, n: (pp, 0, 0))
    band = pl.BlockSpec((None, g * w, 2 * w), lambda pp, n: (jnp.minimum(n, 1), 0, 0))
    return hb, qb, prev, cur, sb, band


def swa_fwd(q, k, v, sink, rider=None):
    s_len = q.shape[0]
    hk = k.shape[1] // HEAD_DIM
    g = q.shape[1] // k.shape[1]
    w, d = WINDOW, HEAD_DIM
    hb, qb, prev, cur, sb, bandb = _swa_specs(hk, g, s_len)

    def body(q_ref, kp_ref, kc_ref, vp_ref, vc_ref, sink_ref, band_ref, o_ref):
        for hh in range(hb):
            qs = (_stack_heads(q_ref, hh * g, g) * (HEAD_DIM ** -0.5)).astype(BF16)
            p, _ = _swa_probs(qs, _window(kp_ref, kc_ref, hh), sink_ref[hh], band_ref[...])
            o = jnp.dot(p.astype(BF16), _window(vp_ref, vc_ref, hh), preferred_element_type=F32)
            for j in range(g):
                o_ref[:, (hh * g + j) * d:(hh * g + j + 1) * d] = o[j * w:(j + 1) * w]

    (o,), rid = _call("swa_fwd", body, (hk // hb, s_len // w), [qb, prev, cur, prev, cur, sb, bandb],
                      (q, k, k, v, v, sink, _swa_band(g, w)), [qb], [jax.ShapeDtypeStruct(q.shape, F32)], (),
                      ("parallel", "parallel"), rider)
    return o, rid


def swa_bwd(q, k, v, sink, o, do, rider=None):
    s_len = q.shape[0]
    hk = k.shape[1] // HEAD_DIM
    g = q.shape[1] // k.shape[1]
    w, d = WINDOW, HEAD_DIM
    scale = HEAD_DIM ** -0.5
    hb, qb, prev, cur, sb, bandb = _swa_specs(hk, g, s_len)

    def body(q_ref, kp_ref, kc_ref, vp_ref, vc_ref, sink_ref, band_ref, o_ref, do_ref, dq_ref, dkp_ref, dkc_ref, dvp_ref, dvc_ref,
             dsink_ref):
        @pl.when(pl.program_id(1) == 0)
        def _():
            dsink_ref[...] = jnp.zeros_like(dsink_ref)

        for hh in range(hb):
            lanes = slice(hh * d, (hh + 1) * d)
            qs = (_stack_heads(q_ref, hh * g, g) * scale).astype(BF16)
            kw, vw = _window(kp_ref, kc_ref, hh), _window(vp_ref, vc_ref, hh)
            p, ps = _swa_probs(qs, kw, sink_ref[hh], band_ref[...])
            dof = _stack_heads(do_ref, hh * g, g)
            dob = dof.astype(BF16)
            delta = jnp.sum(dof * _stack_heads(o_ref, hh * g, g), axis=-1, keepdims=True)
            dp = lax.dot_general(dob, vw, _DIMS["nt"], preferred_element_type=F32)
            ds = p * (dp - delta)
            dsb = ds.astype(BF16)
            dsink_ref[hh] -= ps * delta
            dq = jnp.dot(dsb, kw, preferred_element_type=F32) * scale
            for j in range(g):
                dq_ref[:, (hh * g + j) * d:(hh * g + j + 1) * d] = dq[j * w:(j + 1) * w]
            dkw = lax.dot_general(dsb, qs, _DIMS["tn"], preferred_element_type=F32)
            dvw = lax.dot_general(p.astype(BF16), dob, _DIMS["tn"], preferred_element_type=F32)
            dkp_ref[:, lanes] = dkw[:w]
            dkc_ref[:, lanes] = dkw[w:]
            dvp_ref[:, lanes] = dvw[:w]
            dvc_ref[:, lanes] = dvw[w:]

    kv_shape = jax.ShapeDtypeStruct(k.shape, F32)
    (dq, dkp, dkc, dvp, dvc, dsink), rid = _call(
        "swa_bwd", body, (hk // hb, s_len // w), [qb, prev, cur, prev, cur, sb, bandb, qb, qb],
        (q, k, k, v, v, sink, _swa_band(g, w), o, do),
        [qb, cur, cur, cur, cur, sb],
        [jax.ShapeDtypeStruct(q.shape, F32), kv_shape, kv_shape, kv_shape, kv_shape, jax.ShapeDtypeStruct((hk, g * w, 1), F32)],
        (), ("parallel", "arbitrary"), rider)

    def shift_up(a):
        return jnp.concatenate([a[w:], jnp.zeros_like(a[:w])], axis=0)

    return (dq, dkc + shift_up(dkp), dvc + shift_up(dvp), dsink), rid


def loss_call(y, target):
    s, d = y.shape
    tm = _pick(s, (512, 256, 128))

    def body(y_ref, t_ref, l_ref, dy_ref, dyb_ref):
        e = y_ref[...] - t_ref[...]
        dy = e * (1.0 / d)
        dy_ref[...] = dy
        dyb_ref[...] = dy.astype(BF16)

        @pl.when(pl.program_id(0) == 0)
        def _():
            l_ref[...] = jnp.zeros_like(l_ref)

        l_ref[...] += jnp.sum(jnp.sum(e * e, axis=0, keepdims=True), axis=1, keepdims=True) * (0.5 / d)

    row = pl.BlockSpec((tm, d), lambda i: (i, 0))
    (l, dy, dyb), _ = _call("loss_head", body, (s // tm,), [row, row], (y, target), [pl.BlockSpec((1, 1), lambda i: (0, 0)), row, row],
                            [jax.ShapeDtypeStruct((1, 1), F32), jax.ShapeDtypeStruct((s, d), F32), jax.ShapeDtypeStruct((s, d), BF16)],
                            (), ("arbitrary",))
    return l[0, 0], dy, dyb


def _row_tile(rows, cols, itemsize, block_bytes=1 << 20):
    target = max(16, block_bytes // (cols * itemsize))
    fits = [t for t in range(16, rows + 1, 16) if rows % t == 0 and t <= target]
    return fits[-1] if fits else rows


CAST_STEPS = 8


def cast_place(name, ws, p_idx, rider=None):
    n = len(ws)
    assert all(w.shape[0] % (16 * CAST_STEPS) == 0 for w in ws), [w.shape for w in ws]

    def body(p_ref, *refs):
        for w_ref, o_ref in zip(refs[:n], refs[n:]):
            o_ref[...] = w_ref[...].astype(BF16)

    return _call(
        name, body, (CAST_STEPS,), [pl.BlockSpec((w.shape[0] // CAST_STEPS, w.shape[1]), lambda i, pr: (i, 0)) for w in ws], tuple(ws),
        [pl.BlockSpec((None, w.shape[0] // CAST_STEPS, w.shape[1]), lambda i, pr: (pr[0], i, 0)) for w in ws],
        [jax.ShapeDtypeStruct((N_CHIPS,) + w.shape, BF16) for w in ws], (), ("parallel",), rider, prefetch=(p_idx,))


def chip_sum(name, grad, theirs, c_idx):
    ns, r, cols = grad.shape
    rh = r // 2
    tr = _row_tile(rh, cols, 2, 2 << 20)
    nb = rh // tr

    def body(c_ref, a_ref, b_ref, o_ref):
        o_ref[...] = (a_ref[...].astype(F32) + b_ref[...].astype(F32)).astype(o_ref.dtype)

    return pl.pallas_call(
        body, name=name,
        grid_spec=pltpu.PrefetchScalarGridSpec(
            num_scalar_prefetch=1, grid=(ns, nb),
            in_specs=[pl.BlockSpec((None, tr, cols), lambda q, i, cr: (q, cr[0] * nb + i, 0)),
                      pl.BlockSpec((None, tr, cols), lambda q, i, cr: (q, i, 0))],
            out_specs=pl.BlockSpec((None, tr, cols), lambda q, i, cr: (q, i, 0))),
        out_shape=jax.ShapeDtypeStruct((ns, rh, cols), BF16),
        compiler_params=pltpu.CompilerParams(dimension_semantics=("parallel", "parallel"), vmem_limit_bytes=VMEM_LIMIT),
    )(c_idx, grad, theirs)


def owner_sum(name, sums, got, pc_idx):
    ns, rh, cols = sums.shape
    tr = _row_tile(rh, cols, 4, 2 << 20)
    nb = rh // tr

    def body(pc_ref, a_ref, b_ref, o_ref):
        o_ref[...] = ((a_ref[...].astype(F32) + b_ref[0].astype(F32)) + b_ref[1].astype(F32)) + b_ref[2].astype(F32)

    return pl.pallas_call(
        body, name=name,
        grid_spec=pltpu.PrefetchScalarGridSpec(
            num_scalar_prefetch=1, grid=(nb,),
            in_specs=[pl.BlockSpec((None, tr, cols), lambda i, pc: (pc[0], i, 0)),
                      pl.BlockSpec((3, tr, cols), lambda i, pc: (0, i, 0))],
            out_specs=pl.BlockSpec((tr, cols), lambda i, pc: (pc[1] * nb + i, 0))),
        out_shape=jax.ShapeDtypeStruct((2 * rh, cols), F32),
        compiler_params=pltpu.CompilerParams(dimension_semantics=("parallel",), vmem_limit_bytes=VMEM_LIMIT),
    )(pc_idx, sums, got)


def adamw(name, w, g, m, v):
    r, cols = w.shape
    tr = _row_tile(r, cols, 4)
    c1 = 1.0 / (1.0 - ADAM_B1 ** ADAM_STEP)
    c2 = 1.0 / (1.0 - ADAM_B2 ** ADAM_STEP)

    def body(w_ref, g_ref, m_ref, v_ref, go_ref, d_ref, nm_ref, nv_ref):
        gv = g_ref[...]
        nm = ADAM_B1 * m_ref[...] + (1.0 - ADAM_B1) * gv
        nv = ADAM_B2 * v_ref[...] + (1.0 - ADAM_B2) * (gv * gv)
        go_ref[...] = gv
        d_ref[...] = -ADAM_LR * ((nm * c1) / (jnp.sqrt(nv * c2) + ADAM_EPS) + ADAM_WD * w_ref[...])
        nm_ref[...] = nm
        nv_ref[...] = nv

    blk = pl.BlockSpec((tr, cols), lambda i: (i, 0))
    return _call(name, body, (r // tr,), [blk] * 4, (w, g, m, v), [blk] * 4, [jax.ShapeDtypeStruct((r, cols), F32)] * 4, (), ("parallel",))


def _win_layout(d_model):
    hf = hq = d_model // (2 * HEAD_DIM)
    hk = hq // 4
    sizes = [hf * HEAD_DIM, hf * HEAD_DIM, hf * HEAD_DIM, hf, hq * HEAD_DIM, hk * HEAD_DIM, hk * HEAD_DIM]
    return hf, hq, hk, sizes


class WinPlan:
    def __init__(self, d_model, ns=N_CHIPS):
        self.hf, self.hq, self.hk, self.sizes = _win_layout(d_model)
        self.ns, self.cs = ns, sum(self.sizes) // ns
        self.jump_at = sum(self.sizes[:4])
        self.jump_by = -self.jump_at % _LANES
        self.base = [self.pos(s * self.cs) // _LANES * _LANES for s in range(ns)]
        ends = [self.pos((s + 1) * self.cs - 1) + 1 - self.base[s] for s in range(ns)]
        self.width = -(-max(ends) // _LANES) * _LANES
        self.total = -(-max(b + self.width for b in self.base) // 1024) * 1024
        starts = [0]
        for sz in self.sizes:
            starts.append(starts[-1] + sz)
        self.segments = [(self.pos(a), sz) for a, sz in zip(starts, self.sizes)]

    def pos(self, g):
        return g if g < self.jump_at else g + self.jump_by

    def pieces(self, s):
        g0, g1 = s * self.cs, (s + 1) * self.cs
        cuts = [g0] + ([self.jump_at] if g0 < self.jump_at < g1 else []) + [g1]
        return [(a - g0, b - a, self.pos(a) - self.base[s]) for a, b in zip(cuts[:-1], cuts[1:])]

    def place(self, w, s):
        parts, at = [], 0
        for t0, n, j0 in self.pieces(s):
            parts += [jnp.zeros((w.shape[0], j0 - at), w.dtype), w[:, t0:t0 + n]]
            at = j0 + n
        return jnp.concatenate(parts + [jnp.zeros((w.shape[0], self.width - at), w.dtype)], axis=1)

    def unplace(self, slab, s):
        return jnp.concatenate([slab[:, j0:j0 + n] for _, n, j0 in self.pieces(s)], axis=1)

    def assemble(self, slabs):
        return sum(jnp.pad(slabs[s], ((0, 0), (b, self.total - b - self.width))) for s, b in enumerate(self.base))

    def split(self, full):
        return jnp.stack([full[:, b:b + self.width] for b in self.base])


def _attn_inputs(proj, sm, positions):
    s_len = proj.shape[0]
    plan = WinPlan(sm["norm_mix_g"].shape[0])
    hf, hq, hk = plan.hf, plan.hq, plan.hk
    grp = hq // hk
    q_f, k_f, v_f, f_logit, q_s, k_s, v_s = [proj[:, a:a + n] for a, n in plan.segments]

    q_f = head_rms("fox_qnorm", q_f, sm["fox_q_norm_g"])
    k_f = head_rms("fox_knorm", k_f, sm["fox_k_norm_g"])
    log_f = jax.nn.log_sigmoid(f_logit + sm["b_forget"])
    c = time_cumsum(log_f.T)

    inv_freq = ROPE_THETA ** (-jnp.arange(0, HEAD_DIM, 2, dtype=F32) / HEAD_DIM)
    ang = positions.astype(F32)[:, None] * inv_freq
    cos, sin = jnp.cos(ang), jnp.sin(ang)
    q_s = head_rms_rope("swa_qnorm", q_s, sm["swa_q_norm_g"], cos, sin)
    k_s = head_rms_rope("swa_knorm", k_s, sm["swa_k_norm_g"], cos, sin)
    sink = jnp.broadcast_to(sm["swa_sinks"].reshape(hk, grp, 1, 1), (hk, grp, WINDOW, 1)).reshape(hk, grp * WINDOW, 1)
    tq = fox_tile(s_len)
    return (q_f, k_f, v_f, c[:, :, None], c.reshape(hf, s_len // tq, 1, tq)), (q_s, k_s, v_s, sink)


_BIG = ("ffn1_w_gate", "ffn1_w_up", "ffn1_w_down", "w_in", "w_out", "ffn2_w_gate", "ffn2_w_up", "ffn2_w_down")
_SMALL = ("norm_ffn1_g", "norm_mix_g", "b_forget", "fox_q_norm_g", "fox_k_norm_g", "swa_q_norm_g", "swa_k_norm_g", "swa_sinks",
          "out_norm_fox_g", "out_norm_swa_g", "norm_ffn2_g")
_ATTN_SMALL = ("norm_mix_g", "b_forget", "fox_q_norm_g", "fox_k_norm_g", "swa_q_norm_g", "swa_k_norm_g", "swa_sinks")
_ALL = ("norm_ffn1_g", "ffn1_w_gate", "ffn1_w_up", "ffn1_w_down", "norm_mix_g", "w_in", "b_forget", "fox_q_norm_g", "fox_k_norm_g",
        "swa_q_norm_g", "swa_k_norm_g", "swa_sinks", "out_norm_fox_g", "out_norm_swa_g", "w_out", "norm_ffn2_g", "ffn2_w_gate",
        "ffn2_w_up", "ffn2_w_down")


def _pack_small(d):
    parts = []
    for k in _SMALL:
        v = d[k].reshape(-1)
        rows = -(-v.shape[0] // _LANES)
        parts.append(jnp.pad(v, (0, rows * _LANES - v.shape[0])).reshape(rows, _LANES))
    a = jnp.concatenate(parts, axis=0)
    return jnp.pad(a, ((0, -a.shape[0] % 8), (0, 0)))


def _unpack_small(a, like):
    out, r0 = {}, 0
    for k in _SMALL:
        nvals = like[k].shape[1]
        rows = -(-nvals // _LANES)
        out[k] = a[r0:r0 + rows].reshape(-1)[:nvals].reshape(1, nvals)
        r0 += rows
    return out


def _stacked(w):
    return w.reshape(-1, w.shape[-1])


def _local_step(shards, sm, x, positions, target, p_idx, c_idx, pc_idx):
    ns = N_CHIPS
    full = {}

    def fetch(*jobs):
        names = list(dict.fromkeys(n for n, _, _ in jobs))
        return names, gather([bufs[n] for n in names], [(names.index(n), kind, part) for n, kind, part in jobs])

    def take(names, rid):
        for n, b in zip(names, rid[0]):
            bufs[n] = b

    n1 = ["ffn1_w_gate", "ffn1_w_up", "ffn1_w_down"]
    n2 = ["ffn2_w_gate", "ffn2_w_up", "ffn2_w_down"]
    later = ["w_in", "w_out"] + n2
    placed, _ = cast_place("cast_place_ffn1", [shards[n] for n in n1], p_idx)
    bufs = dict(zip(n1, placed))
    gate1, up1, down1 = n1
    gate2, up2, down2 = n2
    names, rider = fetch((gate1, "ici", WHOLE), (up1, "ici", WHOLE))
    placed, rid = cast_place("cast_place_later", [shards[n] for n in later], p_idx, rider=rider)
    bufs.update(zip(later, placed))
    take(names, rid)
    names, rider = fetch((gate1, "d2d", WHOLE), (up1, "d2d", WHOLE))
    xn1, r1, rid = rms_fwd("ffn1_norm", x, sm["norm_ffn1_g"], BF16, rider=rider)
    take(names, rid)
    names, rider = fetch((down1, "ici", WHOLE))
    (hid1, hdg1, hdu1), rid = ffn_gu("ffn1_gu", xn1, bufs[gate1], bufs[up1], rider=rider)
    take(names, rid)
    names, rider = fetch((down1, "d2d", WHOLE))
    take(names, run_step("gather_d2d_ffn1_down", rider))
    wd1 = _stacked(bufs[down1])
    names, rider = fetch(("w_in", "ici", WHOLE))
    h1, rid = mm_nn("ffn1_down", hid1, wd1, scale=0.5, resid=x, rider=rider)
    take(names, rid)

    names, rider = fetch(("w_in", "d2d", WHOLE))
    u, r_mix, rid = rms_fwd("mix_norm", h1, sm["norm_mix_g"], BF16, rider=rider)
    take(names, rid)
    names, rider = fetch(("w_out", "ici", WHOLE))
    plan = WinPlan(x.shape[1])
    win = plan.assemble(bufs["w_in"])
    proj, rid = mm_nn("mix_inproj", u, win, rider=rider)
    take(names, rid)
    sm_attn = {k: sm[k] for k in _ATTN_SMALL}
    (fox_in, swa_in), attn_vjp = jax.vjp(lambda pr, s: _attn_inputs(pr, s, positions), proj, sm_attn)
    names, rider = fetch((gate2, "ici", WHOLE), (up2, "ici", (0, 1, 4)), ("w_out", "d2d", WHOLE))
    (o_f, lse), rid = fox_fwd(*fox_in, rider=rider)
    take(names, rid)
    names, rider = fetch((up2, "ici", (1, 4, 4)), (gate2, "d2d", WHOLE), (up2, "d2d", (0, 1, 4)))
    o_s, rid = swa_fwd(*swa_in, rider=rider)
    take(names, rid)
    o_fox, o_swa = o_f, o_s
    nf, r_fox = rms_fwd("out_norm_fox", o_fox, sm["out_norm_fox_g"], BF16)
    nsw, r_swa = rms_fwd("out_norm_swa", o_swa, sm["out_norm_swa_g"], BF16)
    o = jnp.concatenate([nf, nsw], axis=-1)
    wout = _stacked(bufs["w_out"])
    names, rider = fetch((down2, "ici", (0, 1, 4)), (up2, "d2d", (1, 4, 4)))
    h2, rid = mm_nn("out_proj", o, wout, resid=h1, rider=rider)
    take(names, rid)

    xn2, r2 = rms_fwd("ffn2_norm", h2, sm["norm_ffn2_g"], BF16)
    names, rider = fetch((down2, "ici", (1, 4, 4)))
    (hid2, hdg2, hdu2), rid = ffn_gu("ffn2_gu", xn2, bufs[gate2], bufs[up2], rider=rider)
    take(names, rid)
    names, rider = fetch((down2, "d2d", WHOLE))
    take(names, run_step("gather_d2d_ffn2_down", rider))
    wd2 = _stacked(bufs["ffn2_w_down"])
    y, _ = mm_nn("ffn2_down", hid2, wd2, scale=0.5, resid=h2)
    loss, dy, dy_b = loss_call(y, target)

    red = {}

    def grad(n, g):
        red[n] = {"grad": g.reshape(ns, -1, g.shape[-1])}

    def ride(*steps):
        def done(rid):
            a0 = n0 = 0
            for rd, cb in steps:
                cb(rid[0][a0:a0 + len(rd.aliased)], rid[1][n0:n0 + len(rd.news)])
                a0, n0 = a0 + len(rd.aliased), n0 + len(rd.news)

        return (combine(*[s[0] for s in steps]) if len(steps) > 1 else steps[0][0]), done

    def xchg(*names):
        def cb(al, news):
            for n, t in zip(names, news):
                red[n]["sum"] = chip_sum("chip_sum_" + n, red[n]["grad"], t, c_idx)

        return exchange_halves([red[n]["grad"] for n in names]), cb

    def scat(n, part=WHOLE):
        def cb(al, news):
            red[n]["got"] = (al or news)[0]

        return scatter_to_owner([red[n]["sum"]], [red[n]["got"]] if "got" in red[n] else None, part), cb

    def own(n):
        red[n]["half"] = owner_sum("owner_sum_" + n, red[n]["sum"], red[n]["got"], pc_idx)

    def join(*names):
        return join_halves([red[n]["half"] for n in names]), lambda al, news: full.update(zip(names, al))

    dwd2, _ = mm_tn("ffn2_dwd", hid2, dy_b, out_dtype=BF16, scale=0.5)
    grad(down2, dwd2)
    rider, done = ride(xchg(down2))
    (dg2, du2), rid = ffn_dh("ffn2_dh", dy_b, wd2, hdg2, hdu2, ns, 0.5, rider=rider)
    done(rid)
    rider, done = ride(scat(down2, (0, 1, 2)))
    dwg2, rid = mm_tn_sharded("ffn2_dwg", xn2, dg2, ns, rider=rider)
    done(rid)
    grad(gate2, dwg2)
    rider, done = ride(scat(down2, (1, 2, 2)), xchg(gate2))
    dwu2, rid = mm_tn_sharded("ffn2_dwu", xn2, du2, ns, rider=rider)
    done(rid)
    grad(up2, dwu2)
    rider, done = ride(scat(gate2, (0, 1, 2)), xchg(up2))
    dxn, rid = mm_nt_sharded("ffn2_dxn_g", dg2, bufs[gate2], rider=rider)
    done(rid)
    rider, done = ride(scat(gate2, (1, 2, 2)))
    dxn, rid = mm_nt_sharded("ffn2_dxn_u", du2, bufs[up2], resid=dxn, rider=rider)
    done(rid)
    dh2, dgain_ffn2, dh2_b = rms_bwd("ffn2_dnorm", h2, sm["norm_ffn2_g"], r2, dxn, dres=dy, also_bf16=True)
    own(down2)
    own(gate2)

    do, _ = mm_nt("out_do", dh2_b, wout)
    dwout, _ = mm_tn("out_dw", o, dh2_b, out_dtype=BF16)
    cf = o_fox.shape[1]
    d_fox, dgain_fox = rms_bwd("out_dnorm_fox", o_fox, sm["out_norm_fox_g"], r_fox, do[:, :cf])
    d_swa, dgain_swa = rms_bwd("out_dnorm_swa", o_swa, sm["out_norm_swa_g"], r_swa, do[:, cf:])
    grad("w_out", dwout)
    rider, done = ride(scat(up2))
    swa_cts, rid = swa_bwd(*swa_in, o_s, d_swa, rider=rider)
    done(rid)
    own(up2)
    rider, done = ride(xchg("w_out"), join(down2, gate2, up2))
    fox_cts, rid = fox_bwd(*fox_in, o_f, lse, d_fox, rider=rider)
    done(rid)
    dproj, dsm_attn = attn_vjp((tuple(fox_cts), tuple(swa_cts)))
    dproj = dproj.astype(BF16)

    rider, done = ride(scat("w_out"))
    du, rid = mm_nt("mix_du", dproj, win, rider=rider)
    done(rid)
    dwin, _ = mm_tn("mix_dwin", u, dproj, out_dtype=BF16)
    grad("w_in", plan.split(dwin))
    rider, done = ride(xchg("w_in"))
    dh1, dgain_mix, dh1_b, rid = rms_bwd("mix_dnorm", h1, sm["norm_mix_g"], r_mix, du, dres=dh2, rider=rider, also_bf16=True)
    done(rid)
    own("w_out")

    rider, done = ride(scat("w_in", (0, 1, 2)))
    dwd1, rid = mm_tn("ffn1_dwd", hid1, dh1_b, out_dtype=BF16, scale=0.5, rider=rider)
    done(rid)
    grad(down1, dwd1)
    rider, done = ride(scat("w_in", (1, 2, 2)), xchg(down1))
    (dg1, du1), rid = ffn_dh("ffn1_dh", dh1_b, wd1, hdg1, hdu1, ns, 0.5, rider=rider)
    done(rid)
    own("w_in")
    rider, done = ride(scat(down1, (0, 1, 2)), join("w_out"))
    dwg1, rid = mm_tn_sharded("ffn1_dwg", xn1, dg1, ns, rider=rider)
    done(rid)
    grad(gate1, dwg1)
    rider, done = ride(scat(down1, (1, 2, 2)), xchg(gate1), join("w_in"))
    dwu1, rid = mm_tn_sharded("ffn1_dwu", xn1, du1, ns, rider=rider)
    done(rid)
    grad(up1, dwu1)
    own(down1)
    rider, done = ride(scat(gate1, (0, 1, 2)), xchg(up1), join(down1))
    dxn, rid = mm_nt_sharded("ffn1_dxn_g", dg1, bufs[gate1], rider=rider)
    done(rid)
    rider, done = ride(scat(gate1, (1, 2, 2)), scat(up1, (0, 1, 4)))
    dxn, rid = mm_nt_sharded("ffn1_dxn_u", du1, bufs[up1], resid=dxn, rider=rider)
    done(rid)
    dx, dgain_ffn1 = rms_bwd("ffn1_dnorm", x, sm["norm_ffn1_g"], r1, dxn, dres=dh1)
    own(gate1)

    rider, done = ride(scat(up1, (1, 4, 4)), join(gate1))
    done(run_step("reduce_tail", rider))
    own(up1)
    rider, done = ride(join(up1))
    done(run_step("join_tail", rider))

    g_small = dict(dsm_attn)
    g_small["norm_mix_g"] = g_small["norm_mix_g"] + dgain_mix
    g_small.update(norm_ffn1_g=dgain_ffn1, norm_ffn2_g=dgain_ffn2, out_norm_fox_g=dgain_fox, out_norm_swa_g=dgain_swa)
    return loss, dx, full, g_small


def kernel(x, positions, norm_ffn1_g, ffn1_w_gate, ffn1_w_up, ffn1_w_down, norm_mix_g, w_in, b_forget, fox_q_norm_g, fox_k_norm_g, swa_q_norm_g, swa_k_norm_g, swa_sinks, out_norm_fox_g, out_norm_swa_g, w_out, norm_ffn2_g, ffn2_w_gate, ffn2_w_up, ffn2_w_down, loss_target, m_norm_ffn1_g, m_ffn1_w_gate, m_ffn1_w_up, m_ffn1_w_down, m_norm_mix_g, m_w_in, m_b_forget, m_fox_q_norm_g, m_fox_k_norm_g, m_swa_q_norm_g, m_swa_k_norm_g, m_swa_sinks, m_out_norm_fox_g, m_out_norm_swa_g, m_w_out, m_norm_ffn2_g, m_ffn2_w_gate, m_ffn2_w_up, m_ffn2_w_down, v_norm_ffn1_g, v_ffn1_w_gate, v_ffn1_w_up, v_ffn1_w_down, v_norm_mix_g, v_w_in, v_b_forget, v_fox_q_norm_g, v_fox_k_norm_g, v_swa_q_norm_g, v_swa_k_norm_g, v_swa_sinks, v_out_norm_fox_g, v_out_norm_swa_g, v_w_out, v_norm_ffn2_g, v_ffn2_w_gate, v_ffn2_w_up, v_ffn2_w_down):
    args = dict(locals())
    w = {k: args[k] for k in _ALL}
    m = {k: args["m_" + k] for k in _ALL}
    v = {k: args["v_" + k] for k in _ALL}
    c_idx = lax.axis_index("c").astype(jnp.int32).reshape(1)
    p_idx = (2 * lax.axis_index("x") + lax.axis_index("y")).astype(jnp.int32).reshape(1)
    pc_idx = jnp.concatenate([p_idx, c_idx])

    small = {k: w[k] for k in _SMALL}
    shards = {k: w[k][0] for k in _BIG}
    plan = WinPlan(x.shape[-1])
    shards["w_in"] = lax.switch(p_idx[0], [functools.partial(plan.place, s=s) for s in range(N_CHIPS)], shards["w_in"])
    loss, grad_x, g_shard, g_small = _local_step(shards, {k: w[k][0] for k in _SMALL}, x[0], positions[0], loss_target[0],
                                                 p_idx, c_idx, pc_idx)
    g_shard["w_in"] = lax.switch(p_idx[0], [functools.partial(plan.unplace, s=s) for s in range(N_CHIPS)], g_shard["w_in"])
    loss = lax.psum(loss, ("x", "y", "c"))
    g_small_sum = _unpack_small(all_reduce_small(_pack_small({k: g_small[k].reshape(1, -1) for k in _SMALL})), small)

    grad_w, delta, new_m, new_v = {}, {}, {}, {}
    for k in _BIG:
        (g, d, nm, nv), _ = adamw("adamw_" + k, w[k][0], g_shard[k], m[k][0], v[k][0])
        grad_w[k], delta[k], new_m[k], new_v[k] = g[None], d[None], nm[None], nv[None]
    (_, d, nm, nv), _ = adamw("adamw_small", _pack_small(small), _pack_small(g_small_sum), _pack_small({k: m[k] for k in _SMALL}),
                              _pack_small({k: v[k] for k in _SMALL}))
    grad_w.update(g_small_sum)
    delta.update(_unpack_small(d, small))
    new_m.update(_unpack_small(nm, small))
    new_v.update(_unpack_small(nv, small))

    return (loss, grad_x[None], *[grad_w[k] for k in _ALL], *[delta[k] for k in _ALL], *[new_m[k] for k in _ALL], *[new_v[k] for k in _ALL])
```

```python
import functools

import jax
import jax.numpy as jnp
from jax import lax
from jax.experimental import pallas as pl
from jax.experimental.pallas import tpu as pltpu

F32 = jnp.float32
BF16 = jnp.bfloat16

HEAD_DIM = 64
WINDOW = 128
ROPE_THETA = 10000.0
EPS = 1e-6
N_CHIPS = 4
N_DEV = 8

ADAM_LR = 0.001
ADAM_B1 = 0.9
ADAM_B2 = 0.999
ADAM_EPS = 1e-08
ADAM_WD = 0.01
ADAM_STEP = 10

V7X_VMEM_BYTES = 64 * 1024 * 1024
VMEM_LIMIT = V7X_VMEM_BYTES - 8 * 1024 * 1024
MASK_VALUE = -1e30

_MESH = pl.DeviceIdType.MESH
_HBM = pl.BlockSpec(memory_space=pl.ANY)
_DIMS = {"nn": (((1,), (0,)), ((), ())), "nt": (((1,), (1,)), ((), ())), "tn": (((0,), (0,)), ((), ()))}


def _pick(n, prefs):
    for p in prefs:
        if n % p == 0:
            return p
    return n


class Rider:
    def __init__(self, reads, aliased, news, nsem, build):
        self.reads, self.aliased, self.news, self.nsem, self.build = list(reads), list(aliased), list(news), nsem, build


class _Shifted:
    def __init__(self, ref, off):
        self.ref, self.off = ref, off

    @property
    def at(self):
        return self

    def __getitem__(self, k):
        return self.ref.at[k + self.off]


def combine(*riders):
    def build(reads, al, news, ssem, rsem):
        out = ([], [], [])
        r0 = a0 = n0 = s0 = 0
        for rd in riders:
            nr, na, nn = len(rd.reads), len(rd.aliased), len(rd.news)
            part = rd.build(reads[r0:r0 + nr], al[a0:a0 + na], news[n0:n0 + nn], _Shifted(ssem, s0), _Shifted(rsem, s0))
            for acc, lst in zip(out, part):
                acc.extend(lst)
            r0, a0, n0, s0 = r0 + nr, a0 + na, n0 + nn, s0 + rd.nsem
        return out

    return Rider(sum((r.reads for r in riders), []), sum((r.aliased for r in riders), []), sum((r.news for r in riders), []),
                 sum(r.nsem for r in riders), build)


def _me():
    return lax.axis_index("x"), lax.axis_index("y"), lax.axis_index("c")


def _other_chips(x, y):
    return [(1 - x, y), (x, 1 - y), (1 - x, 1 - y)]


WHOLE = (0, 1, 1)


def _rows(ref, start, rows, part=WHOLE):
    k0, k1, n = part
    assert rows % n == 0, (rows, part)
    idx = (slice(None),) * (len(ref.shape) - 2) + (pl.ds(start + k0 * (rows // n), (k1 - k0) * (rows // n)), slice(None))
    return ref.at[idx]


def _half(ref, h, part=WHOLE):
    rows = ref.shape[-2] // 2
    return _rows(ref, h * rows, rows, part)


def _remote(src, dst, ssem, rsem, k, to):
    return pltpu.make_async_remote_copy(src_ref=src, dst_ref=dst, send_sem=ssem.at[k], recv_sem=rsem.at[k], device_id=to,
                                        device_id_type=_MESH)


def _later(*args):
    return functools.partial(_remote, *args)


def gather(bufs, jobs):
    def build(reads, al, news, ssem, rsem):
        x, y, c = _me()
        p = 2 * x + y
        starts, arrivals = [], []
        for n, (b, kind, part) in enumerate(jobs):
            for j, chip in enumerate(_other_chips(x, y)):
                q = 2 * chip[0] + chip[1]
                if kind == "ici":
                    src, landing, to = _half(al[b].at[p], c, part), _half(al[b].at[q], c, part), (*chip, c)
                else:
                    src, landing, to = _half(al[b].at[q], c, part), _half(al[b].at[q], 1 - c, part), (x, y, 1 - c)
                starts.append(_later(src, src, ssem, rsem, 3 * n + j, to))
                arrivals.append(_later(landing, landing, ssem, rsem, 3 * n + j, to))
        return starts, arrivals, starts

    return Rider([], bufs, [], 3 * len(jobs), build)


def exchange_halves(grads):
    def build(reads, al, news, ssem, rsem):
        x, y, c = _me()
        cps = [_later(_half(g, 1 - c), t, ssem, rsem, w, (x, y, 1 - c)) for w, (g, t) in enumerate(zip(reads, news))]
        return cps, cps, cps

    return Rider(grads, [], [jax.ShapeDtypeStruct((g.shape[0], g.shape[1] // 2, g.shape[2]), g.dtype) for g in grads], len(grads), build)


def scatter_to_owner(sums, gots=None, part=WHOLE):
    def build(reads, al, news, ssem, rsem):
        x, y, c = _me()
        cps = []
        for w, (s, got) in enumerate(zip(reads, al or news)):
            rows = s.shape[-2]
            for j, chip in enumerate(_other_chips(x, y)):
                cps.append(_later(_rows(s.at[2 * chip[0] + chip[1]], 0, rows, part), _rows(got.at[j], 0, rows, part), ssem, rsem,
                                  3 * w + j, (*chip, c)))
        return cps, cps, cps

    news = [] if gots else [jax.ShapeDtypeStruct((3,) + s.shape[1:], s.dtype) for s in sums]
    return Rider(sums, gots or [], news, 3 * len(sums), build)


def join_halves(fulls):
    def build(reads, al, news, ssem, rsem):
        x, y, c = _me()
        starts, arrivals = [], []
        for w, f in enumerate(al):
            mine, landing = _half(f, c), _half(f, 1 - c)
            starts.append(_later(mine, mine, ssem, rsem, w, (x, y, 1 - c)))
            arrivals.append(_later(landing, landing, ssem, rsem, w, (x, y, 1 - c)))
        return starts, arrivals, starts

    return Rider([], fulls, [], len(fulls), build)


def _start_and_wait(rider, reads, al, news, ssem, rsem, first, last):
    @pl.when(first)
    def _():
        for cp in rider.build(reads, al, news, ssem, rsem)[0]:
            cp().start()

    def finish():
        @pl.when(last)
        def _():
            _, arrivals, sends = rider.build(reads, al, news, ssem, rsem)
            for cp in arrivals:
                cp().wait_recv()
            for cp in sends:
                cp().wait_send()

    return finish


def _call(name, body, grid, in_specs, args, out_specs, out_shape, scratch=(), semantics=None, rider=None, prefetch=()):
    n_pre, n_in, n_out, n_scr = len(prefetch), len(args), len(out_shape), len(scratch)
    nr, na, nn = (len(rider.reads), len(rider.aliased), len(rider.news)) if rider else (0, 0, 0)

    def wrapped(*refs):
        pre, refs = refs[:n_pre], refs[n_pre:]
        ins, reads = refs[:n_in], refs[n_in:n_in + nr]
        o0 = n_in + nr + na
        outs, al, news = refs[o0:o0 + n_out], refs[o0 + n_out:o0 + n_out + na], refs[o0 + n_out + na:o0 + n_out + na + nn]
        s0 = o0 + n_out + na + nn
        scr, (ssem, rsem) = refs[s0:s0 + n_scr], refs[s0 + n_scr:]
        first = functools.reduce(jnp.logical_and, [pl.program_id(a) == 0 for a in range(len(grid))])
        last = functools.reduce(jnp.logical_and, [pl.program_id(a) == g - 1 for a, g in enumerate(grid)])
        finish = _start_and_wait(rider, reads, al, news, ssem, rsem, first, last)
        body(*pre, *ins, *outs, *scr)
        finish()

    kernel_fn, all_in, all_out, shapes, scr = body, list(in_specs), list(out_specs), list(out_shape), list(scratch)
    operands, aliases = (*prefetch, *args), {}
    if rider:
        kernel_fn, semantics = wrapped, ("arbitrary",) * len(grid)
        all_in += [_HBM] * (nr + na)
        all_out += [_HBM] * (na + nn)
        shapes += [jax.ShapeDtypeStruct(a.shape, a.dtype) for a in rider.aliased] + rider.news
        scr += [pltpu.SemaphoreType.DMA((rider.nsem,)), pltpu.SemaphoreType.DMA((rider.nsem,))]
        operands += (*rider.reads, *rider.aliased)
        aliases = {n_pre + n_in + nr + i: n_out + i for i in range(na)}
    params = pltpu.CompilerParams(dimension_semantics=semantics, vmem_limit_bytes=VMEM_LIMIT)
    if n_pre:
        spec = pltpu.PrefetchScalarGridSpec(num_scalar_prefetch=n_pre, grid=grid, in_specs=all_in, out_specs=all_out, scratch_shapes=scr)
        outs = pl.pallas_call(kernel_fn, name=name, grid_spec=spec, out_shape=shapes, input_output_aliases=aliases, compiler_params=params)(*operands)
    else:
        outs = pl.pallas_call(kernel_fn, name=name, grid=grid, in_specs=all_in, out_specs=all_out, out_shape=shapes, scratch_shapes=scr,
                              input_output_aliases=aliases, compiler_params=params)(*operands)
    return list(outs[:n_out]), ((list(outs[n_out:n_out + na]), list(outs[n_out + na:])) if rider else None)


def run_step(name, rider):
    nr, na, nn = len(rider.reads), len(rider.aliased), len(rider.news)

    def body(*refs):
        reads = refs[:nr]
        al, news = refs[nr + na:nr + 2 * na], refs[nr + 2 * na:nr + 2 * na + nn]
        ssem, rsem = refs[nr + 2 * na + nn:]
        starts, arrivals, sends = rider.build(reads, al, news, ssem, rsem)
        for cp in starts:
            cp().start()
        for cp in arrivals:
            cp().wait_recv()
        for cp in sends:
            cp().wait_send()

    outs = pl.pallas_call(
        body, name=name, in_specs=[_HBM] * (nr + na), out_specs=[_HBM] * (na + nn),
        out_shape=[jax.ShapeDtypeStruct(a.shape, a.dtype) for a in rider.aliased] + rider.news,
        input_output_aliases={nr + i: i for i in range(na)},
        scratch_shapes=[pltpu.SemaphoreType.DMA((rider.nsem,)), pltpu.SemaphoreType.DMA((rider.nsem,))],
    )(*rider.reads, *rider.aliased)
    return list(outs[:na]), list(outs[na:])


def all_reduce_small(v):
    rows, lanes = v.shape

    def body(v_ref, o_ref, slots, send_sems, recv_sems):
        x, y, c = _me()
        me = 4 * x + 2 * y + c
        slots[me] = v_ref[...]
        cps = []
        for k in range(1, N_DEV):
            peer = (x ^ (k >> 2), y ^ ((k >> 1) & 1), c ^ (k & 1))
            cps.append(_remote(v_ref, slots.at[me], send_sems, recv_sems, k - 1, peer))
            cps[-1].start()
        for k in range(1, N_DEV):
            theirs = slots.at[me ^ k]
            _remote(theirs, theirs, send_sems, recv_sems, k - 1, (x, y, c)).wait_recv()
        for cp in cps:
            cp.wait_send()
        acc = slots[0]
        for i in range(1, N_DEV):
            acc = acc + slots[i]
        o_ref[...] = acc

    return pl.pallas_call(
        body, name="all_reduce_small",
        in_specs=[pl.BlockSpec(memory_space=pltpu.VMEM)], out_specs=pl.BlockSpec(memory_space=pltpu.VMEM),
        out_shape=jax.ShapeDtypeStruct((rows, lanes), F32),
        scratch_shapes=[pltpu.VMEM((N_DEV, rows, lanes), F32), pltpu.SemaphoreType.DMA((N_DEV - 1,)), pltpu.SemaphoreType.DMA((N_DEV - 1,))],
    )(v)


def _mm_call(name, mode, a, b, a_spec, b_spec, out_shape, out_spec, grid, acc_shape, scale=1.0, resid=None, resid_spec=None, rider=None):
    nk = grid[2]
    dims = _DIMS[mode]
    has_resid = resid is not None

    def body(*refs):
        a_ref, b_ref = refs[:2]
        r_ref = refs[2] if has_resid else None
        o_ref = refs[3] if has_resid else refs[2]

        def finish(r):
            if scale != 1.0:
                r = r * scale
            if has_resid:
                r = r_ref[...].astype(F32) + r
            o_ref[...] = r.astype(o_ref.dtype)

        part = lax.dot_general(a_ref[...].astype(BF16), b_ref[...].astype(BF16), dims, preferred_element_type=F32)
        if nk == 1:
            finish(part)
            return
        acc_ref = refs[-1]
        k = pl.program_id(2)

        @pl.when(k == 0)
        def _():
            acc_ref[...] = part

        @pl.when(k > 0)
        def _():
            acc_ref[...] += part

        @pl.when(k == nk - 1)
        def _():
            finish(acc_ref[...])

    in_specs = [a_spec, b_spec] + ([resid_spec] if has_resid else [])
    args = (a, b) + ((resid,) if has_resid else ())
    (out,), rid = _call(name, body, grid, in_specs, args, [out_spec], [out_shape], [pltpu.VMEM(acc_shape, F32)] if nk > 1 else [],
                        ("parallel", "parallel", "arbitrary"), rider)
    return out, rid


MM_VMEM_BUDGET = 40 * 1024 * 1024
_TILE_OPTS = (2048, 1408, 1024, 512, 256, 128)


def _tiles(m, n, kd, a_item, b_item, o_item, r_item=0, tm=None, tn=None, tk=None):
    def opts(full, fixed, cap):
        return [fixed] if fixed else [t for t in _TILE_OPTS if t <= cap and full % t == 0] or [full]

    best = None
    for cm in opts(m, tm, 1408):
        for cn in opts(n, tn, 1408):
            for ck in opts(kd, tk, 2048):
                blocks = cm * ck * a_item + ck * cn * b_item + cm * cn * (o_item + r_item)
                casts = (cm * ck * 2 if a_item == 4 else 0) + (ck * cn * 2 if b_item == 4 else 0)
                if 2 * blocks + cm * cn * 4 + casts <= MM_VMEM_BUDGET:
                    key = (cm * cn * ck, ck)
                    if best is None or key > best[0]:
                        best = (key, (cm, cn, ck))
    assert best is not None, (m, n, kd)
    return best[1]


def _item(x):
    return jnp.dtype(x.dtype).itemsize


def mm_nn(name, a, b, *, out_dtype=F32, scale=1.0, resid=None, rider=None):
    m, kd = a.shape
    n = b.shape[1]
    tm, tn, tk = _tiles(m, n, kd, _item(a), _item(b), jnp.dtype(out_dtype).itemsize, 0 if resid is None else _item(resid))
    o_spec = pl.BlockSpec((tm, tn), lambda i, j, k: (i, j))
    return _mm_call(
        name, "nn", a, b, pl.BlockSpec((tm, tk), lambda i, j, k: (i, k)), pl.BlockSpec((tk, tn), lambda i, j, k: (k, j)),
        jax.ShapeDtypeStruct((m, n), out_dtype), o_spec, (m // tm, n // tn, kd // tk), (tm, tn), scale, resid, o_spec, rider)


def mm_nt(name, a, b, *, out_dtype=F32, scale=1.0, resid=None, rider=None):
    m, kd = a.shape
    n = b.shape[0]
    tm, tn, tk = _tiles(m, n, kd, _item(a), _item(b), jnp.dtype(out_dtype).itemsize, 0 if resid is None else _item(resid))
    o_spec = pl.BlockSpec((tm, tn), lambda i, j, k: (i, j))
    return _mm_call(
        name, "nt", a, b, pl.BlockSpec((tm, tk), lambda i, j, k: (i, k)), pl.BlockSpec((tn, tk), lambda i, j, k: (j, k)),
        jax.ShapeDtypeStruct((m, n), out_dtype), o_spec, (m // tm, n // tn, kd // tk), (tm, tn), scale, resid, o_spec, rider)


def mm_tn(name, a, b, *, out_dtype=F32, scale=1.0, rider=None):
    kd, m = a.shape
    n = b.shape[1]
    tm, tn, tk = _tiles(m, n, kd, _item(a), _item(b), jnp.dtype(out_dtype).itemsize)
    return _mm_call(
        name, "tn", a, b, pl.BlockSpec((tk, tm), lambda i, j, k: (k, i)), pl.BlockSpec((tk, tn), lambda i, j, k: (k, j)),
        jax.ShapeDtypeStruct((m, n), out_dtype), pl.BlockSpec((tm, tn), lambda i, j, k: (i, j)),
        (m // tm, n // tn, kd // tk), (tm, tn), scale, rider=rider)


def mm_nt_sharded(name, a, w, *, resid=None, rider=None):
    m = a.shape[0]
    ns, n, c = w.shape
    tm, tn, _ = _tiles(m, n, c, _item(a), _item(w), 4, 0 if resid is None else _item(resid), tk=c)
    o_spec = pl.BlockSpec((tm, tn), lambda i, j, k: (i, j))
    return _mm_call(
        name, "nt", a, w, pl.BlockSpec((tm, c), lambda i, j, k: (i, k)), pl.BlockSpec((None, tn, c), lambda i, j, k: (k, j, 0)),
        jax.ShapeDtypeStruct((m, n), F32), o_spec, (m // tm, n // tn, ns), (tm, tn), 1.0, resid, o_spec, rider)


def mm_tn_sharded(name, a, b, ns, *, rider=None):
    kd, m = a.shape
    c = b.shape[1] // ns
    tm, _, tk = _tiles(m, c, kd, _item(a), _item(b), 2, tn=c)
    return _mm_call(
        name, "tn", a, b, pl.BlockSpec((tk, tm), lambda i, j, k: (k, i)), pl.BlockSpec((tk, c), lambda i, j, k: (k, j)),
        jax.ShapeDtypeStruct((ns, m, c), BF16), pl.BlockSpec((None, tm, c), lambda i, j, k: (j, i, 0)),
        (m // tm, ns, kd // tk), (tm, c), rider=rider)


def rms_fwd(name, x, g, out_dtype, rider=None):
    r, c = x.shape
    tm = _pick(r, (512, 256, 128, 64, 8))

    def body(x_ref, g_ref, y_ref, r_ref):
        xf = x_ref[...].astype(F32)
        rstd = lax.rsqrt(jnp.mean(xf * xf, axis=-1, keepdims=True) + EPS)
        y_ref[...] = ((xf * rstd) * g_ref[...]).astype(y_ref.dtype)
        r_ref[...] = rstd

    (y, rstd), rid = _call(
        name, body, (r // tm,), [pl.BlockSpec((tm, c), lambda i: (i, 0)), pl.BlockSpec((1, c), lambda i: (0, 0))], (x, g.reshape(1, c)),
        [pl.BlockSpec((tm, c), lambda i: (i, 0)), pl.BlockSpec((tm, 1), lambda i: (i, 0))],
        [jax.ShapeDtypeStruct((r, c), out_dtype), jax.ShapeDtypeStruct((r, 1), F32)], (), ("parallel",), rider)
    return (y, rstd) if rider is None else (y, rstd, rid)


def rms_bwd(name, x, g, rstd, dy, dres=None, rider=None, also_bf16=False):
    r, c = x.shape
    tm = _pick(r, (512, 256, 128, 64, 8))
    has_res = dres is not None

    def body(*refs):
        x_ref, g_ref, r_ref, dy_ref = refs[:4]
        dres_ref = refs[4] if has_res else None
        dx_ref, dg_ref = refs[4 + has_res:6 + has_res]
        xhat = x_ref[...].astype(F32) * r_ref[...]
        dyf = dy_ref[...].astype(F32)
        gdy = dyf * g_ref[...]
        dx = r_ref[...] * (gdy - xhat * jnp.mean(gdy * xhat, axis=-1, keepdims=True))
        if has_res:
            dx = dx + dres_ref[...]
        dx_ref[...] = dx
        if also_bf16:
            refs[-1][...] = dx.astype(BF16)

        @pl.when(pl.program_id(0) == 0)
        def _():
            dg_ref[...] = jnp.zeros_like(dg_ref)

        dg_ref[...] += jnp.sum(dyf * xhat, axis=0, keepdims=True)

    row = pl.BlockSpec((tm, c), lambda i: (i, 0))
    in_specs = [row, pl.BlockSpec((1, c), lambda i: (0, 0)), pl.BlockSpec((tm, 1), lambda i: (i, 0)), row] + ([row] if has_res else [])
    args = (x, g.reshape(1, c), rstd, dy) + ((dres,) if has_res else ())
    outs, rid = _call(name, body, (r // tm,), in_specs, args, [row, pl.BlockSpec((1, c), lambda i: (0, 0))] + [row] * also_bf16,
                      [jax.ShapeDtypeStruct((r, c), F32), jax.ShapeDtypeStruct((1, c), F32)] + [jax.ShapeDtypeStruct((r, c), BF16)] * also_bf16,
                      (), ("arbitrary",), rider)
    return (outs[0], outs[1].reshape(c), *outs[2:], *([] if rider is None else [rid]))


_LANES = 128


def _head_mean(v):
    if v.shape[1] == HEAD_DIM:
        return jnp.mean(v, axis=-1, keepdims=True)
    low = lax.broadcasted_iota(jnp.int32, v.shape, 1) < HEAD_DIM
    lo = jnp.sum(jnp.where(low, v, 0.0), axis=-1, keepdims=True)
    hi = jnp.sum(jnp.where(low, 0.0, v), axis=-1, keepdims=True)
    return jnp.where(low, lo, hi) * (1.0 / HEAD_DIM)


def _head_groups(c):
    width = _LANES if c % _LANES == 0 else HEAD_DIM
    assert c % width == 0, c
    return width, [slice(k * width, (k + 1) * width) for k in range(c // width)]


def _head_gain(g, width):
    return jnp.tile(g.reshape(1, HEAD_DIM), (1, width // HEAD_DIM))


def _rotate_half(y):
    half = HEAD_DIM // 2
    first = lax.broadcasted_iota(jnp.int32, y.shape, 1) % HEAD_DIM < half
    return jnp.where(first, -pltpu.roll(y, y.shape[1] - half, axis=1), pltpu.roll(y, half, axis=1))


def _rope_tables(rope, width):
    return [jnp.tile(t, (1, 2 * width // HEAD_DIM)) for t in rope]


def head_rms_fwd(name, x, g, rope=None):
    s, c = x.shape
    tm = _pick(s, (256, 128, 8))
    width, groups = _head_groups(c)

    def body(x_ref, g_ref, *refs):
        y_ref = refs[-1]
        for sl in groups:
            xs = x_ref[:, sl]
            y = (xs * lax.rsqrt(_head_mean(xs * xs) + EPS)) * g_ref[...]
            if rope:
                y = y * refs[0][...] + _rotate_half(y) * refs[1][...]
            y_ref[:, sl] = y

    row = pl.BlockSpec((tm, c), lambda i: (i, 0))
    tab = pl.BlockSpec((tm, width), lambda i: (i, 0))
    tables = _rope_tables(rope, width) if rope else []
    (y,), _ = _call(name, body, (s // tm,), [row, pl.BlockSpec((1, width), lambda i: (0, 0))] + [tab] * len(tables),
                    (x, _head_gain(g, width), *tables), [row], [jax.ShapeDtypeStruct((s, c), F32)], (), ("parallel",))
    return y


def head_rms_bwd(name, x, g, dy, rope=None):
    s, c = x.shape
    tm = _pick(s, (256, 128, 8))
    width, groups = _head_groups(c)

    def body(x_ref, g_ref, dy_ref, *refs):
        dx_ref, dg_ref = refs[-2:]

        @pl.when(pl.program_id(0) == 0)
        def _():
            dg_ref[...] = jnp.zeros_like(dg_ref)

        for sl in groups:
            xs, dys = x_ref[:, sl], dy_ref[:, sl]
            if rope:
                dys = dys * refs[0][...] - _rotate_half(dys * refs[1][...])
            rstd = lax.rsqrt(_head_mean(xs * xs) + EPS)
            xhat = xs * rstd
            gdy = dys * g_ref[...]
            dx_ref[:, sl] = rstd * (gdy - xhat * _head_mean(gdy * xhat))
            dg_ref[...] += jnp.sum(dys * xhat, axis=0, keepdims=True)

    row = pl.BlockSpec((tm, c), lambda i: (i, 0))
    vec = pl.BlockSpec((1, width), lambda i: (0, 0))
    tab = pl.BlockSpec((tm, width), lambda i: (i, 0))
    tables = _rope_tables(rope, width) if rope else []
    (dx, dg), _ = _call(name, body, (s // tm,), [row, vec, row] + [tab] * len(tables), (x, _head_gain(g, width), dy, *tables), [row, vec],
                        [jax.ShapeDtypeStruct((s, c), F32), jax.ShapeDtypeStruct((1, width), F32)], (), ("arbitrary",))
    return dx, jnp.sum(dg.reshape(width // HEAD_DIM, HEAD_DIM), axis=0)


@functools.partial(jax.custom_vjp, nondiff_argnums=(0,))
def head_rms(name, x, g):
    return head_rms_fwd(name + "_fwd", x, g)


def _head_rms_fwd(name, x, g):
    return head_rms_fwd(name + "_fwd", x, g), (x, g)


def _head_rms_bwd(name, res, dy):
    return head_rms_bwd(name + "_bwd", *res, dy)


head_rms.defvjp(_head_rms_fwd, _head_rms_bwd)


@functools.partial(jax.custom_vjp, nondiff_argnums=(0,))
def head_rms_rope(name, x, g, cos, sin):
    return head_rms_fwd(name + "_fwd", x, g, (cos, sin))


def _head_rms_rope_fwd(name, x, g, cos, sin):
    return head_rms_fwd(name + "_fwd", x, g, (cos, sin)), (x, g, cos, sin)


def _head_rms_rope_bwd(name, res, dy):
    x, g, cos, sin = res
    return (*head_rms_bwd(name + "_bwd", x, g, dy, (cos, sin)), jnp.zeros_like(cos), jnp.zeros_like(sin))


head_rms_rope.defvjp(_head_rms_rope_fwd, _head_rms_rope_bwd)


def _cumsum_call(name, a, reverse):
    h, s = a.shape
    tb = _LANES
    assert s % tb == 0

    def body(a_ref, o_ref):
        t_in = lax.broadcasted_iota(jnp.int32, (tb, tb), 0)
        t_out = lax.broadcasted_iota(jnp.int32, (tb, tb), 1)
        tri = jnp.where((t_in >= t_out) if reverse else (t_in <= t_out), 1.0, 0.0).astype(BF16)
        carry = jnp.zeros((h, 1), F32)
        blocks = range(s // tb)
        for b in (reversed(blocks) if reverse else blocks):
            cols = slice(b * tb, (b + 1) * tb)
            block = a_ref[:, cols]
            rest, local = block, jnp.zeros((h, tb), F32)
            for _ in range(3):
                piece = rest.astype(BF16)
                local = local + jnp.dot(piece, tri, preferred_element_type=F32)
                rest = rest - piece.astype(F32)
            o_ref[:, cols] = local + carry
            carry = carry + jnp.sum(block, axis=1, keepdims=True)

    whole = pl.BlockSpec((h, s), lambda j: (0, 0))
    (out,), _ = _call(name, body, (1,), [whole], (a,), [whole], [jax.ShapeDtypeStruct((h, s), F32)], (), ("arbitrary",))
    return out


@jax.custom_vjp
def time_cumsum(a):
    return _cumsum_call("gate_cumsum", a, False)


def _time_cumsum_fwd(a):
    return _cumsum_call("gate_cumsum", a, False), None


def _time_cumsum_bwd(_, dc):
    return (_cumsum_call("gate_cumsum_bwd", dc, True),)


time_cumsum.defvjp(_time_cumsum_fwd, _time_cumsum_bwd)


FFN_TM = 512


def _sigmoid(x):
    return 1.0 / (1.0 + jnp.exp(-x))


def ffn_gu(name, xn, wg, wu, rider=None):
    s, d = xn.shape
    ns, _, c = wg.shape
    tm = _pick(s, (FFN_TM, 128))

    def body(x_ref, wg_ref, wu_ref, h_ref, a_ref, b_ref):
        xb = x_ref[...]
        gv = jnp.dot(xb, wg_ref[...], preferred_element_type=F32)
        uv = jnp.dot(xb, wu_ref[...], preferred_element_type=F32)
        sig = _sigmoid(gv)
        silu = gv * sig
        h_ref[...] = (silu * uv).astype(BF16)
        a_ref[...] = (uv * (sig * (1.0 + gv * (1.0 - sig)))).astype(BF16)
        b_ref[...] = silu.astype(BF16)

    w_spec = pl.BlockSpec((None, d, c), lambda j, i: (j, 0, 0))
    o_spec = pl.BlockSpec((tm, c), lambda j, i: (i, j))
    return _call(
        name, body, (ns, s // tm), [pl.BlockSpec((tm, d), lambda j, i: (i, 0)), w_spec, w_spec], (xn, wg, wu),
        [o_spec, o_spec, o_spec], [jax.ShapeDtypeStruct((s, ns * c), BF16)] * 3, [], ("parallel", "parallel"), rider)


def ffn_dh(name, dy, wd, dh_dg, dh_du, ns, scale, rider=None):
    s, d = dy.shape
    f = wd.shape[0]
    c = f // ns
    tm = _pick(s, (FFN_TM, 128))

    def body(dy_ref, wd_ref, a_ref, b_ref, dg_ref, du_ref):
        dh = lax.dot_general(dy_ref[...].astype(BF16), wd_ref[...], _DIMS["nt"], preferred_element_type=F32) * scale
        dg_ref[...] = (dh * a_ref[...].astype(F32)).astype(BF16)
        du_ref[...] = (dh * b_ref[...].astype(F32)).astype(BF16)

    o_spec = pl.BlockSpec((tm, c), lambda j, i: (i, j))
    return _call(
        name, body, (ns, s // tm),
        [pl.BlockSpec((tm, d), lambda j, i: (i, 0)), pl.BlockSpec((c, d), lambda j, i: (j, 0)), o_spec, o_spec], (dy, wd, dh_dg, dh_du),
        [o_spec, o_spec], [jax.ShapeDtypeStruct((s, f), BF16), jax.ShapeDtypeStruct((s, f), BF16)],
        [], ("parallel", "parallel"), rider)


FOX_TQ = 512


def fox_tile(s_len):
    return min(FOX_TQ, s_len)


def _heads_per_block(h):
    return 2 if h % 2 == 0 else 1


def _fox_queries(q):
    return (q * (HEAD_DIM ** -0.5)).astype(BF16)


def _fox_scores(qs, kc, cq, ck, diagonal):
    s = lax.dot_general(qs, kc.astype(BF16), _DIMS["nt"], preferred_element_type=F32) + cq - ck
    if not diagonal:
        return s
    return jnp.where(lax.broadcasted_iota(jnp.int32, s.shape, 0) >= lax.broadcasted_iota(jnp.int32, s.shape, 1), s, MASK_VALUE)


def _fox_specs(h, s_len, tq):
    hb = _heads_per_block(h)
    qb = pl.BlockSpec((tq, hb * HEAD_DIM), lambda pp, i: (i, pp))
    kb = pl.BlockSpec((s_len, hb * HEAD_DIM), lambda pp, i: (0, pp))
    colb = pl.BlockSpec((hb, tq, 1), lambda pp, i: (pp, i, 0))
    rowb = pl.BlockSpec((hb, s_len // tq, 1, tq), lambda pp, i: (pp, 0, 0, 0))
    return hb, qb, kb, colb, rowb


def fox_fwd(q, k, v, cq, ck, rider=None):
    s_len, hd = q.shape
    h, d = hd // HEAD_DIM, HEAD_DIM
    tq = fox_tile(s_len)
    hb, qb, kb, colb, rowb = _fox_specs(h, s_len, tq)

    def body(q_ref, k_ref, v_ref, cq_ref, ck_ref, o_ref, lse_ref):
        i = pl.program_id(1)
        for hh in range(hb):
            lanes = slice(hh * d, (hh + 1) * d)
            qs, cqv = _fox_queries(q_ref[:, lanes]), cq_ref[hh]

            def chunk(c, carry, diagonal=False):
                m, l, acc = carry
                rows = pl.ds(pl.multiple_of(c * tq, tq), tq)
                s = _fox_scores(qs, k_ref[rows, lanes], cqv, ck_ref[hh, c], diagonal)
                m_new = jnp.maximum(m, jnp.max(s, axis=-1, keepdims=True))
                alpha = jnp.exp(m - m_new)
                p = jnp.exp(s - m_new)
                acc = alpha * acc + jnp.dot(p.astype(BF16), v_ref[rows, lanes].astype(BF16), preferred_element_type=F32)
                return m_new, alpha * l + jnp.sum(p, axis=-1, keepdims=True), acc

            init = (jnp.full((tq, 1), MASK_VALUE, F32), jnp.zeros((tq, 1), F32), jnp.zeros((tq, d), F32))
            m, l, acc = chunk(i, lax.fori_loop(0, i, chunk, init), diagonal=True)
            o_ref[:, lanes] = acc / l
            lse_ref[hh] = m + jnp.log(l)

    return _call(
        "fox_fwd", body, (h // hb, s_len // tq), [qb, kb, kb, colb, rowb], (q, k, v, cq, ck), [qb, colb],
        [jax.ShapeDtypeStruct((s_len, hd), F32), jax.ShapeDtypeStruct((h, s_len, 1), F32)], (), ("parallel", "parallel"), rider)


def fox_bwd(q, k, v, cq, ck, o, lse, do, rider=None):
    s_len, hd = q.shape
    h, d = hd // HEAD_DIM, HEAD_DIM
    tq = fox_tile(s_len)
    scale = HEAD_DIM ** -0.5
    hb, qb, kb, colb, rowb = _fox_specs(h, s_len, tq)

    def body(q_ref, k_ref, v_ref, cq_ref, ck_ref, o_ref, lse_ref, do_ref, dq_ref, dk_ref, dv_ref, dcq_ref, dck_ref):
        i = pl.program_id(1)

        @pl.when(i == 0)
        def _():
            dk_ref[...] = jnp.zeros_like(dk_ref)
            dv_ref[...] = jnp.zeros_like(dv_ref)
            dck_ref[...] = jnp.zeros_like(dck_ref)

        heads = []
        for hh in range(hb):
            lanes = slice(hh * d, (hh + 1) * d)
            dof = do_ref[:, lanes]
            heads.append((lanes, _fox_queries(q_ref[:, lanes]), cq_ref[hh], lse_ref[hh], dof.astype(BF16),
                          jnp.sum(dof * o_ref[:, lanes], axis=-1, keepdims=True)))

        def chunk(c, carry, diagonal=False):
            rows = pl.ds(pl.multiple_of(c * tq, tq), tq)
            out, dks, dvs = [], [], []
            for hh, (lanes, qs, cqv, lse_h, dob, delta) in enumerate(heads):
                dq, dcq = carry[hh]
                kc = k_ref[rows, lanes]
                p = jnp.exp(_fox_scores(qs, kc, cqv, ck_ref[hh, c], diagonal) - lse_h)
                dp = lax.dot_general(dob, v_ref[rows, lanes].astype(BF16), _DIMS["nt"], preferred_element_type=F32)
                ds = p * (dp - delta)
                dsb = ds.astype(BF16)
                dvs.append(lax.dot_general(p.astype(BF16), dob, _DIMS["tn"], preferred_element_type=F32))
                dks.append(lax.dot_general(dsb, qs, _DIMS["tn"], preferred_element_type=F32))
                dck_ref[hh, c] -= jnp.sum(ds, axis=0, keepdims=True)
                out.append((dq + jnp.dot(dsb, kc.astype(BF16), preferred_element_type=F32), dcq + jnp.sum(ds, axis=-1, keepdims=True)))
            dk_ref[rows, :] += jnp.concatenate(dks, axis=1)
            dv_ref[rows, :] += jnp.concatenate(dvs, axis=1)
            return tuple(out)

        init = tuple((jnp.zeros((tq, d), F32), jnp.zeros((tq, 1), F32)) for _ in range(hb))
        done = chunk(i, lax.fori_loop(0, i, chunk, init), diagonal=True)
        dq_ref[...] = jnp.concatenate([dq for dq, _ in done], axis=1) * scale
        for hh, (_, dcq) in enumerate(done):
            dcq_ref[hh] = dcq

    return _call(
        "fox_bwd", body, (h // hb, s_len // tq), [qb, kb, kb, colb, rowb, qb, colb, qb], (q, k, v, cq, ck, o, lse, do),
        [qb, kb, kb, colb, rowb],
        [jax.ShapeDtypeStruct((s_len, hd), F32)] * 3
        + [jax.ShapeDtypeStruct((h, s_len, 1), F32), jax.ShapeDtypeStruct((h, s_len // tq, 1, tq), F32)],
        (), ("parallel", "arbitrary"), rider)


def _stack_heads(ref, first, g):
    return jnp.concatenate([ref[:, (first + j) * HEAD_DIM:(first + j + 1) * HEAD_DIM] for j in range(g)], axis=0)


def _window(prev_ref, cur_ref, hh):
    lanes = slice(hh * HEAD_DIM, (hh + 1) * HEAD_DIM)
    return jnp.concatenate([prev_ref[:, lanes], cur_ref[:, lanes]], axis=0).astype(BF16)


def _swa_band(g, w):
    t = lax.broadcasted_iota(jnp.int32, (g * w, 2 * w), 0) % w
    col = lax.broadcasted_iota(jnp.int32, (g * w, 2 * w), 1)
    rel = t + w - col
    band = (rel >= 0) & (rel < w)
    return jnp.where(jnp.stack([band & (col >= w), band]), 0.0, MASK_VALUE).astype(F32)


def _swa_probs(qs, kw, sink, band):
    s = lax.dot_general(qs, kw, _DIMS["nt"], preferred_element_type=F32) + band
    m = jnp.maximum(jnp.max(s, axis=-1, keepdims=True), sink)
    p = jnp.exp(s - m)
    ps = jnp.exp(sink - m)
    linv = 1.0 / (jnp.sum(p, axis=-1, keepdims=True) + ps)
    return p * linv, ps * linv


def _swa_specs(hk, g, s_len):
    w = WINDOW
    assert s_len % w == 0
    hb = _heads_per_block(hk)
    qb = pl.BlockSpec((w, hb * g * HEAD_DIM), lambda pp, n: (n, pp))
    prev = pl.BlockSpec((w, hb * HEAD_DIM), lambda pp, n: (jnp.maximum(n - 1, 0), pp))
    cur = pl.BlockSpec((w, hb * HEAD_DIM), lambda pp, n: (n, pp))
    sb = pl.BlockSpec((hb, g * w, 1), lambda pp, n: (pp, 0, 0))
    band = pl.BlockSpec((None, g * w, 2 * w), lambda pp, n: (jnp.minimum(n, 1), 0, 0))
    return hb, qb, prev, cur, sb, band


def swa_fwd(q, k, v, sink, rider=None):
    s_len = q.shape[0]
    hk = k.shape[1] // HEAD_DIM
    g = q.shape[1] // k.shape[1]
    w, d = WINDOW, HEAD_DIM
    hb, qb, prev, cur, sb, bandb = _swa_specs(hk, g, s_len)

    def body(q_ref, kp_ref, kc_ref, vp_ref, vc_ref, sink_ref, band_ref, o_ref):
        for hh in range(hb):
            qs = (_stack_heads(q_ref, hh * g, g) * (HEAD_DIM ** -0.5)).astype(BF16)
            p, _ = _swa_probs(qs, _window(kp_ref, kc_ref, hh), sink_ref[hh], band_ref[...])
            o = jnp.dot(p.astype(BF16), _window(vp_ref, vc_ref, hh), preferred_element_type=F32)
            for j in range(g):
                o_ref[:, (hh * g + j) * d:(hh * g + j + 1) * d] = o[j * w:(j + 1) * w]

    (o,), rid = _call("swa_fwd", body, (hk // hb, s_len // w), [qb, prev, cur, prev, cur, sb, bandb],
                      (q, k, k, v, v, sink, _swa_band(g, w)), [qb], [jax.ShapeDtypeStruct(q.shape, F32)], (),
                      ("parallel", "parallel"), rider)
    return o, rid


def swa_bwd(q, k, v, sink, o, do, rider=None):
    s_len = q.shape[0]
    hk = k.shape[1] // HEAD_DIM
    g = q.shape[1] // k.shape[1]
    w, d = WINDOW, HEAD_DIM
    scale = HEAD_DIM ** -0.5
    hb, qb, prev, cur, sb, bandb = _swa_specs(hk, g, s_len)

    def body(q_ref, kp_ref, kc_ref, vp_ref, vc_ref, sink_ref, band_ref, o_ref, do_ref, dq_ref, dkp_ref, dkc_ref, dvp_ref, dvc_ref,
             dsink_ref):
        @pl.when(pl.program_id(1) == 0)
        def _():
            dsink_ref[...] = jnp.zeros_like(dsink_ref)

        for hh in range(hb):
            lanes = slice(hh * d, (hh + 1) * d)
            qs = (_stack_heads(q_ref, hh * g, g) * scale).astype(BF16)
            kw, vw = _window(kp_ref, kc_ref, hh), _window(vp_ref, vc_ref, hh)
            p, ps = _swa_probs(qs, kw, sink_ref[hh], band_ref[...])
            dof = _stack_heads(do_ref, hh * g, g)
            dob = dof.astype(BF16)
            delta = jnp.sum(dof * _stack_heads(o_ref, hh * g, g), axis=-1, keepdims=True)
            dp = lax.dot_general(dob, vw, _DIMS["nt"], preferred_element_type=F32)
            ds = p * (dp - delta)
            dsb = ds.astype(BF16)
            dsink_ref[hh] -= ps * delta
            dq = jnp.dot(dsb, kw, preferred_element_type=F32) * scale
            for j in range(g):
                dq_ref[:, (hh * g + j) * d:(hh * g + j + 1) * d] = dq[j * w:(j + 1) * w]
            dkw = lax.dot_general(dsb, qs, _DIMS["tn"], preferred_element_type=F32)
            dvw = lax.dot_general(p.astype(BF16), dob, _DIMS["tn"], preferred_element_type=F32)
            dkp_ref[:, lanes] = dkw[:w]
            dkc_ref[:, lanes] = dkw[w:]
            dvp_ref[:, lanes] = dvw[:w]
            dvc_ref[:, lanes] = dvw[w:]

    kv_shape = jax.ShapeDtypeStruct(k.shape, F32)
    (dq, dkp, dkc, dvp, dvc, dsink), rid = _call(
        "swa_bwd", body, (hk // hb, s_len // w), [qb, prev, cur, prev, cur, sb, bandb, qb, qb],
        (q, k, k, v, v, sink, _swa_band(g, w), o, do),
        [qb, cur, cur, cur, cur, sb],
        [jax.ShapeDtypeStruct(q.shape, F32), kv_shape, kv_shape, kv_shape, kv_shape, jax.ShapeDtypeStruct((hk, g * w, 1), F32)],
        (), ("parallel", "arbitrary"), rider)

    def shift_up(a):
        return jnp.concatenate([a[w:], jnp.zeros_like(a[:w])], axis=0)

    return (dq, dkc + shift_up(dkp), dvc + shift_up(dvp), dsink), rid


def loss_call(y, target):
    s, d = y.shape
    tm = _pick(s, (512, 256, 128))

    def body(y_ref, t_ref, l_ref, dy_ref, dyb_ref):
        e = y_ref[...] - t_ref[...]
        dy = e * (1.0 / d)
        dy_ref[...] = dy
        dyb_ref[...] = dy.astype(BF16)

        @pl.when(pl.program_id(0) == 0)
        def _():
            l_ref[...] = jnp.zeros_like(l_ref)

        l_ref[...] += jnp.sum(jnp.sum(e * e, axis=0, keepdims=True), axis=1, keepdims=True) * (0.5 / d)

    row = pl.BlockSpec((tm, d), lambda i: (i, 0))
    (l, dy, dyb), _ = _call("loss_head", body, (s // tm,), [row, row], (y, target), [pl.BlockSpec((1, 1), lambda i: (0, 0)), row, row],
                            [jax.ShapeDtypeStruct((1, 1), F32), jax.ShapeDtypeStruct((s, d), F32), jax.ShapeDtypeStruct((s, d), BF16)],
                            (), ("arbitrary",))
    return l[0, 0], dy, dyb


def _row_tile(rows, cols, itemsize, block_bytes=1 << 20):
    target = max(16, block_bytes // (cols * itemsize))
    fits = [t for t in range(16, rows + 1, 16) if rows % t == 0 and t <= target]
    return fits[-1] if fits else rows


CAST_STEPS = 8


def cast_place(name, ws, p_idx, rider=None):
    n = len(ws)
    assert all(w.shape[0] % (16 * CAST_STEPS) == 0 for w in ws), [w.shape for w in ws]

    def body(p_ref, *refs):
        for w_ref, o_ref in zip(refs[:n], refs[n:]):
            o_ref[...] = w_ref[...].astype(BF16)

    return _call(
        name, body, (CAST_STEPS,), [pl.BlockSpec((w.shape[0] // CAST_STEPS, w.shape[1]), lambda i, pr: (i, 0)) for w in ws], tuple(ws),
        [pl.BlockSpec((None, w.shape[0] // CAST_STEPS, w.shape[1]), lambda i, pr: (pr[0], i, 0)) for w in ws],
        [jax.ShapeDtypeStruct((N_CHIPS,) + w.shape, BF16) for w in ws], (), ("parallel",), rider, prefetch=(p_idx,))


def chip_sum(name, grad, theirs, c_idx):
    ns, r, cols = grad.shape
    rh = r // 2
    tr = _row_tile(rh, cols, 2, 2 << 20)
    nb = rh // tr

    def body(c_ref, a_ref, b_ref, o_ref):
        o_ref[...] = (a_ref[...].astype(F32) + b_ref[...].astype(F32)).astype(o_ref.dtype)

    return pl.pallas_call(
        body, name=name,
        grid_spec=pltpu.PrefetchScalarGridSpec(
            num_scalar_prefetch=1, grid=(ns, nb),
            in_specs=[pl.BlockSpec((None, tr, cols), lambda q, i, cr: (q, cr[0] * nb + i, 0)),
                      pl.BlockSpec((None, tr, cols), lambda q, i, cr: (q, i, 0))],
            out_specs=pl.BlockSpec((None, tr, cols), lambda q, i, cr: (q, i, 0))),
        out_shape=jax.ShapeDtypeStruct((ns, rh, cols), BF16),
        compiler_params=pltpu.CompilerParams(dimension_semantics=("parallel", "parallel"), vmem_limit_bytes=VMEM_LIMIT),
    )(c_idx, grad, theirs)


def owner_sum(name, sums, got, pc_idx):
    ns, rh, cols = sums.shape
    tr = _row_tile(rh, cols, 4, 2 << 20)
    nb = rh // tr

    def body(pc_ref, a_ref, b_ref, o_ref):
        o_ref[...] = ((a_ref[...].astype(F32) + b_ref[0].astype(F32)) + b_ref[1].astype(F32)) + b_ref[2].astype(F32)

    return pl.pallas_call(
        body, name=name,
        grid_spec=pltpu.PrefetchScalarGridSpec(
            num_scalar_prefetch=1, grid=(nb,),
            in_specs=[pl.BlockSpec((None, tr, cols), lambda i, pc: (pc[0], i, 0)),
                      pl.BlockSpec((3, tr, cols), lambda i, pc: (0, i, 0))],
            out_specs=pl.BlockSpec((tr, cols), lambda i, pc: (pc[1] * nb + i, 0))),
        out_shape=jax.ShapeDtypeStruct((2 * rh, cols), F32),
        compiler_params=pltpu.CompilerParams(dimension_semantics=("parallel",), vmem_limit_bytes=VMEM_LIMIT),
    )(pc_idx, sums, got)


def adamw(name, w, g, m, v):
    r, cols = w.shape
    tr = _row_tile(r, cols, 4)
    c1 = 1.0 / (1.0 - ADAM_B1 ** ADAM_STEP)
    c2 = 1.0 / (1.0 - ADAM_B2 ** ADAM_STEP)

    def body(w_ref, g_ref, m_ref, v_ref, go_ref, d_ref, nm_ref, nv_ref):
        gv = g_ref[...]
        nm = ADAM_B1 * m_ref[...] + (1.0 - ADAM_B1) * gv
        nv = ADAM_B2 * v_ref[...] + (1.0 - ADAM_B2) * (gv * gv)
        go_ref[...] = gv
        d_ref[...] = -ADAM_LR * ((nm * c1) / (jnp.sqrt(nv * c2) + ADAM_EPS) + ADAM_WD * w_ref[...])
        nm_ref[...] = nm
        nv_ref[...] = nv

    blk = pl.BlockSpec((tr, cols), lambda i: (i, 0))
    return _call(name, body, (r // tr,), [blk] * 4, (w, g, m, v), [blk] * 4, [jax.ShapeDtypeStruct((r, cols), F32)] * 4, (), ("parallel",))


def _win_layout(d_model):
    hf = hq = d_model // (2 * HEAD_DIM)
    hk = hq // 4
    sizes = [hf * HEAD_DIM, hf * HEAD_DIM, hf * HEAD_DIM, hf, hq * HEAD_DIM, hk * HEAD_DIM, hk * HEAD_DIM]
    return hf, hq, hk, sizes


class WinPlan:
    def __init__(self, d_model, ns=N_CHIPS):
        self.hf, self.hq, self.hk, self.sizes = _win_layout(d_model)
        self.ns, self.cs = ns, sum(self.sizes) // ns
        self.jump_at = sum(self.sizes[:4])
        self.jump_by = -self.jump_at % _LANES
        self.base = [self.pos(s * self.cs) // _LANES * _LANES for s in range(ns)]
        ends = [self.pos((s + 1) * self.cs - 1) + 1 - self.base[s] for s in range(ns)]
        self.width = -(-max(ends) // _LANES) * _LANES
        self.total = -(-max(b + self.width for b in self.base) // 1024) * 1024
        starts = [0]
        for sz in self.sizes:
            starts.append(starts[-1] + sz)
        self.segments = [(self.pos(a), sz) for a, sz in zip(starts, self.sizes)]

    def pos(self, g):
        return g if g < self.jump_at else g + self.jump_by

    def pieces(self, s):
        g0, g1 = s * self.cs, (s + 1) * self.cs
        cuts = [g0] + ([self.jump_at] if g0 < self.jump_at < g1 else []) + [g1]
        return [(a - g0, b - a, self.pos(a) - self.base[s]) for a, b in zip(cuts[:-1], cuts[1:])]

    def place(self, w, s):
        parts, at = [], 0
        for t0, n, j0 in self.pieces(s):
            parts += [jnp.zeros((w.shape[0], j0 - at), w.dtype), w[:, t0:t0 + n]]
            at = j0 + n
        return jnp.concatenate(parts + [jnp.zeros((w.shape[0], self.width - at), w.dtype)], axis=1)

    def unplace(self, slab, s):
        return jnp.concatenate([slab[:, j0:j0 + n] for _, n, j0 in self.pieces(s)], axis=1)

    def assemble(self, slabs):
        return sum(jnp.pad(slabs[s], ((0, 0), (b, self.total - b - self.width))) for s, b in enumerate(self.base))

    def split(self, full):
        return jnp.stack([full[:, b:b + self.width] for b in self.base])


def _attn_inputs(proj, sm, positions):
    s_len = proj.shape[0]
    plan = WinPlan(sm["norm_mix_g"].shape[0])
    hf, hq, hk = plan.hf, plan.hq, plan.hk
    grp = hq // hk
    q_f, k_f, v_f, f_logit, q_s, k_s, v_s = [proj[:, a:a + n] for a, n in plan.segments]

    q_f = head_rms("fox_qnorm", q_f, sm["fox_q_norm_g"])
    k_f = head_rms("fox_knorm", k_f, sm["fox_k_norm_g"])
    log_f = jax.nn.log_sigmoid(f_logit + sm["b_forget"])
    c = time_cumsum(log_f.T)

    inv_freq = ROPE_THETA ** (-jnp.arange(0, HEAD_DIM, 2, dtype=F32) / HEAD_DIM)
    ang = positions.astype(F32)[:, None] * inv_freq
    cos, sin = jnp.cos(ang), jnp.sin(ang)
    q_s = head_rms_rope("swa_qnorm", q_s, sm["swa_q_norm_g"], cos, sin)
    k_s = head_rms_rope("swa_knorm", k_s, sm["swa_k_norm_g"], cos, sin)
    sink = jnp.broadcast_to(sm["swa_sinks"].reshape(hk, grp, 1, 1), (hk, grp, WINDOW, 1)).reshape(hk, grp * WINDOW, 1)
    tq = fox_tile(s_len)
    return (q_f, k_f, v_f, c[:, :, None], c.reshape(hf, s_len // tq, 1, tq)), (q_s, k_s, v_s, sink)


_BIG = ("ffn1_w_gate", "ffn1_w_up", "ffn1_w_down", "w_in", "w_out", "ffn2_w_gate", "ffn2_w_up", "ffn2_w_down")
_SMALL = ("norm_ffn1_g", "norm_mix_g", "b_forget", "fox_q_norm_g", "fox_k_norm_g", "swa_q_norm_g", "swa_k_norm_g", "swa_sinks",
          "out_norm_fox_g", "out_norm_swa_g", "norm_ffn2_g")
_ATTN_SMALL = ("norm_mix_g", "b_forget", "fox_q_norm_g", "fox_k_norm_g", "swa_q_norm_g", "swa_k_norm_g", "swa_sinks")
_ALL = ("norm_ffn1_g", "ffn1_w_gate", "ffn1_w_up", "ffn1_w_down", "norm_mix_g", "w_in", "b_forget", "fox_q_norm_g", "fox_k_norm_g",
        "swa_q_norm_g", "swa_k_norm_g", "swa_sinks", "out_norm_fox_g", "out_norm_swa_g", "w_out", "norm_ffn2_g", "ffn2_w_gate",
        "ffn2_w_up", "ffn2_w_down")


def _pack_small(d):
    parts = []
    for k in _SMALL:
        v = d[k].reshape(-1)
        rows = -(-v.shape[0] // _LANES)
        parts.append(jnp.pad(v, (0, rows * _LANES - v.shape[0])).reshape(rows, _LANES))
    a = jnp.concatenate(parts, axis=0)
    return jnp.pad(a, ((0, -a.shape[0] % 8), (0, 0)))


def _unpack_small(a, like):
    out, r0 = {}, 0
    for k in _SMALL:
        nvals = like[k].shape[1]
        rows = -(-nvals // _LANES)
        out[k] = a[r0:r0 + rows].reshape(-1)[:nvals].reshape(1, nvals)
        r0 += rows
    return out


def _stacked(w):
    return w.reshape(-1, w.shape[-1])


def _local_step(shards, sm, x, positions, target, p_idx, c_idx, pc_idx):
    ns = N_CHIPS
    full = {}

    def fetch(*jobs):
        names = list(dict.fromkeys(n for n, _, _ in jobs))
        return names, gather([bufs[n] for n in names], [(names.index(n), kind, part) for n, kind, part in jobs])

    def take(names, rid):
        for n, b in zip(names, rid[0]):
            bufs[n] = b

    n1 = ["ffn1_w_gate", "ffn1_w_up", "ffn1_w_down"]
    n2 = ["ffn2_w_gate", "ffn2_w_up", "ffn2_w_down"]
    later = ["w_in", "w_out"] + n2
    placed, _ = cast_place("cast_place_ffn1", [shards[n] for n in n1], p_idx)
    bufs = dict(zip(n1, placed))
    gate1, up1, down1 = n1
    gate2, up2, down2 = n2
    names, rider = fetch((gate1, "ici", WHOLE), (up1, "ici", WHOLE))
    placed, rid = cast_place("cast_place_later", [shards[n] for n in later], p_idx, rider=rider)
    bufs.update(zip(later, placed))
    take(names, rid)
    names, rider = fetch((gate1, "d2d", WHOLE), (up1, "d2d", WHOLE), (down1, "ici", (0, 1, 4)))
    xn1, r1, rid = rms_fwd("ffn1_norm", x, sm["norm_ffn1_g"], BF16, rider=rider)
    take(names, rid)
    names, rider = fetch((down1, "ici", (1, 4, 4)), ("w_in", "ici", (0, 1, 4)))
    (hid1, hdg1, hdu1), rid = ffn_gu("ffn1_gu", xn1, bufs[gate1], bufs[up1], rider=rider)
    take(names, rid)
    names, rider = fetch((down1, "d2d", WHOLE))
    take(names, run_step("gather_d2d_ffn1_down", rider))
    wd1 = _stacked(bufs[down1])
    names, rider = fetch(("w_in", "ici", (1, 4, 4)))
    h1, rid = mm_nn("ffn1_down", hid1, wd1, scale=0.5, resid=x, rider=rider)
    take(names, rid)

    names, rider = fetch(("w_in", "d2d", WHOLE))
    u, r_mix, rid = rms_fwd("mix_norm", h1, sm["norm_mix_g"], BF16, rider=rider)
    take(names, rid)
    names, rider = fetch(("w_out", "ici", WHOLE))
    plan = WinPlan(x.shape[1])
    win = plan.assemble(bufs["w_in"])
    proj, rid = mm_nn("mix_inproj", u, win, rider=rider)
    take(names, rid)
    sm_attn = {k: sm[k] for k in _ATTN_SMALL}
    (fox_in, swa_in), attn_vjp = jax.vjp(lambda pr, s: _attn_inputs(pr, s, positions), proj, sm_attn)
    names, rider = fetch((gate2, "ici", WHOLE), (up2, "ici", (0, 1, 4)), ("w_out", "d2d", WHOLE))
    (o_f, lse), rid = fox_fwd(*fox_in, rider=rider)
    take(names, rid)
    names, rider = fetch((up2, "ici", (1, 4, 4)), (gate2, "d2d", WHOLE), (up2, "d2d", (0, 1, 4)))
    o_s, rid = swa_fwd(*swa_in, rider=rider)
    take(names, rid)
    o_fox, o_swa = o_f, o_s
    nf, r_fox = rms_fwd("out_norm_fox", o_fox, sm["out_norm_fox_g"], BF16)
    nsw, r_swa = rms_fwd("out_norm_swa", o_swa, sm["out_norm_swa_g"], BF16)
    o = jnp.concatenate([nf, nsw], axis=-1)
    wout = _stacked(bufs["w_out"])
    names, rider = fetch((down2, "ici", (0, 1, 4)), (up2, "d2d", (1, 4, 4)))
    h2, rid = mm_nn("out_proj", o, wout, resid=h1, rider=rider)
    take(names, rid)

    xn2, r2 = rms_fwd("ffn2_norm", h2, sm["norm_ffn2_g"], BF16)
    names, rider = fetch((down2, "ici", (1, 4, 4)))
    (hid2, hdg2, hdu2), rid = ffn_gu("ffn2_gu", xn2, bufs[gate2], bufs[up2], rider=rider)
    take(names, rid)
    names, rider = fetch((down2, "d2d", WHOLE))
    take(names, run_step("gather_d2d_ffn2_down", rider))
    wd2 = _stacked(bufs["ffn2_w_down"])
    y, _ = mm_nn("ffn2_down", hid2, wd2, scale=0.5, resid=h2)
    loss, dy, dy_b = loss_call(y, target)

    red = {}

    def grad(n, g):
        red[n] = {"grad": g.reshape(ns, -1, g.shape[-1])}

    def ride(*steps):
        def done(rid):
            a0 = n0 = 0
            for rd, cb in steps:
                cb(rid[0][a0:a0 + len(rd.aliased)], rid[1][n0:n0 + len(rd.news)])
                a0, n0 = a0 + len(rd.aliased), n0 + len(rd.news)

        return (combine(*[s[0] for s in steps]) if len(steps) > 1 else steps[0][0]), done

    def xchg(*names):
        def cb(al, news):
            for n, t in zip(names, news):
                red[n]["sum"] = chip_sum("chip_sum_" + n, red[n]["grad"], t, c_idx)

        return exchange_halves([red[n]["grad"] for n in names]), cb

    def scat(n, part=WHOLE):
        def cb(al, news):
            red[n]["got"] = (al or news)[0]

        return scatter_to_owner([red[n]["sum"]], [red[n]["got"]] if "got" in red[n] else None, part), cb

    def own(n):
        red[n]["half"] = owner_sum("owner_sum_" + n, red[n]["sum"], red[n]["got"], pc_idx)

    def join(*names):
        return join_halves([red[n]["half"] for n in names]), lambda al, news: full.update(zip(names, al))

    dwd2, _ = mm_tn("ffn2_dwd", hid2, dy_b, out_dtype=BF16, scale=0.5)
    grad(down2, dwd2)
    rider, done = ride(xchg(down2))
    (dg2, du2), rid = ffn_dh("ffn2_dh", dy_b, wd2, hdg2, hdu2, ns, 0.5, rider=rider)
    done(rid)
    rider, done = ride(scat(down2, (0, 1, 2)))
    dwg2, rid = mm_tn_sharded("ffn2_dwg", xn2, dg2, ns, rider=rider)
    done(rid)
    grad(gate2, dwg2)
    rider, done = ride(scat(down2, (1, 2, 2)), xchg(gate2))
    dwu2, rid = mm_tn_sharded("ffn2_dwu", xn2, du2, ns, rider=rider)
    done(rid)
    grad(up2, dwu2)
    rider, done = ride(scat(gate2, (0, 1, 2)), xchg(up2))
    dxn, rid = mm_nt_sharded("ffn2_dxn_g", dg2, bufs[gate2], rider=rider)
    done(rid)
    rider, done = ride(scat(gate2, (1, 2, 2)))
    dxn, rid = mm_nt_sharded("ffn2_dxn_u", du2, bufs[up2], resid=dxn, rider=rider)
    done(rid)
    dh2, dgain_ffn2, dh2_b = rms_bwd("ffn2_dnorm", h2, sm["norm_ffn2_g"], r2, dxn, dres=dy, also_bf16=True)
    own(down2)
    own(gate2)

    do, _ = mm_nt("out_do", dh2_b, wout)
    dwout, _ = mm_tn("out_dw", o, dh2_b, out_dtype=BF16)
    cf = o_fox.shape[1]
    d_fox, dgain_fox = rms_bwd("out_dnorm_fox", o_fox, sm["out_norm_fox_g"], r_fox, do[:, :cf])
    d_swa, dgain_swa = rms_bwd("out_dnorm_swa", o_swa, sm["out_norm_swa_g"], r_swa, do[:, cf:])
    grad("w_out", dwout)
    rider, done = ride(scat(up2))
    swa_cts, rid = swa_bwd(*swa_in, o_s, d_swa, rider=rider)
    done(rid)
    own(up2)
    rider, done = ride(xchg("w_out"), join(down2, gate2, up2))
    fox_cts, rid = fox_bwd(*fox_in, o_f, lse, d_fox, rider=rider)
    done(rid)
    dproj, dsm_attn = attn_vjp((tuple(fox_cts), tuple(swa_cts)))
    dproj = dproj.astype(BF16)

    rider, done = ride(scat("w_out"))
    du, rid = mm_nt("mix_du", dproj, win, rider=rider)
    done(rid)
    dwin, _ = mm_tn("mix_dwin", u, dproj, out_dtype=BF16)
    grad("w_in", plan.split(dwin))
    rider, done = ride(xchg("w_in"))
    dh1, dgain_mix, dh1_b, rid = rms_bwd("mix_dnorm", h1, sm["norm_mix_g"], r_mix, du, dres=dh2, rider=rider, also_bf16=True)
    done(rid)
    own("w_out")

    rider, done = ride(scat("w_in", (0, 1, 2)))
    dwd1, rid = mm_tn("ffn1_dwd", hid1, dh1_b, out_dtype=BF16, scale=0.5, rider=rider)
    done(rid)
    grad(down1, dwd1)
    rider, done = ride(scat("w_in", (1, 2, 2)), xchg(down1))
    (dg1, du1), rid = ffn_dh("ffn1_dh", dh1_b, wd1, hdg1, hdu1, ns, 0.5, rider=rider)
    done(rid)
    own("w_in")
    rider, done = ride(scat(down1, (0, 1, 2)), join("w_out"))
    dwg1, rid = mm_tn_sharded("ffn1_dwg", xn1, dg1, ns, rider=rider)
    done(rid)
    grad(gate1, dwg1)
    rider, done = ride(scat(down1, (1, 2, 2)), xchg(gate1), join("w_in"))
    dwu1, rid = mm_tn_sharded("ffn1_dwu", xn1, du1, ns, rider=rider)
    done(rid)
    grad(up1, dwu1)
    own(down1)
    rider, done = ride(scat(gate1, (0, 1, 2)), xchg(up1), join(down1))
    dxn, rid = mm_nt_sharded("ffn1_dxn_g", dg1, bufs[gate1], rider=rider)
    done(rid)
    rider, done = ride(scat(gate1, (1, 2, 2)), scat(up1, (0, 1, 4)))
    dxn, rid = mm_nt_sharded("ffn1_dxn_u", du1, bufs[up1], resid=dxn, rider=rider)
    done(rid)
    dx, dgain_ffn1 = rms_bwd("ffn1_dnorm", x, sm["norm_ffn1_g"], r1, dxn, dres=dh1)
    own(gate1)

    rider, done = ride(scat(up1, (1, 4, 4)), join(gate1))
    done(run_step("reduce_tail", rider))
    own(up1)
    rider, done = ride(join(up1))
    done(run_step("join_tail", rider))

    g_small = dict(dsm_attn)
    g_small["norm_mix_g"] = g_small["norm_mix_g"] + dgain_mix
    g_small.update(norm_ffn1_g=dgain_ffn1, norm_ffn2_g=dgain_ffn2, out_norm_fox_g=dgain_fox, out_norm_swa_g=dgain_swa)
    return loss, dx, full, g_small


def kernel(x, positions, norm_ffn1_g, ffn1_w_gate, ffn1_w_up, ffn1_w_down, norm_mix_g, w_in, b_forget, fox_q_norm_g, fox_k_norm_g, swa_q_norm_g, swa_k_norm_g, swa_sinks, out_norm_fox_g, out_norm_swa_g, w_out, norm_ffn2_g, ffn2_w_gate, ffn2_w_up, ffn2_w_down, loss_target, m_norm_ffn1_g, m_ffn1_w_gate, m_ffn1_w_up, m_ffn1_w_down, m_norm_mix_g, m_w_in, m_b_forget, m_fox_q_norm_g, m_fox_k_norm_g, m_swa_q_norm_g, m_swa_k_norm_g, m_swa_sinks, m_out_norm_fox_g, m_out_norm_swa_g, m_w_out, m_norm_ffn2_g, m_ffn2_w_gate, m_ffn2_w_up, m_ffn2_w_down, v_norm_ffn1_g, v_ffn1_w_gate, v_ffn1_w_up, v_ffn1_w_down, v_norm_mix_g, v_w_in, v_b_forget, v_fox_q_norm_g, v_fox_k_norm_g, v_swa_q_norm_g, v_swa_k_norm_g, v_swa_sinks, v_out_norm_fox_g, v_out_norm_swa_g, v_w_out, v_norm_ffn2_g, v_ffn2_w_gate, v_ffn2_w_up, v_ffn2_w_down):
    args = dict(locals())
    w = {k: args[k] for k in _ALL}
    m = {k: args["m_" + k] for k in _ALL}
    v = {k: args["v_" + k] for k in _ALL}
    c_idx = lax.axis_index("c").astype(jnp.int32).reshape(1)
    p_idx = (2 * lax.axis_index("x") + lax.axis_index("y")).astype(jnp.int32).reshape(1)
    pc_idx = jnp.concatenate([p_idx, c_idx])

    small = {k: w[k] for k in _SMALL}
    shards = {k: w[k][0] for k in _BIG}
    plan = WinPlan(x.shape[-1])
    shards["w_in"] = lax.switch(p_idx[0], [functools.partial(plan.place, s=s) for s in range(N_CHIPS)], shards["w_in"])
    loss, grad_x, g_shard, g_small = _local_step(shards, {k: w[k][0] for k in _SMALL}, x[0], positions[0], loss_target[0],
                                                 p_idx, c_idx, pc_idx)
    g_shard["w_in"] = lax.switch(p_idx[0], [functools.partial(plan.unplace, s=s) for s in range(N_CHIPS)], g_shard["w_in"])
    loss = lax.psum(loss, ("x", "y", "c"))
    g_small_sum = _unpack_small(all_reduce_small(_pack_small({k: g_small[k].reshape(1, -1) for k in _SMALL})), small)

    grad_w, delta, new_m, new_v = {}, {}, {}, {}
    for k in _BIG:
        (g, d, nm, nv), _ = adamw("adamw_" + k, w[k][0], g_shard[k], m[k][0], v[k][0])
        grad_w[k], delta[k], new_m[k], new_v[k] = g[None], d[None], nm[None], nv[None]
    (_, d, nm, nv), _ = adamw("adamw_small", _pack_small(small), _pack_small(g_small_sum), _pack_small({k: m[k] for k in _SMALL}),
                              _pack_small({k: v[k] for k in _SMALL}))
    grad_w.update(g_small_sum)
    delta.update(_unpack_small(d, small))
    new_m.update(_unpack_small(nm, small))
    new_v.update(_unpack_small(nv, small))

    return (loss, grad_x[None], *[grad_w[k] for k in _ALL], *[delta[k] for k in _ALL], *[new_m[k] for k in _ALL], *[new_v[k] for k in _ALL])
```

```python
import functools

import jax
import jax.numpy as jnp
from jax import lax
from jax.experimental import pallas as pl
from jax.experimental.pallas import tpu as pltpu

F32 = jnp.float32
BF16 = jnp.bfloat16

HEAD_DIM = 64
WINDOW = 128
ROPE_THETA = 10000.0
EPS = 1e-6
N_CHIPS = 4
N_DEV = 8

ADAM_LR = 0.001
ADAM_B1 = 0.9
ADAM_B2 = 0.999
ADAM_EPS = 1e-08
ADAM_WD = 0.01
ADAM_STEP = 10

V7X_VMEM_BYTES = 64 * 1024 * 1024
VMEM_LIMIT = V7X_VMEM_BYTES - 8 * 1024 * 1024
MASK_VALUE = -1e30

_MESH = pl.DeviceIdType.MESH
_HBM = pl.BlockSpec(memory_space=pl.ANY)
_DIMS = {"nn": (((1,), (0,)), ((), ())), "nt": (((1,), (1,)), ((), ())), "tn": (((0,), (0,)), ((), ()))}


def _pick(n, prefs):
    for p in prefs:
        if n % p == 0:
            return p
    return n


class Rider:
    def __init__(self, reads, aliased, news, nsem, build):
        self.reads, self.aliased, self.news, self.nsem, self.build = list(reads), list(aliased), list(news), nsem, build


class _Shifted:
    def __init__(self, ref, off):
        self.ref, self.off = ref, off

    @property
    def at(self):
        return self

    def __getitem__(self, k):
        return self.ref.at[k + self.off]


def combine(*riders):
    def build(reads, al, news, ssem, rsem):
        out = ([], [], [])
        r0 = a0 = n0 = s0 = 0
        for rd in riders:
            nr, na, nn = len(rd.reads), len(rd.aliased), len(rd.news)
            part = rd.build(reads[r0:r0 + nr], al[a0:a0 + na], news[n0:n0 + nn], _Shifted(ssem, s0), _Shifted(rsem, s0))
            for acc, lst in zip(out, part):
                acc.extend(lst)
            r0, a0, n0, s0 = r0 + nr, a0 + na, n0 + nn, s0 + rd.nsem
        return out

    return Rider(sum((r.reads for r in riders), []), sum((r.aliased for r in riders), []), sum((r.news for r in riders), []),
                 sum(r.nsem for r in riders), build)


def _me():
    return lax.axis_index("x"), lax.axis_index("y"), lax.axis_index("c")


def _other_chips(x, y):
    return [(1 - x, y), (x, 1 - y), (1 - x, 1 - y)]


WHOLE = (0, 1, 1)


def _rows(ref, start, rows, part=WHOLE):
    k0, k1, n = part
    assert rows % n == 0, (rows, part)
    idx = (slice(None),) * (len(ref.shape) - 2) + (pl.ds(start + k0 * (rows // n), (k1 - k0) * (rows // n)), slice(None))
    return ref.at[idx]


def _half(ref, h, part=WHOLE):
    rows = ref.shape[-2] // 2
    return _rows(ref, h * rows, rows, part)


def _remote(src, dst, ssem, rsem, k, to):
    return pltpu.make_async_remote_copy(src_ref=src, dst_ref=dst, send_sem=ssem.at[k], recv_sem=rsem.at[k], device_id=to,
                                        device_id_type=_MESH)


def _later(*args):
    return functools.partial(_remote, *args)


def gather(bufs, jobs):
    def build(reads, al, news, ssem, rsem):
        x, y, c = _me()
        p = 2 * x + y
        starts, arrivals = [], []
        for n, (b, kind, part) in enumerate(jobs):
            for j, chip in enumerate(_other_chips(x, y)):
                q = 2 * chip[0] + chip[1]
                if kind == "ici":
                    src, landing, to = _half(al[b].at[p], c, part), _half(al[b].at[q], c, part), (*chip, c)
                else:
                    src, landing, to = _half(al[b].at[q], c, part), _half(al[b].at[q], 1 - c, part), (x, y, 1 - c)
                starts.append(_later(src, src, ssem, rsem, 3 * n + j, to))
                arrivals.append(_later(landing, landing, ssem, rsem, 3 * n + j, to))
        return starts, arrivals, starts

    return Rider([], bufs, [], 3 * len(jobs), build)


def exchange_halves(grads):
    def build(reads, al, news, ssem, rsem):
        x, y, c = _me()
        cps = [_later(_half(g, 1 - c), t, ssem, rsem, w, (x, y, 1 - c)) for w, (g, t) in enumerate(zip(reads, news))]
        return cps, cps, cps

    return Rider(grads, [], [jax.ShapeDtypeStruct((g.shape[0], g.shape[1] // 2, g.shape[2]), g.dtype) for g in grads], len(grads), build)


def scatter_to_owner(sums, gots=None, part=WHOLE):
    def build(reads, al, news, ssem, rsem):
        x, y, c = _me()
        cps = []
        for w, (s, got) in enumerate(zip(reads, al or news)):
            rows = s.shape[-2]
            for j, chip in enumerate(_other_chips(x, y)):
                cps.append(_later(_rows(s.at[2 * chip[0] + chip[1]], 0, rows, part), _rows(got.at[j], 0, rows, part), ssem, rsem,
                                  3 * w + j, (*chip, c)))
        return cps, cps, cps

    news = [] if gots else [jax.ShapeDtypeStruct((3,) + s.shape[1:], s.dtype) for s in sums]
    return Rider(sums, gots or [], news, 3 * len(sums), build)


def join_halves(fulls):
    def build(reads, al, news, ssem, rsem):
        x, y, c = _me()
        starts, arrivals = [], []
        for w, f in enumerate(al):
            mine, landing = _half(f, c), _half(f, 1 - c)
            starts.append(_later(mine, mine, ssem, rsem, w, (x, y, 1 - c)))
            arrivals.append(_later(landing, landing, ssem, rsem, w, (x, y, 1 - c)))
        return starts, arrivals, starts

    return Rider([], fulls, [], len(fulls), build)


def _start_and_wait(rider, reads, al, news, ssem, rsem, first, last):
    @pl.when(first)
    def _():
        for cp in rider.build(reads, al, news, ssem, rsem)[0]:
            cp().start()

    def finish():
        @pl.when(last)
        def _():
            _, arrivals, sends = rider.build(reads, al, news, ssem, rsem)
            for cp in arrivals:
                cp().wait_recv()
            for cp in sends:
                cp().wait_send()

    return finish


def _call(name, body, grid, in_specs, args, out_specs, out_shape, scratch=(), semantics=None, rider=None, prefetch=()):
    n_pre, n_in, n_out, n_scr = len(prefetch), len(args), len(out_shape), len(scratch)
    nr, na, nn = (len(rider.reads), len(rider.aliased), len(rider.news)) if rider else (0, 0, 0)

    def wrapped(*refs):
        pre, refs = refs[:n_pre], refs[n_pre:]
        ins, reads = refs[:n_in], refs[n_in:n_in + nr]
        o0 = n_in + nr + na
        outs, al, news = refs[o0:o0 + n_out], refs[o0 + n_out:o0 + n_out + na], refs[o0 + n_out + na:o0 + n_out + na + nn]
        s0 = o0 + n_out + na + nn
        scr, (ssem, rsem) = refs[s0:s0 + n_scr], refs[s0 + n_scr:]
        first = functools.reduce(jnp.logical_and, [pl.program_id(a) == 0 for a in range(len(grid))])
        last = functools.reduce(jnp.logical_and, [pl.program_id(a) == g - 1 for a, g in enumerate(grid)])
        finish = _start_and_wait(rider, reads, al, news, ssem, rsem, first, last)
        body(*pre, *ins, *outs, *scr)
        finish()

    kernel_fn, all_in, all_out, shapes, scr = body, list(in_specs), list(out_specs), list(out_shape), list(scratch)
    operands, aliases = (*prefetch, *args), {}
    if rider:
        kernel_fn, semantics = wrapped, ("arbitrary",) * len(grid)
        all_in += [_HBM] * (nr + na)
        all_out += [_HBM] * (na + nn)
        shapes += [jax.ShapeDtypeStruct(a.shape, a.dtype) for a in rider.aliased] + rider.news
        scr += [pltpu.SemaphoreType.DMA((rider.nsem,)), pltpu.SemaphoreType.DMA((rider.nsem,))]
        operands += (*rider.reads, *rider.aliased)
        aliases = {n_pre + n_in + nr + i: n_out + i for i in range(na)}
    params = pltpu.CompilerParams(dimension_semantics=semantics, vmem_limit_bytes=VMEM_LIMIT)
    if n_pre:
        spec = pltpu.PrefetchScalarGridSpec(num_scalar_prefetch=n_pre, grid=grid, in_specs=all_in, out_specs=all_out, scratch_shapes=scr)
        outs = pl.pallas_call(kernel_fn, name=name, grid_spec=spec, out_shape=shapes, input_output_aliases=aliases, compiler_params=params)(*operands)
    else:
        outs = pl.pallas_call(kernel_fn, name=name, grid=grid, in_specs=all_in, out_specs=all_out, out_shape=shapes, scratch_shapes=scr,
                              input_output_aliases=aliases, compiler_params=params)(*operands)
    return list(outs[:n_out]), ((list(outs[n_out:n_out + na]), list(outs[n_out + na:])) if rider else None)


def run_step(name, rider):
    nr, na, nn = len(rider.reads), len(rider.aliased), len(rider.news)

    def body(*refs):
        reads = refs[:nr]
        al, news = refs[nr + na:nr + 2 * na], refs[nr + 2 * na:nr + 2 * na + nn]
        ssem, rsem = refs[nr + 2 * na + nn:]
        starts, arrivals, sends = rider.build(reads, al, news, ssem, rsem)
        for cp in starts:
            cp().start()
        for cp in arrivals:
            cp().wait_recv()
        for cp in sends:
            cp().wait_send()

    outs = pl.pallas_call(
        body, name=name, in_specs=[_HBM] * (nr + na), out_specs=[_HBM] * (na + nn),
        out_shape=[jax.ShapeDtypeStruct(a.shape, a.dtype) for a in rider.aliased] + rider.news,
        input_output_aliases={nr + i: i for i in range(na)},
        scratch_shapes=[pltpu.SemaphoreType.DMA((rider.nsem,)), pltpu.SemaphoreType.DMA((rider.nsem,))],
    )(*rider.reads, *rider.aliased)
    return list(outs[:na]), list(outs[na:])


def all_reduce_small(v):
    rows, lanes = v.shape

    def body(v_ref, o_ref, slots, send_sems, recv_sems):
        x, y, c = _me()
        me = 4 * x + 2 * y + c
        slots[me] = v_ref[...]
        cps = []
        for k in range(1, N_DEV):
            peer = (x ^ (k >> 2), y ^ ((k >> 1) & 1), c ^ (k & 1))
            cps.append(_remote(v_ref, slots.at[me], send_sems, recv_sems, k - 1, peer))
            cps[-1].start()
        for k in range(1, N_DEV):
            theirs = slots.at[me ^ k]
            _remote(theirs, theirs, send_sems, recv_sems, k - 1, (x, y, c)).wait_recv()
        for cp in cps:
            cp.wait_send()
        acc = slots[0]
        for i in range(1, N_DEV):
            acc = acc + slots[i]
        o_ref[...] = acc

    return pl.pallas_call(
        body, name="all_reduce_small",
        in_specs=[pl.BlockSpec(memory_space=pltpu.VMEM)], out_specs=pl.BlockSpec(memory_space=pltpu.VMEM),
        out_shape=jax.ShapeDtypeStruct((rows, lanes), F32),
        scratch_shapes=[pltpu.VMEM((N_DEV, rows, lanes), F32), pltpu.SemaphoreType.DMA((N_DEV - 1,)), pltpu.SemaphoreType.DMA((N_DEV - 1,))],
    )(v)


def _mm_call(name, mode, a, b, a_spec, b_spec, out_shape, out_spec, grid, acc_shape, scale=1.0, resid=None, resid_spec=None, rider=None):
    nk = grid[2]
    dims = _DIMS[mode]
    has_resid = resid is not None

    def body(*refs):
        a_ref, b_ref = refs[:2]
        r_ref = refs[2] if has_resid else None
        o_ref = refs[3] if has_resid else refs[2]

        def finish(r):
            if scale != 1.0:
                r = r * scale
            if has_resid:
                r = r_ref[...].astype(F32) + r
            o_ref[...] = r.astype(o_ref.dtype)

        part = lax.dot_general(a_ref[...].astype(BF16), b_ref[...].astype(BF16), dims, preferred_element_type=F32)
        if nk == 1:
            finish(part)
            return
        acc_ref = refs[-1]
        k = pl.program_id(2)

        @pl.when(k == 0)
        def _():
            acc_ref[...] = part

        @pl.when(k > 0)
        def _():
            acc_ref[...] += part

        @pl.when(k == nk - 1)
        def _():
            finish(acc_ref[...])

    in_specs = [a_spec, b_spec] + ([resid_spec] if has_resid else [])
    args = (a, b) + ((resid,) if has_resid else ())
    (out,), rid = _call(name, body, grid, in_specs, args, [out_spec], [out_shape], [pltpu.VMEM(acc_shape, F32)] if nk > 1 else [],
                        ("parallel", "parallel", "arbitrary"), rider)
    return out, rid


MM_VMEM_BUDGET = 40 * 1024 * 1024
_TILE_OPTS = (2048, 1408, 1024, 512, 256, 128)


def _tiles(m, n, kd, a_item, b_item, o_item, r_item=0, tm=None, tn=None, tk=None):
    def opts(full, fixed, cap):
        return [fixed] if fixed else [t for t in _TILE_OPTS if t <= cap and full % t == 0] or [full]

    best = None
    for cm in opts(m, tm, 1408):
        for cn in opts(n, tn, 1408):
            for ck in opts(kd, tk, 2048):
                blocks = cm * ck * a_item + ck * cn * b_item + cm * cn * (o_item + r_item)
                casts = (cm * ck * 2 if a_item == 4 else 0) + (ck * cn * 2 if b_item == 4 else 0)
                if 2 * blocks + cm * cn * 4 + casts <= MM_VMEM_BUDGET:
                    key = (cm * cn * ck, ck)
                    if best is None or key > best[0]:
                        best = (key, (cm, cn, ck))
    assert best is not None, (m, n, kd)
    return best[1]


def _item(x):
    return jnp.dtype(x.dtype).itemsize


def mm_nn(name, a, b, *, out_dtype=F32, scale=1.0, resid=None, rider=None):
    m, kd = a.shape
    n = b.shape[1]
    tm, tn, tk = _tiles(m, n, kd, _item(a), _item(b), jnp.dtype(out_dtype).itemsize, 0 if resid is None else _item(resid))
    o_spec = pl.BlockSpec((tm, tn), lambda i, j, k: (i, j))
    return _mm_call(
        name, "nn", a, b, pl.BlockSpec((tm, tk), lambda i, j, k: (i, k)), pl.BlockSpec((tk, tn), lambda i, j, k: (k, j)),
        jax.ShapeDtypeStruct((m, n), out_dtype), o_spec, (m // tm, n // tn, kd // tk), (tm, tn), scale, resid, o_spec, rider)


def mm_nt(name, a, b, *, out_dtype=F32, scale=1.0, resid=None, rider=None):
    m, kd = a.shape
    n = b.shape[0]
    tm, tn, tk = _tiles(m, n, kd, _item(a), _item(b), jnp.dtype(out_dtype).itemsize, 0 if resid is None else _item(resid))
    o_spec = pl.BlockSpec((tm, tn), lambda i, j, k: (i, j))
    return _mm_call(
        name, "nt", a, b, pl.BlockSpec((tm, tk), lambda i, j, k: (i, k)), pl.BlockSpec((tn, tk), lambda i, j, k: (j, k)),
        jax.ShapeDtypeStruct((m, n), out_dtype), o_spec, (m // tm, n // tn, kd // tk), (tm, tn), scale, resid, o_spec, rider)


def mm_tn(name, a, b, *, out_dtype=F32, scale=1.0, rider=None):
    kd, m = a.shape
    n = b.shape[1]
    tm, tn, tk = _tiles(m, n, kd, _item(a), _item(b), jnp.dtype(out_dtype).itemsize)
    return _mm_call(
        name, "tn", a, b, pl.BlockSpec((tk, tm), lambda i, j, k: (k, i)), pl.BlockSpec((tk, tn), lambda i, j, k: (k, j)),
        jax.ShapeDtypeStruct((m, n), out_dtype), pl.BlockSpec((tm, tn), lambda i, j, k: (i, j)),
        (m // tm, n // tn, kd // tk), (tm, tn), scale, rider=rider)


def mm_nt_sharded(name, a, w, *, resid=None, rider=None):
    m = a.shape[0]
    ns, n, c = w.shape
    tm, tn, _ = _tiles(m, n, c, _item(a), _item(w), 4, 0 if resid is None else _item(resid), tk=c)
    o_spec = pl.BlockSpec((tm, tn), lambda i, j, k: (i, j))
    return _mm_call(
        name, "nt", a, w, pl.BlockSpec((tm, c), lambda i, j, k: (i, k)), pl.BlockSpec((None, tn, c), lambda i, j, k: (k, j, 0)),
        jax.ShapeDtypeStruct((m, n), F32), o_spec, (m // tm, n // tn, ns), (tm, tn), 1.0, resid, o_spec, rider)


def mm_tn_sharded(name, a, b, ns, *, rider=None):
    kd, m = a.shape
    c = b.shape[1] // ns
    tm, _, tk = _tiles(m, c, kd, _item(a), _item(b), 2, tn=c)
    return _mm_call(
        name, "tn", a, b, pl.BlockSpec((tk, tm), lambda i, j, k: (k, i)), pl.BlockSpec((tk, c), lambda i, j, k: (k, j)),
        jax.ShapeDtypeStruct((ns, m, c), BF16), pl.BlockSpec((None, tm, c), lambda i, j, k: (j, i, 0)),
        (m // tm, ns, kd // tk), (tm, c), rider=rider)


def rms_fwd(name, x, g, out_dtype, rider=None):
    r, c = x.shape
    tm = _pick(r, (512, 256, 128, 64, 8))

    def body(x_ref, g_ref, y_ref, r_ref):
        xf = x_ref[...].astype(F32)
        rstd = lax.rsqrt(jnp.mean(xf * xf, axis=-1, keepdims=True) + EPS)
        y_ref[...] = ((xf * rstd) * g_ref[...]).astype(y_ref.dtype)
        r_ref[...] = rstd

    (y, rstd), rid = _call(
        name, body, (r // tm,), [pl.BlockSpec((tm, c), lambda i: (i, 0)), pl.BlockSpec((1, c), lambda i: (0, 0))], (x, g.reshape(1, c)),
        [pl.BlockSpec((tm, c), lambda i: (i, 0)), pl.BlockSpec((tm, 1), lambda i: (i, 0))],
        [jax.ShapeDtypeStruct((r, c), out_dtype), jax.ShapeDtypeStruct((r, 1), F32)], (), ("parallel",), rider)
    return (y, rstd) if rider is None else (y, rstd, rid)


def rms_bwd(name, x, g, rstd, dy, dres=None, rider=None, also_bf16=False):
    r, c = x.shape
    tm = _pick(r, (512, 256, 128, 64, 8))
    has_res = dres is not None

    def body(*refs):
        x_ref, g_ref, r_ref, dy_ref = refs[:4]
        dres_ref = refs[4] if has_res else None
        dx_ref, dg_ref = refs[4 + has_res:6 + has_res]
        xhat = x_ref[...].astype(F32) * r_ref[...]
        dyf = dy_ref[...].astype(F32)
        gdy = dyf * g_ref[...]
        dx = r_ref[...] * (gdy - xhat * jnp.mean(gdy * xhat, axis=-1, keepdims=True))
        if has_res:
            dx = dx + dres_ref[...]
        dx_ref[...] = dx
        if also_bf16:
            refs[-1][...] = dx.astype(BF16)

        @pl.when(pl.program_id(0) == 0)
        def _():
            dg_ref[...] = jnp.zeros_like(dg_ref)

        dg_ref[...] += jnp.sum(dyf * xhat, axis=0, keepdims=True)

    row = pl.BlockSpec((tm, c), lambda i: (i, 0))
    in_specs = [row, pl.BlockSpec((1, c), lambda i: (0, 0)), pl.BlockSpec((tm, 1), lambda i: (i, 0)), row] + ([row] if has_res else [])
    args = (x, g.reshape(1, c), rstd, dy) + ((dres,) if has_res else ())
    outs, rid = _call(name, body, (r // tm,), in_specs, args, [row, pl.BlockSpec((1, c), lambda i: (0, 0))] + [row] * also_bf16,
                      [jax.ShapeDtypeStruct((r, c), F32), jax.ShapeDtypeStruct((1, c), F32)] + [jax.ShapeDtypeStruct((r, c), BF16)] * also_bf16,
                      (), ("arbitrary",), rider)
    return (outs[0], outs[1].reshape(c), *outs[2:], *([] if rider is None else [rid]))


_LANES = 128


def _head_mean(v):
    if v.shape[1] == HEAD_DIM:
        return jnp.mean(v, axis=-1, keepdims=True)
    low = lax.broadcasted_iota(jnp.int32, v.shape, 1) < HEAD_DIM
    lo = jnp.sum(jnp.where(low, v, 0.0), axis=-1, keepdims=True)
    hi = jnp.sum(jnp.where(low, 0.0, v), axis=-1, keepdims=True)
    return jnp.where(low, lo, hi) * (1.0 / HEAD_DIM)


def _head_groups(c):
    width = _LANES if c % _LANES == 0 else HEAD_DIM
    assert c % width == 0, c
    return width, [slice(k * width, (k + 1) * width) for k in range(c // width)]


def _head_gain(g, width):
    return jnp.tile(g.reshape(1, HEAD_DIM), (1, width // HEAD_DIM))


def _rotate_half(y):
    half = HEAD_DIM // 2
    first = lax.broadcasted_iota(jnp.int32, y.shape, 1) % HEAD_DIM < half
    return jnp.where(first, -pltpu.roll(y, y.shape[1] - half, axis=1), pltpu.roll(y, half, axis=1))


def _rope_tables(rope, width):
    return [jnp.tile(t, (1, 2 * width // HEAD_DIM)) for t in rope]


def head_rms_fwd(name, x, g, rope=None):
    s, c = x.shape
    tm = _pick(s, (256, 128, 8))
    width, groups = _head_groups(c)

    def body(x_ref, g_ref, *refs):
        y_ref = refs[-1]
        for sl in groups:
            xs = x_ref[:, sl]
            y = (xs * lax.rsqrt(_head_mean(xs * xs) + EPS)) * g_ref[...]
            if rope:
                y = y * refs[0][...] + _rotate_half(y) * refs[1][...]
            y_ref[:, sl] = y

    row = pl.BlockSpec((tm, c), lambda i: (i, 0))
    tab = pl.BlockSpec((tm, width), lambda i: (i, 0))
    tables = _rope_tables(rope, width) if rope else []
    (y,), _ = _call(name, body, (s // tm,), [row, pl.BlockSpec((1, width), lambda i: (0, 0))] + [tab] * len(tables),
                    (x, _head_gain(g, width), *tables), [row], [jax.ShapeDtypeStruct((s, c), F32)], (), ("parallel",))
    return y


def head_rms_bwd(name, x, g, dy, rope=None):
    s, c = x.shape
    tm = _pick(s, (256, 128, 8))
    width, groups = _head_groups(c)

    def body(x_ref, g_ref, dy_ref, *refs):
        dx_ref, dg_ref = refs[-2:]

        @pl.when(pl.program_id(0) == 0)
        def _():
            dg_ref[...] = jnp.zeros_like(dg_ref)

        for sl in groups:
            xs, dys = x_ref[:, sl], dy_ref[:, sl]
            if rope:
                dys = dys * refs[0][...] - _rotate_half(dys * refs[1][...])
            rstd = lax.rsqrt(_head_mean(xs * xs) + EPS)
            xhat = xs * rstd
            gdy = dys * g_ref[...]
            dx_ref[:, sl] = rstd * (gdy - xhat * _head_mean(gdy * xhat))
            dg_ref[...] += jnp.sum(dys * xhat, axis=0, keepdims=True)

    row = pl.BlockSpec((tm, c), lambda i: (i, 0))
    vec = pl.BlockSpec((1, width), lambda i: (0, 0))
    tab = pl.BlockSpec((tm, width), lambda i: (i, 0))
    tables = _rope_tables(rope, width) if rope else []
    (dx, dg), _ = _call(name, body, (s // tm,), [row, vec, row] + [tab] * len(tables), (x, _head_gain(g, width), dy, *tables), [row, vec],
                        [jax.ShapeDtypeStruct((s, c), F32), jax.ShapeDtypeStruct((1, width), F32)], (), ("arbitrary",))
    return dx, jnp.sum(dg.reshape(width // HEAD_DIM, HEAD_DIM), axis=0)


@functools.partial(jax.custom_vjp, nondiff_argnums=(0,))
def head_rms(name, x, g):
    return head_rms_fwd(name + "_fwd", x, g)


def _head_rms_fwd(name, x, g):
    return head_rms_fwd(name + "_fwd", x, g), (x, g)


def _head_rms_bwd(name, res, dy):
    return head_rms_bwd(name + "_bwd", *res, dy)


head_rms.defvjp(_head_rms_fwd, _head_rms_bwd)


@functools.partial(jax.custom_vjp, nondiff_argnums=(0,))
def head_rms_rope(name, x, g, cos, sin):
    return head_rms_fwd(name + "_fwd", x, g, (cos, sin))


def _head_rms_rope_fwd(name, x, g, cos, sin):
    return head_rms_fwd(name + "_fwd", x, g, (cos, sin)), (x, g, cos, sin)


def _head_rms_rope_bwd(name, res, dy):
    x, g, cos, sin = res
    return (*head_rms_bwd(name + "_bwd", x, g, dy, (cos, sin)), jnp.zeros_like(cos), jnp.zeros_like(sin))


head_rms_rope.defvjp(_head_rms_rope_fwd, _head_rms_rope_bwd)


def _cumsum_call(name, a, reverse):
    h, s = a.shape
    tb = _LANES
    assert s % tb == 0

    def body(a_ref, o_ref):
        t_in = lax.broadcasted_iota(jnp.int32, (tb, tb), 0)
        t_out = lax.broadcasted_iota(jnp.int32, (tb, tb), 1)
        tri = jnp.where((t_in >= t_out) if reverse else (t_in <= t_out), 1.0, 0.0).astype(BF16)
        carry = jnp.zeros((h, 1), F32)
        blocks = range(s // tb)
        for b in (reversed(blocks) if reverse else blocks):
            cols = slice(b * tb, (b + 1) * tb)
            block = a_ref[:, cols]
            rest, local = block, jnp.zeros((h, tb), F32)
            for _ in range(3):
                piece = rest.astype(BF16)
                local = local + jnp.dot(piece, tri, preferred_element_type=F32)
                rest = rest - piece.astype(F32)
            o_ref[:, cols] = local + carry
            carry = carry + jnp.sum(block, axis=1, keepdims=True)

    whole = pl.BlockSpec((h, s), lambda j: (0, 0))
    (out,), _ = _call(name, body, (1,), [whole], (a,), [whole], [jax.ShapeDtypeStruct((h, s), F32)], (), ("arbitrary",))
    return out


@jax.custom_vjp
def time_cumsum(a):
    return _cumsum_call("gate_cumsum", a, False)


def _time_cumsum_fwd(a):
    return _cumsum_call("gate_cumsum", a, False), None


def _time_cumsum_bwd(_, dc):
    return (_cumsum_call("gate_cumsum_bwd", dc, True),)


time_cumsum.defvjp(_time_cumsum_fwd, _time_cumsum_bwd)


FFN_TM = 512


def _sigmoid(x):
    return 1.0 / (1.0 + jnp.exp(-x))


def ffn_gu(name, xn, wg, wu, rider=None):
    s, d = xn.shape
    ns, _, c = wg.shape
    tm = _pick(s, (FFN_TM, 128))

    def body(x_ref, wg_ref, wu_ref, h_ref, a_ref, b_ref):
        xb = x_ref[...]
        gv = jnp.dot(xb, wg_ref[...], preferred_element_type=F32)
        uv = jnp.dot(xb, wu_ref[...], preferred_element_type=F32)
        sig = _sigmoid(gv)
        silu = gv * sig
        h_ref[...] = (silu * uv).astype(BF16)
        a_ref[...] = (uv * (sig * (1.0 + gv * (1.0 - sig)))).astype(BF16)
        b_ref[...] = silu.astype(BF16)

    w_spec = pl.BlockSpec((None, d, c), lambda j, i: (j, 0, 0))
    o_spec = pl.BlockSpec((tm, c), lambda j, i: (i, j))
    return _call(
        name, body, (ns, s // tm), [pl.BlockSpec((tm, d), lambda j, i: (i, 0)), w_spec, w_spec], (xn, wg, wu),
        [o_spec, o_spec, o_spec], [jax.ShapeDtypeStruct((s, ns * c), BF16)] * 3, [], ("parallel", "parallel"), rider)


def ffn_dh(name, dy, wd, dh_dg, dh_du, ns, scale, rider=None):
    s, d = dy.shape
    f = wd.shape[0]
    c = f // ns
    tm = _pick(s, (FFN_TM, 128))

    def body(dy_ref, wd_ref, a_ref, b_ref, dg_ref, du_ref):
        dh = lax.dot_general(dy_ref[...].astype(BF16), wd_ref[...], _DIMS["nt"], preferred_element_type=F32) * scale
        dg_ref[...] = (dh * a_ref[...].astype(F32)).astype(BF16)
        du_ref[...] = (dh * b_ref[...].astype(F32)).astype(BF16)

    o_spec = pl.BlockSpec((tm, c), lambda j, i: (i, j))
    return _call(
        name, body, (ns, s // tm),
        [pl.BlockSpec((tm, d), lambda j, i: (i, 0)), pl.BlockSpec((c, d), lambda j, i: (j, 0)), o_spec, o_spec], (dy, wd, dh_dg, dh_du),
        [o_spec, o_spec], [jax.ShapeDtypeStruct((s, f), BF16), jax.ShapeDtypeStruct((s, f), BF16)],
        [], ("parallel", "parallel"), rider)


FOX_TQ = 512


def fox_tile(s_len):
    return min(FOX_TQ, s_len)


def _heads_per_block(h):
    return 2 if h % 2 == 0 else 1


def _fox_queries(q):
    return (q * (HEAD_DIM ** -0.5)).astype(BF16)


def _fox_scores(qs, kc, cq, ck, diagonal):
    s = lax.dot_general(qs, kc.astype(BF16), _DIMS["nt"], preferred_element_type=F32) + cq - ck
    if not diagonal:
        return s
    return jnp.where(lax.broadcasted_iota(jnp.int32, s.shape, 0) >= lax.broadcasted_iota(jnp.int32, s.shape, 1), s, MASK_VALUE)


def _fox_specs(h, s_len, tq):
    hb = _heads_per_block(h)
    qb = pl.BlockSpec((tq, hb * HEAD_DIM), lambda pp, i: (i, pp))
    kb = pl.BlockSpec((s_len, hb * HEAD_DIM), lambda pp, i: (0, pp))
    colb = pl.BlockSpec((hb, tq, 1), lambda pp, i: (pp, i, 0))
    rowb = pl.BlockSpec((hb, s_len // tq, 1, tq), lambda pp, i: (pp, 0, 0, 0))
    return hb, qb, kb, colb, rowb


def fox_fwd(q, k, v, cq, ck, rider=None):
    s_len, hd = q.shape
    h, d = hd // HEAD_DIM, HEAD_DIM
    tq = fox_tile(s_len)
    hb, qb, kb, colb, rowb = _fox_specs(h, s_len, tq)

    def body(q_ref, k_ref, v_ref, cq_ref, ck_ref, o_ref, lse_ref):
        i = pl.program_id(1)
        for hh in range(hb):
            lanes = slice(hh * d, (hh + 1) * d)
            qs, cqv = _fox_queries(q_ref[:, lanes]), cq_ref[hh]

            def chunk(c, carry, diagonal=False):
                m, l, acc = carry
                rows = pl.ds(pl.multiple_of(c * tq, tq), tq)
                s = _fox_scores(qs, k_ref[rows, lanes], cqv, ck_ref[hh, c], diagonal)
                m_new = jnp.maximum(m, jnp.max(s, axis=-1, keepdims=True))
                alpha = jnp.exp(m - m_new)
                p = jnp.exp(s - m_new)
                acc = alpha * acc + jnp.dot(p.astype(BF16), v_ref[rows, lanes].astype(BF16), preferred_element_type=F32)
                return m_new, alpha * l + jnp.sum(p, axis=-1, keepdims=True), acc

            init = (jnp.full((tq, 1), MASK_VALUE, F32), jnp.zeros((tq, 1), F32), jnp.zeros((tq, d), F32))
            m, l, acc = chunk(i, lax.fori_loop(0, i, chunk, init), diagonal=True)
            o_ref[:, lanes] = acc / l
            lse_ref[hh] = m + jnp.log(l)

    return _call(
        "fox_fwd", body, (h // hb, s_len // tq), [qb, kb, kb, colb, rowb], (q, k, v, cq, ck), [qb, colb],
        [jax.ShapeDtypeStruct((s_len, hd), F32), jax.ShapeDtypeStruct((h, s_len, 1), F32)], (), ("parallel", "parallel"), rider)


def fox_bwd(q, k, v, cq, ck, o, lse, do, rider=None):
    s_len, hd = q.shape
    h, d = hd // HEAD_DIM, HEAD_DIM
    tq = fox_tile(s_len)
    scale = HEAD_DIM ** -0.5
    hb, qb, kb, colb, rowb = _fox_specs(h, s_len, tq)

    def body(q_ref, k_ref, v_ref, cq_ref, ck_ref, o_ref, lse_ref, do_ref, dq_ref, dk_ref, dv_ref, dcq_ref, dck_ref):
        i = pl.program_id(1)

        @pl.when(i == 0)
        def _():
            dk_ref[...] = jnp.zeros_like(dk_ref)
            dv_ref[...] = jnp.zeros_like(dv_ref)
            dck_ref[...] = jnp.zeros_like(dck_ref)

        heads = []
        for hh in range(hb):
            lanes = slice(hh * d, (hh + 1) * d)
            dof = do_ref[:, lanes]
            heads.append((lanes, _fox_queries(q_ref[:, lanes]), cq_ref[hh], lse_ref[hh], dof.astype(BF16),
                          jnp.sum(dof * o_ref[:, lanes], axis=-1, keepdims=True)))

        def chunk(c, carry, diagonal=False):
            rows = pl.ds(pl.multiple_of(c * tq, tq), tq)
            out, dks, dvs = [], [], []
            for hh, (lanes, qs, cqv, lse_h, dob, delta) in enumerate(heads):
                dq, dcq = carry[hh]
                kc = k_ref[rows, lanes]
                p = jnp.exp(_fox_scores(qs, kc, cqv, ck_ref[hh, c], diagonal) - lse_h)
                dp = lax.dot_general(dob, v_ref[rows, lanes].astype(BF16), _DIMS["nt"], preferred_element_type=F32)
                ds = p * (dp - delta)
                dsb = ds.astype(BF16)
                dvs.append(lax.dot_general(p.astype(BF16), dob, _DIMS["tn"], preferred_element_type=F32))
                dks.append(lax.dot_general(dsb, qs, _DIMS["tn"], preferred_element_type=F32))
                dck_ref[hh, c] -= jnp.sum(ds, axis=0, keepdims=True)
                out.append((dq + jnp.dot(dsb, kc.astype(BF16), preferred_element_type=F32), dcq + jnp.sum(ds, axis=-1, keepdims=True)))
            dk_ref[rows, :] += jnp.concatenate(dks, axis=1)
            dv_ref[rows, :] += jnp.concatenate(dvs, axis=1)
            return tuple(out)

        init = tuple((jnp.zeros((tq, d), F32), jnp.zeros((tq, 1), F32)) for _ in range(hb))
        done = chunk(i, lax.fori_loop(0, i, chunk, init), diagonal=True)
        dq_ref[...] = jnp.concatenate([dq for dq, _ in done], axis=1) * scale
        for hh, (_, dcq) in enumerate(done):
            dcq_ref[hh] = dcq

    return _call(
        "fox_bwd", body, (h // hb, s_len // tq), [qb, kb, kb, colb, rowb, qb, colb, qb], (q, k, v, cq, ck, o, lse, do),
        [qb, kb, kb, colb, rowb],
        [jax.ShapeDtypeStruct((s_len, hd), F32)] * 3
        + [jax.ShapeDtypeStruct((h, s_len, 1), F32), jax.ShapeDtypeStruct((h, s_len // tq, 1, tq), F32)],
        (), ("parallel", "arbitrary"), rider)


def _stack_heads(ref, first, g):
    return jnp.concatenate([ref[:, (first + j) * HEAD_DIM:(first + j + 1) * HEAD_DIM] for j in range(g)], axis=0)


def _window(prev_ref, cur_ref, hh):
    lanes = slice(hh * HEAD_DIM, (hh + 1) * HEAD_DIM)
    return jnp.concatenate([prev_ref[:, lanes], cur_ref[:, lanes]], axis=0).astype(BF16)


def _swa_band(g, w):
    t = lax.broadcasted_iota(jnp.int32, (g * w, 2 * w), 0) % w
    col = lax.broadcasted_iota(jnp.int32, (g * w, 2 * w), 1)
    rel = t + w - col
    band = (rel >= 0) & (rel < w)
    return jnp.where(jnp.stack([band & (col >= w), band]), 0.0, MASK_VALUE).astype(F32)


def _swa_probs(qs, kw, sink, band):
    s = lax.dot_general(qs, kw, _DIMS["nt"], preferred_element_type=F32) + band
    m = jnp.maximum(jnp.max(s, axis=-1, keepdims=True), sink)
    p = jnp.exp(s - m)
    ps = jnp.exp(sink - m)
    linv = 1.0 / (jnp.sum(p, axis=-1, keepdims=True) + ps)
    return p * linv, ps * linv


def _swa_specs(hk, g, s_len):
    w = WINDOW
    assert s_len % w == 0
    hb = _heads_per_block(hk)
    qb = pl.BlockSpec((w, hb * g * HEAD_DIM), lambda pp, n: (n, pp))
    prev = pl.BlockSpec((w, hb * HEAD_DIM), lambda pp, n: (jnp.maximum(n - 1, 0), pp))
    cur = pl.BlockSpec((w, hb * HEAD_DIM), lambda pp, n: (n, pp))
    sb = pl.BlockSpec((hb, g * w, 1), lambda pp, n: (pp, 0, 0))
    band = pl.BlockSpec((None, g * w, 2 * w), lambda pp, n: (jnp.minimum(n, 1), 0, 0))
    return hb, qb, prev, cur, sb, band


def swa_fwd(q, k, v, sink, rider=None):
    s_len = q.shape[0]
    hk = k.shape[1] // HEAD_DIM
    g = q.shape[1] // k.shape[1]
    w, d = WINDOW, HEAD_DIM
    hb, qb, prev, cur, sb, bandb = _swa_specs(hk, g, s_len)

    def body(q_ref, kp_ref, kc_ref, vp_ref, vc_ref, sink_ref, band_ref, o_ref):
        for hh in range(hb):
            qs = (_stack_heads(q_ref, hh * g, g) * (HEAD_DIM ** -0.5)).astype(BF16)
            p, _ = _swa_probs(qs, _window(kp_ref, kc_ref, hh), sink_ref[hh], band_ref[...])
            o = jnp.dot(p.astype(BF16), _window(vp_ref, vc_ref, hh), preferred_element_type=F32)
            for j in range(g):
                o_ref[:, (hh * g + j) * d:(hh * g + j + 1) * d] = o[j * w:(j + 1) * w]

    (o,), rid = _call("swa_fwd", body, (hk // hb, s_len // w), [qb, prev, cur, prev, cur, sb, bandb],
                      (q, k, k, v, v, sink, _swa_band(g, w)), [qb], [jax.ShapeDtypeStruct(q.shape, F32)], (),
                      ("parallel", "parallel"), rider)
    return o, rid


def swa_bwd(q, k, v, sink, o, do, rider=None):
    s_len = q.shape[0]
    hk = k.shape[1] // HEAD_DIM
    g = q.shape[1] // k.shape[1]
    w, d = WINDOW, HEAD_DIM
    scale = HEAD_DIM ** -0.5
    hb, qb, prev, cur, sb, bandb = _swa_specs(hk, g, s_len)

    def body(q_ref, kp_ref, kc_ref, vp_ref, vc_ref, sink_ref, band_ref, o_ref, do_ref, dq_ref, dkp_ref, dkc_ref, dvp_ref, dvc_ref,
             dsink_ref):
        @pl.when(pl.program_id(1) == 0)
        def _():
            dsink_ref[...] = jnp.zeros_like(dsink_ref)

        for hh in range(hb):
            lanes = slice(hh * d, (hh + 1) * d)
            qs = (_stack_heads(q_ref, hh * g, g) * scale).astype(BF16)
            kw, vw = _window(kp_ref, kc_ref, hh), _window(vp_ref, vc_ref, hh)
            p, ps = _swa_probs(qs, kw, sink_ref[hh], band_ref[...])
            dof = _stack_heads(do_ref, hh * g, g)
            dob = dof.astype(BF16)
            delta = jnp.sum(dof * _stack_heads(o_ref, hh * g, g), axis=-1, keepdims=True)
            dp = lax.dot_general(dob, vw, _DIMS["nt"], preferred_element_type=F32)
            ds = p * (dp - delta)
            dsb = ds.astype(BF16)
            dsink_ref[hh] -= ps * delta
            dq = jnp.dot(dsb, kw, preferred_element_type=F32) * scale
            for j in range(g):
                dq_ref[:, (hh * g + j) * d:(hh * g + j + 1) * d] = dq[j * w:(j + 1) * w]
            dkw = lax.dot_general(dsb, qs, _DIMS["tn"], preferred_element_type=F32)
            dvw = lax.dot_general(p.astype(BF16), dob, _DIMS["tn"], preferred_element_type=F32)
            dkp_ref[:, lanes] = dkw[:w]
            dkc_ref[:, lanes] = dkw[w:]
            dvp_ref[:, lanes] = dvw[:w]
            dvc_ref[:, lanes] = dvw[w:]

    kv_shape = jax.ShapeDtypeStruct(k.shape, F32)
    (dq, dkp, dkc, dvp, dvc, dsink), rid = _call(
        "swa_bwd", body, (hk // hb, s_len // w), [qb, prev, cur, prev, cur, sb, bandb, qb, qb],
        (q, k, k, v, v, sink, _swa_band(g, w), o, do),
        [qb, cur, cur, cur, cur, sb],
        [jax.ShapeDtypeStruct(q.shape, F32), kv_shape, kv_shape, kv_shape, kv_shape, jax.ShapeDtypeStruct((hk, g * w, 1), F32)],
        (), ("parallel", "arbitrary"), rider)

    def shift_up(a):
        return jnp.concatenate([a[w:], jnp.zeros_like(a[:w])], axis=0)

    return (dq, dkc + shift_up(dkp), dvc + shift_up(dvp), dsink), rid


def loss_call(y, target):
    s, d = y.shape
    tm = _pick(s, (512, 256, 128))

    def body(y_ref, t_ref, l_ref, dy_ref, dyb_ref):
        e = y_ref[...] - t_ref[...]
        dy = e * (1.0 / d)
        dy_ref[...] = dy
        dyb_ref[...] = dy.astype(BF16)

        @pl.when(pl.program_id(0) == 0)
        def _():
            l_ref[...] = jnp.zeros_like(l_ref)

        l_ref[...] += jnp.sum(jnp.sum(e * e, axis=0, keepdims=True), axis=1, keepdims=True) * (0.5 / d)

    row = pl.BlockSpec((tm, d), lambda i: (i, 0))
    (l, dy, dyb), _ = _call("loss_head", body, (s // tm,), [row, row], (y, target), [pl.BlockSpec((1, 1), lambda i: (0, 0)), row, row],
                            [jax.ShapeDtypeStruct((1, 1), F32), jax.ShapeDtypeStruct((s, d), F32), jax.ShapeDtypeStruct((s, d), BF16)],
                            (), ("arbitrary",))
    return l[0, 0], dy, dyb


def _row_tile(rows, cols, itemsize, block_bytes=1 << 20):
    target = max(16, block_bytes // (cols * itemsize))
    fits = [t for t in range(16, rows + 1, 16) if rows % t == 0 and t <= target]
    return fits[-1] if fits else rows


CAST_STEPS = 8


def cast_place(name, ws, p_idx, rider=None):
    n = len(ws)
    assert all(w.shape[0] % (16 * CAST_STEPS) == 0 for w in ws), [w.shape for w in ws]

    def body(p_ref, *refs):
        for w_ref, o_ref in zip(refs[:n], refs[n:]):
            o_ref[...] = w_ref[...].astype(BF16)

    return _call(
        name, body, (CAST_STEPS,), [pl.BlockSpec((w.shape[0] // CAST_STEPS, w.shape[1]), lambda i, pr: (i, 0)) for w in ws], tuple(ws),
        [pl.BlockSpec((None, w.shape[0] // CAST_STEPS, w.shape[1]), lambda i, pr: (pr[0], i, 0)) for w in ws],
        [jax.ShapeDtypeStruct((N_CHIPS,) + w.shape, BF16) for w in ws], (), ("parallel",), rider, prefetch=(p_idx,))


def chip_sum(name, grad, theirs, c_idx):
    ns, r, cols = grad.shape
    rh = r // 2
    tr = _row_tile(rh, cols, 2, 2 << 20)
    nb = rh // tr

    def body(c_ref, a_ref, b_ref, o_ref):
        o_ref[...] = (a_ref[...].astype(F32) + b_ref[...].astype(F32)).astype(o_ref.dtype)

    return pl.pallas_call(
        body, name=name,
        grid_spec=pltpu.PrefetchScalarGridSpec(
            num_scalar_prefetch=1, grid=(ns, nb),
            in_specs=[pl.BlockSpec((None, tr, cols), lambda q, i, cr: (q, cr[0] * nb + i, 0)),
                      pl.BlockSpec((None, tr, cols), lambda q, i, cr: (q, i, 0))],
            out_specs=pl.BlockSpec((None, tr, cols), lambda q, i, cr: (q, i, 0))),
        out_shape=jax.ShapeDtypeStruct((ns, rh, cols), BF16),
        compiler_params=pltpu.CompilerParams(dimension_semantics=("parallel", "parallel"), vmem_limit_bytes=VMEM_LIMIT),
    )(c_idx, grad, theirs)


def owner_sum(name, sums, got, pc_idx):
    ns, rh, cols = sums.shape
    tr = _row_tile(rh, cols, 4, 2 << 20)
    nb = rh // tr

    def body(pc_ref, a_ref, b_ref, o_ref):
        o_ref[...] = ((a_ref[...].astype(F32) + b_ref[0].astype(F32)) + b_ref[1].astype(F32)) + b_ref[2].astype(F32)

    return pl.pallas_call(
        body, name=name,
        grid_spec=pltpu.PrefetchScalarGridSpec(
            num_scalar_prefetch=1, grid=(nb,),
            in_specs=[pl.BlockSpec((None, tr, cols), lambda i, pc: (pc[0], i, 0)),
                      pl.BlockSpec((3, tr, cols), lambda i, pc: (0, i, 0))],
            out_specs=pl.BlockSpec((tr, cols), lambda i, pc: (pc[1] * nb + i, 0))),
        out_shape=jax.ShapeDtypeStruct((2 * rh, cols), F32),
        compiler_params=pltpu.CompilerParams(dimension_semantics=("parallel",), vmem_limit_bytes=VMEM_LIMIT),
    )(pc_idx, sums, got)


def adamw(name, w, g, m, v):
    r, cols = w.shape
    tr = _row_tile(r, cols, 4)
    c1 = 1.0 / (1.0 - ADAM_B1 ** ADAM_STEP)
    c2 = 1.0 / (1.0 - ADAM_B2 ** ADAM_STEP)

    def body(w_ref, g_ref, m_ref, v_ref, go_ref, d_ref, nm_ref, nv_ref):
        gv = g_ref[...]
        nm = ADAM_B1 * m_ref[...] + (1.0 - ADAM_B1) * gv
        nv = ADAM_B2 * v_ref[...] + (1.0 - ADAM_B2) * (gv * gv)
        go_ref[...] = gv
        d_ref[...] = -ADAM_LR * ((nm * c1) / (jnp.sqrt(nv * c2) + ADAM_EPS) + ADAM_WD * w_ref[...])
        nm_ref[...] = nm
        nv_ref[...] = nv

    blk = pl.BlockSpec((tr, cols), lambda i: (i, 0))
    return _call(name, body, (r // tr,), [blk] * 4, (w, g, m, v), [blk] * 4, [jax.ShapeDtypeStruct((r, cols), F32)] * 4, (), ("parallel",))


def _win_layout(d_model):
    hf = hq = d_model // (2 * HEAD_DIM)
    hk = hq // 4
    sizes = [hf * HEAD_DIM, hf * HEAD_DIM, hf * HEAD_DIM, hf, hq * HEAD_DIM, hk * HEAD_DIM, hk * HEAD_DIM]
    return hf, hq, hk, sizes


class WinPlan:
    def __init__(self, d_model, ns=N_CHIPS):
        self.hf, self.hq, self.hk, self.sizes = _win_layout(d_model)
        self.ns, self.cs = ns, sum(self.sizes) // ns
        self.jump_at = sum(self.sizes[:4])
        self.jump_by = -self.jump_at % _LANES
        self.base = [self.pos(s * self.cs) // _LANES * _LANES for s in range(ns)]
        ends = [self.pos((s + 1) * self.cs - 1) + 1 - self.base[s] for s in range(ns)]
        self.width = -(-max(ends) // _LANES) * _LANES
        self.total = -(-max(b + self.width for b in self.base) // 1024) * 1024
        starts = [0]
        for sz in self.sizes:
            starts.append(starts[-1] + sz)
        self.segments = [(self.pos(a), sz) for a, sz in zip(starts, self.sizes)]

    def pos(self, g):
        return g if g < self.jump_at else g + self.jump_by

    def pieces(self, s):
        g0, g1 = s * self.cs, (s + 1) * self.cs
        cuts = [g0] + ([self.jump_at] if g0 < self.jump_at < g1 else []) + [g1]
        return [(a - g0, b - a, self.pos(a) - self.base[s]) for a, b in zip(cuts[:-1], cuts[1:])]

    def place(self, w, s):
        parts, at = [], 0
        for t0, n, j0 in self.pieces(s):
            parts += [jnp.zeros((w.shape[0], j0 - at), w.dtype), w[:, t0:t0 + n]]
            at = j0 + n
        return jnp.concatenate(parts + [jnp.zeros((w.shape[0], self.width - at), w.dtype)], axis=1)

    def unplace(self, slab, s):
        return jnp.concatenate([slab[:, j0:j0 + n] for _, n, j0 in self.pieces(s)], axis=1)

    def assemble(self, slabs):
        return sum(jnp.pad(slabs[s], ((0, 0), (b, self.total - b - self.width))) for s, b in enumerate(self.base))

    def split(self, full):
        return jnp.stack([full[:, b:b + self.width] for b in self.base])


def _attn_inputs(proj, sm, positions):
    s_len = proj.shape[0]
    plan = WinPlan(sm["norm_mix_g"].shape[0])
    hf, hq, hk = plan.hf, plan.hq, plan.hk
    grp = hq // hk
    q_f, k_f, v_f, f_logit, q_s, k_s, v_s = [proj[:, a:a + n] for a, n in plan.segments]

    q_f = head_rms("fox_qnorm", q_f, sm["fox_q_norm_g"])
    k_f = head_rms("fox_knorm", k_f, sm["fox_k_norm_g"])
    log_f = jax.nn.log_sigmoid(f_logit + sm["b_forget"])
    c = time_cumsum(log_f.T)

    inv_freq = ROPE_THETA ** (-jnp.arange(0, HEAD_DIM, 2, dtype=F32) / HEAD_DIM)
    ang = positions.astype(F32)[:, None] * inv_freq
    cos, sin = jnp.cos(ang), jnp.sin(ang)
    q_s = head_rms_rope("swa_qnorm", q_s, sm["swa_q_norm_g"], cos, sin)
    k_s = head_rms_rope("swa_knorm", k_s, sm["swa_k_norm_g"], cos, sin)
    sink = jnp.broadcast_to(sm["swa_sinks"].reshape(hk, grp, 1, 1), (hk, grp, WINDOW, 1)).reshape(hk, grp * WINDOW, 1)
    tq = fox_tile(s_len)
    return (q_f, k_f, v_f, c[:, :, None], c.reshape(hf, s_len // tq, 1, tq)), (q_s, k_s, v_s, sink)


_BIG = ("ffn1_w_gate", "ffn1_w_up", "ffn1_w_down", "w_in", "w_out", "ffn2_w_gate", "ffn2_w_up", "ffn2_w_down")
_SMALL = ("norm_ffn1_g", "norm_mix_g", "b_forget", "fox_q_norm_g", "fox_k_norm_g", "swa_q_norm_g", "swa_k_norm_g", "swa_sinks",
          "out_norm_fox_g", "out_norm_swa_g", "norm_ffn2_g")
_ATTN_SMALL = ("norm_mix_g", "b_forget", "fox_q_norm_g", "fox_k_norm_g", "swa_q_norm_g", "swa_k_norm_g", "swa_sinks")
_ALL = ("norm_ffn1_g", "ffn1_w_gate", "ffn1_w_up", "ffn1_w_down", "norm_mix_g", "w_in", "b_forget", "fox_q_norm_g", "fox_k_norm_g",
        "swa_q_norm_g", "swa_k_norm_g", "swa_sinks", "out_norm_fox_g", "out_norm_swa_g", "w_out", "norm_ffn2_g", "ffn2_w_gate",
        "ffn2_w_up", "ffn2_w_down")


def _pack_small(d):
    parts = []
    for k in _SMALL:
        v = d[k].reshape(-1)
        rows = -(-v.shape[0] // _LANES)
        parts.append(jnp.pad(v, (0, rows * _LANES - v.shape[0])).reshape(rows, _LANES))
    a = jnp.concatenate(parts, axis=0)
    return jnp.pad(a, ((0, -a.shape[0] % 8), (0, 0)))


def _unpack_small(a, like):
    out, r0 = {}, 0
    for k in _SMALL:
        nvals = like[k].shape[1]
        rows = -(-nvals // _LANES)
        out[k] = a[r0:r0 + rows].reshape(-1)[:nvals].reshape(1, nvals)
        r0 += rows
    return out


def _stacked(w):
    return w.reshape(-1, w.shape[-1])


def _local_step(shards, sm, x, positions, target, p_idx, c_idx, pc_idx):
    ns = N_CHIPS
    full = {}

    def fetch(*jobs):
        names = list(dict.fromkeys(n for n, _, _ in jobs))
        return names, gather([bufs[n] for n in names], [(names.index(n), kind, part) for n, kind, part in jobs])

    def take(names, rid):
        for n, b in zip(names, rid[0]):
            bufs[n] = b

    n1 = ["ffn1_w_gate", "ffn1_w_up", "ffn1_w_down"]
    n2 = ["ffn2_w_gate", "ffn2_w_up", "ffn2_w_down"]
    later = ["w_in", "w_out"] + n2
    placed, _ = cast_place("cast_place_ffn1", [shards[n] for n in n1], p_idx)
    bufs = dict(zip(n1, placed))
    gate1, up1, down1 = n1
    gate2, up2, down2 = n2
    names, rider = fetch((gate1, "ici", WHOLE), (up1, "ici", WHOLE))
    placed, rid = cast_place("cast_place_later", [shards[n] for n in later], p_idx, rider=rider)
    bufs.update(zip(later, placed))
    take(names, rid)
    names, rider = fetch((gate1, "d2d", WHOLE), (up1, "d2d", WHOLE), (down1, "ici", (0, 1, 4)))
    xn1, r1, rid = rms_fwd("ffn1_norm", x, sm["norm_ffn1_g"], BF16, rider=rider)
    take(names, rid)
    names, rider = fetch((down1, "ici", (1, 4, 4)), ("w_in", "ici", (0, 1, 4)))
    (hid1, hdg1, hdu1), rid = ffn_gu("ffn1_gu", xn1, bufs[gate1], bufs[up1], rider=rider)
    take(names, rid)
    names, rider = fetch((down1, "d2d", WHOLE))
    take(names, run_step("gather_d2d_ffn1_down", rider))
    wd1 = _stacked(bufs[down1])
    names, rider = fetch(("w_in", "ici", (1, 4, 4)))
    h1, rid = mm_nn("ffn1_down", hid1, wd1, scale=0.5, resid=x, rider=rider)
    take(names, rid)

    names, rider = fetch(("w_in", "d2d", WHOLE))
    u, r_mix, rid = rms_fwd("mix_norm", h1, sm["norm_mix_g"], BF16, rider=rider)
    take(names, rid)
    names, rider = fetch(("w_out", "ici", WHOLE))
    plan = WinPlan(x.shape[1])
    win = plan.assemble(bufs["w_in"])
    proj, rid = mm_nn("mix_inproj", u, win, rider=rider)
    take(names, rid)
    sm_attn = {k: sm[k] for k in _ATTN_SMALL}
    (fox_in, swa_in), attn_vjp = jax.vjp(lambda pr, s: _attn_inputs(pr, s, positions), proj, sm_attn)
    names, rider = fetch((gate2, "ici", WHOLE), (up2, "ici", (0, 1, 4)), ("w_out", "d2d", WHOLE))
    (o_f, lse), rid = fox_fwd(*fox_in, rider=rider)
    take(names, rid)
    names, rider = fetch((up2, "ici", (1, 4, 4)), (gate2, "d2d", WHOLE), (up2, "d2d", (0, 1, 4)))
    o_s, rid = swa_fwd(*swa_in, rider=rider)
    take(names, rid)
    o_fox, o_swa = o_f, o_s
    nf, r_fox = rms_fwd("out_norm_fox", o_fox, sm["out_norm_fox_g"], BF16)
    nsw, r_swa = rms_fwd("out_norm_swa", o_swa, sm["out_norm_swa_g"], BF16)
    o = jnp.concatenate([nf, nsw], axis=-1)
    wout = _stacked(bufs["w_out"])
    names, rider = fetch((down2, "ici", (0, 1, 4)), (up2, "d2d", (1, 4, 4)))
    h2, rid = mm_nn("out_proj", o, wout, resid=h1, rider=rider)
    take(names, rid)

    xn2, r2 = rms_fwd("ffn2_norm", h2, sm["norm_ffn2_g"], BF16)
    names, rider = fetch((down2, "ici", (1, 4, 4)))
    (hid2, hdg2, hdu2), rid = ffn_gu("ffn2_gu", xn2, bufs[gate2], bufs[up2], rider=rider)
    take(names, rid)
    names, rider = fetch((down2, "d2d", WHOLE))
    take(names, run_step("gather_d2d_ffn2_down", rider))
    wd2 = _stacked(bufs["ffn2_w_down"])
    y, _ = mm_nn("ffn2_down", hid2, wd2, scale=0.5, resid=h2)
    loss, dy, dy_b = loss_call(y, target)

    red = {}

    def grad(n, g):
        red[n] = {"grad": g.reshape(ns, -1, g.shape[-1])}

    def ride(*steps):
        def done(rid):
            a0 = n0 = 0
            for rd, cb in steps:
                cb(rid[0][a0:a0 + len(rd.aliased)], rid[1][n0:n0 + len(rd.news)])
                a0, n0 = a0 + len(rd.aliased), n0 + len(rd.news)

        return (combine(*[s[0] for s in steps]) if len(steps) > 1 else steps[0][0]), done

    def xchg(*names):
        def cb(al, news):
            for n, t in zip(names, news):
                red[n]["sum"] = chip_sum("chip_sum_" + n, red[n]["grad"], t, c_idx)

        return exchange_halves([red[n]["grad"] for n in names]), cb

    def scat(n, part=WHOLE):
        def cb(al, news):
            red[n]["got"] = (al or news)[0]

        return scatter_to_owner([red[n]["sum"]], [red[n]["got"]] if "got" in red[n] else None, part), cb

    def own(n):
        red[n]["half"] = owner_sum("owner_sum_" + n, red[n]["sum"], red[n]["got"], pc_idx)

    def join(*names):
        return join_halves([red[n]["half"] for n in names]), lambda al, news: full.update(zip(names, al))

    dwd2, _ = mm_tn("ffn2_dwd", hid2, dy_b, out_dtype=BF16, scale=0.5)
    grad(down2, dwd2)
    rider, done = ride(xchg(down2))
    (dg2, du2), rid = ffn_dh("ffn2_dh", dy_b, wd2, hdg2, hdu2, ns, 0.5, rider=rider)
    done(rid)
    rider, done = ride(scat(down2, (0, 1, 2)))
    dwg2, rid = mm_tn_sharded("ffn2_dwg", xn2, dg2, ns, rider=rider)
    done(rid)
    grad(gate2, dwg2)
    rider, done = ride(scat(down2, (1, 2, 2)), xchg(gate2))
    dwu2, rid = mm_tn_sharded("ffn2_dwu", xn2, du2, ns, rider=rider)
    done(rid)
    grad(up2, dwu2)
    rider, done = ride(scat(gate2, (0, 1, 2)), xchg(up2))
    dxn, rid = mm_nt_sharded("ffn2_dxn_g", dg2, bufs[gate2], rider=rider)
    done(rid)
    rider, done = ride(scat(gate2, (1, 2, 2)))
    dxn, rid = mm_nt_sharded("ffn2_dxn_u", du2, bufs[up2], resid=dxn, rider=rider)
    done(rid)
    dh2, dgain_ffn2, dh2_b = rms_bwd("ffn2_dnorm", h2, sm["norm_ffn2_g"], r2, dxn, dres=dy, also_bf16=True)
    own(down2)
    own(gate2)

    do, _ = mm_nt("out_do", dh2_b, wout)
    dwout, _ = mm_tn("out_dw", o, dh2_b, out_dtype=BF16)
    cf = o_fox.shape[1]
    d_fox, dgain_fox = rms_bwd("out_dnorm_fox", o_fox, sm["out_norm_fox_g"], r_fox, do[:, :cf])
    d_swa, dgain_swa = rms_bwd("out_dnorm_swa", o_swa, sm["out_norm_swa_g"], r_swa, do[:, cf:])
    grad("w_out", dwout)
    rider, done = ride(scat(up2))
    swa_cts, rid = swa_bwd(*swa_in, o_s, d_swa, rider=rider)
    done(rid)
    own(up2)
    rider, done = ride(xchg("w_out"), join(down2, gate2, up2))
    fox_cts, rid = fox_bwd(*fox_in, o_f, lse, d_fox, rider=rider)
    done(rid)
    dproj, dsm_attn = attn_vjp((tuple(fox_cts), tuple(swa_cts)))
    dproj = dproj.astype(BF16)

    rider, done = ride(scat("w_out"))
    du, rid = mm_nt("mix_du", dproj, win, rider=rider)
    done(rid)
    dwin, _ = mm_tn("mix_dwin", u, dproj, out_dtype=BF16)
    grad("w_in", plan.split(dwin))
    rider, done = ride(xchg("w_in"))
    dh1, dgain_mix, dh1_b, rid = rms_bwd("mix_dnorm", h1, sm["norm_mix_g"], r_mix, du, dres=dh2, rider=rider, also_bf16=True)
    done(rid)
    own("w_out")

    rider, done = ride(scat("w_in", (0, 1, 2)))
    dwd1, rid = mm_tn("ffn1_dwd", hid1, dh1_b, out_dtype=BF16, scale=0.5, rider=rider)
    done(rid)
    grad(down1, dwd1)
    rider, done = ride(scat("w_in", (1, 2, 2)), xchg(down1))
    (dg1, du1), rid = ffn_dh("ffn1_dh", dh1_b, wd1, hdg1, hdu1, ns, 0.5, rider=rider)
    done(rid)
    own("w_in")
    rider, done = ride(scat(down1, (0, 1, 2)), join("w_out"))
    dwg1, rid = mm_tn_sharded("ffn1_dwg", xn1, dg1, ns, rider=rider)
    done(rid)
    grad(gate1, dwg1)
    rider, done = ride(scat(down1, (1, 2, 2)), xchg(gate1), join("w_in"))
    dwu1, rid = mm_tn_sharded("ffn1_dwu", xn1, du1, ns, rider=rider)
    done(rid)
    grad(up1, dwu1)
    own(down1)
    rider, done = ride(scat(gate1), xchg(up1), join(down1))
    dxn, rid = mm_nt_sharded("ffn1_dxn_g", dg1, bufs[gate1], rider=rider)
    done(rid)
    own(gate1)
    rider, done = ride(scat(up1, (0, 3, 4)))
    dxn, rid = mm_nt_sharded("ffn1_dxn_u", du1, bufs[up1], resid=dxn, rider=rider)
    done(rid)
    dx, dgain_ffn1 = rms_bwd("ffn1_dnorm", x, sm["norm_ffn1_g"], r1, dxn, dres=dh1)

    rider, done = ride(scat(up1, (3, 4, 4)), join(gate1))
    done(run_step("reduce_tail", rider))
    own(up1)
    rider, done = ride(join(up1))
    done(run_step("join_tail", rider))

    g_small = dict(dsm_attn)
    g_small["norm_mix_g"] = g_small["norm_mix_g"] + dgain_mix
    g_small.update(norm_ffn1_g=dgain_ffn1, norm_ffn2_g=dgain_ffn2, out_norm_fox_g=dgain_fox, out_norm_swa_g=dgain_swa)
    return loss, dx, full, g_small


def kernel(x, positions, norm_ffn1_g, ffn1_w_gate, ffn1_w_up, ffn1_w_down, norm_mix_g, w_in, b_forget, fox_q_norm_g, fox_k_norm_g, swa_q_norm_g, swa_k_norm_g, swa_sinks, out_norm_fox_g, out_norm_swa_g, w_out, norm_ffn2_g, ffn2_w_gate, ffn2_w_up, ffn2_w_down, loss_target, m_norm_ffn1_g, m_ffn1_w_gate, m_ffn1_w_up, m_ffn1_w_down, m_norm_mix_g, m_w_in, m_b_forget, m_fox_q_norm_g, m_fox_k_norm_g, m_swa_q_norm_g, m_swa_k_norm_g, m_swa_sinks, m_out_norm_fox_g, m_out_norm_swa_g, m_w_out, m_norm_ffn2_g, m_ffn2_w_gate, m_ffn2_w_up, m_ffn2_w_down, v_norm_ffn1_g, v_ffn1_w_gate, v_ffn1_w_up, v_ffn1_w_down, v_norm_mix_g, v_w_in, v_b_forget, v_fox_q_norm_g, v_fox_k_norm_g, v_swa_q_norm_g, v_swa_k_norm_g, v_swa_sinks, v_out_norm_fox_g, v_out_norm_swa_g, v_w_out, v_norm_ffn2_g, v_ffn2_w_gate, v_ffn2_w_up, v_ffn2_w_down):
    args = dict(locals())
    w = {k: args[k] for k in _ALL}
    m = {k: args["m_" + k] for k in _ALL}
    v = {k: args["v_" + k] for k in _ALL}
    c_idx = lax.axis_index("c").astype(jnp.int32).reshape(1)
    p_idx = (2 * lax.axis_index("x") + lax.axis_index("y")).astype(jnp.int32).reshape(1)
    pc_idx = jnp.concatenate([p_idx, c_idx])

    small = {k: w[k] for k in _SMALL}
    shards = {k: w[k][0] for k in _BIG}
    plan = WinPlan(x.shape[-1])
    shards["w_in"] = lax.switch(p_idx[0], [functools.partial(plan.place, s=s) for s in range(N_CHIPS)], shards["w_in"])
    loss, grad_x, g_shard, g_small = _local_step(shards, {k: w[k][0] for k in _SMALL}, x[0], positions[0], loss_target[0],
                                                 p_idx, c_idx, pc_idx)
    g_shard["w_in"] = lax.switch(p_idx[0], [functools.partial(plan.unplace, s=s) for s in range(N_CHIPS)], g_shard["w_in"])
    loss = lax.psum(loss, ("x", "y", "c"))
    g_small_sum = _unpack_small(all_reduce_small(_pack_small({k: g_small[k].reshape(1, -1) for k in _SMALL})), small)

    grad_w, delta, new_m, new_v = {}, {}, {}, {}
    for k in _BIG:
        (g, d, nm, nv), _ = adamw("adamw_" + k, w[k][0], g_shard[k], m[k][0], v[k][0])
        grad_w[k], delta[k], new_m[k], new_v[k] = g[None], d[None], nm[None], nv[None]
    (_, d, nm, nv), _ = adamw("adamw_small", _pack_small(small), _pack_small(g_small_sum), _pack_small({k: m[k] for k in _SMALL}),
                              _pack_small({k: v[k] for k in _SMALL}))
    grad_w.update(g_small_sum)
    delta.update(_unpack_small(d, small))
    new_m.update(_unpack_small(nm, small))
    new_v.update(_unpack_small(nv, small))

    return (loss, grad_x[None], *[grad_w[k] for k in _ALL], *[delta[k] for k in _ALL], *[new_m[k] for k in _ALL], *[new_v[k] for k in _ALL])
```

```python
import functools

import jax
import jax.numpy as jnp
from jax import lax
from jax.experimental import pallas as pl
from jax.experimental.pallas import tpu as pltpu

F32 = jnp.float32
BF16 = jnp.bfloat16

HEAD_DIM = 64
WINDOW = 128
ROPE_THETA = 10000.0
EPS = 1e-6
N_CHIPS = 4
N_DEV = 8

ADAM_LR = 0.001
ADAM_B1 = 0.9
ADAM_B2 = 0.999
ADAM_EPS = 1e-08
ADAM_WD = 0.01
ADAM_STEP = 10

V7X_VMEM_BYTES = 64 * 1024 * 1024
VMEM_LIMIT = V7X_VMEM_BYTES - 8 * 1024 * 1024
MASK_VALUE = -1e30

_MESH = pl.DeviceIdType.MESH
_HBM = pl.BlockSpec(memory_space=pl.ANY)
_DIMS = {"nn": (((1,), (0,)), ((), ())), "nt": (((1,), (1,)), ((), ())), "tn": (((0,), (0,)), ((), ()))}


def _pick(n, prefs):
    for p in prefs:
        if n % p == 0:
            return p
    return n


class Rider:
    def __init__(self, reads, aliased, news, nsem, build):
        self.reads, self.aliased, self.news, self.nsem, self.build = list(reads), list(aliased), list(news), nsem, build


class _Shifted:
    def __init__(self, ref, off):
        self.ref, self.off = ref, off

    @property
    def at(self):
        return self

    def __getitem__(self, k):
        return self.ref.at[k + self.off]


def combine(*riders):
    def build(reads, al, news, ssem, rsem):
        out = ([], [], [])
        r0 = a0 = n0 = s0 = 0
        for rd in riders:
            nr, na, nn = len(rd.reads), len(rd.aliased), len(rd.news)
            part = rd.build(reads[r0:r0 + nr], al[a0:a0 + na], news[n0:n0 + nn], _Shifted(ssem, s0), _Shifted(rsem, s0))
            for acc, lst in zip(out, part):
                acc.extend(lst)
            r0, a0, n0, s0 = r0 + nr, a0 + na, n0 + nn, s0 + rd.nsem
        return out

    return Rider(sum((r.reads for r in riders), []), sum((r.aliased for r in riders), []), sum((r.news for r in riders), []),
                 sum(r.nsem for r in riders), build)


def _me():
    return lax.axis_index("x"), lax.axis_index("y"), lax.axis_index("c")


def _other_chips(x, y):
    return [(1 - x, y), (x, 1 - y), (1 - x, 1 - y)]


WHOLE = (0, 1, 1)


def _rows(ref, start, rows, part=WHOLE):
    k0, k1, n = part
    assert rows % n == 0, (rows, part)
    idx = (slice(None),) * (len(ref.shape) - 2) + (pl.ds(start + k0 * (rows // n), (k1 - k0) * (rows // n)), slice(None))
    return ref.at[idx]


def _half(ref, h, part=WHOLE):
    rows = ref.shape[-2] // 2
    return _rows(ref, h * rows, rows, part)


def _remote(src, dst, ssem, rsem, k, to):
    return pltpu.make_async_remote_copy(src_ref=src, dst_ref=dst, send_sem=ssem.at[k], recv_sem=rsem.at[k], device_id=to,
                                        device_id_type=_MESH)


def _later(*args):
    return functools.partial(_remote, *args)


def gather(bufs, jobs):
    def build(reads, al, news, ssem, rsem):
        x, y, c = _me()
        p = 2 * x + y
        starts, arrivals = [], []
        for n, (b, kind, part) in enumerate(jobs):
            for j, chip in enumerate(_other_chips(x, y)):
                q = 2 * chip[0] + chip[1]
                if kind == "ici":
                    src, landing, to = _half(al[b].at[p], c, part), _half(al[b].at[q], c, part), (*chip, c)
                else:
                    src, landing, to = _half(al[b].at[q], c, part), _half(al[b].at[q], 1 - c, part), (x, y, 1 - c)
                starts.append(_later(src, src, ssem, rsem, 3 * n + j, to))
                arrivals.append(_later(landing, landing, ssem, rsem, 3 * n + j, to))
        return starts, arrivals, starts

    return Rider([], bufs, [], 3 * len(jobs), build)


def exchange_halves(grads):
    def build(reads, al, news, ssem, rsem):
        x, y, c = _me()
        cps = [_later(_half(g, 1 - c), t, ssem, rsem, w, (x, y, 1 - c)) for w, (g, t) in enumerate(zip(reads, news))]
        return cps, cps, cps

    return Rider(grads, [], [jax.ShapeDtypeStruct((g.shape[0], g.shape[1] // 2, g.shape[2]), g.dtype) for g in grads], len(grads), build)


def scatter_to_owner(sums, gots=None, part=WHOLE):
    def build(reads, al, news, ssem, rsem):
        x, y, c = _me()
        cps = []
        for w, (s, got) in enumerate(zip(reads, al or news)):
            rows = s.shape[-2]
            for j, chip in enumerate(_other_chips(x, y)):
                cps.append(_later(_rows(s.at[2 * chip[0] + chip[1]], 0, rows, part), _rows(got.at[j], 0, rows, part), ssem, rsem,
                                  3 * w + j, (*chip, c)))
        return cps, cps, cps

    news = [] if gots else [jax.ShapeDtypeStruct((3,) + s.shape[1:], s.dtype) for s in sums]
    return Rider(sums, gots or [], news, 3 * len(sums), build)


def join_halves(fulls):
    def build(reads, al, news, ssem, rsem):
        x, y, c = _me()
        starts, arrivals = [], []
        for w, f in enumerate(al):
            mine, landing = _half(f, c), _half(f, 1 - c)
            starts.append(_later(mine, mine, ssem, rsem, w, (x, y, 1 - c)))
            arrivals.append(_later(landing, landing, ssem, rsem, w, (x, y, 1 - c)))
        return starts, arrivals, starts

    return Rider([], fulls, [], len(fulls), build)


def _start_and_wait(rider, reads, al, news, ssem, rsem, first, last):
    @pl.when(first)
    def _():
        for cp in rider.build(reads, al, news, ssem, rsem)[0]:
            cp().start()

    def finish():
        @pl.when(last)
        def _():
            _, arrivals, sends = rider.build(reads, al, news, ssem, rsem)
            for cp in arrivals:
                cp().wait_recv()
            for cp in sends:
                cp().wait_send()

    return finish


def _call(name, body, grid, in_specs, args, out_specs, out_shape, scratch=(), semantics=None, rider=None, prefetch=()):
    n_pre, n_in, n_out, n_scr = len(prefetch), len(args), len(out_shape), len(scratch)
    nr, na, nn = (len(rider.reads), len(rider.aliased), len(rider.news)) if rider else (0, 0, 0)

    def wrapped(*refs):
        pre, refs = refs[:n_pre], refs[n_pre:]
        ins, reads = refs[:n_in], refs[n_in:n_in + nr]
        o0 = n_in + nr + na
        outs, al, news = refs[o0:o0 + n_out], refs[o0 + n_out:o0 + n_out + na], refs[o0 + n_out + na:o0 + n_out + na + nn]
        s0 = o0 + n_out + na + nn
        scr, (ssem, rsem) = refs[s0:s0 + n_scr], refs[s0 + n_scr:]
        first = functools.reduce(jnp.logical_and, [pl.program_id(a) == 0 for a in range(len(grid))])
        last = functools.reduce(jnp.logical_and, [pl.program_id(a) == g - 1 for a, g in enumerate(grid)])
        finish = _start_and_wait(rider, reads, al, news, ssem, rsem, first, last)
        body(*pre, *ins, *outs, *scr)
        finish()

    kernel_fn, all_in, all_out, shapes, scr = body, list(in_specs), list(out_specs), list(out_shape), list(scratch)
    operands, aliases = (*prefetch, *args), {}
    if rider:
        kernel_fn, semantics = wrapped, ("arbitrary",) * len(grid)
        all_in += [_HBM] * (nr + na)
        all_out += [_HBM] * (na + nn)
        shapes += [jax.ShapeDtypeStruct(a.shape, a.dtype) for a in rider.aliased] + rider.news
        scr += [pltpu.SemaphoreType.DMA((rider.nsem,)), pltpu.SemaphoreType.DMA((rider.nsem,))]
        operands += (*rider.reads, *rider.aliased)
        aliases = {n_pre + n_in + nr + i: n_out + i for i in range(na)}
    params = pltpu.CompilerParams(dimension_semantics=semantics, vmem_limit_bytes=VMEM_LIMIT)
    if n_pre:
        spec = pltpu.PrefetchScalarGridSpec(num_scalar_prefetch=n_pre, grid=grid, in_specs=all_in, out_specs=all_out, scratch_shapes=scr)
        outs = pl.pallas_call(kernel_fn, name=name, grid_spec=spec, out_shape=shapes, input_output_aliases=aliases, compiler_params=params)(*operands)
    else:
        outs = pl.pallas_call(kernel_fn, name=name, grid=grid, in_specs=all_in, out_specs=all_out, out_shape=shapes, scratch_shapes=scr,
                              input_output_aliases=aliases, compiler_params=params)(*operands)
    return list(outs[:n_out]), ((list(outs[n_out:n_out + na]), list(outs[n_out + na:])) if rider else None)


def run_step(name, rider):
    nr, na, nn = len(rider.reads), len(rider.aliased), len(rider.news)

    def body(*refs):
        reads = refs[:nr]
        al, news = refs[nr + na:nr + 2 * na], refs[nr + 2 * na:nr + 2 * na + nn]
        ssem, rsem = refs[nr + 2 * na + nn:]
        starts, arrivals, sends = rider.build(reads, al, news, ssem, rsem)
        for cp in starts:
            cp().start()
        for cp in arrivals:
            cp().wait_recv()
        for cp in sends:
            cp().wait_send()

    outs = pl.pallas_call(
        body, name=name, in_specs=[_HBM] * (nr + na), out_specs=[_HBM] * (na + nn),
        out_shape=[jax.ShapeDtypeStruct(a.shape, a.dtype) for a in rider.aliased] + rider.news,
        input_output_aliases={nr + i: i for i in range(na)},
        scratch_shapes=[pltpu.SemaphoreType.DMA((rider.nsem,)), pltpu.SemaphoreType.DMA((rider.nsem,))],
    )(*rider.reads, *rider.aliased)
    return list(outs[:na]), list(outs[na:])


def all_reduce_small(v):
    rows, lanes = v.shape

    def body(v_ref, o_ref, slots, send_sems, recv_sems):
        x, y, c = _me()
        me = 4 * x + 2 * y + c
        slots[me] = v_ref[...]
        cps = []
        for k in range(1, N_DEV):
            peer = (x ^ (k >> 2), y ^ ((k >> 1) & 1), c ^ (k & 1))
            cps.append(_remote(v_ref, slots.at[me], send_sems, recv_sems, k - 1, peer))
            cps[-1].start()
        for k in range(1, N_DEV):
            theirs = slots.at[me ^ k]
            _remote(theirs, theirs, send_sems, recv_sems, k - 1, (x, y, c)).wait_recv()
        for cp in cps:
            cp.wait_send()
        acc = slots[0]
        for i in range(1, N_DEV):
            acc = acc + slots[i]
        o_ref[...] = acc

    return pl.pallas_call(
        body, name="all_reduce_small",
        in_specs=[pl.BlockSpec(memory_space=pltpu.VMEM)], out_specs=pl.BlockSpec(memory_space=pltpu.VMEM),
        out_shape=jax.ShapeDtypeStruct((rows, lanes), F32),
        scratch_shapes=[pltpu.VMEM((N_DEV, rows, lanes), F32), pltpu.SemaphoreType.DMA((N_DEV - 1,)), pltpu.SemaphoreType.DMA((N_DEV - 1,))],
    )(v)


def _mm_call(name, mode, a, b, a_spec, b_spec, out_shape, out_spec, grid, acc_shape, scale=1.0, resid=None, resid_spec=None, rider=None):
    nk = grid[2]
    dims = _DIMS[mode]
    has_resid = resid is not None

    def body(*refs):
        a_ref, b_ref = refs[:2]
        r_ref = refs[2] if has_resid else None
        o_ref = refs[3] if has_resid else refs[2]

        def finish(r):
            if scale != 1.0:
                r = r * scale
            if has_resid:
                r = r_ref[...].astype(F32) + r
            o_ref[...] = r.astype(o_ref.dtype)

        part = lax.dot_general(a_ref[...].astype(BF16), b_ref[...].astype(BF16), dims, preferred_element_type=F32)
        if nk == 1:
            finish(part)
            return
        acc_ref = refs[-1]
        k = pl.program_id(2)

        @pl.when(k == 0)
        def _():
            acc_ref[...] = part

        @pl.when(k > 0)
        def _():
            acc_ref[...] += part

        @pl.when(k == nk - 1)
        def _():
            finish(acc_ref[...])

    in_specs = [a_spec, b_spec] + ([resid_spec] if has_resid else [])
    args = (a, b) + ((resid,) if has_resid else ())
    (out,), rid = _call(name, body, grid, in_specs, args, [out_spec], [out_shape], [pltpu.VMEM(acc_shape, F32)] if nk > 1 else [],
                        ("parallel", "parallel", "arbitrary"), rider)
    return out, rid


MM_VMEM_BUDGET = 40 * 1024 * 1024
_TILE_OPTS = (2048, 1408, 1024, 512, 256, 128)


def _tiles(m, n, kd, a_item, b_item, o_item, r_item=0, tm=None, tn=None, tk=None):
    def opts(full, fixed, cap):
        return [fixed] if fixed else [t for t in _TILE_OPTS if t <= cap and full % t == 0] or [full]

    best = None
    for cm in opts(m, tm, 1408):
        for cn in opts(n, tn, 1408):
            for ck in opts(kd, tk, 2048):
                blocks = cm * ck * a_item + ck * cn * b_item + cm * cn * (o_item + r_item)
                casts = (cm * ck * 2 if a_item == 4 else 0) + (ck * cn * 2 if b_item == 4 else 0)
                if 2 * blocks + cm * cn * 4 + casts <= MM_VMEM_BUDGET:
                    key = (cm * cn * ck, ck)
                    if best is None or key > best[0]:
                        best = (key, (cm, cn, ck))
    assert best is not None, (m, n, kd)
    return best[1]


def _item(x):
    return jnp.dtype(x.dtype).itemsize


def mm_nn(name, a, b, *, out_dtype=F32, scale=1.0, resid=None, rider=None):
    m, kd = a.shape
    n = b.shape[1]
    tm, tn, tk = _tiles(m, n, kd, _item(a), _item(b), jnp.dtype(out_dtype).itemsize, 0 if resid is None else _item(resid))
    o_spec = pl.BlockSpec((tm, tn), lambda i, j, k: (i, j))
    return _mm_call(
        name, "nn", a, b, pl.BlockSpec((tm, tk), lambda i, j, k: (i, k)), pl.BlockSpec((tk, tn), lambda i, j, k: (k, j)),
        jax.ShapeDtypeStruct((m, n), out_dtype), o_spec, (m // tm, n // tn, kd // tk), (tm, tn), scale, resid, o_spec, rider)


def mm_nt(name, a, b, *, out_dtype=F32, scale=1.0, resid=None, rider=None):
    m, kd = a.shape
    n = b.shape[0]
    tm, tn, tk = _tiles(m, n, kd, _item(a), _item(b), jnp.dtype(out_dtype).itemsize, 0 if resid is None else _item(resid))
    o_spec = pl.BlockSpec((tm, tn), lambda i, j, k: (i, j))
    return _mm_call(
        name, "nt", a, b, pl.BlockSpec((tm, tk), lambda i, j, k: (i, k)), pl.BlockSpec((tn, tk), lambda i, j, k: (j, k)),
        jax.ShapeDtypeStruct((m, n), out_dtype), o_spec, (m // tm, n // tn, kd // tk), (tm, tn), scale, resid, o_spec, rider)


def mm_tn(name, a, b, *, out_dtype=F32, scale=1.0, rider=None):
    kd, m = a.shape
    n = b.shape[1]
    tm, tn, tk = _tiles(m, n, kd, _item(a), _item(b), jnp.dtype(out_dtype).itemsize)
    return _mm_call(
        name, "tn", a, b, pl.BlockSpec((tk, tm), lambda i, j, k: (k, i)), pl.BlockSpec((tk, tn), lambda i, j, k: (k, j)),
        jax.ShapeDtypeStruct((m, n), out_dtype), pl.BlockSpec((tm, tn), lambda i, j, k: (i, j)),
        (m // tm, n // tn, kd // tk), (tm, tn), scale, rider=rider)


def mm_nt_sharded(name, a, w, *, resid=None, rider=None):
    m = a.shape[0]
    ns, n, c = w.shape
    tm, tn, _ = _tiles(m, n, c, _item(a), _item(w), 4, 0 if resid is None else _item(resid), tk=c)
    o_spec = pl.BlockSpec((tm, tn), lambda i, j, k: (i, j))
    return _mm_call(
        name, "nt", a, w, pl.BlockSpec((tm, c), lambda i, j, k: (i, k)), pl.BlockSpec((None, tn, c), lambda i, j, k: (k, j, 0)),
        jax.ShapeDtypeStruct((m, n), F32), o_spec, (m // tm, n // tn, ns), (tm, tn), 1.0, resid, o_spec, rider)


def mm_tn_sharded(name, a, b, ns, *, rider=None):
    kd, m = a.shape
    c = b.shape[1] // ns
    tm, _, tk = _tiles(m, c, kd, _item(a), _item(b), 2, tn=c)
    return _mm_call(
        name, "tn", a, b, pl.BlockSpec((tk, tm), lambda i, j, k: (k, i)), pl.BlockSpec((tk, c), lambda i, j, k: (k, j)),
        jax.ShapeDtypeStruct((ns, m, c), BF16), pl.BlockSpec((None, tm, c), lambda i, j, k: (j, i, 0)),
        (m // tm, ns, kd // tk), (tm, c), rider=rider)


def rms_fwd(name, x, g, out_dtype, rider=None):
    r, c = x.shape
    tm = _pick(r, (512, 256, 128, 64, 8))

    def body(x_ref, g_ref, y_ref, r_ref):
        xf = x_ref[...].astype(F32)
        rstd = lax.rsqrt(jnp.mean(xf * xf, axis=-1, keepdims=True) + EPS)
        y_ref[...] = ((xf * rstd) * g_ref[...]).astype(y_ref.dtype)
        r_ref[...] = rstd

    (y, rstd), rid = _call(
        name, body, (r // tm,), [pl.BlockSpec((tm, c), lambda i: (i, 0)), pl.BlockSpec((1, c), lambda i: (0, 0))], (x, g.reshape(1, c)),
        [pl.BlockSpec((tm, c), lambda i: (i, 0)), pl.BlockSpec((tm, 1), lambda i: (i, 0))],
        [jax.ShapeDtypeStruct((r, c), out_dtype), jax.ShapeDtypeStruct((r, 1), F32)], (), ("parallel",), rider)
    return (y, rstd) if rider is None else (y, rstd, rid)


def rms_bwd(name, x, g, rstd, dy, dres=None, rider=None, also_bf16=False):
    r, c = x.shape
    tm = _pick(r, (512, 256, 128, 64, 8))
    has_res = dres is not None

    def body(*refs):
        x_ref, g_ref, r_ref, dy_ref = refs[:4]
        dres_ref = refs[4] if has_res else None
        dx_ref, dg_ref = refs[4 + has_res:6 + has_res]
        xhat = x_ref[...].astype(F32) * r_ref[...]
        dyf = dy_ref[...].astype(F32)
        gdy = dyf * g_ref[...]
        dx = r_ref[...] * (gdy - xhat * jnp.mean(gdy * xhat, axis=-1, keepdims=True))
        if has_res:
            dx = dx + dres_ref[...]
        dx_ref[...] = dx
        if also_bf16:
            refs[-1][...] = dx.astype(BF16)

        @pl.when(pl.program_id(0) == 0)
        def _():
            dg_ref[...] = jnp.zeros_like(dg_ref)

        dg_ref[...] += jnp.sum(dyf * xhat, axis=0, keepdims=True)

    row = pl.BlockSpec((tm, c), lambda i: (i, 0))
    in_specs = [row, pl.BlockSpec((1, c), lambda i: (0, 0)), pl.BlockSpec((tm, 1), lambda i: (i, 0)), row] + ([row] if has_res else [])
    args = (x, g.reshape(1, c), rstd, dy) + ((dres,) if has_res else ())
    outs, rid = _call(name, body, (r // tm,), in_specs, args, [row, pl.BlockSpec((1, c), lambda i: (0, 0))] + [row] * also_bf16,
                      [jax.ShapeDtypeStruct((r, c), F32), jax.ShapeDtypeStruct((1, c), F32)] + [jax.ShapeDtypeStruct((r, c), BF16)] * also_bf16,
                      (), ("arbitrary",), rider)
    return (outs[0], outs[1].reshape(c), *outs[2:], *([] if rider is None else [rid]))


_LANES = 128


def _head_mean(v):
    if v.shape[1] == HEAD_DIM:
        return jnp.mean(v, axis=-1, keepdims=True)
    low = lax.broadcasted_iota(jnp.int32, v.shape, 1) < HEAD_DIM
    lo = jnp.sum(jnp.where(low, v, 0.0), axis=-1, keepdims=True)
    hi = jnp.sum(jnp.where(low, 0.0, v), axis=-1, keepdims=True)
    return jnp.where(low, lo, hi) * (1.0 / HEAD_DIM)


def _head_groups(c):
    width = _LANES if c % _LANES == 0 else HEAD_DIM
    assert c % width == 0, c
    return width, [slice(k * width, (k + 1) * width) for k in range(c // width)]


def _head_gain(g, width):
    return jnp.tile(g.reshape(1, HEAD_DIM), (1, width // HEAD_DIM))


def _rotate_half(y):
    half = HEAD_DIM // 2
    first = lax.broadcasted_iota(jnp.int32, y.shape, 1) % HEAD_DIM < half
    return jnp.where(first, -pltpu.roll(y, y.shape[1] - half, axis=1), pltpu.roll(y, half, axis=1))


def _rope_tables(rope, width):
    return [jnp.tile(t, (1, 2 * width // HEAD_DIM)) for t in rope]


def head_rms_fwd(name, x, g, rope=None):
    s, c = x.shape
    tm = _pick(s, (256, 128, 8))
    width, groups = _head_groups(c)

    def body(x_ref, g_ref, *refs):
        y_ref = refs[-1]
        for sl in groups:
            xs = x_ref[:, sl]
            y = (xs * lax.rsqrt(_head_mean(xs * xs) + EPS)) * g_ref[...]
            if rope:
                y = y * refs[0][...] + _rotate_half(y) * refs[1][...]
            y_ref[:, sl] = y

    row = pl.BlockSpec((tm, c), lambda i: (i, 0))
    tab = pl.BlockSpec((tm, width), lambda i: (i, 0))
    tables = _rope_tables(rope, width) if rope else []
    (y,), _ = _call(name, body, (s // tm,), [row, pl.BlockSpec((1, width), lambda i: (0, 0))] + [tab] * len(tables),
                    (x, _head_gain(g, width), *tables), [row], [jax.ShapeDtypeStruct((s, c), F32)], (), ("parallel",))
    return y


def head_rms_bwd(name, x, g, dy, rope=None):
    s, c = x.shape
    tm = _pick(s, (256, 128, 8))
    width, groups = _head_groups(c)

    def body(x_ref, g_ref, dy_ref, *refs):
        dx_ref, dg_ref = refs[-2:]

        @pl.when(pl.program_id(0) == 0)
        def _():
            dg_ref[...] = jnp.zeros_like(dg_ref)

        for sl in groups:
            xs, dys = x_ref[:, sl], dy_ref[:, sl]
            if rope:
                dys = dys * refs[0][...] - _rotate_half(dys * refs[1][...])
            rstd = lax.rsqrt(_head_mean(xs * xs) + EPS)
            xhat = xs * rstd
            gdy = dys * g_ref[...]
            dx_ref[:, sl] = rstd * (gdy - xhat * _head_mean(gdy * xhat))
            dg_ref[...] += jnp.sum(dys * xhat, axis=0, keepdims=True)

    row = pl.BlockSpec((tm, c), lambda i: (i, 0))
    vec = pl.BlockSpec((1, width), lambda i: (0, 0))
    tab = pl.BlockSpec((tm, width), lambda i: (i, 0))
    tables = _rope_tables(rope, width) if rope else []
    (dx, dg), _ = _call(name, body, (s // tm,), [row, vec, row] + [tab] * len(tables), (x, _head_gain(g, width), dy, *tables), [row, vec],
                        [jax.ShapeDtypeStruct((s, c), F32), jax.ShapeDtypeStruct((1, width), F32)], (), ("arbitrary",))
    return dx, jnp.sum(dg.reshape(width // HEAD_DIM, HEAD_DIM), axis=0)


@functools.partial(jax.custom_vjp, nondiff_argnums=(0,))
def head_rms(name, x, g):
    return head_rms_fwd(name + "_fwd", x, g)


def _head_rms_fwd(name, x, g):
    return head_rms_fwd(name + "_fwd", x, g), (x, g)


def _head_rms_bwd(name, res, dy):
    return head_rms_bwd(name + "_bwd", *res, dy)


head_rms.defvjp(_head_rms_fwd, _head_rms_bwd)


@functools.partial(jax.custom_vjp, nondiff_argnums=(0,))
def head_rms_rope(name, x, g, cos, sin):
    return head_rms_fwd(name + "_fwd", x, g, (cos, sin))


def _head_rms_rope_fwd(name, x, g, cos, sin):
    return head_rms_fwd(name + "_fwd", x, g, (cos, sin)), (x, g, cos, sin)


def _head_rms_rope_bwd(name, res, dy):
    x, g, cos, sin = res
    return (*head_rms_bwd(name + "_bwd", x, g, dy, (cos, sin)), jnp.zeros_like(cos), jnp.zeros_like(sin))


head_rms_rope.defvjp(_head_rms_rope_fwd, _head_rms_rope_bwd)


def _cumsum_call(name, a, reverse):
    h, s = a.shape
    tb = _LANES
    assert s % tb == 0

    def body(a_ref, o_ref):
        t_in = lax.broadcasted_iota(jnp.int32, (tb, tb), 0)
        t_out = lax.broadcasted_iota(jnp.int32, (tb, tb), 1)
        tri = jnp.where((t_in >= t_out) if reverse else (t_in <= t_out), 1.0, 0.0).astype(BF16)
        carry = jnp.zeros((h, 1), F32)
        blocks = range(s // tb)
        for b in (reversed(blocks) if reverse else blocks):
            cols = slice(b * tb, (b + 1) * tb)
            block = a_ref[:, cols]
            rest, local = block, jnp.zeros((h, tb), F32)
            for _ in range(3):
                piece = rest.astype(BF16)
                local = local + jnp.dot(piece, tri, preferred_element_type=F32)
                rest = rest - piece.astype(F32)
            o_ref[:, cols] = local + carry
            carry = carry + jnp.sum(block, axis=1, keepdims=True)

    whole = pl.BlockSpec((h, s), lambda j: (0, 0))
    (out,), _ = _call(name, body, (1,), [whole], (a,), [whole], [jax.ShapeDtypeStruct((h, s), F32)], (), ("arbitrary",))
    return out


@jax.custom_vjp
def time_cumsum(a):
    return _cumsum_call("gate_cumsum", a, False)


def _time_cumsum_fwd(a):
    return _cumsum_call("gate_cumsum", a, False), None


def _time_cumsum_bwd(_, dc):
    return (_cumsum_call("gate_cumsum_bwd", dc, True),)


time_cumsum.defvjp(_time_cumsum_fwd, _time_cumsum_bwd)


FFN_TM = 512


def _sigmoid(x):
    return 1.0 / (1.0 + jnp.exp(-x))


def ffn_gu(name, xn, wg, wu, rider=None):
    s, d = xn.shape
    ns, _, c = wg.shape
    tm = _pick(s, (FFN_TM, 128))

    def body(x_ref, wg_ref, wu_ref, h_ref, a_ref, b_ref):
        xb = x_ref[...]
        gv = jnp.dot(xb, wg_ref[...], preferred_element_type=F32)
        uv = jnp.dot(xb, wu_ref[...], preferred_element_type=F32)
        sig = _sigmoid(gv)
        silu = gv * sig
        h_ref[...] = (silu * uv).astype(BF16)
        a_ref[...] = (uv * (sig * (1.0 + gv * (1.0 - sig)))).astype(BF16)
        b_ref[...] = silu.astype(BF16)

    w_spec = pl.BlockSpec((None, d, c), lambda j, i: (j, 0, 0))
    o_spec = pl.BlockSpec((tm, c), lambda j, i: (i, j))
    return _call(
        name, body, (ns, s // tm), [pl.BlockSpec((tm, d), lambda j, i: (i, 0)), w_spec, w_spec], (xn, wg, wu),
        [o_spec, o_spec, o_spec], [jax.ShapeDtypeStruct((s, ns * c), BF16)] * 3, [], ("parallel", "parallel"), rider)


def ffn_dh(name, dy, wd, dh_dg, dh_du, ns, scale, rider=None):
    s, d = dy.shape
    f = wd.shape[0]
    c = f // ns
    tm = _pick(s, (FFN_TM, 128))

    def body(dy_ref, wd_ref, a_ref, b_ref, dg_ref, du_ref):
        dh = lax.dot_general(dy_ref[...].astype(BF16), wd_ref[...], _DIMS["nt"], preferred_element_type=F32) * scale
        dg_ref[...] = (dh * a_ref[...].astype(F32)).astype(BF16)
        du_ref[...] = (dh * b_ref[...].astype(F32)).astype(BF16)

    o_spec = pl.BlockSpec((tm, c), lambda j, i: (i, j))
    return _call(
        name, body, (ns, s // tm),
        [pl.BlockSpec((tm, d), lambda j, i: (i, 0)), pl.BlockSpec((c, d), lambda j, i: (j, 0)), o_spec, o_spec], (dy, wd, dh_dg, dh_du),
        [o_spec, o_spec], [jax.ShapeDtypeStruct((s, f), BF16), jax.ShapeDtypeStruct((s, f), BF16)],
        [], ("parallel", "parallel"), rider)


FOX_TQ = 256


def fox_tile(s_len):
    return min(FOX_TQ, s_len)


def _heads_per_block(h):
    return 2 if h % 2 == 0 else 1


def _fox_queries(q):
    return (q * (HEAD_DIM ** -0.5)).astype(BF16)


def _fox_scores(qs, kc, cq, ck, diagonal):
    s = lax.dot_general(qs, kc.astype(BF16), _DIMS["nt"], preferred_element_type=F32) + cq - ck
    if not diagonal:
        return s
    return jnp.where(lax.broadcasted_iota(jnp.int32, s.shape, 0) >= lax.broadcasted_iota(jnp.int32, s.shape, 1), s, MASK_VALUE)


def _fox_specs(h, s_len, tq):
    hb = _heads_per_block(h)
    qb = pl.BlockSpec((tq, hb * HEAD_DIM), lambda pp, i: (i, pp))
    kb = pl.BlockSpec((s_len, hb * HEAD_DIM), lambda pp, i: (0, pp))
    colb = pl.BlockSpec((hb, tq, 1), lambda pp, i: (pp, i, 0))
    rowb = pl.BlockSpec((hb, s_len // tq, 1, tq), lambda pp, i: (pp, 0, 0, 0))
    return hb, qb, kb, colb, rowb


def fox_fwd(q, k, v, cq, ck, rider=None):
    s_len, hd = q.shape
    h, d = hd // HEAD_DIM, HEAD_DIM
    tq = fox_tile(s_len)
    hb, qb, kb, colb, rowb = _fox_specs(h, s_len, tq)

    def body(q_ref, k_ref, v_ref, cq_ref, ck_ref, o_ref, lse_ref):
        i = pl.program_id(1)
        for hh in range(hb):
            lanes = slice(hh * d, (hh + 1) * d)
            qs, cqv = _fox_queries(q_ref[:, lanes]), cq_ref[hh]

            def chunk(c, carry, diagonal=False):
                m, l, acc = carry
                rows = pl.ds(pl.multiple_of(c * tq, tq), tq)
                s = _fox_scores(qs, k_ref[rows, lanes], cqv, ck_ref[hh, c], diagonal)
                m_new = jnp.maximum(m, jnp.max(s, axis=-1, keepdims=True))
                alpha = jnp.exp(m - m_new)
                p = jnp.exp(s - m_new)
                acc = alpha * acc + jnp.dot(p.astype(BF16), v_ref[rows, lanes].astype(BF16), preferred_element_type=F32)
                return m_new, alpha * l + jnp.sum(p, axis=-1, keepdims=True), acc

            init = (jnp.full((tq, 1), MASK_VALUE, F32), jnp.zeros((tq, 1), F32), jnp.zeros((tq, d), F32))
            m, l, acc = chunk(i, lax.fori_loop(0, i, chunk, init), diagonal=True)
            o_ref[:, lanes] = acc / l
            lse_ref[hh] = m + jnp.log(l)

    return _call(
        "fox_fwd", body, (h // hb, s_len // tq), [qb, kb, kb, colb, rowb], (q, k, v, cq, ck), [qb, colb],
        [jax.ShapeDtypeStruct((s_len, hd), F32), jax.ShapeDtypeStruct((h, s_len, 1), F32)], (), ("parallel", "parallel"), rider)


def fox_bwd(q, k, v, cq, ck, o, lse, do, rider=None):
    s_len, hd = q.shape
    h, d = hd // HEAD_DIM, HEAD_DIM
    tq = fox_tile(s_len)
    scale = HEAD_DIM ** -0.5
    hb, qb, kb, colb, rowb = _fox_specs(h, s_len, tq)

    def body(q_ref, k_ref, v_ref, cq_ref, ck_ref, o_ref, lse_ref, do_ref, dq_ref, dk_ref, dv_ref, dcq_ref, dck_ref):
        i = pl.program_id(1)

        @pl.when(i == 0)
        def _():
            dk_ref[...] = jnp.zeros_like(dk_ref)
            dv_ref[...] = jnp.zeros_like(dv_ref)
            dck_ref[...] = jnp.zeros_like(dck_ref)

        heads = []
        for hh in range(hb):
            lanes = slice(hh * d, (hh + 1) * d)
            dof = do_ref[:, lanes]
            heads.append((lanes, _fox_queries(q_ref[:, lanes]), cq_ref[hh], lse_ref[hh], dof.astype(BF16),
                          jnp.sum(dof * o_ref[:, lanes], axis=-1, keepdims=True)))

        def chunk(c, carry, diagonal=False):
            rows = pl.ds(pl.multiple_of(c * tq, tq), tq)
            out, dks, dvs = [], [], []
            for hh, (lanes, qs, cqv, lse_h, dob, delta) in enumerate(heads):
                dq, dcq = carry[hh]
                kc = k_ref[rows, lanes]
                p = jnp.exp(_fox_scores(qs, kc, cqv, ck_ref[hh, c], diagonal) - lse_h)
                dp = lax.dot_general(dob, v_ref[rows, lanes].astype(BF16), _DIMS["nt"], preferred_element_type=F32)
                ds = p * (dp - delta)
                dsb = ds.astype(BF16)
                dvs.append(lax.dot_general(p.astype(BF16), dob, _DIMS["tn"], preferred_element_type=F32))
                dks.append(lax.dot_general(dsb, qs, _DIMS["tn"], preferred_element_type=F32))
                dck_ref[hh, c] -= jnp.sum(ds, axis=0, keepdims=True)
                out.append((dq + jnp.dot(dsb, kc.astype(BF16), preferred_element_type=F32), dcq + jnp.sum(ds, axis=-1, keepdims=True)))
            dk_ref[rows, :] += jnp.concatenate(dks, axis=1)
            dv_ref[rows, :] += jnp.concatenate(dvs, axis=1)
            return tuple(out)

        init = tuple((jnp.zeros((tq, d), F32), jnp.zeros((tq, 1), F32)) for _ in range(hb))
        done = chunk(i, lax.fori_loop(0, i, chunk, init), diagonal=True)
        dq_ref[...] = jnp.concatenate([dq for dq, _ in done], axis=1) * scale
        for hh, (_, dcq) in enumerate(done):
            dcq_ref[hh] = dcq

    return _call(
        "fox_bwd", body, (h // hb, s_len // tq), [qb, kb, kb, colb, rowb, qb, colb, qb], (q, k, v, cq, ck, o, lse, do),
        [qb, kb, kb, colb, rowb],
        [jax.ShapeDtypeStruct((s_len, hd), F32)] * 3
        + [jax.ShapeDtypeStruct((h, s_len, 1), F32), jax.ShapeDtypeStruct((h, s_len // tq, 1, tq), F32)],
        (), ("parallel", "arbitrary"), rider)


def _stack_heads(ref, first, g):
    return jnp.concatenate([ref[:, (first + j) * HEAD_DIM:(first + j + 1) * HEAD_DIM] for j in range(g)], axis=0)


def _window(prev_ref, cur_ref, hh):
    lanes = slice(hh * HEAD_DIM, (hh + 1) * HEAD_DIM)
    return jnp.concatenate([prev_ref[:, lanes], cur_ref[:, lanes]], axis=0).astype(BF16)


def _swa_band(g, w):
    t = lax.broadcasted_iota(jnp.int32, (g * w, 2 * w), 0) % w
    col = lax.broadcasted_iota(jnp.int32, (g * w, 2 * w), 1)
    rel = t + w - col
    band = (rel >= 0) & (rel < w)
    return jnp.where(jnp.stack([band & (col >= w), band]), 0.0, MASK_VALUE).astype(F32)


def _swa_probs(qs, kw, sink, band):
    s = lax.dot_general(qs, kw, _DIMS["nt"], preferred_element_type=F32) + band
    m = jnp.maximum(jnp.max(s, axis=-1, keepdims=True), sink)
    p = jnp.exp(s - m)
    ps = jnp.exp(sink - m)
    linv = 1.0 / (jnp.sum(p, axis=-1, keepdims=True) + ps)
    return p * linv, ps * linv


def _swa_specs(hk, g, s_len):
    w = WINDOW
    assert s_len % w == 0
    hb = _heads_per_block(hk)
    qb = pl.BlockSpec((w, hb * g * HEAD_DIM), lambda pp, n: (n, pp))
    prev = pl.BlockSpec((w, hb * HEAD_DIM), lambda pp, n: (jnp.maximum(n - 1, 0), pp))
    cur = pl.BlockSpec((w, hb * HEAD_DIM), lambda pp, n: (n, pp))
    sb = pl.BlockSpec((hb, g * w, 1), lambda pp, n: (pp, 0, 0))
    band = pl.BlockSpec((None, g * w, 2 * w), lambda pp, n: (jnp.minimum(n, 1), 0, 0))
    return hb, qb, prev, cur, sb, band


def swa_fwd(q, k, v, sink, rider=None):
    s_len = q.shape[0]
    hk = k.shape[1] // HEAD_DIM
    g = q.shape[1] // k.shape[1]
    w, d = WINDOW, HEAD_DIM
    hb, qb, prev, cur, sb, bandb = _swa_specs(hk, g, s_len)

    def body(q_ref, kp_ref, kc_ref, vp_ref, vc_ref, sink_ref, band_ref, o_ref):
        for hh in range(hb):
            qs = (_stack_heads(q_ref, hh * g, g) * (HEAD_DIM ** -0.5)).astype(BF16)
            p, _ = _swa_probs(qs, _window(kp_ref, kc_ref, hh), sink_ref[hh], band_ref[...])
            o = jnp.dot(p.astype(BF16), _window(vp_ref, vc_ref, hh), preferred_element_type=F32)
            for j in range(g):
                o_ref[:, (hh * g + j) * d:(hh * g + j + 1) * d] = o[j * w:(j + 1) * w]

    (o,), rid = _call("swa_fwd", body, (hk // hb, s_len // w), [qb, prev, cur, prev, cur, sb, bandb],
                      (q, k, k, v, v, sink, _swa_band(g, w)), [qb], [jax.ShapeDtypeStruct(q.shape, F32)], (),
                      ("parallel", "parallel"), rider)
    return o, rid


def swa_bwd(q, k, v, sink, o, do, rider=None):
    s_len = q.shape[0]
    hk = k.shape[1] // HEAD_DIM
    g = q.shape[1] // k.shape[1]
    w, d = WINDOW, HEAD_DIM
    scale = HEAD_DIM ** -0.5
    hb, qb, prev, cur, sb, bandb = _swa_specs(hk, g, s_len)

    def body(q_ref, kp_ref, kc_ref, vp_ref, vc_ref, sink_ref, band_ref, o_ref, do_ref, dq_ref, dkp_ref, dkc_ref, dvp_ref, dvc_ref,
             dsink_ref):
        @pl.when(pl.program_id(1) == 0)
        def _():
            dsink_ref[...] = jnp.zeros_like(dsink_ref)

        for hh in range(hb):
            lanes = slice(hh * d, (hh + 1) * d)
            qs = (_stack_heads(q_ref, hh * g, g) * scale).astype(BF16)
            kw, vw = _window(kp_ref, kc_ref, hh), _window(vp_ref, vc_ref, hh)
            p, ps = _swa_probs(qs, kw, sink_ref[hh], band_ref[...])
            dof = _stack_heads(do_ref, hh * g, g)
            dob = dof.astype(BF16)
            delta = jnp.sum(dof * _stack_heads(o_ref, hh * g, g), axis=-1, keepdims=True)
            dp = lax.dot_general(dob, vw, _DIMS["nt"], preferred_element_type=F32)
            ds = p * (dp - delta)
            dsb = ds.astype(BF16)
            dsink_ref[hh] -= ps * delta
            dq = jnp.dot(dsb, kw, preferred_element_type=F32) * scale
            for j in range(g):
                dq_ref[:, (hh * g + j) * d:(hh * g + j + 1) * d] = dq[j * w:(j + 1) * w]
            dkw = lax.dot_general(dsb, qs, _DIMS["tn"], preferred_element_type=F32)
            dvw = lax.dot_general(p.astype(BF16), dob, _DIMS["tn"], preferred_element_type=F32)
            dkp_ref[:, lanes] = dkw[:w]
            dkc_ref[:, lanes] = dkw[w:]
            dvp_ref[:, lanes] = dvw[:w]
            dvc_ref[:, lanes] = dvw[w:]

    kv_shape = jax.ShapeDtypeStruct(k.shape, F32)
    (dq, dkp, dkc, dvp, dvc, dsink), rid = _call(
        "swa_bwd", body, (hk // hb, s_len // w), [qb, prev, cur, prev, cur, sb, bandb, qb, qb],
        (q, k, k, v, v, sink, _swa_band(g, w), o, do),
        [qb, cur, cur, cur, cur, sb],
        [jax.ShapeDtypeStruct(q.shape, F32), kv_shape, kv_shape, kv_shape, kv_shape, jax.ShapeDtypeStruct((hk, g * w, 1), F32)],
        (), ("parallel", "arbitrary"), rider)

    def shift_up(a):
        return jnp.concatenate([a[w:], jnp.zeros_like(a[:w])], axis=0)

    return (dq, dkc + shift_up(dkp), dvc + shift_up(dvp), dsink), rid


def loss_call(y, target):
    s, d = y.shape
    tm = _pick(s, (512, 256, 128))

    def body(y_ref, t_ref, l_ref, dy_ref, dyb_ref):
        e = y_ref[...] - t_ref[...]
        dy = e * (1.0 / d)
        dy_ref[...] = dy
        dyb_ref[...] = dy.astype(BF16)

        @pl.when(pl.program_id(0) == 0)
        def _():
            l_ref[...] = jnp.zeros_like(l_ref)

        l_ref[...] += jnp.sum(jnp.sum(e * e, axis=0, keepdims=True), axis=1, keepdims=True) * (0.5 / d)

    row = pl.BlockSpec((tm, d), lambda i: (i, 0))
    (l, dy, dyb), _ = _call("loss_head", body, (s // tm,), [row, row], (y, target), [pl.BlockSpec((1, 1), lambda i: (0, 0)), row, row],
                            [jax.ShapeDtypeStruct((1, 1), F32), jax.ShapeDtypeStruct((s, d), F32), jax.ShapeDtypeStruct((s, d), BF16)],
                            (), ("arbitrary",))
    return l[0, 0], dy, dyb


def _row_tile(rows, cols, itemsize, block_bytes=1 << 20):
    target = max(16, block_bytes // (cols * itemsize))
    fits = [t for t in range(16, rows + 1, 16) if rows % t == 0 and t <= target]
    return fits[-1] if fits else rows


CAST_STEPS = 8


def cast_place(name, ws, p_idx, rider=None):
    n = len(ws)
    assert all(w.shape[0] % (16 * CAST_STEPS) == 0 for w in ws), [w.shape for w in ws]

    def body(p_ref, *refs):
        for w_ref, o_ref in zip(refs[:n], refs[n:]):
            o_ref[...] = w_ref[...].astype(BF16)

    return _call(
        name, body, (CAST_STEPS,), [pl.BlockSpec((w.shape[0] // CAST_STEPS, w.shape[1]), lambda i, pr: (i, 0)) for w in ws], tuple(ws),
        [pl.BlockSpec((None, w.shape[0] // CAST_STEPS, w.shape[1]), lambda i, pr: (pr[0], i, 0)) for w in ws],
        [jax.ShapeDtypeStruct((N_CHIPS,) + w.shape, BF16) for w in ws], (), ("parallel",), rider, prefetch=(p_idx,))


def chip_sum(name, grad, theirs, c_idx):
    ns, r, cols = grad.shape
    rh = r // 2
    tr = _row_tile(rh, cols, 2, 2 << 20)
    nb = rh // tr

    def body(c_ref, a_ref, b_ref, o_ref):
        o_ref[...] = (a_ref[...].astype(F32) + b_ref[...].astype(F32)).astype(o_ref.dtype)

    return pl.pallas_call(
        body, name=name,
        grid_spec=pltpu.PrefetchScalarGridSpec(
            num_scalar_prefetch=1, grid=(ns, nb),
            in_specs=[pl.BlockSpec((None, tr, cols), lambda q, i, cr: (q, cr[0] * nb + i, 0)),
                      pl.BlockSpec((None, tr, cols), lambda q, i, cr: (q, i, 0))],
            out_specs=pl.BlockSpec((None, tr, cols), lambda q, i, cr: (q, i, 0))),
        out_shape=jax.ShapeDtypeStruct((ns, rh, cols), BF16),
        compiler_params=pltpu.CompilerParams(dimension_semantics=("parallel", "parallel"), vmem_limit_bytes=VMEM_LIMIT),
    )(c_idx, grad, theirs)


def owner_sum(name, sums, got, pc_idx):
    ns, rh, cols = sums.shape
    tr = _row_tile(rh, cols, 4, 2 << 20)
    nb = rh // tr

    def body(pc_ref, a_ref, b_ref, o_ref):
        o_ref[...] = ((a_ref[...].astype(F32) + b_ref[0].astype(F32)) + b_ref[1].astype(F32)) + b_ref[2].astype(F32)

    return pl.pallas_call(
        body, name=name,
        grid_spec=pltpu.PrefetchScalarGridSpec(
            num_scalar_prefetch=1, grid=(nb,),
            in_specs=[pl.BlockSpec((None, tr, cols), lambda i, pc: (pc[0], i, 0)),
                      pl.BlockSpec((3, tr, cols), lambda i, pc: (0, i, 0))],
            out_specs=pl.BlockSpec((tr, cols), lambda i, pc: (pc[1] * nb + i, 0))),
        out_shape=jax.ShapeDtypeStruct((2 * rh, cols), F32),
        compiler_params=pltpu.CompilerParams(dimension_semantics=("parallel",), vmem_limit_bytes=VMEM_LIMIT),
    )(pc_idx, sums, got)


def adamw(name, w, g, m, v):
    r, cols = w.shape
    tr = _row_tile(r, cols, 4)
    c1 = 1.0 / (1.0 - ADAM_B1 ** ADAM_STEP)
    c2 = 1.0 / (1.0 - ADAM_B2 ** ADAM_STEP)

    def body(w_ref, g_ref, m_ref, v_ref, go_ref, d_ref, nm_ref, nv_ref):
        gv = g_ref[...]
        nm = ADAM_B1 * m_ref[...] + (1.0 - ADAM_B1) * gv
        nv = ADAM_B2 * v_ref[...] + (1.0 - ADAM_B2) * (gv * gv)
        go_ref[...] = gv
        d_ref[...] = -ADAM_LR * ((nm * c1) / (jnp.sqrt(nv * c2) + ADAM_EPS) + ADAM_WD * w_ref[...])
        nm_ref[...] = nm
        nv_ref[...] = nv

    blk = pl.BlockSpec((tr, cols), lambda i: (i, 0))
    return _call(name, body, (r // tr,), [blk] * 4, (w, g, m, v), [blk] * 4, [jax.ShapeDtypeStruct((r, cols), F32)] * 4, (), ("parallel",))


def _win_layout(d_model):
    hf = hq = d_model // (2 * HEAD_DIM)
    hk = hq // 4
    sizes = [hf * HEAD_DIM, hf * HEAD_DIM, hf * HEAD_DIM, hf, hq * HEAD_DIM, hk * HEAD_DIM, hk * HEAD_DIM]
    return hf, hq, hk, sizes


class WinPlan:
    def __init__(self, d_model, ns=N_CHIPS):
        self.hf, self.hq, self.hk, self.sizes = _win_layout(d_model)
        self.ns, self.cs = ns, sum(self.sizes) // ns
        self.jump_at = sum(self.sizes[:4])
        self.jump_by = -self.jump_at % _LANES
        self.base = [self.pos(s * self.cs) // _LANES * _LANES for s in range(ns)]
        ends = [self.pos((s + 1) * self.cs - 1) + 1 - self.base[s] for s in range(ns)]
        self.width = -(-max(ends) // _LANES) * _LANES
        self.total = -(-max(b + self.width for b in self.base) // 1024) * 1024
        starts = [0]
        for sz in self.sizes:
            starts.append(starts[-1] + sz)
        self.segments = [(self.pos(a), sz) for a, sz in zip(starts, self.sizes)]

    def pos(self, g):
        return g if g < self.jump_at else g + self.jump_by

    def pieces(self, s):
        g0, g1 = s * self.cs, (s + 1) * self.cs
        cuts = [g0] + ([self.jump_at] if g0 < self.jump_at < g1 else []) + [g1]
        return [(a - g0, b - a, self.pos(a) - self.base[s]) for a, b in zip(cuts[:-1], cuts[1:])]

    def place(self, w, s):
        parts, at = [], 0
        for t0, n, j0 in self.pieces(s):
            parts += [jnp.zeros((w.shape[0], j0 - at), w.dtype), w[:, t0:t0 + n]]
            at = j0 + n
        return jnp.concatenate(parts + [jnp.zeros((w.shape[0], self.width - at), w.dtype)], axis=1)

    def unplace(self, slab, s):
        return jnp.concatenate([slab[:, j0:j0 + n] for _, n, j0 in self.pieces(s)], axis=1)

    def assemble(self, slabs):
        return sum(jnp.pad(slabs[s], ((0, 0), (b, self.total - b - self.width))) for s, b in enumerate(self.base))

    def split(self, full):
        return jnp.stack([full[:, b:b + self.width] for b in self.base])


def _attn_inputs(proj, sm, positions):
    s_len = proj.shape[0]
    plan = WinPlan(sm["norm_mix_g"].shape[0])
    hf, hq, hk = plan.hf, plan.hq, plan.hk
    grp = hq // hk
    q_f, k_f, v_f, f_logit, q_s, k_s, v_s = [proj[:, a:a + n] for a, n in plan.segments]

    q_f = head_rms("fox_qnorm", q_f, sm["fox_q_norm_g"])
    k_f = head_rms("fox_knorm", k_f, sm["fox_k_norm_g"])
    log_f = jax.nn.log_sigmoid(f_logit + sm["b_forget"])
    c = time_cumsum(log_f.T)

    inv_freq = ROPE_THETA ** (-jnp.arange(0, HEAD_DIM, 2, dtype=F32) / HEAD_DIM)
    ang = positions.astype(F32)[:, None] * inv_freq
    cos, sin = jnp.cos(ang), jnp.sin(ang)
    q_s = head_rms_rope("swa_qnorm", q_s, sm["swa_q_norm_g"], cos, sin)
    k_s = head_rms_rope("swa_knorm", k_s, sm["swa_k_norm_g"], cos, sin)
    sink = jnp.broadcast_to(sm["swa_sinks"].reshape(hk, grp, 1, 1), (hk, grp, WINDOW, 1)).reshape(hk, grp * WINDOW, 1)
    tq = fox_tile(s_len)
    return (q_f, k_f, v_f, c[:, :, None], c.reshape(hf, s_len // tq, 1, tq)), (q_s, k_s, v_s, sink)


_BIG = ("ffn1_w_gate", "ffn1_w_up", "ffn1_w_down", "w_in", "w_out", "ffn2_w_gate", "ffn2_w_up", "ffn2_w_down")
_SMALL = ("norm_ffn1_g", "norm_mix_g", "b_forget", "fox_q_norm_g", "fox_k_norm_g", "swa_q_norm_g", "swa_k_norm_g", "swa_sinks",
          "out_norm_fox_g", "out_norm_swa_g", "norm_ffn2_g")
_ATTN_SMALL = ("norm_mix_g", "b_forget", "fox_q_norm_g", "fox_k_norm_g", "swa_q_norm_g", "swa_k_norm_g", "swa_sinks")
_ALL = ("norm_ffn1_g", "ffn1_w_gate", "ffn1_w_up", "ffn1_w_down", "norm_mix_g", "w_in", "b_forget", "fox_q_norm_g", "fox_k_norm_g",
        "swa_q_norm_g", "swa_k_norm_g", "swa_sinks", "out_norm_fox_g", "out_norm_swa_g", "w_out", "norm_ffn2_g", "ffn2_w_gate",
        "ffn2_w_up", "ffn2_w_down")


def _pack_small(d):
    parts = []
    for k in _SMALL:
        v = d[k].reshape(-1)
        rows = -(-v.shape[0] // _LANES)
        parts.append(jnp.pad(v, (0, rows * _LANES - v.shape[0])).reshape(rows, _LANES))
    a = jnp.concatenate(parts, axis=0)
    return jnp.pad(a, ((0, -a.shape[0] % 8), (0, 0)))


def _unpack_small(a, like):
    out, r0 = {}, 0
    for k in _SMALL:
        nvals = like[k].shape[1]
        rows = -(-nvals // _LANES)
        out[k] = a[r0:r0 + rows].reshape(-1)[:nvals].reshape(1, nvals)
        r0 += rows
    return out


def _stacked(w):
    return w.reshape(-1, w.shape[-1])


def _local_step(shards, sm, x, positions, target, p_idx, c_idx, pc_idx):
    ns = N_CHIPS
    full = {}

    def fetch(*jobs):
        names = list(dict.fromkeys(n for n, _, _ in jobs))
        return names, gather([bufs[n] for n in names], [(names.index(n), kind, part) for n, kind, part in jobs])

    def take(names, rid):
        for n, b in zip(names, rid[0]):
            bufs[n] = b

    n1 = ["ffn1_w_gate", "ffn1_w_up", "ffn1_w_down"]
    n2 = ["ffn2_w_gate", "ffn2_w_up", "ffn2_w_down"]
    later = ["w_in", "w_out"] + n2
    placed, _ = cast_place("cast_place_ffn1", [shards[n] for n in n1], p_idx)
    bufs = dict(zip(n1, placed))
    gate1, up1, down1 = n1
    gate2, up2, down2 = n2
    names, rider = fetch((gate1, "ici", WHOLE), (up1, "ici", WHOLE))
    placed, rid = cast_place("cast_place_later", [shards[n] for n in later], p_idx, rider=rider)
    bufs.update(zip(later, placed))
    take(names, rid)
    names, rider = fetch((gate1, "d2d", WHOLE), (up1, "d2d", WHOLE), (down1, "ici", (0, 1, 4)))
    xn1, r1, rid = rms_fwd("ffn1_norm", x, sm["norm_ffn1_g"], BF16, rider=rider)
    take(names, rid)
    names, rider = fetch((down1, "ici", (1, 4, 4)), ("w_in", "ici", (0, 1, 4)))
    (hid1, hdg1, hdu1), rid = ffn_gu("ffn1_gu", xn1, bufs[gate1], bufs[up1], rider=rider)
    take(names, rid)
    names, rider = fetch((down1, "d2d", WHOLE))
    take(names, run_step("gather_d2d_ffn1_down", rider))
    wd1 = _stacked(bufs[down1])
    names, rider = fetch(("w_in", "ici", (1, 4, 4)))
    h1, rid = mm_nn("ffn1_down", hid1, wd1, scale=0.5, resid=x, rider=rider)
    take(names, rid)

    names, rider = fetch(("w_in", "d2d", WHOLE))
    u, r_mix, rid = rms_fwd("mix_norm", h1, sm["norm_mix_g"], BF16, rider=rider)
    take(names, rid)
    names, rider = fetch(("w_out", "ici", WHOLE))
    plan = WinPlan(x.shape[1])
    win = plan.assemble(bufs["w_in"])
    proj, rid = mm_nn("mix_inproj", u, win, rider=rider)
    take(names, rid)
    sm_attn = {k: sm[k] for k in _ATTN_SMALL}
    (fox_in, swa_in), attn_vjp = jax.vjp(lambda pr, s: _attn_inputs(pr, s, positions), proj, sm_attn)
    names, rider = fetch((gate2, "ici", WHOLE), (up2, "ici", (0, 1, 4)), ("w_out", "d2d", WHOLE))
    (o_f, lse), rid = fox_fwd(*fox_in, rider=rider)
    take(names, rid)
    names, rider = fetch((up2, "ici", (1, 4, 4)), (gate2, "d2d", WHOLE), (up2, "d2d", (0, 1, 4)))
    o_s, rid = swa_fwd(*swa_in, rider=rider)
    take(names, rid)
    o_fox, o_swa = o_f, o_s
    nf, r_fox = rms_fwd("out_norm_fox", o_fox, sm["out_norm_fox_g"], BF16)
    nsw, r_swa = rms_fwd("out_norm_swa", o_swa, sm["out_norm_swa_g"], BF16)
    o = jnp.concatenate([nf, nsw], axis=-1)
    wout = _stacked(bufs["w_out"])
    names, rider = fetch((down2, "ici", (0, 1, 4)), (up2, "d2d", (1, 4, 4)))
    h2, rid = mm_nn("out_proj", o, wout, resid=h1, rider=rider)
    take(names, rid)

    xn2, r2 = rms_fwd("ffn2_norm", h2, sm["norm_ffn2_g"], BF16)
    names, rider = fetch((down2, "ici", (1, 4, 4)))
    (hid2, hdg2, hdu2), rid = ffn_gu("ffn2_gu", xn2, bufs[gate2], bufs[up2], rider=rider)
    take(names, rid)
    names, rider = fetch((down2, "d2d", WHOLE))
    take(names, run_step("gather_d2d_ffn2_down", rider))
    wd2 = _stacked(bufs["ffn2_w_down"])
    y, _ = mm_nn("ffn2_down", hid2, wd2, scale=0.5, resid=h2)
    loss, dy, dy_b = loss_call(y, target)

    red = {}

    def grad(n, g):
        red[n] = {"grad": g.reshape(ns, -1, g.shape[-1])}

    def ride(*steps):
        def done(rid):
            a0 = n0 = 0
            for rd, cb in steps:
                cb(rid[0][a0:a0 + len(rd.aliased)], rid[1][n0:n0 + len(rd.news)])
                a0, n0 = a0 + len(rd.aliased), n0 + len(rd.news)

        return (combine(*[s[0] for s in steps]) if len(steps) > 1 else steps[0][0]), done

    def xchg(*names):
        def cb(al, news):
            for n, t in zip(names, news):
                red[n]["sum"] = chip_sum("chip_sum_" + n, red[n]["grad"], t, c_idx)

        return exchange_halves([red[n]["grad"] for n in names]), cb

    def scat(n, part=WHOLE):
        def cb(al, news):
            red[n]["got"] = (al or news)[0]

        return scatter_to_owner([red[n]["sum"]], [red[n]["got"]] if "got" in red[n] else None, part), cb

    def own(n):
        red[n]["half"] = owner_sum("owner_sum_" + n, red[n]["sum"], red[n]["got"], pc_idx)

    def join(*names):
        return join_halves([red[n]["half"] for n in names]), lambda al, news: full.update(zip(names, al))

    dwd2, _ = mm_tn("ffn2_dwd", hid2, dy_b, out_dtype=BF16, scale=0.5)
    grad(down2, dwd2)
    rider, done = ride(xchg(down2))
    (dg2, du2), rid = ffn_dh("ffn2_dh", dy_b, wd2, hdg2, hdu2, ns, 0.5, rider=rider)
    done(rid)
    rider, done = ride(scat(down2, (0, 1, 2)))
    dwg2, rid = mm_tn_sharded("ffn2_dwg", xn2, dg2, ns, rider=rider)
    done(rid)
    grad(gate2, dwg2)
    rider, done = ride(scat(down2, (1, 2, 2)), xchg(gate2))
    dwu2, rid = mm_tn_sharded("ffn2_dwu", xn2, du2, ns, rider=rider)
    done(rid)
    grad(up2, dwu2)
    rider, done = ride(scat(gate2, (0, 1, 2)), xchg(up2))
    dxn, rid = mm_nt_sharded("ffn2_dxn_g", dg2, bufs[gate2], rider=rider)
    done(rid)
    rider, done = ride(scat(gate2, (1, 2, 2)))
    dxn, rid = mm_nt_sharded("ffn2_dxn_u", du2, bufs[up2], resid=dxn, rider=rider)
    done(rid)
    dh2, dgain_ffn2, dh2_b = rms_bwd("ffn2_dnorm", h2, sm["norm_ffn2_g"], r2, dxn, dres=dy, also_bf16=True)
    own(down2)
    own(gate2)

    do, _ = mm_nt("out_do", dh2_b, wout)
    dwout, _ = mm_tn("out_dw", o, dh2_b, out_dtype=BF16)
    cf = o_fox.shape[1]
    d_fox, dgain_fox = rms_bwd("out_dnorm_fox", o_fox, sm["out_norm_fox_g"], r_fox, do[:, :cf])
    d_swa, dgain_swa = rms_bwd("out_dnorm_swa", o_swa, sm["out_norm_swa_g"], r_swa, do[:, cf:])
    grad("w_out", dwout)
    rider, done = ride(scat(up2))
    swa_cts, rid = swa_bwd(*swa_in, o_s, d_swa, rider=rider)
    done(rid)
    own(up2)
    rider, done = ride(xchg("w_out"), join(down2, gate2, up2))
    fox_cts, rid = fox_bwd(*fox_in, o_f, lse, d_fox, rider=rider)
    done(rid)
    dproj, dsm_attn = attn_vjp((tuple(fox_cts), tuple(swa_cts)))
    dproj = dproj.astype(BF16)

    rider, done = ride(scat("w_out"))
    du, rid = mm_nt("mix_du", dproj, win, rider=rider)
    done(rid)
    dwin, _ = mm_tn("mix_dwin", u, dproj, out_dtype=BF16)
    grad("w_in", plan.split(dwin))
    rider, done = ride(xchg("w_in"))
    dh1, dgain_mix, dh1_b, rid = rms_bwd("mix_dnorm", h1, sm["norm_mix_g"], r_mix, du, dres=dh2, rider=rider, also_bf16=True)
    done(rid)
    own("w_out")

    rider, done = ride(scat("w_in", (0, 1, 2)))
    dwd1, rid = mm_tn("ffn1_dwd", hid1, dh1_b, out_dtype=BF16, scale=0.5, rider=rider)
    done(rid)
    grad(down1, dwd1)
    rider, done = ride(scat("w_in", (1, 2, 2)), xchg(down1))
    (dg1, du1), rid = ffn_dh("ffn1_dh", dh1_b, wd1, hdg1, hdu1, ns, 0.5, rider=rider)
    done(rid)
    own("w_in")
    rider, done = ride(scat(down1, (0, 1, 2)), join("w_out"))
    dwg1, rid = mm_tn_sharded("ffn1_dwg", xn1, dg1, ns, rider=rider)
    done(rid)
    grad(gate1, dwg1)
    rider, done = ride(scat(down1, (1, 2, 2)), xchg(gate1), join("w_in"))
    dwu1, rid = mm_tn_sharded("ffn1_dwu", xn1, du1, ns, rider=rider)
    done(rid)
    grad(up1, dwu1)
    own(down1)
    rider, done = ride(scat(gate1), xchg(up1), join(down1))
    dxn, rid = mm_nt_sharded("ffn1_dxn_g", dg1, bufs[gate1], rider=rider)
    done(rid)
    own(gate1)
    rider, done = ride(scat(up1, (0, 3, 4)))
    dxn, rid = mm_nt_sharded("ffn1_dxn_u", du1, bufs[up1], resid=dxn, rider=rider)
    done(rid)
    dx, dgain_ffn1 = rms_bwd("ffn1_dnorm", x, sm["norm_ffn1_g"], r1, dxn, dres=dh1)

    rider, done = ride(scat(up1, (3, 4, 4)), join(gate1))
    done(run_step("reduce_tail", rider))
    own(up1)
    rider, done = ride(join(up1))
    done(run_step("join_tail", rider))

    g_small = dict(dsm_attn)
    g_small["norm_mix_g"] = g_small["norm_mix_g"] + dgain_mix
    g_small.update(norm_ffn1_g=dgain_ffn1, norm_ffn2_g=dgain_ffn2, out_norm_fox_g=dgain_fox, out_norm_swa_g=dgain_swa)
    return loss, dx, full, g_small


def kernel(x, positions, norm_ffn1_g, ffn1_w_gate, ffn1_w_up, ffn1_w_down, norm_mix_g, w_in, b_forget, fox_q_norm_g, fox_k_norm_g, swa_q_norm_g, swa_k_norm_g, swa_sinks, out_norm_fox_g, out_norm_swa_g, w_out, norm_ffn2_g, ffn2_w_gate, ffn2_w_up, ffn2_w_down, loss_target, m_norm_ffn1_g, m_ffn1_w_gate, m_ffn1_w_up, m_ffn1_w_down, m_norm_mix_g, m_w_in, m_b_forget, m_fox_q_norm_g, m_fox_k_norm_g, m_swa_q_norm_g, m_swa_k_norm_g, m_swa_sinks, m_out_norm_fox_g, m_out_norm_swa_g, m_w_out, m_norm_ffn2_g, m_ffn2_w_gate, m_ffn2_w_up, m_ffn2_w_down, v_norm_ffn1_g, v_ffn1_w_gate, v_ffn1_w_up, v_ffn1_w_down, v_norm_mix_g, v_w_in, v_b_forget, v_fox_q_norm_g, v_fox_k_norm_g, v_swa_q_norm_g, v_swa_k_norm_g, v_swa_sinks, v_out_norm_fox_g, v_out_norm_swa_g, v_w_out, v_norm_ffn2_g, v_ffn2_w_gate, v_ffn2_w_up, v_ffn2_w_down):
    args = dict(locals())
    w = {k: args[k] for k in _ALL}
    m = {k: args["m_" + k] for k in _ALL}
    v = {k: args["v_" + k] for k in _ALL}
    c_idx = lax.axis_index("c").astype(jnp.int32).reshape(1)
    p_idx = (2 * lax.axis_index("x") + lax.axis_index("y")).astype(jnp.int32).reshape(1)
    pc_idx = jnp.concatenate([p_idx, c_idx])

    small = {k: w[k] for k in _SMALL}
    shards = {k: w[k][0] for k in _BIG}
    plan = WinPlan(x.shape[-1])
    shards["w_in"] = lax.switch(p_idx[0], [functools.partial(plan.place, s=s) for s in range(N_CHIPS)], shards["w_in"])
    loss, grad_x, g_shard, g_small = _local_step(shards, {k: w[k][0] for k in _SMALL}, x[0], positions[0], loss_target[0],
                                                 p_idx, c_idx, pc_idx)
    g_shard["w_in"] = lax.switch(p_idx[0], [functools.partial(plan.unplace, s=s) for s in range(N_CHIPS)], g_shard["w_in"])
    loss = lax.psum(loss, ("x", "y", "c"))
    g_small_sum = _unpack_small(all_reduce_small(_pack_small({k: g_small[k].reshape(1, -1) for k in _SMALL})), small)

    grad_w, delta, new_m, new_v = {}, {}, {}, {}
    for k in _BIG:
        (g, d, nm, nv), _ = adamw("adamw_" + k, w[k][0], g_shard[k], m[k][0], v[k][0])
        grad_w[k], delta[k], new_m[k], new_v[k] = g[None], d[None], nm[None], nv[None]
    (_, d, nm, nv), _ = adamw("adamw_small", _pack_small(small), _pack_small(g_small_sum), _pack_small({k: m[k] for k in _SMALL}),
                              _pack_small({k: v[k] for k in _SMALL}))
    grad_w.update(g_small_sum)
    delta.update(_unpack_small(d, small))
    new_m.update(_unpack_small(nm, small))
    new_v.update(_unpack_small(nv, small))

    return (loss, grad_x[None], *[grad_w[k] for k in _ALL], *[delta[k] for k in _ALL], *[new_m[k] for k in _ALL], *[new_v[k] for k in _ALL])
```

```python
import functools

import jax
import jax.numpy as jnp
from jax import lax
from jax.experimental import pallas as pl
from jax.experimental.pallas import tpu as pltpu

F32 = jnp.float32
BF16 = jnp.bfloat16

HEAD_DIM = 64
WINDOW = 128
ROPE_THETA = 10000.0
EPS = 1e-6
N_CHIPS = 4
N_DEV = 8

ADAM_LR = 0.001
ADAM_B1 = 0.9
ADAM_B2 = 0.999
ADAM_EPS = 1e-08
ADAM_WD = 0.01
ADAM_STEP = 10

V7X_VMEM_BYTES = 64 * 1024 * 1024
VMEM_LIMIT = V7X_VMEM_BYTES - 8 * 1024 * 1024
MASK_VALUE = -1e30

_MESH = pl.DeviceIdType.MESH
_HBM = pl.BlockSpec(memory_space=pl.ANY)
_DIMS = {"nn": (((1,), (0,)), ((), ())), "nt": (((1,), (1,)), ((), ())), "tn": (((0,), (0,)), ((), ()))}


def _pick(n, prefs):
    for p in prefs:
        if n % p == 0:
            return p
    return n


class Rider:
    def __init__(self, reads, aliased, news, nsem, build):
        self.reads, self.aliased, self.news, self.nsem, self.build = list(reads), list(aliased), list(news), nsem, build


class _Shifted:
    def __init__(self, ref, off):
        self.ref, self.off = ref, off

    @property
    def at(self):
        return self

    def __getitem__(self, k):
        return self.ref.at[k + self.off]


def combine(*riders):
    def build(reads, al, news, ssem, rsem):
        out = ([], [], [])
        r0 = a0 = n0 = s0 = 0
        for rd in riders:
            nr, na, nn = len(rd.reads), len(rd.aliased), len(rd.news)
            part = rd.build(reads[r0:r0 + nr], al[a0:a0 + na], news[n0:n0 + nn], _Shifted(ssem, s0), _Shifted(rsem, s0))
            for acc, lst in zip(out, part):
                acc.extend(lst)
            r0, a0, n0, s0 = r0 + nr, a0 + na, n0 + nn, s0 + rd.nsem
        return out

    return Rider(sum((r.reads for r in riders), []), sum((r.aliased for r in riders), []), sum((r.news for r in riders), []),
                 sum(r.nsem for r in riders), build)


def _me():
    return lax.axis_index("x"), lax.axis_index("y"), lax.axis_index("c")


def _other_chips(x, y):
    return [(1 - x, y), (x, 1 - y), (1 - x, 1 - y)]


WHOLE = (0, 1, 1)


def _rows(ref, start, rows, part=WHOLE):
    k0, k1, n = part
    assert rows % n == 0, (rows, part)
    idx = (slice(None),) * (len(ref.shape) - 2) + (pl.ds(start + k0 * (rows // n), (k1 - k0) * (rows // n)), slice(None))
    return ref.at[idx]


def _half(ref, h, part=WHOLE):
    rows = ref.shape[-2] // 2
    return _rows(ref, h * rows, rows, part)


def _remote(src, dst, ssem, rsem, k, to):
    return pltpu.make_async_remote_copy(src_ref=src, dst_ref=dst, send_sem=ssem.at[k], recv_sem=rsem.at[k], device_id=to,
                                        device_id_type=_MESH)


def _later(*args):
    return functools.partial(_remote, *args)


def gather(bufs, jobs):
    def build(reads, al, news, ssem, rsem):
        x, y, c = _me()
        p = 2 * x + y
        starts, arrivals = [], []
        for n, (b, kind, part) in enumerate(jobs):
            for j, chip in enumerate(_other_chips(x, y)):
                q = 2 * chip[0] + chip[1]
                if kind == "ici":
                    src, landing, to = _half(al[b].at[p], c, part), _half(al[b].at[q], c, part), (*chip, c)
                else:
                    src, landing, to = _half(al[b].at[q], c, part), _half(al[b].at[q], 1 - c, part), (x, y, 1 - c)
                starts.append(_later(src, src, ssem, rsem, 3 * n + j, to))
                arrivals.append(_later(landing, landing, ssem, rsem, 3 * n + j, to))
        return starts, arrivals, starts

    return Rider([], bufs, [], 3 * len(jobs), build)


def exchange_halves(grads):
    def build(reads, al, news, ssem, rsem):
        x, y, c = _me()
        cps = [_later(_half(g, 1 - c), t, ssem, rsem, w, (x, y, 1 - c)) for w, (g, t) in enumerate(zip(reads, news))]
        return cps, cps, cps

    return Rider(grads, [], [jax.ShapeDtypeStruct((g.shape[0], g.shape[1] // 2, g.shape[2]), g.dtype) for g in grads], len(grads), build)


def scatter_to_owner(sums, gots=None, part=WHOLE):
    def build(reads, al, news, ssem, rsem):
        x, y, c = _me()
        cps = []
        for w, (s, got) in enumerate(zip(reads, al or news)):
            rows = s.shape[-2]
            for j, chip in enumerate(_other_chips(x, y)):
                cps.append(_later(_rows(s.at[2 * chip[0] + chip[1]], 0, rows, part), _rows(got.at[j], 0, rows, part), ssem, rsem,
                                  3 * w + j, (*chip, c)))
        return cps, cps, cps

    news = [] if gots else [jax.ShapeDtypeStruct((3,) + s.shape[1:], s.dtype) for s in sums]
    return Rider(sums, gots or [], news, 3 * len(sums), build)


def join_halves(fulls):
    def build(reads, al, news, ssem, rsem):
        x, y, c = _me()
        starts, arrivals = [], []
        for w, f in enumerate(al):
            mine, landing = _half(f, c), _half(f, 1 - c)
            starts.append(_later(mine, mine, ssem, rsem, w, (x, y, 1 - c)))
            arrivals.append(_later(landing, landing, ssem, rsem, w, (x, y, 1 - c)))
        return starts, arrivals, starts

    return Rider([], fulls, [], len(fulls), build)


def _start_and_wait(rider, reads, al, news, ssem, rsem, first, last):
    @pl.when(first)
    def _():
        for cp in rider.build(reads, al, news, ssem, rsem)[0]:
            cp().start()

    def finish():
        @pl.when(last)
        def _():
            _, arrivals, sends = rider.build(reads, al, news, ssem, rsem)
            for cp in arrivals:
                cp().wait_recv()
            for cp in sends:
                cp().wait_send()

    return finish


def _call(name, body, grid, in_specs, args, out_specs, out_shape, scratch=(), semantics=None, rider=None, prefetch=()):
    n_pre, n_in, n_out, n_scr = len(prefetch), len(args), len(out_shape), len(scratch)
    nr, na, nn = (len(rider.reads), len(rider.aliased), len(rider.news)) if rider else (0, 0, 0)

    def wrapped(*refs):
        pre, refs = refs[:n_pre], refs[n_pre:]
        ins, reads = refs[:n_in], refs[n_in:n_in + nr]
        o0 = n_in + nr + na
        outs, al, news = refs[o0:o0 + n_out], refs[o0 + n_out:o0 + n_out + na], refs[o0 + n_out + na:o0 + n_out + na + nn]
        s0 = o0 + n_out + na + nn
        scr, (ssem, rsem) = refs[s0:s0 + n_scr], refs[s0 + n_scr:]
        first = functools.reduce(jnp.logical_and, [pl.program_id(a) == 0 for a in range(len(grid))])
        last = functools.reduce(jnp.logical_and, [pl.program_id(a) == g - 1 for a, g in enumerate(grid)])
        finish = _start_and_wait(rider, reads, al, news, ssem, rsem, first, last)
        body(*pre, *ins, *outs, *scr)
        finish()

    kernel_fn, all_in, all_out, shapes, scr = body, list(in_specs), list(out_specs), list(out_shape), list(scratch)
    operands, aliases = (*prefetch, *args), {}
    if rider:
        kernel_fn, semantics = wrapped, ("arbitrary",) * len(grid)
        all_in += [_HBM] * (nr + na)
        all_out += [_HBM] * (na + nn)
        shapes += [jax.ShapeDtypeStruct(a.shape, a.dtype) for a in rider.aliased] + rider.news
        scr += [pltpu.SemaphoreType.DMA((rider.nsem,)), pltpu.SemaphoreType.DMA((rider.nsem,))]
        operands += (*rider.reads, *rider.aliased)
        aliases = {n_pre + n_in + nr + i: n_out + i for i in range(na)}
    params = pltpu.CompilerParams(dimension_semantics=semantics, vmem_limit_bytes=VMEM_LIMIT)
    if n_pre:
        spec = pltpu.PrefetchScalarGridSpec(num_scalar_prefetch=n_pre, grid=grid, in_specs=all_in, out_specs=all_out, scratch_shapes=scr)
        outs = pl.pallas_call(kernel_fn, name=name, grid_spec=spec, out_shape=shapes, input_output_aliases=aliases, compiler_params=params)(*operands)
    else:
        outs = pl.pallas_call(kernel_fn, name=name, grid=grid, in_specs=all_in, out_specs=all_out, out_shape=shapes, scratch_shapes=scr,
                              input_output_aliases=aliases, compiler_params=params)(*operands)
    return list(outs[:n_out]), ((list(outs[n_out:n_out + na]), list(outs[n_out + na:])) if rider else None)


def run_step(name, rider):
    nr, na, nn = len(rider.reads), len(rider.aliased), len(rider.news)

    def body(*refs):
        reads = refs[:nr]
        al, news = refs[nr + na:nr + 2 * na], refs[nr + 2 * na:nr + 2 * na + nn]
        ssem, rsem = refs[nr + 2 * na + nn:]
        starts, arrivals, sends = rider.build(reads, al, news, ssem, rsem)
        for cp in starts:
            cp().start()
        for cp in arrivals:
            cp().wait_recv()
        for cp in sends:
            cp().wait_send()

    outs = pl.pallas_call(
        body, name=name, in_specs=[_HBM] * (nr + na), out_specs=[_HBM] * (na + nn),
        out_shape=[jax.ShapeDtypeStruct(a.shape, a.dtype) for a in rider.aliased] + rider.news,
        input_output_aliases={nr + i: i for i in range(na)},
        scratch_shapes=[pltpu.SemaphoreType.DMA((rider.nsem,)), pltpu.SemaphoreType.DMA((rider.nsem,))],
    )(*rider.reads, *rider.aliased)
    return list(outs[:na]), list(outs[na:])


def all_reduce_small(v):
    rows, lanes = v.shape

    def body(v_ref, o_ref, slots, send_sems, recv_sems):
        x, y, c = _me()
        me = 4 * x + 2 * y + c
        slots[me] = v_ref[...]
        cps = []
        for k in range(1, N_DEV):
            peer = (x ^ (k >> 2), y ^ ((k >> 1) & 1), c ^ (k & 1))
            cps.append(_remote(v_ref, slots.at[me], send_sems, recv_sems, k - 1, peer))
            cps[-1].start()
        for k in range(1, N_DEV):
            theirs = slots.at[me ^ k]
            _remote(theirs, theirs, send_sems, recv_sems, k - 1, (x, y, c)).wait_recv()
        for cp in cps:
            cp.wait_send()
        acc = slots[0]
        for i in range(1, N_DEV):
            acc = acc + slots[i]
        o_ref[...] = acc

    return pl.pallas_call(
        body, name="all_reduce_small",
        in_specs=[pl.BlockSpec(memory_space=pltpu.VMEM)], out_specs=pl.BlockSpec(memory_space=pltpu.VMEM),
        out_shape=jax.ShapeDtypeStruct((rows, lanes), F32),
        scratch_shapes=[pltpu.VMEM((N_DEV, rows, lanes), F32), pltpu.SemaphoreType.DMA((N_DEV - 1,)), pltpu.SemaphoreType.DMA((N_DEV - 1,))],
    )(v)


def _mm_call(name, mode, a, b, a_spec, b_spec, out_shape, out_spec, grid, acc_shape, scale=1.0, resid=None, resid_spec=None, rider=None):
    nk = grid[2]
    dims = _DIMS[mode]
    has_resid = resid is not None

    def body(*refs):
        a_ref, b_ref = refs[:2]
        r_ref = refs[2] if has_resid else None
        o_ref = refs[3] if has_resid else refs[2]

        def finish(r):
            if scale != 1.0:
                r = r * scale
            if has_resid:
                r = r_ref[...].astype(F32) + r
            o_ref[...] = r.astype(o_ref.dtype)

        part = lax.dot_general(a_ref[...].astype(BF16), b_ref[...].astype(BF16), dims, preferred_element_type=F32)
        if nk == 1:
            finish(part)
            return
        acc_ref = refs[-1]
        k = pl.program_id(2)

        @pl.when(k == 0)
        def _():
            acc_ref[...] = part

        @pl.when(k > 0)
        def _():
            acc_ref[...] += part

        @pl.when(k == nk - 1)
        def _():
            finish(acc_ref[...])

    in_specs = [a_spec, b_spec] + ([resid_spec] if has_resid else [])
    args = (a, b) + ((resid,) if has_resid else ())
    (out,), rid = _call(name, body, grid, in_specs, args, [out_spec], [out_shape], [pltpu.VMEM(acc_shape, F32)] if nk > 1 else [],
                        ("parallel", "parallel", "arbitrary"), rider)
    return out, rid


MM_VMEM_BUDGET = 40 * 1024 * 1024
_TILE_OPTS = (2048, 1408, 1024, 512, 256, 128)


def _tiles(m, n, kd, a_item, b_item, o_item, r_item=0, tm=None, tn=None, tk=None):
    def opts(full, fixed, cap):
        return [fixed] if fixed else [t for t in _TILE_OPTS if t <= cap and full % t == 0] or [full]

    best = None
    for cm in opts(m, tm, 1408):
        for cn in opts(n, tn, 1408):
            for ck in opts(kd, tk, 2048):
                blocks = cm * ck * a_item + ck * cn * b_item + cm * cn * (o_item + r_item)
                casts = (cm * ck * 2 if a_item == 4 else 0) + (ck * cn * 2 if b_item == 4 else 0)
                if 2 * blocks + cm * cn * 4 + casts <= MM_VMEM_BUDGET:
                    key = (cm * cn * ck, ck)
                    if best is None or key > best[0]:
                        best = (key, (cm, cn, ck))
    assert best is not None, (m, n, kd)
    return best[1]


def _item(x):
    return jnp.dtype(x.dtype).itemsize


def mm_nn(name, a, b, *, out_dtype=F32, scale=1.0, resid=None, rider=None):
    m, kd = a.shape
    n = b.shape[1]
    tm, tn, tk = _tiles(m, n, kd, _item(a), _item(b), jnp.dtype(out_dtype).itemsize, 0 if resid is None else _item(resid))
    o_spec = pl.BlockSpec((tm, tn), lambda i, j, k: (i, j))
    return _mm_call(
        name, "nn", a, b, pl.BlockSpec((tm, tk), lambda i, j, k: (i, k)), pl.BlockSpec((tk, tn), lambda i, j, k: (k, j)),
        jax.ShapeDtypeStruct((m, n), out_dtype), o_spec, (m // tm, n // tn, kd // tk), (tm, tn), scale, resid, o_spec, rider)


def mm_nt(name, a, b, *, out_dtype=F32, scale=1.0, resid=None, rider=None):
    m, kd = a.shape
    n = b.shape[0]
    tm, tn, tk = _tiles(m, n, kd, _item(a), _item(b), jnp.dtype(out_dtype).itemsize, 0 if resid is None else _item(resid))
    o_spec = pl.BlockSpec((tm, tn), lambda i, j, k: (i, j))
    return _mm_call(
        name, "nt", a, b, pl.BlockSpec((tm, tk), lambda i, j, k: (i, k)), pl.BlockSpec((tn, tk), lambda i, j, k: (j, k)),
        jax.ShapeDtypeStruct((m, n), out_dtype), o_spec, (m // tm, n // tn, kd // tk), (tm, tn), scale, resid, o_spec, rider)


def mm_tn(name, a, b, *, out_dtype=F32, scale=1.0, rider=None):
    kd, m = a.shape
    n = b.shape[1]
    tm, tn, tk = _tiles(m, n, kd, _item(a), _item(b), jnp.dtype(out_dtype).itemsize)
    return _mm_call(
        name, "tn", a, b, pl.BlockSpec((tk, tm), lambda i, j, k: (k, i)), pl.BlockSpec((tk, tn), lambda i, j, k: (k, j)),
        jax.ShapeDtypeStruct((m, n), out_dtype), pl.BlockSpec((tm, tn), lambda i, j, k: (i, j)),
        (m // tm, n // tn, kd // tk), (tm, tn), scale, rider=rider)


def mm_nt_sharded(name, a, w, *, resid=None, rider=None):
    m = a.shape[0]
    ns, n, c = w.shape
    tm, tn, _ = _tiles(m, n, c, _item(a), _item(w), 4, 0 if resid is None else _item(resid), tk=c)
    o_spec = pl.BlockSpec((tm, tn), lambda i, j, k: (i, j))
    return _mm_call(
        name, "nt", a, w, pl.BlockSpec((tm, c), lambda i, j, k: (i, k)), pl.BlockSpec((None, tn, c), lambda i, j, k: (k, j, 0)),
        jax.ShapeDtypeStruct((m, n), F32), o_spec, (m // tm, n // tn, ns), (tm, tn), 1.0, resid, o_spec, rider)


def mm_tn_sharded(name, a, b, ns, *, rider=None):
    kd, m = a.shape
    c = b.shape[1] // ns
    tm, _, tk = _tiles(m, c, kd, _item(a), _item(b), 2, tn=c)
    return _mm_call(
        name, "tn", a, b, pl.BlockSpec((tk, tm), lambda i, j, k: (k, i)), pl.BlockSpec((tk, c), lambda i, j, k: (k, j)),
        jax.ShapeDtypeStruct((ns, m, c), BF16), pl.BlockSpec((None, tm, c), lambda i, j, k: (j, i, 0)),
        (m // tm, ns, kd // tk), (tm, c), rider=rider)


def rms_fwd(name, x, g, out_dtype, rider=None):
    r, c = x.shape
    tm = _pick(r, (512, 256, 128, 64, 8))

    def body(x_ref, g_ref, y_ref, r_ref):
        xf = x_ref[...].astype(F32)
        rstd = lax.rsqrt(jnp.mean(xf * xf, axis=-1, keepdims=True) + EPS)
        y_ref[...] = ((xf * rstd) * g_ref[...]).astype(y_ref.dtype)
        r_ref[...] = rstd

    (y, rstd), rid = _call(
        name, body, (r // tm,), [pl.BlockSpec((tm, c), lambda i: (i, 0)), pl.BlockSpec((1, c), lambda i: (0, 0))], (x, g.reshape(1, c)),
        [pl.BlockSpec((tm, c), lambda i: (i, 0)), pl.BlockSpec((tm, 1), lambda i: (i, 0))],
        [jax.ShapeDtypeStruct((r, c), out_dtype), jax.ShapeDtypeStruct((r, 1), F32)], (), ("parallel",), rider)
    return (y, rstd) if rider is None else (y, rstd, rid)


def rms_bwd(name, x, g, rstd, dy, dres=None, rider=None, also_bf16=False):
    r, c = x.shape
    tm = _pick(r, (512, 256, 128, 64, 8))
    has_res = dres is not None

    def body(*refs):
        x_ref, g_ref, r_ref, dy_ref = refs[:4]
        dres_ref = refs[4] if has_res else None
        dx_ref, dg_ref = refs[4 + has_res:6 + has_res]
        xhat = x_ref[...].astype(F32) * r_ref[...]
        dyf = dy_ref[...].astype(F32)
        gdy = dyf * g_ref[...]
        dx = r_ref[...] * (gdy - xhat * jnp.mean(gdy * xhat, axis=-1, keepdims=True))
        if has_res:
            dx = dx + dres_ref[...]
        dx_ref[...] = dx
        if also_bf16:
            refs[-1][...] = dx.astype(BF16)

        @pl.when(pl.program_id(0) == 0)
        def _():
            dg_ref[...] = jnp.zeros_like(dg_ref)

        dg_ref[...] += jnp.sum(dyf * xhat, axis=0, keepdims=True)

    row = pl.BlockSpec((tm, c), lambda i: (i, 0))
    in_specs = [row, pl.BlockSpec((1, c), lambda i: (0, 0)), pl.BlockSpec((tm, 1), lambda i: (i, 0)), row] + ([row] if has_res else [])
    args = (x, g.reshape(1, c), rstd, dy) + ((dres,) if has_res else ())
    outs, rid = _call(name, body, (r // tm,), in_specs, args, [row, pl.BlockSpec((1, c), lambda i: (0, 0))] + [row] * also_bf16,
                      [jax.ShapeDtypeStruct((r, c), F32), jax.ShapeDtypeStruct((1, c), F32)] + [jax.ShapeDtypeStruct((r, c), BF16)] * also_bf16,
                      (), ("arbitrary",), rider)
    return (outs[0], outs[1].reshape(c), *outs[2:], *([] if rider is None else [rid]))


_LANES = 128


def _head_mean(v):
    if v.shape[1] == HEAD_DIM:
        return jnp.mean(v, axis=-1, keepdims=True)
    low = lax.broadcasted_iota(jnp.int32, v.shape, 1) < HEAD_DIM
    lo = jnp.sum(jnp.where(low, v, 0.0), axis=-1, keepdims=True)
    hi = jnp.sum(jnp.where(low, 0.0, v), axis=-1, keepdims=True)
    return jnp.where(low, lo, hi) * (1.0 / HEAD_DIM)


def _head_groups(c):
    width = _LANES if c % _LANES == 0 else HEAD_DIM
    assert c % width == 0, c
    return width, [slice(k * width, (k + 1) * width) for k in range(c // width)]


def _head_gain(g, width):
    return jnp.tile(g.reshape(1, HEAD_DIM), (1, width // HEAD_DIM))


def _rotate_half(y):
    half = HEAD_DIM // 2
    first = lax.broadcasted_iota(jnp.int32, y.shape, 1) % HEAD_DIM < half
    return jnp.where(first, -pltpu.roll(y, y.shape[1] - half, axis=1), pltpu.roll(y, half, axis=1))


def _rope_tables(rope, width):
    return [jnp.tile(t, (1, 2 * width // HEAD_DIM)) for t in rope]


def head_rms_fwd(name, x, g, rope=None):
    s, c = x.shape
    tm = _pick(s, (256, 128, 8))
    width, groups = _head_groups(c)

    def body(x_ref, g_ref, *refs):
        y_ref = refs[-1]
        for sl in groups:
            xs = x_ref[:, sl]
            y = (xs * lax.rsqrt(_head_mean(xs * xs) + EPS)) * g_ref[...]
            if rope:
                y = y * refs[0][...] + _rotate_half(y) * refs[1][...]
            y_ref[:, sl] = y

    row = pl.BlockSpec((tm, c), lambda i: (i, 0))
    tab = pl.BlockSpec((tm, width), lambda i: (i, 0))
    tables = _rope_tables(rope, width) if rope else []
    (y,), _ = _call(name, body, (s // tm,), [row, pl.BlockSpec((1, width), lambda i: (0, 0))] + [tab] * len(tables),
                    (x, _head_gain(g, width), *tables), [row], [jax.ShapeDtypeStruct((s, c), F32)], (), ("parallel",))
    return y


def head_rms_bwd(name, x, g, dy, rope=None):
    s, c = x.shape
    tm = _pick(s, (256, 128, 8))
    width, groups = _head_groups(c)

    def body(x_ref, g_ref, dy_ref, *refs):
        dx_ref, dg_ref = refs[-2:]

        @pl.when(pl.program_id(0) == 0)
        def _():
            dg_ref[...] = jnp.zeros_like(dg_ref)

        for sl in groups:
            xs, dys = x_ref[:, sl], dy_ref[:, sl]
            if rope:
                dys = dys * refs[0][...] - _rotate_half(dys * refs[1][...])
            rstd = lax.rsqrt(_head_mean(xs * xs) + EPS)
            xhat = xs * rstd
            gdy = dys * g_ref[...]
            dx_ref[:, sl] = rstd * (gdy - xhat * _head_mean(gdy * xhat))
            dg_ref[...] += jnp.sum(dys * xhat, axis=0, keepdims=True)

    row = pl.BlockSpec((tm, c), lambda i: (i, 0))
    vec = pl.BlockSpec((1, width), lambda i: (0, 0))
    tab = pl.BlockSpec((tm, width), lambda i: (i, 0))
    tables = _rope_tables(rope, width) if rope else []
    (dx, dg), _ = _call(name, body, (s // tm,), [row, vec, row] + [tab] * len(tables), (x, _head_gain(g, width), dy, *tables), [row, vec],
                        [jax.ShapeDtypeStruct((s, c), F32), jax.ShapeDtypeStruct((1, width), F32)], (), ("arbitrary",))
    return dx, jnp.sum(dg.reshape(width // HEAD_DIM, HEAD_DIM), axis=0)


@functools.partial(jax.custom_vjp, nondiff_argnums=(0,))
def head_rms(name, x, g):
    return head_rms_fwd(name + "_fwd", x, g)


def _head_rms_fwd(name, x, g):
    return head_rms_fwd(name + "_fwd", x, g), (x, g)


def _head_rms_bwd(name, res, dy):
    return head_rms_bwd(name + "_bwd", *res, dy)


head_rms.defvjp(_head_rms_fwd, _head_rms_bwd)


@functools.partial(jax.custom_vjp, nondiff_argnums=(0,))
def head_rms_rope(name, x, g, cos, sin):
    return head_rms_fwd(name + "_fwd", x, g, (cos, sin))


def _head_rms_rope_fwd(name, x, g, cos, sin):
    return head_rms_fwd(name + "_fwd", x, g, (cos, sin)), (x, g, cos, sin)


def _head_rms_rope_bwd(name, res, dy):
    x, g, cos, sin = res
    return (*head_rms_bwd(name + "_bwd", x, g, dy, (cos, sin)), jnp.zeros_like(cos), jnp.zeros_like(sin))


head_rms_rope.defvjp(_head_rms_rope_fwd, _head_rms_rope_bwd)


def _cumsum_call(name, a, reverse):
    h, s = a.shape
    tb = _LANES
    assert s % tb == 0

    def body(a_ref, o_ref):
        t_in = lax.broadcasted_iota(jnp.int32, (tb, tb), 0)
        t_out = lax.broadcasted_iota(jnp.int32, (tb, tb), 1)
        tri = jnp.where((t_in >= t_out) if reverse else (t_in <= t_out), 1.0, 0.0).astype(BF16)
        carry = jnp.zeros((h, 1), F32)
        blocks = range(s // tb)
        for b in (reversed(blocks) if reverse else blocks):
            cols = slice(b * tb, (b + 1) * tb)
            block = a_ref[:, cols]
            rest, local = block, jnp.zeros((h, tb), F32)
            for _ in range(3):
                piece = rest.astype(BF16)
                local = local + jnp.dot(piece, tri, preferred_element_type=F32)
                rest = rest - piece.astype(F32)
            o_ref[:, cols] = local + carry
            carry = carry + jnp.sum(block, axis=1, keepdims=True)

    whole = pl.BlockSpec((h, s), lambda j: (0, 0))
    (out,), _ = _call(name, body, (1,), [whole], (a,), [whole], [jax.ShapeDtypeStruct((h, s), F32)], (), ("arbitrary",))
    return out


@jax.custom_vjp
def time_cumsum(a):
    return _cumsum_call("gate_cumsum", a, False)


def _time_cumsum_fwd(a):
    return _cumsum_call("gate_cumsum", a, False), None


def _time_cumsum_bwd(_, dc):
    return (_cumsum_call("gate_cumsum_bwd", dc, True),)


time_cumsum.defvjp(_time_cumsum_fwd, _time_cumsum_bwd)


FFN_TM = 512


def _sigmoid(x):
    return 1.0 / (1.0 + jnp.exp(-x))


def ffn_gu(name, xn, wg, wu, rider=None):
    s, d = xn.shape
    ns, _, c = wg.shape
    tm = _pick(s, (FFN_TM, 128))

    def body(x_ref, wg_ref, wu_ref, h_ref, a_ref, b_ref):
        xb = x_ref[...]
        gv = jnp.dot(xb, wg_ref[...], preferred_element_type=F32)
        uv = jnp.dot(xb, wu_ref[...], preferred_element_type=F32)
        sig = _sigmoid(gv)
        silu = gv * sig
        h_ref[...] = (silu * uv).astype(BF16)
        a_ref[...] = (uv * (sig * (1.0 + gv * (1.0 - sig)))).astype(BF16)
        b_ref[...] = silu.astype(BF16)

    w_spec = pl.BlockSpec((None, d, c), lambda j, i: (j, 0, 0))
    o_spec = pl.BlockSpec((tm, c), lambda j, i: (i, j))
    return _call(
        name, body, (ns, s // tm), [pl.BlockSpec((tm, d), lambda j, i: (i, 0)), w_spec, w_spec], (xn, wg, wu),
        [o_spec, o_spec, o_spec], [jax.ShapeDtypeStruct((s, ns * c), BF16)] * 3, [], ("parallel", "parallel"), rider)


def ffn_dh(name, dy, wd, dh_dg, dh_du, ns, scale, rider=None):
    s, d = dy.shape
    f = wd.shape[0]
    c = f // ns
    tm = _pick(s, (FFN_TM, 128))

    def body(dy_ref, wd_ref, a_ref, b_ref, dg_ref, du_ref):
        dh = lax.dot_general(dy_ref[...].astype(BF16), wd_ref[...], _DIMS["nt"], preferred_element_type=F32) * scale
        dg_ref[...] = (dh * a_ref[...].astype(F32)).astype(BF16)
        du_ref[...] = (dh * b_ref[...].astype(F32)).astype(BF16)

    o_spec = pl.BlockSpec((tm, c), lambda j, i: (i, j))
    return _call(
        name, body, (ns, s // tm),
        [pl.BlockSpec((tm, d), lambda j, i: (i, 0)), pl.BlockSpec((c, d), lambda j, i: (j, 0)), o_spec, o_spec], (dy, wd, dh_dg, dh_du),
        [o_spec, o_spec], [jax.ShapeDtypeStruct((s, f), BF16), jax.ShapeDtypeStruct((s, f), BF16)],
        [], ("parallel", "parallel"), rider)


FOX_TQ = 1024


def fox_tile(s_len):
    return min(FOX_TQ, s_len)


def _heads_per_block(h):
    return 2 if h % 2 == 0 else 1


def _fox_queries(q):
    return (q * (HEAD_DIM ** -0.5)).astype(BF16)


def _fox_scores(qs, kc, cq, ck, diagonal):
    s = lax.dot_general(qs, kc.astype(BF16), _DIMS["nt"], preferred_element_type=F32) + cq - ck
    if not diagonal:
        return s
    return jnp.where(lax.broadcasted_iota(jnp.int32, s.shape, 0) >= lax.broadcasted_iota(jnp.int32, s.shape, 1), s, MASK_VALUE)


def _fox_specs(h, s_len, tq):
    hb = _heads_per_block(h)
    qb = pl.BlockSpec((tq, hb * HEAD_DIM), lambda pp, i: (i, pp))
    kb = pl.BlockSpec((s_len, hb * HEAD_DIM), lambda pp, i: (0, pp))
    colb = pl.BlockSpec((hb, tq, 1), lambda pp, i: (pp, i, 0))
    rowb = pl.BlockSpec((hb, s_len // tq, 1, tq), lambda pp, i: (pp, 0, 0, 0))
    return hb, qb, kb, colb, rowb


def fox_fwd(q, k, v, cq, ck, rider=None):
    s_len, hd = q.shape
    h, d = hd // HEAD_DIM, HEAD_DIM
    tq = fox_tile(s_len)
    hb, qb, kb, colb, rowb = _fox_specs(h, s_len, tq)

    def body(q_ref, k_ref, v_ref, cq_ref, ck_ref, o_ref, lse_ref):
        i = pl.program_id(1)
        for hh in range(hb):
            lanes = slice(hh * d, (hh + 1) * d)
            qs, cqv = _fox_queries(q_ref[:, lanes]), cq_ref[hh]

            def chunk(c, carry, diagonal=False):
                m, l, acc = carry
                rows = pl.ds(pl.multiple_of(c * tq, tq), tq)
                s = _fox_scores(qs, k_ref[rows, lanes], cqv, ck_ref[hh, c], diagonal)
                m_new = jnp.maximum(m, jnp.max(s, axis=-1, keepdims=True))
                alpha = jnp.exp(m - m_new)
                p = jnp.exp(s - m_new)
                acc = alpha * acc + jnp.dot(p.astype(BF16), v_ref[rows, lanes].astype(BF16), preferred_element_type=F32)
                return m_new, alpha * l + jnp.sum(p, axis=-1, keepdims=True), acc

            init = (jnp.full((tq, 1), MASK_VALUE, F32), jnp.zeros((tq, 1), F32), jnp.zeros((tq, d), F32))
            m, l, acc = chunk(i, lax.fori_loop(0, i, chunk, init), diagonal=True)
            o_ref[:, lanes] = acc / l
            lse_ref[hh] = m + jnp.log(l)

    return _call(
        "fox_fwd", body, (h // hb, s_len // tq), [qb, kb, kb, colb, rowb], (q, k, v, cq, ck), [qb, colb],
        [jax.ShapeDtypeStruct((s_len, hd), F32), jax.ShapeDtypeStruct((h, s_len, 1), F32)], (), ("parallel", "parallel"), rider)


def fox_bwd(q, k, v, cq, ck, o, lse, do, rider=None):
    s_len, hd = q.shape
    h, d = hd // HEAD_DIM, HEAD_DIM
    tq = fox_tile(s_len)
    scale = HEAD_DIM ** -0.5
    hb, qb, kb, colb, rowb = _fox_specs(h, s_len, tq)

    def body(q_ref, k_ref, v_ref, cq_ref, ck_ref, o_ref, lse_ref, do_ref, dq_ref, dk_ref, dv_ref, dcq_ref, dck_ref):
        i = pl.program_id(1)

        @pl.when(i == 0)
        def _():
            dk_ref[...] = jnp.zeros_like(dk_ref)
            dv_ref[...] = jnp.zeros_like(dv_ref)
            dck_ref[...] = jnp.zeros_like(dck_ref)

        heads = []
        for hh in range(hb):
            lanes = slice(hh * d, (hh + 1) * d)
            dof = do_ref[:, lanes]
            heads.append((lanes, _fox_queries(q_ref[:, lanes]), cq_ref[hh], lse_ref[hh], dof.astype(BF16),
                          jnp.sum(dof * o_ref[:, lanes], axis=-1, keepdims=True)))

        def chunk(c, carry, diagonal=False):
            rows = pl.ds(pl.multiple_of(c * tq, tq), tq)
            out, dks, dvs = [], [], []
            for hh, (lanes, qs, cqv, lse_h, dob, delta) in enumerate(heads):
                dq, dcq = carry[hh]
                kc = k_ref[rows, lanes]
                p = jnp.exp(_fox_scores(qs, kc, cqv, ck_ref[hh, c], diagonal) - lse_h)
                dp = lax.dot_general(dob, v_ref[rows, lanes].astype(BF16), _DIMS["nt"], preferred_element_type=F32)
                ds = p * (dp - delta)
                dsb = ds.astype(BF16)
                dvs.append(lax.dot_general(p.astype(BF16), dob, _DIMS["tn"], preferred_element_type=F32))
                dks.append(lax.dot_general(dsb, qs, _DIMS["tn"], preferred_element_type=F32))
                dck_ref[hh, c] -= jnp.sum(ds, axis=0, keepdims=True)
                out.append((dq + jnp.dot(dsb, kc.astype(BF16), preferred_element_type=F32), dcq + jnp.sum(ds, axis=-1, keepdims=True)))
            dk_ref[rows, :] += jnp.concatenate(dks, axis=1)
            dv_ref[rows, :] += jnp.concatenate(dvs, axis=1)
            return tuple(out)

        init = tuple((jnp.zeros((tq, d), F32), jnp.zeros((tq, 1), F32)) for _ in range(hb))
        done = chunk(i, lax.fori_loop(0, i, chunk, init), diagonal=True)
        dq_ref[...] = jnp.concatenate([dq for dq, _ in done], axis=1) * scale
        for hh, (_, dcq) in enumerate(done):
            dcq_ref[hh] = dcq

    return _call(
        "fox_bwd", body, (h // hb, s_len // tq), [qb, kb, kb, colb, rowb, qb, colb, qb], (q, k, v, cq, ck, o, lse, do),
        [qb, kb, kb, colb, rowb],
        [jax.ShapeDtypeStruct((s_len, hd), F32)] * 3
        + [jax.ShapeDtypeStruct((h, s_len, 1), F32), jax.ShapeDtypeStruct((h, s_len // tq, 1, tq), F32)],
        (), ("parallel", "arbitrary"), rider)


def _stack_heads(ref, first, g):
    return jnp.concatenate([ref[:, (first + j) * HEAD_DIM:(first + j + 1) * HEAD_DIM] for j in range(g)], axis=0)


def _window(prev_ref, cur_ref, hh):
    lanes = slice(hh * HEAD_DIM, (hh + 1) * HEAD_DIM)
    return jnp.concatenate([prev_ref[:, lanes], cur_ref[:, lanes]], axis=0).astype(BF16)


def _swa_band(g, w):
    t = lax.broadcasted_iota(jnp.int32, (g * w, 2 * w), 0) % w
    col = lax.broadcasted_iota(jnp.int32, (g * w, 2 * w), 1)
    rel = t + w - col
    band = (rel >= 0) & (rel < w)
    return jnp.where(jnp.stack([band & (col >= w), band]), 0.0, MASK_VALUE).astype(F32)


def _swa_probs(qs, kw, sink, band):
    s = lax.dot_general(qs, kw, _DIMS["nt"], preferred_element_type=F32) + band
    m = jnp.maximum(jnp.max(s, axis=-1, keepdims=True), sink)
    p = jnp.exp(s - m)
    ps = jnp.exp(sink - m)
    linv = 1.0 / (jnp.sum(p, axis=-1, keepdims=True) + ps)
    return p * linv, ps * linv


def _swa_specs(hk, g, s_len):
    w = WINDOW
    assert s_len % w == 0
    hb = _heads_per_block(hk)
    qb = pl.BlockSpec((w, hb * g * HEAD_DIM), lambda pp, n: (n, pp))
    prev = pl.BlockSpec((w, hb * HEAD_DIM), lambda pp, n: (jnp.maximum(n - 1, 0), pp))
    cur = pl.BlockSpec((w, hb * HEAD_DIM), lambda pp, n: (n, pp))
    sb = pl.BlockSpec((hb, g * w, 1), lambda pp, n: (pp, 0, 0))
    band = pl.BlockSpec((None, g * w, 2 * w), lambda pp, n: (jnp.minimum(n, 1), 0, 0))
    return hb, qb, prev, cur, sb, band


def swa_fwd(q, k, v, sink, rider=None):
    s_len = q.shape[0]
    hk = k.shape[1] // HEAD_DIM
    g = q.shape[1] // k.shape[1]
    w, d = WINDOW, HEAD_DIM
    hb, qb, prev, cur, sb, bandb = _swa_specs(hk, g, s_len)

    def body(q_ref, kp_ref, kc_ref, vp_ref, vc_ref, sink_ref, band_ref, o_ref):
        for hh in range(hb):
            qs = (_stack_heads(q_ref, hh * g, g) * (HEAD_DIM ** -0.5)).astype(BF16)
            p, _ = _swa_probs(qs, _window(kp_ref, kc_ref, hh), sink_ref[hh], band_ref[...])
            o = jnp.dot(p.astype(BF16), _window(vp_ref, vc_ref, hh), preferred_element_type=F32)
            for j in range(g):
                o_ref[:, (hh * g + j) * d:(hh * g + j + 1) * d] = o[j * w:(j + 1) * w]

    (o,), rid = _call("swa_fwd", body, (hk // hb, s_len // w), [qb, prev, cur, prev, cur, sb, bandb],
                      (q, k, k, v, v, sink, _swa_band(g, w)), [qb], [jax.ShapeDtypeStruct(q.shape, F32)], (),
                      ("parallel", "parallel"), rider)
    return o, rid


def swa_bwd(q, k, v, sink, o, do, rider=None):
    s_len = q.shape[0]
    hk = k.shape[1] // HEAD_DIM
    g = q.shape[1] // k.shape[1]
    w, d = WINDOW, HEAD_DIM
    scale = HEAD_DIM ** -0.5
    hb, qb, prev, cur, sb, bandb = _swa_specs(hk, g, s_len)

    def body(q_ref, kp_ref, kc_ref, vp_ref, vc_ref, sink_ref, band_ref, o_ref, do_ref, dq_ref, dkp_ref, dkc_ref, dvp_ref, dvc_ref,
             dsink_ref):
        @pl.when(pl.program_id(1) == 0)
        def _():
            dsink_ref[...] = jnp.zeros_like(dsink_ref)

        for hh in range(hb):
            lanes = slice(hh * d, (hh + 1) * d)
            qs = (_stack_heads(q_ref, hh * g, g) * scale).astype(BF16)
            kw, vw = _window(kp_ref, kc_ref, hh), _window(vp_ref, vc_ref, hh)
            p, ps = _swa_probs(qs, kw, sink_ref[hh], band_ref[...])
            dof = _stack_heads(do_ref, hh * g, g)
            dob = dof.astype(BF16)
            delta = jnp.sum(dof * _stack_heads(o_ref, hh * g, g), axis=-1, keepdims=True)
            dp = lax.dot_general(dob, vw, _DIMS["nt"], preferred_element_type=F32)
            ds = p * (dp - delta)
            dsb = ds.astype(BF16)
            dsink_ref[hh] -= ps * delta
            dq = jnp.dot(dsb, kw, preferred_element_type=F32) * scale
            for j in range(g):
                dq_ref[:, (hh * g + j) * d:(hh * g + j + 1) * d] = dq[j * w:(j + 1) * w]
            dkw = lax.dot_general(dsb, qs, _DIMS["tn"], preferred_element_type=F32)
            dvw = lax.dot_general(p.astype(BF16), dob, _DIMS["tn"], preferred_element_type=F32)
            dkp_ref[:, lanes] = dkw[:w]
            dkc_ref[:, lanes] = dkw[w:]
            dvp_ref[:, lanes] = dvw[:w]
            dvc_ref[:, lanes] = dvw[w:]

    kv_shape = jax.ShapeDtypeStruct(k.shape, F32)
    (dq, dkp, dkc, dvp, dvc, dsink), rid = _call(
        "swa_bwd", body, (hk // hb, s_len // w), [qb, prev, cur, prev, cur, sb, bandb, qb, qb],
        (q, k, k, v, v, sink, _swa_band(g, w), o, do),
        [qb, cur, cur, cur, cur, sb],
        [jax.ShapeDtypeStruct(q.shape, F32), kv_shape, kv_shape, kv_shape, kv_shape, jax.ShapeDtypeStruct((hk, g * w, 1), F32)],
        (), ("parallel", "arbitrary"), rider)

    def shift_up(a):
        return jnp.concatenate([a[w:], jnp.zeros_like(a[:w])], axis=0)

    return (dq, dkc + shift_up(dkp), dvc + shift_up(dvp), dsink), rid


def loss_call(y, target):
    s, d = y.shape
    tm = _pick(s, (512, 256, 128))

    def body(y_ref, t_ref, l_ref, dy_ref, dyb_ref):
        e = y_ref[...] - t_ref[...]
        dy = e * (1.0 / d)
        dy_ref[...] = dy
        dyb_ref[...] = dy.astype(BF16)

        @pl.when(pl.program_id(0) == 0)
        def _():
            l_ref[...] = jnp.zeros_like(l_ref)

        l_ref[...] += jnp.sum(jnp.sum(e * e, axis=0, keepdims=True), axis=1, keepdims=True) * (0.5 / d)

    row = pl.BlockSpec((tm, d), lambda i: (i, 0))
    (l, dy, dyb), _ = _call("loss_head", body, (s // tm,), [row, row], (y, target), [pl.BlockSpec((1, 1), lambda i: (0, 0)), row, row],
                            [jax.ShapeDtypeStruct((1, 1), F32), jax.ShapeDtypeStruct((s, d), F32), jax.ShapeDtypeStruct((s, d), BF16)],
                            (), ("arbitrary",))
    return l[0, 0], dy, dyb


def _row_tile(rows, cols, itemsize, block_bytes=1 << 20):
    target = max(16, block_bytes // (cols * itemsize))
    fits = [t for t in range(16, rows + 1, 16) if rows % t == 0 and t <= target]
    return fits[-1] if fits else rows


CAST_STEPS = 8


def cast_place(name, ws, p_idx, rider=None):
    n = len(ws)
    assert all(w.shape[0] % (16 * CAST_STEPS) == 0 for w in ws), [w.shape for w in ws]

    def body(p_ref, *refs):
        for w_ref, o_ref in zip(refs[:n], refs[n:]):
            o_ref[...] = w_ref[...].astype(BF16)

    return _call(
        name, body, (CAST_STEPS,), [pl.BlockSpec((w.shape[0] // CAST_STEPS, w.shape[1]), lambda i, pr: (i, 0)) for w in ws], tuple(ws),
        [pl.BlockSpec((None, w.shape[0] // CAST_STEPS, w.shape[1]), lambda i, pr: (pr[0], i, 0)) for w in ws],
        [jax.ShapeDtypeStruct((N_CHIPS,) + w.shape, BF16) for w in ws], (), ("parallel",), rider, prefetch=(p_idx,))


def chip_sum(name, grad, theirs, c_idx):
    ns, r, cols = grad.shape
    rh = r // 2
    tr = _row_tile(rh, cols, 2, 2 << 20)
    nb = rh // tr

    def body(c_ref, a_ref, b_ref, o_ref):
        o_ref[...] = (a_ref[...].astype(F32) + b_ref[...].astype(F32)).astype(o_ref.dtype)

    return pl.pallas_call(
        body, name=name,
        grid_spec=pltpu.PrefetchScalarGridSpec(
            num_scalar_prefetch=1, grid=(ns, nb),
            in_specs=[pl.BlockSpec((None, tr, cols), lambda q, i, cr: (q, cr[0] * nb + i, 0)),
                      pl.BlockSpec((None, tr, cols), lambda q, i, cr: (q, i, 0))],
            out_specs=pl.BlockSpec((None, tr, cols), lambda q, i, cr: (q, i, 0))),
        out_shape=jax.ShapeDtypeStruct((ns, rh, cols), BF16),
        compiler_params=pltpu.CompilerParams(dimension_semantics=("parallel", "parallel"), vmem_limit_bytes=VMEM_LIMIT),
    )(c_idx, grad, theirs)


def owner_sum(name, sums, got, pc_idx):
    ns, rh, cols = sums.shape
    tr = _row_tile(rh, cols, 4, 2 << 20)
    nb = rh // tr

    def body(pc_ref, a_ref, b_ref, o_ref):
        o_ref[...] = ((a_ref[...].astype(F32) + b_ref[0].astype(F32)) + b_ref[1].astype(F32)) + b_ref[2].astype(F32)

    return pl.pallas_call(
        body, name=name,
        grid_spec=pltpu.PrefetchScalarGridSpec(
            num_scalar_prefetch=1, grid=(nb,),
            in_specs=[pl.BlockSpec((None, tr, cols), lambda i, pc: (pc[0], i, 0)),
                      pl.BlockSpec((3, tr, cols), lambda i, pc: (0, i, 0))],
            out_specs=pl.BlockSpec((tr, cols), lambda i, pc: (pc[1] * nb + i, 0))),
        out_shape=jax.ShapeDtypeStruct((2 * rh, cols), F32),
        compiler_params=pltpu.CompilerParams(dimension_semantics=("parallel",), vmem_limit_bytes=VMEM_LIMIT),
    )(pc_idx, sums, got)


def adamw(name, w, g, m, v):
    r, cols = w.shape
    tr = _row_tile(r, cols, 4)
    c1 = 1.0 / (1.0 - ADAM_B1 ** ADAM_STEP)
    c2 = 1.0 / (1.0 - ADAM_B2 ** ADAM_STEP)

    def body(w_ref, g_ref, m_ref, v_ref, go_ref, d_ref, nm_ref, nv_ref):
        gv = g_ref[...]
        nm = ADAM_B1 * m_ref[...] + (1.0 - ADAM_B1) * gv
        nv = ADAM_B2 * v_ref[...] + (1.0 - ADAM_B2) * (gv * gv)
        go_ref[...] = gv
        d_ref[...] = -ADAM_LR * ((nm * c1) / (jnp.sqrt(nv * c2) + ADAM_EPS) + ADAM_WD * w_ref[...])
        nm_ref[...] = nm
        nv_ref[...] = nv

    blk = pl.BlockSpec((tr, cols), lambda i: (i, 0))
    return _call(name, body, (r // tr,), [blk] * 4, (w, g, m, v), [blk] * 4, [jax.ShapeDtypeStruct((r, cols), F32)] * 4, (), ("parallel",))


def _win_layout(d_model):
    hf = hq = d_model // (2 * HEAD_DIM)
    hk = hq // 4
    sizes = [hf * HEAD_DIM, hf * HEAD_DIM, hf * HEAD_DIM, hf, hq * HEAD_DIM, hk * HEAD_DIM, hk * HEAD_DIM]
    return hf, hq, hk, sizes


class WinPlan:
    def __init__(self, d_model, ns=N_CHIPS):
        self.hf, self.hq, self.hk, self.sizes = _win_layout(d_model)
        self.ns, self.cs = ns, sum(self.sizes) // ns
        self.jump_at = sum(self.sizes[:4])
        self.jump_by = -self.jump_at % _LANES
        self.base = [self.pos(s * self.cs) // _LANES * _LANES for s in range(ns)]
        ends = [self.pos((s + 1) * self.cs - 1) + 1 - self.base[s] for s in range(ns)]
        self.width = -(-max(ends) // _LANES) * _LANES
        self.total = -(-max(b + self.width for b in self.base) // 1024) * 1024
        starts = [0]
        for sz in self.sizes:
            starts.append(starts[-1] + sz)
        self.segments = [(self.pos(a), sz) for a, sz in zip(starts, self.sizes)]

    def pos(self, g):
        return g if g < self.jump_at else g + self.jump_by

    def pieces(self, s):
        g0, g1 = s * self.cs, (s + 1) * self.cs
        cuts = [g0] + ([self.jump_at] if g0 < self.jump_at < g1 else []) + [g1]
        return [(a - g0, b - a, self.pos(a) - self.base[s]) for a, b in zip(cuts[:-1], cuts[1:])]

    def place(self, w, s):
        parts, at = [], 0
        for t0, n, j0 in self.pieces(s):
            parts += [jnp.zeros((w.shape[0], j0 - at), w.dtype), w[:, t0:t0 + n]]
            at = j0 + n
        return jnp.concatenate(parts + [jnp.zeros((w.shape[0], self.width - at), w.dtype)], axis=1)

    def unplace(self, slab, s):
        return jnp.concatenate([slab[:, j0:j0 + n] for _, n, j0 in self.pieces(s)], axis=1)

    def assemble(self, slabs):
        return sum(jnp.pad(slabs[s], ((0, 0), (b, self.total - b - self.width))) for s, b in enumerate(self.base))

    def split(self, full):
        return jnp.stack([full[:, b:b + self.width] for b in self.base])


def _attn_inputs(proj, sm, positions):
    s_len = proj.shape[0]
    plan = WinPlan(sm["norm_mix_g"].shape[0])
    hf, hq, hk = plan.hf, plan.hq, plan.hk
    grp = hq // hk
    q_f, k_f, v_f, f_logit, q_s, k_s, v_s = [proj[:, a:a + n] for a, n in plan.segments]

    q_f = head_rms("fox_qnorm", q_f, sm["fox_q_norm_g"])
    k_f = head_rms("fox_knorm", k_f, sm["fox_k_norm_g"])
    log_f = jax.nn.log_sigmoid(f_logit + sm["b_forget"])
    c = time_cumsum(log_f.T)

    inv_freq = ROPE_THETA ** (-jnp.arange(0, HEAD_DIM, 2, dtype=F32) / HEAD_DIM)
    ang = positions.astype(F32)[:, None] * inv_freq
    cos, sin = jnp.cos(ang), jnp.sin(ang)
    q_s = head_rms_rope("swa_qnorm", q_s, sm["swa_q_norm_g"], cos, sin)
    k_s = head_rms_rope("swa_knorm", k_s, sm["swa_k_norm_g"], cos, sin)
    sink = jnp.broadcast_to(sm["swa_sinks"].reshape(hk, grp, 1, 1), (hk, grp, WINDOW, 1)).reshape(hk, grp * WINDOW, 1)
    tq = fox_tile(s_len)
    return (q_f, k_f, v_f, c[:, :, None], c.reshape(hf, s_len // tq, 1, tq)), (q_s, k_s, v_s, sink)


_BIG = ("ffn1_w_gate", "ffn1_w_up", "ffn1_w_down", "w_in", "w_out", "ffn2_w_gate", "ffn2_w_up", "ffn2_w_down")
_SMALL = ("norm_ffn1_g", "norm_mix_g", "b_forget", "fox_q_norm_g", "fox_k_norm_g", "swa_q_norm_g", "swa_k_norm_g", "swa_sinks",
          "out_norm_fox_g", "out_norm_swa_g", "norm_ffn2_g")
_ATTN_SMALL = ("norm_mix_g", "b_forget", "fox_q_norm_g", "fox_k_norm_g", "swa_q_norm_g", "swa_k_norm_g", "swa_sinks")
_ALL = ("norm_ffn1_g", "ffn1_w_gate", "ffn1_w_up", "ffn1_w_down", "norm_mix_g", "w_in", "b_forget", "fox_q_norm_g", "fox_k_norm_g",
        "swa_q_norm_g", "swa_k_norm_g", "swa_sinks", "out_norm_fox_g", "out_norm_swa_g", "w_out", "norm_ffn2_g", "ffn2_w_gate",
        "ffn2_w_up", "ffn2_w_down")


def _pack_small(d):
    parts = []
    for k in _SMALL:
        v = d[k].reshape(-1)
        rows = -(-v.shape[0] // _LANES)
        parts.append(jnp.pad(v, (0, rows * _LANES - v.shape[0])).reshape(rows, _LANES))
    a = jnp.concatenate(parts, axis=0)
    return jnp.pad(a, ((0, -a.shape[0] % 8), (0, 0)))


def _unpack_small(a, like):
    out, r0 = {}, 0
    for k in _SMALL:
        nvals = like[k].shape[1]
        rows = -(-nvals // _LANES)
        out[k] = a[r0:r0 + rows].reshape(-1)[:nvals].reshape(1, nvals)
        r0 += rows
    return out


def _stacked(w):
    return w.reshape(-1, w.shape[-1])


def _local_step(shards, sm, x, positions, target, p_idx, c_idx, pc_idx):
    ns = N_CHIPS
    full = {}

    def fetch(*jobs):
        names = list(dict.fromkeys(n for n, _, _ in jobs))
        return names, gather([bufs[n] for n in names], [(names.index(n), kind, part) for n, kind, part in jobs])

    def take(names, rid):
        for n, b in zip(names, rid[0]):
            bufs[n] = b

    n1 = ["ffn1_w_gate", "ffn1_w_up", "ffn1_w_down"]
    n2 = ["ffn2_w_gate", "ffn2_w_up", "ffn2_w_down"]
    later = ["w_in", "w_out"] + n2
    placed, _ = cast_place("cast_place_ffn1", [shards[n] for n in n1], p_idx)
    bufs = dict(zip(n1, placed))
    gate1, up1, down1 = n1
    gate2, up2, down2 = n2
    names, rider = fetch((gate1, "ici", WHOLE), (up1, "ici", WHOLE))
    placed, rid = cast_place("cast_place_later", [shards[n] for n in later], p_idx, rider=rider)
    bufs.update(zip(later, placed))
    take(names, rid)
    names, rider = fetch((gate1, "d2d", WHOLE), (up1, "d2d", WHOLE), (down1, "ici", (0, 1, 4)))
    xn1, r1, rid = rms_fwd("ffn1_norm", x, sm["norm_ffn1_g"], BF16, rider=rider)
    take(names, rid)
    names, rider = fetch((down1, "ici", (1, 4, 4)), ("w_in", "ici", (0, 1, 4)))
    (hid1, hdg1, hdu1), rid = ffn_gu("ffn1_gu", xn1, bufs[gate1], bufs[up1], rider=rider)
    take(names, rid)
    names, rider = fetch((down1, "d2d", WHOLE))
    take(names, run_step("gather_d2d_ffn1_down", rider))
    wd1 = _stacked(bufs[down1])
    names, rider = fetch(("w_in", "ici", (1, 4, 4)))
    h1, rid = mm_nn("ffn1_down", hid1, wd1, scale=0.5, resid=x, rider=rider)
    take(names, rid)

    names, rider = fetch(("w_in", "d2d", WHOLE))
    u, r_mix, rid = rms_fwd("mix_norm", h1, sm["norm_mix_g"], BF16, rider=rider)
    take(names, rid)
    names, rider = fetch(("w_out", "ici", WHOLE))
    plan = WinPlan(x.shape[1])
    win = plan.assemble(bufs["w_in"])
    proj, rid = mm_nn("mix_inproj", u, win, rider=rider)
    take(names, rid)
    sm_attn = {k: sm[k] for k in _ATTN_SMALL}
    (fox_in, swa_in), attn_vjp = jax.vjp(lambda pr, s: _attn_inputs(pr, s, positions), proj, sm_attn)
    names, rider = fetch((gate2, "ici", WHOLE), (up2, "ici", (0, 1, 4)), ("w_out", "d2d", WHOLE))
    (o_f, lse), rid = fox_fwd(*fox_in, rider=rider)
    take(names, rid)
    names, rider = fetch((up2, "ici", (1, 4, 4)), (gate2, "d2d", WHOLE), (up2, "d2d", (0, 1, 4)))
    o_s, rid = swa_fwd(*swa_in, rider=rider)
    take(names, rid)
    o_fox, o_swa = o_f, o_s
    nf, r_fox = rms_fwd("out_norm_fox", o_fox, sm["out_norm_fox_g"], BF16)
    nsw, r_swa = rms_fwd("out_norm_swa", o_swa, sm["out_norm_swa_g"], BF16)
    o = jnp.concatenate([nf, nsw], axis=-1)
    wout = _stacked(bufs["w_out"])
    names, rider = fetch((down2, "ici", (0, 1, 4)), (up2, "d2d", (1, 4, 4)))
    h2, rid = mm_nn("out_proj", o, wout, resid=h1, rider=rider)
    take(names, rid)

    xn2, r2 = rms_fwd("ffn2_norm", h2, sm["norm_ffn2_g"], BF16)
    names, rider = fetch((down2, "ici", (1, 4, 4)))
    (hid2, hdg2, hdu2), rid = ffn_gu("ffn2_gu", xn2, bufs[gate2], bufs[up2], rider=rider)
    take(names, rid)
    names, rider = fetch((down2, "d2d", WHOLE))
    take(names, run_step("gather_d2d_ffn2_down", rider))
    wd2 = _stacked(bufs["ffn2_w_down"])
    y, _ = mm_nn("ffn2_down", hid2, wd2, scale=0.5, resid=h2)
    loss, dy, dy_b = loss_call(y, target)

    red = {}

    def grad(n, g):
        red[n] = {"grad": g.reshape(ns, -1, g.shape[-1])}

    def ride(*steps):
        def done(rid):
            a0 = n0 = 0
            for rd, cb in steps:
                cb(rid[0][a0:a0 + len(rd.aliased)], rid[1][n0:n0 + len(rd.news)])
                a0, n0 = a0 + len(rd.aliased), n0 + len(rd.news)

        return (combine(*[s[0] for s in steps]) if len(steps) > 1 else steps[0][0]), done

    def xchg(*names):
        def cb(al, news):
            for n, t in zip(names, news):
                red[n]["sum"] = chip_sum("chip_sum_" + n, red[n]["grad"], t, c_idx)

        return exchange_halves([red[n]["grad"] for n in names]), cb

    def scat(n, part=WHOLE):
        def cb(al, news):
            red[n]["got"] = (al or news)[0]

        return scatter_to_owner([red[n]["sum"]], [red[n]["got"]] if "got" in red[n] else None, part), cb

    def own(n):
        red[n]["half"] = owner_sum("owner_sum_" + n, red[n]["sum"], red[n]["got"], pc_idx)

    def join(*names):
        return join_halves([red[n]["half"] for n in names]), lambda al, news: full.update(zip(names, al))

    dwd2, _ = mm_tn("ffn2_dwd", hid2, dy_b, out_dtype=BF16, scale=0.5)
    grad(down2, dwd2)
    rider, done = ride(xchg(down2))
    (dg2, du2), rid = ffn_dh("ffn2_dh", dy_b, wd2, hdg2, hdu2, ns, 0.5, rider=rider)
    done(rid)
    rider, done = ride(scat(down2, (0, 1, 2)))
    dwg2, rid = mm_tn_sharded("ffn2_dwg", xn2, dg2, ns, rider=rider)
    done(rid)
    grad(gate2, dwg2)
    rider, done = ride(scat(down2, (1, 2, 2)), xchg(gate2))
    dwu2, rid = mm_tn_sharded("ffn2_dwu", xn2, du2, ns, rider=rider)
    done(rid)
    grad(up2, dwu2)
    rider, done = ride(scat(gate2, (0, 1, 2)), xchg(up2))
    dxn, rid = mm_nt_sharded("ffn2_dxn_g", dg2, bufs[gate2], rider=rider)
    done(rid)
    rider, done = ride(scat(gate2, (1, 2, 2)))
    dxn, rid = mm_nt_sharded("ffn2_dxn_u", du2, bufs[up2], resid=dxn, rider=rider)
    done(rid)
    dh2, dgain_ffn2, dh2_b = rms_bwd("ffn2_dnorm", h2, sm["norm_ffn2_g"], r2, dxn, dres=dy, also_bf16=True)
    own(down2)
    own(gate2)

    do, _ = mm_nt("out_do", dh2_b, wout)
    dwout, _ = mm_tn("out_dw", o, dh2_b, out_dtype=BF16)
    cf = o_fox.shape[1]
    d_fox, dgain_fox = rms_bwd("out_dnorm_fox", o_fox, sm["out_norm_fox_g"], r_fox, do[:, :cf])
    d_swa, dgain_swa = rms_bwd("out_dnorm_swa", o_swa, sm["out_norm_swa_g"], r_swa, do[:, cf:])
    grad("w_out", dwout)
    rider, done = ride(scat(up2))
    swa_cts, rid = swa_bwd(*swa_in, o_s, d_swa, rider=rider)
    done(rid)
    own(up2)
    rider, done = ride(xchg("w_out"), join(down2, gate2, up2))
    fox_cts, rid = fox_bwd(*fox_in, o_f, lse, d_fox, rider=rider)
    done(rid)
    dproj, dsm_attn = attn_vjp((tuple(fox_cts), tuple(swa_cts)))
    dproj = dproj.astype(BF16)

    rider, done = ride(scat("w_out"))
    du, rid = mm_nt("mix_du", dproj, win, rider=rider)
    done(rid)
    dwin, _ = mm_tn("mix_dwin", u, dproj, out_dtype=BF16)
    grad("w_in", plan.split(dwin))
    rider, done = ride(xchg("w_in"))
    dh1, dgain_mix, dh1_b, rid = rms_bwd("mix_dnorm", h1, sm["norm_mix_g"], r_mix, du, dres=dh2, rider=rider, also_bf16=True)
    done(rid)
    own("w_out")

    rider, done = ride(scat("w_in", (0, 1, 2)))
    dwd1, rid = mm_tn("ffn1_dwd", hid1, dh1_b, out_dtype=BF16, scale=0.5, rider=rider)
    done(rid)
    grad(down1, dwd1)
    rider, done = ride(scat("w_in", (1, 2, 2)), xchg(down1))
    (dg1, du1), rid = ffn_dh("ffn1_dh", dh1_b, wd1, hdg1, hdu1, ns, 0.5, rider=rider)
    done(rid)
    own("w_in")
    rider, done = ride(scat(down1, (0, 1, 2)), join("w_out"))
    dwg1, rid = mm_tn_sharded("ffn1_dwg", xn1, dg1, ns, rider=rider)
    done(rid)
    grad(gate1, dwg1)
    rider, done = ride(scat(down1, (1, 2, 2)), xchg(gate1), join("w_in"))
    dwu1, rid = mm_tn_sharded("ffn1_dwu", xn1, du1, ns, rider=rider)
    done(rid)
    grad(up1, dwu1)
    own(down1)
    rider, done = ride(scat(gate1), xchg(up1), join(down1))
    dxn, rid = mm_nt_sharded("ffn1_dxn_g", dg1, bufs[gate1], rider=rider)
    done(rid)
    own(gate1)
    rider, done = ride(scat(up1, (0, 3, 4)))
    dxn, rid = mm_nt_sharded("ffn1_dxn_u", du1, bufs[up1], resid=dxn, rider=rider)
    done(rid)
    dx, dgain_ffn1 = rms_bwd("ffn1_dnorm", x, sm["norm_ffn1_g"], r1, dxn, dres=dh1)

    rider, done = ride(scat(up1, (3, 4, 4)), join(gate1))
    done(run_step("reduce_tail", rider))
    own(up1)
    rider, done = ride(join(up1))
    done(run_step("join_tail", rider))

    g_small = dict(dsm_attn)
    g_small["norm_mix_g"] = g_small["norm_mix_g"] + dgain_mix
    g_small.update(norm_ffn1_g=dgain_ffn1, norm_ffn2_g=dgain_ffn2, out_norm_fox_g=dgain_fox, out_norm_swa_g=dgain_swa)
    return loss, dx, full, g_small


def kernel(x, positions, norm_ffn1_g, ffn1_w_gate, ffn1_w_up, ffn1_w_down, norm_mix_g, w_in, b_forget, fox_q_norm_g, fox_k_norm_g, swa_q_norm_g, swa_k_norm_g, swa_sinks, out_norm_fox_g, out_norm_swa_g, w_out, norm_ffn2_g, ffn2_w_gate, ffn2_w_up, ffn2_w_down, loss_target, m_norm_ffn1_g, m_ffn1_w_gate, m_ffn1_w_up, m_ffn1_w_down, m_norm_mix_g, m_w_in, m_b_forget, m_fox_q_norm_g, m_fox_k_norm_g, m_swa_q_norm_g, m_swa_k_norm_g, m_swa_sinks, m_out_norm_fox_g, m_out_norm_swa_g, m_w_out, m_norm_ffn2_g, m_ffn2_w_gate, m_ffn2_w_up, m_ffn2_w_down, v_norm_ffn1_g, v_ffn1_w_gate, v_ffn1_w_up, v_ffn1_w_down, v_norm_mix_g, v_w_in, v_b_forget, v_fox_q_norm_g, v_fox_k_norm_g, v_swa_q_norm_g, v_swa_k_norm_g, v_swa_sinks, v_out_norm_fox_g, v_out_norm_swa_g, v_w_out, v_norm_ffn2_g, v_ffn2_w_gate, v_ffn2_w_up, v_ffn2_w_down):
    args = dict(locals())
    w = {k: args[k] for k in _ALL}
    m = {k: args["m_" + k] for k in _ALL}
    v = {k: args["v_" + k] for k in _ALL}
    c_idx = lax.axis_index("c").astype(jnp.int32).reshape(1)
    p_idx = (2 * lax.axis_index("x") + lax.axis_index("y")).astype(jnp.int32).reshape(1)
    pc_idx = jnp.concatenate([p_idx, c_idx])

    small = {k: w[k] for k in _SMALL}
    shards = {k: w[k][0] for k in _BIG}
    plan = WinPlan(x.shape[-1])
    shards["w_in"] = lax.switch(p_idx[0], [functools.partial(plan.place, s=s) for s in range(N_CHIPS)], shards["w_in"])
    loss, grad_x, g_shard, g_small = _local_step(shards, {k: w[k][0] for k in _SMALL}, x[0], positions[0], loss_target[0],
                                                 p_idx, c_idx, pc_idx)
    g_shard["w_in"] = lax.switch(p_idx[0], [functools.partial(plan.unplace, s=s) for s in range(N_CHIPS)], g_shard["w_in"])
    loss = lax.psum(loss, ("x", "y", "c"))
    g_small_sum = _unpack_small(all_reduce_small(_pack_small({k: g_small[k].reshape(1, -1) for k in _SMALL})), small)

    grad_w, delta, new_m, new_v = {}, {}, {}, {}
    for k in _BIG:
        (g, d, nm, nv), _ = adamw("adamw_" + k, w[k][0], g_shard[k], m[k][0], v[k][0])
        grad_w[k], delta[k], new_m[k], new_v[k] = g[None], d[None], nm[None], nv[None]
    (_, d, nm, nv), _ = adamw("adamw_small", _pack_small(small), _pack_small(g_small_sum), _pack_small({k: m[k] for k in _SMALL}),
                              _pack_small({k: v[k] for k in _SMALL}))
    grad_w.update(g_small_sum)
    delta.update(_unpack_small(d, small))
    new_m.update(_unpack_small(nm, small))
    new_v.update(_unpack_small(nv, small))

    return (loss, grad_x[None], *[grad_w[k] for k in _ALL], *[delta[k] for k in _ALL], *[new_m[k] for k in _ALL], *[new_v[k] for k in _ALL])
```

```python
import functools

import jax
import jax.numpy as jnp
from jax import lax
from jax.experimental import pallas as pl
from jax.experimental.pallas import tpu as pltpu

F32 = jnp.float32
BF16 = jnp.bfloat16

HEAD_DIM = 64
WINDOW = 128
ROPE_THETA = 10000.0
EPS = 1e-6
N_CHIPS = 4
N_DEV = 8

ADAM_LR = 0.001
ADAM_B1 = 0.9
ADAM_B2 = 0.999
ADAM_EPS = 1e-08
ADAM_WD = 0.01
ADAM_STEP = 10

V7X_VMEM_BYTES = 64 * 1024 * 1024
VMEM_LIMIT = V7X_VMEM_BYTES - 8 * 1024 * 1024
MASK_VALUE = -1e30

_MESH = pl.DeviceIdType.MESH
_HBM = pl.BlockSpec(memory_space=pl.ANY)
_DIMS = {"nn": (((1,), (0,)), ((), ())), "nt": (((1,), (1,)), ((), ())), "tn": (((0,), (0,)), ((), ()))}


def _pick(n, prefs):
    for p in prefs:
        if n % p == 0:
            return p
    return n


class Rider:
    def __init__(self, reads, aliased, news, nsem, build):
        self.reads, self.aliased, self.news, self.nsem, self.build = list(reads), list(aliased), list(news), nsem, build


class _Shifted:
    def __init__(self, ref, off):
        self.ref, self.off = ref, off

    @property
    def at(self):
        return self

    def __getitem__(self, k):
        return self.ref.at[k + self.off]


def combine(*riders):
    def build(reads, al, news, ssem, rsem):
        out = ([], [], [])
        r0 = a0 = n0 = s0 = 0
        for rd in riders:
            nr, na, nn = len(rd.reads), len(rd.aliased), len(rd.news)
            part = rd.build(reads[r0:r0 + nr], al[a0:a0 + na], news[n0:n0 + nn], _Shifted(ssem, s0), _Shifted(rsem, s0))
            for acc, lst in zip(out, part):
                acc.extend(lst)
            r0, a0, n0, s0 = r0 + nr, a0 + na, n0 + nn, s0 + rd.nsem
        return out

    return Rider(sum((r.reads for r in riders), []), sum((r.aliased for r in riders), []), sum((r.news for r in riders), []),
                 sum(r.nsem for r in riders), build)


def _me():
    return lax.axis_index("x"), lax.axis_index("y"), lax.axis_index("c")


def _other_chips(x, y):
    return [(1 - x, y), (x, 1 - y), (1 - x, 1 - y)]


WHOLE = (0, 1, 1)


def _rows(ref, start, rows, part=WHOLE):
    k0, k1, n = part
    assert rows % n == 0, (rows, part)
    idx = (slice(None),) * (len(ref.shape) - 2) + (pl.ds(start + k0 * (rows // n), (k1 - k0) * (rows // n)), slice(None))
    return ref.at[idx]


def _half(ref, h, part=WHOLE):
    rows = ref.shape[-2] // 2
    return _rows(ref, h * rows, rows, part)


def _remote(src, dst, ssem, rsem, k, to):
    return pltpu.make_async_remote_copy(src_ref=src, dst_ref=dst, send_sem=ssem.at[k], recv_sem=rsem.at[k], device_id=to,
                                        device_id_type=_MESH)


def _later(*args):
    return functools.partial(_remote, *args)


def gather(bufs, jobs):
    def build(reads, al, news, ssem, rsem):
        x, y, c = _me()
        p = 2 * x + y
        starts, arrivals = [], []
        for n, (b, kind, part) in enumerate(jobs):
            for j, chip in enumerate(_other_chips(x, y)):
                q = 2 * chip[0] + chip[1]
                if kind == "ici":
                    src, landing, to = _half(al[b].at[p], c, part), _half(al[b].at[q], c, part), (*chip, c)
                else:
                    src, landing, to = _half(al[b].at[q], c, part), _half(al[b].at[q], 1 - c, part), (x, y, 1 - c)
                starts.append(_later(src, src, ssem, rsem, 3 * n + j, to))
                arrivals.append(_later(landing, landing, ssem, rsem, 3 * n + j, to))
        return starts, arrivals, starts

    return Rider([], bufs, [], 3 * len(jobs), build)


def exchange_halves(grads):
    def build(reads, al, news, ssem, rsem):
        x, y, c = _me()
        cps = [_later(_half(g, 1 - c), t, ssem, rsem, w, (x, y, 1 - c)) for w, (g, t) in enumerate(zip(reads, news))]
        return cps, cps, cps

    return Rider(grads, [], [jax.ShapeDtypeStruct((g.shape[0], g.shape[1] // 2, g.shape[2]), g.dtype) for g in grads], len(grads), build)


def scatter_to_owner(sums, gots=None, part=WHOLE):
    def build(reads, al, news, ssem, rsem):
        x, y, c = _me()
        cps = []
        for w, (s, got) in enumerate(zip(reads, al or news)):
            rows = s.shape[-2]
            for j, chip in enumerate(_other_chips(x, y)):
                cps.append(_later(_rows(s.at[2 * chip[0] + chip[1]], 0, rows, part), _rows(got.at[j], 0, rows, part), ssem, rsem,
                                  3 * w + j, (*chip, c)))
        return cps, cps, cps

    news = [] if gots else [jax.ShapeDtypeStruct((3,) + s.shape[1:], s.dtype) for s in sums]
    return Rider(sums, gots or [], news, 3 * len(sums), build)


def join_halves(fulls):
    def build(reads, al, news, ssem, rsem):
        x, y, c = _me()
        starts, arrivals = [], []
        for w, f in enumerate(al):
            mine, landing = _half(f, c), _half(f, 1 - c)
            starts.append(_later(mine, mine, ssem, rsem, w, (x, y, 1 - c)))
            arrivals.append(_later(landing, landing, ssem, rsem, w, (x, y, 1 - c)))
        return starts, arrivals, starts

    return Rider([], fulls, [], len(fulls), build)


def _start_and_wait(rider, reads, al, news, ssem, rsem, first, last):
    @pl.when(first)
    def _():
        for cp in rider.build(reads, al, news, ssem, rsem)[0]:
            cp().start()

    def finish():
        @pl.when(last)
        def _():
            _, arrivals, sends = rider.build(reads, al, news, ssem, rsem)
            for cp in arrivals:
                cp().wait_recv()
            for cp in sends:
                cp().wait_send()

    return finish


def _call(name, body, grid, in_specs, args, out_specs, out_shape, scratch=(), semantics=None, rider=None, prefetch=()):
    n_pre, n_in, n_out, n_scr = len(prefetch), len(args), len(out_shape), len(scratch)
    nr, na, nn = (len(rider.reads), len(rider.aliased), len(rider.news)) if rider else (0, 0, 0)

    def wrapped(*refs):
        pre, refs = refs[:n_pre], refs[n_pre:]
        ins, reads = refs[:n_in], refs[n_in:n_in + nr]
        o0 = n_in + nr + na
        outs, al, news = refs[o0:o0 + n_out], refs[o0 + n_out:o0 + n_out + na], refs[o0 + n_out + na:o0 + n_out + na + nn]
        s0 = o0 + n_out + na + nn
        scr, (ssem, rsem) = refs[s0:s0 + n_scr], refs[s0 + n_scr:]
        first = functools.reduce(jnp.logical_and, [pl.program_id(a) == 0 for a in range(len(grid))])
        last = functools.reduce(jnp.logical_and, [pl.program_id(a) == g - 1 for a, g in enumerate(grid)])
        finish = _start_and_wait(rider, reads, al, news, ssem, rsem, first, last)
        body(*pre, *ins, *outs, *scr)
        finish()

    kernel_fn, all_in, all_out, shapes, scr = body, list(in_specs), list(out_specs), list(out_shape), list(scratch)
    operands, aliases = (*prefetch, *args), {}
    if rider:
        kernel_fn, semantics = wrapped, ("arbitrary",) * len(grid)
        all_in += [_HBM] * (nr + na)
        all_out += [_HBM] * (na + nn)
        shapes += [jax.ShapeDtypeStruct(a.shape, a.dtype) for a in rider.aliased] + rider.news
        scr += [pltpu.SemaphoreType.DMA((rider.nsem,)), pltpu.SemaphoreType.DMA((rider.nsem,))]
        operands += (*rider.reads, *rider.aliased)
        aliases = {n_pre + n_in + nr + i: n_out + i for i in range(na)}
    params = pltpu.CompilerParams(dimension_semantics=semantics, vmem_limit_bytes=VMEM_LIMIT)
    if n_pre:
        spec = pltpu.PrefetchScalarGridSpec(num_scalar_prefetch=n_pre, grid=grid, in_specs=all_in, out_specs=all_out, scratch_shapes=scr)
        outs = pl.pallas_call(kernel_fn, name=name, grid_spec=spec, out_shape=shapes, input_output_aliases=aliases, compiler_params=params)(*operands)
    else:
        outs = pl.pallas_call(kernel_fn, name=name, grid=grid, in_specs=all_in, out_specs=all_out, out_shape=shapes, scratch_shapes=scr,
                              input_output_aliases=aliases, compiler_params=params)(*operands)
    return list(outs[:n_out]), ((list(outs[n_out:n_out + na]), list(outs[n_out + na:])) if rider else None)


def run_step(name, rider):
    nr, na, nn = len(rider.reads), len(rider.aliased), len(rider.news)

    def body(*refs):
        reads = refs[:nr]
        al, news = refs[nr + na:nr + 2 * na], refs[nr + 2 * na:nr + 2 * na + nn]
        ssem, rsem = refs[nr + 2 * na + nn:]
        starts, arrivals, sends = rider.build(reads, al, news, ssem, rsem)
        for cp in starts:
            cp().start()
        for cp in arrivals:
            cp().wait_recv()
        for cp in sends:
            cp().wait_send()

    outs = pl.pallas_call(
        body, name=name, in_specs=[_HBM] * (nr + na), out_specs=[_HBM] * (na + nn),
        out_shape=[jax.ShapeDtypeStruct(a.shape, a.dtype) for a in rider.aliased] + rider.news,
        input_output_aliases={nr + i: i for i in range(na)},
        scratch_shapes=[pltpu.SemaphoreType.DMA((rider.nsem,)), pltpu.SemaphoreType.DMA((rider.nsem,))],
    )(*rider.reads, *rider.aliased)
    return list(outs[:na]), list(outs[na:])


def all_reduce_small(v):
    rows, lanes = v.shape

    def body(v_ref, o_ref, slots, send_sems, recv_sems):
        x, y, c = _me()
        me = 4 * x + 2 * y + c
        slots[me] = v_ref[...]
        cps = []
        for k in range(1, N_DEV):
            peer = (x ^ (k >> 2), y ^ ((k >> 1) & 1), c ^ (k & 1))
            cps.append(_remote(v_ref, slots.at[me], send_sems, recv_sems, k - 1, peer))
            cps[-1].start()
        for k in range(1, N_DEV):
            theirs = slots.at[me ^ k]
            _remote(theirs, theirs, send_sems, recv_sems, k - 1, (x, y, c)).wait_recv()
        for cp in cps:
            cp.wait_send()
        acc = slots[0]
        for i in range(1, N_DEV):
            acc = acc + slots[i]
        o_ref[...] = acc

    return pl.pallas_call(
        body, name="all_reduce_small",
        in_specs=[pl.BlockSpec(memory_space=pltpu.VMEM)], out_specs=pl.BlockSpec(memory_space=pltpu.VMEM),
        out_shape=jax.ShapeDtypeStruct((rows, lanes), F32),
        scratch_shapes=[pltpu.VMEM((N_DEV, rows, lanes), F32), pltpu.SemaphoreType.DMA((N_DEV - 1,)), pltpu.SemaphoreType.DMA((N_DEV - 1,))],
    )(v)


def _mm_call(name, mode, a, b, a_spec, b_spec, out_shape, out_spec, grid, acc_shape, scale=1.0, resid=None, resid_spec=None, rider=None):
    nk = grid[2]
    dims = _DIMS[mode]
    has_resid = resid is not None

    def body(*refs):
        a_ref, b_ref = refs[:2]
        r_ref = refs[2] if has_resid else None
        o_ref = refs[3] if has_resid else refs[2]

        def finish(r):
            if scale != 1.0:
                r = r * scale
            if has_resid:
                r = r_ref[...].astype(F32) + r
            o_ref[...] = r.astype(o_ref.dtype)

        part = lax.dot_general(a_ref[...].astype(BF16), b_ref[...].astype(BF16), dims, preferred_element_type=F32)
        if nk == 1:
            finish(part)
            return
        acc_ref = refs[-1]
        k = pl.program_id(2)

        @pl.when(k == 0)
        def _():
            acc_ref[...] = part

        @pl.when(k > 0)
        def _():
            acc_ref[...] += part

        @pl.when(k == nk - 1)
        def _():
            finish(acc_ref[...])

    in_specs = [a_spec, b_spec] + ([resid_spec] if has_resid else [])
    args = (a, b) + ((resid,) if has_resid else ())
    (out,), rid = _call(name, body, grid, in_specs, args, [out_spec], [out_shape], [pltpu.VMEM(acc_shape, F32)] if nk > 1 else [],
                        ("parallel", "parallel", "arbitrary"), rider)
    return out, rid


MM_VMEM_BUDGET = 40 * 1024 * 1024
_TILE_OPTS = (2048, 1408, 1024, 512, 256, 128)


def _tiles(m, n, kd, a_item, b_item, o_item, r_item=0, tm=None, tn=None, tk=None):
    def opts(full, fixed, cap):
        return [fixed] if fixed else [t for t in _TILE_OPTS if t <= cap and full % t == 0] or [full]

    best = None
    for cm in opts(m, tm, 1408):
        for cn in opts(n, tn, 1408):
            for ck in opts(kd, tk, 2048):
                blocks = cm * ck * a_item + ck * cn * b_item + cm * cn * (o_item + r_item)
                casts = (cm * ck * 2 if a_item == 4 else 0) + (ck * cn * 2 if b_item == 4 else 0)
                if 2 * blocks + cm * cn * 4 + casts <= MM_VMEM_BUDGET:
                    key = (cm * cn * ck, ck)
                    if best is None or key > best[0]:
                        best = (key, (cm, cn, ck))
    assert best is not None, (m, n, kd)
    return best[1]


def _item(x):
    return jnp.dtype(x.dtype).itemsize


def mm_nn(name, a, b, *, out_dtype=F32, scale=1.0, resid=None, rider=None):
    m, kd = a.shape
    n = b.shape[1]
    tm, tn, tk = _tiles(m, n, kd, _item(a), _item(b), jnp.dtype(out_dtype).itemsize, 0 if resid is None else _item(resid))
    o_spec = pl.BlockSpec((tm, tn), lambda i, j, k: (i, j))
    return _mm_call(
        name, "nn", a, b, pl.BlockSpec((tm, tk), lambda i, j, k: (i, k)), pl.BlockSpec((tk, tn), lambda i, j, k: (k, j)),
        jax.ShapeDtypeStruct((m, n), out_dtype), o_spec, (m // tm, n // tn, kd // tk), (tm, tn), scale, resid, o_spec, rider)


def mm_nt(name, a, b, *, out_dtype=F32, scale=1.0, resid=None, rider=None):
    m, kd = a.shape
    n = b.shape[0]
    tm, tn, tk = _tiles(m, n, kd, _item(a), _item(b), jnp.dtype(out_dtype).itemsize, 0 if resid is None else _item(resid))
    o_spec = pl.BlockSpec((tm, tn), lambda i, j, k: (i, j))
    return _mm_call(
        name, "nt", a, b, pl.BlockSpec((tm, tk), lambda i, j, k: (i, k)), pl.BlockSpec((tn, tk), lambda i, j, k: (j, k)),
        jax.ShapeDtypeStruct((m, n), out_dtype), o_spec, (m // tm, n // tn, kd // tk), (tm, tn), scale, resid, o_spec, rider)


def mm_tn(name, a, b, *, out_dtype=F32, scale=1.0, rider=None):
    kd, m = a.shape
    n = b.shape[1]
    tm, tn, tk = _tiles(m, n, kd, _item(a), _item(b), jnp.dtype(out_dtype).itemsize)
    return _mm_call(
        name, "tn", a, b, pl.BlockSpec((tk, tm), lambda i, j, k: (k, i)), pl.BlockSpec((tk, tn), lambda i, j, k: (k, j)),
        jax.ShapeDtypeStruct((m, n), out_dtype), pl.BlockSpec((tm, tn), lambda i, j, k: (i, j)),
        (m // tm, n // tn, kd // tk), (tm, tn), scale, rider=rider)


def mm_nt_sharded(name, a, w, *, resid=None, rider=None):
    m = a.shape[0]
    ns, n, c = w.shape
    tm, tn, _ = _tiles(m, n, c, _item(a), _item(w), 4, 0 if resid is None else _item(resid), tk=c)
    o_spec = pl.BlockSpec((tm, tn), lambda i, j, k: (i, j))
    return _mm_call(
        name, "nt", a, w, pl.BlockSpec((tm, c), lambda i, j, k: (i, k)), pl.BlockSpec((None, tn, c), lambda i, j, k: (k, j, 0)),
        jax.ShapeDtypeStruct((m, n), F32), o_spec, (m // tm, n // tn, ns), (tm, tn), 1.0, resid, o_spec, rider)


def mm_tn_sharded(name, a, b, ns, *, rider=None):
    kd, m = a.shape
    c = b.shape[1] // ns
    tm, _, tk = _tiles(m, c, kd, _item(a), _item(b), 2, tn=c)
    return _mm_call(
        name, "tn", a, b, pl.BlockSpec((tk, tm), lambda i, j, k: (k, i)), pl.BlockSpec((tk, c), lambda i, j, k: (k, j)),
        jax.ShapeDtypeStruct((ns, m, c), BF16), pl.BlockSpec((None, tm, c), lambda i, j, k: (j, i, 0)),
        (m // tm, ns, kd // tk), (tm, c), rider=rider)


def rms_fwd(name, x, g, out_dtype, rider=None):
    r, c = x.shape
    tm = _pick(r, (512, 256, 128, 64, 8))

    def body(x_ref, g_ref, y_ref, r_ref):
        xf = x_ref[...].astype(F32)
        rstd = lax.rsqrt(jnp.mean(xf * xf, axis=-1, keepdims=True) + EPS)
        y_ref[...] = ((xf * rstd) * g_ref[...]).astype(y_ref.dtype)
        r_ref[...] = rstd

    (y, rstd), rid = _call(
        name, body, (r // tm,), [pl.BlockSpec((tm, c), lambda i: (i, 0)), pl.BlockSpec((1, c), lambda i: (0, 0))], (x, g.reshape(1, c)),
        [pl.BlockSpec((tm, c), lambda i: (i, 0)), pl.BlockSpec((tm, 1), lambda i: (i, 0))],
        [jax.ShapeDtypeStruct((r, c), out_dtype), jax.ShapeDtypeStruct((r, 1), F32)], (), ("parallel",), rider)
    return (y, rstd) if rider is None else (y, rstd, rid)


def rms_bwd(name, x, g, rstd, dy, dres=None, rider=None, also_bf16=False):
    r, c = x.shape
    tm = _pick(r, (512, 256, 128, 64, 8))
    has_res = dres is not None

    def body(*refs):
        x_ref, g_ref, r_ref, dy_ref = refs[:4]
        dres_ref = refs[4] if has_res else None
        dx_ref, dg_ref = refs[4 + has_res:6 + has_res]
        xhat = x_ref[...].astype(F32) * r_ref[...]
        dyf = dy_ref[...].astype(F32)
        gdy = dyf * g_ref[...]
        dx = r_ref[...] * (gdy - xhat * jnp.mean(gdy * xhat, axis=-1, keepdims=True))
        if has_res:
            dx = dx + dres_ref[...]
        dx_ref[...] = dx
        if also_bf16:
            refs[-1][...] = dx.astype(BF16)

        @pl.when(pl.program_id(0) == 0)
        def _():
            dg_ref[...] = jnp.zeros_like(dg_ref)

        dg_ref[...] += jnp.sum(dyf * xhat, axis=0, keepdims=True)

    row = pl.BlockSpec((tm, c), lambda i: (i, 0))
    in_specs = [row, pl.BlockSpec((1, c), lambda i: (0, 0)), pl.BlockSpec((tm, 1), lambda i: (i, 0)), row] + ([row] if has_res else [])
    args = (x, g.reshape(1, c), rstd, dy) + ((dres,) if has_res else ())
    outs, rid = _call(name, body, (r // tm,), in_specs, args, [row, pl.BlockSpec((1, c), lambda i: (0, 0))] + [row] * also_bf16,
                      [jax.ShapeDtypeStruct((r, c), F32), jax.ShapeDtypeStruct((1, c), F32)] + [jax.ShapeDtypeStruct((r, c), BF16)] * also_bf16,
                      (), ("arbitrary",), rider)
    return (outs[0], outs[1].reshape(c), *outs[2:], *([] if rider is None else [rid]))


_LANES = 128


def _head_mean(v):
    if v.shape[1] == HEAD_DIM:
        return jnp.mean(v, axis=-1, keepdims=True)
    low = lax.broadcasted_iota(jnp.int32, v.shape, 1) < HEAD_DIM
    lo = jnp.sum(jnp.where(low, v, 0.0), axis=-1, keepdims=True)
    hi = jnp.sum(jnp.where(low, 0.0, v), axis=-1, keepdims=True)
    return jnp.where(low, lo, hi) * (1.0 / HEAD_DIM)


def _head_groups(c):
    width = _LANES if c % _LANES == 0 else HEAD_DIM
    assert c % width == 0, c
    return width, [slice(k * width, (k + 1) * width) for k in range(c // width)]


def _head_gain(g, width):
    return jnp.tile(g.reshape(1, HEAD_DIM), (1, width // HEAD_DIM))


def _rotate_half(y):
    half = HEAD_DIM // 2
    first = lax.broadcasted_iota(jnp.int32, y.shape, 1) % HEAD_DIM < half
    return jnp.where(first, -pltpu.roll(y, y.shape[1] - half, axis=1), pltpu.roll(y, half, axis=1))


def _rope_tables(rope, width):
    return [jnp.tile(t, (1, 2 * width // HEAD_DIM)) for t in rope]


def head_rms_fwd(name, x, g, rope=None):
    s, c = x.shape
    tm = _pick(s, (512, 256, 128, 8))
    width, groups = _head_groups(c)

    def body(x_ref, g_ref, *refs):
        y_ref = refs[-1]
        for sl in groups:
            xs = x_ref[:, sl]
            y = (xs * lax.rsqrt(_head_mean(xs * xs) + EPS)) * g_ref[...]
            if rope:
                y = y * refs[0][...] + _rotate_half(y) * refs[1][...]
            y_ref[:, sl] = y

    row = pl.BlockSpec((tm, c), lambda i: (i, 0))
    tab = pl.BlockSpec((tm, width), lambda i: (i, 0))
    tables = _rope_tables(rope, width) if rope else []
    (y,), _ = _call(name, body, (s // tm,), [row, pl.BlockSpec((1, width), lambda i: (0, 0))] + [tab] * len(tables),
                    (x, _head_gain(g, width), *tables), [row], [jax.ShapeDtypeStruct((s, c), F32)], (), ("parallel",))
    return y


def head_rms_bwd(name, x, g, dy, rope=None):
    s, c = x.shape
    tm = _pick(s, (512, 256, 128, 8))
    width, groups = _head_groups(c)

    def body(x_ref, g_ref, dy_ref, *refs):
        dx_ref, dg_ref = refs[-2:]

        @pl.when(pl.program_id(0) == 0)
        def _():
            dg_ref[...] = jnp.zeros_like(dg_ref)

        for sl in groups:
            xs, dys = x_ref[:, sl], dy_ref[:, sl]
            if rope:
                dys = dys * refs[0][...] - _rotate_half(dys * refs[1][...])
            rstd = lax.rsqrt(_head_mean(xs * xs) + EPS)
            xhat = xs * rstd
            gdy = dys * g_ref[...]
            dx_ref[:, sl] = rstd * (gdy - xhat * _head_mean(gdy * xhat))
            dg_ref[...] += jnp.sum(dys * xhat, axis=0, keepdims=True)

    row = pl.BlockSpec((tm, c), lambda i: (i, 0))
    vec = pl.BlockSpec((1, width), lambda i: (0, 0))
    tab = pl.BlockSpec((tm, width), lambda i: (i, 0))
    tables = _rope_tables(rope, width) if rope else []
    (dx, dg), _ = _call(name, body, (s // tm,), [row, vec, row] + [tab] * len(tables), (x, _head_gain(g, width), dy, *tables), [row, vec],
                        [jax.ShapeDtypeStruct((s, c), F32), jax.ShapeDtypeStruct((1, width), F32)], (), ("arbitrary",))
    return dx, jnp.sum(dg.reshape(width // HEAD_DIM, HEAD_DIM), axis=0)


@functools.partial(jax.custom_vjp, nondiff_argnums=(0,))
def head_rms(name, x, g):
    return head_rms_fwd(name + "_fwd", x, g)


def _head_rms_fwd(name, x, g):
    return head_rms_fwd(name + "_fwd", x, g), (x, g)


def _head_rms_bwd(name, res, dy):
    return head_rms_bwd(name + "_bwd", *res, dy)


head_rms.defvjp(_head_rms_fwd, _head_rms_bwd)


@functools.partial(jax.custom_vjp, nondiff_argnums=(0,))
def head_rms_rope(name, x, g, cos, sin):
    return head_rms_fwd(name + "_fwd", x, g, (cos, sin))


def _head_rms_rope_fwd(name, x, g, cos, sin):
    return head_rms_fwd(name + "_fwd", x, g, (cos, sin)), (x, g, cos, sin)


def _head_rms_rope_bwd(name, res, dy):
    x, g, cos, sin = res
    return (*head_rms_bwd(name + "_bwd", x, g, dy, (cos, sin)), jnp.zeros_like(cos), jnp.zeros_like(sin))


head_rms_rope.defvjp(_head_rms_rope_fwd, _head_rms_rope_bwd)


def _cumsum_call(name, a, reverse):
    h, s = a.shape
    tb = _LANES
    assert s % tb == 0

    def body(a_ref, o_ref):
        t_in = lax.broadcasted_iota(jnp.int32, (tb, tb), 0)
        t_out = lax.broadcasted_iota(jnp.int32, (tb, tb), 1)
        tri = jnp.where((t_in >= t_out) if reverse else (t_in <= t_out), 1.0, 0.0).astype(BF16)
        carry = jnp.zeros((h, 1), F32)
        blocks = range(s // tb)
        for b in (reversed(blocks) if reverse else blocks):
            cols = slice(b * tb, (b + 1) * tb)
            block = a_ref[:, cols]
            rest, local = block, jnp.zeros((h, tb), F32)
            for _ in range(3):
                piece = rest.astype(BF16)
                local = local + jnp.dot(piece, tri, preferred_element_type=F32)
                rest = rest - piece.astype(F32)
            o_ref[:, cols] = local + carry
            carry = carry + jnp.sum(block, axis=1, keepdims=True)

    whole = pl.BlockSpec((h, s), lambda j: (0, 0))
    (out,), _ = _call(name, body, (1,), [whole], (a,), [whole], [jax.ShapeDtypeStruct((h, s), F32)], (), ("arbitrary",))
    return out


@jax.custom_vjp
def time_cumsum(a):
    return _cumsum_call("gate_cumsum", a, False)


def _time_cumsum_fwd(a):
    return _cumsum_call("gate_cumsum", a, False), None


def _time_cumsum_bwd(_, dc):
    return (_cumsum_call("gate_cumsum_bwd", dc, True),)


time_cumsum.defvjp(_time_cumsum_fwd, _time_cumsum_bwd)


FFN_TM = 512


def _sigmoid(x):
    return 1.0 / (1.0 + jnp.exp(-x))


def ffn_gu(name, xn, wg, wu, rider=None):
    s, d = xn.shape
    ns, _, c = wg.shape
    tm = _pick(s, (FFN_TM, 128))

    def body(x_ref, wg_ref, wu_ref, h_ref, a_ref, b_ref):
        xb = x_ref[...]
        gv = jnp.dot(xb, wg_ref[...], preferred_element_type=F32)
        uv = jnp.dot(xb, wu_ref[...], preferred_element_type=F32)
        sig = _sigmoid(gv)
        silu = gv * sig
        h_ref[...] = (silu * uv).astype(BF16)
        a_ref[...] = (uv * (sig * (1.0 + gv * (1.0 - sig)))).astype(BF16)
        b_ref[...] = silu.astype(BF16)

    w_spec = pl.BlockSpec((None, d, c), lambda j, i: (j, 0, 0))
    o_spec = pl.BlockSpec((tm, c), lambda j, i: (i, j))
    return _call(
        name, body, (ns, s // tm), [pl.BlockSpec((tm, d), lambda j, i: (i, 0)), w_spec, w_spec], (xn, wg, wu),
        [o_spec, o_spec, o_spec], [jax.ShapeDtypeStruct((s, ns * c), BF16)] * 3, [], ("parallel", "parallel"), rider)


def ffn_dh(name, dy, wd, dh_dg, dh_du, ns, scale, rider=None):
    s, d = dy.shape
    f = wd.shape[0]
    c = f // ns
    tm = _pick(s, (FFN_TM, 128))

    def body(dy_ref, wd_ref, a_ref, b_ref, dg_ref, du_ref):
        dh = lax.dot_general(dy_ref[...].astype(BF16), wd_ref[...], _DIMS["nt"], preferred_element_type=F32) * scale
        dg_ref[...] = (dh * a_ref[...].astype(F32)).astype(BF16)
        du_ref[...] = (dh * b_ref[...].astype(F32)).astype(BF16)

    o_spec = pl.BlockSpec((tm, c), lambda j, i: (i, j))
    return _call(
        name, body, (ns, s // tm),
        [pl.BlockSpec((tm, d), lambda j, i: (i, 0)), pl.BlockSpec((c, d), lambda j, i: (j, 0)), o_spec, o_spec], (dy, wd, dh_dg, dh_du),
        [o_spec, o_spec], [jax.ShapeDtypeStruct((s, f), BF16), jax.ShapeDtypeStruct((s, f), BF16)],
        [], ("parallel", "parallel"), rider)


FOX_TQ = 1024


def fox_tile(s_len):
    return min(FOX_TQ, s_len)


def _heads_per_block(h):
    return 2 if h % 2 == 0 else 1


def _fox_queries(q):
    return (q * (HEAD_DIM ** -0.5)).astype(BF16)


def _fox_scores(qs, kc, cq, ck, diagonal):
    s = lax.dot_general(qs, kc.astype(BF16), _DIMS["nt"], preferred_element_type=F32) + cq - ck
    if not diagonal:
        return s
    return jnp.where(lax.broadcasted_iota(jnp.int32, s.shape, 0) >= lax.broadcasted_iota(jnp.int32, s.shape, 1), s, MASK_VALUE)


def _fox_specs(h, s_len, tq):
    hb = _heads_per_block(h)
    qb = pl.BlockSpec((tq, hb * HEAD_DIM), lambda pp, i: (i, pp))
    kb = pl.BlockSpec((s_len, hb * HEAD_DIM), lambda pp, i: (0, pp))
    colb = pl.BlockSpec((hb, tq, 1), lambda pp, i: (pp, i, 0))
    rowb = pl.BlockSpec((hb, s_len // tq, 1, tq), lambda pp, i: (pp, 0, 0, 0))
    return hb, qb, kb, colb, rowb


def fox_fwd(q, k, v, cq, ck, rider=None):
    s_len, hd = q.shape
    h, d = hd // HEAD_DIM, HEAD_DIM
    tq = fox_tile(s_len)
    hb, qb, kb, colb, rowb = _fox_specs(h, s_len, tq)

    def body(q_ref, k_ref, v_ref, cq_ref, ck_ref, o_ref, lse_ref):
        i = pl.program_id(1)
        for hh in range(hb):
            lanes = slice(hh * d, (hh + 1) * d)
            qs, cqv = _fox_queries(q_ref[:, lanes]), cq_ref[hh]

            def chunk(c, carry, diagonal=False):
                m, l, acc = carry
                rows = pl.ds(pl.multiple_of(c * tq, tq), tq)
                s = _fox_scores(qs, k_ref[rows, lanes], cqv, ck_ref[hh, c], diagonal)
                m_new = jnp.maximum(m, jnp.max(s, axis=-1, keepdims=True))
                alpha = jnp.exp(m - m_new)
                p = jnp.exp(s - m_new)
                acc = alpha * acc + jnp.dot(p.astype(BF16), v_ref[rows, lanes].astype(BF16), preferred_element_type=F32)
                return m_new, alpha * l + jnp.sum(p, axis=-1, keepdims=True), acc

            init = (jnp.full((tq, 1), MASK_VALUE, F32), jnp.zeros((tq, 1), F32), jnp.zeros((tq, d), F32))
            m, l, acc = chunk(i, lax.fori_loop(0, i, chunk, init), diagonal=True)
            o_ref[:, lanes] = acc / l
            lse_ref[hh] = m + jnp.log(l)

    return _call(
        "fox_fwd", body, (h // hb, s_len // tq), [qb, kb, kb, colb, rowb], (q, k, v, cq, ck), [qb, colb],
        [jax.ShapeDtypeStruct((s_len, hd), F32), jax.ShapeDtypeStruct((h, s_len, 1), F32)], (), ("parallel", "parallel"), rider)


def fox_bwd(q, k, v, cq, ck, o, lse, do, rider=None):
    s_len, hd = q.shape
    h, d = hd // HEAD_DIM, HEAD_DIM
    tq = fox_tile(s_len)
    scale = HEAD_DIM ** -0.5
    hb, qb, kb, colb, rowb = _fox_specs(h, s_len, tq)

    def body(q_ref, k_ref, v_ref, cq_ref, ck_ref, o_ref, lse_ref, do_ref, dq_ref, dk_ref, dv_ref, dcq_ref, dck_ref):
        i = pl.program_id(1)

        @pl.when(i == 0)
        def _():
            dk_ref[...] = jnp.zeros_like(dk_ref)
            dv_ref[...] = jnp.zeros_like(dv_ref)
            dck_ref[...] = jnp.zeros_like(dck_ref)

        heads = []
        for hh in range(hb):
            lanes = slice(hh * d, (hh + 1) * d)
            dof = do_ref[:, lanes]
            heads.append((lanes, _fox_queries(q_ref[:, lanes]), cq_ref[hh], lse_ref[hh], dof.astype(BF16),
                          jnp.sum(dof * o_ref[:, lanes], axis=-1, keepdims=True)))

        def chunk(c, carry, diagonal=False):
            rows = pl.ds(pl.multiple_of(c * tq, tq), tq)
            out, dks, dvs = [], [], []
            for hh, (lanes, qs, cqv, lse_h, dob, delta) in enumerate(heads):
                dq, dcq = carry[hh]
                kc = k_ref[rows, lanes]
                p = jnp.exp(_fox_scores(qs, kc, cqv, ck_ref[hh, c], diagonal) - lse_h)
                dp = lax.dot_general(dob, v_ref[rows, lanes].astype(BF16), _DIMS["nt"], preferred_element_type=F32)
                ds = p * (dp - delta)
                dsb = ds.astype(BF16)
                dvs.append(lax.dot_general(p.astype(BF16), dob, _DIMS["tn"], preferred_element_type=F32))
                dks.append(lax.dot_general(dsb, qs, _DIMS["tn"], preferred_element_type=F32))
                dck_ref[hh, c] -= jnp.sum(ds, axis=0, keepdims=True)
                out.append((dq + jnp.dot(dsb, kc.astype(BF16), preferred_element_type=F32), dcq + jnp.sum(ds, axis=-1, keepdims=True)))
            dk_ref[rows, :] += jnp.concatenate(dks, axis=1)
            dv_ref[rows, :] += jnp.concatenate(dvs, axis=1)
            return tuple(out)

        init = tuple((jnp.zeros((tq, d), F32), jnp.zeros((tq, 1), F32)) for _ in range(hb))
        done = chunk(i, lax.fori_loop(0, i, chunk, init), diagonal=True)
        dq_ref[...] = jnp.concatenate([dq for dq, _ in done], axis=1) * scale
        for hh, (_, dcq) in enumerate(done):
            dcq_ref[hh] = dcq

    return _call(
        "fox_bwd", body, (h // hb, s_len // tq), [qb, kb, kb, colb, rowb, qb, colb, qb], (q, k, v, cq, ck, o, lse, do),
        [qb, kb, kb, colb, rowb],
        [jax.ShapeDtypeStruct((s_len, hd), F32)] * 3
        + [jax.ShapeDtypeStruct((h, s_len, 1), F32), jax.ShapeDtypeStruct((h, s_len // tq, 1, tq), F32)],
        (), ("parallel", "arbitrary"), rider)


def _stack_heads(ref, first, g):
    return jnp.concatenate([ref[:, (first + j) * HEAD_DIM:(first + j + 1) * HEAD_DIM] for j in range(g)], axis=0)


def _window(prev_ref, cur_ref, hh):
    lanes = slice(hh * HEAD_DIM, (hh + 1) * HEAD_DIM)
    return jnp.concatenate([prev_ref[:, lanes], cur_ref[:, lanes]], axis=0).astype(BF16)


def _swa_band(g, w):
    t = lax.broadcasted_iota(jnp.int32, (g * w, 2 * w), 0) % w
    col = lax.broadcasted_iota(jnp.int32, (g * w, 2 * w), 1)
    rel = t + w - col
    band = (rel >= 0) & (rel < w)
    return jnp.where(jnp.stack([band & (col >= w), band]), 0.0, MASK_VALUE).astype(F32)


def _swa_probs(qs, kw, sink, band):
    s = lax.dot_general(qs, kw, _DIMS["nt"], preferred_element_type=F32) + band
    m = jnp.maximum(jnp.max(s, axis=-1, keepdims=True), sink)
    p = jnp.exp(s - m)
    ps = jnp.exp(sink - m)
    linv = 1.0 / (jnp.sum(p, axis=-1, keepdims=True) + ps)
    return p * linv, ps * linv


def _swa_specs(hk, g, s_len):
    w = WINDOW
    assert s_len % w == 0
    hb = _heads_per_block(hk)
    qb = pl.BlockSpec((w, hb * g * HEAD_DIM), lambda pp, n: (n, pp))
    prev = pl.BlockSpec((w, hb * HEAD_DIM), lambda pp, n: (jnp.maximum(n - 1, 0), pp))
    cur = pl.BlockSpec((w, hb * HEAD_DIM), lambda pp, n: (n, pp))
    sb = pl.BlockSpec((hb, g * w, 1), lambda pp, n: (pp, 0, 0))
    band = pl.BlockSpec((None, g * w, 2 * w), lambda pp, n: (jnp.minimum(n, 1), 0, 0))
    return hb, qb, prev, cur, sb, band


def swa_fwd(q, k, v, sink, rider=None):
    s_len = q.shape[0]
    hk = k.shape[1] // HEAD_DIM
    g = q.shape[1] // k.shape[1]
    w, d = WINDOW, HEAD_DIM
    hb, qb, prev, cur, sb, bandb = _swa_specs(hk, g, s_len)

    def body(q_ref, kp_ref, kc_ref, vp_ref, vc_ref, sink_ref, band_ref, o_ref):
        for hh in range(hb):
            qs = (_stack_heads(q_ref, hh * g, g) * (HEAD_DIM ** -0.5)).astype(BF16)
            p, _ = _swa_probs(qs, _window(kp_ref, kc_ref, hh), sink_ref[hh], band_ref[...])
            o = jnp.dot(p.astype(BF16), _window(vp_ref, vc_ref, hh), preferred_element_type=F32)
            for j in range(g):
                o_ref[:, (hh * g + j) * d:(hh * g + j + 1) * d] = o[j * w:(j + 1) * w]

    (o,), rid = _call("swa_fwd", body, (hk // hb, s_len // w), [qb, prev, cur, prev, cur, sb, bandb],
                      (q, k, k, v, v, sink, _swa_band(g, w)), [qb], [jax.ShapeDtypeStruct(q.shape, F32)], (),
                      ("parallel", "parallel"), rider)
    return o, rid


def swa_bwd(q, k, v, sink, o, do, rider=None):
    s_len = q.shape[0]
    hk = k.shape[1] // HEAD_DIM
    g = q.shape[1] // k.shape[1]
    w, d = WINDOW, HEAD_DIM
    scale = HEAD_DIM ** -0.5
    hb, qb, prev, cur, sb, bandb = _swa_specs(hk, g, s_len)

    def body(q_ref, kp_ref, kc_ref, vp_ref, vc_ref, sink_ref, band_ref, o_ref, do_ref, dq_ref, dkp_ref, dkc_ref, dvp_ref, dvc_ref,
             dsink_ref):
        @pl.when(pl.program_id(1) == 0)
        def _():
            dsink_ref[...] = jnp.zeros_like(dsink_ref)

        for hh in range(hb):
            lanes = slice(hh * d, (hh + 1) * d)
            qs = (_stack_heads(q_ref, hh * g, g) * scale).astype(BF16)
            kw, vw = _window(kp_ref, kc_ref, hh), _window(vp_ref, vc_ref, hh)
            p, ps = _swa_probs(qs, kw, sink_ref[hh], band_ref[...])
            dof = _stack_heads(do_ref, hh * g, g)
            dob = dof.astype(BF16)
            delta = jnp.sum(dof * _stack_heads(o_ref, hh * g, g), axis=-1, keepdims=True)
            dp = lax.dot_general(dob, vw, _DIMS["nt"], preferred_element_type=F32)
            ds = p * (dp - delta)
            dsb = ds.astype(BF16)
            dsink_ref[hh] -= ps * delta
            dq = jnp.dot(dsb, kw, preferred_element_type=F32) * scale
            for j in range(g):
                dq_ref[:, (hh * g + j) * d:(hh * g + j + 1) * d] = dq[j * w:(j + 1) * w]
            dkw = lax.dot_general(dsb, qs, _DIMS["tn"], preferred_element_type=F32)
            dvw = lax.dot_general(p.astype(BF16), dob, _DIMS["tn"], preferred_element_type=F32)
            dkp_ref[:, lanes] = dkw[:w]
            dkc_ref[:, lanes] = dkw[w:]
            dvp_ref[:, lanes] = dvw[:w]
            dvc_ref[:, lanes] = dvw[w:]

    kv_shape = jax.ShapeDtypeStruct(k.shape, F32)
    (dq, dkp, dkc, dvp, dvc, dsink), rid = _call(
        "swa_bwd", body, (hk // hb, s_len // w), [qb, prev, cur, prev, cur, sb, bandb, qb, qb],
        (q, k, k, v, v, sink, _swa_band(g, w), o, do),
        [qb, cur, cur, cur, cur, sb],
        [jax.ShapeDtypeStruct(q.shape, F32), kv_shape, kv_shape, kv_shape, kv_shape, jax.ShapeDtypeStruct((hk, g * w, 1), F32)],
        (), ("parallel", "arbitrary"), rider)

    def shift_up(a):
        return jnp.concatenate([a[w:], jnp.zeros_like(a[:w])], axis=0)

    return (dq, dkc + shift_up(dkp), dvc + shift_up(dvp), dsink), rid


def loss_call(y, target):
    s, d = y.shape
    tm = _pick(s, (512, 256, 128))

    def body(y_ref, t_ref, l_ref, dy_ref, dyb_ref):
        e = y_ref[...] - t_ref[...]
        dy = e * (1.0 / d)
        dy_ref[...] = dy
        dyb_ref[...] = dy.astype(BF16)

        @pl.when(pl.program_id(0) == 0)
        def _():
            l_ref[...] = jnp.zeros_like(l_ref)

        l_ref[...] += jnp.sum(jnp.sum(e * e, axis=0, keepdims=True), axis=1, keepdims=True) * (0.5 / d)

    row = pl.BlockSpec((tm, d), lambda i: (i, 0))
    (l, dy, dyb), _ = _call("loss_head", body, (s // tm,), [row, row], (y, target), [pl.BlockSpec((1, 1), lambda i: (0, 0)), row, row],
                            [jax.ShapeDtypeStruct((1, 1), F32), jax.ShapeDtypeStruct((s, d), F32), jax.ShapeDtypeStruct((s, d), BF16)],
                            (), ("arbitrary",))
    return l[0, 0], dy, dyb


def _row_tile(rows, cols, itemsize, block_bytes=1 << 20):
    target = max(16, block_bytes // (cols * itemsize))
    fits = [t for t in range(16, rows + 1, 16) if rows % t == 0 and t <= target]
    return fits[-1] if fits else rows


CAST_STEPS = 8


def cast_place(name, ws, p_idx, rider=None):
    n = len(ws)
    assert all(w.shape[0] % (16 * CAST_STEPS) == 0 for w in ws), [w.shape for w in ws]

    def body(p_ref, *refs):
        for w_ref, o_ref in zip(refs[:n], refs[n:]):
            o_ref[...] = w_ref[...].astype(BF16)

    return _call(
        name, body, (CAST_STEPS,), [pl.BlockSpec((w.shape[0] // CAST_STEPS, w.shape[1]), lambda i, pr: (i, 0)) for w in ws], tuple(ws),
        [pl.BlockSpec((None, w.shape[0] // CAST_STEPS, w.shape[1]), lambda i, pr: (pr[0], i, 0)) for w in ws],
        [jax.ShapeDtypeStruct((N_CHIPS,) + w.shape, BF16) for w in ws], (), ("parallel",), rider, prefetch=(p_idx,))


def chip_sum(name, grad, theirs, c_idx):
    ns, r, cols = grad.shape
    rh = r // 2
    tr = _row_tile(rh, cols, 2, 2 << 20)
    nb = rh // tr

    def body(c_ref, a_ref, b_ref, o_ref):
        o_ref[...] = (a_ref[...].astype(F32) + b_ref[...].astype(F32)).astype(o_ref.dtype)

    return pl.pallas_call(
        body, name=name,
        grid_spec=pltpu.PrefetchScalarGridSpec(
            num_scalar_prefetch=1, grid=(ns, nb),
            in_specs=[pl.BlockSpec((None, tr, cols), lambda q, i, cr: (q, cr[0] * nb + i, 0)),
                      pl.BlockSpec((None, tr, cols), lambda q, i, cr: (q, i, 0))],
            out_specs=pl.BlockSpec((None, tr, cols), lambda q, i, cr: (q, i, 0))),
        out_shape=jax.ShapeDtypeStruct((ns, rh, cols), BF16),
        compiler_params=pltpu.CompilerParams(dimension_semantics=("parallel", "parallel"), vmem_limit_bytes=VMEM_LIMIT),
    )(c_idx, grad, theirs)


def owner_sum(name, sums, got, pc_idx):
    ns, rh, cols = sums.shape
    tr = _row_tile(rh, cols, 4, 2 << 20)
    nb = rh // tr

    def body(pc_ref, a_ref, b_ref, o_ref):
        o_ref[...] = ((a_ref[...].astype(F32) + b_ref[0].astype(F32)) + b_ref[1].astype(F32)) + b_ref[2].astype(F32)

    return pl.pallas_call(
        body, name=name,
        grid_spec=pltpu.PrefetchScalarGridSpec(
            num_scalar_prefetch=1, grid=(nb,),
            in_specs=[pl.BlockSpec((None, tr, cols), lambda i, pc: (pc[0], i, 0)),
                      pl.BlockSpec((3, tr, cols), lambda i, pc: (0, i, 0))],
            out_specs=pl.BlockSpec((tr, cols), lambda i, pc: (pc[1] * nb + i, 0))),
        out_shape=jax.ShapeDtypeStruct((2 * rh, cols), F32),
        compiler_params=pltpu.CompilerParams(dimension_semantics=("parallel",), vmem_limit_bytes=VMEM_LIMIT),
    )(pc_idx, sums, got)


def adamw(name, w, g, m, v):
    r, cols = w.shape
    tr = _row_tile(r, cols, 4, 2 << 20)
    c1 = 1.0 / (1.0 - ADAM_B1 ** ADAM_STEP)
    c2 = 1.0 / (1.0 - ADAM_B2 ** ADAM_STEP)

    def body(w_ref, g_ref, m_ref, v_ref, go_ref, d_ref, nm_ref, nv_ref):
        gv = g_ref[...]
        nm = ADAM_B1 * m_ref[...] + (1.0 - ADAM_B1) * gv
        nv = ADAM_B2 * v_ref[...] + (1.0 - ADAM_B2) * (gv * gv)
        go_ref[...] = gv
        d_ref[...] = -ADAM_LR * ((nm * c1) / (jnp.sqrt(nv * c2) + ADAM_EPS) + ADAM_WD * w_ref[...])
        nm_ref[...] = nm
        nv_ref[...] = nv

    blk = pl.BlockSpec((tr, cols), lambda i: (i, 0))
    return _call(name, body, (r // tr,), [blk] * 4, (w, g, m, v), [blk] * 4, [jax.ShapeDtypeStruct((r, cols), F32)] * 4, (), ("parallel",))


def _win_layout(d_model):
    hf = hq = d_model // (2 * HEAD_DIM)
    hk = hq // 4
    sizes = [hf * HEAD_DIM, hf * HEAD_DIM, hf * HEAD_DIM, hf, hq * HEAD_DIM, hk * HEAD_DIM, hk * HEAD_DIM]
    return hf, hq, hk, sizes


class WinPlan:
    def __init__(self, d_model, ns=N_CHIPS):
        self.hf, self.hq, self.hk, self.sizes = _win_layout(d_model)
        self.ns, self.cs = ns, sum(self.sizes) // ns
        self.jump_at = sum(self.sizes[:4])
        self.jump_by = -self.jump_at % _LANES
        self.base = [self.pos(s * self.cs) // _LANES * _LANES for s in range(ns)]
        ends = [self.pos((s + 1) * self.cs - 1) + 1 - self.base[s] for s in range(ns)]
        self.width = -(-max(ends) // _LANES) * _LANES
        self.total = -(-max(b + self.width for b in self.base) // 1024) * 1024
        starts = [0]
        for sz in self.sizes:
            starts.append(starts[-1] + sz)
        self.segments = [(self.pos(a), sz) for a, sz in zip(starts, self.sizes)]

    def pos(self, g):
        return g if g < self.jump_at else g + self.jump_by

    def pieces(self, s):
        g0, g1 = s * self.cs, (s + 1) * self.cs
        cuts = [g0] + ([self.jump_at] if g0 < self.jump_at < g1 else []) + [g1]
        return [(a - g0, b - a, self.pos(a) - self.base[s]) for a, b in zip(cuts[:-1], cuts[1:])]

    def place(self, w, s):
        parts, at = [], 0
        for t0, n, j0 in self.pieces(s):
            parts += [jnp.zeros((w.shape[0], j0 - at), w.dtype), w[:, t0:t0 + n]]
            at = j0 + n
        return jnp.concatenate(parts + [jnp.zeros((w.shape[0], self.width - at), w.dtype)], axis=1)

    def unplace(self, slab, s):
        return jnp.concatenate([slab[:, j0:j0 + n] for _, n, j0 in self.pieces(s)], axis=1)

    def assemble(self, slabs):
        return sum(jnp.pad(slabs[s], ((0, 0), (b, self.total - b - self.width))) for s, b in enumerate(self.base))

    def split(self, full):
        return jnp.stack([full[:, b:b + self.width] for b in self.base])


def _attn_inputs(proj, sm, positions):
    s_len = proj.shape[0]
    plan = WinPlan(sm["norm_mix_g"].shape[0])
    hf, hq, hk = plan.hf, plan.hq, plan.hk
    grp = hq // hk
    q_f, k_f, v_f, f_logit, q_s, k_s, v_s = [proj[:, a:a + n] for a, n in plan.segments]

    q_f = head_rms("fox_qnorm", q_f, sm["fox_q_norm_g"])
    k_f = head_rms("fox_knorm", k_f, sm["fox_k_norm_g"])
    log_f = jax.nn.log_sigmoid(f_logit + sm["b_forget"])
    c = time_cumsum(log_f.T)

    inv_freq = ROPE_THETA ** (-jnp.arange(0, HEAD_DIM, 2, dtype=F32) / HEAD_DIM)
    ang = positions.astype(F32)[:, None] * inv_freq
    cos, sin = jnp.cos(ang), jnp.sin(ang)
    q_s = head_rms_rope("swa_qnorm", q_s, sm["swa_q_norm_g"], cos, sin)
    k_s = head_rms_rope("swa_knorm", k_s, sm["swa_k_norm_g"], cos, sin)
    sink = jnp.broadcast_to(sm["swa_sinks"].reshape(hk, grp, 1, 1), (hk, grp, WINDOW, 1)).reshape(hk, grp * WINDOW, 1)
    tq = fox_tile(s_len)
    return (q_f, k_f, v_f, c[:, :, None], c.reshape(hf, s_len // tq, 1, tq)), (q_s, k_s, v_s, sink)


_BIG = ("ffn1_w_gate", "ffn1_w_up", "ffn1_w_down", "w_in", "w_out", "ffn2_w_gate", "ffn2_w_up", "ffn2_w_down")
_SMALL = ("norm_ffn1_g", "norm_mix_g", "b_forget", "fox_q_norm_g", "fox_k_norm_g", "swa_q_norm_g", "swa_k_norm_g", "swa_sinks",
          "out_norm_fox_g", "out_norm_swa_g", "norm_ffn2_g")
_ATTN_SMALL = ("norm_mix_g", "b_forget", "fox_q_norm_g", "fox_k_norm_g", "swa_q_norm_g", "swa_k_norm_g", "swa_sinks")
_ALL = ("norm_ffn1_g", "ffn1_w_gate", "ffn1_w_up", "ffn1_w_down", "norm_mix_g", "w_in", "b_forget", "fox_q_norm_g", "fox_k_norm_g",
        "swa_q_norm_g", "swa_k_norm_g", "swa_sinks", "out_norm_fox_g", "out_norm_swa_g", "w_out", "norm_ffn2_g", "ffn2_w_gate",
        "ffn2_w_up", "ffn2_w_down")


def _pack_small(d):
    parts = []
    for k in _SMALL:
        v = d[k].reshape(-1)
        rows = -(-v.shape[0] // _LANES)
        parts.append(jnp.pad(v, (0, rows * _LANES - v.shape[0])).reshape(rows, _LANES))
    a = jnp.concatenate(parts, axis=0)
    return jnp.pad(a, ((0, -a.shape[0] % 8), (0, 0)))


def _unpack_small(a, like):
    out, r0 = {}, 0
    for k in _SMALL:
        nvals = like[k].shape[1]
        rows = -(-nvals // _LANES)
        out[k] = a[r0:r0 + rows].reshape(-1)[:nvals].reshape(1, nvals)
        r0 += rows
    return out


def _stacked(w):
    return w.reshape(-1, w.shape[-1])


def _local_step(shards, sm, x, positions, target, p_idx, c_idx, pc_idx):
    ns = N_CHIPS
    full = {}

    def fetch(*jobs):
        names = list(dict.fromkeys(n for n, _, _ in jobs))
        return names, gather([bufs[n] for n in names], [(names.index(n), kind, part) for n, kind, part in jobs])

    def take(names, rid):
        for n, b in zip(names, rid[0]):
            bufs[n] = b

    n1 = ["ffn1_w_gate", "ffn1_w_up", "ffn1_w_down"]
    n2 = ["ffn2_w_gate", "ffn2_w_up", "ffn2_w_down"]
    later = ["w_in", "w_out"] + n2
    placed, _ = cast_place("cast_place_ffn1", [shards[n] for n in n1], p_idx)
    bufs = dict(zip(n1, placed))
    gate1, up1, down1 = n1
    gate2, up2, down2 = n2
    names, rider = fetch((gate1, "ici", WHOLE), (up1, "ici", WHOLE))
    placed, rid = cast_place("cast_place_later", [shards[n] for n in later], p_idx, rider=rider)
    bufs.update(zip(later, placed))
    take(names, rid)
    names, rider = fetch((gate1, "d2d", WHOLE), (up1, "d2d", WHOLE), (down1, "ici", (0, 1, 4)))
    xn1, r1, rid = rms_fwd("ffn1_norm", x, sm["norm_ffn1_g"], BF16, rider=rider)
    take(names, rid)
    names, rider = fetch((down1, "ici", (1, 4, 4)), ("w_in", "ici", (0, 1, 4)))
    (hid1, hdg1, hdu1), rid = ffn_gu("ffn1_gu", xn1, bufs[gate1], bufs[up1], rider=rider)
    take(names, rid)
    names, rider = fetch((down1, "d2d", WHOLE))
    take(names, run_step("gather_d2d_ffn1_down", rider))
    wd1 = _stacked(bufs[down1])
    names, rider = fetch(("w_in", "ici", (1, 4, 4)))
    h1, rid = mm_nn("ffn1_down", hid1, wd1, scale=0.5, resid=x, rider=rider)
    take(names, rid)

    names, rider = fetch(("w_in", "d2d", WHOLE))
    u, r_mix, rid = rms_fwd("mix_norm", h1, sm["norm_mix_g"], BF16, rider=rider)
    take(names, rid)
    names, rider = fetch(("w_out", "ici", WHOLE))
    plan = WinPlan(x.shape[1])
    win = plan.assemble(bufs["w_in"])
    proj, rid = mm_nn("mix_inproj", u, win, rider=rider)
    take(names, rid)
    sm_attn = {k: sm[k] for k in _ATTN_SMALL}
    (fox_in, swa_in), attn_vjp = jax.vjp(lambda pr, s: _attn_inputs(pr, s, positions), proj, sm_attn)
    names, rider = fetch((gate2, "ici", WHOLE), (up2, "ici", (0, 1, 4)), ("w_out", "d2d", WHOLE))
    (o_f, lse), rid = fox_fwd(*fox_in, rider=rider)
    take(names, rid)
    names, rider = fetch((up2, "ici", (1, 4, 4)), (gate2, "d2d", WHOLE), (up2, "d2d", (0, 1, 4)))
    o_s, rid = swa_fwd(*swa_in, rider=rider)
    take(names, rid)
    o_fox, o_swa = o_f, o_s
    nf, r_fox = rms_fwd("out_norm_fox", o_fox, sm["out_norm_fox_g"], BF16)
    nsw, r_swa = rms_fwd("out_norm_swa", o_swa, sm["out_norm_swa_g"], BF16)
    o = jnp.concatenate([nf, nsw], axis=-1)
    wout = _stacked(bufs["w_out"])
    names, rider = fetch((down2, "ici", (0, 1, 4)), (up2, "d2d", (1, 4, 4)))
    h2, rid = mm_nn("out_proj", o, wout, resid=h1, rider=rider)
    take(names, rid)

    xn2, r2 = rms_fwd("ffn2_norm", h2, sm["norm_ffn2_g"], BF16)
    names, rider = fetch((down2, "ici", (1, 4, 4)))
    (hid2, hdg2, hdu2), rid = ffn_gu("ffn2_gu", xn2, bufs[gate2], bufs[up2], rider=rider)
    take(names, rid)
    names, rider = fetch((down2, "d2d", WHOLE))
    take(names, run_step("gather_d2d_ffn2_down", rider))
    wd2 = _stacked(bufs["ffn2_w_down"])
    y, _ = mm_nn("ffn2_down", hid2, wd2, scale=0.5, resid=h2)
    loss, dy, dy_b = loss_call(y, target)

    red = {}

    def grad(n, g):
        red[n] = {"grad": g.reshape(ns, -1, g.shape[-1])}

    def ride(*steps):
        def done(rid):
            a0 = n0 = 0
            for rd, cb in steps:
                cb(rid[0][a0:a0 + len(rd.aliased)], rid[1][n0:n0 + len(rd.news)])
                a0, n0 = a0 + len(rd.aliased), n0 + len(rd.news)

        return (combine(*[s[0] for s in steps]) if len(steps) > 1 else steps[0][0]), done

    def xchg(*names):
        def cb(al, news):
            for n, t in zip(names, news):
                red[n]["sum"] = chip_sum("chip_sum_" + n, red[n]["grad"], t, c_idx)

        return exchange_halves([red[n]["grad"] for n in names]), cb

    def scat(n, part=WHOLE):
        def cb(al, news):
            red[n]["got"] = (al or news)[0]

        return scatter_to_owner([red[n]["sum"]], [red[n]["got"]] if "got" in red[n] else None, part), cb

    def own(n):
        red[n]["half"] = owner_sum("owner_sum_" + n, red[n]["sum"], red[n]["got"], pc_idx)

    def join(*names):
        return join_halves([red[n]["half"] for n in names]), lambda al, news: full.update(zip(names, al))

    dwd2, _ = mm_tn("ffn2_dwd", hid2, dy_b, out_dtype=BF16, scale=0.5)
    grad(down2, dwd2)
    rider, done = ride(xchg(down2))
    (dg2, du2), rid = ffn_dh("ffn2_dh", dy_b, wd2, hdg2, hdu2, ns, 0.5, rider=rider)
    done(rid)
    rider, done = ride(scat(down2, (0, 1, 2)))
    dwg2, rid = mm_tn_sharded("ffn2_dwg", xn2, dg2, ns, rider=rider)
    done(rid)
    grad(gate2, dwg2)
    rider, done = ride(scat(down2, (1, 2, 2)), xchg(gate2))
    dwu2, rid = mm_tn_sharded("ffn2_dwu", xn2, du2, ns, rider=rider)
    done(rid)
    grad(up2, dwu2)
    rider, done = ride(scat(gate2, (0, 1, 2)), xchg(up2))
    dxn, rid = mm_nt_sharded("ffn2_dxn_g", dg2, bufs[gate2], rider=rider)
    done(rid)
    rider, done = ride(scat(gate2, (1, 2, 2)))
    dxn, rid = mm_nt_sharded("ffn2_dxn_u", du2, bufs[up2], resid=dxn, rider=rider)
    done(rid)
    dh2, dgain_ffn2, dh2_b = rms_bwd("ffn2_dnorm", h2, sm["norm_ffn2_g"], r2, dxn, dres=dy, also_bf16=True)
    own(down2)
    own(gate2)

    do, _ = mm_nt("out_do", dh2_b, wout)
    dwout, _ = mm_tn("out_dw", o, dh2_b, out_dtype=BF16)
    cf = o_fox.shape[1]
    d_fox, dgain_fox = rms_bwd("out_dnorm_fox", o_fox, sm["out_norm_fox_g"], r_fox, do[:, :cf])
    d_swa, dgain_swa = rms_bwd("out_dnorm_swa", o_swa, sm["out_norm_swa_g"], r_swa, do[:, cf:])
    grad("w_out", dwout)
    rider, done = ride(scat(up2))
    swa_cts, rid = swa_bwd(*swa_in, o_s, d_swa, rider=rider)
    done(rid)
    own(up2)
    rider, done = ride(xchg("w_out"), join(down2, gate2, up2))
    fox_cts, rid = fox_bwd(*fox_in, o_f, lse, d_fox, rider=rider)
    done(rid)
    dproj, dsm_attn = attn_vjp((tuple(fox_cts), tuple(swa_cts)))
    dproj = dproj.astype(BF16)

    rider, done = ride(scat("w_out"))
    du, rid = mm_nt("mix_du", dproj, win, rider=rider)
    done(rid)
    dwin, _ = mm_tn("mix_dwin", u, dproj, out_dtype=BF16)
    grad("w_in", plan.split(dwin))
    rider, done = ride(xchg("w_in"))
    dh1, dgain_mix, dh1_b, rid = rms_bwd("mix_dnorm", h1, sm["norm_mix_g"], r_mix, du, dres=dh2, rider=rider, also_bf16=True)
    done(rid)
    own("w_out")

    rider, done = ride(scat("w_in", (0, 1, 2)))
    dwd1, rid = mm_tn("ffn1_dwd", hid1, dh1_b, out_dtype=BF16, scale=0.5, rider=rider)
    done(rid)
    grad(down1, dwd1)
    rider, done = ride(scat("w_in", (1, 2, 2)), xchg(down1))
    (dg1, du1), rid = ffn_dh("ffn1_dh", dh1_b, wd1, hdg1, hdu1, ns, 0.5, rider=rider)
    done(rid)
    own("w_in")
    rider, done = ride(scat(down1, (0, 1, 2)), join("w_out"))
    dwg1, rid = mm_tn_sharded("ffn1_dwg", xn1, dg1, ns, rider=rider)
    done(rid)
    grad(gate1, dwg1)
    rider, done = ride(scat(down1, (1, 2, 2)), xchg(gate1), join("w_in"))
    dwu1, rid = mm_tn_sharded("ffn1_dwu", xn1, du1, ns, rider=rider)
    done(rid)
    grad(up1, dwu1)
    own(down1)
    rider, done = ride(scat(gate1), xchg(up1), join(down1))
    dxn, rid = mm_nt_sharded("ffn1_dxn_g", dg1, bufs[gate1], rider=rider)
    done(rid)
    own(gate1)
    rider, done = ride(scat(up1, (0, 3, 4)))
    dxn, rid = mm_nt_sharded("ffn1_dxn_u", du1, bufs[up1], resid=dxn, rider=rider)
    done(rid)
    dx, dgain_ffn1 = rms_bwd("ffn1_dnorm", x, sm["norm_ffn1_g"], r1, dxn, dres=dh1)

    rider, done = ride(scat(up1, (3, 4, 4)), join(gate1))
    done(run_step("reduce_tail", rider))
    own(up1)
    rider, done = ride(join(up1))
    done(run_step("join_tail", rider))

    g_small = dict(dsm_attn)
    g_small["norm_mix_g"] = g_small["norm_mix_g"] + dgain_mix
    g_small.update(norm_ffn1_g=dgain_ffn1, norm_ffn2_g=dgain_ffn2, out_norm_fox_g=dgain_fox, out_norm_swa_g=dgain_swa)
    return loss, dx, full, g_small


def kernel(x, positions, norm_ffn1_g, ffn1_w_gate, ffn1_w_up, ffn1_w_down, norm_mix_g, w_in, b_forget, fox_q_norm_g, fox_k_norm_g, swa_q_norm_g, swa_k_norm_g, swa_sinks, out_norm_fox_g, out_norm_swa_g, w_out, norm_ffn2_g, ffn2_w_gate, ffn2_w_up, ffn2_w_down, loss_target, m_norm_ffn1_g, m_ffn1_w_gate, m_ffn1_w_up, m_ffn1_w_down, m_norm_mix_g, m_w_in, m_b_forget, m_fox_q_norm_g, m_fox_k_norm_g, m_swa_q_norm_g, m_swa_k_norm_g, m_swa_sinks, m_out_norm_fox_g, m_out_norm_swa_g, m_w_out, m_norm_ffn2_g, m_ffn2_w_gate, m_ffn2_w_up, m_ffn2_w_down, v_norm_ffn1_g, v_ffn1_w_gate, v_ffn1_w_up, v_ffn1_w_down, v_norm_mix_g, v_w_in, v_b_forget, v_fox_q_norm_g, v_fox_k_norm_g, v_swa_q_norm_g, v_swa_k_norm_g, v_swa_sinks, v_out_norm_fox_g, v_out_norm_swa_g, v_w_out, v_norm_ffn2_g, v_ffn2_w_gate, v_ffn2_w_up, v_ffn2_w_down):
    args = dict(locals())
    w = {k: args[k] for k in _ALL}
    m = {k: args["m_" + k] for k in _ALL}
    v = {k: args["v_" + k] for k in _ALL}
    c_idx = lax.axis_index("c").astype(jnp.int32).reshape(1)
    p_idx = (2 * lax.axis_index("x") + lax.axis_index("y")).astype(jnp.int32).reshape(1)
    pc_idx = jnp.concatenate([p_idx, c_idx])

    small = {k: w[k] for k in _SMALL}
    shards = {k: w[k][0] for k in _BIG}
    plan = WinPlan(x.shape[-1])
    shards["w_in"] = lax.switch(p_idx[0], [functools.partial(plan.place, s=s) for s in range(N_CHIPS)], shards["w_in"])
    loss, grad_x, g_shard, g_small = _local_step(shards, {k: w[k][0] for k in _SMALL}, x[0], positions[0], loss_target[0],
                                                 p_idx, c_idx, pc_idx)
    g_shard["w_in"] = lax.switch(p_idx[0], [functools.partial(plan.unplace, s=s) for s in range(N_CHIPS)], g_shard["w_in"])
    loss = lax.psum(loss, ("x", "y", "c"))
    g_small_sum = _unpack_small(all_reduce_small(_pack_small({k: g_small[k].reshape(1, -1) for k in _SMALL})), small)

    grad_w, delta, new_m, new_v = {}, {}, {}, {}
    for k in _BIG:
        (g, d, nm, nv), _ = adamw("adamw_" + k, w[k][0], g_shard[k], m[k][0], v[k][0])
        grad_w[k], delta[k], new_m[k], new_v[k] = g[None], d[None], nm[None], nv[None]
    (_, d, nm, nv), _ = adamw("adamw_small", _pack_small(small), _pack_small(g_small_sum), _pack_small({k: m[k] for k in _SMALL}),
                              _pack_small({k: v[k] for k in _SMALL}))
    grad_w.update(g_small_sum)
    delta.update(_unpack_small(d, small))
    new_m.update(_unpack_small(nm, small))
    new_v.update(_unpack_small(nv, small))

    return (loss, grad_x[None], *[grad_w[k] for k in _ALL], *[delta[k] for k in _ALL], *[new_m[k] for k in _ALL], *[new_v[k] for k in _ALL])
```

```python
import functools

import jax
import jax.numpy as jnp
from jax import lax
from jax.experimental import pallas as pl
from jax.experimental.pallas import tpu as pltpu

F32 = jnp.float32
BF16 = jnp.bfloat16

HEAD_DIM = 64
WINDOW = 128
ROPE_THETA = 10000.0
EPS = 1e-6
N_CHIPS = 4
N_DEV = 8

ADAM_LR = 0.001
ADAM_B1 = 0.9
ADAM_B2 = 0.999
ADAM_EPS = 1e-08
ADAM_WD = 0.01
ADAM_STEP = 10

V7X_VMEM_BYTES = 64 * 1024 * 1024
VMEM_LIMIT = V7X_VMEM_BYTES - 8 * 1024 * 1024
MASK_VALUE = -1e30

_MESH = pl.DeviceIdType.MESH
_HBM = pl.BlockSpec(memory_space=pl.ANY)
_DIMS = {"nn": (((1,), (0,)), ((), ())), "nt": (((1,), (1,)), ((), ())), "tn": (((0,), (0,)), ((), ()))}


def _pick(n, prefs):
    for p in prefs:
        if n % p == 0:
            return p
    return n


class Rider:
    def __init__(self, reads, aliased, news, nsem, build):
        self.reads, self.aliased, self.news, self.nsem, self.build = list(reads), list(aliased), list(news), nsem, build


class _Shifted:
    def __init__(self, ref, off):
        self.ref, self.off = ref, off

    @property
    def at(self):
        return self

    def __getitem__(self, k):
        return self.ref.at[k + self.off]


def combine(*riders):
    def build(reads, al, news, ssem, rsem):
        out = ([], [], [])
        r0 = a0 = n0 = s0 = 0
        for rd in riders:
            nr, na, nn = len(rd.reads), len(rd.aliased), len(rd.news)
            part = rd.build(reads[r0:r0 + nr], al[a0:a0 + na], news[n0:n0 + nn], _Shifted(ssem, s0), _Shifted(rsem, s0))
            for acc, lst in zip(out, part):
                acc.extend(lst)
            r0, a0, n0, s0 = r0 + nr, a0 + na, n0 + nn, s0 + rd.nsem
        return out

    return Rider(sum((r.reads for r in riders), []), sum((r.aliased for r in riders), []), sum((r.news for r in riders), []),
                 sum(r.nsem for r in riders), build)


def _me():
    return lax.axis_index("x"), lax.axis_index("y"), lax.axis_index("c")


def _other_chips(x, y):
    return [(1 - x, y), (x, 1 - y), (1 - x, 1 - y)]


WHOLE = (0, 1, 1)


def _rows(ref, start, rows, part=WHOLE):
    k0, k1, n = part
    assert rows % n == 0, (rows, part)
    idx = (slice(None),) * (len(ref.shape) - 2) + (pl.ds(start + k0 * (rows // n), (k1 - k0) * (rows // n)), slice(None))
    return ref.at[idx]


def _half(ref, h, part=WHOLE):
    rows = ref.shape[-2] // 2
    return _rows(ref, h * rows, rows, part)


def _remote(src, dst, ssem, rsem, k, to):
    return pltpu.make_async_remote_copy(src_ref=src, dst_ref=dst, send_sem=ssem.at[k], recv_sem=rsem.at[k], device_id=to,
                                        device_id_type=_MESH)


def _later(*args):
    return functools.partial(_remote, *args)


def gather(bufs, jobs):
    def build(reads, al, news, ssem, rsem):
        x, y, c = _me()
        p = 2 * x + y
        starts, arrivals = [], []
        for n, (b, kind, part) in enumerate(jobs):
            for j, chip in enumerate(_other_chips(x, y)):
                q = 2 * chip[0] + chip[1]
                if kind == "ici":
                    src, landing, to = _half(al[b].at[p], c, part), _half(al[b].at[q], c, part), (*chip, c)
                else:
                    src, landing, to = _half(al[b].at[q], c, part), _half(al[b].at[q], 1 - c, part), (x, y, 1 - c)
                starts.append(_later(src, src, ssem, rsem, 3 * n + j, to))
                arrivals.append(_later(landing, landing, ssem, rsem, 3 * n + j, to))
        return starts, arrivals, starts

    return Rider([], bufs, [], 3 * len(jobs), build)


def exchange_halves(grads):
    def build(reads, al, news, ssem, rsem):
        x, y, c = _me()
        cps = [_later(_half(g, 1 - c), t, ssem, rsem, w, (x, y, 1 - c)) for w, (g, t) in enumerate(zip(reads, news))]
        return cps, cps, cps

    return Rider(grads, [], [jax.ShapeDtypeStruct((g.shape[0], g.shape[1] // 2, g.shape[2]), g.dtype) for g in grads], len(grads), build)


def scatter_to_owner(sums, gots=None, part=WHOLE):
    def build(reads, al, news, ssem, rsem):
        x, y, c = _me()
        cps = []
        for w, (s, got) in enumerate(zip(reads, al or news)):
            rows = s.shape[-2]
            for j, chip in enumerate(_other_chips(x, y)):
                cps.append(_later(_rows(s.at[2 * chip[0] + chip[1]], 0, rows, part), _rows(got.at[j], 0, rows, part), ssem, rsem,
                                  3 * w + j, (*chip, c)))
        return cps, cps, cps

    news = [] if gots else [jax.ShapeDtypeStruct((3,) + s.shape[1:], s.dtype) for s in sums]
    return Rider(sums, gots or [], news, 3 * len(sums), build)


def join_halves(fulls):
    def build(reads, al, news, ssem, rsem):
        x, y, c = _me()
        starts, arrivals = [], []
        for w, f in enumerate(al):
            mine, landing = _half(f, c), _half(f, 1 - c)
            starts.append(_later(mine, mine, ssem, rsem, w, (x, y, 1 - c)))
            arrivals.append(_later(landing, landing, ssem, rsem, w, (x, y, 1 - c)))
        return starts, arrivals, starts

    return Rider([], fulls, [], len(fulls), build)


def _start_and_wait(rider, reads, al, news, ssem, rsem, first, last):
    @pl.when(first)
    def _():
        for cp in rider.build(reads, al, news, ssem, rsem)[0]:
            cp().start()

    def finish():
        @pl.when(last)
        def _():
            _, arrivals, sends = rider.build(reads, al, news, ssem, rsem)
            for cp in arrivals:
                cp().wait_recv()
            for cp in sends:
                cp().wait_send()

    return finish


def _call(name, body, grid, in_specs, args, out_specs, out_shape, scratch=(), semantics=None, rider=None, prefetch=()):
    n_pre, n_in, n_out, n_scr = len(prefetch), len(args), len(out_shape), len(scratch)
    nr, na, nn = (len(rider.reads), len(rider.aliased), len(rider.news)) if rider else (0, 0, 0)

    def wrapped(*refs):
        pre, refs = refs[:n_pre], refs[n_pre:]
        ins, reads = refs[:n_in], refs[n_in:n_in + nr]
        o0 = n_in + nr + na
        outs, al, news = refs[o0:o0 + n_out], refs[o0 + n_out:o0 + n_out + na], refs[o0 + n_out + na:o0 + n_out + na + nn]
        s0 = o0 + n_out + na + nn
        scr, (ssem, rsem) = refs[s0:s0 + n_scr], refs[s0 + n_scr:]
        first = functools.reduce(jnp.logical_and, [pl.program_id(a) == 0 for a in range(len(grid))])
        last = functools.reduce(jnp.logical_and, [pl.program_id(a) == g - 1 for a, g in enumerate(grid)])
        finish = _start_and_wait(rider, reads, al, news, ssem, rsem, first, last)
        body(*pre, *ins, *outs, *scr)
        finish()

    kernel_fn, all_in, all_out, shapes, scr = body, list(in_specs), list(out_specs), list(out_shape), list(scratch)
    operands, aliases = (*prefetch, *args), {}
    if rider:
        kernel_fn, semantics = wrapped, ("arbitrary",) * len(grid)
        all_in += [_HBM] * (nr + na)
        all_out += [_HBM] * (na + nn)
        shapes += [jax.ShapeDtypeStruct(a.shape, a.dtype) for a in rider.aliased] + rider.news
        scr += [pltpu.SemaphoreType.DMA((rider.nsem,)), pltpu.SemaphoreType.DMA((rider.nsem,))]
        operands += (*rider.reads, *rider.aliased)
        aliases = {n_pre + n_in + nr + i: n_out + i for i in range(na)}
    params = pltpu.CompilerParams(dimension_semantics=semantics, vmem_limit_bytes=VMEM_LIMIT)
    if n_pre:
        spec = pltpu.PrefetchScalarGridSpec(num_scalar_prefetch=n_pre, grid=grid, in_specs=all_in, out_specs=all_out, scratch_shapes=scr)
        outs = pl.pallas_call(kernel_fn, name=name, grid_spec=spec, out_shape=shapes, input_output_aliases=aliases, compiler_params=params)(*operands)
    else:
        outs = pl.pallas_call(kernel_fn, name=name, grid=grid, in_specs=all_in, out_specs=all_out, out_shape=shapes, scratch_shapes=scr,
                              input_output_aliases=aliases, compiler_params=params)(*operands)
    return list(outs[:n_out]), ((list(outs[n_out:n_out + na]), list(outs[n_out + na:])) if rider else None)


def run_step(name, rider):
    nr, na, nn = len(rider.reads), len(rider.aliased), len(rider.news)

    def body(*refs):
        reads = refs[:nr]
        al, news = refs[nr + na:nr + 2 * na], refs[nr + 2 * na:nr + 2 * na + nn]
        ssem, rsem = refs[nr + 2 * na + nn:]
        starts, arrivals, sends = rider.build(reads, al, news, ssem, rsem)
        for cp in starts:
            cp().start()
        for cp in arrivals:
            cp().wait_recv()
        for cp in sends:
            cp().wait_send()

    outs = pl.pallas_call(
        body, name=name, in_specs=[_HBM] * (nr + na), out_specs=[_HBM] * (na + nn),
        out_shape=[jax.ShapeDtypeStruct(a.shape, a.dtype) for a in rider.aliased] + rider.news,
        input_output_aliases={nr + i: i for i in range(na)},
        scratch_shapes=[pltpu.SemaphoreType.DMA((rider.nsem,)), pltpu.SemaphoreType.DMA((rider.nsem,))],
    )(*rider.reads, *rider.aliased)
    return list(outs[:na]), list(outs[na:])


def all_reduce_small(v):
    rows, lanes = v.shape

    def body(v_ref, o_ref, slots, send_sems, recv_sems):
        x, y, c = _me()
        me = 4 * x + 2 * y + c
        slots[me] = v_ref[...]
        cps = []
        for k in range(1, N_DEV):
            peer = (x ^ (k >> 2), y ^ ((k >> 1) & 1), c ^ (k & 1))
            cps.append(_remote(v_ref, slots.at[me], send_sems, recv_sems, k - 1, peer))
            cps[-1].start()
        for k in range(1, N_DEV):
            theirs = slots.at[me ^ k]
            _remote(theirs, theirs, send_sems, recv_sems, k - 1, (x, y, c)).wait_recv()
        for cp in cps:
            cp.wait_send()
        acc = slots[0]
        for i in range(1, N_DEV):
            acc = acc + slots[i]
        o_ref[...] = acc

    return pl.pallas_call(
        body, name="all_reduce_small",
        in_specs=[pl.BlockSpec(memory_space=pltpu.VMEM)], out_specs=pl.BlockSpec(memory_space=pltpu.VMEM),
        out_shape=jax.ShapeDtypeStruct((rows, lanes), F32),
        scratch_shapes=[pltpu.VMEM((N_DEV, rows, lanes), F32), pltpu.SemaphoreType.DMA((N_DEV - 1,)), pltpu.SemaphoreType.DMA((N_DEV - 1,))],
    )(v)


def _mm_call(name, mode, a, b, a_spec, b_spec, out_shape, out_spec, grid, acc_shape, scale=1.0, resid=None, resid_spec=None, rider=None):
    nk = grid[2]
    dims = _DIMS[mode]
    has_resid = resid is not None

    def body(*refs):
        a_ref, b_ref = refs[:2]
        r_ref = refs[2] if has_resid else None
        o_ref = refs[3] if has_resid else refs[2]

        def finish(r):
            if scale != 1.0:
                r = r * scale
            if has_resid:
                r = r_ref[...].astype(F32) + r
            o_ref[...] = r.astype(o_ref.dtype)

        part = lax.dot_general(a_ref[...].astype(BF16), b_ref[...].astype(BF16), dims, preferred_element_type=F32)
        if nk == 1:
            finish(part)
            return
        acc_ref = refs[-1]
        k = pl.program_id(2)

        @pl.when(k == 0)
        def _():
            acc_ref[...] = part

        @pl.when(k > 0)
        def _():
            acc_ref[...] += part

        @pl.when(k == nk - 1)
        def _():
            finish(acc_ref[...])

    in_specs = [a_spec, b_spec] + ([resid_spec] if has_resid else [])
    args = (a, b) + ((resid,) if has_resid else ())
    (out,), rid = _call(name, body, grid, in_specs, args, [out_spec], [out_shape], [pltpu.VMEM(acc_shape, F32)] if nk > 1 else [],
                        ("parallel", "parallel", "arbitrary"), rider)
    return out, rid


MM_VMEM_BUDGET = 40 * 1024 * 1024
_TILE_OPTS = (2048, 1408, 1024, 512, 256, 128)


def _tiles(m, n, kd, a_item, b_item, o_item, r_item=0, tm=None, tn=None, tk=None):
    def opts(full, fixed, cap):
        return [fixed] if fixed else [t for t in _TILE_OPTS if t <= cap and full % t == 0] or [full]

    best = None
    for cm in opts(m, tm, 1408):
        for cn in opts(n, tn, 1408):
            for ck in opts(kd, tk, 2048):
                blocks = cm * ck * a_item + ck * cn * b_item + cm * cn * (o_item + r_item)
                casts = (cm * ck * 2 if a_item == 4 else 0) + (ck * cn * 2 if b_item == 4 else 0)
                if 2 * blocks + cm * cn * 4 + casts <= MM_VMEM_BUDGET:
                    key = (cm * cn * ck, ck)
                    if best is None or key > best[0]:
                        best = (key, (cm, cn, ck))
    assert best is not None, (m, n, kd)
    return best[1]


def _item(x):
    return jnp.dtype(x.dtype).itemsize


def mm_nn(name, a, b, *, out_dtype=F32, scale=1.0, resid=None, rider=None):
    m, kd = a.shape
    n = b.shape[1]
    tm, tn, tk = _tiles(m, n, kd, _item(a), _item(b), jnp.dtype(out_dtype).itemsize, 0 if resid is None else _item(resid))
    o_spec = pl.BlockSpec((tm, tn), lambda i, j, k: (i, j))
    return _mm_call(
        name, "nn", a, b, pl.BlockSpec((tm, tk), lambda i, j, k: (i, k)), pl.BlockSpec((tk, tn), lambda i, j, k: (k, j)),
        jax.ShapeDtypeStruct((m, n), out_dtype), o_spec, (m // tm, n // tn, kd // tk), (tm, tn), scale, resid, o_spec, rider)


def mm_nt(name, a, b, *, out_dtype=F32, scale=1.0, resid=None, rider=None):
    m, kd = a.shape
    n = b.shape[0]
    tm, tn, tk = _tiles(m, n, kd, _item(a), _item(b), jnp.dtype(out_dtype).itemsize, 0 if resid is None else _item(resid))
    o_spec = pl.BlockSpec((tm, tn), lambda i, j, k: (i, j))
    return _mm_call(
        name, "nt", a, b, pl.BlockSpec((tm, tk), lambda i, j, k: (i, k)), pl.BlockSpec((tn, tk), lambda i, j, k: (j, k)),
        jax.ShapeDtypeStruct((m, n), out_dtype), o_spec, (m // tm, n // tn, kd // tk), (tm, tn), scale, resid, o_spec, rider)


def mm_tn(name, a, b, *, out_dtype=F32, scale=1.0, rider=None):
    kd, m = a.shape
    n = b.shape[1]
    tm, tn, tk = _tiles(m, n, kd, _item(a), _item(b), jnp.dtype(out_dtype).itemsize)
    return _mm_call(
        name, "tn", a, b, pl.BlockSpec((tk, tm), lambda i, j, k: (k, i)), pl.BlockSpec((tk, tn), lambda i, j, k: (k, j)),
        jax.ShapeDtypeStruct((m, n), out_dtype), pl.BlockSpec((tm, tn), lambda i, j, k: (i, j)),
        (m // tm, n // tn, kd // tk), (tm, tn), scale, rider=rider)


def mm_nt_sharded(name, a, w, *, resid=None, rider=None):
    m = a.shape[0]
    ns, n, c = w.shape
    tm, tn, _ = _tiles(m, n, c, _item(a), _item(w), 4, 0 if resid is None else _item(resid), tk=c)
    o_spec = pl.BlockSpec((tm, tn), lambda i, j, k: (i, j))
    return _mm_call(
        name, "nt", a, w, pl.BlockSpec((tm, c), lambda i, j, k: (i, k)), pl.BlockSpec((None, tn, c), lambda i, j, k: (k, j, 0)),
        jax.ShapeDtypeStruct((m, n), F32), o_spec, (m // tm, n // tn, ns), (tm, tn), 1.0, resid, o_spec, rider)


def mm_tn_sharded(name, a, b, ns, *, rider=None):
    kd, m = a.shape
    c = b.shape[1] // ns
    tm, _, tk = _tiles(m, c, kd, _item(a), _item(b), 2, tn=c)
    return _mm_call(
        name, "tn", a, b, pl.BlockSpec((tk, tm), lambda i, j, k: (k, i)), pl.BlockSpec((tk, c), lambda i, j, k: (k, j)),
        jax.ShapeDtypeStruct((ns, m, c), BF16), pl.BlockSpec((None, tm, c), lambda i, j, k: (j, i, 0)),
        (m // tm, ns, kd // tk), (tm, c), rider=rider)


def rms_fwd(name, x, g, out_dtype, rider=None):
    r, c = x.shape
    tm = _pick(r, (512, 256, 128, 64, 8))

    def body(x_ref, g_ref, y_ref, r_ref):
        xf = x_ref[...].astype(F32)
        rstd = lax.rsqrt(jnp.mean(xf * xf, axis=-1, keepdims=True) + EPS)
        y_ref[...] = ((xf * rstd) * g_ref[...]).astype(y_ref.dtype)
        r_ref[...] = rstd

    (y, rstd), rid = _call(
        name, body, (r // tm,), [pl.BlockSpec((tm, c), lambda i: (i, 0)), pl.BlockSpec((1, c), lambda i: (0, 0))], (x, g.reshape(1, c)),
        [pl.BlockSpec((tm, c), lambda i: (i, 0)), pl.BlockSpec((tm, 1), lambda i: (i, 0))],
        [jax.ShapeDtypeStruct((r, c), out_dtype), jax.ShapeDtypeStruct((r, 1), F32)], (), ("parallel",), rider)
    return (y, rstd) if rider is None else (y, rstd, rid)


def rms_bwd(name, x, g, rstd, dy, dres=None, rider=None, also_bf16=False):
    r, c = x.shape
    tm = _pick(r, (512, 256, 128, 64, 8))
    has_res = dres is not None

    def body(*refs):
        x_ref, g_ref, r_ref, dy_ref = refs[:4]
        dres_ref = refs[4] if has_res else None
        dx_ref, dg_ref = refs[4 + has_res:6 + has_res]
        xhat = x_ref[...].astype(F32) * r_ref[...]
        dyf = dy_ref[...].astype(F32)
        gdy = dyf * g_ref[...]
        dx = r_ref[...] * (gdy - xhat * jnp.mean(gdy * xhat, axis=-1, keepdims=True))
        if has_res:
            dx = dx + dres_ref[...]
        dx_ref[...] = dx
        if also_bf16:
            refs[-1][...] = dx.astype(BF16)

        @pl.when(pl.program_id(0) == 0)
        def _():
            dg_ref[...] = jnp.zeros_like(dg_ref)

        dg_ref[...] += jnp.sum(dyf * xhat, axis=0, keepdims=True)

    row = pl.BlockSpec((tm, c), lambda i: (i, 0))
    in_specs = [row, pl.BlockSpec((1, c), lambda i: (0, 0)), pl.BlockSpec((tm, 1), lambda i: (i, 0)), row] + ([row] if has_res else [])
    args = (x, g.reshape(1, c), rstd, dy) + ((dres,) if has_res else ())
    outs, rid = _call(name, body, (r // tm,), in_specs, args, [row, pl.BlockSpec((1, c), lambda i: (0, 0))] + [row] * also_bf16,
                      [jax.ShapeDtypeStruct((r, c), F32), jax.ShapeDtypeStruct((1, c), F32)] + [jax.ShapeDtypeStruct((r, c), BF16)] * also_bf16,
                      (), ("arbitrary",), rider)
    return (outs[0], outs[1].reshape(c), *outs[2:], *([] if rider is None else [rid]))


_LANES = 128


def _head_mean(v):
    if v.shape[1] == HEAD_DIM:
        return jnp.mean(v, axis=-1, keepdims=True)
    low = lax.broadcasted_iota(jnp.int32, v.shape, 1) < HEAD_DIM
    lo = jnp.sum(jnp.where(low, v, 0.0), axis=-1, keepdims=True)
    hi = jnp.sum(jnp.where(low, 0.0, v), axis=-1, keepdims=True)
    return jnp.where(low, lo, hi) * (1.0 / HEAD_DIM)


def _head_groups(c):
    width = _LANES if c % _LANES == 0 else HEAD_DIM
    assert c % width == 0, c
    return width, [slice(k * width, (k + 1) * width) for k in range(c // width)]


def _head_gain(g, width):
    return jnp.tile(g.reshape(1, HEAD_DIM), (1, width // HEAD_DIM))


def _rotate_half(y):
    half = HEAD_DIM // 2
    first = lax.broadcasted_iota(jnp.int32, y.shape, 1) % HEAD_DIM < half
    return jnp.where(first, -pltpu.roll(y, y.shape[1] - half, axis=1), pltpu.roll(y, half, axis=1))


def _rope_tables(rope, width):
    return [jnp.tile(t, (1, 2 * width // HEAD_DIM)) for t in rope]


def head_rms_fwd(name, x, g, rope=None):
    s, c = x.shape
    tm = _pick(s, (512, 256, 128, 8))
    width, groups = _head_groups(c)

    def body(x_ref, g_ref, *refs):
        y_ref = refs[-1]
        for sl in groups:
            xs = x_ref[:, sl]
            y = (xs * lax.rsqrt(_head_mean(xs * xs) + EPS)) * g_ref[...]
            if rope:
                y = y * refs[0][...] + _rotate_half(y) * refs[1][...]
            y_ref[:, sl] = y

    row = pl.BlockSpec((tm, c), lambda i: (i, 0))
    tab = pl.BlockSpec((tm, width), lambda i: (i, 0))
    tables = _rope_tables(rope, width) if rope else []
    (y,), _ = _call(name, body, (s // tm,), [row, pl.BlockSpec((1, width), lambda i: (0, 0))] + [tab] * len(tables),
                    (x, _head_gain(g, width), *tables), [row], [jax.ShapeDtypeStruct((s, c), F32)], (), ("parallel",))
    return y


def head_rms_bwd(name, x, g, dy, rope=None):
    s, c = x.shape
    tm = _pick(s, (512, 256, 128, 8))
    width, groups = _head_groups(c)

    def body(x_ref, g_ref, dy_ref, *refs):
        dx_ref, dg_ref = refs[-2:]

        @pl.when(pl.program_id(0) == 0)
        def _():
            dg_ref[...] = jnp.zeros_like(dg_ref)

        for sl in groups:
            xs, dys = x_ref[:, sl], dy_ref[:, sl]
            if rope:
                dys = dys * refs[0][...] - _rotate_half(dys * refs[1][...])
            rstd = lax.rsqrt(_head_mean(xs * xs) + EPS)
            xhat = xs * rstd
            gdy = dys * g_ref[...]
            dx_ref[:, sl] = rstd * (gdy - xhat * _head_mean(gdy * xhat))
            dg_ref[...] += jnp.sum(dys * xhat, axis=0, keepdims=True)

    row = pl.BlockSpec((tm, c), lambda i: (i, 0))
    vec = pl.BlockSpec((1, width), lambda i: (0, 0))
    tab = pl.BlockSpec((tm, width), lambda i: (i, 0))
    tables = _rope_tables(rope, width) if rope else []
    (dx, dg), _ = _call(name, body, (s // tm,), [row, vec, row] + [tab] * len(tables), (x, _head_gain(g, width), dy, *tables), [row, vec],
                        [jax.ShapeDtypeStruct((s, c), F32), jax.ShapeDtypeStruct((1, width), F32)], (), ("arbitrary",))
    return dx, jnp.sum(dg.reshape(width // HEAD_DIM, HEAD_DIM), axis=0)


@functools.partial(jax.custom_vjp, nondiff_argnums=(0,))
def head_rms(name, x, g):
    return head_rms_fwd(name + "_fwd", x, g)


def _head_rms_fwd(name, x, g):
    return head_rms_fwd(name + "_fwd", x, g), (x, g)


def _head_rms_bwd(name, res, dy):
    return head_rms_bwd(name + "_bwd", *res, dy)


head_rms.defvjp(_head_rms_fwd, _head_rms_bwd)


@functools.partial(jax.custom_vjp, nondiff_argnums=(0,))
def head_rms_rope(name, x, g, cos, sin):
    return head_rms_fwd(name + "_fwd", x, g, (cos, sin))


def _head_rms_rope_fwd(name, x, g, cos, sin):
    return head_rms_fwd(name + "_fwd", x, g, (cos, sin)), (x, g, cos, sin)


def _head_rms_rope_bwd(name, res, dy):
    x, g, cos, sin = res
    return (*head_rms_bwd(name + "_bwd", x, g, dy, (cos, sin)), jnp.zeros_like(cos), jnp.zeros_like(sin))


head_rms_rope.defvjp(_head_rms_rope_fwd, _head_rms_rope_bwd)


def _cumsum_call(name, a, reverse):
    h, s = a.shape
    tb = _LANES
    assert s % tb == 0

    def body(a_ref, o_ref):
        t_in = lax.broadcasted_iota(jnp.int32, (tb, tb), 0)
        t_out = lax.broadcasted_iota(jnp.int32, (tb, tb), 1)
        tri = jnp.where((t_in >= t_out) if reverse else (t_in <= t_out), 1.0, 0.0).astype(BF16)
        carry = jnp.zeros((h, 1), F32)
        blocks = range(s // tb)
        for b in (reversed(blocks) if reverse else blocks):
            cols = slice(b * tb, (b + 1) * tb)
            block = a_ref[:, cols]
            rest, local = block, jnp.zeros((h, tb), F32)
            for _ in range(3):
                piece = rest.astype(BF16)
                local = local + jnp.dot(piece, tri, preferred_element_type=F32)
                rest = rest - piece.astype(F32)
            o_ref[:, cols] = local + carry
            carry = carry + jnp.sum(block, axis=1, keepdims=True)

    whole = pl.BlockSpec((h, s), lambda j: (0, 0))
    (out,), _ = _call(name, body, (1,), [whole], (a,), [whole], [jax.ShapeDtypeStruct((h, s), F32)], (), ("arbitrary",))
    return out


@jax.custom_vjp
def time_cumsum(a):
    return _cumsum_call("gate_cumsum", a, False)


def _time_cumsum_fwd(a):
    return _cumsum_call("gate_cumsum", a, False), None


def _time_cumsum_bwd(_, dc):
    return (_cumsum_call("gate_cumsum_bwd", dc, True),)


time_cumsum.defvjp(_time_cumsum_fwd, _time_cumsum_bwd)


FFN_TM = 512


def _sigmoid(x):
    return 1.0 / (1.0 + jnp.exp(-x))


def ffn_gu(name, xn, wg, wu, rider=None):
    s, d = xn.shape
    ns, _, c = wg.shape
    tm = _pick(s, (FFN_TM, 128))

    def body(x_ref, wg_ref, wu_ref, h_ref, a_ref, b_ref):
        xb = x_ref[...]
        gv = jnp.dot(xb, wg_ref[...], preferred_element_type=F32)
        uv = jnp.dot(xb, wu_ref[...], preferred_element_type=F32)
        sig = _sigmoid(gv)
        silu = gv * sig
        h_ref[...] = (silu * uv).astype(BF16)
        a_ref[...] = (uv * (sig * (1.0 + gv * (1.0 - sig)))).astype(BF16)
        b_ref[...] = silu.astype(BF16)

    w_spec = pl.BlockSpec((None, d, c), lambda j, i: (j, 0, 0))
    o_spec = pl.BlockSpec((tm, c), lambda j, i: (i, j))
    return _call(
        name, body, (ns, s // tm), [pl.BlockSpec((tm, d), lambda j, i: (i, 0)), w_spec, w_spec], (xn, wg, wu),
        [o_spec, o_spec, o_spec], [jax.ShapeDtypeStruct((s, ns * c), BF16)] * 3, [], ("parallel", "parallel"), rider)


def ffn_dh(name, dy, wd, dh_dg, dh_du, ns, scale, rider=None):
    s, d = dy.shape
    f = wd.shape[0]
    c = f // ns
    tm = _pick(s, (FFN_TM, 128))

    def body(dy_ref, wd_ref, a_ref, b_ref, dg_ref, du_ref):
        dh = lax.dot_general(dy_ref[...].astype(BF16), wd_ref[...], _DIMS["nt"], preferred_element_type=F32) * scale
        dg_ref[...] = (dh * a_ref[...].astype(F32)).astype(BF16)
        du_ref[...] = (dh * b_ref[...].astype(F32)).astype(BF16)

    o_spec = pl.BlockSpec((tm, c), lambda j, i: (i, j))
    return _call(
        name, body, (ns, s // tm),
        [pl.BlockSpec((tm, d), lambda j, i: (i, 0)), pl.BlockSpec((c, d), lambda j, i: (j, 0)), o_spec, o_spec], (dy, wd, dh_dg, dh_du),
        [o_spec, o_spec], [jax.ShapeDtypeStruct((s, f), BF16), jax.ShapeDtypeStruct((s, f), BF16)],
        [], ("parallel", "parallel"), rider)


FOX_TQ = 1024


def fox_tile(s_len):
    return min(FOX_TQ, s_len)


def _heads_per_block(h):
    return 2 if h % 2 == 0 else 1


def _fox_queries(q):
    return (q * (HEAD_DIM ** -0.5)).astype(BF16)


def _fox_scores(qs, kc, cq, ck, diagonal):
    s = lax.dot_general(qs, kc.astype(BF16), _DIMS["nt"], preferred_element_type=F32) + cq - ck
    if not diagonal:
        return s
    return jnp.where(lax.broadcasted_iota(jnp.int32, s.shape, 0) >= lax.broadcasted_iota(jnp.int32, s.shape, 1), s, MASK_VALUE)


def _fox_specs(h, s_len, tq):
    hb = _heads_per_block(h)
    qb = pl.BlockSpec((tq, hb * HEAD_DIM), lambda pp, i: (i, pp))
    kb = pl.BlockSpec((s_len, hb * HEAD_DIM), lambda pp, i: (0, pp))
    colb = pl.BlockSpec((hb, tq, 1), lambda pp, i: (pp, i, 0))
    rowb = pl.BlockSpec((hb, s_len // tq, 1, tq), lambda pp, i: (pp, 0, 0, 0))
    return hb, qb, kb, colb, rowb


def fox_fwd(q, k, v, cq, ck, rider=None):
    s_len, hd = q.shape
    h, d = hd // HEAD_DIM, HEAD_DIM
    tq = fox_tile(s_len)
    hb, qb, kb, colb, rowb = _fox_specs(h, s_len, tq)

    def body(q_ref, k_ref, v_ref, cq_ref, ck_ref, o_ref, lse_ref):
        i = pl.program_id(1)
        for hh in range(hb):
            lanes = slice(hh * d, (hh + 1) * d)
            qs, cqv = _fox_queries(q_ref[:, lanes]), cq_ref[hh]

            def chunk(c, carry, diagonal=False):
                m, l, acc = carry
                rows = pl.ds(pl.multiple_of(c * tq, tq), tq)
                s = _fox_scores(qs, k_ref[rows, lanes], cqv, ck_ref[hh, c], diagonal)
                m_new = jnp.maximum(m, jnp.max(s, axis=-1, keepdims=True))
                alpha = jnp.exp(m - m_new)
                p = jnp.exp(s - m_new)
                acc = alpha * acc + jnp.dot(p.astype(BF16), v_ref[rows, lanes].astype(BF16), preferred_element_type=F32)
                return m_new, alpha * l + jnp.sum(p, axis=-1, keepdims=True), acc

            init = (jnp.full((tq, 1), MASK_VALUE, F32), jnp.zeros((tq, 1), F32), jnp.zeros((tq, d), F32))
            m, l, acc = chunk(i, lax.fori_loop(0, i, chunk, init), diagonal=True)
            o_ref[:, lanes] = acc / l
            lse_ref[hh] = m + jnp.log(l)

    return _call(
        "fox_fwd", body, (h // hb, s_len // tq), [qb, kb, kb, colb, rowb], (q, k, v, cq, ck), [qb, colb],
        [jax.ShapeDtypeStruct((s_len, hd), F32), jax.ShapeDtypeStruct((h, s_len, 1), F32)], (), ("parallel", "parallel"), rider)


def fox_bwd(q, k, v, cq, ck, o, lse, do, rider=None):
    s_len, hd = q.shape
    h, d = hd // HEAD_DIM, HEAD_DIM
    tq = fox_tile(s_len)
    scale = HEAD_DIM ** -0.5
    hb, qb, kb, colb, rowb = _fox_specs(h, s_len, tq)

    def body(q_ref, k_ref, v_ref, cq_ref, ck_ref, o_ref, lse_ref, do_ref, dq_ref, dk_ref, dv_ref, dcq_ref, dck_ref):
        i = pl.program_id(1)

        @pl.when(i == 0)
        def _():
            dk_ref[...] = jnp.zeros_like(dk_ref)
            dv_ref[...] = jnp.zeros_like(dv_ref)
            dck_ref[...] = jnp.zeros_like(dck_ref)

        heads = []
        for hh in range(hb):
            lanes = slice(hh * d, (hh + 1) * d)
            dof = do_ref[:, lanes]
            heads.append((lanes, _fox_queries(q_ref[:, lanes]), cq_ref[hh], lse_ref[hh], dof.astype(BF16),
                          jnp.sum(dof * o_ref[:, lanes], axis=-1, keepdims=True)))

        def chunk(c, carry, diagonal=False):
            rows = pl.ds(pl.multiple_of(c * tq, tq), tq)
            out, dks, dvs = [], [], []
            for hh, (lanes, qs, cqv, lse_h, dob, delta) in enumerate(heads):
                dq, dcq = carry[hh]
                kc = k_ref[rows, lanes]
                p = jnp.exp(_fox_scores(qs, kc, cqv, ck_ref[hh, c], diagonal) - lse_h)
                dp = lax.dot_general(dob, v_ref[rows, lanes].astype(BF16), _DIMS["nt"], preferred_element_type=F32)
                ds = p * (dp - delta)
                dsb = ds.astype(BF16)
                dvs.append(lax.dot_general(p.astype(BF16), dob, _DIMS["tn"], preferred_element_type=F32))
                dks.append(lax.dot_general(dsb, qs, _DIMS["tn"], preferred_element_type=F32))
                dck_ref[hh, c] -= jnp.sum(ds, axis=0, keepdims=True)
                out.append((dq + jnp.dot(dsb, kc.astype(BF16), preferred_element_type=F32), dcq + jnp.sum(ds, axis=-1, keepdims=True)))
            dk_ref[rows, :] += jnp.concatenate(dks, axis=1)
            dv_ref[rows, :] += jnp.concatenate(dvs, axis=1)
            return tuple(out)

        init = tuple((jnp.zeros((tq, d), F32), jnp.zeros((tq, 1), F32)) for _ in range(hb))
        done = chunk(i, lax.fori_loop(0, i, chunk, init), diagonal=True)
        dq_ref[...] = jnp.concatenate([dq for dq, _ in done], axis=1) * scale
        for hh, (_, dcq) in enumerate(done):
            dcq_ref[hh] = dcq

    return _call(
        "fox_bwd", body, (h // hb, s_len // tq), [qb, kb, kb, colb, rowb, qb, colb, qb], (q, k, v, cq, ck, o, lse, do),
        [qb, kb, kb, colb, rowb],
        [jax.ShapeDtypeStruct((s_len, hd), F32)] * 3
        + [jax.ShapeDtypeStruct((h, s_len, 1), F32), jax.ShapeDtypeStruct((h, s_len // tq, 1, tq), F32)],
        (), ("parallel", "arbitrary"), rider)


def _stack_heads(ref, first, g):
    return jnp.concatenate([ref[:, (first + j) * HEAD_DIM:(first + j + 1) * HEAD_DIM] for j in range(g)], axis=0)


def _window(prev_ref, cur_ref, hh):
    lanes = slice(hh * HEAD_DIM, (hh + 1) * HEAD_DIM)
    return jnp.concatenate([prev_ref[:, lanes], cur_ref[:, lanes]], axis=0).astype(BF16)


def _swa_band(g, w):
    t = lax.broadcasted_iota(jnp.int32, (g * w, 2 * w), 0) % w
    col = lax.broadcasted_iota(jnp.int32, (g * w, 2 * w), 1)
    rel = t + w - col
    band = (rel >= 0) & (rel < w)
    return jnp.where(jnp.stack([band & (col >= w), band]), 0.0, MASK_VALUE).astype(F32)


def _swa_probs(qs, kw, sink, band):
    s = lax.dot_general(qs, kw, _DIMS["nt"], preferred_element_type=F32) + band
    m = jnp.maximum(jnp.max(s, axis=-1, keepdims=True), sink)
    p = jnp.exp(s - m)
    ps = jnp.exp(sink - m)
    linv = 1.0 / (jnp.sum(p, axis=-1, keepdims=True) + ps)
    return p * linv, ps * linv


def _swa_specs(hk, g, s_len):
    w = WINDOW
    assert s_len % w == 0
    hb = _heads_per_block(hk)
    qb = pl.BlockSpec((w, hb * g * HEAD_DIM), lambda pp, n: (n, pp))
    prev = pl.BlockSpec((w, hb * HEAD_DIM), lambda pp, n: (jnp.maximum(n - 1, 0), pp))
    cur = pl.BlockSpec((w, hb * HEAD_DIM), lambda pp, n: (n, pp))
    sb = pl.BlockSpec((hb, g * w, 1), lambda pp, n: (pp, 0, 0))
    band = pl.BlockSpec((None, g * w, 2 * w), lambda pp, n: (jnp.minimum(n, 1), 0, 0))
    return hb, qb, prev, cur, sb, band


def swa_fwd(q, k, v, sink, rider=None):
    s_len = q.shape[0]
    hk = k.shape[1] // HEAD_DIM
    g = q.shape[1] // k.shape[1]
    w, d = WINDOW, HEAD_DIM
    hb, qb, prev, cur, sb, bandb = _swa_specs(hk, g, s_len)

    def body(q_ref, kp_ref, kc_ref, vp_ref, vc_ref, sink_ref, band_ref, o_ref):
        for hh in range(hb):
            qs = (_stack_heads(q_ref, hh * g, g) * (HEAD_DIM ** -0.5)).astype(BF16)
            p, _ = _swa_probs(qs, _window(kp_ref, kc_ref, hh), sink_ref[hh], band_ref[...])
            o = jnp.dot(p.astype(BF16), _window(vp_ref, vc_ref, hh), preferred_element_type=F32)
            for j in range(g):
                o_ref[:, (hh * g + j) * d:(hh * g + j + 1) * d] = o[j * w:(j + 1) * w]

    (o,), rid = _call("swa_fwd", body, (hk // hb, s_len // w), [qb, prev, cur, prev, cur, sb, bandb],
                      (q, k, k, v, v, sink, _swa_band(g, w)), [qb], [jax.ShapeDtypeStruct(q.shape, F32)], (),
                      ("parallel", "parallel"), rider)
    return o, rid


def swa_bwd(q, k, v, sink, o, do, rider=None):
    s_len = q.shape[0]
    hk = k.shape[1] // HEAD_DIM
    g = q.shape[1] // k.shape[1]
    w, d = WINDOW, HEAD_DIM
    scale = HEAD_DIM ** -0.5
    hb, qb, prev, cur, sb, bandb = _swa_specs(hk, g, s_len)

    def body(q_ref, kp_ref, kc_ref, vp_ref, vc_ref, sink_ref, band_ref, o_ref, do_ref, dq_ref, dkp_ref, dkc_ref, dvp_ref, dvc_ref,
             dsink_ref):
        @pl.when(pl.program_id(1) == 0)
        def _():
            dsink_ref[...] = jnp.zeros_like(dsink_ref)

        for hh in range(hb):
            lanes = slice(hh * d, (hh + 1) * d)
            qs = (_stack_heads(q_ref, hh * g, g) * scale).astype(BF16)
            kw, vw = _window(kp_ref, kc_ref, hh), _window(vp_ref, vc_ref, hh)
            p, ps = _swa_probs(qs, kw, sink_ref[hh], band_ref[...])
            dof = _stack_heads(do_ref, hh * g, g)
            dob = dof.astype(BF16)
            delta = jnp.sum(dof * _stack_heads(o_ref, hh * g, g), axis=-1, keepdims=True)
            dp = lax.dot_general(dob, vw, _DIMS["nt"], preferred_element_type=F32)
            ds = p * (dp - delta)
            dsb = ds.astype(BF16)
            dsink_ref[hh] -= ps * delta
            dq = jnp.dot(dsb, kw, preferred_element_type=F32) * scale
            for j in range(g):
                dq_ref[:, (hh * g + j) * d:(hh * g + j + 1) * d] = dq[j * w:(j + 1) * w]
            dkw = lax.dot_general(dsb, qs, _DIMS["tn"], preferred_element_type=F32)
            dvw = lax.dot_general(p.astype(BF16), dob, _DIMS["tn"], preferred_element_type=F32)
            dkp_ref[:, lanes] = dkw[:w]
            dkc_ref[:, lanes] = dkw[w:]
            dvp_ref[:, lanes] = dvw[:w]
            dvc_ref[:, lanes] = dvw[w:]

    kv_shape = jax.ShapeDtypeStruct(k.shape, F32)
    (dq, dkp, dkc, dvp, dvc, dsink), rid = _call(
        "swa_bwd", body, (hk // hb, s_len // w), [qb, prev, cur, prev, cur, sb, bandb, qb, qb],
        (q, k, k, v, v, sink, _swa_band(g, w), o, do),
        [qb, cur, cur, cur, cur, sb],
        [jax.ShapeDtypeStruct(q.shape, F32), kv_shape, kv_shape, kv_shape, kv_shape, jax.ShapeDtypeStruct((hk, g * w, 1), F32)],
        (), ("parallel", "arbitrary"), rider)

    def shift_up(a):
        return jnp.concatenate([a[w:], jnp.zeros_like(a[:w])], axis=0)

    return (dq, dkc + shift_up(dkp), dvc + shift_up(dvp), dsink), rid


def loss_call(y, target):
    s, d = y.shape
    tm = _pick(s, (512, 256, 128))

    def body(y_ref, t_ref, l_ref, dy_ref, dyb_ref):
        e = y_ref[...] - t_ref[...]
        dy = e * (1.0 / d)
        dy_ref[...] = dy
        dyb_ref[...] = dy.astype(BF16)

        @pl.when(pl.program_id(0) == 0)
        def _():
            l_ref[...] = jnp.zeros_like(l_ref)

        l_ref[...] += jnp.sum(jnp.sum(e * e, axis=0, keepdims=True), axis=1, keepdims=True) * (0.5 / d)

    row = pl.BlockSpec((tm, d), lambda i: (i, 0))
    (l, dy, dyb), _ = _call("loss_head", body, (s // tm,), [row, row], (y, target), [pl.BlockSpec((1, 1), lambda i: (0, 0)), row, row],
                            [jax.ShapeDtypeStruct((1, 1), F32), jax.ShapeDtypeStruct((s, d), F32), jax.ShapeDtypeStruct((s, d), BF16)],
                            (), ("arbitrary",))
    return l[0, 0], dy, dyb


def _row_tile(rows, cols, itemsize, block_bytes=1 << 20):
    target = max(16, block_bytes // (cols * itemsize))
    fits = [t for t in range(16, rows + 1, 16) if rows % t == 0 and t <= target]
    return fits[-1] if fits else rows


CAST_STEPS = 8


def cast_place(name, ws, p_idx, rider=None):
    n = len(ws)
    assert all(w.shape[0] % (16 * CAST_STEPS) == 0 for w in ws), [w.shape for w in ws]

    def body(p_ref, *refs):
        for w_ref, o_ref in zip(refs[:n], refs[n:]):
            o_ref[...] = w_ref[...].astype(BF16)

    return _call(
        name, body, (CAST_STEPS,), [pl.BlockSpec((w.shape[0] // CAST_STEPS, w.shape[1]), lambda i, pr: (i, 0)) for w in ws], tuple(ws),
        [pl.BlockSpec((None, w.shape[0] // CAST_STEPS, w.shape[1]), lambda i, pr: (pr[0], i, 0)) for w in ws],
        [jax.ShapeDtypeStruct((N_CHIPS,) + w.shape, BF16) for w in ws], (), ("parallel",), rider, prefetch=(p_idx,))


def chip_sum(name, grad, theirs, c_idx):
    ns, r, cols = grad.shape
    rh = r // 2
    tr = _row_tile(rh, cols, 2, 4 << 20)
    nb = rh // tr

    def body(c_ref, a_ref, b_ref, o_ref):
        o_ref[...] = (a_ref[...].astype(F32) + b_ref[...].astype(F32)).astype(o_ref.dtype)

    return pl.pallas_call(
        body, name=name,
        grid_spec=pltpu.PrefetchScalarGridSpec(
            num_scalar_prefetch=1, grid=(ns, nb),
            in_specs=[pl.BlockSpec((None, tr, cols), lambda q, i, cr: (q, cr[0] * nb + i, 0)),
                      pl.BlockSpec((None, tr, cols), lambda q, i, cr: (q, i, 0))],
            out_specs=pl.BlockSpec((None, tr, cols), lambda q, i, cr: (q, i, 0))),
        out_shape=jax.ShapeDtypeStruct((ns, rh, cols), BF16),
        compiler_params=pltpu.CompilerParams(dimension_semantics=("parallel", "parallel"), vmem_limit_bytes=VMEM_LIMIT),
    )(c_idx, grad, theirs)


def owner_sum(name, sums, got, pc_idx):
    ns, rh, cols = sums.shape
    tr = _row_tile(rh, cols, 4, 4 << 20)
    nb = rh // tr

    def body(pc_ref, a_ref, b_ref, o_ref):
        o_ref[...] = ((a_ref[...].astype(F32) + b_ref[0].astype(F32)) + b_ref[1].astype(F32)) + b_ref[2].astype(F32)

    return pl.pallas_call(
        body, name=name,
        grid_spec=pltpu.PrefetchScalarGridSpec(
            num_scalar_prefetch=1, grid=(nb,),
            in_specs=[pl.BlockSpec((None, tr, cols), lambda i, pc: (pc[0], i, 0)),
                      pl.BlockSpec((3, tr, cols), lambda i, pc: (0, i, 0))],
            out_specs=pl.BlockSpec((tr, cols), lambda i, pc: (pc[1] * nb + i, 0))),
        out_shape=jax.ShapeDtypeStruct((2 * rh, cols), F32),
        compiler_params=pltpu.CompilerParams(dimension_semantics=("parallel",), vmem_limit_bytes=VMEM_LIMIT),
    )(pc_idx, sums, got)


def adamw(name, w, g, m, v):
    r, cols = w.shape
    tr = _row_tile(r, cols, 4, 3 << 20)
    c1 = 1.0 / (1.0 - ADAM_B1 ** ADAM_STEP)
    c2 = 1.0 / (1.0 - ADAM_B2 ** ADAM_STEP)

    def body(w_ref, g_ref, m_ref, v_ref, go_ref, d_ref, nm_ref, nv_ref):
        gv = g_ref[...]
        nm = ADAM_B1 * m_ref[...] + (1.0 - ADAM_B1) * gv
        nv = ADAM_B2 * v_ref[...] + (1.0 - ADAM_B2) * (gv * gv)
        go_ref[...] = gv
        d_ref[...] = -ADAM_LR * ((nm * c1) / (jnp.sqrt(nv * c2) + ADAM_EPS) + ADAM_WD * w_ref[...])
        nm_ref[...] = nm
        nv_ref[...] = nv

    blk = pl.BlockSpec((tr, cols), lambda i: (i, 0))
    return _call(name, body, (r // tr,), [blk] * 4, (w, g, m, v), [blk] * 4, [jax.ShapeDtypeStruct((r, cols), F32)] * 4, (), ("parallel",))


def _win_layout(d_model):
    hf = hq = d_model // (2 * HEAD_DIM)
    hk = hq // 4
    sizes = [hf * HEAD_DIM, hf * HEAD_DIM, hf * HEAD_DIM, hf, hq * HEAD_DIM, hk * HEAD_DIM, hk * HEAD_DIM]
    return hf, hq, hk, sizes


class WinPlan:
    def __init__(self, d_model, ns=N_CHIPS):
        self.hf, self.hq, self.hk, self.sizes = _win_layout(d_model)
        self.ns, self.cs = ns, sum(self.sizes) // ns
        self.jump_at = sum(self.sizes[:4])
        self.jump_by = -self.jump_at % _LANES
        self.base = [self.pos(s * self.cs) // _LANES * _LANES for s in range(ns)]
        ends = [self.pos((s + 1) * self.cs - 1) + 1 - self.base[s] for s in range(ns)]
        self.width = -(-max(ends) // _LANES) * _LANES
        self.total = -(-max(b + self.width for b in self.base) // 1024) * 1024
        starts = [0]
        for sz in self.sizes:
            starts.append(starts[-1] + sz)
        self.segments = [(self.pos(a), sz) for a, sz in zip(starts, self.sizes)]

    def pos(self, g):
        return g if g < self.jump_at else g + self.jump_by

    def pieces(self, s):
        g0, g1 = s * self.cs, (s + 1) * self.cs
        cuts = [g0] + ([self.jump_at] if g0 < self.jump_at < g1 else []) + [g1]
        return [(a - g0, b - a, self.pos(a) - self.base[s]) for a, b in zip(cuts[:-1], cuts[1:])]

    def place(self, w, s):
        parts, at = [], 0
        for t0, n, j0 in self.pieces(s):
            parts += [jnp.zeros((w.shape[0], j0 - at), w.dtype), w[:, t0:t0 + n]]
            at = j0 + n
        return jnp.concatenate(parts + [jnp.zeros((w.shape[0], self.width - at), w.dtype)], axis=1)

    def unplace(self, slab, s):
        return jnp.concatenate([slab[:, j0:j0 + n] for _, n, j0 in self.pieces(s)], axis=1)

    def assemble(self, slabs):
        return sum(jnp.pad(slabs[s], ((0, 0), (b, self.total - b - self.width))) for s, b in enumerate(self.base))

    def split(self, full):
        return jnp.stack([full[:, b:b + self.width] for b in self.base])


def _attn_inputs(proj, sm, positions):
    s_len = proj.shape[0]
    plan = WinPlan(sm["norm_mix_g"].shape[0])
    hf, hq, hk = plan.hf, plan.hq, plan.hk
    grp = hq // hk
    q_f, k_f, v_f, f_logit, q_s, k_s, v_s = [proj[:, a:a + n] for a, n in plan.segments]

    q_f = head_rms("fox_qnorm", q_f, sm["fox_q_norm_g"])
    k_f = head_rms("fox_knorm", k_f, sm["fox_k_norm_g"])
    log_f = jax.nn.log_sigmoid(f_logit + sm["b_forget"])
    c = time_cumsum(log_f.T)

    inv_freq = ROPE_THETA ** (-jnp.arange(0, HEAD_DIM, 2, dtype=F32) / HEAD_DIM)
    ang = positions.astype(F32)[:, None] * inv_freq
    cos, sin = jnp.cos(ang), jnp.sin(ang)
    q_s = head_rms_rope("swa_qnorm", q_s, sm["swa_q_norm_g"], cos, sin)
    k_s = head_rms_rope("swa_knorm", k_s, sm["swa_k_norm_g"], cos, sin)
    sink = jnp.broadcast_to(sm["swa_sinks"].reshape(hk, grp, 1, 1), (hk, grp, WINDOW, 1)).reshape(hk, grp * WINDOW, 1)
    tq = fox_tile(s_len)
    return (q_f, k_f, v_f, c[:, :, None], c.reshape(hf, s_len // tq, 1, tq)), (q_s, k_s, v_s, sink)


_BIG = ("ffn1_w_gate", "ffn1_w_up", "ffn1_w_down", "w_in", "w_out", "ffn2_w_gate", "ffn2_w_up", "ffn2_w_down")
_SMALL = ("norm_ffn1_g", "norm_mix_g", "b_forget", "fox_q_norm_g", "fox_k_norm_g", "swa_q_norm_g", "swa_k_norm_g", "swa_sinks",
          "out_norm_fox_g", "out_norm_swa_g", "norm_ffn2_g")
_ATTN_SMALL = ("norm_mix_g", "b_forget", "fox_q_norm_g", "fox_k_norm_g", "swa_q_norm_g", "swa_k_norm_g", "swa_sinks")
_ALL = ("norm_ffn1_g", "ffn1_w_gate", "ffn1_w_up", "ffn1_w_down", "norm_mix_g", "w_in", "b_forget", "fox_q_norm_g", "fox_k_norm_g",
        "swa_q_norm_g", "swa_k_norm_g", "swa_sinks", "out_norm_fox_g", "out_norm_swa_g", "w_out", "norm_ffn2_g", "ffn2_w_gate",
        "ffn2_w_up", "ffn2_w_down")


def _pack_small(d):
    parts = []
    for k in _SMALL:
        v = d[k].reshape(-1)
        rows = -(-v.shape[0] // _LANES)
        parts.append(jnp.pad(v, (0, rows * _LANES - v.shape[0])).reshape(rows, _LANES))
    a = jnp.concatenate(parts, axis=0)
    return jnp.pad(a, ((0, -a.shape[0] % 8), (0, 0)))


def _unpack_small(a, like):
    out, r0 = {}, 0
    for k in _SMALL:
        nvals = like[k].shape[1]
        rows = -(-nvals // _LANES)
        out[k] = a[r0:r0 + rows].reshape(-1)[:nvals].reshape(1, nvals)
        r0 += rows
    return out


def _stacked(w):
    return w.reshape(-1, w.shape[-1])


def _local_step(shards, sm, x, positions, target, p_idx, c_idx, pc_idx):
    ns = N_CHIPS
    full = {}

    def fetch(*jobs):
        names = list(dict.fromkeys(n for n, _, _ in jobs))
        return names, gather([bufs[n] for n in names], [(names.index(n), kind, part) for n, kind, part in jobs])

    def take(names, rid):
        for n, b in zip(names, rid[0]):
            bufs[n] = b

    n1 = ["ffn1_w_gate", "ffn1_w_up", "ffn1_w_down"]
    n2 = ["ffn2_w_gate", "ffn2_w_up", "ffn2_w_down"]
    later = ["w_in", "w_out"] + n2
    placed, _ = cast_place("cast_place_ffn1", [shards[n] for n in n1], p_idx)
    bufs = dict(zip(n1, placed))
    gate1, up1, down1 = n1
    gate2, up2, down2 = n2
    names, rider = fetch((gate1, "ici", WHOLE), (up1, "ici", WHOLE))
    placed, rid = cast_place("cast_place_later", [shards[n] for n in later], p_idx, rider=rider)
    bufs.update(zip(later, placed))
    take(names, rid)
    names, rider = fetch((gate1, "d2d", WHOLE), (up1, "d2d", WHOLE), (down1, "ici", (0, 1, 4)))
    xn1, r1, rid = rms_fwd("ffn1_norm", x, sm["norm_ffn1_g"], BF16, rider=rider)
    take(names, rid)
    names, rider = fetch((down1, "ici", (1, 4, 4)), ("w_in", "ici", (0, 1, 4)))
    (hid1, hdg1, hdu1), rid = ffn_gu("ffn1_gu", xn1, bufs[gate1], bufs[up1], rider=rider)
    take(names, rid)
    names, rider = fetch((down1, "d2d", WHOLE))
    take(names, run_step("gather_d2d_ffn1_down", rider))
    wd1 = _stacked(bufs[down1])
    names, rider = fetch(("w_in", "ici", (1, 4, 4)))
    h1, rid = mm_nn("ffn1_down", hid1, wd1, scale=0.5, resid=x, rider=rider)
    take(names, rid)

    names, rider = fetch(("w_in", "d2d", WHOLE))
    u, r_mix, rid = rms_fwd("mix_norm", h1, sm["norm_mix_g"], BF16, rider=rider)
    take(names, rid)
    names, rider = fetch(("w_out", "ici", WHOLE))
    plan = WinPlan(x.shape[1])
    win = plan.assemble(bufs["w_in"])
    proj, rid = mm_nn("mix_inproj", u, win, rider=rider)
    take(names, rid)
    sm_attn = {k: sm[k] for k in _ATTN_SMALL}
    (fox_in, swa_in), attn_vjp = jax.vjp(lambda pr, s: _attn_inputs(pr, s, positions), proj, sm_attn)
    names, rider = fetch((gate2, "ici", WHOLE), (up2, "ici", (0, 1, 4)), ("w_out", "d2d", WHOLE))
    (o_f, lse), rid = fox_fwd(*fox_in, rider=rider)
    take(names, rid)
    names, rider = fetch((up2, "ici", (1, 4, 4)), (gate2, "d2d", WHOLE), (up2, "d2d", (0, 1, 4)))
    o_s, rid = swa_fwd(*swa_in, rider=rider)
    take(names, rid)
    o_fox, o_swa = o_f, o_s
    nf, r_fox = rms_fwd("out_norm_fox", o_fox, sm["out_norm_fox_g"], BF16)
    nsw, r_swa = rms_fwd("out_norm_swa", o_swa, sm["out_norm_swa_g"], BF16)
    o = jnp.concatenate([nf, nsw], axis=-1)
    wout = _stacked(bufs["w_out"])
    names, rider = fetch((down2, "ici", (0, 1, 4)), (up2, "d2d", (1, 4, 4)))
    h2, rid = mm_nn("out_proj", o, wout, resid=h1, rider=rider)
    take(names, rid)

    xn2, r2 = rms_fwd("ffn2_norm", h2, sm["norm_ffn2_g"], BF16)
    names, rider = fetch((down2, "ici", (1, 4, 4)))
    (hid2, hdg2, hdu2), rid = ffn_gu("ffn2_gu", xn2, bufs[gate2], bufs[up2], rider=rider)
    take(names, rid)
    names, rider = fetch((down2, "d2d", WHOLE))
    take(names, run_step("gather_d2d_ffn2_down", rider))
    wd2 = _stacked(bufs["ffn2_w_down"])
    y, _ = mm_nn("ffn2_down", hid2, wd2, scale=0.5, resid=h2)
    loss, dy, dy_b = loss_call(y, target)

    red = {}

    def grad(n, g):
        red[n] = {"grad": g.reshape(ns, -1, g.shape[-1])}

    def ride(*steps):
        def done(rid):
            a0 = n0 = 0
            for rd, cb in steps:
                cb(rid[0][a0:a0 + len(rd.aliased)], rid[1][n0:n0 + len(rd.news)])
                a0, n0 = a0 + len(rd.aliased), n0 + len(rd.news)

        return (combine(*[s[0] for s in steps]) if len(steps) > 1 else steps[0][0]), done

    def xchg(*names):
        def cb(al, news):
            for n, t in zip(names, news):
                red[n]["sum"] = chip_sum("chip_sum_" + n, red[n]["grad"], t, c_idx)

        return exchange_halves([red[n]["grad"] for n in names]), cb

    def scat(n, part=WHOLE):
        def cb(al, news):
            red[n]["got"] = (al or news)[0]

        return scatter_to_owner([red[n]["sum"]], [red[n]["got"]] if "got" in red[n] else None, part), cb

    def own(n):
        red[n]["half"] = owner_sum("owner_sum_" + n, red[n]["sum"], red[n]["got"], pc_idx)

    def join(*names):
        return join_halves([red[n]["half"] for n in names]), lambda al, news: full.update(zip(names, al))

    dwd2, _ = mm_tn("ffn2_dwd", hid2, dy_b, out_dtype=BF16, scale=0.5)
    grad(down2, dwd2)
    rider, done = ride(xchg(down2))
    (dg2, du2), rid = ffn_dh("ffn2_dh", dy_b, wd2, hdg2, hdu2, ns, 0.5, rider=rider)
    done(rid)
    rider, done = ride(scat(down2, (0, 1, 2)))
    dwg2, rid = mm_tn_sharded("ffn2_dwg", xn2, dg2, ns, rider=rider)
    done(rid)
    grad(gate2, dwg2)
    rider, done = ride(scat(down2, (1, 2, 2)), xchg(gate2))
    dwu2, rid = mm_tn_sharded("ffn2_dwu", xn2, du2, ns, rider=rider)
    done(rid)
    grad(up2, dwu2)
    rider, done = ride(scat(gate2, (0, 1, 2)), xchg(up2))
    dxn, rid = mm_nt_sharded("ffn2_dxn_g", dg2, bufs[gate2], rider=rider)
    done(rid)
    rider, done = ride(scat(gate2, (1, 2, 2)))
    dxn, rid = mm_nt_sharded("ffn2_dxn_u", du2, bufs[up2], resid=dxn, rider=rider)
    done(rid)
    dh2, dgain_ffn2, dh2_b = rms_bwd("ffn2_dnorm", h2, sm["norm_ffn2_g"], r2, dxn, dres=dy, also_bf16=True)
    own(down2)
    own(gate2)

    do, _ = mm_nt("out_do", dh2_b, wout)
    dwout, _ = mm_tn("out_dw", o, dh2_b, out_dtype=BF16)
    cf = o_fox.shape[1]
    d_fox, dgain_fox = rms_bwd("out_dnorm_fox", o_fox, sm["out_norm_fox_g"], r_fox, do[:, :cf])
    d_swa, dgain_swa = rms_bwd("out_dnorm_swa", o_swa, sm["out_norm_swa_g"], r_swa, do[:, cf:])
    grad("w_out", dwout)
    rider, done = ride(scat(up2))
    swa_cts, rid = swa_bwd(*swa_in, o_s, d_swa, rider=rider)
    done(rid)
    own(up2)
    rider, done = ride(xchg("w_out"), join(down2, gate2, up2))
    fox_cts, rid = fox_bwd(*fox_in, o_f, lse, d_fox, rider=rider)
    done(rid)
    dproj, dsm_attn = attn_vjp((tuple(fox_cts), tuple(swa_cts)))
    dproj = dproj.astype(BF16)

    rider, done = ride(scat("w_out"))
    du, rid = mm_nt("mix_du", dproj, win, rider=rider)
    done(rid)
    dwin, _ = mm_tn("mix_dwin", u, dproj, out_dtype=BF16)
    grad("w_in", plan.split(dwin))
    rider, done = ride(xchg("w_in"))
    dh1, dgain_mix, dh1_b, rid = rms_bwd("mix_dnorm", h1, sm["norm_mix_g"], r_mix, du, dres=dh2, rider=rider, also_bf16=True)
    done(rid)
    own("w_out")

    rider, done = ride(scat("w_in", (0, 1, 2)))
    dwd1, rid = mm_tn("ffn1_dwd", hid1, dh1_b, out_dtype=BF16, scale=0.5, rider=rider)
    done(rid)
    grad(down1, dwd1)
    rider, done = ride(scat("w_in", (1, 2, 2)), xchg(down1))
    (dg1, du1), rid = ffn_dh("ffn1_dh", dh1_b, wd1, hdg1, hdu1, ns, 0.5, rider=rider)
    done(rid)
    own("w_in")
    rider, done = ride(scat(down1, (0, 1, 2)), join("w_out"))
    dwg1, rid = mm_tn_sharded("ffn1_dwg", xn1, dg1, ns, rider=rider)
    done(rid)
    grad(gate1, dwg1)
    rider, done = ride(scat(down1, (1, 2, 2)), xchg(gate1), join("w_in"))
    dwu1, rid = mm_tn_sharded("ffn1_dwu", xn1, du1, ns, rider=rider)
    done(rid)
    grad(up1, dwu1)
    own(down1)
    rider, done = ride(scat(gate1), xchg(up1), join(down1))
    dxn, rid = mm_nt_sharded("ffn1_dxn_g", dg1, bufs[gate1], rider=rider)
    done(rid)
    own(gate1)
    rider, done = ride(scat(up1, (0, 3, 4)))
    dxn, rid = mm_nt_sharded("ffn1_dxn_u", du1, bufs[up1], resid=dxn, rider=rider)
    done(rid)
    dx, dgain_ffn1 = rms_bwd("ffn1_dnorm", x, sm["norm_ffn1_g"], r1, dxn, dres=dh1)

    rider, done = ride(scat(up1, (3, 4, 4)), join(gate1))
    done(run_step("reduce_tail", rider))
    own(up1)
    rider, done = ride(join(up1))
    done(run_step("join_tail", rider))

    g_small = dict(dsm_attn)
    g_small["norm_mix_g"] = g_small["norm_mix_g"] + dgain_mix
    g_small.update(norm_ffn1_g=dgain_ffn1, norm_ffn2_g=dgain_ffn2, out_norm_fox_g=dgain_fox, out_norm_swa_g=dgain_swa)
    return loss, dx, full, g_small


def kernel(x, positions, norm_ffn1_g, ffn1_w_gate, ffn1_w_up, ffn1_w_down, norm_mix_g, w_in, b_forget, fox_q_norm_g, fox_k_norm_g, swa_q_norm_g, swa_k_norm_g, swa_sinks, out_norm_fox_g, out_norm_swa_g, w_out, norm_ffn2_g, ffn2_w_gate, ffn2_w_up, ffn2_w_down, loss_target, m_norm_ffn1_g, m_ffn1_w_gate, m_ffn1_w_up, m_ffn1_w_down, m_norm_mix_g, m_w_in, m_b_forget, m_fox_q_norm_g, m_fox_k_norm_g, m_swa_q_norm_g, m_swa_k_norm_g, m_swa_sinks, m_out_norm_fox_g, m_out_norm_swa_g, m_w_out, m_norm_ffn2_g, m_ffn2_w_gate, m_ffn2_w_up, m_ffn2_w_down, v_norm_ffn1_g, v_ffn1_w_gate, v_ffn1_w_up, v_ffn1_w_down, v_norm_mix_g, v_w_in, v_b_forget, v_fox_q_norm_g, v_fox_k_norm_g, v_swa_q_norm_g, v_swa_k_norm_g, v_swa_sinks, v_out_norm_fox_g, v_out_norm_swa_g, v_w_out, v_norm_ffn2_g, v_ffn2_w_gate, v_ffn2_w_up, v_ffn2_w_down):
    args = dict(locals())
    w = {k: args[k] for k in _ALL}
    m = {k: args["m_" + k] for k in _ALL}
    v = {k: args["v_" + k] for k in _ALL}
    c_idx = lax.axis_index("c").astype(jnp.int32).reshape(1)
    p_idx = (2 * lax.axis_index("x") + lax.axis_index("y")).astype(jnp.int32).reshape(1)
    pc_idx = jnp.concatenate([p_idx, c_idx])

    small = {k: w[k] for k in _SMALL}
    shards = {k: w[k][0] for k in _BIG}
    plan = WinPlan(x.shape[-1])
    shards["w_in"] = lax.switch(p_idx[0], [functools.partial(plan.place, s=s) for s in range(N_CHIPS)], shards["w_in"])
    loss, grad_x, g_shard, g_small = _local_step(shards, {k: w[k][0] for k in _SMALL}, x[0], positions[0], loss_target[0],
                                                 p_idx, c_idx, pc_idx)
    g_shard["w_in"] = lax.switch(p_idx[0], [functools.partial(plan.unplace, s=s) for s in range(N_CHIPS)], g_shard["w_in"])
    loss = lax.psum(loss, ("x", "y", "c"))
    g_small_sum = _unpack_small(all_reduce_small(_pack_small({k: g_small[k].reshape(1, -1) for k in _SMALL})), small)

    grad_w, delta, new_m, new_v = {}, {}, {}, {}
    for k in _BIG:
        (g, d, nm, nv), _ = adamw("adamw_" + k, w[k][0], g_shard[k], m[k][0], v[k][0])
        grad_w[k], delta[k], new_m[k], new_v[k] = g[None], d[None], nm[None], nv[None]
    (_, d, nm, nv), _ = adamw("adamw_small", _pack_small(small), _pack_small(g_small_sum), _pack_small({k: m[k] for k in _SMALL}),
                              _pack_small({k: v[k] for k in _SMALL}))
    grad_w.update(g_small_sum)
    delta.update(_unpack_small(d, small))
    new_m.update(_unpack_small(nm, small))
    new_v.update(_unpack_small(nv, small))

    return (loss, grad_x[None], *[grad_w[k] for k in _ALL], *[delta[k] for k in _ALL], *[new_m[k] for k in _ALL], *[new_v[k] for k in _ALL])
```

```python
import functools

import jax
import jax.numpy as jnp
from jax import lax
from jax.experimental import pallas as pl
from jax.experimental.pallas import tpu as pltpu

F32 = jnp.float32
BF16 = jnp.bfloat16

HEAD_DIM = 64
WINDOW = 128
ROPE_THETA = 10000.0
EPS = 1e-6
N_CHIPS = 4
N_DEV = 8

ADAM_LR = 0.001
ADAM_B1 = 0.9
ADAM_B2 = 0.999
ADAM_EPS = 1e-08
ADAM_WD = 0.01
ADAM_STEP = 10

V7X_VMEM_BYTES = 64 * 1024 * 1024
VMEM_LIMIT = V7X_VMEM_BYTES - 8 * 1024 * 1024
MASK_VALUE = -1e30

_MESH = pl.DeviceIdType.MESH
_HBM = pl.BlockSpec(memory_space=pl.ANY)
_DIMS = {"nn": (((1,), (0,)), ((), ())), "nt": (((1,), (1,)), ((), ())), "tn": (((0,), (0,)), ((), ()))}


def _pick(n, prefs):
    for p in prefs:
        if n % p == 0:
            return p
    return n


class Rider:
    def __init__(self, reads, aliased, news, nsem, build):
        self.reads, self.aliased, self.news, self.nsem, self.build = list(reads), list(aliased), list(news), nsem, build


class _Shifted:
    def __init__(self, ref, off):
        self.ref, self.off = ref, off

    @property
    def at(self):
        return self

    def __getitem__(self, k):
        return self.ref.at[k + self.off]


def combine(*riders):
    def build(reads, al, news, ssem, rsem):
        out = ([], [], [])
        r0 = a0 = n0 = s0 = 0
        for rd in riders:
            nr, na, nn = len(rd.reads), len(rd.aliased), len(rd.news)
            part = rd.build(reads[r0:r0 + nr], al[a0:a0 + na], news[n0:n0 + nn], _Shifted(ssem, s0), _Shifted(rsem, s0))
            for acc, lst in zip(out, part):
                acc.extend(lst)
            r0, a0, n0, s0 = r0 + nr, a0 + na, n0 + nn, s0 + rd.nsem
        return out

    return Rider(sum((r.reads for r in riders), []), sum((r.aliased for r in riders), []), sum((r.news for r in riders), []),
                 sum(r.nsem for r in riders), build)


def _me():
    return lax.axis_index("x"), lax.axis_index("y"), lax.axis_index("c")


def _other_chips(x, y):
    return [(1 - x, y), (x, 1 - y), (1 - x, 1 - y)]


WHOLE = (0, 1, 1)


def _rows(ref, start, rows, part=WHOLE):
    k0, k1, n = part
    assert rows % n == 0, (rows, part)
    idx = (slice(None),) * (len(ref.shape) - 2) + (pl.ds(start + k0 * (rows // n), (k1 - k0) * (rows // n)), slice(None))
    return ref.at[idx]


def _half(ref, h, part=WHOLE):
    rows = ref.shape[-2] // 2
    return _rows(ref, h * rows, rows, part)


def _remote(src, dst, ssem, rsem, k, to):
    return pltpu.make_async_remote_copy(src_ref=src, dst_ref=dst, send_sem=ssem.at[k], recv_sem=rsem.at[k], device_id=to,
                                        device_id_type=_MESH)


def _later(*args):
    return functools.partial(_remote, *args)


def gather(bufs, jobs):
    def build(reads, al, news, ssem, rsem):
        x, y, c = _me()
        p = 2 * x + y
        starts, arrivals = [], []
        for n, (b, kind, part) in enumerate(jobs):
            for j, chip in enumerate(_other_chips(x, y)):
                q = 2 * chip[0] + chip[1]
                if kind == "ici":
                    src, landing, to = _half(al[b].at[p], c, part), _half(al[b].at[q], c, part), (*chip, c)
                else:
                    src, landing, to = _half(al[b].at[q], c, part), _half(al[b].at[q], 1 - c, part), (x, y, 1 - c)
                starts.append(_later(src, src, ssem, rsem, 3 * n + j, to))
                arrivals.append(_later(landing, landing, ssem, rsem, 3 * n + j, to))
        return starts, arrivals, starts

    return Rider([], bufs, [], 3 * len(jobs), build)


def exchange_halves(grads):
    def build(reads, al, news, ssem, rsem):
        x, y, c = _me()
        cps = [_later(_half(g, 1 - c), t, ssem, rsem, w, (x, y, 1 - c)) for w, (g, t) in enumerate(zip(reads, news))]
        return cps, cps, cps

    return Rider(grads, [], [jax.ShapeDtypeStruct((g.shape[0], g.shape[1] // 2, g.shape[2]), g.dtype) for g in grads], len(grads), build)


def scatter_to_owner(sums, gots=None, part=WHOLE):
    def build(reads, al, news, ssem, rsem):
        x, y, c = _me()
        cps = []
        for w, (s, got) in enumerate(zip(reads, al or news)):
            rows = s.shape[-2]
            for j, chip in enumerate(_other_chips(x, y)):
                cps.append(_later(_rows(s.at[2 * chip[0] + chip[1]], 0, rows, part), _rows(got.at[j], 0, rows, part), ssem, rsem,
                                  3 * w + j, (*chip, c)))
        return cps, cps, cps

    news = [] if gots else [jax.ShapeDtypeStruct((3,) + s.shape[1:], s.dtype) for s in sums]
    return Rider(sums, gots or [], news, 3 * len(sums), build)


def join_halves(fulls):
    def build(reads, al, news, ssem, rsem):
        x, y, c = _me()
        starts, arrivals = [], []
        for w, f in enumerate(al):
            mine, landing = _half(f, c), _half(f, 1 - c)
            starts.append(_later(mine, mine, ssem, rsem, w, (x, y, 1 - c)))
            arrivals.append(_later(landing, landing, ssem, rsem, w, (x, y, 1 - c)))
        return starts, arrivals, starts

    return Rider([], fulls, [], len(fulls), build)


def _start_and_wait(rider, reads, al, news, ssem, rsem, first, last):
    @pl.when(first)
    def _():
        for cp in rider.build(reads, al, news, ssem, rsem)[0]:
            cp().start()

    def finish():
        @pl.when(last)
        def _():
            _, arrivals, sends = rider.build(reads, al, news, ssem, rsem)
            for cp in arrivals:
                cp().wait_recv()
            for cp in sends:
                cp().wait_send()

    return finish


def _call(name, body, grid, in_specs, args, out_specs, out_shape, scratch=(), semantics=None, rider=None, prefetch=()):
    n_pre, n_in, n_out, n_scr = len(prefetch), len(args), len(out_shape), len(scratch)
    nr, na, nn = (len(rider.reads), len(rider.aliased), len(rider.news)) if rider else (0, 0, 0)

    def wrapped(*refs):
        pre, refs = refs[:n_pre], refs[n_pre:]
        ins, reads = refs[:n_in], refs[n_in:n_in + nr]
        o0 = n_in + nr + na
        outs, al, news = refs[o0:o0 + n_out], refs[o0 + n_out:o0 + n_out + na], refs[o0 + n_out + na:o0 + n_out + na + nn]
        s0 = o0 + n_out + na + nn
        scr, (ssem, rsem) = refs[s0:s0 + n_scr], refs[s0 + n_scr:]
        first = functools.reduce(jnp.logical_and, [pl.program_id(a) == 0 for a in range(len(grid))])
        last = functools.reduce(jnp.logical_and, [pl.program_id(a) == g - 1 for a, g in enumerate(grid)])
        finish = _start_and_wait(rider, reads, al, news, ssem, rsem, first, last)
        body(*pre, *ins, *outs, *scr)
        finish()

    kernel_fn, all_in, all_out, shapes, scr = body, list(in_specs), list(out_specs), list(out_shape), list(scratch)
    operands, aliases = (*prefetch, *args), {}
    if rider:
        kernel_fn, semantics = wrapped, ("arbitrary",) * len(grid)
        all_in += [_HBM] * (nr + na)
        all_out += [_HBM] * (na + nn)
        shapes += [jax.ShapeDtypeStruct(a.shape, a.dtype) for a in rider.aliased] + rider.news
        scr += [pltpu.SemaphoreType.DMA((rider.nsem,)), pltpu.SemaphoreType.DMA((rider.nsem,))]
        operands += (*rider.reads, *rider.aliased)
        aliases = {n_pre + n_in + nr + i: n_out + i for i in range(na)}
    params = pltpu.CompilerParams(dimension_semantics=semantics, vmem_limit_bytes=VMEM_LIMIT)
    if n_pre:
        spec = pltpu.PrefetchScalarGridSpec(num_scalar_prefetch=n_pre, grid=grid, in_specs=all_in, out_specs=all_out, scratch_shapes=scr)
        outs = pl.pallas_call(kernel_fn, name=name, grid_spec=spec, out_shape=shapes, input_output_aliases=aliases, compiler_params=params)(*operands)
    else:
        outs = pl.pallas_call(kernel_fn, name=name, grid=grid, in_specs=all_in, out_specs=all_out, out_shape=shapes, scratch_shapes=scr,
                              input_output_aliases=aliases, compiler_params=params)(*operands)
    return list(outs[:n_out]), ((list(outs[n_out:n_out + na]), list(outs[n_out + na:])) if rider else None)


def run_step(name, rider):
    nr, na, nn = len(rider.reads), len(rider.aliased), len(rider.news)

    def body(*refs):
        reads = refs[:nr]
        al, news = refs[nr + na:nr + 2 * na], refs[nr + 2 * na:nr + 2 * na + nn]
        ssem, rsem = refs[nr + 2 * na + nn:]
        starts, arrivals, sends = rider.build(reads, al, news, ssem, rsem)
        for cp in starts:
            cp().start()
        for cp in arrivals:
            cp().wait_recv()
        for cp in sends:
            cp().wait_send()

    outs = pl.pallas_call(
        body, name=name, in_specs=[_HBM] * (nr + na), out_specs=[_HBM] * (na + nn),
        out_shape=[jax.ShapeDtypeStruct(a.shape, a.dtype) for a in rider.aliased] + rider.news,
        input_output_aliases={nr + i: i for i in range(na)},
        scratch_shapes=[pltpu.SemaphoreType.DMA((rider.nsem,)), pltpu.SemaphoreType.DMA((rider.nsem,))],
    )(*rider.reads, *rider.aliased)
    return list(outs[:na]), list(outs[na:])


def all_reduce_small(v):
    rows, lanes = v.shape

    def body(v_ref, o_ref, slots, send_sems, recv_sems):
        x, y, c = _me()
        me = 4 * x + 2 * y + c
        slots[me] = v_ref[...]
        cps = []
        for k in range(1, N_DEV):
            peer = (x ^ (k >> 2), y ^ ((k >> 1) & 1), c ^ (k & 1))
            cps.append(_remote(v_ref, slots.at[me], send_sems, recv_sems, k - 1, peer))
            cps[-1].start()
        for k in range(1, N_DEV):
            theirs = slots.at[me ^ k]
            _remote(theirs, theirs, send_sems, recv_sems, k - 1, (x, y, c)).wait_recv()
        for cp in cps:
            cp.wait_send()
        acc = slots[0]
        for i in range(1, N_DEV):
            acc = acc + slots[i]
        o_ref[...] = acc

    return pl.pallas_call(
        body, name="all_reduce_small",
        in_specs=[pl.BlockSpec(memory_space=pltpu.VMEM)], out_specs=pl.BlockSpec(memory_space=pltpu.VMEM),
        out_shape=jax.ShapeDtypeStruct((rows, lanes), F32),
        scratch_shapes=[pltpu.VMEM((N_DEV, rows, lanes), F32), pltpu.SemaphoreType.DMA((N_DEV - 1,)), pltpu.SemaphoreType.DMA((N_DEV - 1,))],
    )(v)


def _mm_call(name, mode, a, b, a_spec, b_spec, out_shape, out_spec, grid, acc_shape, scale=1.0, resid=None, resid_spec=None, rider=None):
    nk = grid[2]
    dims = _DIMS[mode]
    has_resid = resid is not None

    def body(*refs):
        a_ref, b_ref = refs[:2]
        r_ref = refs[2] if has_resid else None
        o_ref = refs[3] if has_resid else refs[2]

        def finish(r):
            if scale != 1.0:
                r = r * scale
            if has_resid:
                r = r_ref[...].astype(F32) + r
            o_ref[...] = r.astype(o_ref.dtype)

        part = lax.dot_general(a_ref[...].astype(BF16), b_ref[...].astype(BF16), dims, preferred_element_type=F32)
        if nk == 1:
            finish(part)
            return
        acc_ref = refs[-1]
        k = pl.program_id(2)

        @pl.when(k == 0)
        def _():
            acc_ref[...] = part

        @pl.when(k > 0)
        def _():
            acc_ref[...] += part

        @pl.when(k == nk - 1)
        def _():
            finish(acc_ref[...])

    in_specs = [a_spec, b_spec] + ([resid_spec] if has_resid else [])
    args = (a, b) + ((resid,) if has_resid else ())
    (out,), rid = _call(name, body, grid, in_specs, args, [out_spec], [out_shape], [pltpu.VMEM(acc_shape, F32)] if nk > 1 else [],
                        ("parallel", "parallel", "arbitrary"), rider)
    return out, rid


MM_VMEM_BUDGET = 40 * 1024 * 1024
_TILE_OPTS = (2048, 1408, 1024, 512, 256, 128)


def _tiles(m, n, kd, a_item, b_item, o_item, r_item=0, tm=None, tn=None, tk=None):
    def opts(full, fixed, cap):
        return [fixed] if fixed else [t for t in _TILE_OPTS if t <= cap and full % t == 0] or [full]

    best = None
    for cm in opts(m, tm, 1408):
        for cn in opts(n, tn, 1408):
            for ck in opts(kd, tk, 2048):
                blocks = cm * ck * a_item + ck * cn * b_item + cm * cn * (o_item + r_item)
                casts = (cm * ck * 2 if a_item == 4 else 0) + (ck * cn * 2 if b_item == 4 else 0)
                if 2 * blocks + cm * cn * 4 + casts <= MM_VMEM_BUDGET:
                    key = (cm * cn * ck, ck)
                    if best is None or key > best[0]:
                        best = (key, (cm, cn, ck))
    assert best is not None, (m, n, kd)
    return best[1]


def _item(x):
    return jnp.dtype(x.dtype).itemsize


def mm_nn(name, a, b, *, out_dtype=F32, scale=1.0, resid=None, rider=None):
    m, kd = a.shape
    n = b.shape[1]
    tm, tn, tk = _tiles(m, n, kd, _item(a), _item(b), jnp.dtype(out_dtype).itemsize, 0 if resid is None else _item(resid))
    o_spec = pl.BlockSpec((tm, tn), lambda i, j, k: (i, j))
    return _mm_call(
        name, "nn", a, b, pl.BlockSpec((tm, tk), lambda i, j, k: (i, k)), pl.BlockSpec((tk, tn), lambda i, j, k: (k, j)),
        jax.ShapeDtypeStruct((m, n), out_dtype), o_spec, (m // tm, n // tn, kd // tk), (tm, tn), scale, resid, o_spec, rider)


def mm_nt(name, a, b, *, out_dtype=F32, scale=1.0, resid=None, rider=None):
    m, kd = a.shape
    n = b.shape[0]
    tm, tn, tk = _tiles(m, n, kd, _item(a), _item(b), jnp.dtype(out_dtype).itemsize, 0 if resid is None else _item(resid))
    o_spec = pl.BlockSpec((tm, tn), lambda i, j, k: (i, j))
    return _mm_call(
        name, "nt", a, b, pl.BlockSpec((tm, tk), lambda i, j, k: (i, k)), pl.BlockSpec((tn, tk), lambda i, j, k: (j, k)),
        jax.ShapeDtypeStruct((m, n), out_dtype), o_spec, (m // tm, n // tn, kd // tk), (tm, tn), scale, resid, o_spec, rider)


def mm_tn(name, a, b, *, out_dtype=F32, scale=1.0, rider=None):
    kd, m = a.shape
    n = b.shape[1]
    tm, tn, tk = _tiles(m, n, kd, _item(a), _item(b), jnp.dtype(out_dtype).itemsize)
    return _mm_call(
        name, "tn", a, b, pl.BlockSpec((tk, tm), lambda i, j, k: (k, i)), pl.BlockSpec((tk, tn), lambda i, j, k: (k, j)),
        jax.ShapeDtypeStruct((m, n), out_dtype), pl.BlockSpec((tm, tn), lambda i, j, k: (i, j)),
        (m // tm, n // tn, kd // tk), (tm, tn), scale, rider=rider)


def mm_nt_sharded(name, a, w, *, resid=None, rider=None):
    m = a.shape[0]
    ns, n, c = w.shape
    tm, tn, _ = _tiles(m, n, c, _item(a), _item(w), 4, 0 if resid is None else _item(resid), tk=c)
    o_spec = pl.BlockSpec((tm, tn), lambda i, j, k: (i, j))
    return _mm_call(
        name, "nt", a, w, pl.BlockSpec((tm, c), lambda i, j, k: (i, k)), pl.BlockSpec((None, tn, c), lambda i, j, k: (k, j, 0)),
        jax.ShapeDtypeStruct((m, n), F32), o_spec, (m // tm, n // tn, ns), (tm, tn), 1.0, resid, o_spec, rider)


def mm_tn_sharded(name, a, b, ns, *, rider=None):
    kd, m = a.shape
    c = b.shape[1] // ns
    tm, _, tk = _tiles(m, c, kd, _item(a), _item(b), 2, tn=c)
    return _mm_call(
        name, "tn", a, b, pl.BlockSpec((tk, tm), lambda i, j, k: (k, i)), pl.BlockSpec((tk, c), lambda i, j, k: (k, j)),
        jax.ShapeDtypeStruct((ns, m, c), BF16), pl.BlockSpec((None, tm, c), lambda i, j, k: (j, i, 0)),
        (m // tm, ns, kd // tk), (tm, c), rider=rider)


def rms_fwd(name, x, g, out_dtype, rider=None):
    r, c = x.shape
    tm = _pick(r, (512, 256, 128, 64, 8))

    def body(x_ref, g_ref, y_ref, r_ref):
        xf = x_ref[...].astype(F32)
        rstd = lax.rsqrt(jnp.mean(xf * xf, axis=-1, keepdims=True) + EPS)
        y_ref[...] = ((xf * rstd) * g_ref[...]).astype(y_ref.dtype)
        r_ref[...] = rstd

    (y, rstd), rid = _call(
        name, body, (r // tm,), [pl.BlockSpec((tm, c), lambda i: (i, 0)), pl.BlockSpec((1, c), lambda i: (0, 0))], (x, g.reshape(1, c)),
        [pl.BlockSpec((tm, c), lambda i: (i, 0)), pl.BlockSpec((tm, 1), lambda i: (i, 0))],
        [jax.ShapeDtypeStruct((r, c), out_dtype), jax.ShapeDtypeStruct((r, 1), F32)], (), ("parallel",), rider)
    return (y, rstd) if rider is None else (y, rstd, rid)


def rms_bwd(name, x, g, rstd, dy, dres=None, rider=None, also_bf16=False):
    r, c = x.shape
    tm = _pick(r, (512, 256, 128, 64, 8))
    has_res = dres is not None

    def body(*refs):
        x_ref, g_ref, r_ref, dy_ref = refs[:4]
        dres_ref = refs[4] if has_res else None
        dx_ref, dg_ref = refs[4 + has_res:6 + has_res]
        xhat = x_ref[...].astype(F32) * r_ref[...]
        dyf = dy_ref[...].astype(F32)
        gdy = dyf * g_ref[...]
        dx = r_ref[...] * (gdy - xhat * jnp.mean(gdy * xhat, axis=-1, keepdims=True))
        if has_res:
            dx = dx + dres_ref[...]
        dx_ref[...] = dx
        if also_bf16:
            refs[-1][...] = dx.astype(BF16)

        @pl.when(pl.program_id(0) == 0)
        def _():
            dg_ref[...] = jnp.zeros_like(dg_ref)

        dg_ref[...] += jnp.sum(dyf * xhat, axis=0, keepdims=True)

    row = pl.BlockSpec((tm, c), lambda i: (i, 0))
    in_specs = [row, pl.BlockSpec((1, c), lambda i: (0, 0)), pl.BlockSpec((tm, 1), lambda i: (i, 0)), row] + ([row] if has_res else [])
    args = (x, g.reshape(1, c), rstd, dy) + ((dres,) if has_res else ())
    outs, rid = _call(name, body, (r // tm,), in_specs, args, [row, pl.BlockSpec((1, c), lambda i: (0, 0))] + [row] * also_bf16,
                      [jax.ShapeDtypeStruct((r, c), F32), jax.ShapeDtypeStruct((1, c), F32)] + [jax.ShapeDtypeStruct((r, c), BF16)] * also_bf16,
                      (), ("arbitrary",), rider)
    return (outs[0], outs[1].reshape(c), *outs[2:], *([] if rider is None else [rid]))


_LANES = 128


def _head_mean(v):
    if v.shape[1] == HEAD_DIM:
        return jnp.mean(v, axis=-1, keepdims=True)
    low = lax.broadcasted_iota(jnp.int32, v.shape, 1) < HEAD_DIM
    lo = jnp.sum(jnp.where(low, v, 0.0), axis=-1, keepdims=True)
    hi = jnp.sum(jnp.where(low, 0.0, v), axis=-1, keepdims=True)
    return jnp.where(low, lo, hi) * (1.0 / HEAD_DIM)


def _head_groups(c):
    width = _LANES if c % _LANES == 0 else HEAD_DIM
    assert c % width == 0, c
    return width, [slice(k * width, (k + 1) * width) for k in range(c // width)]


def _head_gain(g, width):
    return jnp.tile(g.reshape(1, HEAD_DIM), (1, width // HEAD_DIM))


def _rotate_half(y):
    half = HEAD_DIM // 2
    first = lax.broadcasted_iota(jnp.int32, y.shape, 1) % HEAD_DIM < half
    return jnp.where(first, -pltpu.roll(y, y.shape[1] - half, axis=1), pltpu.roll(y, half, axis=1))


def _rope_tables(rope, width):
    return [jnp.tile(t, (1, 2 * width // HEAD_DIM)) for t in rope]


def head_rms_fwd(name, x, g, rope=None):
    s, c = x.shape
    tm = _pick(s, (512, 256, 128, 8))
    width, groups = _head_groups(c)

    def body(x_ref, g_ref, *refs):
        y_ref = refs[-1]
        for sl in groups:
            xs = x_ref[:, sl]
            y = (xs * lax.rsqrt(_head_mean(xs * xs) + EPS)) * g_ref[...]
            if rope:
                y = y * refs[0][...] + _rotate_half(y) * refs[1][...]
            y_ref[:, sl] = y

    row = pl.BlockSpec((tm, c), lambda i: (i, 0))
    tab = pl.BlockSpec((tm, width), lambda i: (i, 0))
    tables = _rope_tables(rope, width) if rope else []
    (y,), _ = _call(name, body, (s // tm,), [row, pl.BlockSpec((1, width), lambda i: (0, 0))] + [tab] * len(tables),
                    (x, _head_gain(g, width), *tables), [row], [jax.ShapeDtypeStruct((s, c), F32)], (), ("parallel",))
    return y


def head_rms_bwd(name, x, g, dy, rope=None):
    s, c = x.shape
    tm = _pick(s, (512, 256, 128, 8))
    width, groups = _head_groups(c)

    def body(x_ref, g_ref, dy_ref, *refs):
        dx_ref, dg_ref = refs[-2:]

        @pl.when(pl.program_id(0) == 0)
        def _():
            dg_ref[...] = jnp.zeros_like(dg_ref)

        for sl in groups:
            xs, dys = x_ref[:, sl], dy_ref[:, sl]
            if rope:
                dys = dys * refs[0][...] - _rotate_half(dys * refs[1][...])
            rstd = lax.rsqrt(_head_mean(xs * xs) + EPS)
            xhat = xs * rstd
            gdy = dys * g_ref[...]
            dx_ref[:, sl] = rstd * (gdy - xhat * _head_mean(gdy * xhat))
            dg_ref[...] += jnp.sum(dys * xhat, axis=0, keepdims=True)

    row = pl.BlockSpec((tm, c), lambda i: (i, 0))
    vec = pl.BlockSpec((1, width), lambda i: (0, 0))
    tab = pl.BlockSpec((tm, width), lambda i: (i, 0))
    tables = _rope_tables(rope, width) if rope else []
    (dx, dg), _ = _call(name, body, (s // tm,), [row, vec, row] + [tab] * len(tables), (x, _head_gain(g, width), dy, *tables), [row, vec],
                        [jax.ShapeDtypeStruct((s, c), F32), jax.ShapeDtypeStruct((1, width), F32)], (), ("arbitrary",))
    return dx, jnp.sum(dg.reshape(width // HEAD_DIM, HEAD_DIM), axis=0)


@functools.partial(jax.custom_vjp, nondiff_argnums=(0,))
def head_rms(name, x, g):
    return head_rms_fwd(name + "_fwd", x, g)


def _head_rms_fwd(name, x, g):
    return head_rms_fwd(name + "_fwd", x, g), (x, g)


def _head_rms_bwd(name, res, dy):
    return head_rms_bwd(name + "_bwd", *res, dy)


head_rms.defvjp(_head_rms_fwd, _head_rms_bwd)


@functools.partial(jax.custom_vjp, nondiff_argnums=(0,))
def head_rms_rope(name, x, g, cos, sin):
    return head_rms_fwd(name + "_fwd", x, g, (cos, sin))


def _head_rms_rope_fwd(name, x, g, cos, sin):
    return head_rms_fwd(name + "_fwd", x, g, (cos, sin)), (x, g, cos, sin)


def _head_rms_rope_bwd(name, res, dy):
    x, g, cos, sin = res
    return (*head_rms_bwd(name + "_bwd", x, g, dy, (cos, sin)), jnp.zeros_like(cos), jnp.zeros_like(sin))


head_rms_rope.defvjp(_head_rms_rope_fwd, _head_rms_rope_bwd)


def _cumsum_call(name, a, reverse):
    h, s = a.shape
    tb = _LANES
    assert s % tb == 0

    def body(a_ref, o_ref):
        t_in = lax.broadcasted_iota(jnp.int32, (tb, tb), 0)
        t_out = lax.broadcasted_iota(jnp.int32, (tb, tb), 1)
        tri = jnp.where((t_in >= t_out) if reverse else (t_in <= t_out), 1.0, 0.0).astype(BF16)
        carry = jnp.zeros((h, 1), F32)
        blocks = range(s // tb)
        for b in (reversed(blocks) if reverse else blocks):
            cols = slice(b * tb, (b + 1) * tb)
            block = a_ref[:, cols]
            rest, local = block, jnp.zeros((h, tb), F32)
            for _ in range(3):
                piece = rest.astype(BF16)
                local = local + jnp.dot(piece, tri, preferred_element_type=F32)
                rest = rest - piece.astype(F32)
            o_ref[:, cols] = local + carry
            carry = carry + jnp.sum(block, axis=1, keepdims=True)

    whole = pl.BlockSpec((h, s), lambda j: (0, 0))
    (out,), _ = _call(name, body, (1,), [whole], (a,), [whole], [jax.ShapeDtypeStruct((h, s), F32)], (), ("arbitrary",))
    return out


@jax.custom_vjp
def time_cumsum(a):
    return _cumsum_call("gate_cumsum", a, False)


def _time_cumsum_fwd(a):
    return _cumsum_call("gate_cumsum", a, False), None


def _time_cumsum_bwd(_, dc):
    return (_cumsum_call("gate_cumsum_bwd", dc, True),)


time_cumsum.defvjp(_time_cumsum_fwd, _time_cumsum_bwd)


FFN_TM = 512


def _sigmoid(x):
    return 1.0 / (1.0 + jnp.exp(-x))


def ffn_gu(name, xn, wg, wu, rider=None):
    s, d = xn.shape
    ns, _, c = wg.shape
    tm = _pick(s, (FFN_TM, 128))

    def body(x_ref, wg_ref, wu_ref, h_ref, a_ref, b_ref):
        xb = x_ref[...]
        gv = jnp.dot(xb, wg_ref[...], preferred_element_type=F32)
        uv = jnp.dot(xb, wu_ref[...], preferred_element_type=F32)
        sig = _sigmoid(gv)
        silu = gv * sig
        h_ref[...] = (silu * uv).astype(BF16)
        a_ref[...] = (uv * (sig * (1.0 + gv * (1.0 - sig)))).astype(BF16)
        b_ref[...] = silu.astype(BF16)

    w_spec = pl.BlockSpec((None, d, c), lambda j, i: (j, 0, 0))
    o_spec = pl.BlockSpec((tm, c), lambda j, i: (i, j))
    return _call(
        name, body, (ns, s // tm), [pl.BlockSpec((tm, d), lambda j, i: (i, 0)), w_spec, w_spec], (xn, wg, wu),
        [o_spec, o_spec, o_spec], [jax.ShapeDtypeStruct((s, ns * c), BF16)] * 3, [], ("parallel", "parallel"), rider)


def ffn_dh(name, dy, wd, dh_dg, dh_du, ns, scale, rider=None):
    s, d = dy.shape
    f = wd.shape[0]
    c = f // ns
    tm = _pick(s, (FFN_TM, 128))

    def body(dy_ref, wd_ref, a_ref, b_ref, dg_ref, du_ref):
        dh = lax.dot_general(dy_ref[...].astype(BF16), wd_ref[...], _DIMS["nt"], preferred_element_type=F32) * scale
        dg_ref[...] = (dh * a_ref[...].astype(F32)).astype(BF16)
        du_ref[...] = (dh * b_ref[...].astype(F32)).astype(BF16)

    o_spec = pl.BlockSpec((tm, c), lambda j, i: (i, j))
    return _call(
        name, body, (ns, s // tm),
        [pl.BlockSpec((tm, d), lambda j, i: (i, 0)), pl.BlockSpec((c, d), lambda j, i: (j, 0)), o_spec, o_spec], (dy, wd, dh_dg, dh_du),
        [o_spec, o_spec], [jax.ShapeDtypeStruct((s, f), BF16), jax.ShapeDtypeStruct((s, f), BF16)],
        [], ("parallel", "parallel"), rider)


FOX_TQ = 1024


def fox_tile(s_len):
    return min(FOX_TQ, s_len)


def _heads_per_block(h):
    return 2 if h % 2 == 0 else 1


def _fox_queries(q):
    return (q * (HEAD_DIM ** -0.5)).astype(BF16)


def _fox_scores(qs, kc, cq, ck, diagonal):
    s = lax.dot_general(qs, kc.astype(BF16), _DIMS["nt"], preferred_element_type=F32) + cq - ck
    if not diagonal:
        return s
    return jnp.where(lax.broadcasted_iota(jnp.int32, s.shape, 0) >= lax.broadcasted_iota(jnp.int32, s.shape, 1), s, MASK_VALUE)


def _fox_specs(h, s_len, tq):
    hb = _heads_per_block(h)
    qb = pl.BlockSpec((tq, hb * HEAD_DIM), lambda pp, i: (i, pp))
    kb = pl.BlockSpec((s_len, hb * HEAD_DIM), lambda pp, i: (0, pp))
    colb = pl.BlockSpec((hb, tq, 1), lambda pp, i: (pp, i, 0))
    rowb = pl.BlockSpec((hb, s_len // tq, 1, tq), lambda pp, i: (pp, 0, 0, 0))
    return hb, qb, kb, colb, rowb


def fox_fwd(q, k, v, cq, ck, rider=None):
    s_len, hd = q.shape
    h, d = hd // HEAD_DIM, HEAD_DIM
    tq = fox_tile(s_len)
    hb, qb, kb, colb, rowb = _fox_specs(h, s_len, tq)

    def body(q_ref, k_ref, v_ref, cq_ref, ck_ref, o_ref, lse_ref):
        i = pl.program_id(1)
        for hh in range(hb):
            lanes = slice(hh * d, (hh + 1) * d)
            qs, cqv = _fox_queries(q_ref[:, lanes]), cq_ref[hh]

            def chunk(c, carry, diagonal=False):
                m, l, acc = carry
                rows = pl.ds(pl.multiple_of(c * tq, tq), tq)
                s = _fox_scores(qs, k_ref[rows, lanes], cqv, ck_ref[hh, c], diagonal)
                m_new = jnp.maximum(m, jnp.max(s, axis=-1, keepdims=True))
                alpha = jnp.exp(m - m_new)
                p = jnp.exp(s - m_new)
                acc = alpha * acc + jnp.dot(p.astype(BF16), v_ref[rows, lanes].astype(BF16), preferred_element_type=F32)
                return m_new, alpha * l + jnp.sum(p, axis=-1, keepdims=True), acc

            init = (jnp.full((tq, 1), MASK_VALUE, F32), jnp.zeros((tq, 1), F32), jnp.zeros((tq, d), F32))
            m, l, acc = chunk(i, lax.fori_loop(0, i, chunk, init), diagonal=True)
            o_ref[:, lanes] = acc / l
            lse_ref[hh] = m + jnp.log(l)

    return _call(
        "fox_fwd", body, (h // hb, s_len // tq), [qb, kb, kb, colb, rowb], (q, k, v, cq, ck), [qb, colb],
        [jax.ShapeDtypeStruct((s_len, hd), F32), jax.ShapeDtypeStruct((h, s_len, 1), F32)], (), ("parallel", "parallel"), rider)


def fox_bwd(q, k, v, cq, ck, o, lse, do, rider=None):
    s_len, hd = q.shape
    h, d = hd // HEAD_DIM, HEAD_DIM
    tq = fox_tile(s_len)
    scale = HEAD_DIM ** -0.5
    hb, qb, kb, colb, rowb = _fox_specs(h, s_len, tq)

    def body(q_ref, k_ref, v_ref, cq_ref, ck_ref, o_ref, lse_ref, do_ref, dq_ref, dk_ref, dv_ref, dcq_ref, dck_ref):
        i = pl.program_id(1)

        @pl.when(i == 0)
        def _():
            dk_ref[...] = jnp.zeros_like(dk_ref)
            dv_ref[...] = jnp.zeros_like(dv_ref)
            dck_ref[...] = jnp.zeros_like(dck_ref)

        heads = []
        for hh in range(hb):
            lanes = slice(hh * d, (hh + 1) * d)
            dof = do_ref[:, lanes]
            heads.append((lanes, _fox_queries(q_ref[:, lanes]), cq_ref[hh], lse_ref[hh], dof.astype(BF16),
                          jnp.sum(dof * o_ref[:, lanes], axis=-1, keepdims=True)))

        def chunk(c, carry, diagonal=False):
            rows = pl.ds(pl.multiple_of(c * tq, tq), tq)
            out, dks, dvs = [], [], []
            for hh, (lanes, qs, cqv, lse_h, dob, delta) in enumerate(heads):
                dq, dcq = carry[hh]
                kc = k_ref[rows, lanes]
                p = jnp.exp(_fox_scores(qs, kc, cqv, ck_ref[hh, c], diagonal) - lse_h)
                dp = lax.dot_general(dob, v_ref[rows, lanes].astype(BF16), _DIMS["nt"], preferred_element_type=F32)
                ds = p * (dp - delta)
                dsb = ds.astype(BF16)
                dvs.append(lax.dot_general(p.astype(BF16), dob, _DIMS["tn"], preferred_element_type=F32))
                dks.append(lax.dot_general(dsb, qs, _DIMS["tn"], preferred_element_type=F32))
                dck_ref[hh, c] -= jnp.sum(ds, axis=0, keepdims=True)
                out.append((dq + jnp.dot(dsb, kc.astype(BF16), preferred_element_type=F32), dcq + jnp.sum(ds, axis=-1, keepdims=True)))
            dk_ref[rows, :] += jnp.concatenate(dks, axis=1)
            dv_ref[rows, :] += jnp.concatenate(dvs, axis=1)
            return tuple(out)

        init = tuple((jnp.zeros((tq, d), F32), jnp.zeros((tq, 1), F32)) for _ in range(hb))
        done = chunk(i, lax.fori_loop(0, i, chunk, init), diagonal=True)
        dq_ref[...] = jnp.concatenate([dq for dq, _ in done], axis=1) * scale
        for hh, (_, dcq) in enumerate(done):
            dcq_ref[hh] = dcq

    return _call(
        "fox_bwd", body, (h // hb, s_len // tq), [qb, kb, kb, colb, rowb, qb, colb, qb], (q, k, v, cq, ck, o, lse, do),
        [qb, kb, kb, colb, rowb],
        [jax.ShapeDtypeStruct((s_len, hd), F32)] * 3
        + [jax.ShapeDtypeStruct((h, s_len, 1), F32), jax.ShapeDtypeStruct((h, s_len // tq, 1, tq), F32)],
        (), ("parallel", "arbitrary"), rider)


def _window(prev_ref, cur_ref, hh):
    lanes = slice(hh * HEAD_DIM, (hh + 1) * HEAD_DIM)
    return jnp.concatenate([prev_ref[:, lanes], cur_ref[:, lanes]], axis=0).astype(BF16)


def _swa_band(g, w):
    t = lax.broadcasted_iota(jnp.int32, (g * w, 2 * w), 0) % w
    col = lax.broadcasted_iota(jnp.int32, (g * w, 2 * w), 1)
    rel = t + w - col
    band = (rel >= 0) & (rel < w)
    return jnp.where(jnp.stack([band & (col >= w), band]), 0.0, MASK_VALUE).astype(F32)


def _swa_probs(qs, kw, sink, band):
    s = lax.dot_general(qs, kw, _DIMS["nt"], preferred_element_type=F32) + band
    m = jnp.maximum(jnp.max(s, axis=-1, keepdims=True), sink)
    p = jnp.exp(s - m)
    ps = jnp.exp(sink - m)
    linv = 1.0 / (jnp.sum(p, axis=-1, keepdims=True) + ps)
    return p * linv, ps * linv


def _swa_specs(hk, g, s_len):
    w = WINDOW
    assert s_len % w == 0
    hb = _heads_per_block(hk)
    qb = pl.BlockSpec((w, hb * g * HEAD_DIM), lambda pp, n: (n, pp))
    prev = pl.BlockSpec((w, hb * HEAD_DIM), lambda pp, n: (jnp.maximum(n - 1, 0), pp))
    cur = pl.BlockSpec((w, hb * HEAD_DIM), lambda pp, n: (n, pp))
    sb = pl.BlockSpec((hb, g * w, 1), lambda pp, n: (pp, 0, 0))
    band = pl.BlockSpec((None, g * w, 2 * w), lambda pp, n: (jnp.minimum(n, 1), 0, 0))
    return hb, qb, prev, cur, sb, band


def swa_fwd(q, k, v, sink, rider=None):
    s_len = q.shape[0]
    hk = k.shape[1] // HEAD_DIM
    g = q.shape[1] // k.shape[1]
    w, d = WINDOW, HEAD_DIM
    hb, qb, prev, cur, sb, bandb = _swa_specs(hk, g, s_len)

    def body(q_ref, kp_ref, kc_ref, vp_ref, vc_ref, sink_ref, band_ref, o_ref):
        band = band_ref[:w]
        for hh in range(hb):
            kw, vw = _window(kp_ref, kc_ref, hh), _window(vp_ref, vc_ref, hh)
            for j in range(g):
                lanes = slice((hh * g + j) * d, (hh * g + j + 1) * d)
                qs = (q_ref[:, lanes] * (HEAD_DIM ** -0.5)).astype(BF16)
                p, _ = _swa_probs(qs, kw, sink_ref[hh, j * w:(j + 1) * w], band)
                o_ref[:, lanes] = jnp.dot(p.astype(BF16), vw, preferred_element_type=F32)

    (o,), rid = _call("swa_fwd", body, (hk // hb, s_len // w), [qb, prev, cur, prev, cur, sb, bandb],
                      (q, k, k, v, v, sink, _swa_band(g, w)), [qb], [jax.ShapeDtypeStruct(q.shape, F32)], (),
                      ("parallel", "parallel"), rider)
    return o, rid


def swa_bwd(q, k, v, sink, o, do, rider=None):
    s_len = q.shape[0]
    hk = k.shape[1] // HEAD_DIM
    g = q.shape[1] // k.shape[1]
    w, d = WINDOW, HEAD_DIM
    scale = HEAD_DIM ** -0.5
    hb, qb, prev, cur, sb, bandb = _swa_specs(hk, g, s_len)

    def body(q_ref, kp_ref, kc_ref, vp_ref, vc_ref, sink_ref, band_ref, o_ref, do_ref, dq_ref, dkp_ref, dkc_ref, dvp_ref, dvc_ref,
             dsink_ref):
        @pl.when(pl.program_id(1) == 0)
        def _():
            dsink_ref[...] = jnp.zeros_like(dsink_ref)

        band = band_ref[:w]
        for hh in range(hb):
            lanes = slice(hh * d, (hh + 1) * d)
            kw, vw = _window(kp_ref, kc_ref, hh), _window(vp_ref, vc_ref, hh)
            dkw, dvw = jnp.zeros((2 * w, d), F32), jnp.zeros((2 * w, d), F32)
            for j in range(g):
                ql = slice((hh * g + j) * d, (hh * g + j + 1) * d)
                rows = slice(j * w, (j + 1) * w)
                qs = (q_ref[:, ql] * scale).astype(BF16)
                p, ps = _swa_probs(qs, kw, sink_ref[hh, rows], band)
                dof = do_ref[:, ql]
                dob = dof.astype(BF16)
                delta = jnp.sum(dof * o_ref[:, ql], axis=-1, keepdims=True)
                dp = lax.dot_general(dob, vw, _DIMS["nt"], preferred_element_type=F32)
                ds = p * (dp - delta)
                dsb = ds.astype(BF16)
                dsink_ref[hh, rows] -= ps * delta
                dq_ref[:, ql] = jnp.dot(dsb, kw, preferred_element_type=F32) * scale
                dkw = dkw + lax.dot_general(dsb, qs, _DIMS["tn"], preferred_element_type=F32)
                dvw = dvw + lax.dot_general(p.astype(BF16), dob, _DIMS["tn"], preferred_element_type=F32)
            dkp_ref[:, lanes] = dkw[:w]
            dkc_ref[:, lanes] = dkw[w:]
            dvp_ref[:, lanes] = dvw[:w]
            dvc_ref[:, lanes] = dvw[w:]

    kv_shape = jax.ShapeDtypeStruct(k.shape, F32)
    (dq, dkp, dkc, dvp, dvc, dsink), rid = _call(
        "swa_bwd", body, (hk // hb, s_len // w), [qb, prev, cur, prev, cur, sb, bandb, qb, qb],
        (q, k, k, v, v, sink, _swa_band(g, w), o, do),
        [qb, cur, cur, cur, cur, sb],
        [jax.ShapeDtypeStruct(q.shape, F32), kv_shape, kv_shape, kv_shape, kv_shape, jax.ShapeDtypeStruct((hk, g * w, 1), F32)],
        (), ("parallel", "arbitrary"), rider)

    def shift_up(a):
        return jnp.concatenate([a[w:], jnp.zeros_like(a[:w])], axis=0)

    return (dq, dkc + shift_up(dkp), dvc + shift_up(dvp), dsink), rid


def loss_call(y, target):
    s, d = y.shape
    tm = _pick(s, (512, 256, 128))

    def body(y_ref, t_ref, l_ref, dy_ref, dyb_ref):
        e = y_ref[...] - t_ref[...]
        dy = e * (1.0 / d)
        dy_ref[...] = dy
        dyb_ref[...] = dy.astype(BF16)

        @pl.when(pl.program_id(0) == 0)
        def _():
            l_ref[...] = jnp.zeros_like(l_ref)

        l_ref[...] += jnp.sum(jnp.sum(e * e, axis=0, keepdims=True), axis=1, keepdims=True) * (0.5 / d)

    row = pl.BlockSpec((tm, d), lambda i: (i, 0))
    (l, dy, dyb), _ = _call("loss_head", body, (s // tm,), [row, row], (y, target), [pl.BlockSpec((1, 1), lambda i: (0, 0)), row, row],
                            [jax.ShapeDtypeStruct((1, 1), F32), jax.ShapeDtypeStruct((s, d), F32), jax.ShapeDtypeStruct((s, d), BF16)],
                            (), ("arbitrary",))
    return l[0, 0], dy, dyb


def _row_tile(rows, cols, itemsize, block_bytes=1 << 20):
    target = max(16, block_bytes // (cols * itemsize))
    fits = [t for t in range(16, rows + 1, 16) if rows % t == 0 and t <= target]
    return fits[-1] if fits else rows


CAST_STEPS = 8


def cast_place(name, ws, p_idx, rider=None):
    n = len(ws)
    assert all(w.shape[0] % (16 * CAST_STEPS) == 0 for w in ws), [w.shape for w in ws]

    def body(p_ref, *refs):
        for w_ref, o_ref in zip(refs[:n], refs[n:]):
            o_ref[...] = w_ref[...].astype(BF16)

    return _call(
        name, body, (CAST_STEPS,), [pl.BlockSpec((w.shape[0] // CAST_STEPS, w.shape[1]), lambda i, pr: (i, 0)) for w in ws], tuple(ws),
        [pl.BlockSpec((None, w.shape[0] // CAST_STEPS, w.shape[1]), lambda i, pr: (pr[0], i, 0)) for w in ws],
        [jax.ShapeDtypeStruct((N_CHIPS,) + w.shape, BF16) for w in ws], (), ("parallel",), rider, prefetch=(p_idx,))


def chip_sum(name, grad, theirs, c_idx):
    ns, r, cols = grad.shape
    rh = r // 2
    tr = _row_tile(rh, cols, 2, 4 << 20)
    nb = rh // tr

    def body(c_ref, a_ref, b_ref, o_ref):
        o_ref[...] = (a_ref[...].astype(F32) + b_ref[...].astype(F32)).astype(o_ref.dtype)

    return pl.pallas_call(
        body, name=name,
        grid_spec=pltpu.PrefetchScalarGridSpec(
            num_scalar_prefetch=1, grid=(ns, nb),
            in_specs=[pl.BlockSpec((None, tr, cols), lambda q, i, cr: (q, cr[0] * nb + i, 0)),
                      pl.BlockSpec((None, tr, cols), lambda q, i, cr: (q, i, 0))],
            out_specs=pl.BlockSpec((None, tr, cols), lambda q, i, cr: (q, i, 0))),
        out_shape=jax.ShapeDtypeStruct((ns, rh, cols), BF16),
        compiler_params=pltpu.CompilerParams(dimension_semantics=("parallel", "parallel"), vmem_limit_bytes=VMEM_LIMIT),
    )(c_idx, grad, theirs)


def owner_sum(name, sums, got, pc_idx):
    ns, rh, cols = sums.shape
    tr = _row_tile(rh, cols, 4, 4 << 20)
    nb = rh // tr

    def body(pc_ref, a_ref, b_ref, o_ref):
        o_ref[...] = ((a_ref[...].astype(F32) + b_ref[0].astype(F32)) + b_ref[1].astype(F32)) + b_ref[2].astype(F32)

    return pl.pallas_call(
        body, name=name,
        grid_spec=pltpu.PrefetchScalarGridSpec(
            num_scalar_prefetch=1, grid=(nb,),
            in_specs=[pl.BlockSpec((None, tr, cols), lambda i, pc: (pc[0], i, 0)),
                      pl.BlockSpec((3, tr, cols), lambda i, pc: (0, i, 0))],
            out_specs=pl.BlockSpec((tr, cols), lambda i, pc: (pc[1] * nb + i, 0))),
        out_shape=jax.ShapeDtypeStruct((2 * rh, cols), F32),
        compiler_params=pltpu.CompilerParams(dimension_semantics=("parallel",), vmem_limit_bytes=VMEM_LIMIT),
    )(pc_idx, sums, got)


def adamw(name, w, g, m, v):
    r, cols = w.shape
    tr = _row_tile(r, cols, 4, 3 << 20)
    c1 = 1.0 / (1.0 - ADAM_B1 ** ADAM_STEP)
    c2 = 1.0 / (1.0 - ADAM_B2 ** ADAM_STEP)

    def body(w_ref, g_ref, m_ref, v_ref, go_ref, d_ref, nm_ref, nv_ref):
        gv = g_ref[...]
        nm = ADAM_B1 * m_ref[...] + (1.0 - ADAM_B1) * gv
        nv = ADAM_B2 * v_ref[...] + (1.0 - ADAM_B2) * (gv * gv)
        go_ref[...] = gv
        d_ref[...] = -ADAM_LR * ((nm * c1) / (jnp.sqrt(nv * c2) + ADAM_EPS) + ADAM_WD * w_ref[...])
        nm_ref[...] = nm
        nv_ref[...] = nv

    blk = pl.BlockSpec((tr, cols), lambda i: (i, 0))
    return _call(name, body, (r // tr,), [blk] * 4, (w, g, m, v), [blk] * 4, [jax.ShapeDtypeStruct((r, cols), F32)] * 4, (), ("parallel",))


def _win_layout(d_model):
    hf = hq = d_model // (2 * HEAD_DIM)
    hk = hq // 4
    sizes = [hf * HEAD_DIM, hf * HEAD_DIM, hf * HEAD_DIM, hf, hq * HEAD_DIM, hk * HEAD_DIM, hk * HEAD_DIM]
    return hf, hq, hk, sizes


class WinPlan:
    def __init__(self, d_model, ns=N_CHIPS):
        self.hf, self.hq, self.hk, self.sizes = _win_layout(d_model)
        self.ns, self.cs = ns, sum(self.sizes) // ns
        self.jump_at = sum(self.sizes[:4])
        self.jump_by = -self.jump_at % _LANES
        self.base = [self.pos(s * self.cs) // _LANES * _LANES for s in range(ns)]
        ends = [self.pos((s + 1) * self.cs - 1) + 1 - self.base[s] for s in range(ns)]
        self.width = -(-max(ends) // _LANES) * _LANES
        self.total = -(-max(b + self.width for b in self.base) // 1024) * 1024
        starts = [0]
        for sz in self.sizes:
            starts.append(starts[-1] + sz)
        self.segments = [(self.pos(a), sz) for a, sz in zip(starts, self.sizes)]

    def pos(self, g):
        return g if g < self.jump_at else g + self.jump_by

    def pieces(self, s):
        g0, g1 = s * self.cs, (s + 1) * self.cs
        cuts = [g0] + ([self.jump_at] if g0 < self.jump_at < g1 else []) + [g1]
        return [(a - g0, b - a, self.pos(a) - self.base[s]) for a, b in zip(cuts[:-1], cuts[1:])]

    def place(self, w, s):
        parts, at = [], 0
        for t0, n, j0 in self.pieces(s):
            parts += [jnp.zeros((w.shape[0], j0 - at), w.dtype), w[:, t0:t0 + n]]
            at = j0 + n
        return jnp.concatenate(parts + [jnp.zeros((w.shape[0], self.width - at), w.dtype)], axis=1)

    def unplace(self, slab, s):
        return jnp.concatenate([slab[:, j0:j0 + n] for _, n, j0 in self.pieces(s)], axis=1)

    def assemble(self, slabs):
        return sum(jnp.pad(slabs[s], ((0, 0), (b, self.total - b - self.width))) for s, b in enumerate(self.base))

    def split(self, full):
        return jnp.stack([full[:, b:b + self.width] for b in self.base])


def _attn_inputs(proj, sm, positions):
    s_len = proj.shape[0]
    plan = WinPlan(sm["norm_mix_g"].shape[0])
    hf, hq, hk = plan.hf, plan.hq, plan.hk
    grp = hq // hk
    q_f, k_f, v_f, f_logit, q_s, k_s, v_s = [proj[:, a:a + n] for a, n in plan.segments]

    q_f = head_rms("fox_qnorm", q_f, sm["fox_q_norm_g"])
    k_f = head_rms("fox_knorm", k_f, sm["fox_k_norm_g"])
    log_f = jax.nn.log_sigmoid(f_logit + sm["b_forget"])
    c = time_cumsum(log_f.T)

    inv_freq = ROPE_THETA ** (-jnp.arange(0, HEAD_DIM, 2, dtype=F32) / HEAD_DIM)
    ang = positions.astype(F32)[:, None] * inv_freq
    cos, sin = jnp.cos(ang), jnp.sin(ang)
    q_s = head_rms_rope("swa_qnorm", q_s, sm["swa_q_norm_g"], cos, sin)
    k_s = head_rms_rope("swa_knorm", k_s, sm["swa_k_norm_g"], cos, sin)
    sink = jnp.broadcast_to(sm["swa_sinks"].reshape(hk, grp, 1, 1), (hk, grp, WINDOW, 1)).reshape(hk, grp * WINDOW, 1)
    tq = fox_tile(s_len)
    return (q_f, k_f, v_f, c[:, :, None], c.reshape(hf, s_len // tq, 1, tq)), (q_s, k_s, v_s, sink)


_BIG = ("ffn1_w_gate", "ffn1_w_up", "ffn1_w_down", "w_in", "w_out", "ffn2_w_gate", "ffn2_w_up", "ffn2_w_down")
_SMALL = ("norm_ffn1_g", "norm_mix_g", "b_forget", "fox_q_norm_g", "fox_k_norm_g", "swa_q_norm_g", "swa_k_norm_g", "swa_sinks",
          "out_norm_fox_g", "out_norm_swa_g", "norm_ffn2_g")
_ATTN_SMALL = ("norm_mix_g", "b_forget", "fox_q_norm_g", "fox_k_norm_g", "swa_q_norm_g", "swa_k_norm_g", "swa_sinks")
_ALL = ("norm_ffn1_g", "ffn1_w_gate", "ffn1_w_up", "ffn1_w_down", "norm_mix_g", "w_in", "b_forget", "fox_q_norm_g", "fox_k_norm_g",
        "swa_q_norm_g", "swa_k_norm_g", "swa_sinks", "out_norm_fox_g", "out_norm_swa_g", "w_out", "norm_ffn2_g", "ffn2_w_gate",
        "ffn2_w_up", "ffn2_w_down")


def _pack_small(d):
    parts = []
    for k in _SMALL:
        v = d[k].reshape(-1)
        rows = -(-v.shape[0] // _LANES)
        parts.append(jnp.pad(v, (0, rows * _LANES - v.shape[0])).reshape(rows, _LANES))
    a = jnp.concatenate(parts, axis=0)
    return jnp.pad(a, ((0, -a.shape[0] % 8), (0, 0)))


def _unpack_small(a, like):
    out, r0 = {}, 0
    for k in _SMALL:
        nvals = like[k].shape[1]
        rows = -(-nvals // _LANES)
        out[k] = a[r0:r0 + rows].reshape(-1)[:nvals].reshape(1, nvals)
        r0 += rows
    return out


def _stacked(w):
    return w.reshape(-1, w.shape[-1])


def _local_step(shards, sm, x, positions, target, p_idx, c_idx, pc_idx):
    ns = N_CHIPS
    full = {}

    def fetch(*jobs):
        names = list(dict.fromkeys(n for n, _, _ in jobs))
        return names, gather([bufs[n] for n in names], [(names.index(n), kind, part) for n, kind, part in jobs])

    def take(names, rid):
        for n, b in zip(names, rid[0]):
            bufs[n] = b

    n1 = ["ffn1_w_gate", "ffn1_w_up", "ffn1_w_down"]
    n2 = ["ffn2_w_gate", "ffn2_w_up", "ffn2_w_down"]
    later = ["w_in", "w_out"] + n2
    placed, _ = cast_place("cast_place_ffn1", [shards[n] for n in n1], p_idx)
    bufs = dict(zip(n1, placed))
    gate1, up1, down1 = n1
    gate2, up2, down2 = n2
    names, rider = fetch((gate1, "ici", WHOLE), (up1, "ici", WHOLE))
    placed, rid = cast_place("cast_place_later", [shards[n] for n in later], p_idx, rider=rider)
    bufs.update(zip(later, placed))
    take(names, rid)
    names, rider = fetch((gate1, "d2d", WHOLE), (up1, "d2d", WHOLE), (down1, "ici", (0, 1, 4)))
    xn1, r1, rid = rms_fwd("ffn1_norm", x, sm["norm_ffn1_g"], BF16, rider=rider)
    take(names, rid)
    names, rider = fetch((down1, "ici", (1, 4, 4)), ("w_in", "ici", (0, 1, 4)))
    (hid1, hdg1, hdu1), rid = ffn_gu("ffn1_gu", xn1, bufs[gate1], bufs[up1], rider=rider)
    take(names, rid)
    names, rider = fetch((down1, "d2d", WHOLE))
    take(names, run_step("gather_d2d_ffn1_down", rider))
    wd1 = _stacked(bufs[down1])
    names, rider = fetch(("w_in", "ici", (1, 4, 4)))
    h1, rid = mm_nn("ffn1_down", hid1, wd1, scale=0.5, resid=x, rider=rider)
    take(names, rid)

    names, rider = fetch(("w_in", "d2d", WHOLE))
    u, r_mix, rid = rms_fwd("mix_norm", h1, sm["norm_mix_g"], BF16, rider=rider)
    take(names, rid)
    names, rider = fetch(("w_out", "ici", WHOLE))
    plan = WinPlan(x.shape[1])
    win = plan.assemble(bufs["w_in"])
    proj, rid = mm_nn("mix_inproj", u, win, rider=rider)
    take(names, rid)
    sm_attn = {k: sm[k] for k in _ATTN_SMALL}
    (fox_in, swa_in), attn_vjp = jax.vjp(lambda pr, s: _attn_inputs(pr, s, positions), proj, sm_attn)
    names, rider = fetch((gate2, "ici", WHOLE), (up2, "ici", (0, 1, 4)), ("w_out", "d2d", WHOLE))
    (o_f, lse), rid = fox_fwd(*fox_in, rider=rider)
    take(names, rid)
    names, rider = fetch((up2, "ici", (1, 4, 4)), (gate2, "d2d", WHOLE), (up2, "d2d", (0, 1, 4)))
    o_s, rid = swa_fwd(*swa_in, rider=rider)
    take(names, rid)
    o_fox, o_swa = o_f, o_s
    nf, r_fox = rms_fwd("out_norm_fox", o_fox, sm["out_norm_fox_g"], BF16)
    nsw, r_swa = rms_fwd("out_norm_swa", o_swa, sm["out_norm_swa_g"], BF16)
    o = jnp.concatenate([nf, nsw], axis=-1)
    wout = _stacked(bufs["w_out"])
    names, rider = fetch((down2, "ici", (0, 1, 4)), (up2, "d2d", (1, 4, 4)))
    h2, rid = mm_nn("out_proj", o, wout, resid=h1, rider=rider)
    take(names, rid)

    xn2, r2 = rms_fwd("ffn2_norm", h2, sm["norm_ffn2_g"], BF16)
    names, rider = fetch((down2, "ici", (1, 4, 4)))
    (hid2, hdg2, hdu2), rid = ffn_gu("ffn2_gu", xn2, bufs[gate2], bufs[up2], rider=rider)
    take(names, rid)
    names, rider = fetch((down2, "d2d", WHOLE))
    take(names, run_step("gather_d2d_ffn2_down", rider))
    wd2 = _stacked(bufs["ffn2_w_down"])
    y, _ = mm_nn("ffn2_down", hid2, wd2, scale=0.5, resid=h2)
    loss, dy, dy_b = loss_call(y, target)

    red = {}

    def grad(n, g):
        red[n] = {"grad": g.reshape(ns, -1, g.shape[-1])}

    def ride(*steps):
        def done(rid):
            a0 = n0 = 0
            for rd, cb in steps:
                cb(rid[0][a0:a0 + len(rd.aliased)], rid[1][n0:n0 + len(rd.news)])
                a0, n0 = a0 + len(rd.aliased), n0 + len(rd.news)

        return (combine(*[s[0] for s in steps]) if len(steps) > 1 else steps[0][0]), done

    def xchg(*names):
        def cb(al, news):
            for n, t in zip(names, news):
                red[n]["sum"] = chip_sum("chip_sum_" + n, red[n]["grad"], t, c_idx)

        return exchange_halves([red[n]["grad"] for n in names]), cb

    def scat(n, part=WHOLE):
        def cb(al, news):
            red[n]["got"] = (al or news)[0]

        return scatter_to_owner([red[n]["sum"]], [red[n]["got"]] if "got" in red[n] else None, part), cb

    def own(n):
        red[n]["half"] = owner_sum("owner_sum_" + n, red[n]["sum"], red[n]["got"], pc_idx)

    def join(*names):
        return join_halves([red[n]["half"] for n in names]), lambda al, news: full.update(zip(names, al))

    dwd2, _ = mm_tn("ffn2_dwd", hid2, dy_b, out_dtype=BF16, scale=0.5)
    grad(down2, dwd2)
    rider, done = ride(xchg(down2))
    (dg2, du2), rid = ffn_dh("ffn2_dh", dy_b, wd2, hdg2, hdu2, ns, 0.5, rider=rider)
    done(rid)
    rider, done = ride(scat(down2, (0, 1, 2)))
    dwg2, rid = mm_tn_sharded("ffn2_dwg", xn2, dg2, ns, rider=rider)
    done(rid)
    grad(gate2, dwg2)
    rider, done = ride(scat(down2, (1, 2, 2)), xchg(gate2))
    dwu2, rid = mm_tn_sharded("ffn2_dwu", xn2, du2, ns, rider=rider)
    done(rid)
    grad(up2, dwu2)
    rider, done = ride(scat(gate2, (0, 1, 2)), xchg(up2))
    dxn, rid = mm_nt_sharded("ffn2_dxn_g", dg2, bufs[gate2], rider=rider)
    done(rid)
    rider, done = ride(scat(gate2, (1, 2, 2)))
    dxn, rid = mm_nt_sharded("ffn2_dxn_u", du2, bufs[up2], resid=dxn, rider=rider)
    done(rid)
    dh2, dgain_ffn2, dh2_b = rms_bwd("ffn2_dnorm", h2, sm["norm_ffn2_g"], r2, dxn, dres=dy, also_bf16=True)
    own(down2)
    own(gate2)

    do, _ = mm_nt("out_do", dh2_b, wout)
    dwout, _ = mm_tn("out_dw", o, dh2_b, out_dtype=BF16)
    cf = o_fox.shape[1]
    d_fox, dgain_fox = rms_bwd("out_dnorm_fox", o_fox, sm["out_norm_fox_g"], r_fox, do[:, :cf])
    d_swa, dgain_swa = rms_bwd("out_dnorm_swa", o_swa, sm["out_norm_swa_g"], r_swa, do[:, cf:])
    grad("w_out", dwout)
    rider, done = ride(scat(up2))
    swa_cts, rid = swa_bwd(*swa_in, o_s, d_swa, rider=rider)
    done(rid)
    own(up2)
    rider, done = ride(xchg("w_out"), join(down2, gate2, up2))
    fox_cts, rid = fox_bwd(*fox_in, o_f, lse, d_fox, rider=rider)
    done(rid)
    dproj, dsm_attn = attn_vjp((tuple(fox_cts), tuple(swa_cts)))
    dproj = dproj.astype(BF16)

    rider, done = ride(scat("w_out"))
    du, rid = mm_nt("mix_du", dproj, win, rider=rider)
    done(rid)
    dwin, _ = mm_tn("mix_dwin", u, dproj, out_dtype=BF16)
    grad("w_in", plan.split(dwin))
    rider, done = ride(xchg("w_in"))
    dh1, dgain_mix, dh1_b, rid = rms_bwd("mix_dnorm", h1, sm["norm_mix_g"], r_mix, du, dres=dh2, rider=rider, also_bf16=True)
    done(rid)
    own("w_out")

    rider, done = ride(scat("w_in", (0, 1, 2)))
    dwd1, rid = mm_tn("ffn1_dwd", hid1, dh1_b, out_dtype=BF16, scale=0.5, rider=rider)
    done(rid)
    grad(down1, dwd1)
    rider, done = ride(scat("w_in", (1, 2, 2)), xchg(down1))
    (dg1, du1), rid = ffn_dh("ffn1_dh", dh1_b, wd1, hdg1, hdu1, ns, 0.5, rider=rider)
    done(rid)
    own("w_in")
    rider, done = ride(scat(down1, (0, 1, 2)), join("w_out"))
    dwg1, rid = mm_tn_sharded("ffn1_dwg", xn1, dg1, ns, rider=rider)
    done(rid)
    grad(gate1, dwg1)
    rider, done = ride(scat(down1, (1, 2, 2)), xchg(gate1), join("w_in"))
    dwu1, rid = mm_tn_sharded("ffn1_dwu", xn1, du1, ns, rider=rider)
    done(rid)
    grad(up1, dwu1)
    own(down1)
    rider, done = ride(scat(gate1), xchg(up1), join(down1))
    dxn, rid = mm_nt_sharded("ffn1_dxn_g", dg1, bufs[gate1], rider=rider)
    done(rid)
    own(gate1)
    rider, done = ride(scat(up1, (0, 3, 4)))
    dxn, rid = mm_nt_sharded("ffn1_dxn_u", du1, bufs[up1], resid=dxn, rider=rider)
    done(rid)
    dx, dgain_ffn1 = rms_bwd("ffn1_dnorm", x, sm["norm_ffn1_g"], r1, dxn, dres=dh1)

    rider, done = ride(scat(up1, (3, 4, 4)), join(gate1))
    done(run_step("reduce_tail", rider))
    own(up1)
    rider, done = ride(join(up1))
    done(run_step("join_tail", rider))

    g_small = dict(dsm_attn)
    g_small["norm_mix_g"] = g_small["norm_mix_g"] + dgain_mix
    g_small.update(norm_ffn1_g=dgain_ffn1, norm_ffn2_g=dgain_ffn2, out_norm_fox_g=dgain_fox, out_norm_swa_g=dgain_swa)
    return loss, dx, full, g_small


def kernel(x, positions, norm_ffn1_g, ffn1_w_gate, ffn1_w_up, ffn1_w_down, norm_mix_g, w_in, b_forget, fox_q_norm_g, fox_k_norm_g, swa_q_norm_g, swa_k_norm_g, swa_sinks, out_norm_fox_g, out_norm_swa_g, w_out, norm_ffn2_g, ffn2_w_gate, ffn2_w_up, ffn2_w_down, loss_target, m_norm_ffn1_g, m_ffn1_w_gate, m_ffn1_w_up, m_ffn1_w_down, m_norm_mix_g, m_w_in, m_b_forget, m_fox_q_norm_g, m_fox_k_norm_g, m_swa_q_norm_g, m_swa_k_norm_g, m_swa_sinks, m_out_norm_fox_g, m_out_norm_swa_g, m_w_out, m_norm_ffn2_g, m_ffn2_w_gate, m_ffn2_w_up, m_ffn2_w_down, v_norm_ffn1_g, v_ffn1_w_gate, v_ffn1_w_up, v_ffn1_w_down, v_norm_mix_g, v_w_in, v_b_forget, v_fox_q_norm_g, v_fox_k_norm_g, v_swa_q_norm_g, v_swa_k_norm_g, v_swa_sinks, v_out_norm_fox_g, v_out_norm_swa_g, v_w_out, v_norm_ffn2_g, v_ffn2_w_gate, v_ffn2_w_up, v_ffn2_w_down):
    args = dict(locals())
    w = {k: args[k] for k in _ALL}
    m = {k: args["m_" + k] for k in _ALL}
    v = {k: args["v_" + k] for k in _ALL}
    c_idx = lax.axis_index("c").astype(jnp.int32).reshape(1)
    p_idx = (2 * lax.axis_index("x") + lax.axis_index("y")).astype(jnp.int32).reshape(1)
    pc_idx = jnp.concatenate([p_idx, c_idx])

    small = {k: w[k] for k in _SMALL}
    shards = {k: w[k][0] for k in _BIG}
    plan = WinPlan(x.shape[-1])
    shards["w_in"] = lax.switch(p_idx[0], [functools.partial(plan.place, s=s) for s in range(N_CHIPS)], shards["w_in"])
    loss, grad_x, g_shard, g_small = _local_step(shards, {k: w[k][0] for k in _SMALL}, x[0], positions[0], loss_target[0],
                                                 p_idx, c_idx, pc_idx)
    g_shard["w_in"] = lax.switch(p_idx[0], [functools.partial(plan.unplace, s=s) for s in range(N_CHIPS)], g_shard["w_in"])
    loss = lax.psum(loss, ("x", "y", "c"))
    g_small_sum = _unpack_small(all_reduce_small(_pack_small({k: g_small[k].reshape(1, -1) for k in _SMALL})), small)

    grad_w, delta, new_m, new_v = {}, {}, {}, {}
    for k in _BIG:
        (g, d, nm, nv), _ = adamw("adamw_" + k, w[k][0], g_shard[k], m[k][0], v[k][0])
        grad_w[k], delta[k], new_m[k], new_v[k] = g[None], d[None], nm[None], nv[None]
    (_, d, nm, nv), _ = adamw("adamw_small", _pack_small(small), _pack_small(g_small_sum), _pack_small({k: m[k] for k in _SMALL}),
                              _pack_small({k: v[k] for k in _SMALL}))
    grad_w.update(g_small_sum)
    delta.update(_unpack_small(d, small))
    new_m.update(_unpack_small(nm, small))
    new_v.update(_unpack_small(nv, small))

    return (loss, grad_x[None], *[grad_w[k] for k in _ALL], *[delta[k] for k in _ALL], *[new_m[k] for k in _ALL], *[new_v[k] for k in _ALL])
```
